```python
import math
import jax, jax.numpy as jnp
from jax import lax
import numpy as np

D_MODEL = 1024
BATCH = 8
SEQ = 8192
DEPTH = 1

CHUNK = 64
N_META = 16
Q_BLOCK = 128
EPS = 1e-6
D_RNN = 1280
RNN_BLOCKS = 10
RNN_BLOCK_DIM = D_RNN // RNN_BLOCKS
CONV_WIDTH = 4
LRU_C = 8.0
N_HEADS = 8
QK_NOPE = 128
QK_ROPE = 64
V_DIM = 128
Q_RANK = 384
KV_RANK = 256
ROPE_THETA = 10000.0
ATTN_SCALE = 1.0 / math.sqrt(QK_NOPE + QK_ROPE)
N_BRANCH = 2
D_FF = ((8 * D_MODEL // 3 + 255) // 256) * 256
IN_SPLITS = (D_RNN, D_RNN, Q_RANK, KV_RANK, QK_ROPE, N_BRANCH * D_MODEL)
D_IN = sum(IN_SPLITS)
D_BRANCH_IN = D_RNN + N_HEADS * V_DIM
PAD_CHUNK = 2 ** 30
NEG = -1e30

kernel_name = "hybrid_rglru_mla_gated_block"


def rmsnorm(x, g):
    xf = x.astype(jnp.float32)
    y = xf * lax.rsqrt(jnp.mean(xf * xf, axis=-1, keepdims=True) + EPS)
    return (y * g.astype(jnp.float32)).astype(x.dtype)


def apply_rope(x, cos, sin):
    x1, x2 = jnp.split(x.astype(jnp.float32), 2, axis=-1)
    return jnp.concatenate([x1 * cos - x2 * sin, x2 * cos + x1 * sin], axis=-1).astype(x.dtype)


def rglru_branch(u_x, u_gate, conv_w, conv_b, w_a, b_a, w_i, b_i, lam):
    B, L, _ = u_x.shape
    xp = jnp.pad(u_x, ((0, 0), (CONV_WIDTH - 1, 0), (0, 0)))
    xc = conv_b + xp[:, 0:L] * conv_w[0]
    for k in range(1, CONV_WIDTH):
        xc = xc + xp[:, k:k + L] * conv_w[k]
    xb = xc.reshape(B, L, RNN_BLOCKS, RNN_BLOCK_DIM)
    r = jax.nn.sigmoid(jnp.einsum('blhi,hij->blhj', xb, w_a).reshape(B, L, D_RNN) + b_a)
    i = jax.nn.sigmoid(jnp.einsum('blhi,hij->blhj', xb, w_i).reshape(B, L, D_RNN) + b_i)
    log_a = LRU_C * r.astype(jnp.float32) * jax.nn.log_sigmoid(lam.astype(jnp.float32))
    a = jnp.exp(log_a)
    b = jnp.sqrt(-jnp.expm1(2.0 * log_a)) * (i * xc).astype(jnp.float32)

    def combine(left, right):
        a_l, b_l = left
        a_r, b_r = right
        return a_l * a_r, a_r * b_l + b_r

    _, h = lax.associative_scan(combine, (a, b), axis=1)
    return h.astype(u_x.dtype) * jax.nn.gelu(u_gate)


def mla_branch(u_q, u_kv, u_kr, q_norm_g, w_uq, kv_norm_g, w_ukv, cos, sin, chunk_id):
    B, L, _ = u_q.shape
    nb = L // Q_BLOCK
    q = (rmsnorm(u_q, q_norm_g) @ w_uq).reshape(B, L, N_HEADS, QK_NOPE + QK_ROPE)
    q_nope, q_rope = q[..., :QK_NOPE], q[..., QK_NOPE:]
    q_rope = apply_rope(q_rope, cos[:, None, :], sin[:, None, :])
    kv = (rmsnorm(u_kv, kv_norm_g) @ w_ukv).reshape(B, L, N_HEADS, QK_NOPE + V_DIM)
    k_nope, v = kv[..., :QK_NOPE], kv[..., QK_NOPE:]
    k_rope = apply_rope(u_kr, cos, sin)

    def to_blocks(t):
        t = t.reshape((B, nb, Q_BLOCK) + t.shape[2:])
        return jnp.moveaxis(t, 1, 0)

    def attend(args):
        qn, qr, qc = args
        s = (jnp.einsum('bqhd,bkhd->bhqk', qn, k_nope)
             + jnp.einsum('bqhr,bkr->bhqk', qr, k_rope)).astype(jnp.float32) * ATTN_SCALE
        mask = chunk_id[None, :] <= qc[:, None]
        s = jnp.where(mask[None, None], s, NEG)
        p = jax.nn.softmax(s, axis=-1).astype(v.dtype)
        return jnp.einsum('bhqk,bkhd->bqhd', p, v)

    o = lax.map(attend, (to_blocks(q_nope), to_blocks(q_rope), chunk_id.reshape(nb, Q_BLOCK)))
    return jnp.moveaxis(o, 0, 1).reshape(B, L, N_HEADS * V_DIM)


def hybrid_layer(h, cos, sin, chunk_id, norm_mix_g, w_in, b_gate, conv_w, conv_b, w_rec_a, b_rec_a,
                 w_rec_i, b_rec_i, lru_lambda, q_norm_g, w_uq, kv_norm_g, w_ukv, w_branch, w_out,
                 norm_ffn_g, w_ffn_in, w_ffn_out):
    B, L, D = h.shape
    z = rmsnorm(h, norm_mix_g)
    u = z @ w_in
    u_x, u_g, u_q, u_kv, u_kr, u_m = jnp.split(u, np.cumsum(IN_SPLITS)[:-1].tolist(), axis=-1)
    y_rnn = rglru_branch(u_x, u_g, conv_w, conv_b, w_rec_a, b_rec_a, w_rec_i, b_rec_i, lru_lambda)
    y_att = mla_branch(u_q, u_kv, u_kr, q_norm_g, w_uq, kv_norm_g, w_ukv, cos, sin, chunk_id)
    p_rnn = y_rnn @ w_branch[:D_RNN]
    p_att = y_att @ w_branch[D_RNN:]
    gates = jax.nn.sigmoid(u_m + b_gate.reshape(-1)).reshape(B, L, N_BRANCH, D)
    mixed = gates[:, :, 0] * p_rnn + gates[:, :, 1] * p_att
    h = h + mixed @ w_out
    zf = rmsnorm(h, norm_ffn_g)
    gate, up = jnp.split(zf @ w_ffn_in, 2, axis=-1)
    return h + (jax.nn.silu(gate) * up) @ w_ffn_out


def _fwd_setup_inputs(seed: int = 0) -> dict:
    key = jax.random.key(seed)
    ks = jax.random.split(key, 24)
    f32 = jnp.float32

    def nrm(k, shape, scale):
        return jax.random.normal(k, shape, f32) * scale

    a0 = jax.random.uniform(ks[11], (DEPTH, D_RNN), f32, 0.9, 0.999)
    return {
        "x": nrm(ks[0], (BATCH, SEQ, D_MODEL), 1.0),
        "meta_tokens": nrm(ks[1], (N_META, D_MODEL), 1.0),
        "norm_mix_g": 1.0 + nrm(ks[2], (DEPTH, D_MODEL), 0.02),
        "w_in": nrm(ks[3], (DEPTH, D_MODEL, D_IN), D_MODEL ** -0.5),
        "b_gate": nrm(ks[4], (DEPTH, N_BRANCH, D_MODEL), 0.02),
        "conv_w": nrm(ks[5], (DEPTH, CONV_WIDTH, D_RNN), CONV_WIDTH ** -0.5),
        "conv_b": nrm(ks[6], (DEPTH, D_RNN), 0.02),
        "w_rec_a": nrm(ks[7], (DEPTH, RNN_BLOCKS, RNN_BLOCK_DIM, RNN_BLOCK_DIM), RNN_BLOCK_DIM ** -0.5),
        "b_rec_a": nrm(ks[8], (DEPTH, D_RNN), 0.02),
        "w_rec_i": nrm(ks[9], (DEPTH, RNN_BLOCKS, RNN_BLOCK_DIM, RNN_BLOCK_DIM), RNN_BLOCK_DIM ** -0.5),
        "b_rec_i": nrm(ks[10], (DEPTH, D_RNN), 0.02),
        "lru_lambda": jnp.log(a0) - jnp.log1p(-a0),
        "q_norm_g": 1.0 + nrm(ks[12], (DEPTH, Q_RANK), 0.02),
        "w_uq": nrm(ks[13], (DEPTH, Q_RANK, N_HEADS * (QK_NOPE + QK_ROPE)), Q_RANK ** -0.5),
        "kv_norm_g": 1.0 + nrm(ks[14], (DEPTH, KV_RANK), 0.02),
        "w_ukv": nrm(ks[15], (DEPTH, KV_RANK, N_HEADS * (QK_NOPE + V_DIM)), KV_RANK ** -0.5),
        "w_branch": nrm(ks[16], (DEPTH, D_BRANCH_IN, D_MODEL), 1024 ** -0.5),
        "w_out": nrm(ks[17], (DEPTH, D_MODEL, D_MODEL), D_MODEL ** -0.5),
        "norm_ffn_g": 1.0 + nrm(ks[18], (DEPTH, D_MODEL), 0.02),
        "w_ffn_in": nrm(ks[19], (DEPTH, D_MODEL, 2 * D_FF), D_MODEL ** -0.5),
        "w_ffn_out": nrm(ks[20], (DEPTH, D_FF, D_MODEL), D_FF ** -0.5),
        "final_norm_g": 1.0 + nrm(ks[21], (D_MODEL,), 0.02),
    }


def _fwd_reference(x, meta_tokens, norm_mix_g, w_in, b_gate, conv_w, conv_b, w_rec_a, b_rec_a, w_rec_i,
              b_rec_i, lru_lambda, q_norm_g, w_uq, kv_norm_g, w_ukv, w_branch, w_out, norm_ffn_g,
              w_ffn_in, w_ffn_out, final_norm_g):
    B, S, D = x.shape
    L = N_META + S
    Lp = ((L + Q_BLOCK - 1) // Q_BLOCK) * Q_BLOCK
    meta = jnp.broadcast_to(meta_tokens.astype(x.dtype)[None], (B, N_META, D))
    h = jnp.pad(jnp.concatenate([meta, x], axis=1), ((0, 0), (0, Lp - L), (0, 0)))
    idx = jnp.arange(Lp, dtype=jnp.int32)
    chunk_id = jnp.where(idx < N_META, 0, (idx - N_META) // CHUNK + 1)
    chunk_id = jnp.where(idx >= L, PAD_CHUNK, chunk_id)
    inv_freq = ROPE_THETA ** (-jnp.arange(0, QK_ROPE, 2, dtype=jnp.float32) / QK_ROPE)
    ang = idx.astype(jnp.float32)[:, None] * inv_freq[None, :]
    cos, sin = jnp.cos(ang), jnp.sin(ang)
    for l in range(DEPTH):
        h = hybrid_layer(h, cos, sin, chunk_id, norm_mix_g[l], w_in[l], b_gate[l], conv_w[l], conv_b[l],
                         w_rec_a[l], b_rec_a[l], w_rec_i[l], b_rec_i[l], lru_lambda[l], q_norm_g[l],
                         w_uq[l], kv_norm_g[l], w_ukv[l], w_branch[l], w_out[l], norm_ffn_g[l],
                         w_ffn_in[l], w_ffn_out[l])
    h = rmsnorm(h, final_norm_g)
    return h[:, N_META:L]


import jax as _jax
import jax.numpy as _jnp

TWIN_FORMAT = 'train_step'
FWD_PARAMS = ['x', 'meta_tokens', 'norm_mix_g', 'w_in', 'b_gate', 'conv_w', 'conv_b', 'w_rec_a', 'b_rec_a', 'w_rec_i', 'b_rec_i', 'lru_lambda', 'q_norm_g', 'w_uq', 'kv_norm_g', 'w_ukv', 'w_branch', 'w_out', 'norm_ffn_g', 'w_ffn_in', 'w_ffn_out', 'final_norm_g']
TWIN_WEIGHTS = ['meta_tokens', 'norm_mix_g', 'w_in', 'b_gate', 'conv_w', 'conv_b', 'w_rec_a', 'b_rec_a', 'w_rec_i', 'b_rec_i', 'lru_lambda', 'q_norm_g', 'w_uq', 'kv_norm_g', 'w_ukv', 'w_branch', 'w_out', 'norm_ffn_g', 'w_ffn_in', 'w_ffn_out', 'final_norm_g']
TWIN_DIFF_INPUT = 'x'
TWIN_INPUTS = ['x', 'meta_tokens', 'norm_mix_g', 'w_in', 'b_gate', 'conv_w', 'conv_b', 'w_rec_a', 'b_rec_a', 'w_rec_i', 'b_rec_i', 'lru_lambda', 'q_norm_g', 'w_uq', 'kv_norm_g', 'w_ukv', 'w_branch', 'w_out', 'norm_ffn_g', 'w_ffn_in', 'w_ffn_out', 'final_norm_g', 'loss_target', 'm_meta_tokens', 'm_norm_mix_g', 'm_w_in', 'm_b_gate', 'm_conv_w', 'm_conv_b', 'm_w_rec_a', 'm_b_rec_a', 'm_w_rec_i', 'm_b_rec_i', 'm_lru_lambda', 'm_q_norm_g', 'm_w_uq', 'm_kv_norm_g', 'm_w_ukv', 'm_w_branch', 'm_w_out', 'm_norm_ffn_g', 'm_w_ffn_in', 'm_w_ffn_out', 'm_final_norm_g', 'v_meta_tokens', 'v_norm_mix_g', 'v_w_in', 'v_b_gate', 'v_conv_w', 'v_conv_b', 'v_w_rec_a', 'v_b_rec_a', 'v_w_rec_i', 'v_b_rec_i', 'v_lru_lambda', 'v_q_norm_g', 'v_w_uq', 'v_kv_norm_g', 'v_w_ukv', 'v_w_branch', 'v_w_out', 'v_norm_ffn_g', 'v_w_ffn_in', 'v_w_ffn_out', 'v_final_norm_g']
TWIN_OUTPUTS = ['loss', 'grad_x', 'grad_meta_tokens', 'grad_norm_mix_g', 'grad_w_in', 'grad_b_gate', 'grad_conv_w', 'grad_conv_b', 'grad_w_rec_a', 'grad_b_rec_a', 'grad_w_rec_i', 'grad_b_rec_i', 'grad_lru_lambda', 'grad_q_norm_g', 'grad_w_uq', 'grad_kv_norm_g', 'grad_w_ukv', 'grad_w_branch', 'grad_w_out', 'grad_norm_ffn_g', 'grad_w_ffn_in', 'grad_w_ffn_out', 'grad_final_norm_g', 'delta_meta_tokens', 'delta_norm_mix_g', 'delta_w_in', 'delta_b_gate', 'delta_conv_w', 'delta_conv_b', 'delta_w_rec_a', 'delta_b_rec_a', 'delta_w_rec_i', 'delta_b_rec_i', 'delta_lru_lambda', 'delta_q_norm_g', 'delta_w_uq', 'delta_kv_norm_g', 'delta_w_ukv', 'delta_w_branch', 'delta_w_out', 'delta_norm_ffn_g', 'delta_w_ffn_in', 'delta_w_ffn_out', 'delta_final_norm_g', 'new_m_meta_tokens', 'new_m_norm_mix_g', 'new_m_w_in', 'new_m_b_gate', 'new_m_conv_w', 'new_m_conv_b', 'new_m_w_rec_a', 'new_m_b_rec_a', 'new_m_w_rec_i', 'new_m_b_rec_i', 'new_m_lru_lambda', 'new_m_q_norm_g', 'new_m_w_uq', 'new_m_kv_norm_g', 'new_m_w_ukv', 'new_m_w_branch', 'new_m_w_out', 'new_m_norm_ffn_g', 'new_m_w_ffn_in', 'new_m_w_ffn_out', 'new_m_final_norm_g', 'new_v_meta_tokens', 'new_v_norm_mix_g', 'new_v_w_in', 'new_v_b_gate', 'new_v_conv_w', 'new_v_conv_b', 'new_v_w_rec_a', 'new_v_b_rec_a', 'new_v_w_rec_i', 'new_v_b_rec_i', 'new_v_lru_lambda', 'new_v_q_norm_g', 'new_v_w_uq', 'new_v_kv_norm_g', 'new_v_w_ukv', 'new_v_w_branch', 'new_v_w_out', 'new_v_norm_ffn_g', 'new_v_w_ffn_in', 'new_v_w_ffn_out', 'new_v_final_norm_g']
TWIN_LEAF_KINDS = {'loss': 'loss', 'grad_x': 'grad_x', 'grad_meta_tokens': 'grad_w', 'grad_norm_mix_g': 'grad_w', 'grad_w_in': 'grad_w', 'grad_b_gate': 'grad_w', 'grad_conv_w': 'grad_w', 'grad_conv_b': 'grad_w', 'grad_w_rec_a': 'grad_w', 'grad_b_rec_a': 'grad_w', 'grad_w_rec_i': 'grad_w', 'grad_b_rec_i': 'grad_w', 'grad_lru_lambda': 'grad_w', 'grad_q_norm_g': 'grad_w', 'grad_w_uq': 'grad_w', 'grad_kv_norm_g': 'grad_w', 'grad_w_ukv': 'grad_w', 'grad_w_branch': 'grad_w', 'grad_w_out': 'grad_w', 'grad_norm_ffn_g': 'grad_w', 'grad_w_ffn_in': 'grad_w', 'grad_w_ffn_out': 'grad_w', 'grad_final_norm_g': 'grad_w', 'delta_meta_tokens': 'delta_w', 'delta_norm_mix_g': 'delta_w', 'delta_w_in': 'delta_w', 'delta_b_gate': 'delta_w', 'delta_conv_w': 'delta_w', 'delta_conv_b': 'delta_w', 'delta_w_rec_a': 'delta_w', 'delta_b_rec_a': 'delta_w', 'delta_w_rec_i': 'delta_w', 'delta_b_rec_i': 'delta_w', 'delta_lru_lambda': 'delta_w', 'delta_q_norm_g': 'delta_w', 'delta_w_uq': 'delta_w', 'delta_kv_norm_g': 'delta_w', 'delta_w_ukv': 'delta_w', 'delta_w_branch': 'delta_w', 'delta_w_out': 'delta_w', 'delta_norm_ffn_g': 'delta_w', 'delta_w_ffn_in': 'delta_w', 'delta_w_ffn_out': 'delta_w', 'delta_final_norm_g': 'delta_w', 'new_m_meta_tokens': 'new_m', 'new_m_norm_mix_g': 'new_m', 'new_m_w_in': 'new_m', 'new_m_b_gate': 'new_m', 'new_m_conv_w': 'new_m', 'new_m_conv_b': 'new_m', 'new_m_w_rec_a': 'new_m', 'new_m_b_rec_a': 'new_m', 'new_m_w_rec_i': 'new_m', 'new_m_b_rec_i': 'new_m', 'new_m_lru_lambda': 'new_m', 'new_m_q_norm_g': 'new_m', 'new_m_w_uq': 'new_m', 'new_m_kv_norm_g': 'new_m', 'new_m_w_ukv': 'new_m', 'new_m_w_branch': 'new_m', 'new_m_w_out': 'new_m', 'new_m_norm_ffn_g': 'new_m', 'new_m_w_ffn_in': 'new_m', 'new_m_w_ffn_out': 'new_m', 'new_m_final_norm_g': 'new_m', 'new_v_meta_tokens': 'new_v', 'new_v_norm_mix_g': 'new_v', 'new_v_w_in': 'new_v', 'new_v_b_gate': 'new_v', 'new_v_conv_w': 'new_v', 'new_v_conv_b': 'new_v', 'new_v_w_rec_a': 'new_v', 'new_v_b_rec_a': 'new_v', 'new_v_w_rec_i': 'new_v', 'new_v_b_rec_i': 'new_v', 'new_v_lru_lambda': 'new_v', 'new_v_q_norm_g': 'new_v', 'new_v_w_uq': 'new_v', 'new_v_kv_norm_g': 'new_v', 'new_v_w_ukv': 'new_v', 'new_v_w_branch': 'new_v', 'new_v_w_out': 'new_v', 'new_v_norm_ffn_g': 'new_v', 'new_v_w_ffn_in': 'new_v', 'new_v_w_ffn_out': 'new_v', 'new_v_final_norm_g': 'new_v'}


def _forward(args):
    return _fwd_reference(*[args[k] for k in FWD_PARAMS])


def _output_shape():
    def fwd():
        inp = _fwd_setup_inputs(0)
        return _fwd_reference(*[inp[k] for k in FWD_PARAMS])
    out = _jax.eval_shape(fwd)
    return out.shape, out.dtype

N_MICROBATCH = 1
ADAM_LR = 0.001
ADAM_B1 = 0.9
ADAM_B2 = 0.999
ADAM_EPS = 1e-08
ADAM_WD = 0.01
ADAM_STEP = 10
PER_EXAMPLE_BATCH_AXIS = {'x': 0, 'loss_target': 0}
SHARED_INPUTS = []
_WEIGHT_DTYPES = {'meta_tokens': _jnp.float32, 'norm_mix_g': _jnp.float32, 'w_in': _jnp.float32, 'b_gate': _jnp.float32, 'conv_w': _jnp.float32, 'conv_b': _jnp.float32, 'w_rec_a': _jnp.float32, 'b_rec_a': _jnp.float32, 'w_rec_i': _jnp.float32, 'b_rec_i': _jnp.float32, 'lru_lambda': _jnp.float32, 'q_norm_g': _jnp.float32, 'w_uq': _jnp.float32, 'kv_norm_g': _jnp.float32, 'w_ukv': _jnp.float32, 'w_branch': _jnp.float32, 'w_out': _jnp.float32, 'norm_ffn_g': _jnp.float32, 'w_ffn_in': _jnp.float32, 'w_ffn_out': _jnp.float32, 'final_norm_g': _jnp.float32}
MOMENT_SCALE = {'meta_tokens': 3.373016e-03, 'norm_mix_g': 1.002230e-01, 'w_in': 4.378823e-02, 'b_gate': 1.772716e-02, 'conv_w': 6.039204e-02, 'conv_b': 2.523223e-01, 'w_rec_a': 1.372689e-02, 'b_rec_a': 1.255807e-02, 'w_rec_i': 2.367297e-02, 'b_rec_i': 2.574864e-02, 'lru_lambda': 2.663122e-02, 'q_norm_g': 3.566234e-02, 'w_uq': 1.729962e-02, 'kv_norm_g': 6.432462e-02, 'w_ukv': 2.002234e-02, 'w_branch': 4.264162e-02, 'w_out': 6.406200e-02, 'norm_ffn_g': 1.845109e-01, 'w_ffn_in': 7.730499e-02, 'w_ffn_out': 1.266691e-01, 'final_norm_g': 6.404175e+01}


def _to_microbatches(a, axis):
    t = _jnp.moveaxis(a, axis, 0)
    t = t.reshape((N_MICROBATCH, t.shape[0] // N_MICROBATCH) + t.shape[1:])
    return _jnp.moveaxis(t, 1, axis + 1)


def setup_inputs(seed: int = 0) -> dict:
    inp = _fwd_setup_inputs(seed)
    key = _jax.random.fold_in(_jax.random.key(seed), 7919)
    shape, _ = _output_shape()
    out = dict(inp)
    out["loss_target"] = _jax.random.normal(_jax.random.fold_in(key, 0), shape, _jnp.float32)
    for i, name in enumerate(TWIN_WEIGHTS):
        w = inp[name].astype(_jnp.float32)
        if MOMENT_SCALE is None:
            s = _jnp.sqrt(_jnp.mean(_jnp.square(w)) + 1e-30)
        else:
            s = MOMENT_SCALE[name]
        km, kv = _jax.random.split(_jax.random.fold_in(key, i + 1))
        out[name] = w
        out["m_" + name] = s * _jax.random.normal(km, w.shape, _jnp.float32)
        out["v_" + name] = (s * s) * _jax.random.uniform(kv, w.shape, _jnp.float32, 0.5, 1.5)
    if N_MICROBATCH > 1:
        for name, axis in PER_EXAMPLE_BATCH_AXIS.items():
            out[name] = _to_microbatches(out[name], axis)
    return {'x': out['x'], 'meta_tokens': out['meta_tokens'], 'norm_mix_g': out['norm_mix_g'], 'w_in': out['w_in'], 'b_gate': out['b_gate'], 'conv_w': out['conv_w'], 'conv_b': out['conv_b'], 'w_rec_a': out['w_rec_a'], 'b_rec_a': out['b_rec_a'], 'w_rec_i': out['w_rec_i'], 'b_rec_i': out['b_rec_i'], 'lru_lambda': out['lru_lambda'], 'q_norm_g': out['q_norm_g'], 'w_uq': out['w_uq'], 'kv_norm_g': out['kv_norm_g'], 'w_ukv': out['w_ukv'], 'w_branch': out['w_branch'], 'w_out': out['w_out'], 'norm_ffn_g': out['norm_ffn_g'], 'w_ffn_in': out['w_ffn_in'], 'w_ffn_out': out['w_ffn_out'], 'final_norm_g': out['final_norm_g'], 'loss_target': out['loss_target'], 'm_meta_tokens': out['m_meta_tokens'], 'm_norm_mix_g': out['m_norm_mix_g'], 'm_w_in': out['m_w_in'], 'm_b_gate': out['m_b_gate'], 'm_conv_w': out['m_conv_w'], 'm_conv_b': out['m_conv_b'], 'm_w_rec_a': out['m_w_rec_a'], 'm_b_rec_a': out['m_b_rec_a'], 'm_w_rec_i': out['m_w_rec_i'], 'm_b_rec_i': out['m_b_rec_i'], 'm_lru_lambda': out['m_lru_lambda'], 'm_q_norm_g': out['m_q_norm_g'], 'm_w_uq': out['m_w_uq'], 'm_kv_norm_g': out['m_kv_norm_g'], 'm_w_ukv': out['m_w_ukv'], 'm_w_branch': out['m_w_branch'], 'm_w_out': out['m_w_out'], 'm_norm_ffn_g': out['m_norm_ffn_g'], 'm_w_ffn_in': out['m_w_ffn_in'], 'm_w_ffn_out': out['m_w_ffn_out'], 'm_final_norm_g': out['m_final_norm_g'], 'v_meta_tokens': out['v_meta_tokens'], 'v_norm_mix_g': out['v_norm_mix_g'], 'v_w_in': out['v_w_in'], 'v_b_gate': out['v_b_gate'], 'v_conv_w': out['v_conv_w'], 'v_conv_b': out['v_conv_b'], 'v_w_rec_a': out['v_w_rec_a'], 'v_b_rec_a': out['v_b_rec_a'], 'v_w_rec_i': out['v_w_rec_i'], 'v_b_rec_i': out['v_b_rec_i'], 'v_lru_lambda': out['v_lru_lambda'], 'v_q_norm_g': out['v_q_norm_g'], 'v_w_uq': out['v_w_uq'], 'v_kv_norm_g': out['v_kv_norm_g'], 'v_w_ukv': out['v_w_ukv'], 'v_w_branch': out['v_w_branch'], 'v_w_out': out['v_w_out'], 'v_norm_ffn_g': out['v_norm_ffn_g'], 'v_w_ffn_in': out['v_w_ffn_in'], 'v_w_ffn_out': out['v_w_ffn_out'], 'v_final_norm_g': out['v_final_norm_g']}


def _loss(weights, diff, rest, loss_target):
    with _jax.named_scope("forward"):
        args = {**rest, TWIN_DIFF_INPUT: diff, **{k: w.astype(_WEIGHT_DTYPES[k]) for k, w in weights.items()}}
        y = _forward(args)
    with _jax.named_scope("loss_head"):
        err = _jnp.square(y.astype(_jnp.float32) - loss_target)
        return 0.5 * _jnp.sum(_jnp.mean(err, axis=-1)) if err.ndim else 0.5 * err


def _adamw(w, g, m, v):
    m = ADAM_B1 * m + (1.0 - ADAM_B1) * g
    v = ADAM_B2 * v + (1.0 - ADAM_B2) * _jnp.square(g)
    m_hat = m / (1.0 - ADAM_B1 ** ADAM_STEP)
    v_hat = v / (1.0 - ADAM_B2 ** ADAM_STEP)
    delta = -ADAM_LR * (m_hat / (_jnp.sqrt(v_hat) + ADAM_EPS) + ADAM_WD * w)
    return delta, m, v


def reference(x, meta_tokens, norm_mix_g, w_in, b_gate, conv_w, conv_b, w_rec_a, b_rec_a, w_rec_i, b_rec_i, lru_lambda, q_norm_g, w_uq, kv_norm_g, w_ukv, w_branch, w_out, norm_ffn_g, w_ffn_in, w_ffn_out, final_norm_g, loss_target, m_meta_tokens, m_norm_mix_g, m_w_in, m_b_gate, m_conv_w, m_conv_b, m_w_rec_a, m_b_rec_a, m_w_rec_i, m_b_rec_i, m_lru_lambda, m_q_norm_g, m_w_uq, m_kv_norm_g, m_w_ukv, m_w_branch, m_w_out, m_norm_ffn_g, m_w_ffn_in, m_w_ffn_out, m_final_norm_g, v_meta_tokens, v_norm_mix_g, v_w_in, v_b_gate, v_conv_w, v_conv_b, v_w_rec_a, v_b_rec_a, v_w_rec_i, v_b_rec_i, v_lru_lambda, v_q_norm_g, v_w_uq, v_kv_norm_g, v_w_ukv, v_w_branch, v_w_out, v_norm_ffn_g, v_w_ffn_in, v_w_ffn_out, v_final_norm_g):
    given = dict(x=x, meta_tokens=meta_tokens, norm_mix_g=norm_mix_g, w_in=w_in, b_gate=b_gate, conv_w=conv_w, conv_b=conv_b, w_rec_a=w_rec_a, b_rec_a=b_rec_a, w_rec_i=w_rec_i, b_rec_i=b_rec_i, lru_lambda=lru_lambda, q_norm_g=q_norm_g, w_uq=w_uq, kv_norm_g=kv_norm_g, w_ukv=w_ukv, w_branch=w_branch, w_out=w_out, norm_ffn_g=norm_ffn_g, w_ffn_in=w_ffn_in, w_ffn_out=w_ffn_out, final_norm_g=final_norm_g, loss_target=loss_target, m_meta_tokens=m_meta_tokens, m_norm_mix_g=m_norm_mix_g, m_w_in=m_w_in, m_b_gate=m_b_gate, m_conv_w=m_conv_w, m_conv_b=m_conv_b, m_w_rec_a=m_w_rec_a, m_b_rec_a=m_b_rec_a, m_w_rec_i=m_w_rec_i, m_b_rec_i=m_b_rec_i, m_lru_lambda=m_lru_lambda, m_q_norm_g=m_q_norm_g, m_w_uq=m_w_uq, m_kv_norm_g=m_kv_norm_g, m_w_ukv=m_w_ukv, m_w_branch=m_w_branch, m_w_out=m_w_out, m_norm_ffn_g=m_norm_ffn_g, m_w_ffn_in=m_w_ffn_in, m_w_ffn_out=m_w_ffn_out, m_final_norm_g=m_final_norm_g, v_meta_tokens=v_meta_tokens, v_norm_mix_g=v_norm_mix_g, v_w_in=v_w_in, v_b_gate=v_b_gate, v_conv_w=v_conv_w, v_conv_b=v_conv_b, v_w_rec_a=v_w_rec_a, v_b_rec_a=v_b_rec_a, v_w_rec_i=v_w_rec_i, v_b_rec_i=v_b_rec_i, v_lru_lambda=v_lru_lambda, v_q_norm_g=v_q_norm_g, v_w_uq=v_w_uq, v_kv_norm_g=v_kv_norm_g, v_w_ukv=v_w_ukv, v_w_branch=v_w_branch, v_w_out=v_w_out, v_norm_ffn_g=v_norm_ffn_g, v_w_ffn_in=v_w_ffn_in, v_w_ffn_out=v_w_ffn_out, v_final_norm_g=v_final_norm_g)
    weights = {n: given[n] for n in TWIN_WEIGHTS}
    shared = {n: given[n] for n in SHARED_INPUTS}
    per_example = {n: given[n] for n in ['x']}
    grad_fn = _jax.value_and_grad(_loss, argnums=(0, 1))

    def one_microbatch(ex, loss_target):
        ex = dict(ex)
        diff = ex.pop(TWIN_DIFF_INPUT)
        return grad_fn(weights, diff, {**shared, **ex}, loss_target)

    if N_MICROBATCH == 1:
        loss, (grad_w, grad_x) = one_microbatch(per_example, given["loss_target"])
    else:
        def body(carry, xs):
            loss_sum, grad_sum = carry
            l_k, (gw_k, gx_k) = one_microbatch(xs[0], xs[1])
            with _jax.named_scope("update"):
                return (loss_sum + l_k, _jax.tree.map(_jnp.add, grad_sum, gw_k)), gx_k

        init = (_jnp.zeros((), _jnp.float32), _jax.tree.map(_jnp.zeros_like, weights))
        (loss, grad_w), grad_x = _jax.lax.scan(body, init, (per_example, given["loss_target"]))
    with _jax.named_scope("update"):
        delta_w, new_m, new_v = {}, {}, {}
        for n in TWIN_WEIGHTS:
            delta_w[n], new_m[n], new_v[n] = _adamw(weights[n], grad_w[n], given["m_" + n], given["v_" + n])
    return (loss, grad_x, *[grad_w[n] for n in TWIN_WEIGHTS], *[delta_w[n] for n in TWIN_WEIGHTS],
            *[new_m[n] for n in TWIN_WEIGHTS], *[new_v[n] for n in TWIN_WEIGHTS])
```

```python
import functools
import math

import jax
import jax.numpy as jnp
from jax import lax
from jax.experimental import pallas as pl
from jax.experimental.pallas import tpu as pltpu

F32 = jnp.float32
BF16 = jnp.bfloat16

N_DEV = 8
MESH_AXES = ("x", "y", "c")
LANES = 128
SUBLANES = 8
VMEM_LIMIT = 56 * 1024 * 1024

N_META = 16
CHUNK_SHIFT = 6
CHUNK_BIAS = 64 - N_META
EPS = 1e-6
D_RNN = 1280
RNN_BLOCKS = 10
CONV_WIDTH = 4
LRU_C = 8.0
N_HEADS = 8
QK_NOPE = 128
QK_ROPE = 64
V_DIM = 128
HEAD_PAD = 256
Q_RANK = 384
KV_RANK = 256
ROPE_THETA = 10000.0
ATTN_SCALE = 1.0 / math.sqrt(QK_NOPE + QK_ROPE)
NEG = -1e30
D_FF = 2816

ADAM_LR = 0.001
ADAM_B1 = 0.9
ADAM_B2 = 0.999
ADAM_EPS = 1e-08
ADAM_WD = 0.01
ADAM_STEP = 10

NN = (((1,), (0,)), ((), ()))
NT = (((1,), (1,)), ((), ()))
TN = (((0,), (0,)), ((), ()))


def _pick(n, cap, base=LANES):
    best = None
    for t in range(base, min(n, cap) + 1, base):
        if n % t == 0:
            best = t
    return best if best is not None else n


def _params(sem=None):
    return pltpu.CompilerParams(dimension_semantics=sem, vmem_limit_bytes=VMEM_LIMIT)


def _sig(x):
    return 1.0 / (1.0 + jnp.exp(-x))


def _exchange(srcs, gather, name):
    n = len(srcs)
    out_shape = [jax.ShapeDtypeStruct((N_DEV,) + (s.shape if gather else s.shape[1:]), s.dtype) for s in srcs]

    def body(*refs):
        src, dst = refs[:n], refs[n:2 * n]
        send_sems, recv_sems, local_sems = refs[2 * n:]
        x, y, c = lax.axis_index("x"), lax.axis_index("y"), lax.axis_index("c")
        me = 4 * x + 2 * y + c
        local = []
        for t in range(n):
            cp = pltpu.make_async_copy(src[t] if gather else src[t].at[me], dst[t].at[me], local_sems.at[t])
            cp.start()
            local.append(cp)
        sends, recvs = [], []
        for k in range(1, N_DEV):
            px = 1 - x if k & 4 else x
            py = 1 - y if k & 2 else y
            pc = 1 - c if k & 1 else c
            peer = 4 * px + 2 * py + pc
            for t in range(n):
                cp = pltpu.make_async_remote_copy(
                    src_ref=src[t] if gather else src[t].at[peer], dst_ref=dst[t].at[me],
                    send_sem=send_sems.at[t, k - 1], recv_sem=recv_sems.at[t, k - 1],
                    device_id=(px, py, pc), device_id_type=pl.DeviceIdType.MESH)
                cp.start()
                sends.append(cp)
                recvs.append(pltpu.make_async_remote_copy(
                    src_ref=src[t] if gather else src[t].at[peer], dst_ref=dst[t].at[peer],
                    send_sem=send_sems.at[t, k - 1], recv_sem=recv_sems.at[t, k - 1],
                    device_id=(px, py, pc), device_id_type=pl.DeviceIdType.MESH))
        for cp in recvs:
            cp.wait_recv()
        for cp in sends:
            cp.wait_send()
        for cp in local:
            cp.wait()

    any_spec = pl.BlockSpec(memory_space=pl.ANY)
    return pl.pallas_call(
        body, name=name, out_shape=out_shape,
        in_specs=[any_spec] * n, out_specs=[any_spec] * n,
        scratch_shapes=[pltpu.SemaphoreType.DMA((n, N_DEV - 1)), pltpu.SemaphoreType.DMA((n, N_DEV - 1)),
                        pltpu.SemaphoreType.DMA((n,))],
    )(*srcs)


def _mm(pairs, mode, name, res=None, out_dtype=F32):
    pairs = [p if len(p) == 4 else (p[0], p[1], p[0].shape[1], 0) for p in pairs]
    m = pairs[0][0].shape[0]
    n = pairs[0][1].shape[1] if mode == "nn" else pairs[0][1].shape[0]
    tm, tn = _pick(m, 640), _pick(n, 640)
    np_ = len(pairs)
    dims = NN if mode == "nn" else NT

    def body(*refs):
        acc = None
        for s in range(np_):
            d = lax.dot_general(refs[2 * s][...].astype(BF16), refs[2 * s + 1][...].astype(BF16), dims,
                                preferred_element_type=F32)
            acc = d if acc is None else acc + d
        if res is not None:
            acc = acc + refs[2 * np_][...]
        refs[-1][...] = acc.astype(out_dtype)

    in_specs, args = [], []
    for a, b, kt, kb in pairs:
        in_specs.append(pl.BlockSpec((tm, kt), lambda i, j, kb=kb: (i, kb)))
        if mode == "nn":
            in_specs.append(pl.BlockSpec((kt, tn), lambda i, j, kb=kb: (kb, j)))
        else:
            in_specs.append(pl.BlockSpec((tn, kt), lambda i, j, kb=kb: (j, kb)))
        args += [a, b]
    if res is not None:
        in_specs.append(pl.BlockSpec((tm, tn), lambda i, j: (i, j)))
        args.append(res)
    return pl.pallas_call(
        body, name=name, grid=(m // tm, n // tn), in_specs=in_specs,
        out_specs=pl.BlockSpec((tm, tn), lambda i, j: (i, j)),
        out_shape=jax.ShapeDtypeStruct((m, n), out_dtype),
        compiler_params=_params(("parallel", "parallel")),
    )(*args)


def _mm_tn(a, b, name):
    m, k = a.shape
    n = b.shape[1]
    tm, tk, tn = _pick(m, 640), _pick(k, 1408), _pick(n, 640)

    def body(a_ref, b_ref, o_ref):
        @pl.when(pl.program_id(2) == 0)
        def _():
            o_ref[...] = jnp.zeros_like(o_ref)

        o_ref[...] += lax.dot_general(a_ref[...].astype(BF16), b_ref[...].astype(BF16), TN,
                                      preferred_element_type=F32)

    return pl.pallas_call(
        body, name=name, grid=(k // tk, n // tn, m // tm),
        in_specs=[pl.BlockSpec((tm, tk), lambda i, j, r: (r, i)), pl.BlockSpec((tm, tn), lambda i, j, r: (r, j))],
        out_specs=pl.BlockSpec((tk, tn), lambda i, j, r: (i, j)),
        out_shape=jax.ShapeDtypeStruct((k, n), F32),
        compiler_params=_params(("parallel", "parallel", "arbitrary")),
    )(a, b)


ROW_TILE_BYTES = 6 * 1024 * 1024


def _row_tile(rows, row_in, row_out):
    per_row = sum((r[1] * r[0].dtype.itemsize) if isinstance(r, tuple) else (r.shape[1] * r.dtype.itemsize)
                  for r in row_in)
    per_row += sum(w * jnp.dtype(dt).itemsize for w, dt in row_out)
    return _pick(rows, min(640, max(LANES, ROW_TILE_BYTES // per_row)))


def _rowcall(body, name, rows, row_in, full_in, row_out, acc_out=()):
    tr = _row_tile(rows, row_in, row_out)
    n_steps = rows // tr
    in_specs, args = [], []
    for r in row_in:
        arr, w, cb = r if isinstance(r, tuple) else (r, r.shape[1], 0)
        in_specs.append(pl.BlockSpec((tr, w), lambda i, cb=cb: (i, cb)))
        args.append(arr)
    for f in full_in:
        in_specs.append(pl.BlockSpec(f.shape, lambda i, nd=f.ndim: (0,) * nd))
        args.append(f)
    out_specs = [pl.BlockSpec((tr, w), lambda i: (i, 0)) for w, _ in row_out]
    out_shape = [jax.ShapeDtypeStruct((rows, w), dt) for w, dt in row_out]
    for shp, dt in acc_out:
        out_specs.append(pl.BlockSpec(shp, lambda i, nd=len(shp): (0,) * nd))
        out_shape.append(jax.ShapeDtypeStruct(shp, dt))

    def wrapped(*refs):
        body(pl.program_id(0), n_steps, *refs)

    return pl.pallas_call(
        wrapped, name=name, grid=(n_steps,), in_specs=in_specs, out_specs=out_specs, out_shape=out_shape,
        compiler_params=_params(("arbitrary",) if acc_out else ("parallel",)),
    )(*args)


def _rmsnorm_fwd(x, g, name):
    rows, w = x.shape

    def body(i, n, x_ref, g_ref, o_ref):
        xv = x_ref[...]
        r = lax.rsqrt(jnp.mean(xv * xv, axis=-1, keepdims=True) + EPS)
        o_ref[...] = (xv * r * g_ref[...]).astype(BF16)

    return _rowcall(body, name, rows,[x], [g], [(w, BF16)])[0]


def _rmsnorm_bwd_math(xv, dy, g):
    w = xv.shape[-1]
    r = lax.rsqrt(jnp.mean(xv * xv, axis=-1, keepdims=True) + EPS)
    t = dy * g
    dx = r * t - xv * (r * r * r * (jnp.sum(t * xv, axis=-1, keepdims=True) / w))
    dg = jnp.sum(dy * xv * r, axis=0, keepdims=True)
    return dx, dg


def _rmsnorm_bwd(x, dy, g, name, res=None, out_dtype=F32):
    rows, w = x.shape

    def body(i, n, *refs):
        x_ref, dy_ref = refs[0], refs[1]
        g_ref, dx_ref, dg_ref = refs[-3], refs[-2], refs[-1]
        dx, dg = _rmsnorm_bwd_math(x_ref[...], dy_ref[...], g_ref[...])
        if res is not None:
            dx = dx + refs[2][...]
        dx_ref[...] = dx.astype(out_dtype)

        @pl.when(i == 0)
        def _():
            dg_ref[...] = jnp.zeros_like(dg_ref)

        dg_ref[...] += dg

    row_in = [x, dy] + ([res] if res is not None else [])
    return _rowcall(body, name, rows,row_in, [g], [(w, out_dtype)], [((1, w), F32)])


def _loss_bwd(h2, tgt, g, seq, name):
    rows, w = h2.shape
    tr = _row_tile(rows, [h2, tgt], [(w, F32)])

    def body(i, n, h_ref, t_ref, g_ref, dh_ref, dg_ref, lcol_ref, loss_ref):
        hv, gv = h_ref[...], g_ref[...]
        row = i * tr + lax.broadcasted_iota(jnp.int32, (tr, w), 0)
        valid = jnp.logical_and(row >= N_META, row < N_META + seq)
        r = lax.rsqrt(jnp.mean(hv * hv, axis=-1, keepdims=True) + EPS)
        err = jnp.where(valid, hv * r * gv - t_ref[...], 0.0)
        dx, dg = _rmsnorm_bwd_math(hv, err * (1.0 / w), gv)
        dh_ref[...] = dx

        @pl.when(i == 0)
        def _():
            dg_ref[...] = jnp.zeros_like(dg_ref)
            lcol_ref[...] = jnp.zeros_like(lcol_ref)

        dg_ref[...] += dg
        lcol_ref[...] += jnp.sum(err * err, axis=0, keepdims=True)

        @pl.when(i == n - 1)
        def _():
            total = jnp.sum(lcol_ref[...], axis=1, keepdims=True) * (0.5 / w)
            loss_ref[...] = jnp.broadcast_to(total, loss_ref.shape)

    return _rowcall(body, name, rows, [h2, tgt], [g], [(w, F32)],
                    [((1, w), F32), ((1, w), F32), ((1, LANES), F32)])


def _mix_fwd(um, p_rnn, p_att, bg, name):
    rows, d = p_rnn.shape

    def body(i, n, u0_ref, u1_ref, pr_ref, pa_ref, bg_ref, o_ref):
        g0 = _sig(u0_ref[...] + bg_ref[:, :d])
        g1 = _sig(u1_ref[...] + bg_ref[:, d:])
        o_ref[...] = (g0 * pr_ref[...] + g1 * pa_ref[...]).astype(BF16)

    return _rowcall(body, name, rows,[(um, d, 0), (um, d, 1), p_rnn, p_att], [bg],
                    [(d, BF16)])[0]


def _mix_bwd(um, p_rnn, p_att, dmix, bg, name):
    rows, d = p_rnn.shape

    def body(i, n, u0_ref, u1_ref, pr_ref, pa_ref, dm_ref, bg_ref, dpr_ref, dpa_ref, dum_ref, dbg_ref):
        g0 = _sig(u0_ref[...] + bg_ref[:, :d])
        g1 = _sig(u1_ref[...] + bg_ref[:, d:])
        dm = dm_ref[...]
        dpr_ref[...] = (dm * g0).astype(BF16)
        dpa_ref[...] = (dm * g1).astype(BF16)
        du0 = dm * pr_ref[...] * g0 * (1.0 - g0)
        du1 = dm * pa_ref[...] * g1 * (1.0 - g1)
        dum_ref[:, :d] = du0.astype(BF16)
        dum_ref[:, d:] = du1.astype(BF16)

        @pl.when(i == 0)
        def _():
            dbg_ref[...] = jnp.zeros_like(dbg_ref)

        dbg_ref[:, :d] += jnp.sum(du0, axis=0, keepdims=True)
        dbg_ref[:, d:] += jnp.sum(du1, axis=0, keepdims=True)

    return _rowcall(body, name, rows,[(um, d, 0), (um, d, 1), p_rnn, p_att, dmix], [bg],
                    [(d, BF16), (d, BF16), (2 * d, BF16)], [((1, 2 * d), F32)])


def _swiglu_fwd(gu, name):
    rows, w2 = gu.shape
    f = w2 // 2

    def body(i, n, g_ref, u_ref, o_ref):
        gate = g_ref[...]
        o_ref[...] = (gate * _sig(gate) * u_ref[...]).astype(BF16)

    return _rowcall(body, name, rows,[(gu, f, 0), (gu, f, 1)], [], [(f, BF16)])[0]


def _swiglu_bwd(gu, dact, name):
    rows, w2 = gu.shape
    f = w2 // 2

    def body(i, n, g_ref, u_ref, da_ref, o_ref):
        gate, da = g_ref[...], da_ref[...]
        sg = _sig(gate)
        o_ref[:, :f] = (da * u_ref[...] * (sg * (1.0 + gate * (1.0 - sg)))).astype(BF16)
        o_ref[:, f:] = (da * gate * sg).astype(BF16)

    return _rowcall(body, name, rows,[(gu, f, 0), (gu, f, 1), dact], [], [(w2, BF16)])[0]


def _rope_tables(lp):
    idx = jnp.arange(lp, dtype=jnp.int32).astype(F32)
    inv_freq = ROPE_THETA ** (-jnp.arange(0, QK_ROPE, 2, dtype=F32) / QK_ROPE)
    ang = idx[:, None] * inv_freq[None, :]
    cos, sin = jnp.cos(ang), jnp.sin(ang)
    half = QK_ROPE // 2
    z = lambda wdt: jnp.zeros((lp, wdt), F32)
    tc = jnp.concatenate([cos, cos, z(LANES - 2 * half)], axis=1)
    ts1 = jnp.concatenate([-sin, z(LANES - half)], axis=1)
    ts2 = jnp.concatenate([z(half), sin, z(LANES - 2 * half)], axis=1)
    return tc, ts1, ts2


def _rope(xv, tc, ts1, ts2):
    half = QK_ROPE // 2
    return xv * tc + pltpu.roll(xv, LANES - half, 1) * ts1 + pltpu.roll(xv, half, 1) * ts2


def _rope_t(dv, tc, ts1, ts2):
    half = QK_ROPE // 2
    return dv * tc + pltpu.roll(dv * ts1, half, 1) + pltpu.roll(dv * ts2, LANES - half, 1)


def _rope_fwd(qpad, kpad, ukr, tabs, name):
    rows, w = qpad.shape

    def body(i, n, q_ref, k_ref, r_ref, c_ref, s1_ref, s2_ref, qo_ref, ko_ref):
        tc, ts1, ts2 = c_ref[...], s1_ref[...], s2_ref[...]
        kr = _rope(r_ref[...], tc, ts1, ts2).astype(BF16)
        for h in range(N_HEADS):
            lo, mid, hi = h * HEAD_PAD, h * HEAD_PAD + QK_NOPE, (h + 1) * HEAD_PAD
            qo_ref[:, lo:mid] = q_ref[:, lo:mid].astype(BF16)
            qo_ref[:, mid:hi] = _rope(q_ref[:, mid:hi], tc, ts1, ts2).astype(BF16)
            ko_ref[:, lo:mid] = k_ref[:, lo:mid].astype(BF16)
            ko_ref[:, mid:hi] = kr

    return _rowcall(body, name, rows,[qpad, kpad, ukr, *tabs], [], [(w, BF16), (w, BF16)])


def _rope_bwd(dq, dk, tabs, name):
    rows, w = dq.shape

    def body(i, n, dq_ref, dk_ref, c_ref, s1_ref, s2_ref, qo_ref, ko_ref, ro_ref):
        tc, ts1, ts2 = c_ref[...], s1_ref[...], s2_ref[...]
        dkr = None
        for h in range(N_HEADS):
            lo, mid, hi = h * HEAD_PAD, h * HEAD_PAD + QK_NOPE, (h + 1) * HEAD_PAD
            qo_ref[:, lo:mid] = dq_ref[:, lo:mid].astype(BF16)
            qo_ref[:, mid:hi] = _rope_t(dq_ref[:, mid:hi], tc, ts1, ts2).astype(BF16)
            ko_ref[:, lo:mid] = dk_ref[:, lo:mid].astype(BF16)
            ko_ref[:, mid:hi] = jnp.zeros((ko_ref.shape[0], hi - mid), BF16)
            part = dk_ref[:, mid:hi]
            dkr = part if dkr is None else dkr + part
        ro_ref[...] = _rope_t(dkr, tc, ts1, ts2).astype(BF16)

    return _rowcall(body, name, rows,[dq, dk, *tabs], [],
                    [(w, BF16), (w, BF16), (LANES, BF16)])


def _visible(q0, k0, tq, tk):
    qrow = q0 + lax.broadcasted_iota(jnp.int32, (tq, tk), 0)
    kcol = k0 + lax.broadcasted_iota(jnp.int32, (tq, tk), 1)
    return ((kcol + CHUNK_BIAS) >> CHUNK_SHIFT) <= ((qrow + CHUNK_BIAS) >> CHUNK_SHIFT)


def _lanes(v, width):
    return jnp.tile(v, (1, width // LANES))


def _attn_fwd(q, k, v, t, name):
    lp = q.shape[0]
    nt = lp // t

    def body(q_ref, k_ref, v_ref, o_ref, lse_ref, m_s, l_s, acc_s):
        i = pl.program_id(1)
        qv = q_ref[...]
        m_s[...] = jnp.full(m_s.shape, NEG, F32)
        l_s[...] = jnp.zeros(l_s.shape, F32)
        acc_s[...] = jnp.zeros(acc_s.shape, F32)

        def step(j, masked):
            r0 = pl.multiple_of(j * t, t)
            kv_, vv = k_ref[pl.ds(r0, t), :], v_ref[pl.ds(r0, t), :]
            s = lax.dot_general(qv, kv_, NT, preferred_element_type=F32) * ATTN_SCALE
            if masked:
                s = jnp.where(_visible(i * t, j * t, t, t), s, NEG)
            m_prev = m_s[...]
            m_new = jnp.maximum(m_prev, jnp.max(s, axis=1, keepdims=True))
            alpha = jnp.exp(m_prev - m_new)
            p = jnp.exp(s - _lanes(m_new, t))
            l_s[...] = alpha * l_s[...] + jnp.sum(p, axis=1, keepdims=True)
            acc_s[...] = alpha * acc_s[...] + jnp.dot(p.astype(BF16), vv, preferred_element_type=F32)
            m_s[...] = m_new

        def full_step(j, c):
            step(j, False)
            return c

        def diag_step(j, c):
            step(j, True)
            return c

        lax.fori_loop(0, i, full_step, 0)
        lax.fori_loop(i, jnp.minimum(i + 2, nt), diag_step, 0)
        o_ref[...] = (acc_s[...] / l_s[...]).astype(BF16)
        lse_ref[...] = m_s[...] + jnp.log(l_s[...])

    return pl.pallas_call(
        body, name=name, grid=(N_HEADS, nt),
        in_specs=[pl.BlockSpec((t, HEAD_PAD), lambda h, i: (i, h)),
                  pl.BlockSpec((lp, HEAD_PAD), lambda h, i: (0, h)),
                  pl.BlockSpec((lp, V_DIM), lambda h, i: (0, h))],
        out_specs=[pl.BlockSpec((t, V_DIM), lambda h, i: (i, h)),
                   pl.BlockSpec((None, t, LANES), lambda h, i: (h, i, 0))],
        out_shape=[jax.ShapeDtypeStruct((lp, N_HEADS * V_DIM), BF16),
                   jax.ShapeDtypeStruct((N_HEADS, lp, LANES), F32)],
        scratch_shapes=[pltpu.VMEM((t, LANES), F32), pltpu.VMEM((t, LANES), F32), pltpu.VMEM((t, V_DIM), F32)],
        compiler_params=_params(("parallel", "arbitrary")),
    )(q, k, v)


def _attn_bwd_dq(q, k, v, do, o, lse, t, name):
    lp = q.shape[0]
    nt = lp // t

    def body(q_ref, k_ref, v_ref, do_ref, o_ref, lse_ref, dq_ref, dl_ref, acc_s):
        i = pl.program_id(1)
        qv, dov = q_ref[...], do_ref[...]
        delta = jnp.sum(dov.astype(F32) * o_ref[...].astype(F32), axis=1, keepdims=True)
        dl_ref[...] = jnp.broadcast_to(delta, dl_ref.shape)
        lse_w = _lanes(lse_ref[...], t)
        acc_s[...] = jnp.zeros(acc_s.shape, F32)

        def step(j, masked):
            r0 = pl.multiple_of(j * t, t)
            kv_, vv = k_ref[pl.ds(r0, t), :], v_ref[pl.ds(r0, t), :]
            s = lax.dot_general(qv, kv_, NT, preferred_element_type=F32) * ATTN_SCALE
            if masked:
                s = jnp.where(_visible(i * t, j * t, t, t), s, NEG)
            p = jnp.exp(s - lse_w)
            dp = lax.dot_general(dov, vv, NT, preferred_element_type=F32)
            ds = (p * (dp - delta) * ATTN_SCALE).astype(BF16)
            acc_s[...] += jnp.dot(ds, kv_, preferred_element_type=F32)

        def full_step(j, c):
            step(j, False)
            return c

        def diag_step(j, c):
            step(j, True)
            return c

        lax.fori_loop(0, i, full_step, 0)
        lax.fori_loop(i, jnp.minimum(i + 2, nt), diag_step, 0)
        dq_ref[...] = acc_s[...]

    return pl.pallas_call(
        body, name=name, grid=(N_HEADS, nt),
        in_specs=[pl.BlockSpec((t, HEAD_PAD), lambda h, i: (i, h)),
                  pl.BlockSpec((lp, HEAD_PAD), lambda h, i: (0, h)),
                  pl.BlockSpec((lp, V_DIM), lambda h, i: (0, h)),
                  pl.BlockSpec((t, V_DIM), lambda h, i: (i, h)),
                  pl.BlockSpec((t, V_DIM), lambda h, i: (i, h)),
                  pl.BlockSpec((None, t, LANES), lambda h, i: (h, i, 0))],
        out_specs=[pl.BlockSpec((t, HEAD_PAD), lambda h, i: (i, h)),
                   pl.BlockSpec((None, t, LANES), lambda h, i: (h, i, 0))],
        out_shape=[jax.ShapeDtypeStruct((lp, N_HEADS * HEAD_PAD), F32),
                   jax.ShapeDtypeStruct((N_HEADS, lp, LANES), F32)],
        scratch_shapes=[pltpu.VMEM((t, HEAD_PAD), F32)],
        compiler_params=_params(("parallel", "arbitrary")),
    )(q, k, v, do, o, lse)


def _attn_bwd_dkv(q, k, v, do, lse, delta, t, name):
    lp = q.shape[0]
    nt = lp // t

    def body(q_ref, k_ref, v_ref, do_ref, lse_ref, dl_ref, dk_ref, dv_ref, dk_s, dv_s):
        j = pl.program_id(1)
        kv_, vv = k_ref[...], v_ref[...]
        dk_s[...] = jnp.zeros(dk_s.shape, F32)
        dv_s[...] = jnp.zeros(dv_s.shape, F32)

        def step(i, masked):
            r0 = pl.multiple_of(i * t, t)
            qv, dov = q_ref[pl.ds(r0, t), :], do_ref[pl.ds(r0, t), :]
            s = lax.dot_general(qv, kv_, NT, preferred_element_type=F32) * ATTN_SCALE
            if masked:
                s = jnp.where(_visible(i * t, j * t, t, t), s, NEG)
            p = jnp.exp(s - _lanes(lse_ref[pl.ds(r0, t), :], t))
            dv_s[...] += lax.dot_general(p.astype(BF16), dov, TN, preferred_element_type=F32)
            dp = lax.dot_general(dov, vv, NT, preferred_element_type=F32)
            ds = (p * (dp - _lanes(dl_ref[pl.ds(r0, t), :], t)) * ATTN_SCALE).astype(BF16)
            dk_s[...] += lax.dot_general(ds, qv, TN, preferred_element_type=F32)

        def diag_step(i, c):
            step(i, True)
            return c

        def full_step(i, c):
            step(i, False)
            return c

        lax.fori_loop(jnp.maximum(j - 1, 0), jnp.minimum(j + 1, nt), diag_step, 0)
        lax.fori_loop(j + 1, nt, full_step, 0)
        dk_ref[...] = dk_s[...]
        dv_ref[...] = dv_s[...].astype(BF16)

    return pl.pallas_call(
        body, name=name, grid=(N_HEADS, nt),
        in_specs=[pl.BlockSpec((lp, HEAD_PAD), lambda h, j: (0, h)),
                  pl.BlockSpec((t, HEAD_PAD), lambda h, j: (j, h)),
                  pl.BlockSpec((t, V_DIM), lambda h, j: (j, h)),
                  pl.BlockSpec((lp, V_DIM), lambda h, j: (0, h)),
                  pl.BlockSpec((None, lp, LANES), lambda h, j: (h, 0, 0)),
                  pl.BlockSpec((None, lp, LANES), lambda h, j: (h, 0, 0))],
        out_specs=[pl.BlockSpec((t, HEAD_PAD), lambda h, j: (j, h)),
                   pl.BlockSpec((t, V_DIM), lambda h, j: (j, h))],
        out_shape=[jax.ShapeDtypeStruct((lp, N_HEADS * HEAD_PAD), F32),
                   jax.ShapeDtypeStruct((lp, N_HEADS * V_DIM), BF16)],
        scratch_shapes=[pltpu.VMEM((t, HEAD_PAD), F32), pltpu.VMEM((t, V_DIM), F32)],
        compiler_params=_params(("parallel", "arbitrary")),
    )(q, k, v, do, lse, delta)


def _shift_down(cur, prev8, k):
    r = pltpu.roll(cur, k, 0)
    row8 = lax.broadcasted_iota(jnp.int32, prev8.shape, 0)
    first = jnp.where(row8 < k, pltpu.roll(prev8, k, 0), r[0:SUBLANES])
    return jnp.concatenate([first, r[SUBLANES:]], axis=0)


def _shift_up(cur, next8, k):
    t = cur.shape[0]
    r = pltpu.roll(cur, t - k, 0)
    row8 = lax.broadcasted_iota(jnp.int32, next8.shape, 0)
    last = jnp.where(row8 >= SUBLANES - k, pltpu.roll(next8, SUBLANES - k, 0), r[t - SUBLANES:])
    return jnp.concatenate([r[:t - SUBLANES], last], axis=0)


def _scan_down(a, b):
    t = a.shape[0]
    row = lax.broadcasted_iota(jnp.int32, a.shape, 0)
    s = 1
    while s < t:
        keep = row >= s
        a_sh = jnp.where(keep, pltpu.roll(a, s, 0), 1.0)
        b_sh = jnp.where(keep, pltpu.roll(b, s, 0), 0.0)
        b = a * b_sh + b
        a = a * a_sh
        s *= 2
    return a, b


def _scan_up(a, b):
    t = a.shape[0]
    row = lax.broadcasted_iota(jnp.int32, a.shape, 0)
    s = 1
    while s < t:
        keep = row < t - s
        a_sh = jnp.where(keep, pltpu.roll(a, t - s, 0), 1.0)
        b_sh = jnp.where(keep, pltpu.roll(b, t - s, 0), 0.0)
        b = a * b_sh + b
        a = a * a_sh
        s *= 2
    return a, b


def _neg_expm1(y):
    series = -y * (1.0 + y * (0.5 + y * (1.0 / 6.0 + y * (1.0 / 24.0 + y * (1.0 / 120.0)))))
    return jnp.where(y > -0.1, series, 1.0 - jnp.exp(y))


def _log_sigmoid(x):
    return jnp.minimum(x, 0.0) - jnp.log(1.0 + jnp.exp(-jnp.abs(x)))


GELU_C = math.sqrt(2.0 / math.pi)
GELU_K = 0.044715


def _gelu(x):
    th = jnp.tanh(GELU_C * (x + GELU_K * x * x * x))
    return 0.5 * x * (1.0 + th), th


def _block_mm(xb, w_ref, dims):
    rb = D_RNN // RNN_BLOCKS
    return jnp.concatenate(
        [lax.dot_general(xb[:, h * rb:(h + 1) * rb], w_ref[h], dims, preferred_element_type=F32)
         for h in range(RNN_BLOCKS)], axis=1)


def _rglru_gates(ux, prev8, pv_ref, wa_ref, wi_ref):
    shifted = [ux] + [_shift_down(ux, prev8, k) for k in range(1, CONV_WIDTH)]
    xc = pv_ref[4:5, :] + pv_ref[3:4, :] * ux
    for k in range(1, CONV_WIDTH):
        xc = xc + pv_ref[3 - k:4 - k, :] * shifted[k]
    xcb = xc.astype(BF16)
    r_g = _sig(_block_mm(xcb, wa_ref, NN) + pv_ref[5:6, :])
    i_g = _sig(_block_mm(xcb, wi_ref, NN) + pv_ref[6:7, :])
    log_a = LRU_C * r_g * _log_sigmoid(pv_ref[7:8, :])
    a = jnp.exp(log_a)
    mm = jnp.sqrt(_neg_expm1(2.0 * log_a))
    return dict(shifted=shifted, xc=xc, xcb=xcb, r=r_g, i=i_g, a=a, mm=mm)


def _rglru_fwd(ux, ug, pv, wa, wi, t, name):
    lp, d = ux.shape

    def body(ux_ref, ug_ref, pv_ref, wa_ref, wi_ref, y_ref, h_ref, tail_s, hc_s):
        @pl.when(pl.program_id(0) == 0)
        def _():
            tail_s[...] = jnp.zeros_like(tail_s)
            hc_s[...] = jnp.zeros_like(hc_s)

        uxv = ux_ref[...]
        gt = _rglru_gates(uxv, tail_s[...], pv_ref, wa_ref, wi_ref)
        tail_s[...] = ux_ref[t - SUBLANES:t, :]
        cum_a, hloc = _scan_down(gt["a"], gt["mm"] * (gt["i"] * gt["xc"]))
        h_ref[...] = hloc + cum_a * hc_s[0:1, :]
        hc_s[...] = h_ref[t - SUBLANES:t, :]
        hc_s[0:1, :] = h_ref[t - 1:t, :]
        y_ref[...] = (h_ref[...] * _gelu(ug_ref[...])[0]).astype(BF16)

    tile = pl.BlockSpec((t, d), lambda i: (i, 0))
    return pl.pallas_call(
        body, name=name, grid=(lp // t,),
        in_specs=[tile, tile, pl.BlockSpec(pv.shape, lambda i: (0, 0)),
                  pl.BlockSpec(wa.shape, lambda i: (0, 0, 0)), pl.BlockSpec(wi.shape, lambda i: (0, 0, 0))],
        out_specs=[tile, tile],
        out_shape=[jax.ShapeDtypeStruct((lp, d), BF16), jax.ShapeDtypeStruct((lp, d), F32)],
        scratch_shapes=[pltpu.VMEM((SUBLANES, d), F32), pltpu.VMEM((SUBLANES, d), F32)],
        compiler_params=_params(("arbitrary",)),
    )(ux, ug, pv, wa, wi)


def _rglru_bwd(ux, ug, hs, dy, pv, wa, wi, t, name):
    lp, d = ux.shape
    nt = lp // t
    per = t // SUBLANES
    rb = d // RNN_BLOCKS

    def body(ux_ref, uxp_ref, ug_ref, h_ref, hp_ref, dy_ref, pv_ref, wa_ref, wi_ref,
             dux_ref, dug_ref, dpv_ref, dwa_ref, dwi_ref, ca_s, cg_s, cx_s):
        step = pl.program_id(0)
        first_tile = step == nt - 1

        @pl.when(step == 0)
        def _():
            for ref in (ca_s, cg_s, cx_s, dpv_ref, dwa_ref, dwi_ref):
                ref[...] = jnp.zeros_like(ref)

        uxv = ux_ref[...]
        prev8 = jnp.where(first_tile, 0.0, uxp_ref[...])
        hprev8 = jnp.where(first_tile, 0.0, hp_ref[...])
        gt = _rglru_gates(uxv, prev8, pv_ref, wa_ref, wi_ref)
        a, mm, r_g, i_g, xc = gt["a"], gt["mm"], gt["r"], gt["i"], gt["xc"]
        hv = h_ref[...]
        hprev = _shift_down(hv, hprev8, 1)
        ugv, dyv = ug_ref[...], dy_ref[...]
        gel, th = _gelu(ugv)
        dgel = 0.5 * (1.0 + th) + 0.5 * ugv * (1.0 - th * th) * (GELU_C * (1.0 + 3.0 * GELU_K * ugv * ugv))
        dug_ref[...] = (dyv * hv * dgel).astype(BF16)
        a_up = _shift_up(a, ca_s[...], 1)
        cum_a, gloc = _scan_up(a_up, dyv * gel)
        gv = gloc + cum_a * cg_s[0:1, :]
        ca_s[...] = a[0:SUBLANES]
        cg_s[...] = gv[0:SUBLANES]
        ixc = i_g * xc
        d_ixc = gv * mm
        d_log_a = gv * hprev * a - (gv * ixc) * (a * a) / mm
        logsig = _log_sigmoid(pv_ref[7:8, :])
        d_pre_a = d_log_a * (LRU_C * logsig) * r_g * (1.0 - r_g)
        d_pre_i = d_ixc * xc * i_g * (1.0 - i_g)
        dab, dib = d_pre_a.astype(BF16), d_pre_i.astype(BF16)
        d_xc = d_ixc * i_g + _block_mm(dab, wa_ref, NT) + _block_mm(dib, wi_ref, NT)
        xcb = gt["xcb"]
        for h in range(RNN_BLOCKS):
            cols = slice(h * rb, (h + 1) * rb)
            dwa_ref[h] += lax.dot_general(xcb[:, cols], dab[:, cols], TN, preferred_element_type=F32)
            dwi_ref[h] += lax.dot_general(xcb[:, cols], dib[:, cols], TN, preferred_element_type=F32)
        csum = lambda v: jnp.sum(v, axis=0, keepdims=True)
        for k in range(CONV_WIDTH):
            dpv_ref[3 - k:4 - k, :] += csum(d_xc * gt["shifted"][k])
        dpv_ref[4:5, :] += csum(d_xc)
        dpv_ref[5:6, :] += csum(d_pre_a)
        dpv_ref[6:7, :] += csum(d_pre_i)
        dpv_ref[7:8, :] += csum(d_log_a * (LRU_C * r_g)) * _sig(-pv_ref[7:8, :])
        dux = pv_ref[3:4, :] * d_xc
        for k in range(1, CONV_WIDTH):
            dux = dux + pv_ref[3 - k:4 - k, :] * _shift_up(d_xc, cx_s[...], k)
        cx_s[...] = d_xc[0:SUBLANES]
        dux_ref[...] = dux.astype(BF16)

    rev = lambda i: (nt - 1 - i, 0)
    before = lambda i: (jnp.maximum((nt - 1 - i) * per - 1, 0), 0)
    tile = pl.BlockSpec((t, d), rev)
    tail = pl.BlockSpec((SUBLANES, d), before)
    fixed2 = lambda arr: pl.BlockSpec(arr.shape, lambda i: (0, 0))
    fixed3 = lambda arr: pl.BlockSpec(arr.shape, lambda i: (0, 0, 0))
    return pl.pallas_call(
        body, name=name, grid=(nt,),
        in_specs=[tile, tail, tile, tile, tail, tile, fixed2(pv), fixed3(wa), fixed3(wi)],
        out_specs=[tile, tile, fixed2(pv), fixed3(wa), fixed3(wi)],
        out_shape=[jax.ShapeDtypeStruct((lp, d), BF16), jax.ShapeDtypeStruct((lp, d), BF16),
                   jax.ShapeDtypeStruct(pv.shape, F32), jax.ShapeDtypeStruct(wa.shape, F32),
                   jax.ShapeDtypeStruct(wi.shape, F32)],
        scratch_shapes=[pltpu.VMEM((SUBLANES, d), F32)] * 3,
        compiler_params=_params(("arbitrary",)),
    )(ux, ux, ug, hs, hs, dy, pv, wa, wi)


def _adamw(w, m, v, parts, name):
    rows, cols = w.shape
    tr = _pick(rows, 256, SUBLANES)
    c1 = 1.0 / (1.0 - ADAM_B1 ** ADAM_STEP)
    c2 = 1.0 / (1.0 - ADAM_B2 ** ADAM_STEP)

    def body(w_ref, m_ref, v_ref, p_ref, g_ref, d_ref, mo_ref, vo_ref):
        g = p_ref[0]
        for q in range(1, N_DEV):
            g = g + p_ref[q]
        mn = ADAM_B1 * m_ref[...] + (1.0 - ADAM_B1) * g
        vn = ADAM_B2 * v_ref[...] + (1.0 - ADAM_B2) * (g * g)
        g_ref[...] = g
        mo_ref[...] = mn
        vo_ref[...] = vn
        d_ref[...] = -ADAM_LR * ((mn * c1) / (jnp.sqrt(vn * c2) + ADAM_EPS) + ADAM_WD * w_ref[...])

    blk = pl.BlockSpec((tr, cols), lambda i: (i, 0))
    return pl.pallas_call(
        body, name=name, grid=(rows // tr,),
        in_specs=[blk, blk, blk, pl.BlockSpec((N_DEV, tr, cols), lambda i: (0, i, 0))],
        out_specs=[blk] * 4, out_shape=[jax.ShapeDtypeStruct((rows, cols), F32)] * 4,
        compiler_params=_params(("parallel",)),
    )(w, m, v, parts)


WEIGHTS = ("meta_tokens", "norm_mix_g", "w_in", "b_gate", "conv_w", "conv_b", "w_rec_a", "b_rec_a", "w_rec_i",
           "b_rec_i", "lru_lambda", "q_norm_g", "w_uq", "kv_norm_g", "w_ukv", "w_branch", "w_out", "norm_ffn_g",
           "w_ffn_in", "w_ffn_out", "final_norm_g")
SHARDED = {"meta_tokens": True, "w_in": True, "b_gate": True, "conv_w": True, "w_uq": True, "w_ukv": True,
           "w_branch": False, "w_out": False, "w_ffn_in": True, "w_ffn_out": False}
REPLICATED = tuple(n for n in WEIGHTS if n not in SHARDED)


def _as2d(a):
    return a.reshape(-1, a.shape[-1])


def _full_from_gathered(g, by_cols):
    if by_cols:
        return jnp.transpose(g, (1, 0, 2)).reshape(g.shape[1], N_DEV * g.shape[2])
    return g.reshape(N_DEV * g.shape[1], g.shape[2])


def _blocks_from_full(full, by_cols):
    if by_cols:
        r, c = full.shape
        return jnp.transpose(full.reshape(r, N_DEV, c // N_DEV), (1, 0, 2))
    return full.reshape(N_DEV, full.shape[0] // N_DEV, full.shape[1])


def _pack(arrs):
    flat = jnp.concatenate([a.reshape(-1) for a in arrs])
    rows = -(-flat.shape[0] // (LANES * SUBLANES)) * SUBLANES
    return jnp.pad(flat, (0, rows * LANES - flat.shape[0])).reshape(rows, LANES)


def _unpack(packed, like):
    flat = packed.reshape(-1)
    out, off = [], 0
    for a in like:
        out.append(flat[off:off + a.size].reshape(a.shape))
        off += a.size
    return out


def kernel(x, meta_tokens, norm_mix_g, w_in, b_gate, conv_w, conv_b, w_rec_a, b_rec_a, w_rec_i, b_rec_i, lru_lambda, q_norm_g, w_uq, kv_norm_g, w_ukv, w_branch, w_out, norm_ffn_g, w_ffn_in, w_ffn_out, final_norm_g, loss_target, m_meta_tokens, m_norm_mix_g, m_w_in, m_b_gate, m_conv_w, m_conv_b, m_w_rec_a, m_b_rec_a, m_w_rec_i, m_b_rec_i, m_lru_lambda, m_q_norm_g, m_w_uq, m_kv_norm_g, m_w_ukv, m_w_branch, m_w_out, m_norm_ffn_g, m_w_ffn_in, m_w_ffn_out, m_final_norm_g, v_meta_tokens, v_norm_mix_g, v_w_in, v_b_gate, v_conv_w, v_conv_b, v_w_rec_a, v_b_rec_a, v_w_rec_i, v_b_rec_i, v_lru_lambda, v_q_norm_g, v_w_uq, v_kv_norm_g, v_w_ukv, v_w_branch, v_w_out, v_norm_ffn_g, v_w_ffn_in, v_w_ffn_out, v_final_norm_g):
    w = dict(meta_tokens=meta_tokens, norm_mix_g=norm_mix_g, w_in=w_in, b_gate=b_gate, conv_w=conv_w, conv_b=conv_b,
             w_rec_a=w_rec_a, b_rec_a=b_rec_a, w_rec_i=w_rec_i, b_rec_i=b_rec_i, lru_lambda=lru_lambda,
             q_norm_g=q_norm_g, w_uq=w_uq, kv_norm_g=kv_norm_g, w_ukv=w_ukv, w_branch=w_branch, w_out=w_out,
             norm_ffn_g=norm_ffn_g, w_ffn_in=w_ffn_in, w_ffn_out=w_ffn_out, final_norm_g=final_norm_g)
    m = dict(meta_tokens=m_meta_tokens, norm_mix_g=m_norm_mix_g, w_in=m_w_in, b_gate=m_b_gate, conv_w=m_conv_w,
             conv_b=m_conv_b, w_rec_a=m_w_rec_a, b_rec_a=m_b_rec_a, w_rec_i=m_w_rec_i, b_rec_i=m_b_rec_i,
             lru_lambda=m_lru_lambda, q_norm_g=m_q_norm_g, w_uq=m_w_uq, kv_norm_g=m_kv_norm_g, w_ukv=m_w_ukv,
             w_branch=m_w_branch, w_out=m_w_out, norm_ffn_g=m_norm_ffn_g, w_ffn_in=m_w_ffn_in,
             w_ffn_out=m_w_ffn_out, final_norm_g=m_final_norm_g)
    v = dict(meta_tokens=v_meta_tokens, norm_mix_g=v_norm_mix_g, w_in=v_w_in, b_gate=v_b_gate, conv_w=v_conv_w,
             conv_b=v_conv_b, w_rec_a=v_w_rec_a, b_rec_a=v_b_rec_a, w_rec_i=v_w_rec_i, b_rec_i=v_b_rec_i,
             lru_lambda=v_lru_lambda, q_norm_g=v_q_norm_g, w_uq=v_w_uq, kv_norm_g=v_kv_norm_g, w_ukv=v_w_ukv,
             w_branch=v_w_branch, w_out=v_w_out, norm_ffn_g=v_norm_ffn_g, w_ffn_in=v_w_ffn_in,
             w_ffn_out=v_w_ffn_out, final_norm_g=v_final_norm_g)

    seq, d_model = x.shape[1], x.shape[2]
    length = N_META + seq
    lp = -(-length // LANES) * LANES
    t_attn = _pick(lp, 640)
    t_rnn = LANES

    small = ("meta_tokens", "b_gate", "conv_w")
    names = list(SHARDED)
    gathered = _exchange([_as2d(w[n]) if n in small else _as2d(w[n]).astype(BF16) for n in names],
                         True, "gather_weights")
    full = {n: _full_from_gathered(g, SHARDED[n]) for n, g in zip(names, gathered)}

    splits = (D_RNN, D_RNN, Q_RANK, KV_RANK, QK_ROPE, 2 * d_model)
    offs = [0]
    for s in splits:
        offs.append(offs[-1] + s)
    w_x, w_g, w_q, w_kv, w_kr, w_m = (full["w_in"][:, offs[s]:offs[s + 1]] for s in range(6))
    w_kr = jnp.pad(w_kr, ((0, 0), (0, LANES - QK_ROPE)))
    w_uq_pad = jnp.pad(full["w_uq"].reshape(Q_RANK, N_HEADS, QK_NOPE + QK_ROPE),
                       ((0, 0), (0, 0), (0, HEAD_PAD - QK_NOPE - QK_ROPE))).reshape(Q_RANK, N_HEADS * HEAD_PAD)
    w_ukv3 = full["w_ukv"].reshape(KV_RANK, N_HEADS, QK_NOPE + V_DIM)
    w_k_pad = jnp.pad(w_ukv3[:, :, :QK_NOPE], ((0, 0), (0, 0), (0, HEAD_PAD - QK_NOPE))).reshape(
        KV_RANK, N_HEADS * HEAD_PAD)
    w_v = w_ukv3[:, :, QK_NOPE:].reshape(KV_RANK, N_HEADS * V_DIM)
    wb_r, wb_a = full["w_branch"][:D_RNN], full["w_branch"][D_RNN:]
    bg = full["b_gate"].reshape(1, 2 * d_model)
    pv = jnp.concatenate([full["conv_w"], conv_b, b_rec_a, b_rec_i, lru_lambda], axis=0)
    wa_b, wi_b = w_rec_a[0].astype(BF16), w_rec_i[0].astype(BF16)
    g_final = final_norm_g.reshape(1, d_model)

    h0 = jnp.concatenate([full["meta_tokens"], x[0], jnp.zeros((lp - length, d_model), F32)], axis=0)
    tgt = jnp.pad(loss_target[0], ((N_META, lp - length), (0, 0)))
    tabs = _rope_tables(lp)

    z = _rmsnorm_fwd(h0, norm_mix_g, "norm_mix")
    ux = _mm([(z, w_x)], "nn", "in_x")
    ug = _mm([(z, w_g)], "nn", "in_g")
    uq = _mm([(z, w_q)], "nn", "in_q")
    ukv = _mm([(z, w_kv)], "nn", "in_kv")
    ukr = _mm([(z, w_kr)], "nn", "in_kr")
    um = _mm([(z, w_m)], "nn", "in_m")
    y_rnn, hs = _rglru_fwd(ux, ug, pv, wa_b, wi_b, t_rnn, "rglru_fwd")
    qn = _rmsnorm_fwd(uq, q_norm_g, "norm_q")
    kvn = _rmsnorm_fwd(ukv, kv_norm_g, "norm_kv")
    qpad = _mm([(qn, w_uq_pad)], "nn", "up_q")
    kpad = _mm([(kvn, w_k_pad)], "nn", "up_k")
    vh = _mm([(kvn, w_v)], "nn", "up_v", out_dtype=BF16)
    qh, kh = _rope_fwd(qpad, kpad, ukr, tabs, "rope_fwd")
    oh, lse = _attn_fwd(qh, kh, vh, t_attn, "attn_fwd")
    p_rnn = _mm([(y_rnn, wb_r)], "nn", "branch_rnn")
    p_att = _mm([(oh, wb_a)], "nn", "branch_att")
    mixed = _mix_fwd(um, p_rnn, p_att, bg, "mix_fwd")
    h1 = _mm([(mixed, full["w_out"])], "nn", "out_proj", res=h0)
    zf = _rmsnorm_fwd(h1, norm_ffn_g, "norm_ffn")
    gu = _mm([(zf, full["w_ffn_in"])], "nn", "ffn_in")
    act = _swiglu_fwd(gu, "swiglu_fwd")
    h2 = _mm([(act, full["w_ffn_out"])], "nn", "ffn_out", res=h1)
    dh2, dg_final, _, loss_part = _loss_bwd(h2, tgt, g_final, seq, "loss_bwd")

    d_act = _mm([(dh2, full["w_ffn_out"])], "nt", "d_act")
    dw_ffn_out = _mm_tn(act, dh2, "dw_ffn_out")
    d_gu = _swiglu_bwd(gu, d_act, "swiglu_bwd")
    dw_ffn_in = _mm_tn(zf, d_gu, "dw_ffn_in")
    d_zf = _mm([(d_gu, full["w_ffn_in"], D_FF, 0), (d_gu, full["w_ffn_in"], D_FF, 1)], "nt", "d_zf")
    dh1, dg_ffn = _rmsnorm_bwd(h1, d_zf, norm_ffn_g, "norm_ffn_bwd", res=dh2)
    d_mixed = _mm([(dh1, full["w_out"])], "nt", "d_mixed")
    dw_out = _mm_tn(mixed, dh1, "dw_out")
    d_prnn, d_patt, d_um, dbg = _mix_bwd(um, p_rnn, p_att, d_mixed, bg, "mix_bwd")
    d_yrnn = _mm([(d_prnn, wb_r)], "nt", "d_yrnn")
    d_oh = _mm([(d_patt, wb_a)], "nt", "d_oh", out_dtype=BF16)
    dwb_r = _mm_tn(y_rnn, d_prnn, "dw_branch_rnn")
    dwb_a = _mm_tn(oh, d_patt, "dw_branch_att")
    dqh, delta = _attn_bwd_dq(qh, kh, vh, d_oh, oh, lse, t_attn, "attn_bwd_dq")
    dkh, dvh = _attn_bwd_dkv(qh, kh, vh, d_oh, lse, delta, t_attn, "attn_bwd_dkv")
    dqpad, dkpad, dukr = _rope_bwd(dqh, dkh, tabs, "rope_bwd")
    d_qn = _mm([(dqpad, w_uq_pad)], "nt", "d_qn")
    dw_uq_pad = _mm_tn(qn, dqpad, "dw_uq")
    d_kvn = _mm([(dkpad, w_k_pad), (dvh, w_v)], "nt", "d_kvn")
    dw_k_pad = _mm_tn(kvn, dkpad, "dw_uk")
    dw_v = _mm_tn(kvn, dvh, "dw_uv")
    duq, dg_q = _rmsnorm_bwd(uq, d_qn, q_norm_g, "norm_q_bwd", out_dtype=BF16)
    dukv, dg_kv = _rmsnorm_bwd(ukv, d_kvn, kv_norm_g, "norm_kv_bwd", out_dtype=BF16)
    dux, dug, dpv, dwa, dwi = _rglru_bwd(ux, ug, hs, d_yrnn, pv, wa_b, wi_b, t_rnn, "rglru_bwd")
    d_z = _mm([(dux, w_x), (dug, w_g), (duq, w_q), (dukv, w_kv), (dukr, w_kr), (d_um, w_m)], "nt", "d_z")
    dw_in = jnp.concatenate([
        _mm_tn(z, dux, "dw_in_x"), _mm_tn(z, dug, "dw_in_g"), _mm_tn(z, duq, "dw_in_q"),
        _mm_tn(z, dukv, "dw_in_kv"), _mm_tn(z, dukr, "dw_in_kr")[:, :QK_ROPE], _mm_tn(z, d_um, "dw_in_m")], axis=1)
    dh0, dg_mix = _rmsnorm_bwd(h0, d_z, norm_mix_g, "norm_mix_bwd", res=dh1)

    dw_uq = dw_uq_pad.reshape(Q_RANK, N_HEADS, HEAD_PAD)[:, :, :QK_NOPE + QK_ROPE].reshape(Q_RANK, -1)
    dw_ukv = jnp.concatenate([dw_k_pad.reshape(KV_RANK, N_HEADS, HEAD_PAD)[:, :, :QK_NOPE],
                              dw_v.reshape(KV_RANK, N_HEADS, V_DIM)], axis=2).reshape(KV_RANK, -1)
    grad_full = dict(
        meta_tokens=dh0[:N_META], w_in=dw_in, b_gate=dbg.reshape(2, d_model), conv_w=dpv[:CONV_WIDTH],
        w_uq=dw_uq, w_ukv=dw_ukv, w_branch=jnp.concatenate([dwb_r, dwb_a], axis=0), w_out=dw_out,
        w_ffn_in=dw_ffn_in, w_ffn_out=dw_ffn_out)
    grad_rep = dict(
        norm_mix_g=dg_mix, conv_b=dpv[4:5], w_rec_a=dwa, b_rec_a=dpv[5:6], w_rec_i=dwi, b_rec_i=dpv[6:7],
        lru_lambda=dpv[7:8], q_norm_g=dg_q, kv_norm_g=dg_kv, norm_ffn_g=dg_ffn, final_norm_g=dg_final)

    to_send = [_blocks_from_full(grad_full[n], SHARDED[n]) for n in names]
    rep_pack = _pack([grad_rep[n] for n in REPLICATED])
    recv = _exchange(to_send, False, "scatter_grads")
    recv_rep = _exchange([rep_pack], True, "gather_small_grads")[0]

    grads, deltas, new_m, new_v = {}, {}, {}, {}
    for n, parts in zip(names, recv):
        g2, d2, m2, v2 = _adamw(_as2d(w[n]), _as2d(m[n]), _as2d(v[n]), parts, "adamw_" + n)
        for store, val in ((grads, g2), (deltas, d2), (new_m, m2), (new_v, v2)):
            store[n] = val.reshape(w[n].shape)
    rep_like = [w[n] for n in REPLICATED]
    outs = _adamw(_pack(rep_like), _pack([m[n] for n in REPLICATED]), _pack([v[n] for n in REPLICATED]),
                  recv_rep, "adamw_replicated")
    for store, val in zip((grads, deltas, new_m, new_v), outs):
        for n, a in zip(REPLICATED, _unpack(val, rep_like)):
            store[n] = a

    loss = lax.psum(loss_part[0, 0], MESH_AXES)
    grad_x = dh0[N_META:length][None]
    return (loss, grad_x, *[grads[n] for n in WEIGHTS], *[deltas[n] for n in WEIGHTS],
            *[new_m[n] for n in WEIGHTS], *[new_v[n] for n in WEIGHTS])
```

```python
import functools
import math

import jax
import jax.numpy as jnp
from jax import lax
from jax.experimental import pallas as pl
from jax.experimental.pallas import tpu as pltpu

F32 = jnp.float32
BF16 = jnp.bfloat16

N_DEV = 8
MESH_AXES = ("x", "y", "c")
LANES = 128
SUBLANES = 8
VMEM_LIMIT = 56 * 1024 * 1024

N_META = 16
CHUNK_SHIFT = 6
CHUNK_BIAS = 64 - N_META
EPS = 1e-6
D_RNN = 1280
RNN_BLOCKS = 10
CONV_WIDTH = 4
LRU_C = 8.0
N_HEADS = 8
QK_NOPE = 128
QK_ROPE = 64
V_DIM = 128
HEAD_PAD = 256
Q_RANK = 384
KV_RANK = 256
ROPE_THETA = 10000.0
ATTN_SCALE = 1.0 / math.sqrt(QK_NOPE + QK_ROPE)
NEG = -1e30
LOG2E = 1.0 / math.log(2.0)
SCALE_LOG2E = ATTN_SCALE * LOG2E
Q_SPLIT = 2
Q_ALIGN = LANES // Q_SPLIT
SPILL = LANES
D_FF = 2816

ADAM_LR = 0.001
ADAM_B1 = 0.9
ADAM_B2 = 0.999
ADAM_EPS = 1e-08
ADAM_WD = 0.01
ADAM_STEP = 10

NN = (((1,), (0,)), ((), ()))
NT = (((1,), (1,)), ((), ()))
TN = (((0,), (0,)), ((), ()))


def _pick(n, cap, base=LANES):
    best = None
    for t in range(base, min(n, cap) + 1, base):
        if n % t == 0:
            best = t
    return best if best is not None else n


def _params(sem=None):
    return pltpu.CompilerParams(dimension_semantics=sem, vmem_limit_bytes=VMEM_LIMIT)


def _sig(x):
    return 1.0 / (1.0 + jnp.exp(-x))


def _exchange(srcs, gather, name):
    n = len(srcs)
    out_shape = [jax.ShapeDtypeStruct((N_DEV,) + (s.shape if g else s.shape[1:]), s.dtype)
                 for s, g in zip(srcs, gather)]

    def body(*refs):
        src, dst = refs[:n], refs[n:2 * n]
        send_sems, recv_sems, local_sems = refs[2 * n:]
        x, y, c = lax.axis_index("x"), lax.axis_index("y"), lax.axis_index("c")
        me = 4 * x + 2 * y + c
        local = []
        for t in range(n):
            cp = pltpu.make_async_copy(src[t] if gather[t] else src[t].at[me], dst[t].at[me], local_sems.at[t])
            cp.start()
            local.append(cp)
        sends, recvs = [], []
        for k in range(1, N_DEV):
            px = 1 - x if k & 4 else x
            py = 1 - y if k & 2 else y
            pc = 1 - c if k & 1 else c
            peer = 4 * px + 2 * py + pc
            for t in range(n):
                cp = pltpu.make_async_remote_copy(
                    src_ref=src[t] if gather[t] else src[t].at[peer], dst_ref=dst[t].at[me],
                    send_sem=send_sems.at[t, k - 1], recv_sem=recv_sems.at[t, k - 1],
                    device_id=(px, py, pc), device_id_type=pl.DeviceIdType.MESH)
                cp.start()
                sends.append(cp)
                recvs.append(pltpu.make_async_remote_copy(
                    src_ref=src[t] if gather[t] else src[t].at[peer], dst_ref=dst[t].at[peer],
                    send_sem=send_sems.at[t, k - 1], recv_sem=recv_sems.at[t, k - 1],
                    device_id=(px, py, pc), device_id_type=pl.DeviceIdType.MESH))
        for cp in recvs:
            cp.wait_recv()
        for cp in sends:
            cp.wait_send()
        for cp in local:
            cp.wait()

    any_spec = pl.BlockSpec(memory_space=pl.ANY)
    return pl.pallas_call(
        body, name=name, out_shape=out_shape,
        in_specs=[any_spec] * n, out_specs=[any_spec] * n,
        scratch_shapes=[pltpu.SemaphoreType.DMA((n, N_DEV - 1)), pltpu.SemaphoreType.DMA((n, N_DEV - 1)),
                        pltpu.SemaphoreType.DMA((n,))],
    )(*srcs)


def _mm(pairs, mode, name, res=None, out_dtype=F32):
    pairs = [p if len(p) == 4 else (p[0], p[1], p[0].shape[1], 0) for p in pairs]
    m = pairs[0][0].shape[0]
    n = pairs[0][1].shape[1] if mode == "nn" else pairs[0][1].shape[0]
    tm, tn = _pick(m, 640), _pick(n, 640)
    np_ = len(pairs)
    dims = NN if mode == "nn" else NT

    def body(*refs):
        acc = None
        for s in range(np_):
            d = lax.dot_general(refs[2 * s][...].astype(BF16), refs[2 * s + 1][...].astype(BF16), dims,
                                preferred_element_type=F32)
            acc = d if acc is None else acc + d
        if res is not None:
            acc = acc + refs[2 * np_][...]
        refs[-1][...] = acc.astype(out_dtype)

    in_specs, args = [], []
    for a, b, kt, kb in pairs:
        in_specs.append(pl.BlockSpec((tm, kt), lambda i, j, kb=kb: (i, kb)))
        if mode == "nn":
            in_specs.append(pl.BlockSpec((kt, tn), lambda i, j, kb=kb: (kb, j)))
        else:
            in_specs.append(pl.BlockSpec((tn, kt), lambda i, j, kb=kb: (j, kb)))
        args += [a, b]
    if res is not None:
        in_specs.append(pl.BlockSpec((tm, tn), lambda i, j: (i, j)))
        args.append(res)
    return pl.pallas_call(
        body, name=name, grid=(m // tm, n // tn), in_specs=in_specs,
        out_specs=pl.BlockSpec((tm, tn), lambda i, j: (i, j)),
        out_shape=jax.ShapeDtypeStruct((m, n), out_dtype),
        compiler_params=_params(("parallel", "parallel")),
    )(*args)


def _mm_tn(a, b, name):
    m, k = a.shape
    n = b.shape[1]
    tm, tk, tn = _pick(m, 640), _pick(k, 1408), _pick(n, 640)

    def body(a_ref, b_ref, o_ref):
        @pl.when(pl.program_id(2) == 0)
        def _():
            o_ref[...] = jnp.zeros_like(o_ref)

        o_ref[...] += lax.dot_general(a_ref[...].astype(BF16), b_ref[...].astype(BF16), TN,
                                      preferred_element_type=F32)

    return pl.pallas_call(
        body, name=name, grid=(k // tk, n // tn, m // tm),
        in_specs=[pl.BlockSpec((tm, tk), lambda i, j, r: (r, i)), pl.BlockSpec((tm, tn), lambda i, j, r: (r, j))],
        out_specs=pl.BlockSpec((tk, tn), lambda i, j, r: (i, j)),
        out_shape=jax.ShapeDtypeStruct((k, n), F32),
        compiler_params=_params(("parallel", "parallel", "arbitrary")),
    )(a, b)


ROW_TILE_BYTES = 6 * 1024 * 1024


def _row_tile(rows, row_in, row_out):
    per_row = sum((r[1] * r[0].dtype.itemsize) if isinstance(r, tuple) else (r.shape[1] * r.dtype.itemsize)
                  for r in row_in)
    per_row += sum(w * jnp.dtype(dt).itemsize for w, dt in row_out)
    return _pick(rows, min(640, max(LANES, ROW_TILE_BYTES // per_row)))


def _rowcall(body, name, rows, row_in, full_in, row_out, acc_out=()):
    tr = _row_tile(rows, row_in, row_out)
    n_steps = rows // tr
    in_specs, args = [], []
    for r in row_in:
        arr, w, cb = r if isinstance(r, tuple) else (r, r.shape[1], 0)
        in_specs.append(pl.BlockSpec((tr, w), lambda i, cb=cb: (i, cb)))
        args.append(arr)
    for f in full_in:
        in_specs.append(pl.BlockSpec(f.shape, lambda i, nd=f.ndim: (0,) * nd))
        args.append(f)
    out_specs = [pl.BlockSpec((tr, w), lambda i: (i, 0)) for w, _ in row_out]
    out_shape = [jax.ShapeDtypeStruct((rows, w), dt) for w, dt in row_out]
    for shp, dt in acc_out:
        out_specs.append(pl.BlockSpec(shp, lambda i, nd=len(shp): (0,) * nd))
        out_shape.append(jax.ShapeDtypeStruct(shp, dt))

    def wrapped(*refs):
        body(pl.program_id(0), n_steps, *refs)

    return pl.pallas_call(
        wrapped, name=name, grid=(n_steps,), in_specs=in_specs, out_specs=out_specs, out_shape=out_shape,
        compiler_params=_params(("arbitrary",) if acc_out else ("parallel",)),
    )(*args)


def _rmsnorm_fwd(x, g, name):
    rows, w = x.shape

    def body(i, n, x_ref, g_ref, o_ref):
        xv = x_ref[...]
        r = lax.rsqrt(jnp.mean(xv * xv, axis=-1, keepdims=True) + EPS)
        o_ref[...] = (xv * r * g_ref[...]).astype(BF16)

    return _rowcall(body, name, rows,[x], [g], [(w, BF16)])[0]


def _rmsnorm_bwd_math(xv, dy, g):
    w = xv.shape[-1]
    r = lax.rsqrt(jnp.mean(xv * xv, axis=-1, keepdims=True) + EPS)
    t = dy * g
    dx = r * t - xv * (r * r * r * (jnp.sum(t * xv, axis=-1, keepdims=True) / w))
    dg = jnp.sum(dy * xv * r, axis=0, keepdims=True)
    return dx, dg


def _rmsnorm_bwd(x, dy, g, name, res=None, out_dtype=F32):
    rows, w = x.shape

    def body(i, n, *refs):
        x_ref, dy_ref = refs[0], refs[1]
        g_ref, dx_ref, dg_ref = refs[-3], refs[-2], refs[-1]
        dx, dg = _rmsnorm_bwd_math(x_ref[...], dy_ref[...], g_ref[...])
        if res is not None:
            dx = dx + refs[2][...]
        dx_ref[...] = dx.astype(out_dtype)

        @pl.when(i == 0)
        def _():
            dg_ref[...] = jnp.zeros_like(dg_ref)

        dg_ref[...] += dg

    row_in = [x, dy] + ([res] if res is not None else [])
    return _rowcall(body, name, rows,row_in, [g], [(w, out_dtype)], [((1, w), F32)])


def _loss_bwd(h2, tgt, g, seq, name):
    rows, w = h2.shape
    tr = _row_tile(rows, [h2, tgt], [(w, F32)])

    def body(i, n, h_ref, t_ref, g_ref, dh_ref, dg_ref, lcol_ref, loss_ref):
        hv, gv = h_ref[...], g_ref[...]
        row = i * tr + lax.broadcasted_iota(jnp.int32, (tr, w), 0)
        valid = jnp.logical_and(row >= N_META, row < N_META + seq)
        r = lax.rsqrt(jnp.mean(hv * hv, axis=-1, keepdims=True) + EPS)
        err = jnp.where(valid, hv * r * gv - t_ref[...], 0.0)
        dx, dg = _rmsnorm_bwd_math(hv, err * (1.0 / w), gv)
        dh_ref[...] = dx

        @pl.when(i == 0)
        def _():
            dg_ref[...] = jnp.zeros_like(dg_ref)
            lcol_ref[...] = jnp.zeros_like(lcol_ref)

        dg_ref[...] += dg
        lcol_ref[...] += jnp.sum(err * err, axis=0, keepdims=True)

        @pl.when(i == n - 1)
        def _():
            total = jnp.sum(lcol_ref[...], axis=1, keepdims=True) * (0.5 / w)
            loss_ref[...] = jnp.broadcast_to(total, loss_ref.shape)

    return _rowcall(body, name, rows, [h2, tgt], [g], [(w, F32)],
                    [((1, w), F32), ((1, w), F32), ((1, LANES), F32)])


def _mix_fwd(um, p_rnn, p_att, bg, name):
    rows, d = p_rnn.shape

    def body(i, n, u0_ref, u1_ref, pr_ref, pa_ref, bg_ref, o_ref):
        g0 = _sig(u0_ref[...] + bg_ref[:, :d])
        g1 = _sig(u1_ref[...] + bg_ref[:, d:])
        o_ref[...] = (g0 * pr_ref[...] + g1 * pa_ref[...]).astype(BF16)

    return _rowcall(body, name, rows,[(um, d, 0), (um, d, 1), p_rnn, p_att], [bg],
                    [(d, BF16)])[0]


def _mix_bwd(um, p_rnn, p_att, dmix, bg, name):
    rows, d = p_rnn.shape

    def body(i, n, u0_ref, u1_ref, pr_ref, pa_ref, dm_ref, bg_ref, dpr_ref, dpa_ref, dum_ref, dbg_ref):
        g0 = _sig(u0_ref[...] + bg_ref[:, :d])
        g1 = _sig(u1_ref[...] + bg_ref[:, d:])
        dm = dm_ref[...]
        dpr_ref[...] = (dm * g0).astype(BF16)
        dpa_ref[...] = (dm * g1).astype(BF16)
        du0 = dm * pr_ref[...] * g0 * (1.0 - g0)
        du1 = dm * pa_ref[...] * g1 * (1.0 - g1)
        dum_ref[:, :d] = du0.astype(BF16)
        dum_ref[:, d:] = du1.astype(BF16)

        @pl.when(i == 0)
        def _():
            dbg_ref[...] = jnp.zeros_like(dbg_ref)

        dbg_ref[:, :d] += jnp.sum(du0, axis=0, keepdims=True)
        dbg_ref[:, d:] += jnp.sum(du1, axis=0, keepdims=True)

    return _rowcall(body, name, rows,[(um, d, 0), (um, d, 1), p_rnn, p_att, dmix], [bg],
                    [(d, BF16), (d, BF16), (2 * d, BF16)], [((1, 2 * d), F32)])


def _swiglu_fwd(gu, name):
    rows, w2 = gu.shape
    f = w2 // 2

    def body(i, n, g_ref, u_ref, o_ref):
        gate = g_ref[...]
        o_ref[...] = (gate * _sig(gate) * u_ref[...]).astype(BF16)

    return _rowcall(body, name, rows,[(gu, f, 0), (gu, f, 1)], [], [(f, BF16)])[0]


def _swiglu_bwd(gu, dact, name):
    rows, w2 = gu.shape
    f = w2 // 2

    def body(i, n, g_ref, u_ref, da_ref, o_ref):
        gate, da = g_ref[...], da_ref[...]
        sg = _sig(gate)
        o_ref[:, :f] = (da * u_ref[...] * (sg * (1.0 + gate * (1.0 - sg)))).astype(BF16)
        o_ref[:, f:] = (da * gate * sg).astype(BF16)

    return _rowcall(body, name, rows,[(gu, f, 0), (gu, f, 1), dact], [], [(w2, BF16)])[0]


def _rope_tables(lp):
    idx = jnp.arange(lp, dtype=jnp.int32).astype(F32)
    inv_freq = ROPE_THETA ** (-jnp.arange(0, QK_ROPE, 2, dtype=F32) / QK_ROPE)
    ang = idx[:, None] * inv_freq[None, :]
    cos, sin = jnp.cos(ang), jnp.sin(ang)
    half = QK_ROPE // 2
    z = lambda wdt: jnp.zeros((lp, wdt), F32)
    tc = jnp.concatenate([cos, cos, z(LANES - 2 * half)], axis=1)
    ts1 = jnp.concatenate([-sin, z(LANES - half)], axis=1)
    ts2 = jnp.concatenate([z(half), sin, z(LANES - 2 * half)], axis=1)
    return tc, ts1, ts2


def _rope(xv, tc, ts1, ts2):
    half = QK_ROPE // 2
    return xv * tc + pltpu.roll(xv, LANES - half, 1) * ts1 + pltpu.roll(xv, half, 1) * ts2


def _rope_t(dv, tc, ts1, ts2):
    half = QK_ROPE // 2
    return dv * tc + pltpu.roll(dv * ts1, half, 1) + pltpu.roll(dv * ts2, LANES - half, 1)


def _rope_fwd(qpad, kpad, ukr, tabs, name):
    rows, w = qpad.shape

    def body(i, n, q_ref, k_ref, r_ref, c_ref, s1_ref, s2_ref, qo_ref, ko_ref):
        tc, ts1, ts2 = c_ref[...], s1_ref[...], s2_ref[...]
        kr = _rope(r_ref[...], tc, ts1, ts2).astype(BF16)
        for h in range(N_HEADS):
            lo, mid, hi = h * HEAD_PAD, h * HEAD_PAD + QK_NOPE, (h + 1) * HEAD_PAD
            qo_ref[:, lo:mid] = q_ref[:, lo:mid].astype(BF16)
            qo_ref[:, mid:hi] = _rope(q_ref[:, mid:hi], tc, ts1, ts2).astype(BF16)
            ko_ref[:, lo:mid] = k_ref[:, lo:mid].astype(BF16)
            ko_ref[:, mid:hi] = kr

    return _rowcall(body, name, rows,[qpad, kpad, ukr, *tabs], [], [(w, BF16), (w, BF16)])


def _rope_bwd(dq, dk, tabs, name):
    rows, w = dq.shape

    def body(i, n, dq_ref, dk_ref, c_ref, s1_ref, s2_ref, qo_ref, ko_ref, ro_ref):
        tc, ts1, ts2 = c_ref[...], s1_ref[...], s2_ref[...]
        dkr = None
        for h in range(N_HEADS):
            lo, mid, hi = h * HEAD_PAD, h * HEAD_PAD + QK_NOPE, (h + 1) * HEAD_PAD
            qo_ref[:, lo:mid] = dq_ref[:, lo:mid].astype(BF16)
            qo_ref[:, mid:hi] = _rope_t(dq_ref[:, mid:hi], tc, ts1, ts2).astype(BF16)
            ko_ref[:, lo:mid] = dk_ref[:, lo:mid].astype(BF16)
            ko_ref[:, mid:hi] = jnp.zeros((ko_ref.shape[0], hi - mid), BF16)
            part = dk_ref[:, mid:hi]
            dkr = part if dkr is None else dkr + part
        ro_ref[...] = _rope_t(dkr, tc, ts1, ts2).astype(BF16)

    return _rowcall(body, name, rows,[dq, dk, *tabs], [],
                    [(w, BF16), (w, BF16), (LANES, BF16)])


def _visible(q0, k0, tq, tk):
    qrow = q0 + lax.broadcasted_iota(jnp.int32, (tq, tk), 0)
    kcol = k0 + lax.broadcasted_iota(jnp.int32, (tq, tk), 1)
    return ((kcol + CHUNK_BIAS) >> CHUNK_SHIFT) <= ((qrow + CHUNK_BIAS) >> CHUNK_SHIFT)


def _lanes(v, width):
    return jnp.tile(v, (1, width // LANES))


def _attn_fwd(q, k, v, t, name):
    lp = q.shape[0]
    nt = lp // t

    th = t // Q_SPLIT

    def body(q_ref, k_ref, v_ref, o_ref, lse_ref, m_s, l_s, acc_s):
        i = pl.program_id(1)
        m_s[...] = jnp.full(m_s.shape, NEG, F32)
        l_s[...] = jnp.zeros(l_s.shape, F32)
        acc_s[...] = jnp.zeros(acc_s.shape, F32)

        def update(rows, kv_, vv, mask):
            s = lax.dot_general(q_ref[rows, :], kv_, NT, preferred_element_type=F32)
            if mask is not None:
                s = jnp.where(mask, s, NEG)
            m_prev = m_s[rows, :]
            m_new = jnp.maximum(m_prev, jnp.max(s, axis=1, keepdims=True))
            alpha = jnp.exp2((m_prev - m_new) * SCALE_LOG2E)
            p = jnp.exp2((s - _lanes(m_new, s.shape[1])) * SCALE_LOG2E)
            l_s[rows, :] = alpha * l_s[rows, :] + jnp.sum(p, axis=1, keepdims=True)
            acc_s[rows, :] = alpha * acc_s[rows, :] + jnp.dot(p.astype(BF16), vv, preferred_element_type=F32)
            m_s[rows, :] = m_new

        def full_step(j, c):
            r0 = pl.multiple_of(j * t, t)
            kv_, vv = k_ref[pl.ds(r0, t), :], v_ref[pl.ds(r0, t), :]
            for hh in range(Q_SPLIT):
                update(slice(hh * th, (hh + 1) * th), kv_, vv, None)
            return c

        lax.fori_loop(0, i, full_step, 0)
        r0 = pl.multiple_of(i * t, t)
        kv_, vv = k_ref[pl.ds(r0, t), :], v_ref[pl.ds(r0, t), :]
        for hh in range(Q_SPLIT):
            update(slice(hh * th, (hh + 1) * th), kv_, vv, _visible(i * t + hh * th, i * t, th, t))

        @pl.when(i + 1 < nt)
        def _():
            r1 = pl.multiple_of((i + 1) * t, t)
            update(slice(t - SPILL, t), k_ref[pl.ds(r1, SPILL), :], v_ref[pl.ds(r1, SPILL), :],
                   _visible(i * t + t - SPILL, (i + 1) * t, SPILL, SPILL))

        o_ref[...] = (acc_s[...] / l_s[...]).astype(BF16)
        lse_ref[...] = m_s[...] * ATTN_SCALE + jnp.log(l_s[...])

    return pl.pallas_call(
        body, name=name, grid=(N_HEADS, nt),
        in_specs=[pl.BlockSpec((t, HEAD_PAD), lambda h, i: (i, h)),
                  pl.BlockSpec((lp, HEAD_PAD), lambda h, i: (0, h)),
                  pl.BlockSpec((lp, V_DIM), lambda h, i: (0, h))],
        out_specs=[pl.BlockSpec((t, V_DIM), lambda h, i: (i, h)),
                   pl.BlockSpec((None, t, LANES), lambda h, i: (h, i, 0))],
        out_shape=[jax.ShapeDtypeStruct((lp, N_HEADS * V_DIM), BF16),
                   jax.ShapeDtypeStruct((N_HEADS, lp, LANES), F32)],
        scratch_shapes=[pltpu.VMEM((t, LANES), F32), pltpu.VMEM((t, LANES), F32), pltpu.VMEM((t, V_DIM), F32)],
        compiler_params=_params(("parallel", "arbitrary")),
    )(q, k, v)


def _attn_bwd_dq(q, k, v, do, o, lse, t, name):
    lp = q.shape[0]
    nt = lp // t

    th = t // Q_SPLIT

    def body(q_ref, k_ref, v_ref, do_ref, o_ref, lse_ref, dq_ref, dl_ref, acc_s, l2_s):
        i = pl.program_id(1)
        delta = jnp.sum(do_ref[...].astype(F32) * o_ref[...].astype(F32), axis=1, keepdims=True)
        dl_ref[...] = jnp.broadcast_to(delta, dl_ref.shape)
        l2_s[...] = lse_ref[...] * LOG2E
        acc_s[...] = jnp.zeros(acc_s.shape, F32)

        def update(rows, kv_, vv, mask):
            s = lax.dot_general(q_ref[rows, :], kv_, NT, preferred_element_type=F32)
            if mask is not None:
                s = jnp.where(mask, s, NEG)
            width = s.shape[1]
            p = jnp.exp2(s * SCALE_LOG2E - _lanes(l2_s[rows, :], width))
            dp = lax.dot_general(do_ref[rows, :], vv, NT, preferred_element_type=F32)
            ds = (p * (dp - _lanes(dl_ref[rows, :], width)) * ATTN_SCALE).astype(BF16)
            acc_s[rows, :] += jnp.dot(ds, kv_, preferred_element_type=F32)

        def full_step(j, c):
            r0 = pl.multiple_of(j * t, t)
            kv_, vv = k_ref[pl.ds(r0, t), :], v_ref[pl.ds(r0, t), :]
            for hh in range(Q_SPLIT):
                update(slice(hh * th, (hh + 1) * th), kv_, vv, None)
            return c

        lax.fori_loop(0, i, full_step, 0)
        r0 = pl.multiple_of(i * t, t)
        kv_, vv = k_ref[pl.ds(r0, t), :], v_ref[pl.ds(r0, t), :]
        for hh in range(Q_SPLIT):
            update(slice(hh * th, (hh + 1) * th), kv_, vv, _visible(i * t + hh * th, i * t, th, t))

        @pl.when(i + 1 < nt)
        def _():
            r1 = pl.multiple_of((i + 1) * t, t)
            update(slice(t - SPILL, t), k_ref[pl.ds(r1, SPILL), :], v_ref[pl.ds(r1, SPILL), :],
                   _visible(i * t + t - SPILL, (i + 1) * t, SPILL, SPILL))

        dq_ref[...] = acc_s[...]

    return pl.pallas_call(
        body, name=name, grid=(N_HEADS, nt),
        in_specs=[pl.BlockSpec((t, HEAD_PAD), lambda h, i: (i, h)),
                  pl.BlockSpec((lp, HEAD_PAD), lambda h, i: (0, h)),
                  pl.BlockSpec((lp, V_DIM), lambda h, i: (0, h)),
                  pl.BlockSpec((t, V_DIM), lambda h, i: (i, h)),
                  pl.BlockSpec((t, V_DIM), lambda h, i: (i, h)),
                  pl.BlockSpec((None, t, LANES), lambda h, i: (h, i, 0))],
        out_specs=[pl.BlockSpec((t, HEAD_PAD), lambda h, i: (i, h)),
                   pl.BlockSpec((None, t, LANES), lambda h, i: (h, i, 0))],
        out_shape=[jax.ShapeDtypeStruct((lp, N_HEADS * HEAD_PAD), F32),
                   jax.ShapeDtypeStruct((N_HEADS, lp, LANES), F32)],
        scratch_shapes=[pltpu.VMEM((t, HEAD_PAD), F32), pltpu.VMEM((t, LANES), F32)],
        compiler_params=_params(("parallel", "arbitrary")),
    )(q, k, v, do, o, lse)


def _attn_bwd_dkv(q, k, v, do, lse, delta, t, name):
    lp = q.shape[0]
    nt = lp // t

    th = t // Q_SPLIT

    def body(q_ref, k_ref, v_ref, do_ref, lse_ref, dl_ref, dk_ref, dv_ref, dk_s, dv_s):
        j = pl.program_id(1)
        dk_s[...] = jnp.zeros(dk_s.shape, F32)
        dv_s[...] = jnp.zeros(dv_s.shape, F32)

        def update(q0, nq, krows, mask):
            qrows = pl.ds(q0, nq)
            kv_, vv = k_ref[krows, :], v_ref[krows, :]
            qv, dov = q_ref[qrows, :], do_ref[qrows, :]
            s = lax.dot_general(qv, kv_, NT, preferred_element_type=F32)
            if mask is not None:
                s = jnp.where(mask, s, NEG)
            width = s.shape[1]
            p = jnp.exp2(s * SCALE_LOG2E - _lanes(lse_ref[qrows, :], width) * LOG2E)
            dv_s[krows, :] += lax.dot_general(p.astype(BF16), dov, TN, preferred_element_type=F32)
            dp = lax.dot_general(dov, vv, NT, preferred_element_type=F32)
            ds = (p * (dp - _lanes(dl_ref[qrows, :], width)) * ATTN_SCALE).astype(BF16)
            dk_s[krows, :] += lax.dot_general(ds, qv, TN, preferred_element_type=F32)

        @pl.when(j > 0)
        def _():
            q0 = pl.multiple_of(j * t - SPILL, SPILL)
            update(q0, SPILL, slice(0, SPILL), _visible(j * t - SPILL, j * t, SPILL, SPILL))

        for hh in range(Q_SPLIT):
            q0 = pl.multiple_of(j * t + hh * th, Q_ALIGN)
            update(q0, th, slice(0, t), _visible(j * t + hh * th, j * t, th, t))

        def full_step(i, c):
            for hh in range(Q_SPLIT):
                update(pl.multiple_of(i * t + hh * th, Q_ALIGN), th, slice(0, t), None)
            return c

        lax.fori_loop(j + 1, nt, full_step, 0)
        dk_ref[...] = dk_s[...]
        dv_ref[...] = dv_s[...].astype(BF16)

    return pl.pallas_call(
        body, name=name, grid=(N_HEADS, nt),
        in_specs=[pl.BlockSpec((lp, HEAD_PAD), lambda h, j: (0, h)),
                  pl.BlockSpec((t, HEAD_PAD), lambda h, j: (j, h)),
                  pl.BlockSpec((t, V_DIM), lambda h, j: (j, h)),
                  pl.BlockSpec((lp, V_DIM), lambda h, j: (0, h)),
                  pl.BlockSpec((None, lp, LANES), lambda h, j: (h, 0, 0)),
                  pl.BlockSpec((None, lp, LANES), lambda h, j: (h, 0, 0))],
        out_specs=[pl.BlockSpec((t, HEAD_PAD), lambda h, j: (j, h)),
                   pl.BlockSpec((t, V_DIM), lambda h, j: (j, h))],
        out_shape=[jax.ShapeDtypeStruct((lp, N_HEADS * HEAD_PAD), F32),
                   jax.ShapeDtypeStruct((lp, N_HEADS * V_DIM), BF16)],
        scratch_shapes=[pltpu.VMEM((t, HEAD_PAD), F32), pltpu.VMEM((t, V_DIM), F32)],
        compiler_params=_params(("parallel", "arbitrary")),
    )(q, k, v, do, lse, delta)


def _shift_down(cur, prev8, k):
    r = pltpu.roll(cur, k, 0)
    row8 = lax.broadcasted_iota(jnp.int32, prev8.shape, 0)
    first = jnp.where(row8 < k, pltpu.roll(prev8, k, 0), r[0:SUBLANES])
    return jnp.concatenate([first, r[SUBLANES:]], axis=0)


def _shift_up(cur, next8, k):
    t = cur.shape[0]
    r = pltpu.roll(cur, t - k, 0)
    row8 = lax.broadcasted_iota(jnp.int32, next8.shape, 0)
    last = jnp.where(row8 >= SUBLANES - k, pltpu.roll(next8, SUBLANES - k, 0), r[t - SUBLANES:])
    return jnp.concatenate([r[:t - SUBLANES], last], axis=0)


def _scan_down(a, b):
    t = a.shape[0]
    row = lax.broadcasted_iota(jnp.int32, a.shape, 0)
    s = 1
    while s < t:
        keep = row >= s
        a_sh = jnp.where(keep, pltpu.roll(a, s, 0), 1.0)
        b_sh = jnp.where(keep, pltpu.roll(b, s, 0), 0.0)
        b = a * b_sh + b
        a = a * a_sh
        s *= 2
    return a, b


def _scan_up(a, b):
    t = a.shape[0]
    row = lax.broadcasted_iota(jnp.int32, a.shape, 0)
    s = 1
    while s < t:
        keep = row < t - s
        a_sh = jnp.where(keep, pltpu.roll(a, t - s, 0), 1.0)
        b_sh = jnp.where(keep, pltpu.roll(b, t - s, 0), 0.0)
        b = a * b_sh + b
        a = a * a_sh
        s *= 2
    return a, b


def _neg_expm1(y):
    series = -y * (1.0 + y * (0.5 + y * (1.0 / 6.0 + y * (1.0 / 24.0 + y * (1.0 / 120.0)))))
    return jnp.where(y > -0.1, series, 1.0 - jnp.exp(y))


def _log_sigmoid(x):
    return jnp.minimum(x, 0.0) - jnp.log(1.0 + jnp.exp(-jnp.abs(x)))


GELU_C = math.sqrt(2.0 / math.pi)
GELU_K = 0.044715


def _gelu(x):
    th = jnp.tanh(GELU_C * (x + GELU_K * x * x * x))
    return 0.5 * x * (1.0 + th), th


def _block_mm(xb, w_ref, dims):
    rb = D_RNN // RNN_BLOCKS
    return jnp.concatenate(
        [lax.dot_general(xb[:, h * rb:(h + 1) * rb], w_ref[h], dims, preferred_element_type=F32)
         for h in range(RNN_BLOCKS)], axis=1)


def _rglru_gates(ux, prev8, pv_ref, wa_ref, wi_ref):
    shifted = [ux] + [_shift_down(ux, prev8, k) for k in range(1, CONV_WIDTH)]
    xc = pv_ref[4:5, :] + pv_ref[3:4, :] * ux
    for k in range(1, CONV_WIDTH):
        xc = xc + pv_ref[3 - k:4 - k, :] * shifted[k]
    xcb = xc.astype(BF16)
    r_g = _sig(_block_mm(xcb, wa_ref, NN) + pv_ref[5:6, :])
    i_g = _sig(_block_mm(xcb, wi_ref, NN) + pv_ref[6:7, :])
    log_a = LRU_C * r_g * _log_sigmoid(pv_ref[7:8, :])
    a = jnp.exp(log_a)
    mm = jnp.sqrt(_neg_expm1(2.0 * log_a))
    return dict(shifted=shifted, xc=xc, xcb=xcb, r=r_g, i=i_g, a=a, mm=mm)


def _rglru_fwd(ux, ug, pv, wa, wi, t, name):
    lp, d = ux.shape

    def body(ux_ref, ug_ref, pv_ref, wa_ref, wi_ref, y_ref, h_ref, tail_s, hc_s):
        @pl.when(pl.program_id(0) == 0)
        def _():
            tail_s[...] = jnp.zeros_like(tail_s)
            hc_s[...] = jnp.zeros_like(hc_s)

        uxv = ux_ref[...]
        gt = _rglru_gates(uxv, tail_s[...], pv_ref, wa_ref, wi_ref)
        tail_s[...] = ux_ref[t - SUBLANES:t, :]
        cum_a, hloc = _scan_down(gt["a"], gt["mm"] * (gt["i"] * gt["xc"]))
        h_ref[...] = hloc + cum_a * hc_s[0:1, :]
        hc_s[...] = h_ref[t - SUBLANES:t, :]
        hc_s[0:1, :] = h_ref[t - 1:t, :]
        y_ref[...] = (h_ref[...] * _gelu(ug_ref[...])[0]).astype(BF16)

    tile = pl.BlockSpec((t, d), lambda i: (i, 0))
    return pl.pallas_call(
        body, name=name, grid=(lp // t,),
        in_specs=[tile, tile, pl.BlockSpec(pv.shape, lambda i: (0, 0)),
                  pl.BlockSpec(wa.shape, lambda i: (0, 0, 0)), pl.BlockSpec(wi.shape, lambda i: (0, 0, 0))],
        out_specs=[tile, tile],
        out_shape=[jax.ShapeDtypeStruct((lp, d), BF16), jax.ShapeDtypeStruct((lp, d), F32)],
        scratch_shapes=[pltpu.VMEM((SUBLANES, d), F32), pltpu.VMEM((SUBLANES, d), F32)],
        compiler_params=_params(("arbitrary",)),
    )(ux, ug, pv, wa, wi)


def _rglru_bwd(ux, ug, hs, dy, pv, wa, wi, t, name):
    lp, d = ux.shape
    nt = lp // t
    per = t // SUBLANES
    rb = d // RNN_BLOCKS

    def body(ux_ref, uxp_ref, ug_ref, h_ref, hp_ref, dy_ref, pv_ref, wa_ref, wi_ref,
             dux_ref, dug_ref, dpv_ref, dwa_ref, dwi_ref, ca_s, cg_s, cx_s):
        step = pl.program_id(0)
        first_tile = step == nt - 1

        @pl.when(step == 0)
        def _():
            for ref in (ca_s, cg_s, cx_s, dpv_ref, dwa_ref, dwi_ref):
                ref[...] = jnp.zeros_like(ref)

        uxv = ux_ref[...]
        prev8 = jnp.where(first_tile, 0.0, uxp_ref[...])
        hprev8 = jnp.where(first_tile, 0.0, hp_ref[...])
        gt = _rglru_gates(uxv, prev8, pv_ref, wa_ref, wi_ref)
        a, mm, r_g, i_g, xc = gt["a"], gt["mm"], gt["r"], gt["i"], gt["xc"]
        hv = h_ref[...]
        hprev = _shift_down(hv, hprev8, 1)
        ugv, dyv = ug_ref[...], dy_ref[...]
        gel, th = _gelu(ugv)
        dgel = 0.5 * (1.0 + th) + 0.5 * ugv * (1.0 - th * th) * (GELU_C * (1.0 + 3.0 * GELU_K * ugv * ugv))
        dug_ref[...] = (dyv * hv * dgel).astype(BF16)
        a_up = _shift_up(a, ca_s[...], 1)
        cum_a, gloc = _scan_up(a_up, dyv * gel)
        gv = gloc + cum_a * cg_s[0:1, :]
        ca_s[...] = a[0:SUBLANES]
        cg_s[...] = gv[0:SUBLANES]
        ixc = i_g * xc
        d_ixc = gv * mm
        d_log_a = gv * hprev * a - (gv * ixc) * (a * a) / mm
        logsig = _log_sigmoid(pv_ref[7:8, :])
        d_pre_a = d_log_a * (LRU_C * logsig) * r_g * (1.0 - r_g)
        d_pre_i = d_ixc * xc * i_g * (1.0 - i_g)
        dab, dib = d_pre_a.astype(BF16), d_pre_i.astype(BF16)
        d_xc = d_ixc * i_g + _block_mm(dab, wa_ref, NT) + _block_mm(dib, wi_ref, NT)
        xcb = gt["xcb"]
        for h in range(RNN_BLOCKS):
            cols = slice(h * rb, (h + 1) * rb)
            dwa_ref[h] += lax.dot_general(xcb[:, cols], dab[:, cols], TN, preferred_element_type=F32)
            dwi_ref[h] += lax.dot_general(xcb[:, cols], dib[:, cols], TN, preferred_element_type=F32)
        csum = lambda v: jnp.sum(v, axis=0, keepdims=True)
        for k in range(CONV_WIDTH):
            dpv_ref[3 - k:4 - k, :] += csum(d_xc * gt["shifted"][k])
        dpv_ref[4:5, :] += csum(d_xc)
        dpv_ref[5:6, :] += csum(d_pre_a)
        dpv_ref[6:7, :] += csum(d_pre_i)
        dpv_ref[7:8, :] += csum(d_log_a * (LRU_C * r_g)) * _sig(-pv_ref[7:8, :])
        dux = pv_ref[3:4, :] * d_xc
        for k in range(1, CONV_WIDTH):
            dux = dux + pv_ref[3 - k:4 - k, :] * _shift_up(d_xc, cx_s[...], k)
        cx_s[...] = d_xc[0:SUBLANES]
        dux_ref[...] = dux.astype(BF16)

    rev = lambda i: (nt - 1 - i, 0)
    before = lambda i: (jnp.maximum((nt - 1 - i) * per - 1, 0), 0)
    tile = pl.BlockSpec((t, d), rev)
    tail = pl.BlockSpec((SUBLANES, d), before)
    fixed2 = lambda arr: pl.BlockSpec(arr.shape, lambda i: (0, 0))
    fixed3 = lambda arr: pl.BlockSpec(arr.shape, lambda i: (0, 0, 0))
    return pl.pallas_call(
        body, name=name, grid=(nt,),
        in_specs=[tile, tail, tile, tile, tail, tile, fixed2(pv), fixed3(wa), fixed3(wi)],
        out_specs=[tile, tile, fixed2(pv), fixed3(wa), fixed3(wi)],
        out_shape=[jax.ShapeDtypeStruct((lp, d), BF16), jax.ShapeDtypeStruct((lp, d), BF16),
                   jax.ShapeDtypeStruct(pv.shape, F32), jax.ShapeDtypeStruct(wa.shape, F32),
                   jax.ShapeDtypeStruct(wi.shape, F32)],
        scratch_shapes=[pltpu.VMEM((SUBLANES, d), F32)] * 3,
        compiler_params=_params(("arbitrary",)),
    )(ux, ux, ug, hs, hs, dy, pv, wa, wi)


def _adamw(w, m, v, parts, name):
    rows, cols = w.shape
    tr = _pick(rows, 256, SUBLANES)
    c1 = 1.0 / (1.0 - ADAM_B1 ** ADAM_STEP)
    c2 = 1.0 / (1.0 - ADAM_B2 ** ADAM_STEP)

    def body(w_ref, m_ref, v_ref, p_ref, g_ref, d_ref, mo_ref, vo_ref):
        g = p_ref[0].astype(F32)
        for q in range(1, N_DEV):
            g = g + p_ref[q].astype(F32)
        mn = ADAM_B1 * m_ref[...] + (1.0 - ADAM_B1) * g
        vn = ADAM_B2 * v_ref[...] + (1.0 - ADAM_B2) * (g * g)
        g_ref[...] = g
        mo_ref[...] = mn
        vo_ref[...] = vn
        d_ref[...] = -ADAM_LR * ((mn * c1) / (jnp.sqrt(vn * c2) + ADAM_EPS) + ADAM_WD * w_ref[...])

    blk = pl.BlockSpec((tr, cols), lambda i: (i, 0))
    return pl.pallas_call(
        body, name=name, grid=(rows // tr,),
        in_specs=[blk, blk, blk, pl.BlockSpec((N_DEV, tr, cols), lambda i: (0, i, 0))],
        out_specs=[blk] * 4, out_shape=[jax.ShapeDtypeStruct((rows, cols), F32)] * 4,
        compiler_params=_params(("parallel",)),
    )(w, m, v, parts)


WEIGHTS = ("meta_tokens", "norm_mix_g", "w_in", "b_gate", "conv_w", "conv_b", "w_rec_a", "b_rec_a", "w_rec_i",
           "b_rec_i", "lru_lambda", "q_norm_g", "w_uq", "kv_norm_g", "w_ukv", "w_branch", "w_out", "norm_ffn_g",
           "w_ffn_in", "w_ffn_out", "final_norm_g")
SHARDED = {"meta_tokens": True, "w_in": True, "b_gate": True, "conv_w": True, "w_uq": True, "w_ukv": True,
           "w_branch": False, "w_out": False, "w_ffn_in": True, "w_ffn_out": False}
REPLICATED = tuple(n for n in WEIGHTS if n not in SHARDED)


def _as2d(a):
    return a.reshape(-1, a.shape[-1])


def _full_from_gathered(g, by_cols):
    if by_cols:
        return jnp.transpose(g, (1, 0, 2)).reshape(g.shape[1], N_DEV * g.shape[2])
    return g.reshape(N_DEV * g.shape[1], g.shape[2])


def _blocks_from_full(full, by_cols):
    if by_cols:
        r, c = full.shape
        return jnp.transpose(full.reshape(r, N_DEV, c // N_DEV), (1, 0, 2))
    return full.reshape(N_DEV, full.shape[0] // N_DEV, full.shape[1])


def _pack(arrs):
    flat = jnp.concatenate([a.reshape(-1) for a in arrs])
    rows = -(-flat.shape[0] // (LANES * SUBLANES)) * SUBLANES
    return jnp.pad(flat, (0, rows * LANES - flat.shape[0])).reshape(rows, LANES)


def _unpack(packed, like):
    flat = packed.reshape(-1)
    out, off = [], 0
    for a in like:
        out.append(flat[off:off + a.size].reshape(a.shape))
        off += a.size
    return out


def kernel(x, meta_tokens, norm_mix_g, w_in, b_gate, conv_w, conv_b, w_rec_a, b_rec_a, w_rec_i, b_rec_i, lru_lambda, q_norm_g, w_uq, kv_norm_g, w_ukv, w_branch, w_out, norm_ffn_g, w_ffn_in, w_ffn_out, final_norm_g, loss_target, m_meta_tokens, m_norm_mix_g, m_w_in, m_b_gate, m_conv_w, m_conv_b, m_w_rec_a, m_b_rec_a, m_w_rec_i, m_b_rec_i, m_lru_lambda, m_q_norm_g, m_w_uq, m_kv_norm_g, m_w_ukv, m_w_branch, m_w_out, m_norm_ffn_g, m_w_ffn_in, m_w_ffn_out, m_final_norm_g, v_meta_tokens, v_norm_mix_g, v_w_in, v_b_gate, v_conv_w, v_conv_b, v_w_rec_a, v_b_rec_a, v_w_rec_i, v_b_rec_i, v_lru_lambda, v_q_norm_g, v_w_uq, v_kv_norm_g, v_w_ukv, v_w_branch, v_w_out, v_norm_ffn_g, v_w_ffn_in, v_w_ffn_out, v_final_norm_g):
    w = dict(meta_tokens=meta_tokens, norm_mix_g=norm_mix_g, w_in=w_in, b_gate=b_gate, conv_w=conv_w, conv_b=conv_b,
             w_rec_a=w_rec_a, b_rec_a=b_rec_a, w_rec_i=w_rec_i, b_rec_i=b_rec_i, lru_lambda=lru_lambda,
             q_norm_g=q_norm_g, w_uq=w_uq, kv_norm_g=kv_norm_g, w_ukv=w_ukv, w_branch=w_branch, w_out=w_out,
             norm_ffn_g=norm_ffn_g, w_ffn_in=w_ffn_in, w_ffn_out=w_ffn_out, final_norm_g=final_norm_g)
    m = dict(meta_tokens=m_meta_tokens, norm_mix_g=m_norm_mix_g, w_in=m_w_in, b_gate=m_b_gate, conv_w=m_conv_w,
             conv_b=m_conv_b, w_rec_a=m_w_rec_a, b_rec_a=m_b_rec_a, w_rec_i=m_w_rec_i, b_rec_i=m_b_rec_i,
             lru_lambda=m_lru_lambda, q_norm_g=m_q_norm_g, w_uq=m_w_uq, kv_norm_g=m_kv_norm_g, w_ukv=m_w_ukv,
             w_branch=m_w_branch, w_out=m_w_out, norm_ffn_g=m_norm_ffn_g, w_ffn_in=m_w_ffn_in,
             w_ffn_out=m_w_ffn_out, final_norm_g=m_final_norm_g)
    v = dict(meta_tokens=v_meta_tokens, norm_mix_g=v_norm_mix_g, w_in=v_w_in, b_gate=v_b_gate, conv_w=v_conv_w,
             conv_b=v_conv_b, w_rec_a=v_w_rec_a, b_rec_a=v_b_rec_a, w_rec_i=v_w_rec_i, b_rec_i=v_b_rec_i,
             lru_lambda=v_lru_lambda, q_norm_g=v_q_norm_g, w_uq=v_w_uq, kv_norm_g=v_kv_norm_g, w_ukv=v_w_ukv,
             w_branch=v_w_branch, w_out=v_w_out, norm_ffn_g=v_norm_ffn_g, w_ffn_in=v_w_ffn_in,
             w_ffn_out=v_w_ffn_out, final_norm_g=v_final_norm_g)

    seq, d_model = x.shape[1], x.shape[2]
    length = N_META + seq
    lp = -(-length // LANES) * LANES
    t_attn = _pick(lp, 640)
    t_rnn = LANES

    small = ("meta_tokens", "b_gate", "conv_w")
    names = list(SHARDED)
    gathered = _exchange([_as2d(w[n]) if n in small else _as2d(w[n]).astype(BF16) for n in names],
                         [True] * len(names), "gather_weights")
    full = {n: _full_from_gathered(g, SHARDED[n]) for n, g in zip(names, gathered)}

    splits = (D_RNN, D_RNN, Q_RANK, KV_RANK, QK_ROPE, 2 * d_model)
    offs = [0]
    for s in splits:
        offs.append(offs[-1] + s)
    w_x, w_g, w_q, w_kv, w_kr, w_m = (full["w_in"][:, offs[s]:offs[s + 1]] for s in range(6))
    w_kr = jnp.pad(w_kr, ((0, 0), (0, LANES - QK_ROPE)))
    w_uq_pad = jnp.pad(full["w_uq"].reshape(Q_RANK, N_HEADS, QK_NOPE + QK_ROPE),
                       ((0, 0), (0, 0), (0, HEAD_PAD - QK_NOPE - QK_ROPE))).reshape(Q_RANK, N_HEADS * HEAD_PAD)
    w_ukv3 = full["w_ukv"].reshape(KV_RANK, N_HEADS, QK_NOPE + V_DIM)
    w_k_pad = jnp.pad(w_ukv3[:, :, :QK_NOPE], ((0, 0), (0, 0), (0, HEAD_PAD - QK_NOPE))).reshape(
        KV_RANK, N_HEADS * HEAD_PAD)
    w_v = w_ukv3[:, :, QK_NOPE:].reshape(KV_RANK, N_HEADS * V_DIM)
    wb_r, wb_a = full["w_branch"][:D_RNN], full["w_branch"][D_RNN:]
    bg = full["b_gate"].reshape(1, 2 * d_model)
    pv = jnp.concatenate([full["conv_w"], conv_b, b_rec_a, b_rec_i, lru_lambda], axis=0)
    wa_b, wi_b = w_rec_a[0].astype(BF16), w_rec_i[0].astype(BF16)
    g_final = final_norm_g.reshape(1, d_model)

    h0 = jnp.concatenate([full["meta_tokens"], x[0], jnp.zeros((lp - length, d_model), F32)], axis=0)
    tgt = jnp.pad(loss_target[0], ((N_META, lp - length), (0, 0)))
    tabs = _rope_tables(lp)

    z = _rmsnorm_fwd(h0, norm_mix_g, "norm_mix")
    ux = _mm([(z, w_x)], "nn", "in_x")
    ug = _mm([(z, w_g)], "nn", "in_g")
    uq = _mm([(z, w_q)], "nn", "in_q")
    ukv = _mm([(z, w_kv)], "nn", "in_kv")
    ukr = _mm([(z, w_kr)], "nn", "in_kr")
    um = _mm([(z, w_m)], "nn", "in_m")
    y_rnn, hs = _rglru_fwd(ux, ug, pv, wa_b, wi_b, t_rnn, "rglru_fwd")
    qn = _rmsnorm_fwd(uq, q_norm_g, "norm_q")
    kvn = _rmsnorm_fwd(ukv, kv_norm_g, "norm_kv")
    qpad = _mm([(qn, w_uq_pad)], "nn", "up_q")
    kpad = _mm([(kvn, w_k_pad)], "nn", "up_k")
    vh = _mm([(kvn, w_v)], "nn", "up_v", out_dtype=BF16)
    qh, kh = _rope_fwd(qpad, kpad, ukr, tabs, "rope_fwd")
    oh, lse = _attn_fwd(qh, kh, vh, t_attn, "attn_fwd")
    p_rnn = _mm([(y_rnn, wb_r)], "nn", "branch_rnn")
    p_att = _mm([(oh, wb_a)], "nn", "branch_att")
    mixed = _mix_fwd(um, p_rnn, p_att, bg, "mix_fwd")
    h1 = _mm([(mixed, full["w_out"])], "nn", "out_proj", res=h0)
    zf = _rmsnorm_fwd(h1, norm_ffn_g, "norm_ffn")
    gu = _mm([(zf, full["w_ffn_in"])], "nn", "ffn_in")
    act = _swiglu_fwd(gu, "swiglu_fwd")
    h2 = _mm([(act, full["w_ffn_out"])], "nn", "ffn_out", res=h1)
    dh2, dg_final, _, loss_part = _loss_bwd(h2, tgt, g_final, seq, "loss_bwd")

    d_act = _mm([(dh2, full["w_ffn_out"])], "nt", "d_act")
    dw_ffn_out = _mm_tn(act, dh2, "dw_ffn_out")
    d_gu = _swiglu_bwd(gu, d_act, "swiglu_bwd")
    dw_ffn_in = _mm_tn(zf, d_gu, "dw_ffn_in")
    d_zf = _mm([(d_gu, full["w_ffn_in"], D_FF, 0), (d_gu, full["w_ffn_in"], D_FF, 1)], "nt", "d_zf")
    dh1, dg_ffn = _rmsnorm_bwd(h1, d_zf, norm_ffn_g, "norm_ffn_bwd", res=dh2)
    d_mixed = _mm([(dh1, full["w_out"])], "nt", "d_mixed")
    dw_out = _mm_tn(mixed, dh1, "dw_out")
    d_prnn, d_patt, d_um, dbg = _mix_bwd(um, p_rnn, p_att, d_mixed, bg, "mix_bwd")
    d_yrnn = _mm([(d_prnn, wb_r)], "nt", "d_yrnn")
    d_oh = _mm([(d_patt, wb_a)], "nt", "d_oh", out_dtype=BF16)
    dwb_r = _mm_tn(y_rnn, d_prnn, "dw_branch_rnn")
    dwb_a = _mm_tn(oh, d_patt, "dw_branch_att")
    dqh, delta = _attn_bwd_dq(qh, kh, vh, d_oh, oh, lse, t_attn, "attn_bwd_dq")
    dkh, dvh = _attn_bwd_dkv(qh, kh, vh, d_oh, lse, delta, t_attn, "attn_bwd_dkv")
    dqpad, dkpad, dukr = _rope_bwd(dqh, dkh, tabs, "rope_bwd")
    d_qn = _mm([(dqpad, w_uq_pad)], "nt", "d_qn")
    dw_uq_pad = _mm_tn(qn, dqpad, "dw_uq")
    d_kvn = _mm([(dkpad, w_k_pad), (dvh, w_v)], "nt", "d_kvn")
    dw_k_pad = _mm_tn(kvn, dkpad, "dw_uk")
    dw_v = _mm_tn(kvn, dvh, "dw_uv")
    duq, dg_q = _rmsnorm_bwd(uq, d_qn, q_norm_g, "norm_q_bwd", out_dtype=BF16)
    dukv, dg_kv = _rmsnorm_bwd(ukv, d_kvn, kv_norm_g, "norm_kv_bwd", out_dtype=BF16)
    dux, dug, dpv, dwa, dwi = _rglru_bwd(ux, ug, hs, d_yrnn, pv, wa_b, wi_b, t_rnn, "rglru_bwd")
    d_z = _mm([(dux, w_x), (dug, w_g), (duq, w_q), (dukv, w_kv), (dukr, w_kr), (d_um, w_m)], "nt", "d_z")
    dw_in = jnp.concatenate([
        _mm_tn(z, dux, "dw_in_x"), _mm_tn(z, dug, "dw_in_g"), _mm_tn(z, duq, "dw_in_q"),
        _mm_tn(z, dukv, "dw_in_kv"), _mm_tn(z, dukr, "dw_in_kr")[:, :QK_ROPE], _mm_tn(z, d_um, "dw_in_m")], axis=1)
    dh0, dg_mix = _rmsnorm_bwd(h0, d_z, norm_mix_g, "norm_mix_bwd", res=dh1)

    dw_uq = dw_uq_pad.reshape(Q_RANK, N_HEADS, HEAD_PAD)[:, :, :QK_NOPE + QK_ROPE].reshape(Q_RANK, -1)
    dw_ukv = jnp.concatenate([dw_k_pad.reshape(KV_RANK, N_HEADS, HEAD_PAD)[:, :, :QK_NOPE],
                              dw_v.reshape(KV_RANK, N_HEADS, V_DIM)], axis=2).reshape(KV_RANK, -1)
    grad_full = dict(
        meta_tokens=dh0[:N_META], w_in=dw_in, b_gate=dbg.reshape(2, d_model), conv_w=dpv[:CONV_WIDTH],
        w_uq=dw_uq, w_ukv=dw_ukv, w_branch=jnp.concatenate([dwb_r, dwb_a], axis=0), w_out=dw_out,
        w_ffn_in=dw_ffn_in, w_ffn_out=dw_ffn_out)
    grad_rep = dict(
        norm_mix_g=dg_mix, conv_b=dpv[4:5], w_rec_a=dwa, b_rec_a=dpv[5:6], w_rec_i=dwi, b_rec_i=dpv[6:7],
        lru_lambda=dpv[7:8], q_norm_g=dg_q, kv_norm_g=dg_kv, norm_ffn_g=dg_ffn, final_norm_g=dg_final)

    to_send = [_blocks_from_full(grad_full[n], SHARDED[n]).astype(F32 if n in small else BF16) for n in names]
    rep_pack = _pack([grad_rep[n] for n in REPLICATED])
    *recv, recv_rep = _exchange(to_send + [rep_pack], [False] * len(names) + [True], "exchange_grads")

    grads, deltas, new_m, new_v = {}, {}, {}, {}
    for n, parts in zip(names, recv):
        g2, d2, m2, v2 = _adamw(_as2d(w[n]), _as2d(m[n]), _as2d(v[n]), parts, "adamw_" + n)
        for store, val in ((grads, g2), (deltas, d2), (new_m, m2), (new_v, v2)):
            store[n] = val.reshape(w[n].shape)
    rep_like = [w[n] for n in REPLICATED]
    outs = _adamw(_pack(rep_like), _pack([m[n] for n in REPLICATED]), _pack([v[n] for n in REPLICATED]),
                  recv_rep, "adamw_replicated")
    for store, val in zip((grads, deltas, new_m, new_v), outs):
        for n, a in zip(REPLICATED, _unpack(val, rep_like)):
            store[n] = a

    loss = lax.psum(loss_part[0, 0], MESH_AXES)
    grad_x = dh0[N_META:length][None]
    return (loss, grad_x, *[grads[n] for n in WEIGHTS], *[deltas[n] for n in WEIGHTS],
            *[new_m[n] for n in WEIGHTS], *[new_v[n] for n in WEIGHTS])
```

```python
import functools
import math

import jax
import jax.numpy as jnp
from jax import lax
from jax.experimental import pallas as pl
from jax.experimental.pallas import tpu as pltpu

F32 = jnp.float32
BF16 = jnp.bfloat16

N_DEV = 8
MESH_AXES = ("x", "y", "c")
LANES = 128
SUBLANES = 8
VMEM_LIMIT = 56 * 1024 * 1024

N_META = 16
CHUNK_SHIFT = 6
CHUNK_BIAS = 64 - N_META
EPS = 1e-6
D_RNN = 1280
RNN_BLOCKS = 10
CONV_WIDTH = 4
LRU_C = 8.0
N_HEADS = 8
QK_NOPE = 128
QK_ROPE = 64
V_DIM = 128
HEAD_PAD = 256
Q_RANK = 384
KV_RANK = 256
ROPE_THETA = 10000.0
ATTN_SCALE = 1.0 / math.sqrt(QK_NOPE + QK_ROPE)
NEG = -1e30
LOG2E = 1.0 / math.log(2.0)
SCALE_LOG2E = ATTN_SCALE * LOG2E
Q_SPLIT = 2
Q_ALIGN = LANES // Q_SPLIT
ROW_GROUP = 32
SPILL = LANES
D_FF = 2816

ADAM_LR = 0.001
ADAM_B1 = 0.9
ADAM_B2 = 0.999
ADAM_EPS = 1e-08
ADAM_WD = 0.01
ADAM_STEP = 10

NN = (((1,), (0,)), ((), ()))
NT = (((1,), (1,)), ((), ()))
TN = (((0,), (0,)), ((), ()))


def _pick(n, cap, base=LANES):
    best = None
    for t in range(base, min(n, cap) + 1, base):
        if n % t == 0:
            best = t
    return best if best is not None else n


def _params(sem=None):
    return pltpu.CompilerParams(dimension_semantics=sem, vmem_limit_bytes=VMEM_LIMIT)


def _sig(x):
    return 1.0 / (1.0 + jnp.exp(-x))


def _exchange(srcs, gather, name):
    n = len(srcs)
    out_shape = [jax.ShapeDtypeStruct((N_DEV,) + (s.shape if g else s.shape[1:]), s.dtype)
                 for s, g in zip(srcs, gather)]

    def body(*refs):
        src, dst = refs[:n], refs[n:2 * n]
        send_sems, recv_sems, local_sems = refs[2 * n:]
        x, y, c = lax.axis_index("x"), lax.axis_index("y"), lax.axis_index("c")
        me = 4 * x + 2 * y + c
        local = []
        for t in range(n):
            cp = pltpu.make_async_copy(src[t] if gather[t] else src[t].at[me], dst[t].at[me], local_sems.at[t])
            cp.start()
            local.append(cp)
        sends, recvs = [], []
        for k in range(1, N_DEV):
            px = 1 - x if k & 4 else x
            py = 1 - y if k & 2 else y
            pc = 1 - c if k & 1 else c
            peer = 4 * px + 2 * py + pc
            for t in range(n):
                cp = pltpu.make_async_remote_copy(
                    src_ref=src[t] if gather[t] else src[t].at[peer], dst_ref=dst[t].at[me],
                    send_sem=send_sems.at[t, k - 1], recv_sem=recv_sems.at[t, k - 1],
                    device_id=(px, py, pc), device_id_type=pl.DeviceIdType.MESH)
                cp.start()
                sends.append(cp)
                recvs.append(pltpu.make_async_remote_copy(
                    src_ref=src[t] if gather[t] else src[t].at[peer], dst_ref=dst[t].at[peer],
                    send_sem=send_sems.at[t, k - 1], recv_sem=recv_sems.at[t, k - 1],
                    device_id=(px, py, pc), device_id_type=pl.DeviceIdType.MESH))
        for cp in recvs:
            cp.wait_recv()
        for cp in sends:
            cp.wait_send()
        for cp in local:
            cp.wait()

    any_spec = pl.BlockSpec(memory_space=pl.ANY)
    return pl.pallas_call(
        body, name=name, out_shape=out_shape,
        in_specs=[any_spec] * n, out_specs=[any_spec] * n,
        scratch_shapes=[pltpu.SemaphoreType.DMA((n, N_DEV - 1)), pltpu.SemaphoreType.DMA((n, N_DEV - 1)),
                        pltpu.SemaphoreType.DMA((n,))],
    )(*srcs)


def _mm(pairs, mode, name, res=None, out_dtype=F32):
    pairs = [p if len(p) == 4 else (p[0], p[1], p[0].shape[1], 0) for p in pairs]
    m = pairs[0][0].shape[0]
    n = pairs[0][1].shape[1] if mode == "nn" else pairs[0][1].shape[0]
    tm, tn = _pick(m, 640), _pick(n, 640)
    np_ = len(pairs)
    dims = NN if mode == "nn" else NT

    def body(*refs):
        acc = None
        for s in range(np_):
            d = lax.dot_general(refs[2 * s][...].astype(BF16), refs[2 * s + 1][...].astype(BF16), dims,
                                preferred_element_type=F32)
            acc = d if acc is None else acc + d
        if res is not None:
            acc = acc + refs[2 * np_][...]
        refs[-1][...] = acc.astype(out_dtype)

    in_specs, args = [], []
    for a, b, kt, kb in pairs:
        in_specs.append(pl.BlockSpec((tm, kt), lambda i, j, kb=kb: (i, kb)))
        if mode == "nn":
            in_specs.append(pl.BlockSpec((kt, tn), lambda i, j, kb=kb: (kb, j)))
        else:
            in_specs.append(pl.BlockSpec((tn, kt), lambda i, j, kb=kb: (j, kb)))
        args += [a, b]
    if res is not None:
        in_specs.append(pl.BlockSpec((tm, tn), lambda i, j: (i, j)))
        args.append(res)
    return pl.pallas_call(
        body, name=name, grid=(m // tm, n // tn), in_specs=in_specs,
        out_specs=pl.BlockSpec((tm, tn), lambda i, j: (i, j)),
        out_shape=jax.ShapeDtypeStruct((m, n), out_dtype),
        compiler_params=_params(("parallel", "parallel")),
    )(*args)


def _mm_tn(a, b, name):
    m, k = a.shape
    n = b.shape[1]
    tm, tk, tn = _pick(m, 640), _pick(k, 1408), _pick(n, 640)

    def body(a_ref, b_ref, o_ref):
        @pl.when(pl.program_id(2) == 0)
        def _():
            o_ref[...] = jnp.zeros_like(o_ref)

        o_ref[...] += lax.dot_general(a_ref[...].astype(BF16), b_ref[...].astype(BF16), TN,
                                      preferred_element_type=F32)

    return pl.pallas_call(
        body, name=name, grid=(k // tk, n // tn, m // tm),
        in_specs=[pl.BlockSpec((tm, tk), lambda i, j, r: (r, i)), pl.BlockSpec((tm, tn), lambda i, j, r: (r, j))],
        out_specs=pl.BlockSpec((tk, tn), lambda i, j, r: (i, j)),
        out_shape=jax.ShapeDtypeStruct((k, n), F32),
        compiler_params=_params(("parallel", "parallel", "arbitrary")),
    )(a, b)


ROW_TILE_BYTES = 6 * 1024 * 1024


def _row_tile(rows, row_in, row_out):
    per_row = sum((r[1] * r[0].dtype.itemsize) if isinstance(r, tuple) else (r.shape[1] * r.dtype.itemsize)
                  for r in row_in)
    per_row += sum(w * jnp.dtype(dt).itemsize for w, dt in row_out)
    return _pick(rows, min(640, max(LANES, ROW_TILE_BYTES // per_row)))


def _rowcall(body, name, rows, row_in, full_in, row_out, acc_out=()):
    tr = _row_tile(rows, row_in, row_out)
    n_steps = rows // tr
    in_specs, args = [], []
    for r in row_in:
        arr, w, cb = r if isinstance(r, tuple) else (r, r.shape[1], 0)
        in_specs.append(pl.BlockSpec((tr, w), lambda i, cb=cb: (i, cb)))
        args.append(arr)
    for f in full_in:
        in_specs.append(pl.BlockSpec(f.shape, lambda i, nd=f.ndim: (0,) * nd))
        args.append(f)
    out_specs = [pl.BlockSpec((tr, w), lambda i: (i, 0)) for w, _ in row_out]
    out_shape = [jax.ShapeDtypeStruct((rows, w), dt) for w, dt in row_out]
    for shp, dt in acc_out:
        out_specs.append(pl.BlockSpec(shp, lambda i, nd=len(shp): (0,) * nd))
        out_shape.append(jax.ShapeDtypeStruct(shp, dt))

    def wrapped(*refs):
        body(pl.program_id(0), n_steps, *refs)

    return pl.pallas_call(
        wrapped, name=name, grid=(n_steps,), in_specs=in_specs, out_specs=out_specs, out_shape=out_shape,
        compiler_params=_params(("arbitrary",) if acc_out else ("parallel",)),
    )(*args)


def _rmsnorm_fwd(x, g, name):
    rows, w = x.shape

    def body(i, n, x_ref, g_ref, o_ref):
        xv = x_ref[...]
        r = lax.rsqrt(jnp.mean(xv * xv, axis=-1, keepdims=True) + EPS)
        o_ref[...] = (xv * r * g_ref[...]).astype(BF16)

    return _rowcall(body, name, rows,[x], [g], [(w, BF16)])[0]


def _rmsnorm_bwd_math(xv, dy, g):
    w = xv.shape[-1]
    r = lax.rsqrt(jnp.mean(xv * xv, axis=-1, keepdims=True) + EPS)
    t = dy * g
    dx = r * t - xv * (r * r * r * (jnp.sum(t * xv, axis=-1, keepdims=True) / w))
    dg = jnp.sum(dy * xv * r, axis=0, keepdims=True)
    return dx, dg


def _rmsnorm_bwd(x, dy, g, name, res=None, out_dtype=F32):
    rows, w = x.shape

    def body(i, n, *refs):
        x_ref, dy_ref = refs[0], refs[1]
        g_ref, dx_ref, dg_ref = refs[-3], refs[-2], refs[-1]
        dx, dg = _rmsnorm_bwd_math(x_ref[...], dy_ref[...], g_ref[...])
        if res is not None:
            dx = dx + refs[2][...]
        dx_ref[...] = dx.astype(out_dtype)

        @pl.when(i == 0)
        def _():
            dg_ref[...] = jnp.zeros_like(dg_ref)

        dg_ref[...] += dg

    row_in = [x, dy] + ([res] if res is not None else [])
    return _rowcall(body, name, rows,row_in, [g], [(w, out_dtype)], [((1, w), F32)])


def _loss_bwd(h2, tgt, g, seq, name):
    rows, w = h2.shape
    tr = _row_tile(rows, [h2, tgt], [(w, F32)])

    def body(i, n, h_ref, t_ref, g_ref, dh_ref, dg_ref, lcol_ref, loss_ref):
        hv, gv = h_ref[...], g_ref[...]
        row = i * tr + lax.broadcasted_iota(jnp.int32, (tr, w), 0)
        valid = jnp.logical_and(row >= N_META, row < N_META + seq)
        r = lax.rsqrt(jnp.mean(hv * hv, axis=-1, keepdims=True) + EPS)
        err = jnp.where(valid, hv * r * gv - t_ref[...], 0.0)
        dx, dg = _rmsnorm_bwd_math(hv, err * (1.0 / w), gv)
        dh_ref[...] = dx

        @pl.when(i == 0)
        def _():
            dg_ref[...] = jnp.zeros_like(dg_ref)
            lcol_ref[...] = jnp.zeros_like(lcol_ref)

        dg_ref[...] += dg
        lcol_ref[...] += jnp.sum(err * err, axis=0, keepdims=True)

        @pl.when(i == n - 1)
        def _():
            total = jnp.sum(lcol_ref[...], axis=1, keepdims=True) * (0.5 / w)
            loss_ref[...] = jnp.broadcast_to(total, loss_ref.shape)

    return _rowcall(body, name, rows, [h2, tgt], [g], [(w, F32)],
                    [((1, w), F32), ((1, w), F32), ((1, LANES), F32)])


def _mix_fwd(um, p_rnn, p_att, bg, name):
    rows, d = p_rnn.shape

    def body(i, n, u0_ref, u1_ref, pr_ref, pa_ref, bg_ref, o_ref):
        g0 = _sig(u0_ref[...] + bg_ref[:, :d])
        g1 = _sig(u1_ref[...] + bg_ref[:, d:])
        o_ref[...] = (g0 * pr_ref[...] + g1 * pa_ref[...]).astype(BF16)

    return _rowcall(body, name, rows,[(um, d, 0), (um, d, 1), p_rnn, p_att], [bg],
                    [(d, BF16)])[0]


def _mix_bwd(um, p_rnn, p_att, dmix, bg, name):
    rows, d = p_rnn.shape

    def body(i, n, u0_ref, u1_ref, pr_ref, pa_ref, dm_ref, bg_ref, dpr_ref, dpa_ref, dum_ref, dbg_ref):
        g0 = _sig(u0_ref[...] + bg_ref[:, :d])
        g1 = _sig(u1_ref[...] + bg_ref[:, d:])
        dm = dm_ref[...]
        dpr_ref[...] = (dm * g0).astype(BF16)
        dpa_ref[...] = (dm * g1).astype(BF16)
        du0 = dm * pr_ref[...] * g0 * (1.0 - g0)
        du1 = dm * pa_ref[...] * g1 * (1.0 - g1)
        dum_ref[:, :d] = du0.astype(BF16)
        dum_ref[:, d:] = du1.astype(BF16)

        @pl.when(i == 0)
        def _():
            dbg_ref[...] = jnp.zeros_like(dbg_ref)

        dbg_ref[:, :d] += jnp.sum(du0, axis=0, keepdims=True)
        dbg_ref[:, d:] += jnp.sum(du1, axis=0, keepdims=True)

    return _rowcall(body, name, rows,[(um, d, 0), (um, d, 1), p_rnn, p_att, dmix], [bg],
                    [(d, BF16), (d, BF16), (2 * d, BF16)], [((1, 2 * d), F32)])


def _swiglu_fwd(gu, name):
    rows, w2 = gu.shape
    f = w2 // 2

    def body(i, n, g_ref, u_ref, o_ref):
        gate = g_ref[...]
        o_ref[...] = (gate * _sig(gate) * u_ref[...]).astype(BF16)

    return _rowcall(body, name, rows,[(gu, f, 0), (gu, f, 1)], [], [(f, BF16)])[0]


def _swiglu_bwd(gu, dact, name):
    rows, w2 = gu.shape
    f = w2 // 2

    def body(i, n, g_ref, u_ref, da_ref, o_ref):
        gate, da = g_ref[...], da_ref[...]
        sg = _sig(gate)
        o_ref[:, :f] = (da * u_ref[...] * (sg * (1.0 + gate * (1.0 - sg)))).astype(BF16)
        o_ref[:, f:] = (da * gate * sg).astype(BF16)

    return _rowcall(body, name, rows,[(gu, f, 0), (gu, f, 1), dact], [], [(w2, BF16)])[0]


def _rope_tables(lp):
    idx = jnp.arange(lp, dtype=jnp.int32).astype(F32)
    inv_freq = ROPE_THETA ** (-jnp.arange(0, QK_ROPE, 2, dtype=F32) / QK_ROPE)
    ang = idx[:, None] * inv_freq[None, :]
    cos, sin = jnp.cos(ang), jnp.sin(ang)
    half = QK_ROPE // 2
    z = lambda wdt: jnp.zeros((lp, wdt), F32)
    tc = jnp.concatenate([cos, cos, z(LANES - 2 * half)], axis=1)
    ts1 = jnp.concatenate([-sin, z(LANES - half)], axis=1)
    ts2 = jnp.concatenate([z(half), sin, z(LANES - 2 * half)], axis=1)
    return tc, ts1, ts2


def _rope(xv, tc, ts1, ts2):
    half = QK_ROPE // 2
    return xv * tc + pltpu.roll(xv, LANES - half, 1) * ts1 + pltpu.roll(xv, half, 1) * ts2


def _rope_t(dv, tc, ts1, ts2):
    half = QK_ROPE // 2
    return dv * tc + pltpu.roll(dv * ts1, half, 1) + pltpu.roll(dv * ts2, LANES - half, 1)


def _rope_fwd(qpad, kpad, ukr, tabs, name):
    rows, w = qpad.shape

    def body(i, n, q_ref, k_ref, r_ref, c_ref, s1_ref, s2_ref, qo_ref, ko_ref):
        tc, ts1, ts2 = c_ref[...], s1_ref[...], s2_ref[...]
        kr = _rope(r_ref[...], tc, ts1, ts2).astype(BF16)
        for h in range(N_HEADS):
            lo, mid, hi = h * HEAD_PAD, h * HEAD_PAD + QK_NOPE, (h + 1) * HEAD_PAD
            qo_ref[:, lo:mid] = q_ref[:, lo:mid].astype(BF16)
            qo_ref[:, mid:hi] = _rope(q_ref[:, mid:hi], tc, ts1, ts2).astype(BF16)
            ko_ref[:, lo:mid] = k_ref[:, lo:mid].astype(BF16)
            ko_ref[:, mid:hi] = kr

    return _rowcall(body, name, rows,[qpad, kpad, ukr, *tabs], [], [(w, BF16), (w, BF16)])


def _rope_bwd(dq, dk, tabs, name):
    rows, w = dq.shape

    def body(i, n, dq_ref, dk_ref, c_ref, s1_ref, s2_ref, qo_ref, ko_ref, ro_ref):
        tc, ts1, ts2 = c_ref[...], s1_ref[...], s2_ref[...]
        dkr = None
        for h in range(N_HEADS):
            lo, mid, hi = h * HEAD_PAD, h * HEAD_PAD + QK_NOPE, (h + 1) * HEAD_PAD
            qo_ref[:, lo:mid] = dq_ref[:, lo:mid].astype(BF16)
            qo_ref[:, mid:hi] = _rope_t(dq_ref[:, mid:hi], tc, ts1, ts2).astype(BF16)
            ko_ref[:, lo:mid] = dk_ref[:, lo:mid].astype(BF16)
            ko_ref[:, mid:hi] = jnp.zeros((ko_ref.shape[0], hi - mid), BF16)
            part = dk_ref[:, mid:hi]
            dkr = part if dkr is None else dkr + part
        ro_ref[...] = _rope_t(dkr, tc, ts1, ts2).astype(BF16)

    return _rowcall(body, name, rows,[dq, dk, *tabs], [],
                    [(w, BF16), (w, BF16), (LANES, BF16)])


def _visible(q0, k0, tq, tk):
    qrow = q0 + lax.broadcasted_iota(jnp.int32, (tq, tk), 0)
    kcol = k0 + lax.broadcasted_iota(jnp.int32, (tq, tk), 1)
    return ((kcol + CHUNK_BIAS) >> CHUNK_SHIFT) <= ((qrow + CHUNK_BIAS) >> CHUNK_SHIFT)


def _visible_t(k0, q0, tk, tq):
    krow = k0 + lax.broadcasted_iota(jnp.int32, (tk, tq), 0)
    qcol = q0 + lax.broadcasted_iota(jnp.int32, (tk, tq), 1)
    return ((krow + CHUNK_BIAS) >> CHUNK_SHIFT) <= ((qcol + CHUNK_BIAS) >> CHUNK_SHIFT)


def _lanes(v, width):
    return jnp.tile(v, (1, width // LANES))


def _pipelined_chunks(n_full, scores, absorb):
    scores(0, 0)

    def pair(jj, carry):
        a = 2 * jj
        scores(a + 1, 1)
        absorb(a, 0, False)
        scores(a + 2, 0)
        absorb(a + 1, 1, False)
        return carry

    lax.fori_loop(0, n_full // 2, pair, 0)

    @pl.when(n_full % 2 == 0)
    def _():
        absorb(n_full, 0, True)

    @pl.when(n_full % 2 == 1)
    def _():
        scores(n_full, 1)
        absorb(n_full - 1, 0, False)
        absorb(n_full, 1, True)


def _attn_fwd(q, k, v, t, name):
    lp = q.shape[0]
    nt = lp // t

    def body(q_ref, k_ref, v_ref, o_ref, lse_ref, m_s, l_s, acc_s, a_s, sa_s, sb_s, p_s):
        i = pl.program_id(1)
        m_s[...] = jnp.full(m_s.shape, NEG, F32)
        l_s[...] = jnp.zeros(l_s.shape, F32)
        acc_s[...] = jnp.zeros(acc_s.shape, F32)

        s_bufs = (sa_s, sb_s)

        def scores(j, slot):
            r0 = pl.multiple_of(j * t, t)
            s_bufs[slot][...] = lax.dot_general(q_ref[...], k_ref[pl.ds(r0, t), :], NT,
                                                preferred_element_type=F32)

        def absorb(j, slot, masked):
            for r in range(0, t, ROW_GROUP):
                rows = slice(r, r + ROW_GROUP)
                s = s_bufs[slot][rows, :]
                if masked:
                    s = jnp.where(_visible(i * t + r, i * t, ROW_GROUP, t), s, NEG)
                m_prev = m_s[rows, :]
                m_new = jnp.maximum(m_prev, jnp.max(s, axis=1, keepdims=True))
                alpha = jnp.exp2((m_prev - m_new) * SCALE_LOG2E)
                p = jnp.exp2((s - _lanes(m_new, t)) * SCALE_LOG2E)
                l_s[rows, :] = alpha * l_s[rows, :] + jnp.sum(p, axis=1, keepdims=True)
                m_s[rows, :] = m_new
                a_s[rows, :] = alpha
                p_s[rows, :] = p.astype(BF16)
            r0 = pl.multiple_of(j * t, t)
            acc_s[...] = a_s[...] * acc_s[...] + jnp.dot(p_s[...], v_ref[pl.ds(r0, t), :],
                                                         preferred_element_type=F32)

        @pl.when(i + 1 < nt)
        def _():
            r1 = pl.multiple_of((i + 1) * t, t)
            rows = slice(t - SPILL, t)
            s = lax.dot_general(q_ref[rows, :], k_ref[pl.ds(r1, SPILL), :], NT, preferred_element_type=F32)
            s = jnp.where(_visible(i * t + t - SPILL, (i + 1) * t, SPILL, SPILL), s, NEG)
            m_prev = m_s[rows, :]
            m_new = jnp.maximum(m_prev, jnp.max(s, axis=1, keepdims=True))
            alpha = jnp.exp2((m_prev - m_new) * SCALE_LOG2E)
            p = jnp.exp2((s - m_new) * SCALE_LOG2E)
            l_s[rows, :] = alpha * l_s[rows, :] + jnp.sum(p, axis=1, keepdims=True)
            acc_s[rows, :] = alpha * acc_s[rows, :] + jnp.dot(p.astype(BF16), v_ref[pl.ds(r1, SPILL), :],
                                                              preferred_element_type=F32)
            m_s[rows, :] = m_new

        _pipelined_chunks(i, scores, absorb)
        o_ref[...] = (acc_s[...] / l_s[...]).astype(BF16)
        lse_ref[...] = m_s[...] * ATTN_SCALE + jnp.log(l_s[...])

    return pl.pallas_call(
        body, name=name, grid=(N_HEADS, nt),
        in_specs=[pl.BlockSpec((t, HEAD_PAD), lambda h, i: (i, h)),
                  pl.BlockSpec((lp, HEAD_PAD), lambda h, i: (0, h)),
                  pl.BlockSpec((lp, V_DIM), lambda h, i: (0, h))],
        out_specs=[pl.BlockSpec((t, V_DIM), lambda h, i: (i, h)),
                   pl.BlockSpec((None, t, LANES), lambda h, i: (h, i, 0))],
        out_shape=[jax.ShapeDtypeStruct((lp, N_HEADS * V_DIM), BF16),
                   jax.ShapeDtypeStruct((N_HEADS, lp, LANES), F32)],
        scratch_shapes=[pltpu.VMEM((t, LANES), F32), pltpu.VMEM((t, LANES), F32), pltpu.VMEM((t, V_DIM), F32),
                        pltpu.VMEM((t, LANES), F32), pltpu.VMEM((t, t), F32), pltpu.VMEM((t, t), F32),
                        pltpu.VMEM((t, t), BF16)],
        compiler_params=_params(("parallel", "arbitrary")),
    )(q, k, v)


def _attn_bwd_dq(q, k, v, do, o, lse, t, name):
    lp = q.shape[0]
    nt = lp // t

    def body(q_ref, k_ref, v_ref, do_ref, o_ref, lse_ref, dq_ref, l2row_ref, dlrow_ref,
             acc_s, l2_s, dl_s, sa_s, sb_s, da_s, db_s, ds_s):
        i = pl.program_id(1)
        delta = jnp.sum(do_ref[...].astype(F32) * o_ref[...].astype(F32), axis=1, keepdims=True)
        dl_s[...] = jnp.broadcast_to(delta, dl_s.shape)
        l2_s[...] = lse_ref[...] * LOG2E
        l2row_ref[...] = l2_s[...].T[0:SUBLANES, :]
        dlrow_ref[...] = dl_s[...].T[0:SUBLANES, :]
        acc_s[...] = jnp.zeros(acc_s.shape, F32)
        s_bufs, d_bufs = (sa_s, sb_s), (da_s, db_s)

        def dscores(s, dp, rows, width):
            p = jnp.exp2(s * SCALE_LOG2E - _lanes(l2_s[rows, :], width))
            return (p * (dp - _lanes(dl_s[rows, :], width))).astype(BF16)

        @pl.when(i + 1 < nt)
        def _():
            r1 = pl.multiple_of((i + 1) * t, t)
            rows = slice(t - SPILL, t)
            ks, vs = k_ref[pl.ds(r1, SPILL), :], v_ref[pl.ds(r1, SPILL), :]
            s = lax.dot_general(q_ref[rows, :], ks, NT, preferred_element_type=F32)
            s = jnp.where(_visible(i * t + t - SPILL, (i + 1) * t, SPILL, SPILL), s, NEG)
            dp = lax.dot_general(do_ref[rows, :], vs, NT, preferred_element_type=F32)
            acc_s[rows, :] += jnp.dot(dscores(s, dp, rows, SPILL), ks, preferred_element_type=F32)

        def scores(j, slot):
            r0 = pl.multiple_of(j * t, t)
            s_bufs[slot][...] = lax.dot_general(q_ref[...], k_ref[pl.ds(r0, t), :], NT,
                                                preferred_element_type=F32)
            d_bufs[slot][...] = lax.dot_general(do_ref[...], v_ref[pl.ds(r0, t), :], NT,
                                                preferred_element_type=F32)

        def absorb(j, slot, masked):
            for r in range(0, t, ROW_GROUP):
                rows = slice(r, r + ROW_GROUP)
                s = s_bufs[slot][rows, :]
                if masked:
                    s = jnp.where(_visible(i * t + r, i * t, ROW_GROUP, t), s, NEG)
                ds_s[rows, :] = dscores(s, d_bufs[slot][rows, :], rows, t)
            r0 = pl.multiple_of(j * t, t)
            acc_s[...] += jnp.dot(ds_s[...], k_ref[pl.ds(r0, t), :], preferred_element_type=F32)

        _pipelined_chunks(i, scores, absorb)
        dq_ref[...] = acc_s[...] * ATTN_SCALE

    stat_row = pl.BlockSpec((None, None, SUBLANES, t), lambda h, i: (h, i, 0, 0))
    return pl.pallas_call(
        body, name=name, grid=(N_HEADS, nt),
        in_specs=[pl.BlockSpec((t, HEAD_PAD), lambda h, i: (i, h)),
                  pl.BlockSpec((lp, HEAD_PAD), lambda h, i: (0, h)),
                  pl.BlockSpec((lp, V_DIM), lambda h, i: (0, h)),
                  pl.BlockSpec((t, V_DIM), lambda h, i: (i, h)),
                  pl.BlockSpec((t, V_DIM), lambda h, i: (i, h)),
                  pl.BlockSpec((None, t, LANES), lambda h, i: (h, i, 0))],
        out_specs=[pl.BlockSpec((t, HEAD_PAD), lambda h, i: (i, h)), stat_row, stat_row],
        out_shape=[jax.ShapeDtypeStruct((lp, N_HEADS * HEAD_PAD), F32),
                   jax.ShapeDtypeStruct((N_HEADS, nt, SUBLANES, t), F32),
                   jax.ShapeDtypeStruct((N_HEADS, nt, SUBLANES, t), F32)],
        scratch_shapes=[pltpu.VMEM((t, HEAD_PAD), F32), pltpu.VMEM((t, LANES), F32), pltpu.VMEM((t, LANES), F32),
                        pltpu.VMEM((t, t), F32), pltpu.VMEM((t, t), F32), pltpu.VMEM((t, t), F32),
                        pltpu.VMEM((t, t), F32), pltpu.VMEM((t, t), BF16)],
        compiler_params=_params(("parallel", "arbitrary")),
    )(q, k, v, do, o, lse)


def _attn_bwd_dkv(q, k, v, do, l2row, dlrow, t, name):
    lp = q.shape[0]
    nt = lp // t

    def body(q_ref, k_ref, v_ref, do_ref, l2_ref, dl_ref, dk_ref, dv_ref,
             dk_s, dv_s, sa_s, sb_s, da_s, db_s, p_s, ds_s):
        j = pl.program_id(1)
        dk_s[...] = jnp.zeros(dk_s.shape, F32)
        dv_s[...] = jnp.zeros(dv_s.shape, F32)
        s_bufs, d_bufs = (sa_s, sb_s), (da_s, db_s)

        def weights(st, dpt, l2r, dlr):
            pt = jnp.exp2(st * SCALE_LOG2E - l2r)
            return pt.astype(BF16), (pt * (dpt - dlr)).astype(BF16)

        @pl.when(j > 0)
        def _():
            q0 = pl.multiple_of(j * t - SPILL, SPILL)
            rows = slice(0, SPILL)
            qs, dos = q_ref[pl.ds(q0, SPILL), :], do_ref[pl.ds(q0, SPILL), :]
            st = lax.dot_general(k_ref[rows, :], qs, NT, preferred_element_type=F32)
            st = jnp.where(_visible_t(j * t, j * t - SPILL, SPILL, SPILL), st, NEG)
            dpt = lax.dot_general(v_ref[rows, :], dos, NT, preferred_element_type=F32)
            pt, dst = weights(st, dpt, l2_ref[j - 1, 0:1, t - SPILL:], dl_ref[j - 1, 0:1, t - SPILL:])
            dv_s[rows, :] += jnp.dot(pt, dos, preferred_element_type=F32)
            dk_s[rows, :] += jnp.dot(dst, qs, preferred_element_type=F32)

        def scores(c, slot):
            r0 = pl.multiple_of((nt - 1 - c) * t, t)
            s_bufs[slot][...] = lax.dot_general(k_ref[...], q_ref[pl.ds(r0, t), :], NT,
                                                preferred_element_type=F32)
            d_bufs[slot][...] = lax.dot_general(v_ref[...], do_ref[pl.ds(r0, t), :], NT,
                                                preferred_element_type=F32)

        def absorb(c, slot, masked):
            i = nt - 1 - c
            l2r, dlr = l2_ref[i, 0:1, :], dl_ref[i, 0:1, :]
            for r in range(0, t, ROW_GROUP):
                rows = slice(r, r + ROW_GROUP)
                st = s_bufs[slot][rows, :]
                if masked:
                    st = jnp.where(_visible_t(j * t + r, j * t, ROW_GROUP, t), st, NEG)
                p_s[rows, :], ds_s[rows, :] = weights(st, d_bufs[slot][rows, :], l2r, dlr)
            r0 = pl.multiple_of(i * t, t)
            dv_s[...] += jnp.dot(p_s[...], do_ref[pl.ds(r0, t), :], preferred_element_type=F32)
            dk_s[...] += jnp.dot(ds_s[...], q_ref[pl.ds(r0, t), :], preferred_element_type=F32)

        _pipelined_chunks(nt - 1 - j, scores, absorb)
        dk_ref[...] = dk_s[...] * ATTN_SCALE
        dv_ref[...] = dv_s[...].astype(BF16)

    stat_rows = pl.BlockSpec((None, nt, SUBLANES, t), lambda h, j: (h, 0, 0, 0))
    return pl.pallas_call(
        body, name=name, grid=(N_HEADS, nt),
        in_specs=[pl.BlockSpec((lp, HEAD_PAD), lambda h, j: (0, h)),
                  pl.BlockSpec((t, HEAD_PAD), lambda h, j: (j, h)),
                  pl.BlockSpec((t, V_DIM), lambda h, j: (j, h)),
                  pl.BlockSpec((lp, V_DIM), lambda h, j: (0, h)),
                  stat_rows, stat_rows],
        out_specs=[pl.BlockSpec((t, HEAD_PAD), lambda h, j: (j, h)),
                   pl.BlockSpec((t, V_DIM), lambda h, j: (j, h))],
        out_shape=[jax.ShapeDtypeStruct((lp, N_HEADS * HEAD_PAD), F32),
                   jax.ShapeDtypeStruct((lp, N_HEADS * V_DIM), BF16)],
        scratch_shapes=[pltpu.VMEM((t, HEAD_PAD), F32), pltpu.VMEM((t, V_DIM), F32),
                        pltpu.VMEM((t, t), F32), pltpu.VMEM((t, t), F32), pltpu.VMEM((t, t), F32),
                        pltpu.VMEM((t, t), F32), pltpu.VMEM((t, t), BF16), pltpu.VMEM((t, t), BF16)],
        compiler_params=_params(("parallel", "arbitrary")),
    )(q, k, v, do, l2row, dlrow)


def _shift_down(cur, prev8, k):
    r = pltpu.roll(cur, k, 0)
    row8 = lax.broadcasted_iota(jnp.int32, prev8.shape, 0)
    first = jnp.where(row8 < k, pltpu.roll(prev8, k, 0), r[0:SUBLANES])
    return jnp.concatenate([first, r[SUBLANES:]], axis=0)


def _shift_up(cur, next8, k):
    t = cur.shape[0]
    r = pltpu.roll(cur, t - k, 0)
    row8 = lax.broadcasted_iota(jnp.int32, next8.shape, 0)
    last = jnp.where(row8 >= SUBLANES - k, pltpu.roll(next8, SUBLANES - k, 0), r[t - SUBLANES:])
    return jnp.concatenate([r[:t - SUBLANES], last], axis=0)


def _scan_down(a, b):
    t = a.shape[0]
    row = lax.broadcasted_iota(jnp.int32, a.shape, 0)
    s = 1
    while s < t:
        keep = row >= s
        a_sh = jnp.where(keep, pltpu.roll(a, s, 0), 1.0)
        b_sh = jnp.where(keep, pltpu.roll(b, s, 0), 0.0)
        b = a * b_sh + b
        a = a * a_sh
        s *= 2
    return a, b


def _scan_up(a, b):
    t = a.shape[0]
    row = lax.broadcasted_iota(jnp.int32, a.shape, 0)
    s = 1
    while s < t:
        keep = row < t - s
        a_sh = jnp.where(keep, pltpu.roll(a, t - s, 0), 1.0)
        b_sh = jnp.where(keep, pltpu.roll(b, t - s, 0), 0.0)
        b = a * b_sh + b
        a = a * a_sh
        s *= 2
    return a, b


def _neg_expm1(y):
    series = -y * (1.0 + y * (0.5 + y * (1.0 / 6.0 + y * (1.0 / 24.0 + y * (1.0 / 120.0)))))
    return jnp.where(y > -0.1, series, 1.0 - jnp.exp(y))


def _log_sigmoid(x):
    return jnp.minimum(x, 0.0) - jnp.log(1.0 + jnp.exp(-jnp.abs(x)))


GELU_C = math.sqrt(2.0 / math.pi)
GELU_K = 0.044715


def _gelu(x):
    th = jnp.tanh(GELU_C * (x + GELU_K * x * x * x))
    return 0.5 * x * (1.0 + th), th


def _block_mm(xb, w_ref, dims):
    rb = D_RNN // RNN_BLOCKS
    return jnp.concatenate(
        [lax.dot_general(xb[:, h * rb:(h + 1) * rb], w_ref[h], dims, preferred_element_type=F32)
         for h in range(RNN_BLOCKS)], axis=1)


def _rglru_gates(ux, prev8, pv_ref, wa_ref, wi_ref):
    shifted = [ux] + [_shift_down(ux, prev8, k) for k in range(1, CONV_WIDTH)]
    xc = pv_ref[4:5, :] + pv_ref[3:4, :] * ux
    for k in range(1, CONV_WIDTH):
        xc = xc + pv_ref[3 - k:4 - k, :] * shifted[k]
    xcb = xc.astype(BF16)
    r_g = _sig(_block_mm(xcb, wa_ref, NN) + pv_ref[5:6, :])
    i_g = _sig(_block_mm(xcb, wi_ref, NN) + pv_ref[6:7, :])
    log_a = LRU_C * r_g * _log_sigmoid(pv_ref[7:8, :])
    a = jnp.exp(log_a)
    mm = jnp.sqrt(_neg_expm1(2.0 * log_a))
    return dict(shifted=shifted, xc=xc, xcb=xcb, r=r_g, i=i_g, a=a, mm=mm)


def _rglru_fwd(ux, ug, pv, wa, wi, t, name):
    lp, d = ux.shape

    def body(ux_ref, ug_ref, pv_ref, wa_ref, wi_ref, y_ref, h_ref, tail_s, hc_s):
        @pl.when(pl.program_id(0) == 0)
        def _():
            tail_s[...] = jnp.zeros_like(tail_s)
            hc_s[...] = jnp.zeros_like(hc_s)

        uxv = ux_ref[...]
        gt = _rglru_gates(uxv, tail_s[...], pv_ref, wa_ref, wi_ref)
        tail_s[...] = ux_ref[t - SUBLANES:t, :]
        cum_a, hloc = _scan_down(gt["a"], gt["mm"] * (gt["i"] * gt["xc"]))
        h_ref[...] = hloc + cum_a * hc_s[0:1, :]
        hc_s[...] = h_ref[t - SUBLANES:t, :]
        hc_s[0:1, :] = h_ref[t - 1:t, :]
        y_ref[...] = (h_ref[...] * _gelu(ug_ref[...])[0]).astype(BF16)

    tile = pl.BlockSpec((t, d), lambda i: (i, 0))
    return pl.pallas_call(
        body, name=name, grid=(lp // t,),
        in_specs=[tile, tile, pl.BlockSpec(pv.shape, lambda i: (0, 0)),
                  pl.BlockSpec(wa.shape, lambda i: (0, 0, 0)), pl.BlockSpec(wi.shape, lambda i: (0, 0, 0))],
        out_specs=[tile, tile],
        out_shape=[jax.ShapeDtypeStruct((lp, d), BF16), jax.ShapeDtypeStruct((lp, d), F32)],
        scratch_shapes=[pltpu.VMEM((SUBLANES, d), F32), pltpu.VMEM((SUBLANES, d), F32)],
        compiler_params=_params(("arbitrary",)),
    )(ux, ug, pv, wa, wi)


def _rglru_bwd(ux, ug, hs, dy, pv, wa, wi, t, name):
    lp, d = ux.shape
    nt = lp // t
    per = t // SUBLANES
    rb = d // RNN_BLOCKS

    def body(ux_ref, uxp_ref, ug_ref, h_ref, hp_ref, dy_ref, pv_ref, wa_ref, wi_ref,
             dux_ref, dug_ref, dpv_ref, dwa_ref, dwi_ref, ca_s, cg_s, cx_s):
        step = pl.program_id(0)
        first_tile = step == nt - 1

        @pl.when(step == 0)
        def _():
            for ref in (ca_s, cg_s, cx_s, dpv_ref, dwa_ref, dwi_ref):
                ref[...] = jnp.zeros_like(ref)

        uxv = ux_ref[...]
        prev8 = jnp.where(first_tile, 0.0, uxp_ref[...])
        hprev8 = jnp.where(first_tile, 0.0, hp_ref[...])
        gt = _rglru_gates(uxv, prev8, pv_ref, wa_ref, wi_ref)
        a, mm, r_g, i_g, xc = gt["a"], gt["mm"], gt["r"], gt["i"], gt["xc"]
        hv = h_ref[...]
        hprev = _shift_down(hv, hprev8, 1)
        ugv, dyv = ug_ref[...], dy_ref[...]
        gel, th = _gelu(ugv)
        dgel = 0.5 * (1.0 + th) + 0.5 * ugv * (1.0 - th * th) * (GELU_C * (1.0 + 3.0 * GELU_K * ugv * ugv))
        dug_ref[...] = (dyv * hv * dgel).astype(BF16)
        a_up = _shift_up(a, ca_s[...], 1)
        cum_a, gloc = _scan_up(a_up, dyv * gel)
        gv = gloc + cum_a * cg_s[0:1, :]
        ca_s[...] = a[0:SUBLANES]
        cg_s[...] = gv[0:SUBLANES]
        ixc = i_g * xc
        d_ixc = gv * mm
        d_log_a = gv * hprev * a - (gv * ixc) * (a * a) / mm
        logsig = _log_sigmoid(pv_ref[7:8, :])
        d_pre_a = d_log_a * (LRU_C * logsig) * r_g * (1.0 - r_g)
        d_pre_i = d_ixc * xc * i_g * (1.0 - i_g)
        dab, dib = d_pre_a.astype(BF16), d_pre_i.astype(BF16)
        d_xc = d_ixc * i_g + _block_mm(dab, wa_ref, NT) + _block_mm(dib, wi_ref, NT)
        xcb = gt["xcb"]
        for h in range(RNN_BLOCKS):
            cols = slice(h * rb, (h + 1) * rb)
            dwa_ref[h] += lax.dot_general(xcb[:, cols], dab[:, cols], TN, preferred_element_type=F32)
            dwi_ref[h] += lax.dot_general(xcb[:, cols], dib[:, cols], TN, preferred_element_type=F32)
        csum = lambda v: jnp.sum(v, axis=0, keepdims=True)
        for k in range(CONV_WIDTH):
            dpv_ref[3 - k:4 - k, :] += csum(d_xc * gt["shifted"][k])
        dpv_ref[4:5, :] += csum(d_xc)
        dpv_ref[5:6, :] += csum(d_pre_a)
        dpv_ref[6:7, :] += csum(d_pre_i)
        dpv_ref[7:8, :] += csum(d_log_a * (LRU_C * r_g)) * _sig(-pv_ref[7:8, :])
        dux = pv_ref[3:4, :] * d_xc
        for k in range(1, CONV_WIDTH):
            dux = dux + pv_ref[3 - k:4 - k, :] * _shift_up(d_xc, cx_s[...], k)
        cx_s[...] = d_xc[0:SUBLANES]
        dux_ref[...] = dux.astype(BF16)

    rev = lambda i: (nt - 1 - i, 0)
    before = lambda i: (jnp.maximum((nt - 1 - i) * per - 1, 0), 0)
    tile = pl.BlockSpec((t, d), rev)
    tail = pl.BlockSpec((SUBLANES, d), before)
    fixed2 = lambda arr: pl.BlockSpec(arr.shape, lambda i: (0, 0))
    fixed3 = lambda arr: pl.BlockSpec(arr.shape, lambda i: (0, 0, 0))
    return pl.pallas_call(
        body, name=name, grid=(nt,),
        in_specs=[tile, tail, tile, tile, tail, tile, fixed2(pv), fixed3(wa), fixed3(wi)],
        out_specs=[tile, tile, fixed2(pv), fixed3(wa), fixed3(wi)],
        out_shape=[jax.ShapeDtypeStruct((lp, d), BF16), jax.ShapeDtypeStruct((lp, d), BF16),
                   jax.ShapeDtypeStruct(pv.shape, F32), jax.ShapeDtypeStruct(wa.shape, F32),
                   jax.ShapeDtypeStruct(wi.shape, F32)],
        scratch_shapes=[pltpu.VMEM((SUBLANES, d), F32)] * 3,
        compiler_params=_params(("arbitrary",)),
    )(ux, ux, ug, hs, hs, dy, pv, wa, wi)


def _adamw(w, m, v, parts, name):
    rows, cols = w.shape
    tr = _pick(rows, 256, SUBLANES)
    c1 = 1.0 / (1.0 - ADAM_B1 ** ADAM_STEP)
    c2 = 1.0 / (1.0 - ADAM_B2 ** ADAM_STEP)

    def body(w_ref, m_ref, v_ref, p_ref, g_ref, d_ref, mo_ref, vo_ref):
        g = p_ref[0].astype(F32)
        for q in range(1, N_DEV):
            g = g + p_ref[q].astype(F32)
        mn = ADAM_B1 * m_ref[...] + (1.0 - ADAM_B1) * g
        vn = ADAM_B2 * v_ref[...] + (1.0 - ADAM_B2) * (g * g)
        g_ref[...] = g
        mo_ref[...] = mn
        vo_ref[...] = vn
        d_ref[...] = -ADAM_LR * ((mn * c1) / (jnp.sqrt(vn * c2) + ADAM_EPS) + ADAM_WD * w_ref[...])

    blk = pl.BlockSpec((tr, cols), lambda i: (i, 0))
    return pl.pallas_call(
        body, name=name, grid=(rows // tr,),
        in_specs=[blk, blk, blk, pl.BlockSpec((N_DEV, tr, cols), lambda i: (0, i, 0))],
        out_specs=[blk] * 4, out_shape=[jax.ShapeDtypeStruct((rows, cols), F32)] * 4,
        compiler_params=_params(("parallel",)),
    )(w, m, v, parts)


WEIGHTS = ("meta_tokens", "norm_mix_g", "w_in", "b_gate", "conv_w", "conv_b", "w_rec_a", "b_rec_a", "w_rec_i",
           "b_rec_i", "lru_lambda", "q_norm_g", "w_uq", "kv_norm_g", "w_ukv", "w_branch", "w_out", "norm_ffn_g",
           "w_ffn_in", "w_ffn_out", "final_norm_g")
SHARDED = {"meta_tokens": True, "w_in": True, "b_gate": True, "conv_w": True, "w_uq": True, "w_ukv": True,
           "w_branch": False, "w_out": False, "w_ffn_in": True, "w_ffn_out": False}
REPLICATED = tuple(n for n in WEIGHTS if n not in SHARDED)


def _as2d(a):
    return a.reshape(-1, a.shape[-1])


def _full_from_gathered(g, by_cols):
    if by_cols:
        return jnp.transpose(g, (1, 0, 2)).reshape(g.shape[1], N_DEV * g.shape[2])
    return g.reshape(N_DEV * g.shape[1], g.shape[2])


def _blocks_from_full(full, by_cols):
    if by_cols:
        r, c = full.shape
        return jnp.transpose(full.reshape(r, N_DEV, c // N_DEV), (1, 0, 2))
    return full.reshape(N_DEV, full.shape[0] // N_DEV, full.shape[1])


def _pack(arrs):
    flat = jnp.concatenate([a.reshape(-1) for a in arrs])
    rows = -(-flat.shape[0] // (LANES * SUBLANES)) * SUBLANES
    return jnp.pad(flat, (0, rows * LANES - flat.shape[0])).reshape(rows, LANES)


def _unpack(packed, like):
    flat = packed.reshape(-1)
    out, off = [], 0
    for a in like:
        out.append(flat[off:off + a.size].reshape(a.shape))
        off += a.size
    return out


def kernel(x, meta_tokens, norm_mix_g, w_in, b_gate, conv_w, conv_b, w_rec_a, b_rec_a, w_rec_i, b_rec_i, lru_lambda, q_norm_g, w_uq, kv_norm_g, w_ukv, w_branch, w_out, norm_ffn_g, w_ffn_in, w_ffn_out, final_norm_g, loss_target, m_meta_tokens, m_norm_mix_g, m_w_in, m_b_gate, m_conv_w, m_conv_b, m_w_rec_a, m_b_rec_a, m_w_rec_i, m_b_rec_i, m_lru_lambda, m_q_norm_g, m_w_uq, m_kv_norm_g, m_w_ukv, m_w_branch, m_w_out, m_norm_ffn_g, m_w_ffn_in, m_w_ffn_out, m_final_norm_g, v_meta_tokens, v_norm_mix_g, v_w_in, v_b_gate, v_conv_w, v_conv_b, v_w_rec_a, v_b_rec_a, v_w_rec_i, v_b_rec_i, v_lru_lambda, v_q_norm_g, v_w_uq, v_kv_norm_g, v_w_ukv, v_w_branch, v_w_out, v_norm_ffn_g, v_w_ffn_in, v_w_ffn_out, v_final_norm_g):
    w = dict(meta_tokens=meta_tokens, norm_mix_g=norm_mix_g, w_in=w_in, b_gate=b_gate, conv_w=conv_w, conv_b=conv_b,
             w_rec_a=w_rec_a, b_rec_a=b_rec_a, w_rec_i=w_rec_i, b_rec_i=b_rec_i, lru_lambda=lru_lambda,
             q_norm_g=q_norm_g, w_uq=w_uq, kv_norm_g=kv_norm_g, w_ukv=w_ukv, w_branch=w_branch, w_out=w_out,
             norm_ffn_g=norm_ffn_g, w_ffn_in=w_ffn_in, w_ffn_out=w_ffn_out, final_norm_g=final_norm_g)
    m = dict(meta_tokens=m_meta_tokens, norm_mix_g=m_norm_mix_g, w_in=m_w_in, b_gate=m_b_gate, conv_w=m_conv_w,
             conv_b=m_conv_b, w_rec_a=m_w_rec_a, b_rec_a=m_b_rec_a, w_rec_i=m_w_rec_i, b_rec_i=m_b_rec_i,
             lru_lambda=m_lru_lambda, q_norm_g=m_q_norm_g, w_uq=m_w_uq, kv_norm_g=m_kv_norm_g, w_ukv=m_w_ukv,
             w_branch=m_w_branch, w_out=m_w_out, norm_ffn_g=m_norm_ffn_g, w_ffn_in=m_w_ffn_in,
             w_ffn_out=m_w_ffn_out, final_norm_g=m_final_norm_g)
    v = dict(meta_tokens=v_meta_tokens, norm_mix_g=v_norm_mix_g, w_in=v_w_in, b_gate=v_b_gate, conv_w=v_conv_w,
             conv_b=v_conv_b, w_rec_a=v_w_rec_a, b_rec_a=v_b_rec_a, w_rec_i=v_w_rec_i, b_rec_i=v_b_rec_i,
             lru_lambda=v_lru_lambda, q_norm_g=v_q_norm_g, w_uq=v_w_uq, kv_norm_g=v_kv_norm_g, w_ukv=v_w_ukv,
             w_branch=v_w_branch, w_out=v_w_out, norm_ffn_g=v_norm_ffn_g, w_ffn_in=v_w_ffn_in,
             w_ffn_out=v_w_ffn_out, final_norm_g=v_final_norm_g)

    seq, d_model = x.shape[1], x.shape[2]
    length = N_META + seq
    lp = -(-length // LANES) * LANES
    t_attn = _pick(lp, 640)
    t_rnn = LANES

    small = ("meta_tokens", "b_gate", "conv_w")
    names = list(SHARDED)
    gathered = _exchange([_as2d(w[n]) if n in small else _as2d(w[n]).astype(BF16) for n in names],
                         [True] * len(names), "gather_weights")
    full = {n: _full_from_gathered(g, SHARDED[n]) for n, g in zip(names, gathered)}

    splits = (D_RNN, D_RNN, Q_RANK, KV_RANK, QK_ROPE, 2 * d_model)
    offs = [0]
    for s in splits:
        offs.append(offs[-1] + s)
    w_x, w_g, w_q, w_kv, w_kr, w_m = (full["w_in"][:, offs[s]:offs[s + 1]] for s in range(6))
    w_kr = jnp.pad(w_kr, ((0, 0), (0, LANES - QK_ROPE)))
    w_uq_pad = jnp.pad(full["w_uq"].reshape(Q_RANK, N_HEADS, QK_NOPE + QK_ROPE),
                       ((0, 0), (0, 0), (0, HEAD_PAD - QK_NOPE - QK_ROPE))).reshape(Q_RANK, N_HEADS * HEAD_PAD)
    w_ukv3 = full["w_ukv"].reshape(KV_RANK, N_HEADS, QK_NOPE + V_DIM)
    w_k_pad = jnp.pad(w_ukv3[:, :, :QK_NOPE], ((0, 0), (0, 0), (0, HEAD_PAD - QK_NOPE))).reshape(
        KV_RANK, N_HEADS * HEAD_PAD)
    w_v = w_ukv3[:, :, QK_NOPE:].reshape(KV_RANK, N_HEADS * V_DIM)
    wb_r, wb_a = full["w_branch"][:D_RNN], full["w_branch"][D_RNN:]
    bg = full["b_gate"].reshape(1, 2 * d_model)
    pv = jnp.concatenate([full["conv_w"], conv_b, b_rec_a, b_rec_i, lru_lambda], axis=0)
    wa_b, wi_b = w_rec_a[0].astype(BF16), w_rec_i[0].astype(BF16)
    g_final = final_norm_g.reshape(1, d_model)

    h0 = jnp.concatenate([full["meta_tokens"], x[0], jnp.zeros((lp - length, d_model), F32)], axis=0)
    tgt = jnp.pad(loss_target[0], ((N_META, lp - length), (0, 0)))
    tabs = _rope_tables(lp)

    z = _rmsnorm_fwd(h0, norm_mix_g, "norm_mix")
    ux = _mm([(z, w_x)], "nn", "in_x")
    ug = _mm([(z, w_g)], "nn", "in_g")
    uq = _mm([(z, w_q)], "nn", "in_q")
    ukv = _mm([(z, w_kv)], "nn", "in_kv")
    ukr = _mm([(z, w_kr)], "nn", "in_kr")
    um = _mm([(z, w_m)], "nn", "in_m")
    y_rnn, hs = _rglru_fwd(ux, ug, pv, wa_b, wi_b, t_rnn, "rglru_fwd")
    qn = _rmsnorm_fwd(uq, q_norm_g, "norm_q")
    kvn = _rmsnorm_fwd(ukv, kv_norm_g, "norm_kv")
    qpad = _mm([(qn, w_uq_pad)], "nn", "up_q")
    kpad = _mm([(kvn, w_k_pad)], "nn", "up_k")
    vh = _mm([(kvn, w_v)], "nn", "up_v", out_dtype=BF16)
    qh, kh = _rope_fwd(qpad, kpad, ukr, tabs, "rope_fwd")
    oh, lse = _attn_fwd(qh, kh, vh, t_attn, "attn_fwd")
    p_rnn = _mm([(y_rnn, wb_r)], "nn", "branch_rnn")
    p_att = _mm([(oh, wb_a)], "nn", "branch_att")
    mixed = _mix_fwd(um, p_rnn, p_att, bg, "mix_fwd")
    h1 = _mm([(mixed, full["w_out"])], "nn", "out_proj", res=h0)
    zf = _rmsnorm_fwd(h1, norm_ffn_g, "norm_ffn")
    gu = _mm([(zf, full["w_ffn_in"])], "nn", "ffn_in")
    act = _swiglu_fwd(gu, "swiglu_fwd")
    h2 = _mm([(act, full["w_ffn_out"])], "nn", "ffn_out", res=h1)
    dh2, dg_final, _, loss_part = _loss_bwd(h2, tgt, g_final, seq, "loss_bwd")

    d_act = _mm([(dh2, full["w_ffn_out"])], "nt", "d_act")
    dw_ffn_out = _mm_tn(act, dh2, "dw_ffn_out")
    d_gu = _swiglu_bwd(gu, d_act, "swiglu_bwd")
    dw_ffn_in = _mm_tn(zf, d_gu, "dw_ffn_in")
    d_zf = _mm([(d_gu, full["w_ffn_in"], D_FF, 0), (d_gu, full["w_ffn_in"], D_FF, 1)], "nt", "d_zf")
    dh1, dg_ffn = _rmsnorm_bwd(h1, d_zf, norm_ffn_g, "norm_ffn_bwd", res=dh2)
    d_mixed = _mm([(dh1, full["w_out"])], "nt", "d_mixed")
    dw_out = _mm_tn(mixed, dh1, "dw_out")
    d_prnn, d_patt, d_um, dbg = _mix_bwd(um, p_rnn, p_att, d_mixed, bg, "mix_bwd")
    d_yrnn = _mm([(d_prnn, wb_r)], "nt", "d_yrnn")
    d_oh = _mm([(d_patt, wb_a)], "nt", "d_oh", out_dtype=BF16)
    dwb_r = _mm_tn(y_rnn, d_prnn, "dw_branch_rnn")
    dwb_a = _mm_tn(oh, d_patt, "dw_branch_att")
    dqh, l2row, dlrow = _attn_bwd_dq(qh, kh, vh, d_oh, oh, lse, t_attn, "attn_bwd_dq")
    dkh, dvh = _attn_bwd_dkv(qh, kh, vh, d_oh, l2row, dlrow, t_attn, "attn_bwd_dkv")
    dqpad, dkpad, dukr = _rope_bwd(dqh, dkh, tabs, "rope_bwd")
    d_qn = _mm([(dqpad, w_uq_pad)], "nt", "d_qn")
    dw_uq_pad = _mm_tn(qn, dqpad, "dw_uq")
    d_kvn = _mm([(dkpad, w_k_pad), (dvh, w_v)], "nt", "d_kvn")
    dw_k_pad = _mm_tn(kvn, dkpad, "dw_uk")
    dw_v = _mm_tn(kvn, dvh, "dw_uv")
    duq, dg_q = _rmsnorm_bwd(uq, d_qn, q_norm_g, "norm_q_bwd", out_dtype=BF16)
    dukv, dg_kv = _rmsnorm_bwd(ukv, d_kvn, kv_norm_g, "norm_kv_bwd", out_dtype=BF16)
    dux, dug, dpv, dwa, dwi = _rglru_bwd(ux, ug, hs, d_yrnn, pv, wa_b, wi_b, t_rnn, "rglru_bwd")
    d_z = _mm([(dux, w_x), (dug, w_g), (duq, w_q), (dukv, w_kv), (dukr, w_kr), (d_um, w_m)], "nt", "d_z")
    dw_in = jnp.concatenate([
        _mm_tn(z, dux, "dw_in_x"), _mm_tn(z, dug, "dw_in_g"), _mm_tn(z, duq, "dw_in_q"),
        _mm_tn(z, dukv, "dw_in_kv"), _mm_tn(z, dukr, "dw_in_kr")[:, :QK_ROPE], _mm_tn(z, d_um, "dw_in_m")], axis=1)
    dh0, dg_mix = _rmsnorm_bwd(h0, d_z, norm_mix_g, "norm_mix_bwd", res=dh1)

    dw_uq = dw_uq_pad.reshape(Q_RANK, N_HEADS, HEAD_PAD)[:, :, :QK_NOPE + QK_ROPE].reshape(Q_RANK, -1)
    dw_ukv = jnp.concatenate([dw_k_pad.reshape(KV_RANK, N_HEADS, HEAD_PAD)[:, :, :QK_NOPE],
                              dw_v.reshape(KV_RANK, N_HEADS, V_DIM)], axis=2).reshape(KV_RANK, -1)
    grad_full = dict(
        meta_tokens=dh0[:N_META], w_in=dw_in, b_gate=dbg.reshape(2, d_model), conv_w=dpv[:CONV_WIDTH],
        w_uq=dw_uq, w_ukv=dw_ukv, w_branch=jnp.concatenate([dwb_r, dwb_a], axis=0), w_out=dw_out,
        w_ffn_in=dw_ffn_in, w_ffn_out=dw_ffn_out)
    grad_rep = dict(
        norm_mix_g=dg_mix, conv_b=dpv[4:5], w_rec_a=dwa, b_rec_a=dpv[5:6], w_rec_i=dwi, b_rec_i=dpv[6:7],
        lru_lambda=dpv[7:8], q_norm_g=dg_q, kv_norm_g=dg_kv, norm_ffn_g=dg_ffn, final_norm_g=dg_final)

    to_send = [_blocks_from_full(grad_full[n], SHARDED[n]).astype(F32 if n in small else BF16) for n in names]
    rep_pack = _pack([grad_rep[n] for n in REPLICATED])
    *recv, recv_rep = _exchange(to_send + [rep_pack], [False] * len(names) + [True], "exchange_grads")

    grads, deltas, new_m, new_v = {}, {}, {}, {}
    for n, parts in zip(names, recv):
        g2, d2, m2, v2 = _adamw(_as2d(w[n]), _as2d(m[n]), _as2d(v[n]), parts, "adamw_" + n)
        for store, val in ((grads, g2), (deltas, d2), (new_m, m2), (new_v, v2)):
            store[n] = val.reshape(w[n].shape)
    rep_like = [w[n] for n in REPLICATED]
    outs = _adamw(_pack(rep_like), _pack([m[n] for n in REPLICATED]), _pack([v[n] for n in REPLICATED]),
                  recv_rep, "adamw_replicated")
    for store, val in zip((grads, deltas, new_m, new_v), outs):
        for n, a in zip(REPLICATED, _unpack(val, rep_like)):
            store[n] = a

    loss = lax.psum(loss_part[0, 0], MESH_AXES)
    grad_x = dh0[N_META:length][None]
    return (loss, grad_x, *[grads[n] for n in WEIGHTS], *[deltas[n] for n in WEIGHTS],
            *[new_m[n] for n in WEIGHTS], *[new_v[n] for n in WEIGHTS])
```

```python
import functools
import math

import jax
import jax.numpy as jnp
from jax import lax
from jax.experimental import pallas as pl
from jax.experimental.pallas import tpu as pltpu

F32 = jnp.float32
BF16 = jnp.bfloat16

N_DEV = 8
MESH_AXES = ("x", "y", "c")
LANES = 128
SUBLANES = 8
VMEM_LIMIT = 56 * 1024 * 1024

N_META = 16
CHUNK_SHIFT = 6
CHUNK_BIAS = 64 - N_META
EPS = 1e-6
D_RNN = 1280
RNN_BLOCKS = 10
CONV_WIDTH = 4
LRU_C = 8.0
N_HEADS = 8
QK_NOPE = 128
QK_ROPE = 64
V_DIM = 128
HEAD_PAD = 256
Q_RANK = 384
KV_RANK = 256
ROPE_THETA = 10000.0
ATTN_SCALE = 1.0 / math.sqrt(QK_NOPE + QK_ROPE)
NEG = -1e30
LOG2E = 1.0 / math.log(2.0)
SCALE_LOG2E = ATTN_SCALE * LOG2E
Q_SPLIT = 2
Q_ALIGN = LANES // Q_SPLIT
ROW_GROUP = 32
SPILL = LANES
D_FF = 2816

ADAM_LR = 0.001
ADAM_B1 = 0.9
ADAM_B2 = 0.999
ADAM_EPS = 1e-08
ADAM_WD = 0.01
ADAM_STEP = 10

NN = (((1,), (0,)), ((), ()))
NT = (((1,), (1,)), ((), ()))
TN = (((0,), (0,)), ((), ()))


def _pick(n, cap, base=LANES):
    best = None
    for t in range(base, min(n, cap) + 1, base):
        if n % t == 0:
            best = t
    return best if best is not None else n


def _params(sem=None):
    return pltpu.CompilerParams(dimension_semantics=sem, vmem_limit_bytes=VMEM_LIMIT)


def _sig(x):
    return 1.0 / (1.0 + jnp.exp(-x))


def _exchange(srcs, gather, name):
    n = len(srcs)
    out_shape = [jax.ShapeDtypeStruct((N_DEV,) + (s.shape if g else s.shape[1:]), s.dtype)
                 for s, g in zip(srcs, gather)]

    def body(*refs):
        src, dst = refs[:n], refs[n:2 * n]
        send_sems, recv_sems, local_sems = refs[2 * n:]
        x, y, c = lax.axis_index("x"), lax.axis_index("y"), lax.axis_index("c")
        me = 4 * x + 2 * y + c
        local = []
        for t in range(n):
            cp = pltpu.make_async_copy(src[t] if gather[t] else src[t].at[me], dst[t].at[me], local_sems.at[t])
            cp.start()
            local.append(cp)
        sends, recvs = [], []
        for k in range(1, N_DEV):
            px = 1 - x if k & 4 else x
            py = 1 - y if k & 2 else y
            pc = 1 - c if k & 1 else c
            peer = 4 * px + 2 * py + pc
            for t in range(n):
                cp = pltpu.make_async_remote_copy(
                    src_ref=src[t] if gather[t] else src[t].at[peer], dst_ref=dst[t].at[me],
                    send_sem=send_sems.at[t, k - 1], recv_sem=recv_sems.at[t, k - 1],
                    device_id=(px, py, pc), device_id_type=pl.DeviceIdType.MESH)
                cp.start()
                sends.append(cp)
                recvs.append(pltpu.make_async_remote_copy(
                    src_ref=src[t] if gather[t] else src[t].at[peer], dst_ref=dst[t].at[peer],
                    send_sem=send_sems.at[t, k - 1], recv_sem=recv_sems.at[t, k - 1],
                    device_id=(px, py, pc), device_id_type=pl.DeviceIdType.MESH))
        for cp in recvs:
            cp.wait_recv()
        for cp in sends:
            cp.wait_send()
        for cp in local:
            cp.wait()

    any_spec = pl.BlockSpec(memory_space=pl.ANY)
    return pl.pallas_call(
        body, name=name, out_shape=out_shape,
        in_specs=[any_spec] * n, out_specs=[any_spec] * n,
        scratch_shapes=[pltpu.SemaphoreType.DMA((n, N_DEV - 1)), pltpu.SemaphoreType.DMA((n, N_DEV - 1)),
                        pltpu.SemaphoreType.DMA((n,))],
    )(*srcs)


HBM_SPEC = pl.BlockSpec(memory_space=pltpu.HBM)
SEM_SPEC = pl.BlockSpec(memory_space=pltpu.SEMAPHORE)
DATAFLOW = pltpu.SideEffectType.DATAFLOW_SIDE_EFFECTING


def _peers(x, y, c):
    out = []
    for k in range(1, N_DEV):
        px = 1 - x if k & 4 else x
        py = 1 - y if k & 2 else y
        pc = 1 - c if k & 1 else c
        out.append((k, (px, py, pc), 4 * px + 2 * py + pc))
    return out


def _split_copies(src, land, gather, send_sems, recv_sems, local_sems):
    x, y, c = lax.axis_index("x"), lax.axis_index("y"), lax.axis_index("c")
    me = 4 * x + 2 * y + c
    n = len(src)
    local = [pltpu.make_async_copy(src[t] if gather[t] else src[t].at[me], land[t].at[me], local_sems.at[t])
             for t in range(n)]
    sends, recvs = [], []
    for k, pos, peer in _peers(x, y, c):
        for t in range(n):
            mine = src[t] if gather[t] else src[t].at[peer]
            slot = t * (N_DEV - 1) + k - 1
            common = dict(send_sem=send_sems.at[slot], recv_sem=recv_sems.at[slot], device_id=pos,
                          device_id_type=pl.DeviceIdType.MESH)
            sends.append(pltpu.make_async_remote_copy(src_ref=mine, dst_ref=land[t].at[me], **common))
            recvs.append(pltpu.make_async_remote_copy(src_ref=mine, dst_ref=land[t].at[peer], **common))
    return local, sends, recvs


def _exchange_start(srcs, gather, after, name):
    n = len(srcs)
    lands = [lax.empty((N_DEV,) + (s.shape if g else s.shape[1:]), s.dtype) for s, g in zip(srcs, gather)]

    def body(*refs):
        src, land = refs[:n], refs[n:2 * n]
        send_sems, recv_sems, local_sems = refs[2 * n + 1:2 * n + 4]
        local, sends, _ = _split_copies(src, land, gather, send_sems, recv_sems, local_sems)
        for cp in local + sends:
            cp.start()
        refs[-1][...] = jnp.zeros_like(refs[-1])

    hbm = lambda a: pltpu.HBM(a.shape, a.dtype)
    outs = pl.pallas_call(
        body, name=name,
        out_shape=(pltpu.SemaphoreType.DMA((n * (N_DEV - 1),)), pltpu.SemaphoreType.DMA((n * (N_DEV - 1),)),
                   pltpu.SemaphoreType.DMA((n,)), *[hbm(s) for s in srcs], *[hbm(a) for a in lands],
                   jax.ShapeDtypeStruct((SUBLANES, LANES), F32)),
        in_specs=[HBM_SPEC] * (2 * n) + [pl.BlockSpec(memory_space=pl.ANY)],
        out_specs=(SEM_SPEC, SEM_SPEC, SEM_SPEC, *[HBM_SPEC] * (2 * n), pl.BlockSpec(memory_space=pltpu.VMEM)),
        input_output_aliases={t: 3 + t for t in range(2 * n)},
        compiler_params=pltpu.CompilerParams(has_side_effects=DATAFLOW),
    )(*[pltpu.with_memory_space_constraint(a, pltpu.HBM) for a in list(srcs) + lands], after)
    return dict(sems=outs[:3], srcs=outs[3:3 + n], lands=outs[3 + n:3 + 2 * n], token=outs[-1], gather=gather)


def _exchange_wait(handle, after, name):
    srcs, lands, gather = handle["srcs"], handle["lands"], handle["gather"]
    n = len(srcs)

    def body(*refs):
        src, land = refs[:n], refs[n:2 * n]
        send_sems, recv_sems, local_sems = refs[2 * n:2 * n + 3]
        local, sends, recvs = _split_copies(src, land, gather, send_sems, recv_sems, local_sems)
        for cp in sends:
            cp.wait_send()
        for cp in recvs:
            cp.wait_recv()
        for cp in local:
            cp.wait()

    hbm = lambda a: pltpu.HBM(a.shape, a.dtype)
    outs = pl.pallas_call(
        body, name=name, out_shape=(*[hbm(s) for s in srcs], *[hbm(a) for a in lands]),
        in_specs=[HBM_SPEC] * (2 * n) + [SEM_SPEC] * 3 + [pl.BlockSpec(memory_space=pl.ANY)],
        out_specs=[HBM_SPEC] * (2 * n), input_output_aliases={t: t for t in range(2 * n)},
        compiler_params=pltpu.CompilerParams(has_side_effects=DATAFLOW),
    )(*srcs, *lands, *handle["sems"], after)
    return outs[n:]


def _mm(pairs, mode, name, res=None, out_dtype=F32):
    pairs = [p if len(p) == 4 else (p[0], p[1], p[0].shape[1], 0) for p in pairs]
    m = pairs[0][0].shape[0]
    n = pairs[0][1].shape[1] if mode == "nn" else pairs[0][1].shape[0]
    tm, tn = _pick(m, 640), _pick(n, 1408)
    np_ = len(pairs)
    dims = NN if mode == "nn" else NT

    def body(*refs):
        acc = None
        for s in range(np_):
            d = lax.dot_general(refs[2 * s][...].astype(BF16), refs[2 * s + 1][...].astype(BF16), dims,
                                preferred_element_type=F32)
            acc = d if acc is None else acc + d
        if res is not None:
            acc = acc + refs[2 * np_][...]
        refs[-1][...] = acc.astype(out_dtype)

    in_specs, args = [], []
    for a, b, kt, kb in pairs:
        in_specs.append(pl.BlockSpec((tm, kt), lambda i, j, kb=kb: (i, kb)))
        if mode == "nn":
            in_specs.append(pl.BlockSpec((kt, tn), lambda i, j, kb=kb: (kb, j)))
        else:
            in_specs.append(pl.BlockSpec((tn, kt), lambda i, j, kb=kb: (j, kb)))
        args += [a, b]
    if res is not None:
        in_specs.append(pl.BlockSpec((tm, tn), lambda i, j: (i, j)))
        args.append(res)
    return pl.pallas_call(
        body, name=name, grid=(m // tm, n // tn), in_specs=in_specs,
        out_specs=pl.BlockSpec((tm, tn), lambda i, j: (i, j)),
        out_shape=jax.ShapeDtypeStruct((m, n), out_dtype),
        compiler_params=_params(("parallel", "parallel")),
    )(*args)


def _mm_tn(a, b, name):
    m, k = a.shape
    n = b.shape[1]
    tm, tk, tn = _pick(m, 1664), _pick(k, 1408), _pick(n, 1408)

    def body(a_ref, b_ref, o_ref):
        @pl.when(pl.program_id(2) == 0)
        def _():
            o_ref[...] = jnp.zeros_like(o_ref)

        o_ref[...] += lax.dot_general(a_ref[...].astype(BF16), b_ref[...].astype(BF16), TN,
                                      preferred_element_type=F32)

    return pl.pallas_call(
        body, name=name, grid=(k // tk, n // tn, m // tm),
        in_specs=[pl.BlockSpec((tm, tk), lambda i, j, r: (r, i)), pl.BlockSpec((tm, tn), lambda i, j, r: (r, j))],
        out_specs=pl.BlockSpec((tk, tn), lambda i, j, r: (i, j)),
        out_shape=jax.ShapeDtypeStruct((k, n), F32),
        compiler_params=_params(("parallel", "parallel", "arbitrary")),
    )(a, b)


ROW_TILE_BYTES = 6 * 1024 * 1024


def _row_tile(rows, row_in, row_out):
    per_row = sum((r[1] * r[0].dtype.itemsize) if isinstance(r, tuple) else (r.shape[1] * r.dtype.itemsize)
                  for r in row_in)
    per_row += sum(w * jnp.dtype(dt).itemsize for w, dt in row_out)
    return _pick(rows, min(640, max(LANES, ROW_TILE_BYTES // per_row)))


def _rowcall(body, name, rows, row_in, full_in, row_out, acc_out=()):
    tr = _row_tile(rows, row_in, row_out)
    n_steps = rows // tr
    in_specs, args = [], []
    for r in row_in:
        arr, w, cb = r if isinstance(r, tuple) else (r, r.shape[1], 0)
        in_specs.append(pl.BlockSpec((tr, w), lambda i, cb=cb: (i, cb)))
        args.append(arr)
    for f in full_in:
        in_specs.append(pl.BlockSpec(f.shape, lambda i, nd=f.ndim: (0,) * nd))
        args.append(f)
    out_specs = [pl.BlockSpec((tr, w), lambda i: (i, 0)) for w, _ in row_out]
    out_shape = [jax.ShapeDtypeStruct((rows, w), dt) for w, dt in row_out]
    for shp, dt in acc_out:
        out_specs.append(pl.BlockSpec(shp, lambda i, nd=len(shp): (0,) * nd))
        out_shape.append(jax.ShapeDtypeStruct(shp, dt))

    def wrapped(*refs):
        body(pl.program_id(0), n_steps, *refs)

    return pl.pallas_call(
        wrapped, name=name, grid=(n_steps,), in_specs=in_specs, out_specs=out_specs, out_shape=out_shape,
        compiler_params=_params(("arbitrary",) if acc_out else ("parallel",)),
    )(*args)


def _rmsnorm_fwd(x, g, name):
    rows, w = x.shape

    def body(i, n, x_ref, g_ref, o_ref):
        xv = x_ref[...]
        r = lax.rsqrt(jnp.mean(xv * xv, axis=-1, keepdims=True) + EPS)
        o_ref[...] = (xv * r * g_ref[...]).astype(BF16)

    return _rowcall(body, name, rows,[x], [g], [(w, BF16)])[0]


def _rmsnorm_bwd_math(xv, dy, g):
    w = xv.shape[-1]
    r = lax.rsqrt(jnp.mean(xv * xv, axis=-1, keepdims=True) + EPS)
    t = dy * g
    dx = r * t - xv * (r * r * r * (jnp.sum(t * xv, axis=-1, keepdims=True) / w))
    dg = jnp.sum(dy * xv * r, axis=0, keepdims=True)
    return dx, dg


def _rmsnorm_bwd(x, dy, g, name, res=None, out_dtype=F32):
    rows, w = x.shape

    def body(i, n, *refs):
        x_ref, dy_ref = refs[0], refs[1]
        g_ref, dx_ref, dg_ref = refs[-3], refs[-2], refs[-1]
        dx, dg = _rmsnorm_bwd_math(x_ref[...], dy_ref[...], g_ref[...])
        if res is not None:
            dx = dx + refs[2][...]
        dx_ref[...] = dx.astype(out_dtype)

        @pl.when(i == 0)
        def _():
            dg_ref[...] = jnp.zeros_like(dg_ref)

        dg_ref[...] += dg

    row_in = [x, dy] + ([res] if res is not None else [])
    return _rowcall(body, name, rows,row_in, [g], [(w, out_dtype)], [((1, w), F32)])


def _loss_bwd(h2, tgt, g, seq, name):
    rows, w = h2.shape
    tr = _row_tile(rows, [h2, tgt], [(w, F32)])

    def body(i, n, h_ref, t_ref, g_ref, dh_ref, dg_ref, lcol_ref, loss_ref):
        hv, gv = h_ref[...], g_ref[...]
        row = i * tr + lax.broadcasted_iota(jnp.int32, (tr, w), 0)
        valid = jnp.logical_and(row >= N_META, row < N_META + seq)
        r = lax.rsqrt(jnp.mean(hv * hv, axis=-1, keepdims=True) + EPS)
        err = jnp.where(valid, hv * r * gv - t_ref[...], 0.0)
        dx, dg = _rmsnorm_bwd_math(hv, err * (1.0 / w), gv)
        dh_ref[...] = dx

        @pl.when(i == 0)
        def _():
            dg_ref[...] = jnp.zeros_like(dg_ref)
            lcol_ref[...] = jnp.zeros_like(lcol_ref)

        dg_ref[...] += dg
        lcol_ref[...] += jnp.sum(err * err, axis=0, keepdims=True)

        @pl.when(i == n - 1)
        def _():
            total = jnp.sum(lcol_ref[...], axis=1, keepdims=True) * (0.5 / w)
            loss_ref[...] = jnp.broadcast_to(total, loss_ref.shape)

    return _rowcall(body, name, rows, [h2, tgt], [g], [(w, F32)],
                    [((1, w), F32), ((1, w), F32), ((1, LANES), F32)])


def _mix_fwd(um, p_rnn, p_att, bg, name):
    rows, d = p_rnn.shape

    def body(i, n, u0_ref, u1_ref, pr_ref, pa_ref, bg_ref, o_ref):
        g0 = _sig(u0_ref[...] + bg_ref[:, :d])
        g1 = _sig(u1_ref[...] + bg_ref[:, d:])
        o_ref[...] = (g0 * pr_ref[...] + g1 * pa_ref[...]).astype(BF16)

    return _rowcall(body, name, rows,[(um, d, 0), (um, d, 1), p_rnn, p_att], [bg],
                    [(d, BF16)])[0]


def _mix_bwd(um, p_rnn, p_att, dmix, bg, name):
    rows, d = p_rnn.shape

    def body(i, n, u0_ref, u1_ref, pr_ref, pa_ref, dm_ref, bg_ref, dpr_ref, dpa_ref, dum_ref, dbg_ref):
        g0 = _sig(u0_ref[...] + bg_ref[:, :d])
        g1 = _sig(u1_ref[...] + bg_ref[:, d:])
        dm = dm_ref[...]
        dpr_ref[...] = (dm * g0).astype(BF16)
        dpa_ref[...] = (dm * g1).astype(BF16)
        du0 = dm * pr_ref[...] * g0 * (1.0 - g0)
        du1 = dm * pa_ref[...] * g1 * (1.0 - g1)
        dum_ref[:, :d] = du0.astype(BF16)
        dum_ref[:, d:] = du1.astype(BF16)

        @pl.when(i == 0)
        def _():
            dbg_ref[...] = jnp.zeros_like(dbg_ref)

        dbg_ref[:, :d] += jnp.sum(du0, axis=0, keepdims=True)
        dbg_ref[:, d:] += jnp.sum(du1, axis=0, keepdims=True)

    return _rowcall(body, name, rows,[(um, d, 0), (um, d, 1), p_rnn, p_att, dmix], [bg],
                    [(d, BF16), (d, BF16), (2 * d, BF16)], [((1, 2 * d), F32)])


def _swiglu_fwd(gu, name):
    rows, w2 = gu.shape
    f = w2 // 2

    def body(i, n, g_ref, u_ref, o_ref):
        gate = g_ref[...]
        o_ref[...] = (gate * _sig(gate) * u_ref[...]).astype(BF16)

    return _rowcall(body, name, rows,[(gu, f, 0), (gu, f, 1)], [], [(f, BF16)])[0]


def _swiglu_bwd(gu, dact, name):
    rows, w2 = gu.shape
    f = w2 // 2

    def body(i, n, g_ref, u_ref, da_ref, o_ref):
        gate, da = g_ref[...], da_ref[...]
        sg = _sig(gate)
        o_ref[:, :f] = (da * u_ref[...] * (sg * (1.0 + gate * (1.0 - sg)))).astype(BF16)
        o_ref[:, f:] = (da * gate * sg).astype(BF16)

    return _rowcall(body, name, rows,[(gu, f, 0), (gu, f, 1), dact], [], [(w2, BF16)])[0]


def _rope_tables(lp):
    idx = jnp.arange(lp, dtype=jnp.int32).astype(F32)
    inv_freq = ROPE_THETA ** (-jnp.arange(0, QK_ROPE, 2, dtype=F32) / QK_ROPE)
    ang = idx[:, None] * inv_freq[None, :]
    cos, sin = jnp.cos(ang), jnp.sin(ang)
    half = QK_ROPE // 2
    z = lambda wdt: jnp.zeros((lp, wdt), F32)
    tc = jnp.concatenate([cos, cos, z(LANES - 2 * half)], axis=1)
    ts1 = jnp.concatenate([-sin, z(LANES - half)], axis=1)
    ts2 = jnp.concatenate([z(half), sin, z(LANES - 2 * half)], axis=1)
    return tc, ts1, ts2


def _rope(xv, tc, ts1, ts2):
    half = QK_ROPE // 2
    return xv * tc + pltpu.roll(xv, LANES - half, 1) * ts1 + pltpu.roll(xv, half, 1) * ts2


def _rope_t(dv, tc, ts1, ts2):
    half = QK_ROPE // 2
    return dv * tc + pltpu.roll(dv * ts1, half, 1) + pltpu.roll(dv * ts2, LANES - half, 1)


def _rope_fwd(qpad, kpad, ukr, tabs, name):
    rows, w = qpad.shape

    def body(i, n, q_ref, k_ref, r_ref, c_ref, s1_ref, s2_ref, qo_ref, ko_ref):
        tc, ts1, ts2 = c_ref[...], s1_ref[...], s2_ref[...]
        kr = _rope(r_ref[...], tc, ts1, ts2).astype(BF16)
        for h in range(N_HEADS):
            lo, mid, hi = h * HEAD_PAD, h * HEAD_PAD + QK_NOPE, (h + 1) * HEAD_PAD
            qo_ref[:, lo:mid] = q_ref[:, lo:mid].astype(BF16)
            qo_ref[:, mid:hi] = _rope(q_ref[:, mid:hi], tc, ts1, ts2).astype(BF16)
            ko_ref[:, lo:mid] = k_ref[:, lo:mid].astype(BF16)
            ko_ref[:, mid:hi] = kr

    return _rowcall(body, name, rows,[qpad, kpad, ukr, *tabs], [], [(w, BF16), (w, BF16)])


def _rope_bwd(dq, dk, tabs, name):
    rows, w = dq.shape

    def body(i, n, dq_ref, dk_ref, c_ref, s1_ref, s2_ref, qo_ref, ko_ref, ro_ref):
        tc, ts1, ts2 = c_ref[...], s1_ref[...], s2_ref[...]
        dkr = None
        for h in range(N_HEADS):
            lo, mid, hi = h * HEAD_PAD, h * HEAD_PAD + QK_NOPE, (h + 1) * HEAD_PAD
            qo_ref[:, lo:mid] = dq_ref[:, lo:mid].astype(BF16)
            qo_ref[:, mid:hi] = _rope_t(dq_ref[:, mid:hi], tc, ts1, ts2).astype(BF16)
            ko_ref[:, lo:mid] = dk_ref[:, lo:mid].astype(BF16)
            ko_ref[:, mid:hi] = jnp.zeros((ko_ref.shape[0], hi - mid), BF16)
            part = dk_ref[:, mid:hi]
            dkr = part if dkr is None else dkr + part
        ro_ref[...] = _rope_t(dkr, tc, ts1, ts2).astype(BF16)

    return _rowcall(body, name, rows,[dq, dk, *tabs], [],
                    [(w, BF16), (w, BF16), (LANES, BF16)])


def _visible(q0, k0, tq, tk):
    qrow = q0 + lax.broadcasted_iota(jnp.int32, (tq, tk), 0)
    kcol = k0 + lax.broadcasted_iota(jnp.int32, (tq, tk), 1)
    return ((kcol + CHUNK_BIAS) >> CHUNK_SHIFT) <= ((qrow + CHUNK_BIAS) >> CHUNK_SHIFT)


def _visible_t(k0, q0, tk, tq):
    krow = k0 + lax.broadcasted_iota(jnp.int32, (tk, tq), 0)
    qcol = q0 + lax.broadcasted_iota(jnp.int32, (tk, tq), 1)
    return ((krow + CHUNK_BIAS) >> CHUNK_SHIFT) <= ((qcol + CHUNK_BIAS) >> CHUNK_SHIFT)


def _lanes(v, width):
    return jnp.tile(v, (1, width // LANES))


def _pipelined_chunks(n_full, scores, absorb):
    scores(0, 0)

    def pair(jj, carry):
        a = 2 * jj
        scores(a + 1, 1)
        absorb(a, 0, False)
        scores(a + 2, 0)
        absorb(a + 1, 1, False)
        return carry

    lax.fori_loop(0, n_full // 2, pair, 0)

    @pl.when(n_full % 2 == 0)
    def _():
        absorb(n_full, 0, True)

    @pl.when(n_full % 2 == 1)
    def _():
        scores(n_full, 1)
        absorb(n_full - 1, 0, False)
        absorb(n_full, 1, True)


def _attn_fwd(q, k, v, t, name):
    lp = q.shape[0]
    nt = lp // t

    def body(q_ref, k_ref, v_ref, o_ref, lse_ref, m_s, l_s, acc_s, a_s, sa_s, sb_s, p_s):
        i = pl.program_id(1)
        m_s[...] = jnp.full(m_s.shape, NEG, F32)
        l_s[...] = jnp.zeros(l_s.shape, F32)
        acc_s[...] = jnp.zeros(acc_s.shape, F32)

        s_bufs = (sa_s, sb_s)

        def scores(j, slot):
            r0 = pl.multiple_of(j * t, t)
            s_bufs[slot][...] = lax.dot_general(q_ref[...], k_ref[pl.ds(r0, t), :], NT,
                                                preferred_element_type=F32)

        def absorb(j, slot, masked):
            for r in range(0, t, ROW_GROUP):
                rows = slice(r, r + ROW_GROUP)
                s = s_bufs[slot][rows, :]
                if masked:
                    s = jnp.where(_visible(i * t + r, i * t, ROW_GROUP, t), s, NEG)
                m_prev = m_s[rows, :]
                m_new = jnp.maximum(m_prev, jnp.max(s, axis=1, keepdims=True))
                alpha = jnp.exp2((m_prev - m_new) * SCALE_LOG2E)
                p = jnp.exp2((s - _lanes(m_new, t)) * SCALE_LOG2E)
                l_s[rows, :] = alpha * l_s[rows, :] + jnp.sum(p, axis=1, keepdims=True)
                m_s[rows, :] = m_new
                a_s[rows, :] = alpha
                p_s[rows, :] = p.astype(BF16)
            r0 = pl.multiple_of(j * t, t)
            acc_s[...] = a_s[...] * acc_s[...] + jnp.dot(p_s[...], v_ref[pl.ds(r0, t), :],
                                                         preferred_element_type=F32)

        @pl.when(i + 1 < nt)
        def _():
            r1 = pl.multiple_of((i + 1) * t, t)
            rows = slice(t - SPILL, t)
            s = lax.dot_general(q_ref[rows, :], k_ref[pl.ds(r1, SPILL), :], NT, preferred_element_type=F32)
            s = jnp.where(_visible(i * t + t - SPILL, (i + 1) * t, SPILL, SPILL), s, NEG)
            m_prev = m_s[rows, :]
            m_new = jnp.maximum(m_prev, jnp.max(s, axis=1, keepdims=True))
            alpha = jnp.exp2((m_prev - m_new) * SCALE_LOG2E)
            p = jnp.exp2((s - m_new) * SCALE_LOG2E)
            l_s[rows, :] = alpha * l_s[rows, :] + jnp.sum(p, axis=1, keepdims=True)
            acc_s[rows, :] = alpha * acc_s[rows, :] + jnp.dot(p.astype(BF16), v_ref[pl.ds(r1, SPILL), :],
                                                              preferred_element_type=F32)
            m_s[rows, :] = m_new

        _pipelined_chunks(i, scores, absorb)
        o_ref[...] = (acc_s[...] / l_s[...]).astype(BF16)
        lse_ref[...] = m_s[...] * ATTN_SCALE + jnp.log(l_s[...])

    return pl.pallas_call(
        body, name=name, grid=(N_HEADS, nt),
        in_specs=[pl.BlockSpec((t, HEAD_PAD), lambda h, i: (i, h)),
                  pl.BlockSpec((lp, HEAD_PAD), lambda h, i: (0, h)),
                  pl.BlockSpec((lp, V_DIM), lambda h, i: (0, h))],
        out_specs=[pl.BlockSpec((t, V_DIM), lambda h, i: (i, h)),
                   pl.BlockSpec((None, t, LANES), lambda h, i: (h, i, 0))],
        out_shape=[jax.ShapeDtypeStruct((lp, N_HEADS * V_DIM), BF16),
                   jax.ShapeDtypeStruct((N_HEADS, lp, LANES), F32)],
        scratch_shapes=[pltpu.VMEM((t, LANES), F32), pltpu.VMEM((t, LANES), F32), pltpu.VMEM((t, V_DIM), F32),
                        pltpu.VMEM((t, LANES), F32), pltpu.VMEM((t, t), F32), pltpu.VMEM((t, t), F32),
                        pltpu.VMEM((t, t), BF16)],
        compiler_params=_params(("parallel", "arbitrary")),
    )(q, k, v)


def _attn_bwd_dq(q, k, v, do, o, lse, t, name):
    lp = q.shape[0]
    nt = lp // t

    def body(q_ref, k_ref, v_ref, do_ref, o_ref, lse_ref, dq_ref, l2row_ref, dlrow_ref,
             acc_s, l2_s, dl_s, sa_s, sb_s, da_s, db_s, ds_s):
        i = pl.program_id(1)
        delta = jnp.sum(do_ref[...].astype(F32) * o_ref[...].astype(F32), axis=1, keepdims=True)
        dl_s[...] = jnp.broadcast_to(delta, dl_s.shape)
        l2_s[...] = lse_ref[...] * LOG2E
        l2row_ref[...] = l2_s[...].T[0:SUBLANES, :]
        dlrow_ref[...] = dl_s[...].T[0:SUBLANES, :]
        acc_s[...] = jnp.zeros(acc_s.shape, F32)
        s_bufs, d_bufs = (sa_s, sb_s), (da_s, db_s)

        def dscores(s, dp, rows, width):
            p = jnp.exp2(s * SCALE_LOG2E - _lanes(l2_s[rows, :], width))
            return (p * (dp - _lanes(dl_s[rows, :], width))).astype(BF16)

        @pl.when(i + 1 < nt)
        def _():
            r1 = pl.multiple_of((i + 1) * t, t)
            rows = slice(t - SPILL, t)
            ks, vs = k_ref[pl.ds(r1, SPILL), :], v_ref[pl.ds(r1, SPILL), :]
            s = lax.dot_general(q_ref[rows, :], ks, NT, preferred_element_type=F32)
            s = jnp.where(_visible(i * t + t - SPILL, (i + 1) * t, SPILL, SPILL), s, NEG)
            dp = lax.dot_general(do_ref[rows, :], vs, NT, preferred_element_type=F32)
            acc_s[rows, :] += jnp.dot(dscores(s, dp, rows, SPILL), ks, preferred_element_type=F32)

        def scores(j, slot):
            r0 = pl.multiple_of(j * t, t)
            s_bufs[slot][...] = lax.dot_general(q_ref[...], k_ref[pl.ds(r0, t), :], NT,
                                                preferred_element_type=F32)
            d_bufs[slot][...] = lax.dot_general(do_ref[...], v_ref[pl.ds(r0, t), :], NT,
                                                preferred_element_type=F32)

        def absorb(j, slot, masked):
            for r in range(0, t, ROW_GROUP):
                rows = slice(r, r + ROW_GROUP)
                s = s_bufs[slot][rows, :]
                if masked:
                    s = jnp.where(_visible(i * t + r, i * t, ROW_GROUP, t), s, NEG)
                ds_s[rows, :] = dscores(s, d_bufs[slot][rows, :], rows, t)
            r0 = pl.multiple_of(j * t, t)
            acc_s[...] += jnp.dot(ds_s[...], k_ref[pl.ds(r0, t), :], preferred_element_type=F32)

        _pipelined_chunks(i, scores, absorb)
        dq_ref[...] = acc_s[...] * ATTN_SCALE

    stat_row = pl.BlockSpec((None, None, SUBLANES, t), lambda h, i: (h, i, 0, 0))
    return pl.pallas_call(
        body, name=name, grid=(N_HEADS, nt),
        in_specs=[pl.BlockSpec((t, HEAD_PAD), lambda h, i: (i, h)),
                  pl.BlockSpec((lp, HEAD_PAD), lambda h, i: (0, h)),
                  pl.BlockSpec((lp, V_DIM), lambda h, i: (0, h)),
                  pl.BlockSpec((t, V_DIM), lambda h, i: (i, h)),
                  pl.BlockSpec((t, V_DIM), lambda h, i: (i, h)),
                  pl.BlockSpec((None, t, LANES), lambda h, i: (h, i, 0))],
        out_specs=[pl.BlockSpec((t, HEAD_PAD), lambda h, i: (i, h)), stat_row, stat_row],
        out_shape=[jax.ShapeDtypeStruct((lp, N_HEADS * HEAD_PAD), F32),
                   jax.ShapeDtypeStruct((N_HEADS, nt, SUBLANES, t), F32),
                   jax.ShapeDtypeStruct((N_HEADS, nt, SUBLANES, t), F32)],
        scratch_shapes=[pltpu.VMEM((t, HEAD_PAD), F32), pltpu.VMEM((t, LANES), F32), pltpu.VMEM((t, LANES), F32),
                        pltpu.VMEM((t, t), F32), pltpu.VMEM((t, t), F32), pltpu.VMEM((t, t), F32),
                        pltpu.VMEM((t, t), F32), pltpu.VMEM((t, t), BF16)],
        compiler_params=_params(("parallel", "arbitrary")),
    )(q, k, v, do, o, lse)


def _attn_bwd_dkv(q, k, v, do, l2row, dlrow, t, name):
    lp = q.shape[0]
    nt = lp // t

    def body(q_ref, k_ref, v_ref, do_ref, l2_ref, dl_ref, dk_ref, dv_ref,
             dk_s, dv_s, sa_s, sb_s, da_s, db_s, p_s, ds_s):
        j = pl.program_id(1)
        dk_s[...] = jnp.zeros(dk_s.shape, F32)
        dv_s[...] = jnp.zeros(dv_s.shape, F32)
        s_bufs, d_bufs = (sa_s, sb_s), (da_s, db_s)

        def weights(st, dpt, l2r, dlr):
            pt = jnp.exp2(st * SCALE_LOG2E - l2r)
            return pt.astype(BF16), (pt * (dpt - dlr)).astype(BF16)

        @pl.when(j > 0)
        def _():
            q0 = pl.multiple_of(j * t - SPILL, SPILL)
            rows = slice(0, SPILL)
            qs, dos = q_ref[pl.ds(q0, SPILL), :], do_ref[pl.ds(q0, SPILL), :]
            st = lax.dot_general(k_ref[rows, :], qs, NT, preferred_element_type=F32)
            st = jnp.where(_visible_t(j * t, j * t - SPILL, SPILL, SPILL), st, NEG)
            dpt = lax.dot_general(v_ref[rows, :], dos, NT, preferred_element_type=F32)
            pt, dst = weights(st, dpt, l2_ref[j - 1, 0:1, t - SPILL:], dl_ref[j - 1, 0:1, t - SPILL:])
            dv_s[rows, :] += jnp.dot(pt, dos, preferred_element_type=F32)
            dk_s[rows, :] += jnp.dot(dst, qs, preferred_element_type=F32)

        def scores(c, slot):
            r0 = pl.multiple_of((nt - 1 - c) * t, t)
            s_bufs[slot][...] = lax.dot_general(k_ref[...], q_ref[pl.ds(r0, t), :], NT,
                                                preferred_element_type=F32)
            d_bufs[slot][...] = lax.dot_general(v_ref[...], do_ref[pl.ds(r0, t), :], NT,
                                                preferred_element_type=F32)

        def absorb(c, slot, masked):
            i = nt - 1 - c
            l2r, dlr = l2_ref[i, 0:1, :], dl_ref[i, 0:1, :]
            for r in range(0, t, ROW_GROUP):
                rows = slice(r, r + ROW_GROUP)
                st = s_bufs[slot][rows, :]
                if masked:
                    st = jnp.where(_visible_t(j * t + r, j * t, ROW_GROUP, t), st, NEG)
                p_s[rows, :], ds_s[rows, :] = weights(st, d_bufs[slot][rows, :], l2r, dlr)
            r0 = pl.multiple_of(i * t, t)
            dv_s[...] += jnp.dot(p_s[...], do_ref[pl.ds(r0, t), :], preferred_element_type=F32)
            dk_s[...] += jnp.dot(ds_s[...], q_ref[pl.ds(r0, t), :], preferred_element_type=F32)

        _pipelined_chunks(nt - 1 - j, scores, absorb)
        dk_ref[...] = dk_s[...] * ATTN_SCALE
        dv_ref[...] = dv_s[...].astype(BF16)

    stat_rows = pl.BlockSpec((None, nt, SUBLANES, t), lambda h, j: (h, 0, 0, 0))
    return pl.pallas_call(
        body, name=name, grid=(N_HEADS, nt),
        in_specs=[pl.BlockSpec((lp, HEAD_PAD), lambda h, j: (0, h)),
                  pl.BlockSpec((t, HEAD_PAD), lambda h, j: (j, h)),
                  pl.BlockSpec((t, V_DIM), lambda h, j: (j, h)),
                  pl.BlockSpec((lp, V_DIM), lambda h, j: (0, h)),
                  stat_rows, stat_rows],
        out_specs=[pl.BlockSpec((t, HEAD_PAD), lambda h, j: (j, h)),
                   pl.BlockSpec((t, V_DIM), lambda h, j: (j, h))],
        out_shape=[jax.ShapeDtypeStruct((lp, N_HEADS * HEAD_PAD), F32),
                   jax.ShapeDtypeStruct((lp, N_HEADS * V_DIM), BF16)],
        scratch_shapes=[pltpu.VMEM((t, HEAD_PAD), F32), pltpu.VMEM((t, V_DIM), F32),
                        pltpu.VMEM((t, t), F32), pltpu.VMEM((t, t), F32), pltpu.VMEM((t, t), F32),
                        pltpu.VMEM((t, t), F32), pltpu.VMEM((t, t), BF16), pltpu.VMEM((t, t), BF16)],
        compiler_params=_params(("parallel", "arbitrary")),
    )(q, k, v, do, l2row, dlrow)


def _shift_down(cur, prev8, k):
    r = pltpu.roll(cur, k, 0)
    row8 = lax.broadcasted_iota(jnp.int32, prev8.shape, 0)
    first = jnp.where(row8 < k, pltpu.roll(prev8, k, 0), r[0:SUBLANES])
    return jnp.concatenate([first, r[SUBLANES:]], axis=0)


def _shift_up(cur, next8, k):
    t = cur.shape[0]
    r = pltpu.roll(cur, t - k, 0)
    row8 = lax.broadcasted_iota(jnp.int32, next8.shape, 0)
    last = jnp.where(row8 >= SUBLANES - k, pltpu.roll(next8, SUBLANES - k, 0), r[t - SUBLANES:])
    return jnp.concatenate([r[:t - SUBLANES], last], axis=0)


def _scan_down(a, b):
    t = a.shape[0]
    row = lax.broadcasted_iota(jnp.int32, a.shape, 0)
    s = 1
    while s < t:
        keep = row >= s
        a_sh = jnp.where(keep, pltpu.roll(a, s, 0), 1.0)
        b_sh = jnp.where(keep, pltpu.roll(b, s, 0), 0.0)
        b = a * b_sh + b
        a = a * a_sh
        s *= 2
    return a, b


def _scan_up(a, b):
    t = a.shape[0]
    row = lax.broadcasted_iota(jnp.int32, a.shape, 0)
    s = 1
    while s < t:
        keep = row < t - s
        a_sh = jnp.where(keep, pltpu.roll(a, t - s, 0), 1.0)
        b_sh = jnp.where(keep, pltpu.roll(b, t - s, 0), 0.0)
        b = a * b_sh + b
        a = a * a_sh
        s *= 2
    return a, b


def _neg_expm1(y):
    series = -y * (1.0 + y * (0.5 + y * (1.0 / 6.0 + y * (1.0 / 24.0 + y * (1.0 / 120.0)))))
    return jnp.where(y > -0.1, series, 1.0 - jnp.exp(y))


def _log_sigmoid(x):
    return jnp.minimum(x, 0.0) - jnp.log(1.0 + jnp.exp(-jnp.abs(x)))


GELU_C = math.sqrt(2.0 / math.pi)
GELU_K = 0.044715


def _gelu(x):
    th = jnp.tanh(GELU_C * (x + GELU_K * x * x * x))
    return 0.5 * x * (1.0 + th), th


def _block_mm(xb, w_ref, dims):
    rb = D_RNN // RNN_BLOCKS
    return jnp.concatenate(
        [lax.dot_general(xb[:, h * rb:(h + 1) * rb], w_ref[h], dims, preferred_element_type=F32)
         for h in range(RNN_BLOCKS)], axis=1)


def _rglru_gates(ux, prev8, pv_ref, wa_ref, wi_ref):
    shifted = [ux] + [_shift_down(ux, prev8, k) for k in range(1, CONV_WIDTH)]
    xc = pv_ref[4:5, :] + pv_ref[3:4, :] * ux
    for k in range(1, CONV_WIDTH):
        xc = xc + pv_ref[3 - k:4 - k, :] * shifted[k]
    xcb = xc.astype(BF16)
    r_g = _sig(_block_mm(xcb, wa_ref, NN) + pv_ref[5:6, :])
    i_g = _sig(_block_mm(xcb, wi_ref, NN) + pv_ref[6:7, :])
    log_a = LRU_C * r_g * _log_sigmoid(pv_ref[7:8, :])
    a = jnp.exp(log_a)
    mm = jnp.sqrt(_neg_expm1(2.0 * log_a))
    return dict(shifted=shifted, xc=xc, xcb=xcb, r=r_g, i=i_g, a=a, mm=mm)


def _rglru_fwd(ux, ug, pv, wa, wi, t, name):
    lp, d = ux.shape

    def body(ux_ref, ug_ref, pv_ref, wa_ref, wi_ref, y_ref, h_ref, tail_s, hc_s):
        @pl.when(pl.program_id(0) == 0)
        def _():
            tail_s[...] = jnp.zeros_like(tail_s)
            hc_s[...] = jnp.zeros_like(hc_s)

        uxv = ux_ref[...]
        gt = _rglru_gates(uxv, tail_s[...], pv_ref, wa_ref, wi_ref)
        tail_s[...] = ux_ref[t - SUBLANES:t, :]
        cum_a, hloc = _scan_down(gt["a"], gt["mm"] * (gt["i"] * gt["xc"]))
        h_ref[...] = hloc + cum_a * hc_s[0:1, :]
        hc_s[...] = h_ref[t - SUBLANES:t, :]
        hc_s[0:1, :] = h_ref[t - 1:t, :]
        y_ref[...] = (h_ref[...] * _gelu(ug_ref[...])[0]).astype(BF16)

    tile = pl.BlockSpec((t, d), lambda i: (i, 0))
    return pl.pallas_call(
        body, name=name, grid=(lp // t,),
        in_specs=[tile, tile, pl.BlockSpec(pv.shape, lambda i: (0, 0)),
                  pl.BlockSpec(wa.shape, lambda i: (0, 0, 0)), pl.BlockSpec(wi.shape, lambda i: (0, 0, 0))],
        out_specs=[tile, tile],
        out_shape=[jax.ShapeDtypeStruct((lp, d), BF16), jax.ShapeDtypeStruct((lp, d), F32)],
        scratch_shapes=[pltpu.VMEM((SUBLANES, d), F32), pltpu.VMEM((SUBLANES, d), F32)],
        compiler_params=_params(("arbitrary",)),
    )(ux, ug, pv, wa, wi)


def _rglru_bwd(ux, ug, hs, dy, pv, wa, wi, t, name):
    lp, d = ux.shape
    nt = lp // t
    per = t // SUBLANES
    rb = d // RNN_BLOCKS

    def body(ux_ref, uxp_ref, ug_ref, h_ref, hp_ref, dy_ref, pv_ref, wa_ref, wi_ref,
             dux_ref, dug_ref, dpv_ref, dwa_ref, dwi_ref, ca_s, cg_s, cx_s):
        step = pl.program_id(0)
        first_tile = step == nt - 1

        @pl.when(step == 0)
        def _():
            for ref in (ca_s, cg_s, cx_s, dpv_ref, dwa_ref, dwi_ref):
                ref[...] = jnp.zeros_like(ref)

        uxv = ux_ref[...]
        prev8 = jnp.where(first_tile, 0.0, uxp_ref[...])
        hprev8 = jnp.where(first_tile, 0.0, hp_ref[...])
        gt = _rglru_gates(uxv, prev8, pv_ref, wa_ref, wi_ref)
        a, mm, r_g, i_g, xc = gt["a"], gt["mm"], gt["r"], gt["i"], gt["xc"]
        hv = h_ref[...]
        hprev = _shift_down(hv, hprev8, 1)
        ugv, dyv = ug_ref[...], dy_ref[...]
        gel, th = _gelu(ugv)
        dgel = 0.5 * (1.0 + th) + 0.5 * ugv * (1.0 - th * th) * (GELU_C * (1.0 + 3.0 * GELU_K * ugv * ugv))
        dug_ref[...] = (dyv * hv * dgel).astype(BF16)
        a_up = _shift_up(a, ca_s[...], 1)
        cum_a, gloc = _scan_up(a_up, dyv * gel)
        gv = gloc + cum_a * cg_s[0:1, :]
        ca_s[...] = a[0:SUBLANES]
        cg_s[...] = gv[0:SUBLANES]
        ixc = i_g * xc
        d_ixc = gv * mm
        d_log_a = gv * hprev * a - (gv * ixc) * (a * a) / mm
        logsig = _log_sigmoid(pv_ref[7:8, :])
        d_pre_a = d_log_a * (LRU_C * logsig) * r_g * (1.0 - r_g)
        d_pre_i = d_ixc * xc * i_g * (1.0 - i_g)
        dab, dib = d_pre_a.astype(BF16), d_pre_i.astype(BF16)
        d_xc = d_ixc * i_g + _block_mm(dab, wa_ref, NT) + _block_mm(dib, wi_ref, NT)
        xcb = gt["xcb"]
        for h in range(RNN_BLOCKS):
            cols = slice(h * rb, (h + 1) * rb)
            dwa_ref[h] += lax.dot_general(xcb[:, cols], dab[:, cols], TN, preferred_element_type=F32)
            dwi_ref[h] += lax.dot_general(xcb[:, cols], dib[:, cols], TN, preferred_element_type=F32)
        csum = lambda v: jnp.sum(v, axis=0, keepdims=True)
        for k in range(CONV_WIDTH):
            dpv_ref[3 - k:4 - k, :] += csum(d_xc * gt["shifted"][k])
        dpv_ref[4:5, :] += csum(d_xc)
        dpv_ref[5:6, :] += csum(d_pre_a)
        dpv_ref[6:7, :] += csum(d_pre_i)
        dpv_ref[7:8, :] += csum(d_log_a * (LRU_C * r_g)) * _sig(-pv_ref[7:8, :])
        dux = pv_ref[3:4, :] * d_xc
        for k in range(1, CONV_WIDTH):
            dux = dux + pv_ref[3 - k:4 - k, :] * _shift_up(d_xc, cx_s[...], k)
        cx_s[...] = d_xc[0:SUBLANES]
        dux_ref[...] = dux.astype(BF16)

    rev = lambda i: (nt - 1 - i, 0)
    before = lambda i: (jnp.maximum((nt - 1 - i) * per - 1, 0), 0)
    tile = pl.BlockSpec((t, d), rev)
    tail = pl.BlockSpec((SUBLANES, d), before)
    fixed2 = lambda arr: pl.BlockSpec(arr.shape, lambda i: (0, 0))
    fixed3 = lambda arr: pl.BlockSpec(arr.shape, lambda i: (0, 0, 0))
    return pl.pallas_call(
        body, name=name, grid=(nt,),
        in_specs=[tile, tail, tile, tile, tail, tile, fixed2(pv), fixed3(wa), fixed3(wi)],
        out_specs=[tile, tile, fixed2(pv), fixed3(wa), fixed3(wi)],
        out_shape=[jax.ShapeDtypeStruct((lp, d), BF16), jax.ShapeDtypeStruct((lp, d), BF16),
                   jax.ShapeDtypeStruct(pv.shape, F32), jax.ShapeDtypeStruct(wa.shape, F32),
                   jax.ShapeDtypeStruct(wi.shape, F32)],
        scratch_shapes=[pltpu.VMEM((SUBLANES, d), F32)] * 3,
        compiler_params=_params(("arbitrary",)),
    )(ux, ux, ug, hs, hs, dy, pv, wa, wi)


def _adamw(w, m, v, parts, name):
    rows, cols = w.shape
    tr = _pick(rows, 256, SUBLANES)
    c1 = 1.0 / (1.0 - ADAM_B1 ** ADAM_STEP)
    c2 = 1.0 / (1.0 - ADAM_B2 ** ADAM_STEP)

    def body(w_ref, m_ref, v_ref, p_ref, g_ref, d_ref, mo_ref, vo_ref):
        g = p_ref[0].astype(F32)
        for q in range(1, N_DEV):
            g = g + p_ref[q].astype(F32)
        mn = ADAM_B1 * m_ref[...] + (1.0 - ADAM_B1) * g
        vn = ADAM_B2 * v_ref[...] + (1.0 - ADAM_B2) * (g * g)
        g_ref[...] = g
        mo_ref[...] = mn
        vo_ref[...] = vn
        d_ref[...] = -ADAM_LR * ((mn * c1) / (jnp.sqrt(vn * c2) + ADAM_EPS) + ADAM_WD * w_ref[...])

    blk = pl.BlockSpec((tr, cols), lambda i: (i, 0))
    return pl.pallas_call(
        body, name=name, grid=(rows // tr,),
        in_specs=[blk, blk, blk, pl.BlockSpec((N_DEV, tr, cols), lambda i: (0, i, 0))],
        out_specs=[blk] * 4, out_shape=[jax.ShapeDtypeStruct((rows, cols), F32)] * 4,
        compiler_params=_params(("parallel",)),
    )(w, m, v, parts)


WEIGHTS = ("meta_tokens", "norm_mix_g", "w_in", "b_gate", "conv_w", "conv_b", "w_rec_a", "b_rec_a", "w_rec_i",
           "b_rec_i", "lru_lambda", "q_norm_g", "w_uq", "kv_norm_g", "w_ukv", "w_branch", "w_out", "norm_ffn_g",
           "w_ffn_in", "w_ffn_out", "final_norm_g")
SHARDED = {"meta_tokens": True, "w_in": True, "b_gate": True, "conv_w": True, "w_uq": True, "w_ukv": True,
           "w_branch": False, "w_out": False, "w_ffn_in": True, "w_ffn_out": False}
REPLICATED = tuple(n for n in WEIGHTS if n not in SHARDED)


def _as2d(a):
    return a.reshape(-1, a.shape[-1])


def _full_from_gathered(g, by_cols):
    if by_cols:
        return jnp.transpose(g, (1, 0, 2)).reshape(g.shape[1], N_DEV * g.shape[2])
    return g.reshape(N_DEV * g.shape[1], g.shape[2])


def _blocks_from_full(full, by_cols):
    if by_cols:
        r, c = full.shape
        return jnp.transpose(full.reshape(r, N_DEV, c // N_DEV), (1, 0, 2))
    return full.reshape(N_DEV, full.shape[0] // N_DEV, full.shape[1])


def _pack(arrs):
    flat = jnp.concatenate([a.reshape(-1) for a in arrs])
    rows = -(-flat.shape[0] // (LANES * SUBLANES)) * SUBLANES
    return jnp.pad(flat, (0, rows * LANES - flat.shape[0])).reshape(rows, LANES)


def _unpack(packed, like):
    flat = packed.reshape(-1)
    out, off = [], 0
    for a in like:
        out.append(flat[off:off + a.size].reshape(a.shape))
        off += a.size
    return out


def kernel(x, meta_tokens, norm_mix_g, w_in, b_gate, conv_w, conv_b, w_rec_a, b_rec_a, w_rec_i, b_rec_i, lru_lambda, q_norm_g, w_uq, kv_norm_g, w_ukv, w_branch, w_out, norm_ffn_g, w_ffn_in, w_ffn_out, final_norm_g, loss_target, m_meta_tokens, m_norm_mix_g, m_w_in, m_b_gate, m_conv_w, m_conv_b, m_w_rec_a, m_b_rec_a, m_w_rec_i, m_b_rec_i, m_lru_lambda, m_q_norm_g, m_w_uq, m_kv_norm_g, m_w_ukv, m_w_branch, m_w_out, m_norm_ffn_g, m_w_ffn_in, m_w_ffn_out, m_final_norm_g, v_meta_tokens, v_norm_mix_g, v_w_in, v_b_gate, v_conv_w, v_conv_b, v_w_rec_a, v_b_rec_a, v_w_rec_i, v_b_rec_i, v_lru_lambda, v_q_norm_g, v_w_uq, v_kv_norm_g, v_w_ukv, v_w_branch, v_w_out, v_norm_ffn_g, v_w_ffn_in, v_w_ffn_out, v_final_norm_g):
    w = dict(meta_tokens=meta_tokens, norm_mix_g=norm_mix_g, w_in=w_in, b_gate=b_gate, conv_w=conv_w, conv_b=conv_b,
             w_rec_a=w_rec_a, b_rec_a=b_rec_a, w_rec_i=w_rec_i, b_rec_i=b_rec_i, lru_lambda=lru_lambda,
             q_norm_g=q_norm_g, w_uq=w_uq, kv_norm_g=kv_norm_g, w_ukv=w_ukv, w_branch=w_branch, w_out=w_out,
             norm_ffn_g=norm_ffn_g, w_ffn_in=w_ffn_in, w_ffn_out=w_ffn_out, final_norm_g=final_norm_g)
    m = dict(meta_tokens=m_meta_tokens, norm_mix_g=m_norm_mix_g, w_in=m_w_in, b_gate=m_b_gate, conv_w=m_conv_w,
             conv_b=m_conv_b, w_rec_a=m_w_rec_a, b_rec_a=m_b_rec_a, w_rec_i=m_w_rec_i, b_rec_i=m_b_rec_i,
             lru_lambda=m_lru_lambda, q_norm_g=m_q_norm_g, w_uq=m_w_uq, kv_norm_g=m_kv_norm_g, w_ukv=m_w_ukv,
             w_branch=m_w_branch, w_out=m_w_out, norm_ffn_g=m_norm_ffn_g, w_ffn_in=m_w_ffn_in,
             w_ffn_out=m_w_ffn_out, final_norm_g=m_final_norm_g)
    v = dict(meta_tokens=v_meta_tokens, norm_mix_g=v_norm_mix_g, w_in=v_w_in, b_gate=v_b_gate, conv_w=v_conv_w,
             conv_b=v_conv_b, w_rec_a=v_w_rec_a, b_rec_a=v_b_rec_a, w_rec_i=v_w_rec_i, b_rec_i=v_b_rec_i,
             lru_lambda=v_lru_lambda, q_norm_g=v_q_norm_g, w_uq=v_w_uq, kv_norm_g=v_kv_norm_g, w_ukv=v_w_ukv,
             w_branch=v_w_branch, w_out=v_w_out, norm_ffn_g=v_norm_ffn_g, w_ffn_in=v_w_ffn_in,
             w_ffn_out=v_w_ffn_out, final_norm_g=v_final_norm_g)

    seq, d_model = x.shape[1], x.shape[2]
    length = N_META + seq
    lp = -(-length // LANES) * LANES
    t_attn = _pick(lp, 640)
    t_rnn = LANES

    small = ("meta_tokens", "b_gate", "conv_w")
    names = list(SHARDED)
    first, mid, late = ("meta_tokens", "b_gate", "conv_w", "w_in"), ("w_uq", "w_ukv", "w_branch", "w_out"), (
        "w_ffn_in", "w_ffn_out")
    payload = lambda n: _as2d(w[n]) if n in small else _as2d(w[n]).astype(BF16)
    got = _exchange([payload(n) for n in first], [True] * len(first), "gather_first")
    mid_h = _exchange_start([payload(n) for n in mid], [True] * len(mid), got[0], "gather_mid_start")
    late_h = _exchange_start([payload(n) for n in late], [True] * len(late), mid_h["token"], "gather_late_start")
    full = {n: _full_from_gathered(g, SHARDED[n]) for n, g in zip(first, got)}

    splits = (D_RNN, D_RNN, Q_RANK, KV_RANK, QK_ROPE, 2 * d_model)
    offs = [0]
    for s in splits:
        offs.append(offs[-1] + s)
    w_x, w_g, w_q, w_kv, w_kr, w_m = (full["w_in"][:, offs[s]:offs[s + 1]] for s in range(6))
    w_kr = jnp.pad(w_kr, ((0, 0), (0, LANES - QK_ROPE)))
    bg = full["b_gate"].reshape(1, 2 * d_model)
    pv = jnp.concatenate([full["conv_w"], conv_b, b_rec_a, b_rec_i, lru_lambda], axis=0)
    wa_b, wi_b = w_rec_a[0].astype(BF16), w_rec_i[0].astype(BF16)
    g_final = final_norm_g.reshape(1, d_model)

    h0 = jnp.concatenate([full["meta_tokens"], x[0], jnp.zeros((lp - length, d_model), F32)], axis=0)
    tgt = jnp.pad(loss_target[0], ((N_META, lp - length), (0, 0)))
    tabs = _rope_tables(lp)

    z = _rmsnorm_fwd(h0, norm_mix_g, "norm_mix")
    ux = _mm([(z, w_x)], "nn", "in_x")
    ug = _mm([(z, w_g)], "nn", "in_g")
    uq = _mm([(z, w_q)], "nn", "in_q")
    ukv = _mm([(z, w_kv)], "nn", "in_kv")
    ukr = _mm([(z, w_kr)], "nn", "in_kr")
    um = _mm([(z, w_m)], "nn", "in_m")
    for n, g in zip(mid, _exchange_wait(mid_h, um, "gather_mid_wait")):
        full[n] = _full_from_gathered(g, SHARDED[n])
    w_uq_pad = jnp.pad(full["w_uq"].reshape(Q_RANK, N_HEADS, QK_NOPE + QK_ROPE),
                       ((0, 0), (0, 0), (0, HEAD_PAD - QK_NOPE - QK_ROPE))).reshape(Q_RANK, N_HEADS * HEAD_PAD)
    w_ukv3 = full["w_ukv"].reshape(KV_RANK, N_HEADS, QK_NOPE + V_DIM)
    w_k_pad = jnp.pad(w_ukv3[:, :, :QK_NOPE], ((0, 0), (0, 0), (0, HEAD_PAD - QK_NOPE))).reshape(
        KV_RANK, N_HEADS * HEAD_PAD)
    w_v = w_ukv3[:, :, QK_NOPE:].reshape(KV_RANK, N_HEADS * V_DIM)
    wb_r, wb_a = full["w_branch"][:D_RNN], full["w_branch"][D_RNN:]
    y_rnn, hs = _rglru_fwd(ux, ug, pv, wa_b, wi_b, t_rnn, "rglru_fwd")
    qn = _rmsnorm_fwd(uq, q_norm_g, "norm_q")
    kvn = _rmsnorm_fwd(ukv, kv_norm_g, "norm_kv")
    qpad = _mm([(qn, w_uq_pad)], "nn", "up_q")
    kpad = _mm([(kvn, w_k_pad)], "nn", "up_k")
    vh = _mm([(kvn, w_v)], "nn", "up_v", out_dtype=BF16)
    qh, kh = _rope_fwd(qpad, kpad, ukr, tabs, "rope_fwd")
    oh, lse = _attn_fwd(qh, kh, vh, t_attn, "attn_fwd")
    p_rnn = _mm([(y_rnn, wb_r)], "nn", "branch_rnn")
    p_att = _mm([(oh, wb_a)], "nn", "branch_att")
    mixed = _mix_fwd(um, p_rnn, p_att, bg, "mix_fwd")
    h1 = _mm([(mixed, full["w_out"])], "nn", "out_proj", res=h0)
    for n, g in zip(late, _exchange_wait(late_h, h1, "gather_late_wait")):
        full[n] = _full_from_gathered(g, SHARDED[n])
    zf = _rmsnorm_fwd(h1, norm_ffn_g, "norm_ffn")
    gu = _mm([(zf, full["w_ffn_in"])], "nn", "ffn_in")
    act = _swiglu_fwd(gu, "swiglu_fwd")
    h2 = _mm([(act, full["w_ffn_out"])], "nn", "ffn_out", res=h1)
    dh2, dg_final, _, loss_part = _loss_bwd(h2, tgt, g_final, seq, "loss_bwd")

    d_act = _mm([(dh2, full["w_ffn_out"])], "nt", "d_act")
    dw_ffn_out = _mm_tn(act, dh2, "dw_ffn_out")
    d_gu = _swiglu_bwd(gu, d_act, "swiglu_bwd")
    dw_ffn_in = _mm_tn(zf, d_gu, "dw_ffn_in")
    blocks = lambda n, g: _blocks_from_full(g, SHARDED[n]).astype(F32 if n in small else BF16)
    sent = {("w_ffn_in", "w_ffn_out"): _exchange_start(
        [blocks("w_ffn_in", dw_ffn_in), blocks("w_ffn_out", dw_ffn_out)], [False] * 2, dg_final, "scatter_ffn_start")}
    d_zf = _mm([(d_gu, full["w_ffn_in"], D_FF, 0), (d_gu, full["w_ffn_in"], D_FF, 1)], "nt", "d_zf")
    dh1, dg_ffn = _rmsnorm_bwd(h1, d_zf, norm_ffn_g, "norm_ffn_bwd", res=dh2)
    d_mixed = _mm([(dh1, full["w_out"])], "nt", "d_mixed")
    dw_out = _mm_tn(mixed, dh1, "dw_out")
    d_prnn, d_patt, d_um, dbg = _mix_bwd(um, p_rnn, p_att, d_mixed, bg, "mix_bwd")
    d_yrnn = _mm([(d_prnn, wb_r)], "nt", "d_yrnn")
    d_oh = _mm([(d_patt, wb_a)], "nt", "d_oh", out_dtype=BF16)
    dwb_r = _mm_tn(y_rnn, d_prnn, "dw_branch_rnn")
    dwb_a = _mm_tn(oh, d_patt, "dw_branch_att")
    sent[("w_out", "w_branch")] = _exchange_start(
        [blocks("w_out", dw_out), blocks("w_branch", jnp.concatenate([dwb_r, dwb_a], axis=0))], [False] * 2,
        dg_ffn, "scatter_mix_start")
    dqh, l2row, dlrow = _attn_bwd_dq(qh, kh, vh, d_oh, oh, lse, t_attn, "attn_bwd_dq")
    dkh, dvh = _attn_bwd_dkv(qh, kh, vh, d_oh, l2row, dlrow, t_attn, "attn_bwd_dkv")
    dqpad, dkpad, dukr = _rope_bwd(dqh, dkh, tabs, "rope_bwd")
    d_qn = _mm([(dqpad, w_uq_pad)], "nt", "d_qn")
    dw_uq_pad = _mm_tn(qn, dqpad, "dw_uq")
    d_kvn = _mm([(dkpad, w_k_pad), (dvh, w_v)], "nt", "d_kvn")
    dw_k_pad = _mm_tn(kvn, dkpad, "dw_uk")
    dw_v = _mm_tn(kvn, dvh, "dw_uv")
    dw_uq = dw_uq_pad.reshape(Q_RANK, N_HEADS, HEAD_PAD)[:, :, :QK_NOPE + QK_ROPE].reshape(Q_RANK, -1)
    dw_ukv = jnp.concatenate([dw_k_pad.reshape(KV_RANK, N_HEADS, HEAD_PAD)[:, :, :QK_NOPE],
                              dw_v.reshape(KV_RANK, N_HEADS, V_DIM)], axis=2).reshape(KV_RANK, -1)
    sent[("w_uq", "w_ukv")] = _exchange_start([blocks("w_uq", dw_uq), blocks("w_ukv", dw_ukv)], [False] * 2,
                                              dbg, "scatter_attn_start")
    duq, dg_q = _rmsnorm_bwd(uq, d_qn, q_norm_g, "norm_q_bwd", out_dtype=BF16)
    dukv, dg_kv = _rmsnorm_bwd(ukv, d_kvn, kv_norm_g, "norm_kv_bwd", out_dtype=BF16)
    dux, dug, dpv, dwa, dwi = _rglru_bwd(ux, ug, hs, d_yrnn, pv, wa_b, wi_b, t_rnn, "rglru_bwd")
    d_z = _mm([(dux, w_x), (dug, w_g), (duq, w_q), (dukv, w_kv), (dukr, w_kr), (d_um, w_m)], "nt", "d_z")
    dw_in = jnp.concatenate([
        _mm_tn(z, dux, "dw_in_x"), _mm_tn(z, dug, "dw_in_g"), _mm_tn(z, duq, "dw_in_q"),
        _mm_tn(z, dukv, "dw_in_kv"), _mm_tn(z, dukr, "dw_in_kr")[:, :QK_ROPE], _mm_tn(z, d_um, "dw_in_m")], axis=1)
    dh0, dg_mix = _rmsnorm_bwd(h0, d_z, norm_mix_g, "norm_mix_bwd", res=dh1)

    grad_last = dict(meta_tokens=dh0[:N_META], b_gate=dbg.reshape(2, d_model), conv_w=dpv[:CONV_WIDTH], w_in=dw_in)
    grad_rep = dict(
        norm_mix_g=dg_mix, conv_b=dpv[4:5], w_rec_a=dwa, b_rec_a=dpv[5:6], w_rec_i=dwi, b_rec_i=dpv[6:7],
        lru_lambda=dpv[7:8], q_norm_g=dg_q, kv_norm_g=dg_kv, norm_ffn_g=dg_ffn, final_norm_g=dg_final)

    rep_pack = _pack([grad_rep[n] for n in REPLICATED])
    *recv_last, recv_rep = _exchange([blocks(n, grad_last[n]) for n in first] + [rep_pack],
                                     [False] * len(first) + [True], "exchange_last_grads")
    received = dict(zip(first, recv_last))
    for group, handle in sent.items():
        received.update(zip(group, _exchange_wait(handle, recv_rep, "scatter_wait_" + group[0])))

    grads, deltas, new_m, new_v = {}, {}, {}, {}
    for n in names:
        parts = received[n]
        g2, d2, m2, v2 = _adamw(_as2d(w[n]), _as2d(m[n]), _as2d(v[n]), parts, "adamw_" + n)
        for store, val in ((grads, g2), (deltas, d2), (new_m, m2), (new_v, v2)):
            store[n] = val.reshape(w[n].shape)
    rep_like = [w[n] for n in REPLICATED]
    outs = _adamw(_pack(rep_like), _pack([m[n] for n in REPLICATED]), _pack([v[n] for n in REPLICATED]),
                  recv_rep, "adamw_replicated")
    for store, val in zip((grads, deltas, new_m, new_v), outs):
        for n, a in zip(REPLICATED, _unpack(val, rep_like)):
            store[n] = a

    loss = lax.psum(loss_part[0, 0], MESH_AXES)
    grad_x = dh0[N_META:length][None]
    return (loss, grad_x, *[grads[n] for n in WEIGHTS], *[deltas[n] for n in WEIGHTS],
            *[new_m[n] for n in WEIGHTS], *[new_v[n] for n in WEIGHTS])
```

```python
import functools
import math

import jax
import jax.numpy as jnp
from jax import lax
from jax.experimental import pallas as pl
from jax.experimental.pallas import tpu as pltpu

F32 = jnp.float32
BF16 = jnp.bfloat16

N_DEV = 8
MESH_AXES = ("x", "y", "c")
LANES = 128
SUBLANES = 8
VMEM_LIMIT = 56 * 1024 * 1024

N_META = 16
CHUNK_SHIFT = 6
CHUNK_BIAS = 64 - N_META
EPS = 1e-6
D_RNN = 1280
RNN_BLOCKS = 10
CONV_WIDTH = 4
LRU_C = 8.0
N_HEADS = 8
QK_NOPE = 128
QK_ROPE = 64
V_DIM = 128
HEAD_PAD = 256
Q_RANK = 384
KV_RANK = 256
ROPE_THETA = 10000.0
ATTN_SCALE = 1.0 / math.sqrt(QK_NOPE + QK_ROPE)
NEG = -1e30
LOG2E = 1.0 / math.log(2.0)
SCALE_LOG2E = ATTN_SCALE * LOG2E
Q_SPLIT = 2
Q_ALIGN = LANES // Q_SPLIT
ROW_GROUP = 32
SPILL = LANES
D_FF = 2816

ADAM_LR = 0.001
ADAM_B1 = 0.9
ADAM_B2 = 0.999
ADAM_EPS = 1e-08
ADAM_WD = 0.01
ADAM_STEP = 10

NN = (((1,), (0,)), ((), ()))
NT = (((1,), (1,)), ((), ()))
TN = (((0,), (0,)), ((), ()))


def _pick(n, cap, base=LANES):
    best = None
    for t in range(base, min(n, cap) + 1, base):
        if n % t == 0:
            best = t
    return best if best is not None else n


def _params(sem=None):
    return pltpu.CompilerParams(dimension_semantics=sem, vmem_limit_bytes=VMEM_LIMIT)


def _sig(x):
    return 1.0 / (1.0 + jnp.exp(-x))


def _exchange(srcs, gather, name):
    n = len(srcs)
    out_shape = [jax.ShapeDtypeStruct((N_DEV,) + (s.shape if g else s.shape[1:]), s.dtype)
                 for s, g in zip(srcs, gather)]

    def body(*refs):
        src, dst = refs[:n], refs[n:2 * n]
        send_sems, recv_sems, local_sems = refs[2 * n:]
        x, y, c = lax.axis_index("x"), lax.axis_index("y"), lax.axis_index("c")
        me = 4 * x + 2 * y + c
        local = []
        for t in range(n):
            cp = pltpu.make_async_copy(src[t] if gather[t] else src[t].at[me], dst[t].at[me], local_sems.at[t])
            cp.start()
            local.append(cp)
        sends, recvs = [], []
        for k in range(1, N_DEV):
            px = 1 - x if k & 4 else x
            py = 1 - y if k & 2 else y
            pc = 1 - c if k & 1 else c
            peer = 4 * px + 2 * py + pc
            for t in range(n):
                cp = pltpu.make_async_remote_copy(
                    src_ref=src[t] if gather[t] else src[t].at[peer], dst_ref=dst[t].at[me],
                    send_sem=send_sems.at[t, k - 1], recv_sem=recv_sems.at[t, k - 1],
                    device_id=(px, py, pc), device_id_type=pl.DeviceIdType.MESH)
                cp.start()
                sends.append(cp)
                recvs.append(pltpu.make_async_remote_copy(
                    src_ref=src[t] if gather[t] else src[t].at[peer], dst_ref=dst[t].at[peer],
                    send_sem=send_sems.at[t, k - 1], recv_sem=recv_sems.at[t, k - 1],
                    device_id=(px, py, pc), device_id_type=pl.DeviceIdType.MESH))
        for cp in recvs:
            cp.wait_recv()
        for cp in sends:
            cp.wait_send()
        for cp in local:
            cp.wait()

    any_spec = pl.BlockSpec(memory_space=pl.ANY)
    return pl.pallas_call(
        body, name=name, out_shape=out_shape,
        in_specs=[any_spec] * n, out_specs=[any_spec] * n,
        scratch_shapes=[pltpu.SemaphoreType.DMA((n, N_DEV - 1)), pltpu.SemaphoreType.DMA((n, N_DEV - 1)),
                        pltpu.SemaphoreType.DMA((n,))],
    )(*srcs)


HBM_SPEC = pl.BlockSpec(memory_space=pltpu.HBM)
SEM_SPEC = pl.BlockSpec(memory_space=pltpu.SEMAPHORE)
DATAFLOW = pltpu.SideEffectType.DATAFLOW_SIDE_EFFECTING


def _peers(x, y, c):
    out = []
    for k in range(1, N_DEV):
        px = 1 - x if k & 4 else x
        py = 1 - y if k & 2 else y
        pc = 1 - c if k & 1 else c
        out.append((k, (px, py, pc), 4 * px + 2 * py + pc))
    return out


def _split_copies(src, land, gather, send_sems, recv_sems, local_sems):
    x, y, c = lax.axis_index("x"), lax.axis_index("y"), lax.axis_index("c")
    me = 4 * x + 2 * y + c
    n = len(src)
    local = [pltpu.make_async_copy(src[t] if gather[t] else src[t].at[me], land[t].at[me], local_sems.at[t])
             for t in range(n)]
    sends, recvs = [], []
    for k, pos, peer in _peers(x, y, c):
        for t in range(n):
            mine = src[t] if gather[t] else src[t].at[peer]
            slot = t * (N_DEV - 1) + k - 1
            common = dict(send_sem=send_sems.at[slot], recv_sem=recv_sems.at[slot], device_id=pos,
                          device_id_type=pl.DeviceIdType.MESH)
            sends.append(pltpu.make_async_remote_copy(src_ref=mine, dst_ref=land[t].at[me], **common))
            recvs.append(pltpu.make_async_remote_copy(src_ref=mine, dst_ref=land[t].at[peer], **common))
    return local, sends, recvs


def _exchange_start(srcs, gather, after, name):
    n = len(srcs)
    lands = [lax.empty((N_DEV,) + (s.shape if g else s.shape[1:]), s.dtype) for s, g in zip(srcs, gather)]

    def body(*refs):
        src, land = refs[:n], refs[n:2 * n]
        send_sems, recv_sems, local_sems = refs[2 * n + 1:2 * n + 4]
        local, sends, _ = _split_copies(src, land, gather, send_sems, recv_sems, local_sems)
        for cp in local + sends:
            cp.start()
        refs[-1][...] = jnp.zeros_like(refs[-1])

    hbm = lambda a: pltpu.HBM(a.shape, a.dtype)
    outs = pl.pallas_call(
        body, name=name,
        out_shape=(pltpu.SemaphoreType.DMA((n * (N_DEV - 1),)), pltpu.SemaphoreType.DMA((n * (N_DEV - 1),)),
                   pltpu.SemaphoreType.DMA((n,)), *[hbm(s) for s in srcs], *[hbm(a) for a in lands],
                   jax.ShapeDtypeStruct((SUBLANES, LANES), F32)),
        in_specs=[HBM_SPEC] * (2 * n) + [pl.BlockSpec(memory_space=pl.ANY)],
        out_specs=(SEM_SPEC, SEM_SPEC, SEM_SPEC, *[HBM_SPEC] * (2 * n), pl.BlockSpec(memory_space=pltpu.VMEM)),
        input_output_aliases={t: 3 + t for t in range(2 * n)},
        compiler_params=pltpu.CompilerParams(has_side_effects=DATAFLOW),
    )(*[pltpu.with_memory_space_constraint(a, pltpu.HBM) for a in list(srcs) + lands], after)
    return dict(sems=outs[:3], srcs=outs[3:3 + n], lands=outs[3 + n:3 + 2 * n], token=outs[-1], gather=gather)


def _exchange_wait(handle, after, name):
    srcs, lands, gather = handle["srcs"], handle["lands"], handle["gather"]
    n = len(srcs)

    def body(*refs):
        src, land = refs[:n], refs[n:2 * n]
        send_sems, recv_sems, local_sems = refs[2 * n:2 * n + 3]
        local, sends, recvs = _split_copies(src, land, gather, send_sems, recv_sems, local_sems)
        for cp in sends:
            cp.wait_send()
        for cp in recvs:
            cp.wait_recv()
        for cp in local:
            cp.wait()

    hbm = lambda a: pltpu.HBM(a.shape, a.dtype)
    outs = pl.pallas_call(
        body, name=name, out_shape=(*[hbm(s) for s in srcs], *[hbm(a) for a in lands]),
        in_specs=[HBM_SPEC] * (2 * n) + [SEM_SPEC] * 3 + [pl.BlockSpec(memory_space=pl.ANY)],
        out_specs=[HBM_SPEC] * (2 * n), input_output_aliases={t: t for t in range(2 * n)},
        compiler_params=pltpu.CompilerParams(has_side_effects=DATAFLOW),
    )(*srcs, *lands, *handle["sems"], after)
    return outs[n:]


def _mm(pairs, mode, name, res=None, out_dtype=F32):
    pairs = [p if len(p) == 4 else (p[0], p[1], p[0].shape[1], 0) for p in pairs]
    m = pairs[0][0].shape[0]
    n = pairs[0][1].shape[1] if mode == "nn" else pairs[0][1].shape[0]
    tm, tn = _pick(m, 640), _pick(n, 1408)
    np_ = len(pairs)
    dims = NN if mode == "nn" else NT

    def body(*refs):
        acc = None
        for s in range(np_):
            d = lax.dot_general(refs[2 * s][...].astype(BF16), refs[2 * s + 1][...].astype(BF16), dims,
                                preferred_element_type=F32)
            acc = d if acc is None else acc + d
        if res is not None:
            acc = acc + refs[2 * np_][...]
        refs[-1][...] = acc.astype(out_dtype)

    in_specs, args = [], []
    for a, b, kt, kb in pairs:
        in_specs.append(pl.BlockSpec((tm, kt), lambda i, j, kb=kb: (i, kb)))
        if mode == "nn":
            in_specs.append(pl.BlockSpec((kt, tn), lambda i, j, kb=kb: (kb, j)))
        else:
            in_specs.append(pl.BlockSpec((tn, kt), lambda i, j, kb=kb: (j, kb)))
        args += [a, b]
    if res is not None:
        in_specs.append(pl.BlockSpec((tm, tn), lambda i, j: (i, j)))
        args.append(res)
    return pl.pallas_call(
        body, name=name, grid=(m // tm, n // tn), in_specs=in_specs,
        out_specs=pl.BlockSpec((tm, tn), lambda i, j: (i, j)),
        out_shape=jax.ShapeDtypeStruct((m, n), out_dtype),
        compiler_params=_params(("parallel", "parallel")),
    )(*args)


def _mm_tn(a, b, name):
    m, k = a.shape
    n = b.shape[1]
    tm, tk, tn = _pick(m, 1664), _pick(k, 1408), _pick(n, 1408)

    def body(a_ref, b_ref, o_ref):
        @pl.when(pl.program_id(2) == 0)
        def _():
            o_ref[...] = jnp.zeros_like(o_ref)

        o_ref[...] += lax.dot_general(a_ref[...].astype(BF16), b_ref[...].astype(BF16), TN,
                                      preferred_element_type=F32)

    return pl.pallas_call(
        body, name=name, grid=(k // tk, n // tn, m // tm),
        in_specs=[pl.BlockSpec((tm, tk), lambda i, j, r: (r, i)), pl.BlockSpec((tm, tn), lambda i, j, r: (r, j))],
        out_specs=pl.BlockSpec((tk, tn), lambda i, j, r: (i, j)),
        out_shape=jax.ShapeDtypeStruct((k, n), F32),
        compiler_params=_params(("parallel", "parallel", "arbitrary")),
    )(a, b)


ROW_TILE_BYTES = 6 * 1024 * 1024


def _row_tile(rows, row_in, row_out):
    per_row = sum((r[1] * r[0].dtype.itemsize) if isinstance(r, tuple) else (r.shape[1] * r.dtype.itemsize)
                  for r in row_in)
    per_row += sum(w * jnp.dtype(dt).itemsize for w, dt in row_out)
    return _pick(rows, min(640, max(LANES, ROW_TILE_BYTES // per_row)))


def _rowcall(body, name, rows, row_in, full_in, row_out, acc_out=()):
    tr = _row_tile(rows, row_in, row_out)
    n_steps = rows // tr
    in_specs, args = [], []
    for r in row_in:
        arr, w, cb = r if isinstance(r, tuple) else (r, r.shape[1], 0)
        in_specs.append(pl.BlockSpec((tr, w), lambda i, cb=cb: (i, cb)))
        args.append(arr)
    for f in full_in:
        in_specs.append(pl.BlockSpec(f.shape, lambda i, nd=f.ndim: (0,) * nd))
        args.append(f)
    out_specs = [pl.BlockSpec((tr, w), lambda i: (i, 0)) for w, _ in row_out]
    out_shape = [jax.ShapeDtypeStruct((rows, w), dt) for w, dt in row_out]
    for shp, dt in acc_out:
        out_specs.append(pl.BlockSpec(shp, lambda i, nd=len(shp): (0,) * nd))
        out_shape.append(jax.ShapeDtypeStruct(shp, dt))

    def wrapped(*refs):
        body(pl.program_id(0), n_steps, *refs)

    return pl.pallas_call(
        wrapped, name=name, grid=(n_steps,), in_specs=in_specs, out_specs=out_specs, out_shape=out_shape,
        compiler_params=_params(("arbitrary",) if acc_out else ("parallel",)),
    )(*args)


def _rmsnorm_fwd(x, g, name):
    rows, w = x.shape

    def body(i, n, x_ref, g_ref, o_ref):
        xv = x_ref[...]
        r = lax.rsqrt(jnp.mean(xv * xv, axis=-1, keepdims=True) + EPS)
        o_ref[...] = (xv * r * g_ref[...]).astype(BF16)

    return _rowcall(body, name, rows,[x], [g], [(w, BF16)])[0]


def _rmsnorm_bwd_math(xv, dy, g):
    w = xv.shape[-1]
    r = lax.rsqrt(jnp.mean(xv * xv, axis=-1, keepdims=True) + EPS)
    t = dy * g
    dx = r * t - xv * (r * r * r * (jnp.sum(t * xv, axis=-1, keepdims=True) / w))
    dg = jnp.sum(dy * xv * r, axis=0, keepdims=True)
    return dx, dg


def _rmsnorm_bwd(x, dy, g, name, res=None, out_dtype=F32, dep=None):
    rows, w = x.shape

    def body(i, n, *refs):
        x_ref, dy_ref = refs[0], refs[1]
        g_ref, dx_ref, dg_ref = refs[-3], refs[-2], refs[-1]
        dx, dg = _rmsnorm_bwd_math(x_ref[...], dy_ref[...], g_ref[...])
        if res is not None:
            dx = dx + refs[2][...]
        dx_ref[...] = dx.astype(out_dtype)

        @pl.when(i == 0)
        def _():
            dg_ref[...] = jnp.zeros_like(dg_ref)

        dg_ref[...] += dg

    row_in = [x, dy] + ([res] if res is not None else [])
    return _rowcall(body, name, rows, row_in, ([dep] if dep is not None else []) + [g], [(w, out_dtype)],
                    [((1, w), F32)])


def _loss_bwd(h2, tgt, g, seq, name):
    rows, w = h2.shape
    tr = _row_tile(rows, [h2, tgt], [(w, F32)])

    def body(i, n, h_ref, t_ref, g_ref, dh_ref, dg_ref, lcol_ref, loss_ref):
        hv, gv = h_ref[...], g_ref[...]
        row = i * tr + lax.broadcasted_iota(jnp.int32, (tr, w), 0)
        valid = jnp.logical_and(row >= N_META, row < N_META + seq)
        r = lax.rsqrt(jnp.mean(hv * hv, axis=-1, keepdims=True) + EPS)
        err = jnp.where(valid, hv * r * gv - t_ref[...], 0.0)
        dx, dg = _rmsnorm_bwd_math(hv, err * (1.0 / w), gv)
        dh_ref[...] = dx

        @pl.when(i == 0)
        def _():
            dg_ref[...] = jnp.zeros_like(dg_ref)
            lcol_ref[...] = jnp.zeros_like(lcol_ref)

        dg_ref[...] += dg
        lcol_ref[...] += jnp.sum(err * err, axis=0, keepdims=True)

        @pl.when(i == n - 1)
        def _():
            total = jnp.sum(lcol_ref[...], axis=1, keepdims=True) * (0.5 / w)
            loss_ref[...] = jnp.broadcast_to(total, loss_ref.shape)

    return _rowcall(body, name, rows, [h2, tgt], [g], [(w, F32)],
                    [((1, w), F32), ((1, w), F32), ((1, LANES), F32)])


def _mix_fwd(um, p_rnn, p_att, bg, name):
    rows, d = p_rnn.shape

    def body(i, n, u0_ref, u1_ref, pr_ref, pa_ref, bg_ref, o_ref):
        g0 = _sig(u0_ref[...] + bg_ref[:, :d])
        g1 = _sig(u1_ref[...] + bg_ref[:, d:])
        o_ref[...] = (g0 * pr_ref[...] + g1 * pa_ref[...]).astype(BF16)

    return _rowcall(body, name, rows,[(um, d, 0), (um, d, 1), p_rnn, p_att], [bg],
                    [(d, BF16)])[0]


def _mix_bwd(um, p_rnn, p_att, dmix, bg, name):
    rows, d = p_rnn.shape

    def body(i, n, u0_ref, u1_ref, pr_ref, pa_ref, dm_ref, bg_ref, dpr_ref, dpa_ref, dum_ref, dbg_ref):
        g0 = _sig(u0_ref[...] + bg_ref[:, :d])
        g1 = _sig(u1_ref[...] + bg_ref[:, d:])
        dm = dm_ref[...]
        dpr_ref[...] = (dm * g0).astype(BF16)
        dpa_ref[...] = (dm * g1).astype(BF16)
        du0 = dm * pr_ref[...] * g0 * (1.0 - g0)
        du1 = dm * pa_ref[...] * g1 * (1.0 - g1)
        dum_ref[:, :d] = du0.astype(BF16)
        dum_ref[:, d:] = du1.astype(BF16)

        @pl.when(i == 0)
        def _():
            dbg_ref[...] = jnp.zeros_like(dbg_ref)

        dbg_ref[:, :d] += jnp.sum(du0, axis=0, keepdims=True)
        dbg_ref[:, d:] += jnp.sum(du1, axis=0, keepdims=True)

    return _rowcall(body, name, rows,[(um, d, 0), (um, d, 1), p_rnn, p_att, dmix], [bg],
                    [(d, BF16), (d, BF16), (2 * d, BF16)], [((1, 2 * d), F32)])


def _swiglu_fwd(gu, name):
    rows, w2 = gu.shape
    f = w2 // 2

    def body(i, n, g_ref, u_ref, o_ref):
        gate = g_ref[...]
        o_ref[...] = (gate * _sig(gate) * u_ref[...]).astype(BF16)

    return _rowcall(body, name, rows,[(gu, f, 0), (gu, f, 1)], [], [(f, BF16)])[0]


def _swiglu_bwd(gu, dact, name):
    rows, w2 = gu.shape
    f = w2 // 2

    def body(i, n, g_ref, u_ref, da_ref, o_ref):
        gate, da = g_ref[...], da_ref[...]
        sg = _sig(gate)
        o_ref[:, :f] = (da * u_ref[...] * (sg * (1.0 + gate * (1.0 - sg)))).astype(BF16)
        o_ref[:, f:] = (da * gate * sg).astype(BF16)

    return _rowcall(body, name, rows,[(gu, f, 0), (gu, f, 1), dact], [], [(w2, BF16)])[0]


def _rope_tables(lp):
    idx = jnp.arange(lp, dtype=jnp.int32).astype(F32)
    inv_freq = ROPE_THETA ** (-jnp.arange(0, QK_ROPE, 2, dtype=F32) / QK_ROPE)
    ang = idx[:, None] * inv_freq[None, :]
    cos, sin = jnp.cos(ang), jnp.sin(ang)
    half = QK_ROPE // 2
    z = lambda wdt: jnp.zeros((lp, wdt), F32)
    tc = jnp.concatenate([cos, cos, z(LANES - 2 * half)], axis=1)
    ts1 = jnp.concatenate([-sin, z(LANES - half)], axis=1)
    ts2 = jnp.concatenate([z(half), sin, z(LANES - 2 * half)], axis=1)
    return tc, ts1, ts2


def _rope(xv, tc, ts1, ts2):
    half = QK_ROPE // 2
    return xv * tc + pltpu.roll(xv, LANES - half, 1) * ts1 + pltpu.roll(xv, half, 1) * ts2


def _rope_t(dv, tc, ts1, ts2):
    half = QK_ROPE // 2
    return dv * tc + pltpu.roll(dv * ts1, half, 1) + pltpu.roll(dv * ts2, LANES - half, 1)


def _rope_fwd(qpad, kpad, ukr, tabs, name):
    rows, w = qpad.shape

    def body(i, n, q_ref, k_ref, r_ref, c_ref, s1_ref, s2_ref, qo_ref, ko_ref):
        tc, ts1, ts2 = c_ref[...], s1_ref[...], s2_ref[...]
        kr = _rope(r_ref[...], tc, ts1, ts2).astype(BF16)
        for h in range(N_HEADS):
            lo, mid, hi = h * HEAD_PAD, h * HEAD_PAD + QK_NOPE, (h + 1) * HEAD_PAD
            qo_ref[:, lo:mid] = q_ref[:, lo:mid].astype(BF16)
            qo_ref[:, mid:hi] = _rope(q_ref[:, mid:hi], tc, ts1, ts2).astype(BF16)
            ko_ref[:, lo:mid] = k_ref[:, lo:mid].astype(BF16)
            ko_ref[:, mid:hi] = kr

    return _rowcall(body, name, rows,[qpad, kpad, ukr, *tabs], [], [(w, BF16), (w, BF16)])


def _rope_bwd(dq, dk, tabs, name):
    rows, w = dq.shape

    def body(i, n, dq_ref, dk_ref, c_ref, s1_ref, s2_ref, qo_ref, ko_ref, ro_ref):
        tc, ts1, ts2 = c_ref[...], s1_ref[...], s2_ref[...]
        dkr = None
        for h in range(N_HEADS):
            lo, mid, hi = h * HEAD_PAD, h * HEAD_PAD + QK_NOPE, (h + 1) * HEAD_PAD
            qo_ref[:, lo:mid] = dq_ref[:, lo:mid].astype(BF16)
            qo_ref[:, mid:hi] = _rope_t(dq_ref[:, mid:hi], tc, ts1, ts2).astype(BF16)
            ko_ref[:, lo:mid] = dk_ref[:, lo:mid].astype(BF16)
            ko_ref[:, mid:hi] = jnp.zeros((ko_ref.shape[0], hi - mid), BF16)
            part = dk_ref[:, mid:hi]
            dkr = part if dkr is None else dkr + part
        ro_ref[...] = _rope_t(dkr, tc, ts1, ts2).astype(BF16)

    return _rowcall(body, name, rows,[dq, dk, *tabs], [],
                    [(w, BF16), (w, BF16), (LANES, BF16)])


def _visible(q0, k0, tq, tk):
    qrow = q0 + lax.broadcasted_iota(jnp.int32, (tq, tk), 0)
    kcol = k0 + lax.broadcasted_iota(jnp.int32, (tq, tk), 1)
    return ((kcol + CHUNK_BIAS) >> CHUNK_SHIFT) <= ((qrow + CHUNK_BIAS) >> CHUNK_SHIFT)


def _visible_t(k0, q0, tk, tq):
    krow = k0 + lax.broadcasted_iota(jnp.int32, (tk, tq), 0)
    qcol = q0 + lax.broadcasted_iota(jnp.int32, (tk, tq), 1)
    return ((krow + CHUNK_BIAS) >> CHUNK_SHIFT) <= ((qcol + CHUNK_BIAS) >> CHUNK_SHIFT)


def _lanes(v, width):
    return jnp.tile(v, (1, width // LANES))


def _pipelined_chunks(n_full, scores, absorb):
    scores(0, 0)

    def pair(jj, carry):
        a = 2 * jj
        scores(a + 1, 1)
        absorb(a, 0, False)
        scores(a + 2, 0)
        absorb(a + 1, 1, False)
        return carry

    lax.fori_loop(0, n_full // 2, pair, 0)

    @pl.when(n_full % 2 == 0)
    def _():
        absorb(n_full, 0, True)

    @pl.when(n_full % 2 == 1)
    def _():
        scores(n_full, 1)
        absorb(n_full - 1, 0, False)
        absorb(n_full, 1, True)


def _attn_fwd(q, k, v, t, name):
    lp = q.shape[0]
    nt = lp // t

    def body(q_ref, k_ref, v_ref, o_ref, lse_ref, m_s, l_s, acc_s, a_s, sa_s, sb_s, p_s):
        i = pl.program_id(1)
        m_s[...] = jnp.full(m_s.shape, NEG, F32)
        l_s[...] = jnp.zeros(l_s.shape, F32)
        acc_s[...] = jnp.zeros(acc_s.shape, F32)

        s_bufs = (sa_s, sb_s)

        def scores(j, slot):
            r0 = pl.multiple_of(j * t, t)
            s_bufs[slot][...] = lax.dot_general(q_ref[...], k_ref[pl.ds(r0, t), :], NT,
                                                preferred_element_type=F32)

        def absorb(j, slot, masked):
            for r in range(0, t, ROW_GROUP):
                rows = slice(r, r + ROW_GROUP)
                s = s_bufs[slot][rows, :]
                if masked:
                    s = jnp.where(_visible(i * t + r, i * t, ROW_GROUP, t), s, NEG)
                m_prev = m_s[rows, :]
                m_new = jnp.maximum(m_prev, jnp.max(s, axis=1, keepdims=True))
                alpha = jnp.exp2((m_prev - m_new) * SCALE_LOG2E)
                p = jnp.exp2((s - _lanes(m_new, t)) * SCALE_LOG2E)
                l_s[rows, :] = alpha * l_s[rows, :] + jnp.sum(p, axis=1, keepdims=True)
                m_s[rows, :] = m_new
                a_s[rows, :] = alpha
                p_s[rows, :] = p.astype(BF16)
            r0 = pl.multiple_of(j * t, t)
            acc_s[...] = a_s[...] * acc_s[...] + jnp.dot(p_s[...], v_ref[pl.ds(r0, t), :],
                                                         preferred_element_type=F32)

        @pl.when(i + 1 < nt)
        def _():
            r1 = pl.multiple_of((i + 1) * t, t)
            rows = slice(t - SPILL, t)
            s = lax.dot_general(q_ref[rows, :], k_ref[pl.ds(r1, SPILL), :], NT, preferred_element_type=F32)
            s = jnp.where(_visible(i * t + t - SPILL, (i + 1) * t, SPILL, SPILL), s, NEG)
            m_prev = m_s[rows, :]
            m_new = jnp.maximum(m_prev, jnp.max(s, axis=1, keepdims=True))
            alpha = jnp.exp2((m_prev - m_new) * SCALE_LOG2E)
            p = jnp.exp2((s - m_new) * SCALE_LOG2E)
            l_s[rows, :] = alpha * l_s[rows, :] + jnp.sum(p, axis=1, keepdims=True)
            acc_s[rows, :] = alpha * acc_s[rows, :] + jnp.dot(p.astype(BF16), v_ref[pl.ds(r1, SPILL), :],
                                                              preferred_element_type=F32)
            m_s[rows, :] = m_new

        _pipelined_chunks(i, scores, absorb)
        o_ref[...] = (acc_s[...] / l_s[...]).astype(BF16)
        lse_ref[...] = m_s[...] * ATTN_SCALE + jnp.log(l_s[...])

    return pl.pallas_call(
        body, name=name, grid=(N_HEADS, nt),
        in_specs=[pl.BlockSpec((t, HEAD_PAD), lambda h, i: (i, h)),
                  pl.BlockSpec((lp, HEAD_PAD), lambda h, i: (0, h)),
                  pl.BlockSpec((lp, V_DIM), lambda h, i: (0, h))],
        out_specs=[pl.BlockSpec((t, V_DIM), lambda h, i: (i, h)),
                   pl.BlockSpec((None, t, LANES), lambda h, i: (h, i, 0))],
        out_shape=[jax.ShapeDtypeStruct((lp, N_HEADS * V_DIM), BF16),
                   jax.ShapeDtypeStruct((N_HEADS, lp, LANES), F32)],
        scratch_shapes=[pltpu.VMEM((t, LANES), F32), pltpu.VMEM((t, LANES), F32), pltpu.VMEM((t, V_DIM), F32),
                        pltpu.VMEM((t, LANES), F32), pltpu.VMEM((t, t), F32), pltpu.VMEM((t, t), F32),
                        pltpu.VMEM((t, t), BF16)],
        compiler_params=_params(("parallel", "arbitrary")),
    )(q, k, v)


def _attn_bwd_dq(q, k, v, do, o, lse, dep, t, name):
    lp = q.shape[0]
    nt = lp // t

    def body(q_ref, k_ref, v_ref, do_ref, o_ref, lse_ref, dep_ref, dq_ref, l2row_ref, dlrow_ref,
             acc_s, l2_s, dl_s, sa_s, sb_s, da_s, db_s, ds_s):
        i = pl.program_id(1)
        delta = jnp.sum(do_ref[...].astype(F32) * o_ref[...].astype(F32), axis=1, keepdims=True)
        dl_s[...] = jnp.broadcast_to(delta, dl_s.shape)
        l2_s[...] = lse_ref[...] * LOG2E
        l2row_ref[...] = l2_s[...].T[0:SUBLANES, :]
        dlrow_ref[...] = dl_s[...].T[0:SUBLANES, :]
        acc_s[...] = jnp.zeros(acc_s.shape, F32)
        s_bufs, d_bufs = (sa_s, sb_s), (da_s, db_s)

        def dscores(s, dp, rows, width):
            p = jnp.exp2(s * SCALE_LOG2E - _lanes(l2_s[rows, :], width))
            return (p * (dp - _lanes(dl_s[rows, :], width))).astype(BF16)

        @pl.when(i + 1 < nt)
        def _():
            r1 = pl.multiple_of((i + 1) * t, t)
            rows = slice(t - SPILL, t)
            ks, vs = k_ref[pl.ds(r1, SPILL), :], v_ref[pl.ds(r1, SPILL), :]
            s = lax.dot_general(q_ref[rows, :], ks, NT, preferred_element_type=F32)
            s = jnp.where(_visible(i * t + t - SPILL, (i + 1) * t, SPILL, SPILL), s, NEG)
            dp = lax.dot_general(do_ref[rows, :], vs, NT, preferred_element_type=F32)
            acc_s[rows, :] += jnp.dot(dscores(s, dp, rows, SPILL), ks, preferred_element_type=F32)

        def scores(j, slot):
            r0 = pl.multiple_of(j * t, t)
            s_bufs[slot][...] = lax.dot_general(q_ref[...], k_ref[pl.ds(r0, t), :], NT,
                                                preferred_element_type=F32)
            d_bufs[slot][...] = lax.dot_general(do_ref[...], v_ref[pl.ds(r0, t), :], NT,
                                                preferred_element_type=F32)

        def absorb(j, slot, masked):
            for r in range(0, t, ROW_GROUP):
                rows = slice(r, r + ROW_GROUP)
                s = s_bufs[slot][rows, :]
                if masked:
                    s = jnp.where(_visible(i * t + r, i * t, ROW_GROUP, t), s, NEG)
                ds_s[rows, :] = dscores(s, d_bufs[slot][rows, :], rows, t)
            r0 = pl.multiple_of(j * t, t)
            acc_s[...] += jnp.dot(ds_s[...], k_ref[pl.ds(r0, t), :], preferred_element_type=F32)

        _pipelined_chunks(i, scores, absorb)
        dq_ref[...] = acc_s[...] * ATTN_SCALE

    stat_row = pl.BlockSpec((None, None, SUBLANES, t), lambda h, i: (h, i, 0, 0))
    return pl.pallas_call(
        body, name=name, grid=(N_HEADS, nt),
        in_specs=[pl.BlockSpec((t, HEAD_PAD), lambda h, i: (i, h)),
                  pl.BlockSpec((lp, HEAD_PAD), lambda h, i: (0, h)),
                  pl.BlockSpec((lp, V_DIM), lambda h, i: (0, h)),
                  pl.BlockSpec((t, V_DIM), lambda h, i: (i, h)),
                  pl.BlockSpec((t, V_DIM), lambda h, i: (i, h)),
                  pl.BlockSpec((None, t, LANES), lambda h, i: (h, i, 0)),
                  pl.BlockSpec(dep.shape, lambda h, i: (0, 0))],
        out_specs=[pl.BlockSpec((t, HEAD_PAD), lambda h, i: (i, h)), stat_row, stat_row],
        out_shape=[jax.ShapeDtypeStruct((lp, N_HEADS * HEAD_PAD), F32),
                   jax.ShapeDtypeStruct((N_HEADS, nt, SUBLANES, t), F32),
                   jax.ShapeDtypeStruct((N_HEADS, nt, SUBLANES, t), F32)],
        scratch_shapes=[pltpu.VMEM((t, HEAD_PAD), F32), pltpu.VMEM((t, LANES), F32), pltpu.VMEM((t, LANES), F32),
                        pltpu.VMEM((t, t), F32), pltpu.VMEM((t, t), F32), pltpu.VMEM((t, t), F32),
                        pltpu.VMEM((t, t), F32), pltpu.VMEM((t, t), BF16)],
        compiler_params=_params(("parallel", "arbitrary")),
    )(q, k, v, do, o, lse, dep)


def _attn_bwd_dkv(q, k, v, do, l2row, dlrow, t, name):
    lp = q.shape[0]
    nt = lp // t

    def body(q_ref, k_ref, v_ref, do_ref, l2_ref, dl_ref, dk_ref, dv_ref,
             dk_s, dv_s, sa_s, sb_s, da_s, db_s, p_s, ds_s):
        j = pl.program_id(1)
        dk_s[...] = jnp.zeros(dk_s.shape, F32)
        dv_s[...] = jnp.zeros(dv_s.shape, F32)
        s_bufs, d_bufs = (sa_s, sb_s), (da_s, db_s)

        def weights(st, dpt, l2r, dlr):
            pt = jnp.exp2(st * SCALE_LOG2E - l2r)
            return pt.astype(BF16), (pt * (dpt - dlr)).astype(BF16)

        @pl.when(j > 0)
        def _():
            q0 = pl.multiple_of(j * t - SPILL, SPILL)
            rows = slice(0, SPILL)
            qs, dos = q_ref[pl.ds(q0, SPILL), :], do_ref[pl.ds(q0, SPILL), :]
            st = lax.dot_general(k_ref[rows, :], qs, NT, preferred_element_type=F32)
            st = jnp.where(_visible_t(j * t, j * t - SPILL, SPILL, SPILL), st, NEG)
            dpt = lax.dot_general(v_ref[rows, :], dos, NT, preferred_element_type=F32)
            pt, dst = weights(st, dpt, l2_ref[j - 1, 0:1, t - SPILL:], dl_ref[j - 1, 0:1, t - SPILL:])
            dv_s[rows, :] += jnp.dot(pt, dos, preferred_element_type=F32)
            dk_s[rows, :] += jnp.dot(dst, qs, preferred_element_type=F32)

        def scores(c, slot):
            r0 = pl.multiple_of((nt - 1 - c) * t, t)
            s_bufs[slot][...] = lax.dot_general(k_ref[...], q_ref[pl.ds(r0, t), :], NT,
                                                preferred_element_type=F32)
            d_bufs[slot][...] = lax.dot_general(v_ref[...], do_ref[pl.ds(r0, t), :], NT,
                                                preferred_element_type=F32)

        def absorb(c, slot, masked):
            i = nt - 1 - c
            l2r, dlr = l2_ref[i, 0:1, :], dl_ref[i, 0:1, :]
            for r in range(0, t, ROW_GROUP):
                rows = slice(r, r + ROW_GROUP)
                st = s_bufs[slot][rows, :]
                if masked:
                    st = jnp.where(_visible_t(j * t + r, j * t, ROW_GROUP, t), st, NEG)
                p_s[rows, :], ds_s[rows, :] = weights(st, d_bufs[slot][rows, :], l2r, dlr)
            r0 = pl.multiple_of(i * t, t)
            dv_s[...] += jnp.dot(p_s[...], do_ref[pl.ds(r0, t), :], preferred_element_type=F32)
            dk_s[...] += jnp.dot(ds_s[...], q_ref[pl.ds(r0, t), :], preferred_element_type=F32)

        _pipelined_chunks(nt - 1 - j, scores, absorb)
        dk_ref[...] = dk_s[...] * ATTN_SCALE
        dv_ref[...] = dv_s[...].astype(BF16)

    stat_rows = pl.BlockSpec((None, nt, SUBLANES, t), lambda h, j: (h, 0, 0, 0))
    return pl.pallas_call(
        body, name=name, grid=(N_HEADS, nt),
        in_specs=[pl.BlockSpec((lp, HEAD_PAD), lambda h, j: (0, h)),
                  pl.BlockSpec((t, HEAD_PAD), lambda h, j: (j, h)),
                  pl.BlockSpec((t, V_DIM), lambda h, j: (j, h)),
                  pl.BlockSpec((lp, V_DIM), lambda h, j: (0, h)),
                  stat_rows, stat_rows],
        out_specs=[pl.BlockSpec((t, HEAD_PAD), lambda h, j: (j, h)),
                   pl.BlockSpec((t, V_DIM), lambda h, j: (j, h))],
        out_shape=[jax.ShapeDtypeStruct((lp, N_HEADS * HEAD_PAD), F32),
                   jax.ShapeDtypeStruct((lp, N_HEADS * V_DIM), BF16)],
        scratch_shapes=[pltpu.VMEM((t, HEAD_PAD), F32), pltpu.VMEM((t, V_DIM), F32),
                        pltpu.VMEM((t, t), F32), pltpu.VMEM((t, t), F32), pltpu.VMEM((t, t), F32),
                        pltpu.VMEM((t, t), F32), pltpu.VMEM((t, t), BF16), pltpu.VMEM((t, t), BF16)],
        compiler_params=_params(("parallel", "arbitrary")),
    )(q, k, v, do, l2row, dlrow)


def _shift_down(cur, prev8, k):
    r = pltpu.roll(cur, k, 0)
    row8 = lax.broadcasted_iota(jnp.int32, prev8.shape, 0)
    first = jnp.where(row8 < k, pltpu.roll(prev8, k, 0), r[0:SUBLANES])
    return jnp.concatenate([first, r[SUBLANES:]], axis=0)


def _shift_up(cur, next8, k):
    t = cur.shape[0]
    r = pltpu.roll(cur, t - k, 0)
    row8 = lax.broadcasted_iota(jnp.int32, next8.shape, 0)
    last = jnp.where(row8 >= SUBLANES - k, pltpu.roll(next8, SUBLANES - k, 0), r[t - SUBLANES:])
    return jnp.concatenate([r[:t - SUBLANES], last], axis=0)


def _scan_down(a, b):
    t = a.shape[0]
    row = lax.broadcasted_iota(jnp.int32, a.shape, 0)
    s = 1
    while s < t:
        keep = row >= s
        a_sh = jnp.where(keep, pltpu.roll(a, s, 0), 1.0)
        b_sh = jnp.where(keep, pltpu.roll(b, s, 0), 0.0)
        b = a * b_sh + b
        a = a * a_sh
        s *= 2
    return a, b


def _scan_up(a, b):
    t = a.shape[0]
    row = lax.broadcasted_iota(jnp.int32, a.shape, 0)
    s = 1
    while s < t:
        keep = row < t - s
        a_sh = jnp.where(keep, pltpu.roll(a, t - s, 0), 1.0)
        b_sh = jnp.where(keep, pltpu.roll(b, t - s, 0), 0.0)
        b = a * b_sh + b
        a = a * a_sh
        s *= 2
    return a, b


def _neg_expm1(y):
    series = -y * (1.0 + y * (0.5 + y * (1.0 / 6.0 + y * (1.0 / 24.0 + y * (1.0 / 120.0)))))
    return jnp.where(y > -0.1, series, 1.0 - jnp.exp(y))


def _log_sigmoid(x):
    return jnp.minimum(x, 0.0) - jnp.log(1.0 + jnp.exp(-jnp.abs(x)))


GELU_C = math.sqrt(2.0 / math.pi)
GELU_K = 0.044715


def _gelu(x):
    th = jnp.tanh(GELU_C * (x + GELU_K * x * x * x))
    return 0.5 * x * (1.0 + th), th


def _block_mm(xb, w_ref, dims):
    rb = D_RNN // RNN_BLOCKS
    return jnp.concatenate(
        [lax.dot_general(xb[:, h * rb:(h + 1) * rb], w_ref[h], dims, preferred_element_type=F32)
         for h in range(RNN_BLOCKS)], axis=1)


def _rglru_gates(ux, prev8, pv_ref, wa_ref, wi_ref):
    shifted = [ux] + [_shift_down(ux, prev8, k) for k in range(1, CONV_WIDTH)]
    xc = pv_ref[4:5, :] + pv_ref[3:4, :] * ux
    for k in range(1, CONV_WIDTH):
        xc = xc + pv_ref[3 - k:4 - k, :] * shifted[k]
    xcb = xc.astype(BF16)
    r_g = _sig(_block_mm(xcb, wa_ref, NN) + pv_ref[5:6, :])
    i_g = _sig(_block_mm(xcb, wi_ref, NN) + pv_ref[6:7, :])
    log_a = LRU_C * r_g * _log_sigmoid(pv_ref[7:8, :])
    a = jnp.exp(log_a)
    mm = jnp.sqrt(_neg_expm1(2.0 * log_a))
    return dict(shifted=shifted, xc=xc, xcb=xcb, r=r_g, i=i_g, a=a, mm=mm)


def _rglru_fwd(ux, ug, pv, wa, wi, t, name):
    lp, d = ux.shape

    def body(ux_ref, ug_ref, pv_ref, wa_ref, wi_ref, y_ref, h_ref, tail_s, hc_s):
        @pl.when(pl.program_id(0) == 0)
        def _():
            tail_s[...] = jnp.zeros_like(tail_s)
            hc_s[...] = jnp.zeros_like(hc_s)

        uxv = ux_ref[...]
        gt = _rglru_gates(uxv, tail_s[...], pv_ref, wa_ref, wi_ref)
        tail_s[...] = ux_ref[t - SUBLANES:t, :]
        cum_a, hloc = _scan_down(gt["a"], gt["mm"] * (gt["i"] * gt["xc"]))
        h_ref[...] = hloc + cum_a * hc_s[0:1, :]
        hc_s[...] = h_ref[t - SUBLANES:t, :]
        hc_s[0:1, :] = h_ref[t - 1:t, :]
        y_ref[...] = (h_ref[...] * _gelu(ug_ref[...])[0]).astype(BF16)

    tile = pl.BlockSpec((t, d), lambda i: (i, 0))
    return pl.pallas_call(
        body, name=name, grid=(lp // t,),
        in_specs=[tile, tile, pl.BlockSpec(pv.shape, lambda i: (0, 0)),
                  pl.BlockSpec(wa.shape, lambda i: (0, 0, 0)), pl.BlockSpec(wi.shape, lambda i: (0, 0, 0))],
        out_specs=[tile, tile],
        out_shape=[jax.ShapeDtypeStruct((lp, d), BF16), jax.ShapeDtypeStruct((lp, d), F32)],
        scratch_shapes=[pltpu.VMEM((SUBLANES, d), F32), pltpu.VMEM((SUBLANES, d), F32)],
        compiler_params=_params(("arbitrary",)),
    )(ux, ug, pv, wa, wi)


def _rglru_bwd(ux, ug, hs, dy, pv, wa, wi, dep, t, name):
    lp, d = ux.shape
    nt = lp // t
    per = t // SUBLANES
    rb = d // RNN_BLOCKS

    def body(ux_ref, uxp_ref, ug_ref, h_ref, hp_ref, dy_ref, pv_ref, wa_ref, wi_ref, dep_ref,
             dux_ref, dug_ref, dpv_ref, dwa_ref, dwi_ref, ca_s, cg_s, cx_s):
        step = pl.program_id(0)
        first_tile = step == nt - 1

        @pl.when(step == 0)
        def _():
            for ref in (ca_s, cg_s, cx_s, dpv_ref, dwa_ref, dwi_ref):
                ref[...] = jnp.zeros_like(ref)

        uxv = ux_ref[...]
        prev8 = jnp.where(first_tile, 0.0, uxp_ref[...])
        hprev8 = jnp.where(first_tile, 0.0, hp_ref[...])
        gt = _rglru_gates(uxv, prev8, pv_ref, wa_ref, wi_ref)
        a, mm, r_g, i_g, xc = gt["a"], gt["mm"], gt["r"], gt["i"], gt["xc"]
        hv = h_ref[...]
        hprev = _shift_down(hv, hprev8, 1)
        ugv, dyv = ug_ref[...], dy_ref[...]
        gel, th = _gelu(ugv)
        dgel = 0.5 * (1.0 + th) + 0.5 * ugv * (1.0 - th * th) * (GELU_C * (1.0 + 3.0 * GELU_K * ugv * ugv))
        dug_ref[...] = (dyv * hv * dgel).astype(BF16)
        a_up = _shift_up(a, ca_s[...], 1)
        cum_a, gloc = _scan_up(a_up, dyv * gel)
        gv = gloc + cum_a * cg_s[0:1, :]
        ca_s[...] = a[0:SUBLANES]
        cg_s[...] = gv[0:SUBLANES]
        ixc = i_g * xc
        d_ixc = gv * mm
        d_log_a = gv * hprev * a - (gv * ixc) * (a * a) / mm
        logsig = _log_sigmoid(pv_ref[7:8, :])
        d_pre_a = d_log_a * (LRU_C * logsig) * r_g * (1.0 - r_g)
        d_pre_i = d_ixc * xc * i_g * (1.0 - i_g)
        dab, dib = d_pre_a.astype(BF16), d_pre_i.astype(BF16)
        d_xc = d_ixc * i_g + _block_mm(dab, wa_ref, NT) + _block_mm(dib, wi_ref, NT)
        xcb = gt["xcb"]
        for h in range(RNN_BLOCKS):
            cols = slice(h * rb, (h + 1) * rb)
            dwa_ref[h] += lax.dot_general(xcb[:, cols], dab[:, cols], TN, preferred_element_type=F32)
            dwi_ref[h] += lax.dot_general(xcb[:, cols], dib[:, cols], TN, preferred_element_type=F32)
        csum = lambda v: jnp.sum(v, axis=0, keepdims=True)
        for k in range(CONV_WIDTH):
            dpv_ref[3 - k:4 - k, :] += csum(d_xc * gt["shifted"][k])
        dpv_ref[4:5, :] += csum(d_xc)
        dpv_ref[5:6, :] += csum(d_pre_a)
        dpv_ref[6:7, :] += csum(d_pre_i)
        dpv_ref[7:8, :] += csum(d_log_a * (LRU_C * r_g)) * _sig(-pv_ref[7:8, :])
        dux = pv_ref[3:4, :] * d_xc
        for k in range(1, CONV_WIDTH):
            dux = dux + pv_ref[3 - k:4 - k, :] * _shift_up(d_xc, cx_s[...], k)
        cx_s[...] = d_xc[0:SUBLANES]
        dux_ref[...] = dux.astype(BF16)

    rev = lambda i: (nt - 1 - i, 0)
    before = lambda i: (jnp.maximum((nt - 1 - i) * per - 1, 0), 0)
    tile = pl.BlockSpec((t, d), rev)
    tail = pl.BlockSpec((SUBLANES, d), before)
    fixed2 = lambda arr: pl.BlockSpec(arr.shape, lambda i: (0, 0))
    fixed3 = lambda arr: pl.BlockSpec(arr.shape, lambda i: (0, 0, 0))
    return pl.pallas_call(
        body, name=name, grid=(nt,),
        in_specs=[tile, tail, tile, tile, tail, tile, fixed2(pv), fixed3(wa), fixed3(wi), fixed2(dep)],
        out_specs=[tile, tile, fixed2(pv), fixed3(wa), fixed3(wi)],
        out_shape=[jax.ShapeDtypeStruct((lp, d), BF16), jax.ShapeDtypeStruct((lp, d), BF16),
                   jax.ShapeDtypeStruct(pv.shape, F32), jax.ShapeDtypeStruct(wa.shape, F32),
                   jax.ShapeDtypeStruct(wi.shape, F32)],
        scratch_shapes=[pltpu.VMEM((SUBLANES, d), F32)] * 3,
        compiler_params=_params(("arbitrary",)),
    )(ux, ux, ug, hs, hs, dy, pv, wa, wi, dep)


def _adamw(w, m, v, parts, dep, name):
    rows, cols = w.shape
    tr = _pick(rows, 256, SUBLANES)
    c1 = 1.0 / (1.0 - ADAM_B1 ** ADAM_STEP)
    c2 = 1.0 / (1.0 - ADAM_B2 ** ADAM_STEP)

    def body(w_ref, m_ref, v_ref, p_ref, dep_ref, g_ref, d_ref, mo_ref, vo_ref):
        g = p_ref[0].astype(F32)
        for q in range(1, N_DEV):
            g = g + p_ref[q].astype(F32)
        mn = ADAM_B1 * m_ref[...] + (1.0 - ADAM_B1) * g
        vn = ADAM_B2 * v_ref[...] + (1.0 - ADAM_B2) * (g * g)
        g_ref[...] = g
        mo_ref[...] = mn
        vo_ref[...] = vn
        d_ref[...] = -ADAM_LR * ((mn * c1) / (jnp.sqrt(vn * c2) + ADAM_EPS) + ADAM_WD * w_ref[...])

    blk = pl.BlockSpec((tr, cols), lambda i: (i, 0))
    return pl.pallas_call(
        body, name=name, grid=(rows // tr,),
        in_specs=[blk, blk, blk, pl.BlockSpec((N_DEV, tr, cols), lambda i: (0, i, 0)),
                  pl.BlockSpec(dep.shape, lambda i: (0, 0))],
        out_specs=[blk] * 4, out_shape=[jax.ShapeDtypeStruct((rows, cols), F32)] * 4,
        compiler_params=_params(("parallel",)),
    )(w, m, v, parts, dep)


WEIGHTS = ("meta_tokens", "norm_mix_g", "w_in", "b_gate", "conv_w", "conv_b", "w_rec_a", "b_rec_a", "w_rec_i",
           "b_rec_i", "lru_lambda", "q_norm_g", "w_uq", "kv_norm_g", "w_ukv", "w_branch", "w_out", "norm_ffn_g",
           "w_ffn_in", "w_ffn_out", "final_norm_g")
SHARDED = {"meta_tokens": True, "w_in": True, "b_gate": True, "conv_w": True, "w_uq": True, "w_ukv": True,
           "w_branch": False, "w_out": False, "w_ffn_in": True, "w_ffn_out": False}
REPLICATED = tuple(n for n in WEIGHTS if n not in SHARDED)


def _as2d(a):
    return a.reshape(-1, a.shape[-1])


def _full_from_gathered(g, by_cols):
    if by_cols:
        return jnp.transpose(g, (1, 0, 2)).reshape(g.shape[1], N_DEV * g.shape[2])
    return g.reshape(N_DEV * g.shape[1], g.shape[2])


def _blocks_from_full(full, by_cols):
    if by_cols:
        r, c = full.shape
        return jnp.transpose(full.reshape(r, N_DEV, c // N_DEV), (1, 0, 2))
    return full.reshape(N_DEV, full.shape[0] // N_DEV, full.shape[1])


def _pack(arrs):
    flat = jnp.concatenate([a.reshape(-1) for a in arrs])
    rows = -(-flat.shape[0] // (LANES * SUBLANES)) * SUBLANES
    return jnp.pad(flat, (0, rows * LANES - flat.shape[0])).reshape(rows, LANES)


def _unpack(packed, like):
    flat = packed.reshape(-1)
    out, off = [], 0
    for a in like:
        out.append(flat[off:off + a.size].reshape(a.shape))
        off += a.size
    return out


def kernel(x, meta_tokens, norm_mix_g, w_in, b_gate, conv_w, conv_b, w_rec_a, b_rec_a, w_rec_i, b_rec_i, lru_lambda, q_norm_g, w_uq, kv_norm_g, w_ukv, w_branch, w_out, norm_ffn_g, w_ffn_in, w_ffn_out, final_norm_g, loss_target, m_meta_tokens, m_norm_mix_g, m_w_in, m_b_gate, m_conv_w, m_conv_b, m_w_rec_a, m_b_rec_a, m_w_rec_i, m_b_rec_i, m_lru_lambda, m_q_norm_g, m_w_uq, m_kv_norm_g, m_w_ukv, m_w_branch, m_w_out, m_norm_ffn_g, m_w_ffn_in, m_w_ffn_out, m_final_norm_g, v_meta_tokens, v_norm_mix_g, v_w_in, v_b_gate, v_conv_w, v_conv_b, v_w_rec_a, v_b_rec_a, v_w_rec_i, v_b_rec_i, v_lru_lambda, v_q_norm_g, v_w_uq, v_kv_norm_g, v_w_ukv, v_w_branch, v_w_out, v_norm_ffn_g, v_w_ffn_in, v_w_ffn_out, v_final_norm_g):
    w = dict(meta_tokens=meta_tokens, norm_mix_g=norm_mix_g, w_in=w_in, b_gate=b_gate, conv_w=conv_w, conv_b=conv_b,
             w_rec_a=w_rec_a, b_rec_a=b_rec_a, w_rec_i=w_rec_i, b_rec_i=b_rec_i, lru_lambda=lru_lambda,
             q_norm_g=q_norm_g, w_uq=w_uq, kv_norm_g=kv_norm_g, w_ukv=w_ukv, w_branch=w_branch, w_out=w_out,
             norm_ffn_g=norm_ffn_g, w_ffn_in=w_ffn_in, w_ffn_out=w_ffn_out, final_norm_g=final_norm_g)
    m = dict(meta_tokens=m_meta_tokens, norm_mix_g=m_norm_mix_g, w_in=m_w_in, b_gate=m_b_gate, conv_w=m_conv_w,
             conv_b=m_conv_b, w_rec_a=m_w_rec_a, b_rec_a=m_b_rec_a, w_rec_i=m_w_rec_i, b_rec_i=m_b_rec_i,
             lru_lambda=m_lru_lambda, q_norm_g=m_q_norm_g, w_uq=m_w_uq, kv_norm_g=m_kv_norm_g, w_ukv=m_w_ukv,
             w_branch=m_w_branch, w_out=m_w_out, norm_ffn_g=m_norm_ffn_g, w_ffn_in=m_w_ffn_in,
             w_ffn_out=m_w_ffn_out, final_norm_g=m_final_norm_g)
    v = dict(meta_tokens=v_meta_tokens, norm_mix_g=v_norm_mix_g, w_in=v_w_in, b_gate=v_b_gate, conv_w=v_conv_w,
             conv_b=v_conv_b, w_rec_a=v_w_rec_a, b_rec_a=v_b_rec_a, w_rec_i=v_w_rec_i, b_rec_i=v_b_rec_i,
             lru_lambda=v_lru_lambda, q_norm_g=v_q_norm_g, w_uq=v_w_uq, kv_norm_g=v_kv_norm_g, w_ukv=v_w_ukv,
             w_branch=v_w_branch, w_out=v_w_out, norm_ffn_g=v_norm_ffn_g, w_ffn_in=v_w_ffn_in,
             w_ffn_out=v_w_ffn_out, final_norm_g=v_final_norm_g)

    seq, d_model = x.shape[1], x.shape[2]
    length = N_META + seq
    lp = -(-length // LANES) * LANES
    t_attn = _pick(lp, 640)
    t_rnn = LANES

    small = ("meta_tokens", "b_gate", "conv_w")
    names = list(SHARDED)
    first, mid, late = ("meta_tokens", "b_gate", "conv_w", "w_in"), ("w_uq", "w_ukv", "w_branch", "w_out"), (
        "w_ffn_in", "w_ffn_out")
    payload = lambda n: _as2d(w[n]) if n in small else _as2d(w[n]).astype(BF16)
    got = _exchange([payload(n) for n in first], [True] * len(first), "gather_first")
    mid_h = _exchange_start([payload(n) for n in mid], [True] * len(mid), got[0], "gather_mid_start")
    late_h = _exchange_start([payload(n) for n in late], [True] * len(late), mid_h["token"], "gather_late_start")
    full = {n: _full_from_gathered(g, SHARDED[n]) for n, g in zip(first, got)}

    splits = (D_RNN, D_RNN, Q_RANK, KV_RANK, QK_ROPE, 2 * d_model)
    offs = [0]
    for s in splits:
        offs.append(offs[-1] + s)
    w_x, w_g, w_q, w_kv, w_kr, w_m = (full["w_in"][:, offs[s]:offs[s + 1]] for s in range(6))
    w_kr = jnp.pad(w_kr, ((0, 0), (0, LANES - QK_ROPE)))
    bg = full["b_gate"].reshape(1, 2 * d_model)
    pv = jnp.concatenate([full["conv_w"], conv_b, b_rec_a, b_rec_i, lru_lambda], axis=0)
    wa_b, wi_b = w_rec_a[0].astype(BF16), w_rec_i[0].astype(BF16)
    g_final = final_norm_g.reshape(1, d_model)

    h0 = jnp.concatenate([full["meta_tokens"], x[0], jnp.zeros((lp - length, d_model), F32)], axis=0)
    tgt = jnp.pad(loss_target[0], ((N_META, lp - length), (0, 0)))
    tabs = _rope_tables(lp)

    z = _rmsnorm_fwd(h0, norm_mix_g, "norm_mix")
    ux = _mm([(z, w_x)], "nn", "in_x")
    ug = _mm([(z, w_g)], "nn", "in_g")
    uq = _mm([(z, w_q)], "nn", "in_q")
    ukv = _mm([(z, w_kv)], "nn", "in_kv")
    ukr = _mm([(z, w_kr)], "nn", "in_kr")
    um = _mm([(z, w_m)], "nn", "in_m")
    for n, g in zip(mid, _exchange_wait(mid_h, um, "gather_mid_wait")):
        full[n] = _full_from_gathered(g, SHARDED[n])
    w_uq_pad = jnp.pad(full["w_uq"].reshape(Q_RANK, N_HEADS, QK_NOPE + QK_ROPE),
                       ((0, 0), (0, 0), (0, HEAD_PAD - QK_NOPE - QK_ROPE))).reshape(Q_RANK, N_HEADS * HEAD_PAD)
    w_ukv3 = full["w_ukv"].reshape(KV_RANK, N_HEADS, QK_NOPE + V_DIM)
    w_k_pad = jnp.pad(w_ukv3[:, :, :QK_NOPE], ((0, 0), (0, 0), (0, HEAD_PAD - QK_NOPE))).reshape(
        KV_RANK, N_HEADS * HEAD_PAD)
    w_v = w_ukv3[:, :, QK_NOPE:].reshape(KV_RANK, N_HEADS * V_DIM)
    wb_r, wb_a = full["w_branch"][:D_RNN], full["w_branch"][D_RNN:]
    y_rnn, hs = _rglru_fwd(ux, ug, pv, wa_b, wi_b, t_rnn, "rglru_fwd")
    qn = _rmsnorm_fwd(uq, q_norm_g, "norm_q")
    kvn = _rmsnorm_fwd(ukv, kv_norm_g, "norm_kv")
    qpad = _mm([(qn, w_uq_pad)], "nn", "up_q")
    kpad = _mm([(kvn, w_k_pad)], "nn", "up_k")
    vh = _mm([(kvn, w_v)], "nn", "up_v", out_dtype=BF16)
    qh, kh = _rope_fwd(qpad, kpad, ukr, tabs, "rope_fwd")
    oh, lse = _attn_fwd(qh, kh, vh, t_attn, "attn_fwd")
    p_rnn = _mm([(y_rnn, wb_r)], "nn", "branch_rnn")
    p_att = _mm([(oh, wb_a)], "nn", "branch_att")
    mixed = _mix_fwd(um, p_rnn, p_att, bg, "mix_fwd")
    h1 = _mm([(mixed, full["w_out"])], "nn", "out_proj", res=h0)
    for n, g in zip(late, _exchange_wait(late_h, h1, "gather_late_wait")):
        full[n] = _full_from_gathered(g, SHARDED[n])
    zf = _rmsnorm_fwd(h1, norm_ffn_g, "norm_ffn")
    gu = _mm([(zf, full["w_ffn_in"])], "nn", "ffn_in")
    act = _swiglu_fwd(gu, "swiglu_fwd")
    h2 = _mm([(act, full["w_ffn_out"])], "nn", "ffn_out", res=h1)
    dh2, dg_final, _, loss_part = _loss_bwd(h2, tgt, g_final, seq, "loss_bwd")

    d_act = _mm([(dh2, full["w_ffn_out"])], "nt", "d_act")
    dw_ffn_out = _mm_tn(act, dh2, "dw_ffn_out")
    d_gu = _swiglu_bwd(gu, d_act, "swiglu_bwd")
    dw_ffn_in = _mm_tn(zf, d_gu, "dw_ffn_in")
    blocks = lambda n, g: _blocks_from_full(g, SHARDED[n]).astype(F32 if n in small else BF16)
    sent = {("w_ffn_in", "w_ffn_out"): _exchange_start(
        [blocks("w_ffn_in", dw_ffn_in), blocks("w_ffn_out", dw_ffn_out)], [False] * 2, dg_final, "scatter_ffn_start")}
    d_zf = _mm([(d_gu, full["w_ffn_in"], D_FF, 0), (d_gu, full["w_ffn_in"], D_FF, 1)], "nt", "d_zf")
    dh1, dg_ffn = _rmsnorm_bwd(h1, d_zf, norm_ffn_g, "norm_ffn_bwd", res=dh2,
                               dep=sent[("w_ffn_in", "w_ffn_out")]["token"])
    d_mixed = _mm([(dh1, full["w_out"])], "nt", "d_mixed")
    dw_out = _mm_tn(mixed, dh1, "dw_out")
    d_prnn, d_patt, d_um, dbg = _mix_bwd(um, p_rnn, p_att, d_mixed, bg, "mix_bwd")
    d_yrnn = _mm([(d_prnn, wb_r)], "nt", "d_yrnn")
    d_oh = _mm([(d_patt, wb_a)], "nt", "d_oh", out_dtype=BF16)
    dwb_r = _mm_tn(y_rnn, d_prnn, "dw_branch_rnn")
    dwb_a = _mm_tn(oh, d_patt, "dw_branch_att")
    sent[("w_out", "w_branch")] = _exchange_start(
        [blocks("w_out", dw_out), blocks("w_branch", jnp.concatenate([dwb_r, dwb_a], axis=0))], [False] * 2,
        dg_ffn, "scatter_mix_start")
    dqh, l2row, dlrow = _attn_bwd_dq(qh, kh, vh, d_oh, oh, lse, sent[("w_out", "w_branch")]["token"], t_attn,
                                     "attn_bwd_dq")
    dkh, dvh = _attn_bwd_dkv(qh, kh, vh, d_oh, l2row, dlrow, t_attn, "attn_bwd_dkv")
    dqpad, dkpad, dukr = _rope_bwd(dqh, dkh, tabs, "rope_bwd")
    d_qn = _mm([(dqpad, w_uq_pad)], "nt", "d_qn")
    dw_uq_pad = _mm_tn(qn, dqpad, "dw_uq")
    d_kvn = _mm([(dkpad, w_k_pad), (dvh, w_v)], "nt", "d_kvn")
    dw_k_pad = _mm_tn(kvn, dkpad, "dw_uk")
    dw_v = _mm_tn(kvn, dvh, "dw_uv")
    dw_uq = dw_uq_pad.reshape(Q_RANK, N_HEADS, HEAD_PAD)[:, :, :QK_NOPE + QK_ROPE].reshape(Q_RANK, -1)
    dw_ukv = jnp.concatenate([dw_k_pad.reshape(KV_RANK, N_HEADS, HEAD_PAD)[:, :, :QK_NOPE],
                              dw_v.reshape(KV_RANK, N_HEADS, V_DIM)], axis=2).reshape(KV_RANK, -1)
    sent[("w_uq", "w_ukv")] = _exchange_start([blocks("w_uq", dw_uq), blocks("w_ukv", dw_ukv)], [False] * 2,
                                              dbg, "scatter_attn_start")
    duq, dg_q = _rmsnorm_bwd(uq, d_qn, q_norm_g, "norm_q_bwd", out_dtype=BF16)
    dukv, dg_kv = _rmsnorm_bwd(ukv, d_kvn, kv_norm_g, "norm_kv_bwd", out_dtype=BF16)
    dux, dug, dpv, dwa, dwi = _rglru_bwd(ux, ug, hs, d_yrnn, pv, wa_b, wi_b, sent[("w_uq", "w_ukv")]["token"],
                                         t_rnn, "rglru_bwd")
    d_z = _mm([(dux, w_x), (dug, w_g), (duq, w_q), (dukv, w_kv), (dukr, w_kr), (d_um, w_m)], "nt", "d_z")
    dw_in = jnp.concatenate([
        _mm_tn(z, dux, "dw_in_x"), _mm_tn(z, dug, "dw_in_g"), _mm_tn(z, duq, "dw_in_q"),
        _mm_tn(z, dukv, "dw_in_kv"), _mm_tn(z, dukr, "dw_in_kr")[:, :QK_ROPE], _mm_tn(z, d_um, "dw_in_m")], axis=1)
    dh0, dg_mix = _rmsnorm_bwd(h0, d_z, norm_mix_g, "norm_mix_bwd", res=dh1)

    grad_last = dict(meta_tokens=dh0[:N_META], b_gate=dbg.reshape(2, d_model), conv_w=dpv[:CONV_WIDTH], w_in=dw_in)
    grad_rep = dict(
        norm_mix_g=dg_mix, conv_b=dpv[4:5], w_rec_a=dwa, b_rec_a=dpv[5:6], w_rec_i=dwi, b_rec_i=dpv[6:7],
        lru_lambda=dpv[7:8], q_norm_g=dg_q, kv_norm_g=dg_kv, norm_ffn_g=dg_ffn, final_norm_g=dg_final)

    rep_pack = _pack([grad_rep[n] for n in REPLICATED])
    last_h = _exchange_start([blocks(n, grad_last[n]) for n in first] + [rep_pack],
                             [False] * len(first) + [True], dg_mix, "scatter_last_start")
    grads, deltas, new_m, new_v = {}, {}, {}, {}

    def update(n, parts, dep):
        g2, d2, m2, v2 = _adamw(_as2d(w[n]), _as2d(m[n]), _as2d(v[n]), parts, dep, "adamw_" + n)
        for store, val in ((grads, g2), (deltas, d2), (new_m, m2), (new_v, v2)):
            store[n] = val.reshape(w[n].shape)

    chain = last_h["token"]
    for group, handle in sent.items():
        for n, parts in zip(group, _exchange_wait(handle, chain, "scatter_wait_" + group[0])):
            update(n, parts, chain)
            chain = _as2d(deltas[n])[:SUBLANES, :LANES]
    *recv_last, recv_rep = _exchange_wait(last_h, chain, "scatter_last_wait")
    for n, parts in zip(first, recv_last):
        update(n, parts, last_h["token"])
    rep_like = [w[n] for n in REPLICATED]
    outs = _adamw(_pack(rep_like), _pack([m[n] for n in REPLICATED]), _pack([v[n] for n in REPLICATED]),
                  recv_rep, last_h["token"], "adamw_replicated")
    for store, val in zip((grads, deltas, new_m, new_v), outs):
        for n, a in zip(REPLICATED, _unpack(val, rep_like)):
            store[n] = a

    loss = lax.psum(loss_part[0, 0], MESH_AXES)
    grad_x = dh0[N_META:length][None]
    return (loss, grad_x, *[grads[n] for n in WEIGHTS], *[deltas[n] for n in WEIGHTS],
            *[new_m[n] for n in WEIGHTS], *[new_v[n] for n in WEIGHTS])
```

```python
import functools
import math

import jax
import jax.numpy as jnp
from jax import lax
from jax.experimental import pallas as pl
from jax.experimental.pallas import tpu as pltpu

F32 = jnp.float32
BF16 = jnp.bfloat16

N_DEV = 8
MESH_AXES = ("x", "y", "c")
LANES = 128
SUBLANES = 8
VMEM_LIMIT = 56 * 1024 * 1024

N_META = 16
CHUNK_SHIFT = 6
CHUNK_BIAS = 64 - N_META
EPS = 1e-6
D_RNN = 1280
RNN_BLOCKS = 10
CONV_WIDTH = 4
LRU_C = 8.0
N_HEADS = 8
QK_NOPE = 128
QK_ROPE = 64
V_DIM = 128
HEAD_PAD = 256
Q_RANK = 384
KV_RANK = 256
ROPE_THETA = 10000.0
ATTN_SCALE = 1.0 / math.sqrt(QK_NOPE + QK_ROPE)
NEG = -1e30
LOG2E = 1.0 / math.log(2.0)
SCALE_LOG2E = ATTN_SCALE * LOG2E
Q_SPLIT = 2
Q_ALIGN = LANES // Q_SPLIT
ROW_GROUP = 32
SPILL = LANES
D_FF = 2816

ADAM_LR = 0.001
ADAM_B1 = 0.9
ADAM_B2 = 0.999
ADAM_EPS = 1e-08
ADAM_WD = 0.01
ADAM_STEP = 10

NN = (((1,), (0,)), ((), ()))
NT = (((1,), (1,)), ((), ()))
TN = (((0,), (0,)), ((), ()))


def _pick(n, cap, base=LANES):
    best = None
    for t in range(base, min(n, cap) + 1, base):
        if n % t == 0:
            best = t
    return best if best is not None else n


def _params(sem=None):
    return pltpu.CompilerParams(dimension_semantics=sem, vmem_limit_bytes=VMEM_LIMIT)


def _sig(x):
    return 1.0 / (1.0 + jnp.exp(-x))


def _exchange(srcs, gather, name):
    n = len(srcs)
    out_shape = [jax.ShapeDtypeStruct((N_DEV,) + (s.shape if g else s.shape[1:]), s.dtype)
                 for s, g in zip(srcs, gather)]

    def body(*refs):
        src, dst = refs[:n], refs[n:2 * n]
        send_sems, recv_sems, local_sems = refs[2 * n:]
        x, y, c = lax.axis_index("x"), lax.axis_index("y"), lax.axis_index("c")
        me = 4 * x + 2 * y + c
        local = []
        for t in range(n):
            cp = pltpu.make_async_copy(src[t] if gather[t] else src[t].at[me], dst[t].at[me], local_sems.at[t])
            cp.start()
            local.append(cp)
        sends, recvs = [], []
        for k in range(1, N_DEV):
            px = 1 - x if k & 4 else x
            py = 1 - y if k & 2 else y
            pc = 1 - c if k & 1 else c
            peer = 4 * px + 2 * py + pc
            for t in range(n):
                cp = pltpu.make_async_remote_copy(
                    src_ref=src[t] if gather[t] else src[t].at[peer], dst_ref=dst[t].at[me],
                    send_sem=send_sems.at[t, k - 1], recv_sem=recv_sems.at[t, k - 1],
                    device_id=(px, py, pc), device_id_type=pl.DeviceIdType.MESH)
                cp.start()
                sends.append(cp)
                recvs.append(pltpu.make_async_remote_copy(
                    src_ref=src[t] if gather[t] else src[t].at[peer], dst_ref=dst[t].at[peer],
                    send_sem=send_sems.at[t, k - 1], recv_sem=recv_sems.at[t, k - 1],
                    device_id=(px, py, pc), device_id_type=pl.DeviceIdType.MESH))
        for cp in recvs:
            cp.wait_recv()
        for cp in sends:
            cp.wait_send()
        for cp in local:
            cp.wait()

    any_spec = pl.BlockSpec(memory_space=pl.ANY)
    return pl.pallas_call(
        body, name=name, out_shape=out_shape,
        in_specs=[any_spec] * n, out_specs=[any_spec] * n,
        scratch_shapes=[pltpu.SemaphoreType.DMA((n, N_DEV - 1)), pltpu.SemaphoreType.DMA((n, N_DEV - 1)),
                        pltpu.SemaphoreType.DMA((n,))],
    )(*srcs)


HBM_SPEC = pl.BlockSpec(memory_space=pltpu.HBM)
SEM_SPEC = pl.BlockSpec(memory_space=pltpu.SEMAPHORE)
DATAFLOW = pltpu.SideEffectType.DATAFLOW_SIDE_EFFECTING


def _peers(x, y, c):
    out = []
    for k in range(1, N_DEV):
        px = 1 - x if k & 4 else x
        py = 1 - y if k & 2 else y
        pc = 1 - c if k & 1 else c
        out.append((k, (px, py, pc), 4 * px + 2 * py + pc))
    return out


def _split_copies(src, land, gather, send_sems, recv_sems, local_sems):
    x, y, c = lax.axis_index("x"), lax.axis_index("y"), lax.axis_index("c")
    me = 4 * x + 2 * y + c
    n = len(src)
    local = [pltpu.make_async_copy(src[t] if gather[t] else src[t].at[me], land[t].at[me], local_sems.at[t])
             for t in range(n)]
    sends, recvs = [], []
    for k, pos, peer in _peers(x, y, c):
        for t in range(n):
            mine = src[t] if gather[t] else src[t].at[peer]
            slot = t * (N_DEV - 1) + k - 1
            common = dict(send_sem=send_sems.at[slot], recv_sem=recv_sems.at[slot], device_id=pos,
                          device_id_type=pl.DeviceIdType.MESH)
            sends.append(pltpu.make_async_remote_copy(src_ref=mine, dst_ref=land[t].at[me], **common))
            recvs.append(pltpu.make_async_remote_copy(src_ref=mine, dst_ref=land[t].at[peer], **common))
    return local, sends, recvs


def _exchange_start(srcs, gather, after, name):
    n = len(srcs)
    lands = [lax.empty((N_DEV,) + (s.shape if g else s.shape[1:]), s.dtype) for s, g in zip(srcs, gather)]

    def body(*refs):
        src, land = refs[:n], refs[n:2 * n]
        send_sems, recv_sems, local_sems = refs[2 * n + 1:2 * n + 4]
        local, sends, _ = _split_copies(src, land, gather, send_sems, recv_sems, local_sems)
        for cp in local + sends:
            cp.start()
        refs[-1][...] = jnp.zeros_like(refs[-1])

    hbm = lambda a: pltpu.HBM(a.shape, a.dtype)
    outs = pl.pallas_call(
        body, name=name,
        out_shape=(pltpu.SemaphoreType.DMA((n * (N_DEV - 1),)), pltpu.SemaphoreType.DMA((n * (N_DEV - 1),)),
                   pltpu.SemaphoreType.DMA((n,)), *[hbm(s) for s in srcs], *[hbm(a) for a in lands],
                   jax.ShapeDtypeStruct((SUBLANES, LANES), F32)),
        in_specs=[HBM_SPEC] * (2 * n) + [pl.BlockSpec(memory_space=pl.ANY)],
        out_specs=(SEM_SPEC, SEM_SPEC, SEM_SPEC, *[HBM_SPEC] * (2 * n), pl.BlockSpec(memory_space=pltpu.VMEM)),
        input_output_aliases={t: 3 + t for t in range(2 * n)},
        compiler_params=pltpu.CompilerParams(has_side_effects=DATAFLOW),
    )(*[pltpu.with_memory_space_constraint(a, pltpu.HBM) for a in list(srcs) + lands], after)
    return dict(sems=outs[:3], srcs=outs[3:3 + n], lands=outs[3 + n:3 + 2 * n], token=outs[-1], gather=gather)


def _exchange_wait(handle, after, name):
    srcs, lands, gather = handle["srcs"], handle["lands"], handle["gather"]
    n = len(srcs)

    def body(*refs):
        src, land = refs[:n], refs[n:2 * n]
        send_sems, recv_sems, local_sems = refs[2 * n:2 * n + 3]
        local, sends, recvs = _split_copies(src, land, gather, send_sems, recv_sems, local_sems)
        for cp in sends:
            cp.wait_send()
        for cp in recvs:
            cp.wait_recv()
        for cp in local:
            cp.wait()

    hbm = lambda a: pltpu.HBM(a.shape, a.dtype)
    outs = pl.pallas_call(
        body, name=name, out_shape=(*[hbm(s) for s in srcs], *[hbm(a) for a in lands]),
        in_specs=[HBM_SPEC] * (2 * n) + [SEM_SPEC] * 3 + [pl.BlockSpec(memory_space=pl.ANY)],
        out_specs=[HBM_SPEC] * (2 * n), input_output_aliases={t: t for t in range(2 * n)},
        compiler_params=pltpu.CompilerParams(has_side_effects=DATAFLOW),
    )(*srcs, *lands, *handle["sems"], after)
    return outs[n:]


def _mm(pairs, mode, name, res=None, out_dtype=F32):
    pairs = [p if len(p) == 4 else (p[0], p[1], p[0].shape[1], 0) for p in pairs]
    m = pairs[0][0].shape[0]
    n = pairs[0][1].shape[1] if mode == "nn" else pairs[0][1].shape[0]
    tm, tn = _pick(m, 640), _pick(n, 1408)
    np_ = len(pairs)
    dims = NN if mode == "nn" else NT

    def body(*refs):
        acc = None
        for s in range(np_):
            d = lax.dot_general(refs[2 * s][...].astype(BF16), refs[2 * s + 1][...].astype(BF16), dims,
                                preferred_element_type=F32)
            acc = d if acc is None else acc + d
        if res is not None:
            acc = acc + refs[2 * np_][...]
        refs[-1][...] = acc.astype(out_dtype)

    in_specs, args = [], []
    for a, b, kt, kb in pairs:
        in_specs.append(pl.BlockSpec((tm, kt), lambda i, j, kb=kb: (i, kb)))
        if mode == "nn":
            in_specs.append(pl.BlockSpec((kt, tn), lambda i, j, kb=kb: (kb, j)))
        else:
            in_specs.append(pl.BlockSpec((tn, kt), lambda i, j, kb=kb: (j, kb)))
        args += [a, b]
    if res is not None:
        in_specs.append(pl.BlockSpec((tm, tn), lambda i, j: (i, j)))
        args.append(res)
    return pl.pallas_call(
        body, name=name, grid=(m // tm, n // tn), in_specs=in_specs,
        out_specs=pl.BlockSpec((tm, tn), lambda i, j: (i, j)),
        out_shape=jax.ShapeDtypeStruct((m, n), out_dtype),
        compiler_params=_params(("parallel", "parallel")),
    )(*args)


def _mm_tn(a, b, name):
    m, k = a.shape
    n = b.shape[1]
    tm, tk, tn = _pick(m, 1664), _pick(k, 1408), _pick(n, 1408)

    def body(a_ref, b_ref, o_ref):
        @pl.when(pl.program_id(2) == 0)
        def _():
            o_ref[...] = jnp.zeros_like(o_ref)

        o_ref[...] += lax.dot_general(a_ref[...].astype(BF16), b_ref[...].astype(BF16), TN,
                                      preferred_element_type=F32)

    return pl.pallas_call(
        body, name=name, grid=(k // tk, n // tn, m // tm),
        in_specs=[pl.BlockSpec((tm, tk), lambda i, j, r: (r, i)), pl.BlockSpec((tm, tn), lambda i, j, r: (r, j))],
        out_specs=pl.BlockSpec((tk, tn), lambda i, j, r: (i, j)),
        out_shape=jax.ShapeDtypeStruct((k, n), F32),
        compiler_params=_params(("parallel", "parallel", "arbitrary")),
    )(a, b)


ROW_TILE_BYTES = 6 * 1024 * 1024


def _row_tile(rows, row_in, row_out):
    per_row = sum((r[1] * r[0].dtype.itemsize) if isinstance(r, tuple) else (r.shape[1] * r.dtype.itemsize)
                  for r in row_in)
    per_row += sum(w * jnp.dtype(dt).itemsize for w, dt in row_out)
    return _pick(rows, min(640, max(LANES, ROW_TILE_BYTES // per_row)))


def _rowcall(body, name, rows, row_in, full_in, row_out, acc_out=()):
    tr = _row_tile(rows, row_in, row_out)
    n_steps = rows // tr
    in_specs, args = [], []
    for r in row_in:
        arr, w, cb = r if isinstance(r, tuple) else (r, r.shape[1], 0)
        in_specs.append(pl.BlockSpec((tr, w), lambda i, cb=cb: (i, cb)))
        args.append(arr)
    for f in full_in:
        in_specs.append(pl.BlockSpec(f.shape, lambda i, nd=f.ndim: (0,) * nd))
        args.append(f)
    out_specs = [pl.BlockSpec((tr, w), lambda i: (i, 0)) for w, _ in row_out]
    out_shape = [jax.ShapeDtypeStruct((rows, w), dt) for w, dt in row_out]
    for shp, dt in acc_out:
        out_specs.append(pl.BlockSpec(shp, lambda i, nd=len(shp): (0,) * nd))
        out_shape.append(jax.ShapeDtypeStruct(shp, dt))

    def wrapped(*refs):
        body(pl.program_id(0), n_steps, *refs)

    return pl.pallas_call(
        wrapped, name=name, grid=(n_steps,), in_specs=in_specs, out_specs=out_specs, out_shape=out_shape,
        compiler_params=_params(("arbitrary",) if acc_out else ("parallel",)),
    )(*args)


def _rmsnorm_fwd(x, g, name):
    rows, w = x.shape

    def body(i, n, x_ref, g_ref, o_ref):
        xv = x_ref[...]
        r = lax.rsqrt(jnp.mean(xv * xv, axis=-1, keepdims=True) + EPS)
        o_ref[...] = (xv * r * g_ref[...]).astype(BF16)

    return _rowcall(body, name, rows,[x], [g], [(w, BF16)])[0]


def _rmsnorm_bwd_math(xv, dy, g):
    w = xv.shape[-1]
    r = lax.rsqrt(jnp.mean(xv * xv, axis=-1, keepdims=True) + EPS)
    t = dy * g
    dx = r * t - xv * (r * r * r * (jnp.sum(t * xv, axis=-1, keepdims=True) / w))
    dg = jnp.sum(dy * xv * r, axis=0, keepdims=True)
    return dx, dg


def _rmsnorm_bwd(x, dy, g, name, res=None, out_dtype=F32, dep=None):
    rows, w = x.shape

    def body(i, n, *refs):
        x_ref, dy_ref = refs[0], refs[1]
        g_ref, dx_ref, dg_ref = refs[-3], refs[-2], refs[-1]
        dx, dg = _rmsnorm_bwd_math(x_ref[...], dy_ref[...], g_ref[...])
        if res is not None:
            dx = dx + refs[2][...]
        dx_ref[...] = dx.astype(out_dtype)

        @pl.when(i == 0)
        def _():
            dg_ref[...] = jnp.zeros_like(dg_ref)

        dg_ref[...] += dg

    row_in = [x, dy] + ([res] if res is not None else [])
    return _rowcall(body, name, rows, row_in, ([dep] if dep is not None else []) + [g], [(w, out_dtype)],
                    [((1, w), F32)])


def _loss_bwd(h2, tgt, g, seq, name):
    rows, w = h2.shape
    tr = _row_tile(rows, [h2, tgt], [(w, F32)])

    def body(i, n, h_ref, t_ref, g_ref, dh_ref, dg_ref, lcol_ref, loss_ref):
        hv, gv = h_ref[...], g_ref[...]
        row = i * tr + lax.broadcasted_iota(jnp.int32, (tr, w), 0)
        valid = jnp.logical_and(row >= N_META, row < N_META + seq)
        r = lax.rsqrt(jnp.mean(hv * hv, axis=-1, keepdims=True) + EPS)
        err = jnp.where(valid, hv * r * gv - t_ref[...], 0.0)
        dx, dg = _rmsnorm_bwd_math(hv, err * (1.0 / w), gv)
        dh_ref[...] = dx

        @pl.when(i == 0)
        def _():
            dg_ref[...] = jnp.zeros_like(dg_ref)
            lcol_ref[...] = jnp.zeros_like(lcol_ref)

        dg_ref[...] += dg
        lcol_ref[...] += jnp.sum(err * err, axis=0, keepdims=True)

        @pl.when(i == n - 1)
        def _():
            total = jnp.sum(lcol_ref[...], axis=1, keepdims=True) * (0.5 / w)
            loss_ref[...] = jnp.broadcast_to(total, loss_ref.shape)

    return _rowcall(body, name, rows, [h2, tgt], [g], [(w, F32)],
                    [((1, w), F32), ((1, w), F32), ((1, LANES), F32)])


def _mix_fwd(um, p_rnn, p_att, bg, name):
    rows, d = p_rnn.shape

    def body(i, n, u0_ref, u1_ref, pr_ref, pa_ref, bg_ref, o_ref):
        g0 = _sig(u0_ref[...].astype(F32) + bg_ref[:, :d])
        g1 = _sig(u1_ref[...].astype(F32) + bg_ref[:, d:])
        o_ref[...] = (g0 * pr_ref[...].astype(F32) + g1 * pa_ref[...].astype(F32)).astype(BF16)

    return _rowcall(body, name, rows,[(um, d, 0), (um, d, 1), p_rnn, p_att], [bg],
                    [(d, BF16)])[0]


def _mix_bwd(um, p_rnn, p_att, dmix, bg, name):
    rows, d = p_rnn.shape

    def body(i, n, u0_ref, u1_ref, pr_ref, pa_ref, dm_ref, bg_ref, dpr_ref, dpa_ref, dum_ref, dbg_ref):
        g0 = _sig(u0_ref[...].astype(F32) + bg_ref[:, :d])
        g1 = _sig(u1_ref[...].astype(F32) + bg_ref[:, d:])
        dm = dm_ref[...].astype(F32)
        dpr_ref[...] = (dm * g0).astype(BF16)
        dpa_ref[...] = (dm * g1).astype(BF16)
        du0 = dm * pr_ref[...].astype(F32) * g0 * (1.0 - g0)
        du1 = dm * pa_ref[...].astype(F32) * g1 * (1.0 - g1)
        dum_ref[:, :d] = du0.astype(BF16)
        dum_ref[:, d:] = du1.astype(BF16)

        @pl.when(i == 0)
        def _():
            dbg_ref[...] = jnp.zeros_like(dbg_ref)

        dbg_ref[:, :d] += jnp.sum(du0, axis=0, keepdims=True)
        dbg_ref[:, d:] += jnp.sum(du1, axis=0, keepdims=True)

    return _rowcall(body, name, rows,[(um, d, 0), (um, d, 1), p_rnn, p_att, dmix], [bg],
                    [(d, BF16), (d, BF16), (2 * d, BF16)], [((1, 2 * d), F32)])


def _swiglu_fwd(gu, name):
    rows, w2 = gu.shape
    f = w2 // 2

    def body(i, n, g_ref, u_ref, o_ref):
        gate = g_ref[...].astype(F32)
        o_ref[...] = (gate * _sig(gate) * u_ref[...].astype(F32)).astype(BF16)

    return _rowcall(body, name, rows,[(gu, f, 0), (gu, f, 1)], [], [(f, BF16)])[0]


def _swiglu_bwd(gu, dact, name):
    rows, w2 = gu.shape
    f = w2 // 2

    def body(i, n, g_ref, u_ref, da_ref, o_ref):
        gate, da = g_ref[...].astype(F32), da_ref[...].astype(F32)
        sg = _sig(gate)
        o_ref[:, :f] = (da * u_ref[...].astype(F32) * (sg * (1.0 + gate * (1.0 - sg)))).astype(BF16)
        o_ref[:, f:] = (da * gate * sg).astype(BF16)

    return _rowcall(body, name, rows,[(gu, f, 0), (gu, f, 1), dact], [], [(w2, BF16)])[0]


def _rope_tables(lp):
    idx = jnp.arange(lp, dtype=jnp.int32).astype(F32)
    inv_freq = ROPE_THETA ** (-jnp.arange(0, QK_ROPE, 2, dtype=F32) / QK_ROPE)
    ang = idx[:, None] * inv_freq[None, :]
    cos, sin = jnp.cos(ang), jnp.sin(ang)
    half = QK_ROPE // 2
    z = lambda wdt: jnp.zeros((lp, wdt), F32)
    tc = jnp.concatenate([cos, cos, z(LANES - 2 * half)], axis=1)
    ts1 = jnp.concatenate([-sin, z(LANES - half)], axis=1)
    ts2 = jnp.concatenate([z(half), sin, z(LANES - 2 * half)], axis=1)
    return tc, ts1, ts2


def _rope(xv, tc, ts1, ts2):
    half = QK_ROPE // 2
    return xv * tc + pltpu.roll(xv, LANES - half, 1) * ts1 + pltpu.roll(xv, half, 1) * ts2


def _rope_t(dv, tc, ts1, ts2):
    half = QK_ROPE // 2
    return dv * tc + pltpu.roll(dv * ts1, half, 1) + pltpu.roll(dv * ts2, LANES - half, 1)


def _rope_fwd(qpad, kpad, ukr, tabs, name):
    rows, w = qpad.shape

    def body(i, n, q_ref, k_ref, r_ref, c_ref, s1_ref, s2_ref, qo_ref, ko_ref):
        tc, ts1, ts2 = c_ref[...], s1_ref[...], s2_ref[...]
        kr = _rope(r_ref[...], tc, ts1, ts2).astype(BF16)
        for h in range(N_HEADS):
            lo, mid, hi = h * HEAD_PAD, h * HEAD_PAD + QK_NOPE, (h + 1) * HEAD_PAD
            qo_ref[:, lo:mid] = q_ref[:, lo:mid].astype(BF16)
            qo_ref[:, mid:hi] = _rope(q_ref[:, mid:hi].astype(F32), tc, ts1, ts2).astype(BF16)
            ko_ref[:, lo:mid] = k_ref[:, lo:mid].astype(BF16)
            ko_ref[:, mid:hi] = kr

    return _rowcall(body, name, rows,[qpad, kpad, ukr, *tabs], [], [(w, BF16), (w, BF16)])


def _rope_bwd(dq, dk, tabs, name):
    rows, w = dq.shape

    def body(i, n, dq_ref, dk_ref, c_ref, s1_ref, s2_ref, qo_ref, ko_ref, ro_ref):
        tc, ts1, ts2 = c_ref[...], s1_ref[...], s2_ref[...]
        dkr = None
        for h in range(N_HEADS):
            lo, mid, hi = h * HEAD_PAD, h * HEAD_PAD + QK_NOPE, (h + 1) * HEAD_PAD
            qo_ref[:, lo:mid] = dq_ref[:, lo:mid].astype(BF16)
            qo_ref[:, mid:hi] = _rope_t(dq_ref[:, mid:hi].astype(F32), tc, ts1, ts2).astype(BF16)
            ko_ref[:, lo:mid] = dk_ref[:, lo:mid].astype(BF16)
            ko_ref[:, mid:hi] = jnp.zeros((ko_ref.shape[0], hi - mid), BF16)
            part = dk_ref[:, mid:hi].astype(F32)
            dkr = part if dkr is None else dkr + part
        ro_ref[...] = _rope_t(dkr, tc, ts1, ts2).astype(BF16)

    return _rowcall(body, name, rows,[dq, dk, *tabs], [],
                    [(w, BF16), (w, BF16), (LANES, BF16)])


def _visible(q0, k0, tq, tk):
    qrow = q0 + lax.broadcasted_iota(jnp.int32, (tq, tk), 0)
    kcol = k0 + lax.broadcasted_iota(jnp.int32, (tq, tk), 1)
    return ((kcol + CHUNK_BIAS) >> CHUNK_SHIFT) <= ((qrow + CHUNK_BIAS) >> CHUNK_SHIFT)


def _visible_t(k0, q0, tk, tq):
    krow = k0 + lax.broadcasted_iota(jnp.int32, (tk, tq), 0)
    qcol = q0 + lax.broadcasted_iota(jnp.int32, (tk, tq), 1)
    return ((krow + CHUNK_BIAS) >> CHUNK_SHIFT) <= ((qcol + CHUNK_BIAS) >> CHUNK_SHIFT)


def _lanes(v, width):
    return jnp.tile(v, (1, width // LANES))


def _pipelined_chunks(n_full, scores, absorb):
    scores(0, 0)

    def pair(jj, carry):
        a = 2 * jj
        scores(a + 1, 1)
        absorb(a, 0, False)
        scores(a + 2, 0)
        absorb(a + 1, 1, False)
        return carry

    lax.fori_loop(0, n_full // 2, pair, 0)

    @pl.when(n_full % 2 == 0)
    def _():
        absorb(n_full, 0, True)

    @pl.when(n_full % 2 == 1)
    def _():
        scores(n_full, 1)
        absorb(n_full - 1, 0, False)
        absorb(n_full, 1, True)


def _attn_fwd(q, k, v, t, name):
    lp = q.shape[0]
    nt = lp // t

    def body(q_ref, k_ref, v_ref, o_ref, lse_ref, m_s, l_s, acc_s, a_s, sa_s, sb_s, p_s):
        i = pl.program_id(1)
        m_s[...] = jnp.full(m_s.shape, NEG, F32)
        l_s[...] = jnp.zeros(l_s.shape, F32)
        acc_s[...] = jnp.zeros(acc_s.shape, F32)

        s_bufs = (sa_s, sb_s)

        def scores(j, slot):
            r0 = pl.multiple_of(j * t, t)
            s_bufs[slot][...] = lax.dot_general(q_ref[...], k_ref[pl.ds(r0, t), :], NT,
                                                preferred_element_type=F32)

        def absorb(j, slot, masked):
            for r in range(0, t, ROW_GROUP):
                rows = slice(r, r + ROW_GROUP)
                s = s_bufs[slot][rows, :]
                if masked:
                    s = jnp.where(_visible(i * t + r, i * t, ROW_GROUP, t), s, NEG)
                m_prev = m_s[rows, :]
                m_new = jnp.maximum(m_prev, jnp.max(s, axis=1, keepdims=True))
                alpha = jnp.exp2((m_prev - m_new) * SCALE_LOG2E)
                p = jnp.exp2((s - _lanes(m_new, t)) * SCALE_LOG2E)
                l_s[rows, :] = alpha * l_s[rows, :] + jnp.sum(p, axis=1, keepdims=True)
                m_s[rows, :] = m_new
                a_s[rows, :] = alpha
                p_s[rows, :] = p.astype(BF16)
            r0 = pl.multiple_of(j * t, t)
            acc_s[...] = a_s[...] * acc_s[...] + jnp.dot(p_s[...], v_ref[pl.ds(r0, t), :],
                                                         preferred_element_type=F32)

        @pl.when(i + 1 < nt)
        def _():
            r1 = pl.multiple_of((i + 1) * t, t)
            rows = slice(t - SPILL, t)
            s = lax.dot_general(q_ref[rows, :], k_ref[pl.ds(r1, SPILL), :], NT, preferred_element_type=F32)
            s = jnp.where(_visible(i * t + t - SPILL, (i + 1) * t, SPILL, SPILL), s, NEG)
            m_prev = m_s[rows, :]
            m_new = jnp.maximum(m_prev, jnp.max(s, axis=1, keepdims=True))
            alpha = jnp.exp2((m_prev - m_new) * SCALE_LOG2E)
            p = jnp.exp2((s - m_new) * SCALE_LOG2E)
            l_s[rows, :] = alpha * l_s[rows, :] + jnp.sum(p, axis=1, keepdims=True)
            acc_s[rows, :] = alpha * acc_s[rows, :] + jnp.dot(p.astype(BF16), v_ref[pl.ds(r1, SPILL), :],
                                                              preferred_element_type=F32)
            m_s[rows, :] = m_new

        _pipelined_chunks(i, scores, absorb)
        o_ref[...] = (acc_s[...] / l_s[...]).astype(BF16)
        lse_ref[...] = m_s[...] * ATTN_SCALE + jnp.log(l_s[...])

    return pl.pallas_call(
        body, name=name, grid=(N_HEADS, nt),
        in_specs=[pl.BlockSpec((t, HEAD_PAD), lambda h, i: (i, h)),
                  pl.BlockSpec((lp, HEAD_PAD), lambda h, i: (0, h)),
                  pl.BlockSpec((lp, V_DIM), lambda h, i: (0, h))],
        out_specs=[pl.BlockSpec((t, V_DIM), lambda h, i: (i, h)),
                   pl.BlockSpec((None, t, LANES), lambda h, i: (h, i, 0))],
        out_shape=[jax.ShapeDtypeStruct((lp, N_HEADS * V_DIM), BF16),
                   jax.ShapeDtypeStruct((N_HEADS, lp, LANES), F32)],
        scratch_shapes=[pltpu.VMEM((t, LANES), F32), pltpu.VMEM((t, LANES), F32), pltpu.VMEM((t, V_DIM), F32),
                        pltpu.VMEM((t, LANES), F32), pltpu.VMEM((t, t), F32), pltpu.VMEM((t, t), F32),
                        pltpu.VMEM((t, t), BF16)],
        compiler_params=_params(("parallel", "arbitrary")),
    )(q, k, v)


def _attn_bwd_dq(q, k, v, do, o, lse, dep, t, name):
    lp = q.shape[0]
    nt = lp // t

    def body(q_ref, k_ref, v_ref, do_ref, o_ref, lse_ref, dep_ref, dq_ref, l2row_ref, dlrow_ref,
             acc_s, l2_s, dl_s, sa_s, sb_s, da_s, db_s, ds_s):
        i = pl.program_id(1)
        delta = jnp.sum(do_ref[...].astype(F32) * o_ref[...].astype(F32), axis=1, keepdims=True)
        dl_s[...] = jnp.broadcast_to(delta, dl_s.shape)
        l2_s[...] = lse_ref[...] * LOG2E
        l2row_ref[...] = l2_s[...].T[0:SUBLANES, :]
        dlrow_ref[...] = dl_s[...].T[0:SUBLANES, :]
        acc_s[...] = jnp.zeros(acc_s.shape, F32)
        s_bufs, d_bufs = (sa_s, sb_s), (da_s, db_s)

        def dscores(s, dp, rows, width):
            p = jnp.exp2(s * SCALE_LOG2E - _lanes(l2_s[rows, :], width))
            return (p * (dp - _lanes(dl_s[rows, :], width))).astype(BF16)

        @pl.when(i + 1 < nt)
        def _():
            r1 = pl.multiple_of((i + 1) * t, t)
            rows = slice(t - SPILL, t)
            ks, vs = k_ref[pl.ds(r1, SPILL), :], v_ref[pl.ds(r1, SPILL), :]
            s = lax.dot_general(q_ref[rows, :], ks, NT, preferred_element_type=F32)
            s = jnp.where(_visible(i * t + t - SPILL, (i + 1) * t, SPILL, SPILL), s, NEG)
            dp = lax.dot_general(do_ref[rows, :], vs, NT, preferred_element_type=F32)
            acc_s[rows, :] += jnp.dot(dscores(s, dp, rows, SPILL), ks, preferred_element_type=F32)

        def scores(j, slot):
            r0 = pl.multiple_of(j * t, t)
            s_bufs[slot][...] = lax.dot_general(q_ref[...], k_ref[pl.ds(r0, t), :], NT,
                                                preferred_element_type=F32)
            d_bufs[slot][...] = lax.dot_general(do_ref[...], v_ref[pl.ds(r0, t), :], NT,
                                                preferred_element_type=F32)

        def absorb(j, slot, masked):
            for r in range(0, t, ROW_GROUP):
                rows = slice(r, r + ROW_GROUP)
                s = s_bufs[slot][rows, :]
                if masked:
                    s = jnp.where(_visible(i * t + r, i * t, ROW_GROUP, t), s, NEG)
                ds_s[rows, :] = dscores(s, d_bufs[slot][rows, :], rows, t)
            r0 = pl.multiple_of(j * t, t)
            acc_s[...] += jnp.dot(ds_s[...], k_ref[pl.ds(r0, t), :], preferred_element_type=F32)

        _pipelined_chunks(i, scores, absorb)
        dq_ref[...] = (acc_s[...] * ATTN_SCALE).astype(BF16)

    stat_row = pl.BlockSpec((None, None, SUBLANES, t), lambda h, i: (h, i, 0, 0))
    return pl.pallas_call(
        body, name=name, grid=(N_HEADS, nt),
        in_specs=[pl.BlockSpec((t, HEAD_PAD), lambda h, i: (i, h)),
                  pl.BlockSpec((lp, HEAD_PAD), lambda h, i: (0, h)),
                  pl.BlockSpec((lp, V_DIM), lambda h, i: (0, h)),
                  pl.BlockSpec((t, V_DIM), lambda h, i: (i, h)),
                  pl.BlockSpec((t, V_DIM), lambda h, i: (i, h)),
                  pl.BlockSpec((None, t, LANES), lambda h, i: (h, i, 0)),
                  pl.BlockSpec(dep.shape, lambda h, i: (0, 0))],
        out_specs=[pl.BlockSpec((t, HEAD_PAD), lambda h, i: (i, h)), stat_row, stat_row],
        out_shape=[jax.ShapeDtypeStruct((lp, N_HEADS * HEAD_PAD), BF16),
                   jax.ShapeDtypeStruct((N_HEADS, nt, SUBLANES, t), F32),
                   jax.ShapeDtypeStruct((N_HEADS, nt, SUBLANES, t), F32)],
        scratch_shapes=[pltpu.VMEM((t, HEAD_PAD), F32), pltpu.VMEM((t, LANES), F32), pltpu.VMEM((t, LANES), F32),
                        pltpu.VMEM((t, t), F32), pltpu.VMEM((t, t), F32), pltpu.VMEM((t, t), F32),
                        pltpu.VMEM((t, t), F32), pltpu.VMEM((t, t), BF16)],
        compiler_params=_params(("parallel", "arbitrary")),
    )(q, k, v, do, o, lse, dep)


def _attn_bwd_dkv(q, k, v, do, l2row, dlrow, t, name):
    lp = q.shape[0]
    nt = lp // t

    def body(q_ref, k_ref, v_ref, do_ref, l2_ref, dl_ref, dk_ref, dv_ref,
             dk_s, dv_s, sa_s, sb_s, da_s, db_s, p_s, ds_s):
        j = pl.program_id(1)
        dk_s[...] = jnp.zeros(dk_s.shape, F32)
        dv_s[...] = jnp.zeros(dv_s.shape, F32)
        s_bufs, d_bufs = (sa_s, sb_s), (da_s, db_s)

        def weights(st, dpt, l2r, dlr):
            pt = jnp.exp2(st * SCALE_LOG2E - l2r)
            return pt.astype(BF16), (pt * (dpt - dlr)).astype(BF16)

        @pl.when(j > 0)
        def _():
            q0 = pl.multiple_of(j * t - SPILL, SPILL)
            rows = slice(0, SPILL)
            qs, dos = q_ref[pl.ds(q0, SPILL), :], do_ref[pl.ds(q0, SPILL), :]
            st = lax.dot_general(k_ref[rows, :], qs, NT, preferred_element_type=F32)
            st = jnp.where(_visible_t(j * t, j * t - SPILL, SPILL, SPILL), st, NEG)
            dpt = lax.dot_general(v_ref[rows, :], dos, NT, preferred_element_type=F32)
            pt, dst = weights(st, dpt, l2_ref[j - 1, 0:1, t - SPILL:], dl_ref[j - 1, 0:1, t - SPILL:])
            dv_s[rows, :] += jnp.dot(pt, dos, preferred_element_type=F32)
            dk_s[rows, :] += jnp.dot(dst, qs, preferred_element_type=F32)

        def scores(c, slot):
            r0 = pl.multiple_of((nt - 1 - c) * t, t)
            s_bufs[slot][...] = lax.dot_general(k_ref[...], q_ref[pl.ds(r0, t), :], NT,
                                                preferred_element_type=F32)
            d_bufs[slot][...] = lax.dot_general(v_ref[...], do_ref[pl.ds(r0, t), :], NT,
                                                preferred_element_type=F32)

        def absorb(c, slot, masked):
            i = nt - 1 - c
            l2r, dlr = l2_ref[i, 0:1, :], dl_ref[i, 0:1, :]
            for r in range(0, t, ROW_GROUP):
                rows = slice(r, r + ROW_GROUP)
                st = s_bufs[slot][rows, :]
                if masked:
                    st = jnp.where(_visible_t(j * t + r, j * t, ROW_GROUP, t), st, NEG)
                p_s[rows, :], ds_s[rows, :] = weights(st, d_bufs[slot][rows, :], l2r, dlr)
            r0 = pl.multiple_of(i * t, t)
            dv_s[...] += jnp.dot(p_s[...], do_ref[pl.ds(r0, t), :], preferred_element_type=F32)
            dk_s[...] += jnp.dot(ds_s[...], q_ref[pl.ds(r0, t), :], preferred_element_type=F32)

        _pipelined_chunks(nt - 1 - j, scores, absorb)
        dk_ref[...] = (dk_s[...] * ATTN_SCALE).astype(BF16)
        dv_ref[...] = dv_s[...].astype(BF16)

    stat_rows = pl.BlockSpec((None, nt, SUBLANES, t), lambda h, j: (h, 0, 0, 0))
    return pl.pallas_call(
        body, name=name, grid=(N_HEADS, nt),
        in_specs=[pl.BlockSpec((lp, HEAD_PAD), lambda h, j: (0, h)),
                  pl.BlockSpec((t, HEAD_PAD), lambda h, j: (j, h)),
                  pl.BlockSpec((t, V_DIM), lambda h, j: (j, h)),
                  pl.BlockSpec((lp, V_DIM), lambda h, j: (0, h)),
                  stat_rows, stat_rows],
        out_specs=[pl.BlockSpec((t, HEAD_PAD), lambda h, j: (j, h)),
                   pl.BlockSpec((t, V_DIM), lambda h, j: (j, h))],
        out_shape=[jax.ShapeDtypeStruct((lp, N_HEADS * HEAD_PAD), BF16),
                   jax.ShapeDtypeStruct((lp, N_HEADS * V_DIM), BF16)],
        scratch_shapes=[pltpu.VMEM((t, HEAD_PAD), F32), pltpu.VMEM((t, V_DIM), F32),
                        pltpu.VMEM((t, t), F32), pltpu.VMEM((t, t), F32), pltpu.VMEM((t, t), F32),
                        pltpu.VMEM((t, t), F32), pltpu.VMEM((t, t), BF16), pltpu.VMEM((t, t), BF16)],
        compiler_params=_params(("parallel", "arbitrary")),
    )(q, k, v, do, l2row, dlrow)


def _shift_down(cur, prev8, k):
    r = pltpu.roll(cur, k, 0)
    row8 = lax.broadcasted_iota(jnp.int32, prev8.shape, 0)
    first = jnp.where(row8 < k, pltpu.roll(prev8, k, 0), r[0:SUBLANES])
    return jnp.concatenate([first, r[SUBLANES:]], axis=0)


def _shift_up(cur, next8, k):
    t = cur.shape[0]
    r = pltpu.roll(cur, t - k, 0)
    row8 = lax.broadcasted_iota(jnp.int32, next8.shape, 0)
    last = jnp.where(row8 >= SUBLANES - k, pltpu.roll(next8, SUBLANES - k, 0), r[t - SUBLANES:])
    return jnp.concatenate([r[:t - SUBLANES], last], axis=0)


def _scan_down(a, b):
    t = a.shape[0]
    row = lax.broadcasted_iota(jnp.int32, a.shape, 0)
    s = 1
    while s < t:
        keep = row >= s
        a_sh = jnp.where(keep, pltpu.roll(a, s, 0), 1.0)
        b_sh = jnp.where(keep, pltpu.roll(b, s, 0), 0.0)
        b = a * b_sh + b
        a = a * a_sh
        s *= 2
    return a, b


def _scan_up(a, b):
    t = a.shape[0]
    row = lax.broadcasted_iota(jnp.int32, a.shape, 0)
    s = 1
    while s < t:
        keep = row < t - s
        a_sh = jnp.where(keep, pltpu.roll(a, t - s, 0), 1.0)
        b_sh = jnp.where(keep, pltpu.roll(b, t - s, 0), 0.0)
        b = a * b_sh + b
        a = a * a_sh
        s *= 2
    return a, b


def _neg_expm1(y):
    series = -y * (1.0 + y * (0.5 + y * (1.0 / 6.0 + y * (1.0 / 24.0 + y * (1.0 / 120.0)))))
    return jnp.where(y > -0.1, series, 1.0 - jnp.exp(y))


def _log_sigmoid(x):
    return jnp.minimum(x, 0.0) - jnp.log(1.0 + jnp.exp(-jnp.abs(x)))


GELU_C = math.sqrt(2.0 / math.pi)
GELU_K = 0.044715


def _gelu(x):
    th = jnp.tanh(GELU_C * (x + GELU_K * x * x * x))
    return 0.5 * x * (1.0 + th), th


def _block_mm(xb, w_ref, dims):
    rb = D_RNN // RNN_BLOCKS
    return jnp.concatenate(
        [lax.dot_general(xb[:, h * rb:(h + 1) * rb], w_ref[h], dims, preferred_element_type=F32)
         for h in range(RNN_BLOCKS)], axis=1)


def _rglru_gates(ux, prev8, pv_ref, wa_ref, wi_ref):
    shifted = [ux] + [_shift_down(ux, prev8, k) for k in range(1, CONV_WIDTH)]
    xc = pv_ref[4:5, :] + pv_ref[3:4, :] * ux
    for k in range(1, CONV_WIDTH):
        xc = xc + pv_ref[3 - k:4 - k, :] * shifted[k]
    xcb = xc.astype(BF16)
    r_g = _sig(_block_mm(xcb, wa_ref, NN) + pv_ref[5:6, :])
    i_g = _sig(_block_mm(xcb, wi_ref, NN) + pv_ref[6:7, :])
    log_a = LRU_C * r_g * _log_sigmoid(pv_ref[7:8, :])
    a = jnp.exp(log_a)
    mm = jnp.sqrt(_neg_expm1(2.0 * log_a))
    return dict(shifted=shifted, xc=xc, xcb=xcb, r=r_g, i=i_g, a=a, mm=mm)


def _rglru_fwd(ux, ug, pv, wa, wi, t, name):
    lp, d = ux.shape

    def body(ux_ref, ug_ref, pv_ref, wa_ref, wi_ref, y_ref, h_ref, tail_s, hc_s):
        @pl.when(pl.program_id(0) == 0)
        def _():
            tail_s[...] = jnp.zeros_like(tail_s)
            hc_s[...] = jnp.zeros_like(hc_s)

        uxv = ux_ref[...]
        gt = _rglru_gates(uxv, tail_s[...], pv_ref, wa_ref, wi_ref)
        tail_s[...] = ux_ref[t - SUBLANES:t, :]
        cum_a, hloc = _scan_down(gt["a"], gt["mm"] * (gt["i"] * gt["xc"]))
        h_ref[...] = hloc + cum_a * hc_s[0:1, :]
        hc_s[...] = h_ref[t - SUBLANES:t, :]
        hc_s[0:1, :] = h_ref[t - 1:t, :]
        y_ref[...] = (h_ref[...] * _gelu(ug_ref[...])[0]).astype(BF16)

    tile = pl.BlockSpec((t, d), lambda i: (i, 0))
    return pl.pallas_call(
        body, name=name, grid=(lp // t,),
        in_specs=[tile, tile, pl.BlockSpec(pv.shape, lambda i: (0, 0)),
                  pl.BlockSpec(wa.shape, lambda i: (0, 0, 0)), pl.BlockSpec(wi.shape, lambda i: (0, 0, 0))],
        out_specs=[tile, tile],
        out_shape=[jax.ShapeDtypeStruct((lp, d), BF16), jax.ShapeDtypeStruct((lp, d), F32)],
        scratch_shapes=[pltpu.VMEM((SUBLANES, d), F32), pltpu.VMEM((SUBLANES, d), F32)],
        compiler_params=_params(("arbitrary",)),
    )(ux, ug, pv, wa, wi)


def _rglru_bwd(ux, ug, hs, dy, pv, wa, wi, dep, t, name):
    lp, d = ux.shape
    nt = lp // t
    per = t // SUBLANES
    rb = d // RNN_BLOCKS

    def body(ux_ref, uxp_ref, ug_ref, h_ref, hp_ref, dy_ref, pv_ref, wa_ref, wi_ref, dep_ref,
             dux_ref, dug_ref, dpv_ref, dwa_ref, dwi_ref, ca_s, cg_s, cx_s):
        step = pl.program_id(0)
        first_tile = step == nt - 1

        @pl.when(step == 0)
        def _():
            for ref in (ca_s, cg_s, cx_s, dpv_ref, dwa_ref, dwi_ref):
                ref[...] = jnp.zeros_like(ref)

        uxv = ux_ref[...]
        prev8 = jnp.where(first_tile, 0.0, uxp_ref[...])
        hprev8 = jnp.where(first_tile, 0.0, hp_ref[...])
        gt = _rglru_gates(uxv, prev8, pv_ref, wa_ref, wi_ref)
        a, mm, r_g, i_g, xc = gt["a"], gt["mm"], gt["r"], gt["i"], gt["xc"]
        hv = h_ref[...]
        hprev = _shift_down(hv, hprev8, 1)
        ugv, dyv = ug_ref[...], dy_ref[...]
        gel, th = _gelu(ugv)
        dgel = 0.5 * (1.0 + th) + 0.5 * ugv * (1.0 - th * th) * (GELU_C * (1.0 + 3.0 * GELU_K * ugv * ugv))
        dug_ref[...] = (dyv * hv * dgel).astype(BF16)
        a_up = _shift_up(a, ca_s[...], 1)
        cum_a, gloc = _scan_up(a_up, dyv * gel)
        gv = gloc + cum_a * cg_s[0:1, :]
        ca_s[...] = a[0:SUBLANES]
        cg_s[...] = gv[0:SUBLANES]
        ixc = i_g * xc
        d_ixc = gv * mm
        d_log_a = gv * hprev * a - (gv * ixc) * (a * a) / mm
        logsig = _log_sigmoid(pv_ref[7:8, :])
        d_pre_a = d_log_a * (LRU_C * logsig) * r_g * (1.0 - r_g)
        d_pre_i = d_ixc * xc * i_g * (1.0 - i_g)
        dab, dib = d_pre_a.astype(BF16), d_pre_i.astype(BF16)
        d_xc = d_ixc * i_g + _block_mm(dab, wa_ref, NT) + _block_mm(dib, wi_ref, NT)
        xcb = gt["xcb"]
        for h in range(RNN_BLOCKS):
            cols = slice(h * rb, (h + 1) * rb)
            dwa_ref[h] += lax.dot_general(xcb[:, cols], dab[:, cols], TN, preferred_element_type=F32)
            dwi_ref[h] += lax.dot_general(xcb[:, cols], dib[:, cols], TN, preferred_element_type=F32)
        csum = lambda v: jnp.sum(v, axis=0, keepdims=True)
        for k in range(CONV_WIDTH):
            dpv_ref[3 - k:4 - k, :] += csum(d_xc * gt["shifted"][k])
        dpv_ref[4:5, :] += csum(d_xc)
        dpv_ref[5:6, :] += csum(d_pre_a)
        dpv_ref[6:7, :] += csum(d_pre_i)
        dpv_ref[7:8, :] += csum(d_log_a * (LRU_C * r_g)) * _sig(-pv_ref[7:8, :])
        dux = pv_ref[3:4, :] * d_xc
        for k in range(1, CONV_WIDTH):
            dux = dux + pv_ref[3 - k:4 - k, :] * _shift_up(d_xc, cx_s[...], k)
        cx_s[...] = d_xc[0:SUBLANES]
        dux_ref[...] = dux.astype(BF16)

    rev = lambda i: (nt - 1 - i, 0)
    before = lambda i: (jnp.maximum((nt - 1 - i) * per - 1, 0), 0)
    tile = pl.BlockSpec((t, d), rev)
    tail = pl.BlockSpec((SUBLANES, d), before)
    fixed2 = lambda arr: pl.BlockSpec(arr.shape, lambda i: (0, 0))
    fixed3 = lambda arr: pl.BlockSpec(arr.shape, lambda i: (0, 0, 0))
    return pl.pallas_call(
        body, name=name, grid=(nt,),
        in_specs=[tile, tail, tile, tile, tail, tile, fixed2(pv), fixed3(wa), fixed3(wi), fixed2(dep)],
        out_specs=[tile, tile, fixed2(pv), fixed3(wa), fixed3(wi)],
        out_shape=[jax.ShapeDtypeStruct((lp, d), BF16), jax.ShapeDtypeStruct((lp, d), BF16),
                   jax.ShapeDtypeStruct(pv.shape, F32), jax.ShapeDtypeStruct(wa.shape, F32),
                   jax.ShapeDtypeStruct(wi.shape, F32)],
        scratch_shapes=[pltpu.VMEM((SUBLANES, d), F32)] * 3,
        compiler_params=_params(("arbitrary",)),
    )(ux, ux, ug, hs, hs, dy, pv, wa, wi, dep)


def _adamw(w, m, v, parts, dep, name):
    rows, cols = w.shape
    tr = _pick(rows, 256, SUBLANES)
    c1 = 1.0 / (1.0 - ADAM_B1 ** ADAM_STEP)
    c2 = 1.0 / (1.0 - ADAM_B2 ** ADAM_STEP)

    def body(w_ref, m_ref, v_ref, p_ref, dep_ref, g_ref, d_ref, mo_ref, vo_ref):
        g = p_ref[0].astype(F32)
        for q in range(1, N_DEV):
            g = g + p_ref[q].astype(F32)
        mn = ADAM_B1 * m_ref[...] + (1.0 - ADAM_B1) * g
        vn = ADAM_B2 * v_ref[...] + (1.0 - ADAM_B2) * (g * g)
        g_ref[...] = g
        mo_ref[...] = mn
        vo_ref[...] = vn
        d_ref[...] = -ADAM_LR * ((mn * c1) / (jnp.sqrt(vn * c2) + ADAM_EPS) + ADAM_WD * w_ref[...])

    blk = pl.BlockSpec((tr, cols), lambda i: (i, 0))
    return pl.pallas_call(
        body, name=name, grid=(rows // tr,),
        in_specs=[blk, blk, blk, pl.BlockSpec((N_DEV, tr, cols), lambda i: (0, i, 0)),
                  pl.BlockSpec(dep.shape, lambda i: (0, 0))],
        out_specs=[blk] * 4, out_shape=[jax.ShapeDtypeStruct((rows, cols), F32)] * 4,
        compiler_params=_params(("parallel",)),
    )(w, m, v, parts, dep)


WEIGHTS = ("meta_tokens", "norm_mix_g", "w_in", "b_gate", "conv_w", "conv_b", "w_rec_a", "b_rec_a", "w_rec_i",
           "b_rec_i", "lru_lambda", "q_norm_g", "w_uq", "kv_norm_g", "w_ukv", "w_branch", "w_out", "norm_ffn_g",
           "w_ffn_in", "w_ffn_out", "final_norm_g")
SHARDED = {"meta_tokens": True, "w_in": True, "b_gate": True, "conv_w": True, "w_uq": True, "w_ukv": True,
           "w_branch": False, "w_out": False, "w_ffn_in": True, "w_ffn_out": False}
REPLICATED = tuple(n for n in WEIGHTS if n not in SHARDED)


def _as2d(a):
    return a.reshape(-1, a.shape[-1])


def _full_from_gathered(g, by_cols):
    if by_cols:
        return jnp.transpose(g, (1, 0, 2)).reshape(g.shape[1], N_DEV * g.shape[2])
    return g.reshape(N_DEV * g.shape[1], g.shape[2])


def _blocks_from_full(full, by_cols):
    if by_cols:
        r, c = full.shape
        return jnp.transpose(full.reshape(r, N_DEV, c // N_DEV), (1, 0, 2))
    return full.reshape(N_DEV, full.shape[0] // N_DEV, full.shape[1])


def kernel(x, meta_tokens, norm_mix_g, w_in, b_gate, conv_w, conv_b, w_rec_a, b_rec_a, w_rec_i, b_rec_i, lru_lambda, q_norm_g, w_uq, kv_norm_g, w_ukv, w_branch, w_out, norm_ffn_g, w_ffn_in, w_ffn_out, final_norm_g, loss_target, m_meta_tokens, m_norm_mix_g, m_w_in, m_b_gate, m_conv_w, m_conv_b, m_w_rec_a, m_b_rec_a, m_w_rec_i, m_b_rec_i, m_lru_lambda, m_q_norm_g, m_w_uq, m_kv_norm_g, m_w_ukv, m_w_branch, m_w_out, m_norm_ffn_g, m_w_ffn_in, m_w_ffn_out, m_final_norm_g, v_meta_tokens, v_norm_mix_g, v_w_in, v_b_gate, v_conv_w, v_conv_b, v_w_rec_a, v_b_rec_a, v_w_rec_i, v_b_rec_i, v_lru_lambda, v_q_norm_g, v_w_uq, v_kv_norm_g, v_w_ukv, v_w_branch, v_w_out, v_norm_ffn_g, v_w_ffn_in, v_w_ffn_out, v_final_norm_g):
    w = dict(meta_tokens=meta_tokens, norm_mix_g=norm_mix_g, w_in=w_in, b_gate=b_gate, conv_w=conv_w, conv_b=conv_b,
             w_rec_a=w_rec_a, b_rec_a=b_rec_a, w_rec_i=w_rec_i, b_rec_i=b_rec_i, lru_lambda=lru_lambda,
             q_norm_g=q_norm_g, w_uq=w_uq, kv_norm_g=kv_norm_g, w_ukv=w_ukv, w_branch=w_branch, w_out=w_out,
             norm_ffn_g=norm_ffn_g, w_ffn_in=w_ffn_in, w_ffn_out=w_ffn_out, final_norm_g=final_norm_g)
    m = dict(meta_tokens=m_meta_tokens, norm_mix_g=m_norm_mix_g, w_in=m_w_in, b_gate=m_b_gate, conv_w=m_conv_w,
             conv_b=m_conv_b, w_rec_a=m_w_rec_a, b_rec_a=m_b_rec_a, w_rec_i=m_w_rec_i, b_rec_i=m_b_rec_i,
             lru_lambda=m_lru_lambda, q_norm_g=m_q_norm_g, w_uq=m_w_uq, kv_norm_g=m_kv_norm_g, w_ukv=m_w_ukv,
             w_branch=m_w_branch, w_out=m_w_out, norm_ffn_g=m_norm_ffn_g, w_ffn_in=m_w_ffn_in,
             w_ffn_out=m_w_ffn_out, final_norm_g=m_final_norm_g)
    v = dict(meta_tokens=v_meta_tokens, norm_mix_g=v_norm_mix_g, w_in=v_w_in, b_gate=v_b_gate, conv_w=v_conv_w,
             conv_b=v_conv_b, w_rec_a=v_w_rec_a, b_rec_a=v_b_rec_a, w_rec_i=v_w_rec_i, b_rec_i=v_b_rec_i,
             lru_lambda=v_lru_lambda, q_norm_g=v_q_norm_g, w_uq=v_w_uq, kv_norm_g=v_kv_norm_g, w_ukv=v_w_ukv,
             w_branch=v_w_branch, w_out=v_w_out, norm_ffn_g=v_norm_ffn_g, w_ffn_in=v_w_ffn_in,
             w_ffn_out=v_w_ffn_out, final_norm_g=v_final_norm_g)

    seq, d_model = x.shape[1], x.shape[2]
    length = N_META + seq
    lp = -(-length // LANES) * LANES
    t_attn = _pick(lp, 640)
    t_rnn = LANES

    small = ("meta_tokens", "b_gate", "conv_w")
    names = list(SHARDED)
    first, mid, late = ("meta_tokens", "b_gate", "conv_w", "w_in"), ("w_uq", "w_ukv", "w_branch", "w_out"), (
        "w_ffn_in", "w_ffn_out")
    payload = lambda n: _as2d(w[n]) if n in small else _as2d(w[n]).astype(BF16)
    got = _exchange([payload(n) for n in first], [True] * len(first), "gather_first")
    mid_h = _exchange_start([payload(n) for n in mid], [True] * len(mid), got[0], "gather_mid_start")
    late_h = _exchange_start([payload(n) for n in late], [True] * len(late), mid_h["token"], "gather_late_start")
    full = {n: _full_from_gathered(g, SHARDED[n]) for n, g in zip(first, got)}

    splits = (D_RNN, D_RNN, Q_RANK, KV_RANK, QK_ROPE, 2 * d_model)
    offs = [0]
    for s in splits:
        offs.append(offs[-1] + s)
    w_x, w_g, w_q, w_kv, w_kr, w_m = (full["w_in"][:, offs[s]:offs[s + 1]] for s in range(6))
    w_kr = jnp.pad(w_kr, ((0, 0), (0, LANES - QK_ROPE)))
    bg = full["b_gate"].reshape(1, 2 * d_model)
    pv = jnp.concatenate([full["conv_w"], conv_b, b_rec_a, b_rec_i, lru_lambda], axis=0)
    wa_b, wi_b = w_rec_a[0].astype(BF16), w_rec_i[0].astype(BF16)
    g_final = final_norm_g.reshape(1, d_model)

    h0 = jnp.concatenate([full["meta_tokens"], x[0], jnp.zeros((lp - length, d_model), F32)], axis=0)
    tgt = jnp.pad(loss_target[0], ((N_META, lp - length), (0, 0)))
    tabs = _rope_tables(lp)

    z = _rmsnorm_fwd(h0, norm_mix_g, "norm_mix")
    ux = _mm([(z, w_x)], "nn", "in_x")
    ug = _mm([(z, w_g)], "nn", "in_g")
    uq = _mm([(z, w_q)], "nn", "in_q")
    ukv = _mm([(z, w_kv)], "nn", "in_kv")
    ukr = _mm([(z, w_kr)], "nn", "in_kr")
    um = _mm([(z, w_m)], "nn", "in_m", out_dtype=BF16)
    for n, g in zip(mid, _exchange_wait(mid_h, um, "gather_mid_wait")):
        full[n] = _full_from_gathered(g, SHARDED[n])
    w_uq_pad = jnp.pad(full["w_uq"].reshape(Q_RANK, N_HEADS, QK_NOPE + QK_ROPE),
                       ((0, 0), (0, 0), (0, HEAD_PAD - QK_NOPE - QK_ROPE))).reshape(Q_RANK, N_HEADS * HEAD_PAD)
    w_ukv3 = full["w_ukv"].reshape(KV_RANK, N_HEADS, QK_NOPE + V_DIM)
    w_k_pad = jnp.pad(w_ukv3[:, :, :QK_NOPE], ((0, 0), (0, 0), (0, HEAD_PAD - QK_NOPE))).reshape(
        KV_RANK, N_HEADS * HEAD_PAD)
    w_v = w_ukv3[:, :, QK_NOPE:].reshape(KV_RANK, N_HEADS * V_DIM)
    wb_r, wb_a = full["w_branch"][:D_RNN], full["w_branch"][D_RNN:]
    y_rnn, hs = _rglru_fwd(ux, ug, pv, wa_b, wi_b, t_rnn, "rglru_fwd")
    qn = _rmsnorm_fwd(uq, q_norm_g, "norm_q")
    kvn = _rmsnorm_fwd(ukv, kv_norm_g, "norm_kv")
    qpad = _mm([(qn, w_uq_pad)], "nn", "up_q", out_dtype=BF16)
    kpad = _mm([(kvn, w_k_pad)], "nn", "up_k", out_dtype=BF16)
    vh = _mm([(kvn, w_v)], "nn", "up_v", out_dtype=BF16)
    qh, kh = _rope_fwd(qpad, kpad, ukr, tabs, "rope_fwd")
    oh, lse = _attn_fwd(qh, kh, vh, t_attn, "attn_fwd")
    p_rnn = _mm([(y_rnn, wb_r)], "nn", "branch_rnn", out_dtype=BF16)
    p_att = _mm([(oh, wb_a)], "nn", "branch_att", out_dtype=BF16)
    mixed = _mix_fwd(um, p_rnn, p_att, bg, "mix_fwd")
    h1 = _mm([(mixed, full["w_out"])], "nn", "out_proj", res=h0)
    for n, g in zip(late, _exchange_wait(late_h, h1, "gather_late_wait")):
        full[n] = _full_from_gathered(g, SHARDED[n])
    zf = _rmsnorm_fwd(h1, norm_ffn_g, "norm_ffn")
    gu = _mm([(zf, full["w_ffn_in"])], "nn", "ffn_in", out_dtype=BF16)
    act = _swiglu_fwd(gu, "swiglu_fwd")
    h2 = _mm([(act, full["w_ffn_out"])], "nn", "ffn_out", res=h1)
    dh2, dg_final, _, loss_part = _loss_bwd(h2, tgt, g_final, seq, "loss_bwd")

    d_act = _mm([(dh2, full["w_ffn_out"])], "nt", "d_act", out_dtype=BF16)
    dw_ffn_out = _mm_tn(act, dh2, "dw_ffn_out")
    d_gu = _swiglu_bwd(gu, d_act, "swiglu_bwd")
    dw_ffn_in = _mm_tn(zf, d_gu, "dw_ffn_in")
    blocks = lambda n, g: _blocks_from_full(g, SHARDED[n]).astype(F32 if n in small else BF16)
    sent = {("w_ffn_in", "w_ffn_out"): _exchange_start(
        [blocks("w_ffn_in", dw_ffn_in), blocks("w_ffn_out", dw_ffn_out)], [False] * 2, dg_final, "scatter_ffn_start")}
    d_zf = _mm([(d_gu, full["w_ffn_in"], D_FF, 0), (d_gu, full["w_ffn_in"], D_FF, 1)], "nt", "d_zf")
    dh1, dg_ffn = _rmsnorm_bwd(h1, d_zf, norm_ffn_g, "norm_ffn_bwd", res=dh2,
                               dep=sent[("w_ffn_in", "w_ffn_out")]["token"])
    d_mixed = _mm([(dh1, full["w_out"])], "nt", "d_mixed", out_dtype=BF16)
    dw_out = _mm_tn(mixed, dh1, "dw_out")
    d_prnn, d_patt, d_um, dbg = _mix_bwd(um, p_rnn, p_att, d_mixed, bg, "mix_bwd")
    d_yrnn = _mm([(d_prnn, wb_r)], "nt", "d_yrnn")
    d_oh = _mm([(d_patt, wb_a)], "nt", "d_oh", out_dtype=BF16)
    dwb_r = _mm_tn(y_rnn, d_prnn, "dw_branch_rnn")
    dwb_a = _mm_tn(oh, d_patt, "dw_branch_att")
    sent[("w_out", "w_branch")] = _exchange_start(
        [blocks("w_out", dw_out), blocks("w_branch", jnp.concatenate([dwb_r, dwb_a], axis=0))], [False] * 2,
        dg_ffn, "scatter_mix_start")
    dqh, l2row, dlrow = _attn_bwd_dq(qh, kh, vh, d_oh, oh, lse, sent[("w_out", "w_branch")]["token"], t_attn,
                                     "attn_bwd_dq")
    dkh, dvh = _attn_bwd_dkv(qh, kh, vh, d_oh, l2row, dlrow, t_attn, "attn_bwd_dkv")
    dqpad, dkpad, dukr = _rope_bwd(dqh, dkh, tabs, "rope_bwd")
    d_qn = _mm([(dqpad, w_uq_pad)], "nt", "d_qn")
    dw_uq_pad = _mm_tn(qn, dqpad, "dw_uq")
    d_kvn = _mm([(dkpad, w_k_pad), (dvh, w_v)], "nt", "d_kvn")
    dw_k_pad = _mm_tn(kvn, dkpad, "dw_uk")
    dw_v = _mm_tn(kvn, dvh, "dw_uv")
    dw_uq = dw_uq_pad.reshape(Q_RANK, N_HEADS, HEAD_PAD)[:, :, :QK_NOPE + QK_ROPE].reshape(Q_RANK, -1)
    dw_ukv = jnp.concatenate([dw_k_pad.reshape(KV_RANK, N_HEADS, HEAD_PAD)[:, :, :QK_NOPE],
                              dw_v.reshape(KV_RANK, N_HEADS, V_DIM)], axis=2).reshape(KV_RANK, -1)
    sent[("w_uq", "w_ukv")] = _exchange_start([blocks("w_uq", dw_uq), blocks("w_ukv", dw_ukv)], [False] * 2,
                                              dbg, "scatter_attn_start")
    duq, dg_q = _rmsnorm_bwd(uq, d_qn, q_norm_g, "norm_q_bwd", out_dtype=BF16)
    dukv, dg_kv = _rmsnorm_bwd(ukv, d_kvn, kv_norm_g, "norm_kv_bwd", out_dtype=BF16)
    dux, dug, dpv, dwa, dwi = _rglru_bwd(ux, ug, hs, d_yrnn, pv, wa_b, wi_b, sent[("w_uq", "w_ukv")]["token"],
                                         t_rnn, "rglru_bwd")
    d_z = _mm([(dux, w_x), (dug, w_g), (duq, w_q), (dukv, w_kv), (dukr, w_kr), (d_um, w_m)], "nt", "d_z")
    dw_in = jnp.concatenate([
        _mm_tn(z, dux, "dw_in_x"), _mm_tn(z, dug, "dw_in_g"), _mm_tn(z, duq, "dw_in_q"),
        _mm_tn(z, dukv, "dw_in_kv"), _mm_tn(z, dukr, "dw_in_kr")[:, :QK_ROPE], _mm_tn(z, d_um, "dw_in_m")], axis=1)
    dh0, dg_mix = _rmsnorm_bwd(h0, d_z, norm_mix_g, "norm_mix_bwd", res=dh1)

    grad_last = dict(meta_tokens=dh0[:N_META], b_gate=dbg.reshape(2, d_model), conv_w=dpv[:CONV_WIDTH], w_in=dw_in)
    grad_rep = dict(
        norm_mix_g=dg_mix, conv_b=dpv[4:5], w_rec_a=dwa, b_rec_a=dpv[5:6], w_rec_i=dwi, b_rec_i=dpv[6:7],
        lru_lambda=dpv[7:8], q_norm_g=dg_q, kv_norm_g=dg_kv, norm_ffn_g=dg_ffn, final_norm_g=dg_final)

    rep_payload = [_as2d(grad_rep[n]).astype(BF16 if n in ("w_rec_a", "w_rec_i") else F32) for n in REPLICATED]
    last_h = _exchange_start([blocks(n, grad_last[n]) for n in first] + rep_payload,
                             [False] * len(first) + [True] * len(REPLICATED), dg_mix, "scatter_last_start")
    grads, deltas, new_m, new_v = {}, {}, {}, {}

    def update(n, parts, dep):
        g2, d2, m2, v2 = _adamw(_as2d(w[n]), _as2d(m[n]), _as2d(v[n]), parts, dep, "adamw_" + n)
        for store, val in ((grads, g2), (deltas, d2), (new_m, m2), (new_v, v2)):
            store[n] = val.reshape(w[n].shape)

    chain = last_h["token"]
    for group, handle in sent.items():
        for n, parts in zip(group, _exchange_wait(handle, chain, "scatter_wait_" + group[0])):
            update(n, parts, chain)
            chain = _as2d(deltas[n])[:SUBLANES, :LANES]
    for n, parts in zip(first + REPLICATED, _exchange_wait(last_h, chain, "scatter_last_wait")):
        update(n, parts, last_h["token"])

    loss = lax.psum(loss_part[0, 0], MESH_AXES)
    grad_x = dh0[N_META:length][None]
    return (loss, grad_x, *[grads[n] for n in WEIGHTS], *[deltas[n] for n in WEIGHTS],
            *[new_m[n] for n in WEIGHTS], *[new_v[n] for n in WEIGHTS])
```

```python
import functools
import math

import jax
import jax.numpy as jnp
from jax import lax
from jax.experimental import pallas as pl
from jax.experimental.pallas import tpu as pltpu

F32 = jnp.float32
BF16 = jnp.bfloat16

N_DEV = 8
MESH_AXES = ("x", "y", "c")
LANES = 128
SUBLANES = 8
VMEM_LIMIT = 56 * 1024 * 1024

N_META = 16
CHUNK_SHIFT = 6
CHUNK_BIAS = 64 - N_META
EPS = 1e-6
D_RNN = 1280
RNN_BLOCKS = 10
CONV_WIDTH = 4
LRU_C = 8.0
N_HEADS = 8
QK_NOPE = 128
QK_ROPE = 64
V_DIM = 128
HEAD_PAD = 256
Q_RANK = 384
KV_RANK = 256
ROPE_THETA = 10000.0
ATTN_SCALE = 1.0 / math.sqrt(QK_NOPE + QK_ROPE)
NEG = -1e30
TINY = 1e-30
LOG2E = 1.0 / math.log(2.0)
SCALE_LOG2E = ATTN_SCALE * LOG2E
Q_SPLIT = 2
Q_ALIGN = LANES // Q_SPLIT
ROW_GROUP = 32
SPILL = LANES
D_FF = 2816

ADAM_LR = 0.001
ADAM_B1 = 0.9
ADAM_B2 = 0.999
ADAM_EPS = 1e-08
ADAM_WD = 0.01
ADAM_STEP = 10

NN = (((1,), (0,)), ((), ()))
NT = (((1,), (1,)), ((), ()))
TN = (((0,), (0,)), ((), ()))


def _pick(n, cap, base=LANES):
    best = None
    for t in range(base, min(n, cap) + 1, base):
        if n % t == 0:
            best = t
    return best if best is not None else n


def _params(sem=None):
    return pltpu.CompilerParams(dimension_semantics=sem, vmem_limit_bytes=VMEM_LIMIT)


def _sig(x):
    return 0.5 + 0.5 * jnp.tanh(0.5 * x)


def _exchange(srcs, gather, name):
    n = len(srcs)
    out_shape = [jax.ShapeDtypeStruct((N_DEV,) + (s.shape if g else s.shape[1:]), s.dtype)
                 for s, g in zip(srcs, gather)]

    def body(*refs):
        src, dst = refs[:n], refs[n:2 * n]
        send_sems, recv_sems, local_sems = refs[2 * n:]
        x, y, c = lax.axis_index("x"), lax.axis_index("y"), lax.axis_index("c")
        me = 4 * x + 2 * y + c
        local = []
        for t in range(n):
            cp = pltpu.make_async_copy(src[t] if gather[t] else src[t].at[me], dst[t].at[me], local_sems.at[t])
            cp.start()
            local.append(cp)
        sends, recvs = [], []
        for k in range(1, N_DEV):
            px = 1 - x if k & 4 else x
            py = 1 - y if k & 2 else y
            pc = 1 - c if k & 1 else c
            peer = 4 * px + 2 * py + pc
            for t in range(n):
                cp = pltpu.make_async_remote_copy(
                    src_ref=src[t] if gather[t] else src[t].at[peer], dst_ref=dst[t].at[me],
                    send_sem=send_sems.at[t, k - 1], recv_sem=recv_sems.at[t, k - 1],
                    device_id=(px, py, pc), device_id_type=pl.DeviceIdType.MESH)
                cp.start()
                sends.append(cp)
                recvs.append(pltpu.make_async_remote_copy(
                    src_ref=src[t] if gather[t] else src[t].at[peer], dst_ref=dst[t].at[peer],
                    send_sem=send_sems.at[t, k - 1], recv_sem=recv_sems.at[t, k - 1],
                    device_id=(px, py, pc), device_id_type=pl.DeviceIdType.MESH))
        for cp in recvs:
            cp.wait_recv()
        for cp in sends:
            cp.wait_send()
        for cp in local:
            cp.wait()

    any_spec = pl.BlockSpec(memory_space=pl.ANY)
    return pl.pallas_call(
        body, name=name, out_shape=out_shape,
        in_specs=[any_spec] * n, out_specs=[any_spec] * n,
        scratch_shapes=[pltpu.SemaphoreType.DMA((n, N_DEV - 1)), pltpu.SemaphoreType.DMA((n, N_DEV - 1)),
                        pltpu.SemaphoreType.DMA((n,))],
    )(*srcs)


HBM_SPEC = pl.BlockSpec(memory_space=pltpu.HBM)
SEM_SPEC = pl.BlockSpec(memory_space=pltpu.SEMAPHORE)
DATAFLOW = pltpu.SideEffectType.DATAFLOW_SIDE_EFFECTING


def _peers(x, y, c):
    out = []
    for k in range(1, N_DEV):
        px = 1 - x if k & 4 else x
        py = 1 - y if k & 2 else y
        pc = 1 - c if k & 1 else c
        out.append((k, (px, py, pc), 4 * px + 2 * py + pc))
    return out


def _split_copies(src, land, gather, send_sems, recv_sems, local_sems):
    x, y, c = lax.axis_index("x"), lax.axis_index("y"), lax.axis_index("c")
    me = 4 * x + 2 * y + c
    n = len(src)
    local = [pltpu.make_async_copy(src[t] if gather[t] else src[t].at[me], land[t].at[me], local_sems.at[t])
             for t in range(n)]
    sends, recvs = [], []
    for k, pos, peer in _peers(x, y, c):
        for t in range(n):
            mine = src[t] if gather[t] else src[t].at[peer]
            slot = t * (N_DEV - 1) + k - 1
            common = dict(send_sem=send_sems.at[slot], recv_sem=recv_sems.at[slot], device_id=pos,
                          device_id_type=pl.DeviceIdType.MESH)
            sends.append(pltpu.make_async_remote_copy(src_ref=mine, dst_ref=land[t].at[me], **common))
            recvs.append(pltpu.make_async_remote_copy(src_ref=mine, dst_ref=land[t].at[peer], **common))
    return local, sends, recvs


def _exchange_start(srcs, gather, after, name):
    n = len(srcs)
    lands = [lax.empty((N_DEV,) + (s.shape if g else s.shape[1:]), s.dtype) for s, g in zip(srcs, gather)]

    def body(*refs):
        src, land = refs[:n], refs[n:2 * n]
        send_sems, recv_sems, local_sems = refs[2 * n + 1:2 * n + 4]
        local, sends, _ = _split_copies(src, land, gather, send_sems, recv_sems, local_sems)
        for cp in local + sends:
            cp.start()
        refs[-1][...] = jnp.zeros_like(refs[-1])

    hbm = lambda a: pltpu.HBM(a.shape, a.dtype)
    outs = pl.pallas_call(
        body, name=name,
        out_shape=(pltpu.SemaphoreType.DMA((n * (N_DEV - 1),)), pltpu.SemaphoreType.DMA((n * (N_DEV - 1),)),
                   pltpu.SemaphoreType.DMA((n,)), *[hbm(s) for s in srcs], *[hbm(a) for a in lands],
                   jax.ShapeDtypeStruct((SUBLANES, LANES), F32)),
        in_specs=[HBM_SPEC] * (2 * n) + [pl.BlockSpec(memory_space=pl.ANY)],
        out_specs=(SEM_SPEC, SEM_SPEC, SEM_SPEC, *[HBM_SPEC] * (2 * n), pl.BlockSpec(memory_space=pltpu.VMEM)),
        input_output_aliases={t: 3 + t for t in range(2 * n)},
        compiler_params=pltpu.CompilerParams(has_side_effects=DATAFLOW),
    )(*[pltpu.with_memory_space_constraint(a, pltpu.HBM) for a in list(srcs) + lands], after)
    return dict(sems=outs[:3], srcs=outs[3:3 + n], lands=outs[3 + n:3 + 2 * n], token=outs[-1], gather=gather)


def _exchange_wait(handle, after, name):
    srcs, lands, gather = handle["srcs"], handle["lands"], handle["gather"]
    n = len(srcs)

    def body(*refs):
        src, land = refs[:n], refs[n:2 * n]
        send_sems, recv_sems, local_sems = refs[2 * n:2 * n + 3]
        local, sends, recvs = _split_copies(src, land, gather, send_sems, recv_sems, local_sems)
        for cp in sends:
            cp.wait_send()
        for cp in recvs:
            cp.wait_recv()
        for cp in local:
            cp.wait()

    hbm = lambda a: pltpu.HBM(a.shape, a.dtype)
    outs = pl.pallas_call(
        body, name=name, out_shape=(*[hbm(s) for s in srcs], *[hbm(a) for a in lands]),
        in_specs=[HBM_SPEC] * (2 * n) + [SEM_SPEC] * 3 + [pl.BlockSpec(memory_space=pl.ANY)],
        out_specs=[HBM_SPEC] * (2 * n), input_output_aliases={t: t for t in range(2 * n)},
        compiler_params=pltpu.CompilerParams(has_side_effects=DATAFLOW),
    )(*srcs, *lands, *handle["sems"], after)
    return outs[n:]


def _mm(pairs, mode, name, res=None, out_dtype=F32):
    pairs = [p if len(p) == 4 else (p[0], p[1], p[0].shape[1], 0) for p in pairs]
    m = pairs[0][0].shape[0]
    n = pairs[0][1].shape[1] if mode == "nn" else pairs[0][1].shape[0]
    tm, tn = _pick(m, 640), _pick(n, 1408)
    np_ = len(pairs)
    dims = NN if mode == "nn" else NT

    def body(*refs):
        acc = None
        for s in range(np_):
            d = lax.dot_general(refs[2 * s][...].astype(BF16), refs[2 * s + 1][...].astype(BF16), dims,
                                preferred_element_type=F32)
            acc = d if acc is None else acc + d
        if res is not None:
            acc = acc + refs[2 * np_][...]
        refs[-1][...] = acc.astype(out_dtype)

    in_specs, args = [], []
    for a, b, kt, kb in pairs:
        in_specs.append(pl.BlockSpec((tm, kt), lambda i, j, kb=kb: (i, kb)))
        if mode == "nn":
            in_specs.append(pl.BlockSpec((kt, tn), lambda i, j, kb=kb: (kb, j)))
        else:
            in_specs.append(pl.BlockSpec((tn, kt), lambda i, j, kb=kb: (j, kb)))
        args += [a, b]
    if res is not None:
        in_specs.append(pl.BlockSpec((tm, tn), lambda i, j: (i, j)))
        args.append(res)
    return pl.pallas_call(
        body, name=name, grid=(m // tm, n // tn), in_specs=in_specs,
        out_specs=pl.BlockSpec((tm, tn), lambda i, j: (i, j)),
        out_shape=jax.ShapeDtypeStruct((m, n), out_dtype),
        compiler_params=_params(("parallel", "parallel")),
    )(*args)


def _mm_tn(a, b, name):
    m, k = a.shape
    n = b.shape[1]
    tm, tk, tn = _pick(m, 1664), _pick(k, 1408), _pick(n, 1408)

    def body(a_ref, b_ref, o_ref):
        @pl.when(pl.program_id(2) == 0)
        def _():
            o_ref[...] = jnp.zeros_like(o_ref)

        o_ref[...] += lax.dot_general(a_ref[...].astype(BF16), b_ref[...].astype(BF16), TN,
                                      preferred_element_type=F32)

    return pl.pallas_call(
        body, name=name, grid=(k // tk, n // tn, m // tm),
        in_specs=[pl.BlockSpec((tm, tk), lambda i, j, r: (r, i)), pl.BlockSpec((tm, tn), lambda i, j, r: (r, j))],
        out_specs=pl.BlockSpec((tk, tn), lambda i, j, r: (i, j)),
        out_shape=jax.ShapeDtypeStruct((k, n), F32),
        compiler_params=_params(("parallel", "parallel", "arbitrary")),
    )(a, b)


ROW_TILE_BYTES = 6 * 1024 * 1024


def _row_tile(rows, row_in, row_out):
    per_row = sum((r[1] * r[0].dtype.itemsize) if isinstance(r, tuple) else (r.shape[1] * r.dtype.itemsize)
                  for r in row_in)
    per_row += sum(w * jnp.dtype(dt).itemsize for w, dt in row_out)
    return _pick(rows, min(640, max(LANES, ROW_TILE_BYTES // per_row)))


def _rowcall(body, name, rows, row_in, full_in, row_out, acc_out=()):
    tr = _row_tile(rows, row_in, row_out)
    n_steps = rows // tr
    in_specs, args = [], []
    for r in row_in:
        arr, w, cb = r if isinstance(r, tuple) else (r, r.shape[1], 0)
        in_specs.append(pl.BlockSpec((tr, w), lambda i, cb=cb: (i, cb)))
        args.append(arr)
    for f in full_in:
        in_specs.append(pl.BlockSpec(f.shape, lambda i, nd=f.ndim: (0,) * nd))
        args.append(f)
    out_specs = [pl.BlockSpec((tr, w), lambda i: (i, 0)) for w, _ in row_out]
    out_shape = [jax.ShapeDtypeStruct((rows, w), dt) for w, dt in row_out]
    for shp, dt in acc_out:
        out_specs.append(pl.BlockSpec(shp, lambda i, nd=len(shp): (0,) * nd))
        out_shape.append(jax.ShapeDtypeStruct(shp, dt))

    def wrapped(*refs):
        body(pl.program_id(0), n_steps, *refs)

    return pl.pallas_call(
        wrapped, name=name, grid=(n_steps,), in_specs=in_specs, out_specs=out_specs, out_shape=out_shape,
        compiler_params=_params(("arbitrary",) if acc_out else ("parallel",)),
    )(*args)


def _rmsnorm_fwd(x, g, name):
    rows, w = x.shape

    def body(i, n, x_ref, g_ref, o_ref):
        xv = x_ref[...]
        r = lax.rsqrt(jnp.mean(xv * xv, axis=-1, keepdims=True) + EPS)
        o_ref[...] = (xv * r * g_ref[...]).astype(BF16)

    return _rowcall(body, name, rows,[x], [g], [(w, BF16)])[0]


def _rmsnorm_bwd_math(xv, dy, g):
    w = xv.shape[-1]
    r = lax.rsqrt(jnp.mean(xv * xv, axis=-1, keepdims=True) + EPS)
    t = dy * g
    dx = r * t - xv * (r * r * r * (jnp.sum(t * xv, axis=-1, keepdims=True) / w))
    dg = jnp.sum(dy * xv * r, axis=0, keepdims=True)
    return dx, dg


def _rmsnorm_bwd(x, dy, g, name, res=None, out_dtype=F32, dep=None):
    rows, w = x.shape

    def body(i, n, *refs):
        x_ref, dy_ref = refs[0], refs[1]
        g_ref, dx_ref, dg_ref = refs[-3], refs[-2], refs[-1]
        dx, dg = _rmsnorm_bwd_math(x_ref[...], dy_ref[...], g_ref[...])
        if res is not None:
            dx = dx + refs[2][...]
        dx_ref[...] = dx.astype(out_dtype)

        @pl.when(i == 0)
        def _():
            dg_ref[...] = jnp.zeros_like(dg_ref)

        dg_ref[...] += dg

    row_in = [x, dy] + ([res] if res is not None else [])
    return _rowcall(body, name, rows, row_in, ([dep] if dep is not None else []) + [g], [(w, out_dtype)],
                    [((1, w), F32)])


def _loss_bwd(h2, tgt, g, seq, name):
    rows, w = h2.shape
    tr = _row_tile(rows, [h2, tgt], [(w, F32)])

    def body(i, n, h_ref, t_ref, g_ref, dh_ref, dg_ref, lcol_ref, loss_ref):
        hv, gv = h_ref[...], g_ref[...]
        row = i * tr + lax.broadcasted_iota(jnp.int32, (tr, w), 0)
        valid = jnp.logical_and(row >= N_META, row < N_META + seq)
        r = lax.rsqrt(jnp.mean(hv * hv, axis=-1, keepdims=True) + EPS)
        err = jnp.where(valid, hv * r * gv - t_ref[...], 0.0)
        dx, dg = _rmsnorm_bwd_math(hv, err * (1.0 / w), gv)
        dh_ref[...] = dx

        @pl.when(i == 0)
        def _():
            dg_ref[...] = jnp.zeros_like(dg_ref)
            lcol_ref[...] = jnp.zeros_like(lcol_ref)

        dg_ref[...] += dg
        lcol_ref[...] += jnp.sum(err * err, axis=0, keepdims=True)

        @pl.when(i == n - 1)
        def _():
            total = jnp.sum(lcol_ref[...], axis=1, keepdims=True) * (0.5 / w)
            loss_ref[...] = jnp.broadcast_to(total, loss_ref.shape)

    return _rowcall(body, name, rows, [h2, tgt], [g], [(w, F32)],
                    [((1, w), F32), ((1, w), F32), ((1, LANES), F32)])


def _mix_fwd(um, p_rnn, p_att, bg, name):
    rows, d = p_rnn.shape

    def body(i, n, u0_ref, u1_ref, pr_ref, pa_ref, bg_ref, o_ref):
        g0 = _sig(u0_ref[...].astype(F32) + bg_ref[:, :d])
        g1 = _sig(u1_ref[...].astype(F32) + bg_ref[:, d:])
        o_ref[...] = (g0 * pr_ref[...].astype(F32) + g1 * pa_ref[...].astype(F32)).astype(BF16)

    return _rowcall(body, name, rows,[(um, d, 0), (um, d, 1), p_rnn, p_att], [bg],
                    [(d, BF16)])[0]


def _mix_bwd(um, p_rnn, p_att, dmix, bg, name):
    rows, d = p_rnn.shape

    def body(i, n, u0_ref, u1_ref, pr_ref, pa_ref, dm_ref, bg_ref, dpr_ref, dpa_ref, dum_ref, dbg_ref):
        g0 = _sig(u0_ref[...].astype(F32) + bg_ref[:, :d])
        g1 = _sig(u1_ref[...].astype(F32) + bg_ref[:, d:])
        dm = dm_ref[...].astype(F32)
        dpr_ref[...] = (dm * g0).astype(BF16)
        dpa_ref[...] = (dm * g1).astype(BF16)
        du0 = dm * pr_ref[...].astype(F32) * g0 * (1.0 - g0)
        du1 = dm * pa_ref[...].astype(F32) * g1 * (1.0 - g1)
        dum_ref[:, :d] = du0.astype(BF16)
        dum_ref[:, d:] = du1.astype(BF16)

        @pl.when(i == 0)
        def _():
            dbg_ref[...] = jnp.zeros_like(dbg_ref)

        dbg_ref[:, :d] += jnp.sum(du0, axis=0, keepdims=True)
        dbg_ref[:, d:] += jnp.sum(du1, axis=0, keepdims=True)

    return _rowcall(body, name, rows,[(um, d, 0), (um, d, 1), p_rnn, p_att, dmix], [bg],
                    [(d, BF16), (d, BF16), (2 * d, BF16)], [((1, 2 * d), F32)])


def _swiglu_fwd(gu, name):
    rows, w2 = gu.shape
    f = w2 // 2

    def body(i, n, g_ref, u_ref, o_ref):
        gate = g_ref[...].astype(F32)
        o_ref[...] = (gate * _sig(gate) * u_ref[...].astype(F32)).astype(BF16)

    return _rowcall(body, name, rows,[(gu, f, 0), (gu, f, 1)], [], [(f, BF16)])[0]


def _swiglu_bwd(gu, dact, name):
    rows, w2 = gu.shape
    f = w2 // 2

    def body(i, n, g_ref, u_ref, da_ref, o_ref):
        gate, da = g_ref[...].astype(F32), da_ref[...].astype(F32)
        sg = _sig(gate)
        o_ref[:, :f] = (da * u_ref[...].astype(F32) * (sg * (1.0 + gate * (1.0 - sg)))).astype(BF16)
        o_ref[:, f:] = (da * gate * sg).astype(BF16)

    return _rowcall(body, name, rows,[(gu, f, 0), (gu, f, 1), dact], [], [(w2, BF16)])[0]


def _rope_tables(lp):
    idx = jnp.arange(lp, dtype=jnp.int32).astype(F32)
    inv_freq = ROPE_THETA ** (-jnp.arange(0, QK_ROPE, 2, dtype=F32) / QK_ROPE)
    ang = idx[:, None] * inv_freq[None, :]
    cos, sin = jnp.cos(ang), jnp.sin(ang)
    half = QK_ROPE // 2
    z = lambda wdt: jnp.zeros((lp, wdt), F32)
    tc = jnp.concatenate([cos, cos, z(LANES - 2 * half)], axis=1)
    ts1 = jnp.concatenate([-sin, z(LANES - half)], axis=1)
    ts2 = jnp.concatenate([z(half), sin, z(LANES - 2 * half)], axis=1)
    return tc, ts1, ts2


def _rope(xv, tc, ts1, ts2):
    half = QK_ROPE // 2
    return xv * tc + pltpu.roll(xv, LANES - half, 1) * ts1 + pltpu.roll(xv, half, 1) * ts2


def _rope_t(dv, tc, ts1, ts2):
    half = QK_ROPE // 2
    return dv * tc + pltpu.roll(dv * ts1, half, 1) + pltpu.roll(dv * ts2, LANES - half, 1)


def _rope_fwd(qpad, kpad, ukr, tabs, name):
    rows, w = qpad.shape

    def body(i, n, q_ref, k_ref, r_ref, c_ref, s1_ref, s2_ref, qo_ref, ko_ref):
        tc, ts1, ts2 = c_ref[...], s1_ref[...], s2_ref[...]
        kr = _rope(r_ref[...], tc, ts1, ts2).astype(BF16)
        for h in range(N_HEADS):
            lo, mid, hi = h * HEAD_PAD, h * HEAD_PAD + QK_NOPE, (h + 1) * HEAD_PAD
            qo_ref[:, lo:mid] = q_ref[:, lo:mid].astype(BF16)
            qo_ref[:, mid:hi] = _rope(q_ref[:, mid:hi].astype(F32), tc, ts1, ts2).astype(BF16)
            ko_ref[:, lo:mid] = k_ref[:, lo:mid].astype(BF16)
            ko_ref[:, mid:hi] = kr

    return _rowcall(body, name, rows,[qpad, kpad, ukr, *tabs], [], [(w, BF16), (w, BF16)])


def _rope_bwd(dq, dk, tabs, name):
    rows, w = dq.shape

    def body(i, n, dq_ref, dk_ref, c_ref, s1_ref, s2_ref, qo_ref, ko_ref, ro_ref):
        tc, ts1, ts2 = c_ref[...], s1_ref[...], s2_ref[...]
        dkr = None
        for h in range(N_HEADS):
            lo, mid, hi = h * HEAD_PAD, h * HEAD_PAD + QK_NOPE, (h + 1) * HEAD_PAD
            qo_ref[:, lo:mid] = dq_ref[:, lo:mid].astype(BF16)
            qo_ref[:, mid:hi] = _rope_t(dq_ref[:, mid:hi].astype(F32), tc, ts1, ts2).astype(BF16)
            ko_ref[:, lo:mid] = dk_ref[:, lo:mid].astype(BF16)
            ko_ref[:, mid:hi] = jnp.zeros((ko_ref.shape[0], hi - mid), BF16)
            part = dk_ref[:, mid:hi].astype(F32)
            dkr = part if dkr is None else dkr + part
        ro_ref[...] = _rope_t(dkr, tc, ts1, ts2).astype(BF16)

    return _rowcall(body, name, rows,[dq, dk, *tabs], [],
                    [(w, BF16), (w, BF16), (LANES, BF16)])


def _visible(q0, k0, tq, tk):
    qrow = q0 + lax.broadcasted_iota(jnp.int32, (tq, tk), 0)
    kcol = k0 + lax.broadcasted_iota(jnp.int32, (tq, tk), 1)
    return ((kcol + CHUNK_BIAS) >> CHUNK_SHIFT) <= ((qrow + CHUNK_BIAS) >> CHUNK_SHIFT)


def _visible_t(k0, q0, tk, tq):
    krow = k0 + lax.broadcasted_iota(jnp.int32, (tk, tq), 0)
    qcol = q0 + lax.broadcasted_iota(jnp.int32, (tk, tq), 1)
    return ((krow + CHUNK_BIAS) >> CHUNK_SHIFT) <= ((qcol + CHUNK_BIAS) >> CHUNK_SHIFT)


def _lanes(v, width):
    return jnp.tile(v, (1, width // LANES))


def _pipelined_chunks(n_full, scores, absorb):
    scores(0, 0)

    def pair(jj, carry):
        a = 2 * jj
        scores(a + 1, 1)
        absorb(a, 0, False)
        scores(a + 2, 0)
        absorb(a + 1, 1, False)
        return carry

    lax.fori_loop(0, n_full // 2, pair, 0)

    @pl.when(n_full % 2 == 0)
    def _():
        absorb(n_full, 0, True)

    @pl.when(n_full % 2 == 1)
    def _():
        scores(n_full, 1)
        absorb(n_full - 1, 0, False)
        absorb(n_full, 1, True)


def _attn_fwd(q, k, v, t, name):
    lp = q.shape[0]
    nt = lp // t

    def body(q_ref, k_ref, v_ref, o_ref, lse_ref, m_s, l_s, acc_s, a_s, sa_s, sb_s, p_s):
        i = pl.program_id(1)
        m_s[...] = jnp.full(m_s.shape, NEG, F32)
        l_s[...] = jnp.zeros(l_s.shape, F32)
        acc_s[...] = jnp.zeros(acc_s.shape, F32)

        s_bufs = (sa_s, sb_s)

        def scores(j, slot):
            r0 = pl.multiple_of(j * t, t)
            s_bufs[slot][...] = lax.dot_general(q_ref[...], k_ref[pl.ds(r0, t), :], NT,
                                                preferred_element_type=F32)

        def absorb(j, slot, masked):
            for r in range(0, t, ROW_GROUP):
                rows = slice(r, r + ROW_GROUP)
                s = s_bufs[slot][rows, :]
                if masked:
                    s = jnp.where(_visible(i * t + r, i * t, ROW_GROUP, t), s, NEG)
                m_prev = m_s[rows, :]
                m_new = jnp.maximum(m_prev, jnp.max(s, axis=1, keepdims=True))
                alpha = jnp.exp2((m_prev - m_new) * SCALE_LOG2E)
                p = jnp.exp2((s - _lanes(m_new, t)) * SCALE_LOG2E)
                l_s[rows, :] = alpha * l_s[rows, :] + jnp.sum(p, axis=1, keepdims=True)
                m_s[rows, :] = m_new
                a_s[rows, :] = alpha
                p_s[rows, :] = p.astype(BF16)
            r0 = pl.multiple_of(j * t, t)
            acc_s[...] = a_s[...] * acc_s[...] + jnp.dot(p_s[...], v_ref[pl.ds(r0, t), :],
                                                         preferred_element_type=F32)

        r1 = pl.multiple_of(jnp.minimum(i + 1, nt - 1) * t, t)
        rows = slice(t - SPILL, t)
        s = lax.dot_general(q_ref[rows, :], k_ref[pl.ds(r1, SPILL), :], NT, preferred_element_type=F32)
        seen = jnp.logical_and(_visible(i * t + t - SPILL, (i + 1) * t, SPILL, SPILL), i + 1 < nt)
        s = jnp.where(seen, s, NEG)
        m_new = jnp.max(s, axis=1, keepdims=True)
        p = jnp.exp2((s - m_new) * SCALE_LOG2E)
        l_s[rows, :] = jnp.broadcast_to(jnp.sum(p, axis=1, keepdims=True), (SPILL, LANES))
        acc_s[rows, :] = jnp.dot(p.astype(BF16), v_ref[pl.ds(r1, SPILL), :], preferred_element_type=F32)
        m_s[rows, :] = jnp.broadcast_to(m_new, (SPILL, LANES))

        _pipelined_chunks(i, scores, absorb)
        o_ref[...] = (acc_s[...] / l_s[...]).astype(BF16)
        lse_ref[...] = m_s[...] * ATTN_SCALE + jnp.log(l_s[...])

    return pl.pallas_call(
        body, name=name, grid=(N_HEADS, nt),
        in_specs=[pl.BlockSpec((t, HEAD_PAD), lambda h, i: (i, h)),
                  pl.BlockSpec((lp, HEAD_PAD), lambda h, i: (0, h)),
                  pl.BlockSpec((lp, V_DIM), lambda h, i: (0, h))],
        out_specs=[pl.BlockSpec((t, V_DIM), lambda h, i: (i, h)),
                   pl.BlockSpec((None, t, LANES), lambda h, i: (h, i, 0))],
        out_shape=[jax.ShapeDtypeStruct((lp, N_HEADS * V_DIM), BF16),
                   jax.ShapeDtypeStruct((N_HEADS, lp, LANES), F32)],
        scratch_shapes=[pltpu.VMEM((t, LANES), F32), pltpu.VMEM((t, LANES), F32), pltpu.VMEM((t, V_DIM), F32),
                        pltpu.VMEM((t, LANES), F32), pltpu.VMEM((t, t), F32), pltpu.VMEM((t, t), F32),
                        pltpu.VMEM((t, t), BF16)],
        compiler_params=_params(("parallel", "arbitrary")),
    )(q, k, v)


def _attn_bwd_dq(q, k, v, do, o, lse, dep, t, name):
    lp = q.shape[0]
    nt = lp // t

    def body(q_ref, k_ref, v_ref, do_ref, o_ref, lse_ref, dep_ref, dq_ref, l2row_ref, dlrow_ref,
             acc_s, l2_s, dl_s, sa_s, sb_s, da_s, db_s, ds_s):
        i = pl.program_id(1)
        delta = jnp.sum(do_ref[...].astype(F32) * o_ref[...].astype(F32), axis=1, keepdims=True)
        dl_s[...] = jnp.broadcast_to(delta, dl_s.shape)
        l2_s[...] = lse_ref[...] * LOG2E
        l2row_ref[...] = l2_s[...].T[0:SUBLANES, :]
        dlrow_ref[...] = dl_s[...].T[0:SUBLANES, :]
        acc_s[...] = jnp.zeros(acc_s.shape, F32)
        s_bufs, d_bufs = (sa_s, sb_s), (da_s, db_s)

        def dscores(s, dp, rows, width):
            p = jnp.exp2(s * SCALE_LOG2E - _lanes(l2_s[rows, :], width))
            return (p * (dp - _lanes(dl_s[rows, :], width))).astype(BF16)

        r1 = pl.multiple_of(jnp.minimum(i + 1, nt - 1) * t, t)
        rows = slice(t - SPILL, t)
        ks, vs = k_ref[pl.ds(r1, SPILL), :], v_ref[pl.ds(r1, SPILL), :]
        s = lax.dot_general(q_ref[rows, :], ks, NT, preferred_element_type=F32)
        seen = jnp.logical_and(_visible(i * t + t - SPILL, (i + 1) * t, SPILL, SPILL), i + 1 < nt)
        s = jnp.where(seen, s, NEG)
        dp = lax.dot_general(do_ref[rows, :], vs, NT, preferred_element_type=F32)
        acc_s[rows, :] = jnp.dot(dscores(s, dp, rows, SPILL), ks, preferred_element_type=F32)

        def scores(j, slot):
            r0 = pl.multiple_of(j * t, t)
            s_bufs[slot][...] = lax.dot_general(q_ref[...], k_ref[pl.ds(r0, t), :], NT,
                                                preferred_element_type=F32)
            d_bufs[slot][...] = lax.dot_general(do_ref[...], v_ref[pl.ds(r0, t), :], NT,
                                                preferred_element_type=F32)

        def absorb(j, slot, masked):
            for r in range(0, t, ROW_GROUP):
                rows = slice(r, r + ROW_GROUP)
                s = s_bufs[slot][rows, :]
                if masked:
                    s = jnp.where(_visible(i * t + r, i * t, ROW_GROUP, t), s, NEG)
                ds_s[rows, :] = dscores(s, d_bufs[slot][rows, :], rows, t)
            r0 = pl.multiple_of(j * t, t)
            acc_s[...] += jnp.dot(ds_s[...], k_ref[pl.ds(r0, t), :], preferred_element_type=F32)

        _pipelined_chunks(i, scores, absorb)
        dq_ref[...] = (acc_s[...] * ATTN_SCALE).astype(BF16)

    stat_row = pl.BlockSpec((None, None, SUBLANES, t), lambda h, i: (h, i, 0, 0))
    return pl.pallas_call(
        body, name=name, grid=(N_HEADS, nt),
        in_specs=[pl.BlockSpec((t, HEAD_PAD), lambda h, i: (i, h)),
                  pl.BlockSpec((lp, HEAD_PAD), lambda h, i: (0, h)),
                  pl.BlockSpec((lp, V_DIM), lambda h, i: (0, h)),
                  pl.BlockSpec((t, V_DIM), lambda h, i: (i, h)),
                  pl.BlockSpec((t, V_DIM), lambda h, i: (i, h)),
                  pl.BlockSpec((None, t, LANES), lambda h, i: (h, i, 0)),
                  pl.BlockSpec(dep.shape, lambda h, i: (0, 0))],
        out_specs=[pl.BlockSpec((t, HEAD_PAD), lambda h, i: (i, h)), stat_row, stat_row],
        out_shape=[jax.ShapeDtypeStruct((lp, N_HEADS * HEAD_PAD), BF16),
                   jax.ShapeDtypeStruct((N_HEADS, nt, SUBLANES, t), F32),
                   jax.ShapeDtypeStruct((N_HEADS, nt, SUBLANES, t), F32)],
        scratch_shapes=[pltpu.VMEM((t, HEAD_PAD), F32), pltpu.VMEM((t, LANES), F32), pltpu.VMEM((t, LANES), F32),
                        pltpu.VMEM((t, t), F32), pltpu.VMEM((t, t), F32), pltpu.VMEM((t, t), F32),
                        pltpu.VMEM((t, t), F32), pltpu.VMEM((t, t), BF16)],
        compiler_params=_params(("parallel", "arbitrary")),
    )(q, k, v, do, o, lse, dep)


def _attn_bwd_dkv(q, k, v, do, l2row, dlrow, t, name):
    lp = q.shape[0]
    nt = lp // t

    def body(q_ref, k_ref, v_ref, do_ref, l2_ref, dl_ref, dk_ref, dv_ref,
             dk_s, dv_s, sa_s, sb_s, da_s, db_s, p_s, ds_s):
        j = pl.program_id(1)
        dk_s[...] = jnp.zeros(dk_s.shape, F32)
        dv_s[...] = jnp.zeros(dv_s.shape, F32)
        s_bufs, d_bufs = (sa_s, sb_s), (da_s, db_s)

        def weights(st, dpt, l2r, dlr):
            pt = jnp.exp2(st * SCALE_LOG2E - l2r)
            return pt.astype(BF16), (pt * (dpt - dlr)).astype(BF16)

        prev = jnp.maximum(j - 1, 0)
        q0 = pl.multiple_of(prev * t + t - SPILL, SPILL)
        rows = slice(0, SPILL)
        qs, dos = q_ref[pl.ds(q0, SPILL), :], do_ref[pl.ds(q0, SPILL), :]
        st = lax.dot_general(k_ref[rows, :], qs, NT, preferred_element_type=F32)
        seen = jnp.logical_and(_visible_t(j * t, j * t - SPILL, SPILL, SPILL), j > 0)
        st = jnp.where(seen, st, NEG)
        dpt = lax.dot_general(v_ref[rows, :], dos, NT, preferred_element_type=F32)
        pt, dst = weights(st, dpt, l2_ref[prev, 0:1, t - SPILL:], dl_ref[prev, 0:1, t - SPILL:])
        dv_s[rows, :] = jnp.dot(pt, dos, preferred_element_type=F32)
        dk_s[rows, :] = jnp.dot(dst, qs, preferred_element_type=F32)

        def scores(c, slot):
            r0 = pl.multiple_of((nt - 1 - c) * t, t)
            s_bufs[slot][...] = lax.dot_general(k_ref[...], q_ref[pl.ds(r0, t), :], NT,
                                                preferred_element_type=F32)
            d_bufs[slot][...] = lax.dot_general(v_ref[...], do_ref[pl.ds(r0, t), :], NT,
                                                preferred_element_type=F32)

        def absorb(c, slot, masked):
            i = nt - 1 - c
            l2r, dlr = l2_ref[i, 0:1, :], dl_ref[i, 0:1, :]
            for r in range(0, t, ROW_GROUP):
                rows = slice(r, r + ROW_GROUP)
                st = s_bufs[slot][rows, :]
                if masked:
                    st = jnp.where(_visible_t(j * t + r, j * t, ROW_GROUP, t), st, NEG)
                p_s[rows, :], ds_s[rows, :] = weights(st, d_bufs[slot][rows, :], l2r, dlr)
            r0 = pl.multiple_of(i * t, t)
            dv_s[...] += jnp.dot(p_s[...], do_ref[pl.ds(r0, t), :], preferred_element_type=F32)
            dk_s[...] += jnp.dot(ds_s[...], q_ref[pl.ds(r0, t), :], preferred_element_type=F32)

        _pipelined_chunks(nt - 1 - j, scores, absorb)
        dk_ref[...] = (dk_s[...] * ATTN_SCALE).astype(BF16)
        dv_ref[...] = dv_s[...].astype(BF16)

    stat_rows = pl.BlockSpec((None, nt, SUBLANES, t), lambda h, j: (h, 0, 0, 0))
    return pl.pallas_call(
        body, name=name, grid=(N_HEADS, nt),
        in_specs=[pl.BlockSpec((lp, HEAD_PAD), lambda h, j: (0, h)),
                  pl.BlockSpec((t, HEAD_PAD), lambda h, j: (j, h)),
                  pl.BlockSpec((t, V_DIM), lambda h, j: (j, h)),
                  pl.BlockSpec((lp, V_DIM), lambda h, j: (0, h)),
                  stat_rows, stat_rows],
        out_specs=[pl.BlockSpec((t, HEAD_PAD), lambda h, j: (j, h)),
                   pl.BlockSpec((t, V_DIM), lambda h, j: (j, h))],
        out_shape=[jax.ShapeDtypeStruct((lp, N_HEADS * HEAD_PAD), BF16),
                   jax.ShapeDtypeStruct((lp, N_HEADS * V_DIM), BF16)],
        scratch_shapes=[pltpu.VMEM((t, HEAD_PAD), F32), pltpu.VMEM((t, V_DIM), F32),
                        pltpu.VMEM((t, t), F32), pltpu.VMEM((t, t), F32), pltpu.VMEM((t, t), F32),
                        pltpu.VMEM((t, t), F32), pltpu.VMEM((t, t), BF16), pltpu.VMEM((t, t), BF16)],
        compiler_params=_params(("parallel", "arbitrary")),
    )(q, k, v, do, l2row, dlrow)


def _shift_down(cur, prev8, k):
    r = pltpu.roll(cur, k, 0)
    row8 = lax.broadcasted_iota(jnp.int32, prev8.shape, 0)
    first = jnp.where(row8 < k, pltpu.roll(prev8, k, 0), r[0:SUBLANES])
    return jnp.concatenate([first, r[SUBLANES:]], axis=0)


def _shift_up(cur, next8, k):
    t = cur.shape[0]
    r = pltpu.roll(cur, t - k, 0)
    row8 = lax.broadcasted_iota(jnp.int32, next8.shape, 0)
    last = jnp.where(row8 >= SUBLANES - k, pltpu.roll(next8, SUBLANES - k, 0), r[t - SUBLANES:])
    return jnp.concatenate([r[:t - SUBLANES], last], axis=0)


def _scan_rows(a, b, edge, reverse):
    t, d = a.shape
    groups = t // SUBLANES
    a3, b3 = a.reshape(groups, SUBLANES, d), b.reshape(groups, SUBLANES, d)
    sub = lax.broadcasted_iota(jnp.int32, a3.shape, 1)
    s = 1
    while s < SUBLANES:
        keep = sub < SUBLANES - s if reverse else sub >= s
        shift = SUBLANES - s if reverse else s
        a_sh = jnp.where(keep, pltpu.roll(a3, shift, 1), 1.0)
        b_sh = jnp.where(keep, pltpu.roll(b3, shift, 1), 0.0)
        b3 = a3 * b_sh + b3
        a3 = a3 * a_sh
        s *= 2
    out = [None] * groups
    for k in (range(groups - 1, -1, -1) if reverse else range(groups)):
        out[k] = b3[k] + a3[k] * edge
        edge = out[k][0:1, :] if reverse else out[k][SUBLANES - 1:SUBLANES, :]
    return jnp.concatenate(out, axis=0)


def _sqrt_one_minus_exp2x(x):
    th = jnp.tanh(x)
    m2 = (-2.0 * th) / (1.0 - th)
    return m2 * lax.rsqrt(jnp.maximum(m2, TINY))


def _log_sigmoid(x):
    return jnp.minimum(x, 0.0) - jnp.log(1.0 + jnp.exp(-jnp.abs(x)))


GELU_C = math.sqrt(2.0 / math.pi)
GELU_K = 0.044715


def _gelu(x):
    th = jnp.tanh(GELU_C * (x + GELU_K * x * x * x))
    return 0.5 * x * (1.0 + th), th


def _block_mm(xb, w_ref, dims):
    rb = D_RNN // RNN_BLOCKS
    return jnp.concatenate(
        [lax.dot_general(xb[:, h * rb:(h + 1) * rb], w_ref[h], dims, preferred_element_type=F32)
         for h in range(RNN_BLOCKS)], axis=1)


def _rglru_gates(ux, prev8, pv_ref, wa_ref, wi_ref):
    shifted = [ux] + [_shift_down(ux, prev8, k) for k in range(1, CONV_WIDTH)]
    xc = pv_ref[4:5, :] + pv_ref[3:4, :] * ux
    for k in range(1, CONV_WIDTH):
        xc = xc + pv_ref[3 - k:4 - k, :] * shifted[k]
    xcb = xc.astype(BF16)
    r_g = _sig(_block_mm(xcb, wa_ref, NN) + pv_ref[5:6, :])
    i_g = _sig(_block_mm(xcb, wi_ref, NN) + pv_ref[6:7, :])
    log_a = LRU_C * r_g * _log_sigmoid(pv_ref[7:8, :])
    a = jnp.exp(log_a)
    mm = _sqrt_one_minus_exp2x(log_a)
    return dict(shifted=shifted, xc=xc, xcb=xcb, r=r_g, i=i_g, a=a, mm=mm)


def _rglru_fwd(ux, ug, pv, wa, wi, t, name):
    lp, d = ux.shape

    def body(ux_ref, ug_ref, pv_ref, wa_ref, wi_ref, y_ref, h_ref, tail_s, hc_s):
        @pl.when(pl.program_id(0) == 0)
        def _():
            tail_s[...] = jnp.zeros_like(tail_s)
            hc_s[...] = jnp.zeros_like(hc_s)

        uxv = ux_ref[...]
        gt = _rglru_gates(uxv, tail_s[...], pv_ref, wa_ref, wi_ref)
        tail_s[...] = ux_ref[t - SUBLANES:t, :]
        h_ref[...] = _scan_rows(gt["a"], gt["mm"] * (gt["i"] * gt["xc"]), hc_s[0:1, :], False)
        hc_s[...] = h_ref[t - SUBLANES:t, :]
        hc_s[0:1, :] = h_ref[t - 1:t, :]
        y_ref[...] = (h_ref[...] * _gelu(ug_ref[...])[0]).astype(BF16)

    tile = pl.BlockSpec((t, d), lambda i: (i, 0))
    return pl.pallas_call(
        body, name=name, grid=(lp // t,),
        in_specs=[tile, tile, pl.BlockSpec(pv.shape, lambda i: (0, 0)),
                  pl.BlockSpec(wa.shape, lambda i: (0, 0, 0)), pl.BlockSpec(wi.shape, lambda i: (0, 0, 0))],
        out_specs=[tile, tile],
        out_shape=[jax.ShapeDtypeStruct((lp, d), BF16), jax.ShapeDtypeStruct((lp, d), F32)],
        scratch_shapes=[pltpu.VMEM((SUBLANES, d), F32), pltpu.VMEM((SUBLANES, d), F32)],
        compiler_params=_params(("arbitrary",)),
    )(ux, ug, pv, wa, wi)


def _rglru_bwd(ux, ug, hs, dy, pv, wa, wi, dep, t, name):
    lp, d = ux.shape
    nt = lp // t
    per = t // SUBLANES
    rb = d // RNN_BLOCKS

    def body(ux_ref, uxp_ref, ug_ref, h_ref, hp_ref, dy_ref, pv_ref, wa_ref, wi_ref, dep_ref,
             dux_ref, dug_ref, dpv_ref, dwa_ref, dwi_ref, ca_s, cg_s, cx_s):
        step = pl.program_id(0)
        first_tile = step == nt - 1

        @pl.when(step == 0)
        def _():
            for ref in (ca_s, cg_s, cx_s, dpv_ref, dwa_ref, dwi_ref):
                ref[...] = jnp.zeros_like(ref)

        uxv = ux_ref[...]
        prev8 = jnp.where(first_tile, 0.0, uxp_ref[...])
        hprev8 = jnp.where(first_tile, 0.0, hp_ref[...])
        gt = _rglru_gates(uxv, prev8, pv_ref, wa_ref, wi_ref)
        a, mm, r_g, i_g, xc = gt["a"], gt["mm"], gt["r"], gt["i"], gt["xc"]
        hv = h_ref[...]
        hprev = _shift_down(hv, hprev8, 1)
        ugv, dyv = ug_ref[...], dy_ref[...]
        gel, th = _gelu(ugv)
        dgel = 0.5 * (1.0 + th) + 0.5 * ugv * (1.0 - th * th) * (GELU_C * (1.0 + 3.0 * GELU_K * ugv * ugv))
        dug_ref[...] = (dyv * hv * dgel).astype(BF16)
        a_up = _shift_up(a, ca_s[...], 1)
        gv = _scan_rows(a_up, dyv * gel, cg_s[0:1, :], True)
        ca_s[...] = a[0:SUBLANES]
        cg_s[...] = gv[0:SUBLANES]
        ixc = i_g * xc
        d_ixc = gv * mm
        d_log_a = gv * hprev * a - (gv * ixc) * (a * a) / mm
        logsig = _log_sigmoid(pv_ref[7:8, :])
        d_pre_a = d_log_a * (LRU_C * logsig) * r_g * (1.0 - r_g)
        d_pre_i = d_ixc * xc * i_g * (1.0 - i_g)
        dab, dib = d_pre_a.astype(BF16), d_pre_i.astype(BF16)
        d_xc = d_ixc * i_g + _block_mm(dab, wa_ref, NT) + _block_mm(dib, wi_ref, NT)
        xcb = gt["xcb"]
        for h in range(RNN_BLOCKS):
            cols = slice(h * rb, (h + 1) * rb)
            dwa_ref[h] += lax.dot_general(xcb[:, cols], dab[:, cols], TN, preferred_element_type=F32)
            dwi_ref[h] += lax.dot_general(xcb[:, cols], dib[:, cols], TN, preferred_element_type=F32)
        csum = lambda v: jnp.sum(v, axis=0, keepdims=True)
        for k in range(CONV_WIDTH):
            dpv_ref[3 - k:4 - k, :] += csum(d_xc * gt["shifted"][k])
        dpv_ref[4:5, :] += csum(d_xc)
        dpv_ref[5:6, :] += csum(d_pre_a)
        dpv_ref[6:7, :] += csum(d_pre_i)
        dpv_ref[7:8, :] += csum(d_log_a * (LRU_C * r_g)) * _sig(-pv_ref[7:8, :])
        dux = pv_ref[3:4, :] * d_xc
        for k in range(1, CONV_WIDTH):
            dux = dux + pv_ref[3 - k:4 - k, :] * _shift_up(d_xc, cx_s[...], k)
        cx_s[...] = d_xc[0:SUBLANES]
        dux_ref[...] = dux.astype(BF16)

    rev = lambda i: (nt - 1 - i, 0)
    before = lambda i: (jnp.maximum((nt - 1 - i) * per - 1, 0), 0)
    tile = pl.BlockSpec((t, d), rev)
    tail = pl.BlockSpec((SUBLANES, d), before)
    fixed2 = lambda arr: pl.BlockSpec(arr.shape, lambda i: (0, 0))
    fixed3 = lambda arr: pl.BlockSpec(arr.shape, lambda i: (0, 0, 0))
    return pl.pallas_call(
        body, name=name, grid=(nt,),
        in_specs=[tile, tail, tile, tile, tail, tile, fixed2(pv), fixed3(wa), fixed3(wi), fixed2(dep)],
        out_specs=[tile, tile, fixed2(pv), fixed3(wa), fixed3(wi)],
        out_shape=[jax.ShapeDtypeStruct((lp, d), BF16), jax.ShapeDtypeStruct((lp, d), BF16),
                   jax.ShapeDtypeStruct(pv.shape, F32), jax.ShapeDtypeStruct(wa.shape, F32),
                   jax.ShapeDtypeStruct(wi.shape, F32)],
        scratch_shapes=[pltpu.VMEM((SUBLANES, d), F32)] * 3,
        compiler_params=_params(("arbitrary",)),
    )(ux, ux, ug, hs, hs, dy, pv, wa, wi, dep)


def _adamw(w, m, v, parts, dep, name):
    rows, cols = w.shape
    tr = _pick(rows, 256, SUBLANES)
    c1 = 1.0 / (1.0 - ADAM_B1 ** ADAM_STEP)
    c2 = 1.0 / (1.0 - ADAM_B2 ** ADAM_STEP)

    def body(w_ref, m_ref, v_ref, p_ref, dep_ref, g_ref, d_ref, mo_ref, vo_ref):
        g = p_ref[0].astype(F32)
        for q in range(1, N_DEV):
            g = g + p_ref[q].astype(F32)
        mn = ADAM_B1 * m_ref[...] + (1.0 - ADAM_B1) * g
        vn = ADAM_B2 * v_ref[...] + (1.0 - ADAM_B2) * (g * g)
        g_ref[...] = g
        mo_ref[...] = mn
        vo_ref[...] = vn
        d_ref[...] = -ADAM_LR * ((mn * c1) / (jnp.sqrt(vn * c2) + ADAM_EPS) + ADAM_WD * w_ref[...])

    blk = pl.BlockSpec((tr, cols), lambda i: (i, 0))
    return pl.pallas_call(
        body, name=name, grid=(rows // tr,),
        in_specs=[blk, blk, blk, pl.BlockSpec((N_DEV, tr, cols), lambda i: (0, i, 0)),
                  pl.BlockSpec(dep.shape, lambda i: (0, 0))],
        out_specs=[blk] * 4, out_shape=[jax.ShapeDtypeStruct((rows, cols), F32)] * 4,
        compiler_params=_params(("parallel",)),
    )(w, m, v, parts, dep)


WEIGHTS = ("meta_tokens", "norm_mix_g", "w_in", "b_gate", "conv_w", "conv_b", "w_rec_a", "b_rec_a", "w_rec_i",
           "b_rec_i", "lru_lambda", "q_norm_g", "w_uq", "kv_norm_g", "w_ukv", "w_branch", "w_out", "norm_ffn_g",
           "w_ffn_in", "w_ffn_out", "final_norm_g")
SHARDED = {"meta_tokens": True, "w_in": True, "b_gate": True, "conv_w": True, "w_uq": True, "w_ukv": True,
           "w_branch": False, "w_out": False, "w_ffn_in": True, "w_ffn_out": False}
REPLICATED = tuple(n for n in WEIGHTS if n not in SHARDED)


def _as2d(a):
    return a.reshape(-1, a.shape[-1])


def _full_from_gathered(g, by_cols):
    if by_cols:
        return jnp.transpose(g, (1, 0, 2)).reshape(g.shape[1], N_DEV * g.shape[2])
    return g.reshape(N_DEV * g.shape[1], g.shape[2])


def _blocks_from_full(full, by_cols):
    if by_cols:
        r, c = full.shape
        return jnp.transpose(full.reshape(r, N_DEV, c // N_DEV), (1, 0, 2))
    return full.reshape(N_DEV, full.shape[0] // N_DEV, full.shape[1])


def kernel(x, meta_tokens, norm_mix_g, w_in, b_gate, conv_w, conv_b, w_rec_a, b_rec_a, w_rec_i, b_rec_i, lru_lambda, q_norm_g, w_uq, kv_norm_g, w_ukv, w_branch, w_out, norm_ffn_g, w_ffn_in, w_ffn_out, final_norm_g, loss_target, m_meta_tokens, m_norm_mix_g, m_w_in, m_b_gate, m_conv_w, m_conv_b, m_w_rec_a, m_b_rec_a, m_w_rec_i, m_b_rec_i, m_lru_lambda, m_q_norm_g, m_w_uq, m_kv_norm_g, m_w_ukv, m_w_branch, m_w_out, m_norm_ffn_g, m_w_ffn_in, m_w_ffn_out, m_final_norm_g, v_meta_tokens, v_norm_mix_g, v_w_in, v_b_gate, v_conv_w, v_conv_b, v_w_rec_a, v_b_rec_a, v_w_rec_i, v_b_rec_i, v_lru_lambda, v_q_norm_g, v_w_uq, v_kv_norm_g, v_w_ukv, v_w_branch, v_w_out, v_norm_ffn_g, v_w_ffn_in, v_w_ffn_out, v_final_norm_g):
    w = dict(meta_tokens=meta_tokens, norm_mix_g=norm_mix_g, w_in=w_in, b_gate=b_gate, conv_w=conv_w, conv_b=conv_b,
             w_rec_a=w_rec_a, b_rec_a=b_rec_a, w_rec_i=w_rec_i, b_rec_i=b_rec_i, lru_lambda=lru_lambda,
             q_norm_g=q_norm_g, w_uq=w_uq, kv_norm_g=kv_norm_g, w_ukv=w_ukv, w_branch=w_branch, w_out=w_out,
             norm_ffn_g=norm_ffn_g, w_ffn_in=w_ffn_in, w_ffn_out=w_ffn_out, final_norm_g=final_norm_g)
    m = dict(meta_tokens=m_meta_tokens, norm_mix_g=m_norm_mix_g, w_in=m_w_in, b_gate=m_b_gate, conv_w=m_conv_w,
             conv_b=m_conv_b, w_rec_a=m_w_rec_a, b_rec_a=m_b_rec_a, w_rec_i=m_w_rec_i, b_rec_i=m_b_rec_i,
             lru_lambda=m_lru_lambda, q_norm_g=m_q_norm_g, w_uq=m_w_uq, kv_norm_g=m_kv_norm_g, w_ukv=m_w_ukv,
             w_branch=m_w_branch, w_out=m_w_out, norm_ffn_g=m_norm_ffn_g, w_ffn_in=m_w_ffn_in,
             w_ffn_out=m_w_ffn_out, final_norm_g=m_final_norm_g)
    v = dict(meta_tokens=v_meta_tokens, norm_mix_g=v_norm_mix_g, w_in=v_w_in, b_gate=v_b_gate, conv_w=v_conv_w,
             conv_b=v_conv_b, w_rec_a=v_w_rec_a, b_rec_a=v_b_rec_a, w_rec_i=v_w_rec_i, b_rec_i=v_b_rec_i,
             lru_lambda=v_lru_lambda, q_norm_g=v_q_norm_g, w_uq=v_w_uq, kv_norm_g=v_kv_norm_g, w_ukv=v_w_ukv,
             w_branch=v_w_branch, w_out=v_w_out, norm_ffn_g=v_norm_ffn_g, w_ffn_in=v_w_ffn_in,
             w_ffn_out=v_w_ffn_out, final_norm_g=v_final_norm_g)

    seq, d_model = x.shape[1], x.shape[2]
    length = N_META + seq
    lp = -(-length // LANES) * LANES
    t_attn = _pick(lp, 640)
    t_rnn = LANES

    small = ("meta_tokens", "b_gate", "conv_w")
    names = list(SHARDED)
    first, mid, late = ("meta_tokens", "b_gate", "conv_w", "w_in"), ("w_uq", "w_ukv", "w_branch", "w_out"), (
        "w_ffn_in", "w_ffn_out")
    payload = lambda n: _as2d(w[n]) if n in small else _as2d(w[n]).astype(BF16)
    got = _exchange([payload(n) for n in small], [True] * len(small), "gather_small")
    in_h = _exchange_start([payload("w_in")], [True], got[0], "gather_in_start")
    mid_h = _exchange_start([payload(n) for n in mid], [True] * len(mid), in_h["token"], "gather_mid_start")
    late_h = _exchange_start([payload(n) for n in late], [True] * len(late), mid_h["token"], "gather_late_start")
    full = {n: _full_from_gathered(g, SHARDED[n]) for n, g in zip(small, got)}
    h0 = jnp.concatenate([full["meta_tokens"], x[0], jnp.zeros((lp - length, d_model), F32)], axis=0)
    z = _rmsnorm_fwd(h0, norm_mix_g, "norm_mix")
    full["w_in"] = _full_from_gathered(_exchange_wait(in_h, z, "gather_in_wait")[0], True)

    splits = (D_RNN, D_RNN, Q_RANK, KV_RANK, QK_ROPE, 2 * d_model)
    offs = [0]
    for s in splits:
        offs.append(offs[-1] + s)
    w_x, w_g, w_q, w_kv, w_kr, w_m = (full["w_in"][:, offs[s]:offs[s + 1]] for s in range(6))
    w_kr = jnp.pad(w_kr, ((0, 0), (0, LANES - QK_ROPE)))
    bg = full["b_gate"].reshape(1, 2 * d_model)
    pv = jnp.concatenate([full["conv_w"], conv_b, b_rec_a, b_rec_i, lru_lambda], axis=0)
    wa_b, wi_b = w_rec_a[0].astype(BF16), w_rec_i[0].astype(BF16)
    g_final = final_norm_g.reshape(1, d_model)

    tgt = jnp.pad(loss_target[0], ((N_META, lp - length), (0, 0)))
    tabs = _rope_tables(lp)

    ux = _mm([(z, w_x)], "nn", "in_x")
    ug = _mm([(z, w_g)], "nn", "in_g")
    uq = _mm([(z, w_q)], "nn", "in_q")
    ukv = _mm([(z, w_kv)], "nn", "in_kv")
    ukr = _mm([(z, w_kr)], "nn", "in_kr")
    um = _mm([(z, w_m)], "nn", "in_m", out_dtype=BF16)
    for n, g in zip(mid, _exchange_wait(mid_h, um, "gather_mid_wait")):
        full[n] = _full_from_gathered(g, SHARDED[n])
    w_uq_pad = jnp.pad(full["w_uq"].reshape(Q_RANK, N_HEADS, QK_NOPE + QK_ROPE),
                       ((0, 0), (0, 0), (0, HEAD_PAD - QK_NOPE - QK_ROPE))).reshape(Q_RANK, N_HEADS * HEAD_PAD)
    w_ukv3 = full["w_ukv"].reshape(KV_RANK, N_HEADS, QK_NOPE + V_DIM)
    w_k_pad = jnp.pad(w_ukv3[:, :, :QK_NOPE], ((0, 0), (0, 0), (0, HEAD_PAD - QK_NOPE))).reshape(
        KV_RANK, N_HEADS * HEAD_PAD)
    w_v = w_ukv3[:, :, QK_NOPE:].reshape(KV_RANK, N_HEADS * V_DIM)
    wb_r, wb_a = full["w_branch"][:D_RNN], full["w_branch"][D_RNN:]
    y_rnn, hs = _rglru_fwd(ux, ug, pv, wa_b, wi_b, t_rnn, "rglru_fwd")
    qn = _rmsnorm_fwd(uq, q_norm_g, "norm_q")
    kvn = _rmsnorm_fwd(ukv, kv_norm_g, "norm_kv")
    qpad = _mm([(qn, w_uq_pad)], "nn", "up_q", out_dtype=BF16)
    kpad = _mm([(kvn, w_k_pad)], "nn", "up_k", out_dtype=BF16)
    vh = _mm([(kvn, w_v)], "nn", "up_v", out_dtype=BF16)
    qh, kh = _rope_fwd(qpad, kpad, ukr, tabs, "rope_fwd")
    oh, lse = _attn_fwd(qh, kh, vh, t_attn, "attn_fwd")
    p_rnn = _mm([(y_rnn, wb_r)], "nn", "branch_rnn", out_dtype=BF16)
    p_att = _mm([(oh, wb_a)], "nn", "branch_att", out_dtype=BF16)
    mixed = _mix_fwd(um, p_rnn, p_att, bg, "mix_fwd")
    h1 = _mm([(mixed, full["w_out"])], "nn", "out_proj", res=h0)
    for n, g in zip(late, _exchange_wait(late_h, h1, "gather_late_wait")):
        full[n] = _full_from_gathered(g, SHARDED[n])
    zf = _rmsnorm_fwd(h1, norm_ffn_g, "norm_ffn")
    gu = _mm([(zf, full["w_ffn_in"])], "nn", "ffn_in", out_dtype=BF16)
    act = _swiglu_fwd(gu, "swiglu_fwd")
    h2 = _mm([(act, full["w_ffn_out"])], "nn", "ffn_out", res=h1)
    dh2, dg_final, _, loss_part = _loss_bwd(h2, tgt, g_final, seq, "loss_bwd")

    d_act = _mm([(dh2, full["w_ffn_out"])], "nt", "d_act", out_dtype=BF16)
    dw_ffn_out = _mm_tn(act, dh2, "dw_ffn_out")
    d_gu = _swiglu_bwd(gu, d_act, "swiglu_bwd")
    dw_ffn_in = _mm_tn(zf, d_gu, "dw_ffn_in")
    blocks = lambda n, g: _blocks_from_full(g, SHARDED[n]).astype(F32 if n in small else BF16)
    sent = {("w_ffn_in", "w_ffn_out"): _exchange_start(
        [blocks("w_ffn_in", dw_ffn_in), blocks("w_ffn_out", dw_ffn_out)], [False] * 2, dg_final, "scatter_ffn_start")}
    d_zf = _mm([(d_gu, full["w_ffn_in"], D_FF, 0), (d_gu, full["w_ffn_in"], D_FF, 1)], "nt", "d_zf")
    dh1, dg_ffn = _rmsnorm_bwd(h1, d_zf, norm_ffn_g, "norm_ffn_bwd", res=dh2,
                               dep=sent[("w_ffn_in", "w_ffn_out")]["token"])
    d_mixed = _mm([(dh1, full["w_out"])], "nt", "d_mixed", out_dtype=BF16)
    dw_out = _mm_tn(mixed, dh1, "dw_out")
    d_prnn, d_patt, d_um, dbg = _mix_bwd(um, p_rnn, p_att, d_mixed, bg, "mix_bwd")
    d_yrnn = _mm([(d_prnn, wb_r)], "nt", "d_yrnn")
    d_oh = _mm([(d_patt, wb_a)], "nt", "d_oh", out_dtype=BF16)
    dwb_r = _mm_tn(y_rnn, d_prnn, "dw_branch_rnn")
    dwb_a = _mm_tn(oh, d_patt, "dw_branch_att")
    sent[("w_out", "w_branch")] = _exchange_start(
        [blocks("w_out", dw_out), blocks("w_branch", jnp.concatenate([dwb_r, dwb_a], axis=0))], [False] * 2,
        dg_ffn, "scatter_mix_start")
    dqh, l2row, dlrow = _attn_bwd_dq(qh, kh, vh, d_oh, oh, lse, sent[("w_out", "w_branch")]["token"], t_attn,
                                     "attn_bwd_dq")
    dkh, dvh = _attn_bwd_dkv(qh, kh, vh, d_oh, l2row, dlrow, t_attn, "attn_bwd_dkv")
    dqpad, dkpad, dukr = _rope_bwd(dqh, dkh, tabs, "rope_bwd")
    d_qn = _mm([(dqpad, w_uq_pad)], "nt", "d_qn")
    dw_uq_pad = _mm_tn(qn, dqpad, "dw_uq")
    d_kvn = _mm([(dkpad, w_k_pad), (dvh, w_v)], "nt", "d_kvn")
    dw_k_pad = _mm_tn(kvn, dkpad, "dw_uk")
    dw_v = _mm_tn(kvn, dvh, "dw_uv")
    dw_uq = dw_uq_pad.reshape(Q_RANK, N_HEADS, HEAD_PAD)[:, :, :QK_NOPE + QK_ROPE].reshape(Q_RANK, -1)
    dw_ukv = jnp.concatenate([dw_k_pad.reshape(KV_RANK, N_HEADS, HEAD_PAD)[:, :, :QK_NOPE],
                              dw_v.reshape(KV_RANK, N_HEADS, V_DIM)], axis=2).reshape(KV_RANK, -1)
    sent[("w_uq", "w_ukv")] = _exchange_start([blocks("w_uq", dw_uq), blocks("w_ukv", dw_ukv)], [False] * 2,
                                              dbg, "scatter_attn_start")
    duq, dg_q = _rmsnorm_bwd(uq, d_qn, q_norm_g, "norm_q_bwd", out_dtype=BF16)
    dukv, dg_kv = _rmsnorm_bwd(ukv, d_kvn, kv_norm_g, "norm_kv_bwd", out_dtype=BF16)
    dux, dug, dpv, dwa, dwi = _rglru_bwd(ux, ug, hs, d_yrnn, pv, wa_b, wi_b, sent[("w_uq", "w_ukv")]["token"],
                                         t_rnn, "rglru_bwd")
    d_z = _mm([(dux, w_x), (dug, w_g), (duq, w_q), (dukv, w_kv), (dukr, w_kr), (d_um, w_m)], "nt", "d_z")
    dw_in = jnp.concatenate([
        _mm_tn(z, dux, "dw_in_x"), _mm_tn(z, dug, "dw_in_g"), _mm_tn(z, duq, "dw_in_q"),
        _mm_tn(z, dukv, "dw_in_kv"), _mm_tn(z, dukr, "dw_in_kr")[:, :QK_ROPE], _mm_tn(z, d_um, "dw_in_m")], axis=1)
    dh0, dg_mix = _rmsnorm_bwd(h0, d_z, norm_mix_g, "norm_mix_bwd", res=dh1)

    grad_last = dict(meta_tokens=dh0[:N_META], b_gate=dbg.reshape(2, d_model), conv_w=dpv[:CONV_WIDTH], w_in=dw_in)
    grad_rep = dict(
        norm_mix_g=dg_mix, conv_b=dpv[4:5], w_rec_a=dwa, b_rec_a=dpv[5:6], w_rec_i=dwi, b_rec_i=dpv[6:7],
        lru_lambda=dpv[7:8], q_norm_g=dg_q, kv_norm_g=dg_kv, norm_ffn_g=dg_ffn, final_norm_g=dg_final)

    rep_payload = [_as2d(grad_rep[n]).astype(BF16 if n in ("w_rec_a", "w_rec_i") else F32) for n in REPLICATED]
    last_h = _exchange_start([blocks(n, grad_last[n]) for n in first] + rep_payload,
                             [False] * len(first) + [True] * len(REPLICATED), dg_mix, "scatter_last_start")
    grads, deltas, new_m, new_v = {}, {}, {}, {}

    def update(n, parts, dep):
        g2, d2, m2, v2 = _adamw(_as2d(w[n]), _as2d(m[n]), _as2d(v[n]), parts, dep, "adamw_" + n)
        for store, val in ((grads, g2), (deltas, d2), (new_m, m2), (new_v, v2)):
            store[n] = val.reshape(w[n].shape)

    chain = last_h["token"]
    for group, handle in sent.items():
        for n, parts in zip(group, _exchange_wait(handle, chain, "scatter_wait_" + group[0])):
            update(n, parts, chain)
            chain = _as2d(deltas[n])[:SUBLANES, :LANES]
    for n, parts in zip(first + REPLICATED, _exchange_wait(last_h, chain, "scatter_last_wait")):
        update(n, parts, last_h["token"])

    loss = lax.psum(loss_part[0, 0], MESH_AXES)
    grad_x = dh0[N_META:length][None]
    return (loss, grad_x, *[grads[n] for n in WEIGHTS], *[deltas[n] for n in WEIGHTS],
            *[new_m[n] for n in WEIGHTS], *[new_v[n] for n in WEIGHTS])
```

```python
import functools
import math

import jax
import jax.numpy as jnp
from jax import lax
from jax.experimental import pallas as pl
from jax.experimental.pallas import tpu as pltpu

F32 = jnp.float32
BF16 = jnp.bfloat16

N_DEV = 8
MESH_AXES = ("x", "y", "c")
LANES = 128
SUBLANES = 8
VMEM_LIMIT = 56 * 1024 * 1024

N_META = 16
CHUNK_SHIFT = 6
CHUNK_BIAS = 64 - N_META
EPS = 1e-6
D_RNN = 1280
RNN_BLOCKS = 10
CONV_WIDTH = 4
LRU_C = 8.0
N_HEADS = 8
QK_NOPE = 128
QK_ROPE = 64
V_DIM = 128
HEAD_PAD = 256
Q_RANK = 384
KV_RANK = 256
ROPE_THETA = 10000.0
ATTN_SCALE = 1.0 / math.sqrt(QK_NOPE + QK_ROPE)
NEG = -1e30
TINY = 1e-30
LOG2E = 1.0 / math.log(2.0)
SCALE_LOG2E = ATTN_SCALE * LOG2E
Q_SPLIT = 2
Q_ALIGN = LANES // Q_SPLIT
ROW_GROUP = 32
SPILL = LANES
D_FF = 2816

ADAM_LR = 0.001
ADAM_B1 = 0.9
ADAM_B2 = 0.999
ADAM_EPS = 1e-08
ADAM_WD = 0.01
ADAM_STEP = 10

NN = (((1,), (0,)), ((), ()))
NT = (((1,), (1,)), ((), ()))
TN = (((0,), (0,)), ((), ()))


def _pick(n, cap, base=LANES):
    best = None
    for t in range(base, min(n, cap) + 1, base):
        if n % t == 0:
            best = t
    return best if best is not None else n


def _params(sem=None):
    return pltpu.CompilerParams(dimension_semantics=sem, vmem_limit_bytes=VMEM_LIMIT)


def _sig(x):
    return 0.5 + 0.5 * jnp.tanh(0.5 * x)


def _exchange(srcs, gather, name):
    n = len(srcs)
    out_shape = [jax.ShapeDtypeStruct((N_DEV,) + (s.shape if g else s.shape[1:]), s.dtype)
                 for s, g in zip(srcs, gather)]

    def body(*refs):
        src, dst = refs[:n], refs[n:2 * n]
        send_sems, recv_sems, local_sems = refs[2 * n:]
        x, y, c = lax.axis_index("x"), lax.axis_index("y"), lax.axis_index("c")
        me = 4 * x + 2 * y + c
        local = []
        for t in range(n):
            cp = pltpu.make_async_copy(src[t] if gather[t] else src[t].at[me], dst[t].at[me], local_sems.at[t])
            cp.start()
            local.append(cp)
        sends, recvs = [], []
        for k in range(1, N_DEV):
            px = 1 - x if k & 4 else x
            py = 1 - y if k & 2 else y
            pc = 1 - c if k & 1 else c
            peer = 4 * px + 2 * py + pc
            for t in range(n):
                cp = pltpu.make_async_remote_copy(
                    src_ref=src[t] if gather[t] else src[t].at[peer], dst_ref=dst[t].at[me],
                    send_sem=send_sems.at[t, k - 1], recv_sem=recv_sems.at[t, k - 1],
                    device_id=(px, py, pc), device_id_type=pl.DeviceIdType.MESH)
                cp.start()
                sends.append(cp)
                recvs.append(pltpu.make_async_remote_copy(
                    src_ref=src[t] if gather[t] else src[t].at[peer], dst_ref=dst[t].at[peer],
                    send_sem=send_sems.at[t, k - 1], recv_sem=recv_sems.at[t, k - 1],
                    device_id=(px, py, pc), device_id_type=pl.DeviceIdType.MESH))
        for cp in recvs:
            cp.wait_recv()
        for cp in sends:
            cp.wait_send()
        for cp in local:
            cp.wait()

    any_spec = pl.BlockSpec(memory_space=pl.ANY)
    return pl.pallas_call(
        body, name=name, out_shape=out_shape,
        in_specs=[any_spec] * n, out_specs=[any_spec] * n,
        scratch_shapes=[pltpu.SemaphoreType.DMA((n, N_DEV - 1)), pltpu.SemaphoreType.DMA((n, N_DEV - 1)),
                        pltpu.SemaphoreType.DMA((n,))],
    )(*srcs)


HBM_SPEC = pl.BlockSpec(memory_space=pltpu.HBM)
SEM_SPEC = pl.BlockSpec(memory_space=pltpu.SEMAPHORE)
DATAFLOW = pltpu.SideEffectType.DATAFLOW_SIDE_EFFECTING


def _peers(x, y, c):
    out = []
    for k in range(1, N_DEV):
        px = 1 - x if k & 4 else x
        py = 1 - y if k & 2 else y
        pc = 1 - c if k & 1 else c
        out.append((k, (px, py, pc), 4 * px + 2 * py + pc))
    return out


def _split_copies(src, land, gather, send_sems, recv_sems, local_sems):
    x, y, c = lax.axis_index("x"), lax.axis_index("y"), lax.axis_index("c")
    me = 4 * x + 2 * y + c
    n = len(src)
    local = [pltpu.make_async_copy(src[t] if gather[t] else src[t].at[me], land[t].at[me], local_sems.at[t])
             for t in range(n)]
    sends, recvs = [], []
    for k, pos, peer in _peers(x, y, c):
        for t in range(n):
            mine = src[t] if gather[t] else src[t].at[peer]
            slot = t * (N_DEV - 1) + k - 1
            common = dict(send_sem=send_sems.at[slot], recv_sem=recv_sems.at[slot], device_id=pos,
                          device_id_type=pl.DeviceIdType.MESH)
            sends.append(pltpu.make_async_remote_copy(src_ref=mine, dst_ref=land[t].at[me], **common))
            recvs.append(pltpu.make_async_remote_copy(src_ref=mine, dst_ref=land[t].at[peer], **common))
    return local, sends, recvs


def _exchange_start(srcs, gather, after, name):
    n = len(srcs)
    lands = [lax.empty((N_DEV,) + (s.shape if g else s.shape[1:]), s.dtype) for s, g in zip(srcs, gather)]

    def body(*refs):
        src, land = refs[:n], refs[n:2 * n]
        send_sems, recv_sems, local_sems = refs[2 * n + 1:2 * n + 4]
        local, sends, _ = _split_copies(src, land, gather, send_sems, recv_sems, local_sems)
        for cp in local + sends:
            cp.start()
        refs[-1][...] = jnp.zeros_like(refs[-1])

    hbm = lambda a: pltpu.HBM(a.shape, a.dtype)
    outs = pl.pallas_call(
        body, name=name,
        out_shape=(pltpu.SemaphoreType.DMA((n * (N_DEV - 1),)), pltpu.SemaphoreType.DMA((n * (N_DEV - 1),)),
                   pltpu.SemaphoreType.DMA((n,)), *[hbm(s) for s in srcs], *[hbm(a) for a in lands],
                   jax.ShapeDtypeStruct((SUBLANES, LANES), F32)),
        in_specs=[HBM_SPEC] * (2 * n) + [pl.BlockSpec(memory_space=pl.ANY)],
        out_specs=(SEM_SPEC, SEM_SPEC, SEM_SPEC, *[HBM_SPEC] * (2 * n), pl.BlockSpec(memory_space=pltpu.VMEM)),
        input_output_aliases={t: 3 + t for t in range(2 * n)},
        compiler_params=pltpu.CompilerParams(has_side_effects=DATAFLOW),
    )(*[pltpu.with_memory_space_constraint(a, pltpu.HBM) for a in list(srcs) + lands], after)
    return dict(sems=outs[:3], srcs=outs[3:3 + n], lands=outs[3 + n:3 + 2 * n], token=outs[-1], gather=gather)


def _exchange_wait(handle, after, name):
    srcs, lands, gather = handle["srcs"], handle["lands"], handle["gather"]
    n = len(srcs)

    def body(*refs):
        src, land = refs[:n], refs[n:2 * n]
        send_sems, recv_sems, local_sems = refs[2 * n:2 * n + 3]
        local, sends, recvs = _split_copies(src, land, gather, send_sems, recv_sems, local_sems)
        for cp in sends:
            cp.wait_send()
        for cp in recvs:
            cp.wait_recv()
        for cp in local:
            cp.wait()

    hbm = lambda a: pltpu.HBM(a.shape, a.dtype)
    outs = pl.pallas_call(
        body, name=name, out_shape=(*[hbm(s) for s in srcs], *[hbm(a) for a in lands]),
        in_specs=[HBM_SPEC] * (2 * n) + [SEM_SPEC] * 3 + [pl.BlockSpec(memory_space=pl.ANY)],
        out_specs=[HBM_SPEC] * (2 * n), input_output_aliases={t: t for t in range(2 * n)},
        compiler_params=pltpu.CompilerParams(has_side_effects=DATAFLOW),
    )(*srcs, *lands, *handle["sems"], after)
    return outs[n:]


def _mm(pairs, mode, name, res=None, out_dtype=F32):
    pairs = [p if len(p) == 4 else (p[0], p[1], p[0].shape[1], 0) for p in pairs]
    m = pairs[0][0].shape[0]
    n = pairs[0][1].shape[1] if mode == "nn" else pairs[0][1].shape[0]
    tm, tn = _pick(m, 640), _pick(n, 1408)
    np_ = len(pairs)
    dims = NN if mode == "nn" else NT

    def body(*refs):
        acc = None
        for s in range(np_):
            d = lax.dot_general(refs[2 * s][...].astype(BF16), refs[2 * s + 1][...].astype(BF16), dims,
                                preferred_element_type=F32)
            acc = d if acc is None else acc + d
        if res is not None:
            acc = acc + refs[2 * np_][...]
        refs[-1][...] = acc.astype(out_dtype)

    in_specs, args = [], []
    for a, b, kt, kb in pairs:
        in_specs.append(pl.BlockSpec((tm, kt), lambda i, j: (i, 0)))
        if mode == "nn":
            in_specs.append(pl.BlockSpec((kt, tn), lambda i, j, kb=kb: (kb, j)))
        else:
            in_specs.append(pl.BlockSpec((tn, kt), lambda i, j, kb=kb: (j, kb)))
        args += [a, b]
    if res is not None:
        in_specs.append(pl.BlockSpec((tm, tn), lambda i, j: (i, j)))
        args.append(res)
    return pl.pallas_call(
        body, name=name, grid=(m // tm, n // tn), in_specs=in_specs,
        out_specs=pl.BlockSpec((tm, tn), lambda i, j: (i, j)),
        out_shape=jax.ShapeDtypeStruct((m, n), out_dtype),
        compiler_params=_params(("parallel", "parallel")),
    )(*args)


def _mm_tn(a, b, name):
    m, k = a.shape
    n = b.shape[1]
    tm, tk, tn = _pick(m, 1664), _pick(k, 1408), _pick(n, 1408)

    def body(a_ref, b_ref, o_ref):
        @pl.when(pl.program_id(2) == 0)
        def _():
            o_ref[...] = jnp.zeros_like(o_ref)

        o_ref[...] += lax.dot_general(a_ref[...].astype(BF16), b_ref[...].astype(BF16), TN,
                                      preferred_element_type=F32)

    return pl.pallas_call(
        body, name=name, grid=(k // tk, n // tn, m // tm),
        in_specs=[pl.BlockSpec((tm, tk), lambda i, j, r: (r, i)), pl.BlockSpec((tm, tn), lambda i, j, r: (r, j))],
        out_specs=pl.BlockSpec((tk, tn), lambda i, j, r: (i, j)),
        out_shape=jax.ShapeDtypeStruct((k, n), F32),
        compiler_params=_params(("parallel", "parallel", "arbitrary")),
    )(a, b)


ROW_TILE_BYTES = 6 * 1024 * 1024


def _row_tile(rows, row_in, row_out):
    per_row = sum((r[1] * r[0].dtype.itemsize) if isinstance(r, tuple) else (r.shape[1] * r.dtype.itemsize)
                  for r in row_in)
    per_row += sum(w * jnp.dtype(dt).itemsize for w, dt in row_out)
    return _pick(rows, min(640, max(LANES, ROW_TILE_BYTES // per_row)))


def _rowcall(body, name, rows, row_in, full_in, row_out, acc_out=()):
    tr = _row_tile(rows, row_in, row_out)
    n_steps = rows // tr
    in_specs, args = [], []
    for r in row_in:
        arr, w, cb = r if isinstance(r, tuple) else (r, r.shape[1], 0)
        in_specs.append(pl.BlockSpec((tr, w), lambda i, cb=cb: (i, cb)))
        args.append(arr)
    for f in full_in:
        in_specs.append(pl.BlockSpec(f.shape, lambda i, nd=f.ndim: (0,) * nd))
        args.append(f)
    out_specs = [pl.BlockSpec((tr, w), lambda i: (i, 0)) for w, _ in row_out]
    out_shape = [jax.ShapeDtypeStruct((rows, w), dt) for w, dt in row_out]
    for shp, dt in acc_out:
        out_specs.append(pl.BlockSpec(shp, lambda i, nd=len(shp): (0,) * nd))
        out_shape.append(jax.ShapeDtypeStruct(shp, dt))

    def wrapped(*refs):
        body(pl.program_id(0), n_steps, *refs)

    return pl.pallas_call(
        wrapped, name=name, grid=(n_steps,), in_specs=in_specs, out_specs=out_specs, out_shape=out_shape,
        compiler_params=_params(("arbitrary",) if acc_out else ("parallel",)),
    )(*args)


def _rmsnorm_fwd(x, g, name):
    rows, w = x.shape

    def body(i, n, x_ref, g_ref, o_ref):
        xv = x_ref[...]
        r = lax.rsqrt(jnp.mean(xv * xv, axis=-1, keepdims=True) + EPS)
        o_ref[...] = (xv * r * g_ref[...]).astype(BF16)

    return _rowcall(body, name, rows,[x], [g], [(w, BF16)])[0]


def _rmsnorm_bwd_math(xv, dy, g):
    w = xv.shape[-1]
    r = lax.rsqrt(jnp.mean(xv * xv, axis=-1, keepdims=True) + EPS)
    t = dy * g
    dx = r * t - xv * (r * r * r * (jnp.sum(t * xv, axis=-1, keepdims=True) / w))
    dg = jnp.sum(dy * xv * r, axis=0, keepdims=True)
    return dx, dg


def _rmsnorm_bwd(x, dy, g, name, res=None, out_dtype=F32, dep=None):
    rows, w = x.shape

    def body(i, n, *refs):
        x_ref, dy_ref = refs[0], refs[1]
        g_ref, dx_ref, dg_ref = refs[-3], refs[-2], refs[-1]
        dx, dg = _rmsnorm_bwd_math(x_ref[...], dy_ref[...], g_ref[...])
        if res is not None:
            dx = dx + refs[2][...]
        dx_ref[...] = dx.astype(out_dtype)

        @pl.when(i == 0)
        def _():
            dg_ref[...] = jnp.zeros_like(dg_ref)

        dg_ref[...] += dg

    row_in = [x, dy] + ([res] if res is not None else [])
    return _rowcall(body, name, rows, row_in, ([dep] if dep is not None else []) + [g], [(w, out_dtype)],
                    [((1, w), F32)])


def _loss_bwd(h2, tgt, g, seq, name):
    rows, w = h2.shape
    tr = _row_tile(rows, [h2, tgt], [(w, F32)])

    def body(i, n, h_ref, t_ref, g_ref, dh_ref, dg_ref, lcol_ref, loss_ref):
        hv, gv = h_ref[...], g_ref[...]
        row = i * tr + lax.broadcasted_iota(jnp.int32, (tr, w), 0)
        valid = jnp.logical_and(row >= N_META, row < N_META + seq)
        r = lax.rsqrt(jnp.mean(hv * hv, axis=-1, keepdims=True) + EPS)
        err = jnp.where(valid, hv * r * gv - t_ref[...], 0.0)
        dx, dg = _rmsnorm_bwd_math(hv, err * (1.0 / w), gv)
        dh_ref[...] = dx

        @pl.when(i == 0)
        def _():
            dg_ref[...] = jnp.zeros_like(dg_ref)
            lcol_ref[...] = jnp.zeros_like(lcol_ref)

        dg_ref[...] += dg
        lcol_ref[...] += jnp.sum(err * err, axis=0, keepdims=True)

        @pl.when(i == n - 1)
        def _():
            total = jnp.sum(lcol_ref[...], axis=1, keepdims=True) * (0.5 / w)
            loss_ref[...] = jnp.broadcast_to(total, loss_ref.shape)

    return _rowcall(body, name, rows, [h2, tgt], [g], [(w, F32)],
                    [((1, w), F32), ((1, w), F32), ((1, LANES), F32)])


def _mix_fwd(um, p_rnn, p_att, bg, name):
    rows, d = p_rnn.shape

    def body(i, n, u0_ref, u1_ref, pr_ref, pa_ref, bg_ref, o_ref):
        g0 = _sig(u0_ref[...].astype(F32) + bg_ref[:, :d])
        g1 = _sig(u1_ref[...].astype(F32) + bg_ref[:, d:])
        o_ref[...] = (g0 * pr_ref[...].astype(F32) + g1 * pa_ref[...].astype(F32)).astype(BF16)

    return _rowcall(body, name, rows,[(um, d, 0), (um, d, 1), p_rnn, p_att], [bg],
                    [(d, BF16)])[0]


def _mix_bwd(um, p_rnn, p_att, dmix, bg, name):
    rows, d = p_rnn.shape

    def body(i, n, u0_ref, u1_ref, pr_ref, pa_ref, dm_ref, bg_ref, dpr_ref, dpa_ref, dum_ref, dbg_ref):
        g0 = _sig(u0_ref[...].astype(F32) + bg_ref[:, :d])
        g1 = _sig(u1_ref[...].astype(F32) + bg_ref[:, d:])
        dm = dm_ref[...].astype(F32)
        dpr_ref[...] = (dm * g0).astype(BF16)
        dpa_ref[...] = (dm * g1).astype(BF16)
        du0 = dm * pr_ref[...].astype(F32) * g0 * (1.0 - g0)
        du1 = dm * pa_ref[...].astype(F32) * g1 * (1.0 - g1)
        dum_ref[:, :d] = du0.astype(BF16)
        dum_ref[:, d:] = du1.astype(BF16)

        @pl.when(i == 0)
        def _():
            dbg_ref[...] = jnp.zeros_like(dbg_ref)

        dbg_ref[:, :d] += jnp.sum(du0, axis=0, keepdims=True)
        dbg_ref[:, d:] += jnp.sum(du1, axis=0, keepdims=True)

    return _rowcall(body, name, rows,[(um, d, 0), (um, d, 1), p_rnn, p_att, dmix], [bg],
                    [(d, BF16), (d, BF16), (2 * d, BF16)], [((1, 2 * d), F32)])


def _ffn_in_swiglu(zf, w, name):
    m, k = zf.shape
    f = w.shape[1] // 2
    tm, tn = _pick(m, 640), _pick(f, 1408)
    nb = f // tn

    def body(a_ref, bg_ref, bu_ref, g_ref, u_ref, act_ref):
        a = a_ref[...]
        gate = jnp.dot(a, bg_ref[...], preferred_element_type=F32)
        up = jnp.dot(a, bu_ref[...], preferred_element_type=F32)
        g_ref[...] = gate.astype(BF16)
        u_ref[...] = up.astype(BF16)
        act_ref[...] = (gate * _sig(gate) * up).astype(BF16)

    tile = pl.BlockSpec((tm, tn), lambda i, j: (i, j))
    return pl.pallas_call(
        body, name=name, grid=(m // tm, nb),
        in_specs=[pl.BlockSpec((tm, k), lambda i, j: (i, 0)), pl.BlockSpec((k, tn), lambda i, j: (0, j)),
                  pl.BlockSpec((k, tn), lambda i, j: (0, j + nb))],
        out_specs=[tile] * 3, out_shape=[jax.ShapeDtypeStruct((m, f), BF16)] * 3,
        compiler_params=_params(("parallel", "parallel")),
    )(zf, w, w)


def _ffn_out_bwd_swiglu(dh, w_out, gate, up, name):
    m, k = dh.shape
    f = w_out.shape[0]
    tm, tn = _pick(m, 640), _pick(f, 1408)

    def body(a_ref, b_ref, g_ref, u_ref, dg_ref, du_ref):
        da = lax.dot_general(a_ref[...].astype(BF16), b_ref[...], NT, preferred_element_type=F32)
        gv = g_ref[...].astype(F32)
        sg = _sig(gv)
        dg_ref[...] = (da * u_ref[...].astype(F32) * (sg * (1.0 + gv * (1.0 - sg)))).astype(BF16)
        du_ref[...] = (da * gv * sg).astype(BF16)

    tile = pl.BlockSpec((tm, tn), lambda i, j: (i, j))
    return pl.pallas_call(
        body, name=name, grid=(m // tm, f // tn),
        in_specs=[pl.BlockSpec((tm, k), lambda i, j: (i, 0)), pl.BlockSpec((tn, k), lambda i, j: (j, 0)), tile, tile],
        out_specs=[tile] * 2, out_shape=[jax.ShapeDtypeStruct((m, f), BF16)] * 2,
        compiler_params=_params(("parallel", "parallel")),
    )(dh, w_out, gate, up)


def _rope_tables(lp):
    idx = jnp.arange(lp, dtype=jnp.int32).astype(F32)
    inv_freq = ROPE_THETA ** (-jnp.arange(0, QK_ROPE, 2, dtype=F32) / QK_ROPE)
    ang = idx[:, None] * inv_freq[None, :]
    cos, sin = jnp.cos(ang), jnp.sin(ang)
    half = QK_ROPE // 2
    z = lambda wdt: jnp.zeros((lp, wdt), F32)
    tc = jnp.concatenate([cos, cos, z(LANES - 2 * half)], axis=1)
    ts1 = jnp.concatenate([-sin, z(LANES - half)], axis=1)
    ts2 = jnp.concatenate([z(half), sin, z(LANES - 2 * half)], axis=1)
    return tc, ts1, ts2


def _rope(xv, tc, ts1, ts2):
    half = QK_ROPE // 2
    return xv * tc + pltpu.roll(xv, LANES - half, 1) * ts1 + pltpu.roll(xv, half, 1) * ts2


def _rope_t(dv, tc, ts1, ts2):
    half = QK_ROPE // 2
    return dv * tc + pltpu.roll(dv * ts1, half, 1) + pltpu.roll(dv * ts2, LANES - half, 1)


def _rope_fwd(qpad, kpad, ukr, tabs, name):
    rows, w = qpad.shape

    def body(i, n, q_ref, k_ref, r_ref, c_ref, s1_ref, s2_ref, qo_ref, ko_ref):
        tc, ts1, ts2 = c_ref[...], s1_ref[...], s2_ref[...]
        kr = _rope(r_ref[...], tc, ts1, ts2).astype(BF16)
        for h in range(N_HEADS):
            lo, mid, hi = h * HEAD_PAD, h * HEAD_PAD + QK_NOPE, (h + 1) * HEAD_PAD
            qo_ref[:, lo:mid] = q_ref[:, lo:mid].astype(BF16)
            qo_ref[:, mid:hi] = _rope(q_ref[:, mid:hi].astype(F32), tc, ts1, ts2).astype(BF16)
            ko_ref[:, lo:mid] = k_ref[:, lo:mid].astype(BF16)
            ko_ref[:, mid:hi] = kr

    return _rowcall(body, name, rows,[qpad, kpad, ukr, *tabs], [], [(w, BF16), (w, BF16)])


def _rope_bwd(dq, dk, tabs, name):
    rows, w = dq.shape

    def body(i, n, dq_ref, dk_ref, c_ref, s1_ref, s2_ref, qo_ref, ko_ref, ro_ref):
        tc, ts1, ts2 = c_ref[...], s1_ref[...], s2_ref[...]
        dkr = None
        for h in range(N_HEADS):
            lo, mid, hi = h * HEAD_PAD, h * HEAD_PAD + QK_NOPE, (h + 1) * HEAD_PAD
            qo_ref[:, lo:mid] = dq_ref[:, lo:mid].astype(BF16)
            qo_ref[:, mid:hi] = _rope_t(dq_ref[:, mid:hi].astype(F32), tc, ts1, ts2).astype(BF16)
            ko_ref[:, lo:mid] = dk_ref[:, lo:mid].astype(BF16)
            ko_ref[:, mid:hi] = jnp.zeros((ko_ref.shape[0], hi - mid), BF16)
            part = dk_ref[:, mid:hi].astype(F32)
            dkr = part if dkr is None else dkr + part
        ro_ref[...] = _rope_t(dkr, tc, ts1, ts2).astype(BF16)

    return _rowcall(body, name, rows,[dq, dk, *tabs], [],
                    [(w, BF16), (w, BF16), (LANES, BF16)])


def _visible(q0, k0, tq, tk):
    qrow = q0 + lax.broadcasted_iota(jnp.int32, (tq, tk), 0)
    kcol = k0 + lax.broadcasted_iota(jnp.int32, (tq, tk), 1)
    return ((kcol + CHUNK_BIAS) >> CHUNK_SHIFT) <= ((qrow + CHUNK_BIAS) >> CHUNK_SHIFT)


def _visible_t(k0, q0, tk, tq):
    krow = k0 + lax.broadcasted_iota(jnp.int32, (tk, tq), 0)
    qcol = q0 + lax.broadcasted_iota(jnp.int32, (tk, tq), 1)
    return ((krow + CHUNK_BIAS) >> CHUNK_SHIFT) <= ((qcol + CHUNK_BIAS) >> CHUNK_SHIFT)


def _lanes(v, width):
    return jnp.tile(v, (1, width // LANES))


def _pipelined_chunks(n_full, scores, absorb):
    scores(0, 0)

    def pair(jj, carry):
        a = 2 * jj
        scores(a + 1, 1)
        absorb(a, 0, False)
        scores(a + 2, 0)
        absorb(a + 1, 1, False)
        return carry

    lax.fori_loop(0, n_full // 2, pair, 0)

    @pl.when(n_full % 2 == 0)
    def _():
        absorb(n_full, 0, True)

    @pl.when(n_full % 2 == 1)
    def _():
        scores(n_full, 1)
        absorb(n_full - 1, 0, False)
        absorb(n_full, 1, True)


def _attn_fwd(q, k, v, t, name):
    lp = q.shape[0]
    nt = lp // t

    def body(q_ref, k_ref, v_ref, o_ref, lse_ref, m_s, l_s, acc_s, a_s, sa_s, sb_s, p_s):
        i = pl.program_id(1)
        m_s[...] = jnp.full(m_s.shape, NEG, F32)
        l_s[...] = jnp.zeros(l_s.shape, F32)
        acc_s[...] = jnp.zeros(acc_s.shape, F32)

        s_bufs = (sa_s, sb_s)

        def scores(j, slot):
            r0 = pl.multiple_of(j * t, t)
            s_bufs[slot][...] = lax.dot_general(q_ref[...], k_ref[pl.ds(r0, t), :], NT,
                                                preferred_element_type=F32)

        def absorb(j, slot, masked):
            for r in range(0, t, ROW_GROUP):
                rows = slice(r, r + ROW_GROUP)
                s = s_bufs[slot][rows, :]
                if masked:
                    s = jnp.where(_visible(i * t + r, i * t, ROW_GROUP, t), s, NEG)
                m_prev = m_s[rows, :]
                m_new = jnp.maximum(m_prev, jnp.max(s, axis=1, keepdims=True))
                alpha = jnp.exp2((m_prev - m_new) * SCALE_LOG2E)
                p = jnp.exp2((s - _lanes(m_new, t)) * SCALE_LOG2E)
                l_s[rows, :] = alpha * l_s[rows, :] + jnp.sum(p, axis=1, keepdims=True)
                m_s[rows, :] = m_new
                a_s[rows, :] = alpha
                p_s[rows, :] = p.astype(BF16)
            r0 = pl.multiple_of(j * t, t)
            acc_s[...] = a_s[...] * acc_s[...] + jnp.dot(p_s[...], v_ref[pl.ds(r0, t), :],
                                                         preferred_element_type=F32)

        r1 = pl.multiple_of(jnp.minimum(i + 1, nt - 1) * t, t)
        rows = slice(t - SPILL, t)
        s = lax.dot_general(q_ref[rows, :], k_ref[pl.ds(r1, SPILL), :], NT, preferred_element_type=F32)
        seen = jnp.logical_and(_visible(i * t + t - SPILL, (i + 1) * t, SPILL, SPILL), i + 1 < nt)
        s = jnp.where(seen, s, NEG)
        m_new = jnp.max(s, axis=1, keepdims=True)
        p = jnp.exp2((s - m_new) * SCALE_LOG2E)
        l_s[rows, :] = jnp.broadcast_to(jnp.sum(p, axis=1, keepdims=True), (SPILL, LANES))
        acc_s[rows, :] = jnp.dot(p.astype(BF16), v_ref[pl.ds(r1, SPILL), :], preferred_element_type=F32)
        m_s[rows, :] = jnp.broadcast_to(m_new, (SPILL, LANES))

        _pipelined_chunks(i, scores, absorb)
        o_ref[...] = (acc_s[...] / l_s[...]).astype(BF16)
        lse_ref[...] = m_s[...] * ATTN_SCALE + jnp.log(l_s[...])

    return pl.pallas_call(
        body, name=name, grid=(N_HEADS, nt),
        in_specs=[pl.BlockSpec((t, HEAD_PAD), lambda h, i: (i, h)),
                  pl.BlockSpec((lp, HEAD_PAD), lambda h, i: (0, h)),
                  pl.BlockSpec((lp, V_DIM), lambda h, i: (0, h))],
        out_specs=[pl.BlockSpec((t, V_DIM), lambda h, i: (i, h)),
                   pl.BlockSpec((None, t, LANES), lambda h, i: (h, i, 0))],
        out_shape=[jax.ShapeDtypeStruct((lp, N_HEADS * V_DIM), BF16),
                   jax.ShapeDtypeStruct((N_HEADS, lp, LANES), F32)],
        scratch_shapes=[pltpu.VMEM((t, LANES), F32), pltpu.VMEM((t, LANES), F32), pltpu.VMEM((t, V_DIM), F32),
                        pltpu.VMEM((t, LANES), F32), pltpu.VMEM((t, t), F32), pltpu.VMEM((t, t), F32),
                        pltpu.VMEM((t, t), BF16)],
        compiler_params=_params(("parallel", "arbitrary")),
    )(q, k, v)


def _attn_bwd_dq(q, k, v, do, o, lse, dep, t, name):
    lp = q.shape[0]
    nt = lp // t

    def body(q_ref, k_ref, v_ref, do_ref, o_ref, lse_ref, dep_ref, dq_ref, l2row_ref, dlrow_ref,
             acc_s, l2_s, dl_s, sa_s, sb_s, da_s, db_s, ds_s):
        i = pl.program_id(1)
        delta = jnp.sum(do_ref[...].astype(F32) * o_ref[...].astype(F32), axis=1, keepdims=True)
        dl_s[...] = jnp.broadcast_to(delta, dl_s.shape)
        l2_s[...] = lse_ref[...] * LOG2E
        l2row_ref[...] = l2_s[...].T[0:SUBLANES, :]
        dlrow_ref[...] = dl_s[...].T[0:SUBLANES, :]
        acc_s[...] = jnp.zeros(acc_s.shape, F32)
        s_bufs, d_bufs = (sa_s, sb_s), (da_s, db_s)

        def dscores(s, dp, rows, width):
            p = jnp.exp2(s * SCALE_LOG2E - _lanes(l2_s[rows, :], width))
            return (p * (dp - _lanes(dl_s[rows, :], width))).astype(BF16)

        r1 = pl.multiple_of(jnp.minimum(i + 1, nt - 1) * t, t)
        rows = slice(t - SPILL, t)
        ks, vs = k_ref[pl.ds(r1, SPILL), :], v_ref[pl.ds(r1, SPILL), :]
        s = lax.dot_general(q_ref[rows, :], ks, NT, preferred_element_type=F32)
        seen = jnp.logical_and(_visible(i * t + t - SPILL, (i + 1) * t, SPILL, SPILL), i + 1 < nt)
        s = jnp.where(seen, s, NEG)
        dp = lax.dot_general(do_ref[rows, :], vs, NT, preferred_element_type=F32)
        acc_s[rows, :] = jnp.dot(dscores(s, dp, rows, SPILL), ks, preferred_element_type=F32)

        def scores(j, slot):
            r0 = pl.multiple_of(j * t, t)
            s_bufs[slot][...] = lax.dot_general(q_ref[...], k_ref[pl.ds(r0, t), :], NT,
                                                preferred_element_type=F32)
            d_bufs[slot][...] = lax.dot_general(do_ref[...], v_ref[pl.ds(r0, t), :], NT,
                                                preferred_element_type=F32)

        def absorb(j, slot, masked):
            for r in range(0, t, ROW_GROUP):
                rows = slice(r, r + ROW_GROUP)
                s = s_bufs[slot][rows, :]
                if masked:
                    s = jnp.where(_visible(i * t + r, i * t, ROW_GROUP, t), s, NEG)
                ds_s[rows, :] = dscores(s, d_bufs[slot][rows, :], rows, t)
            r0 = pl.multiple_of(j * t, t)
            acc_s[...] += jnp.dot(ds_s[...], k_ref[pl.ds(r0, t), :], preferred_element_type=F32)

        _pipelined_chunks(i, scores, absorb)
        dq_ref[...] = (acc_s[...] * ATTN_SCALE).astype(BF16)

    stat_row = pl.BlockSpec((None, None, SUBLANES, t), lambda h, i: (h, i, 0, 0))
    return pl.pallas_call(
        body, name=name, grid=(N_HEADS, nt),
        in_specs=[pl.BlockSpec((t, HEAD_PAD), lambda h, i: (i, h)),
                  pl.BlockSpec((lp, HEAD_PAD), lambda h, i: (0, h)),
                  pl.BlockSpec((lp, V_DIM), lambda h, i: (0, h)),
                  pl.BlockSpec((t, V_DIM), lambda h, i: (i, h)),
                  pl.BlockSpec((t, V_DIM), lambda h, i: (i, h)),
                  pl.BlockSpec((None, t, LANES), lambda h, i: (h, i, 0)),
                  pl.BlockSpec(dep.shape, lambda h, i: (0, 0))],
        out_specs=[pl.BlockSpec((t, HEAD_PAD), lambda h, i: (i, h)), stat_row, stat_row],
        out_shape=[jax.ShapeDtypeStruct((lp, N_HEADS * HEAD_PAD), BF16),
                   jax.ShapeDtypeStruct((N_HEADS, nt, SUBLANES, t), F32),
                   jax.ShapeDtypeStruct((N_HEADS, nt, SUBLANES, t), F32)],
        scratch_shapes=[pltpu.VMEM((t, HEAD_PAD), F32), pltpu.VMEM((t, LANES), F32), pltpu.VMEM((t, LANES), F32),
                        pltpu.VMEM((t, t), F32), pltpu.VMEM((t, t), F32), pltpu.VMEM((t, t), F32),
                        pltpu.VMEM((t, t), F32), pltpu.VMEM((t, t), BF16)],
        compiler_params=_params(("parallel", "arbitrary")),
    )(q, k, v, do, o, lse, dep)


def _attn_bwd_dkv(q, k, v, do, l2row, dlrow, t, name):
    lp = q.shape[0]
    nt = lp // t

    def body(q_ref, k_ref, v_ref, do_ref, l2_ref, dl_ref, dk_ref, dv_ref,
             dk_s, dv_s, sa_s, sb_s, da_s, db_s, p_s, ds_s):
        j = pl.program_id(1)
        dk_s[...] = jnp.zeros(dk_s.shape, F32)
        dv_s[...] = jnp.zeros(dv_s.shape, F32)
        s_bufs, d_bufs = (sa_s, sb_s), (da_s, db_s)

        def weights(st, dpt, l2r, dlr):
            pt = jnp.exp2(st * SCALE_LOG2E - l2r)
            return pt.astype(BF16), (pt * (dpt - dlr)).astype(BF16)

        prev = jnp.maximum(j - 1, 0)
        q0 = pl.multiple_of(prev * t + t - SPILL, SPILL)
        rows = slice(0, SPILL)
        qs, dos = q_ref[pl.ds(q0, SPILL), :], do_ref[pl.ds(q0, SPILL), :]
        st = lax.dot_general(k_ref[rows, :], qs, NT, preferred_element_type=F32)
        seen = jnp.logical_and(_visible_t(j * t, j * t - SPILL, SPILL, SPILL), j > 0)
        st = jnp.where(seen, st, NEG)
        dpt = lax.dot_general(v_ref[rows, :], dos, NT, preferred_element_type=F32)
        pt, dst = weights(st, dpt, l2_ref[prev, 0:1, t - SPILL:], dl_ref[prev, 0:1, t - SPILL:])
        dv_s[rows, :] = jnp.dot(pt, dos, preferred_element_type=F32)
        dk_s[rows, :] = jnp.dot(dst, qs, preferred_element_type=F32)

        def scores(c, slot):
            r0 = pl.multiple_of((nt - 1 - c) * t, t)
            s_bufs[slot][...] = lax.dot_general(k_ref[...], q_ref[pl.ds(r0, t), :], NT,
                                                preferred_element_type=F32)
            d_bufs[slot][...] = lax.dot_general(v_ref[...], do_ref[pl.ds(r0, t), :], NT,
                                                preferred_element_type=F32)

        def absorb(c, slot, masked):
            i = nt - 1 - c
            l2r, dlr = l2_ref[i, 0:1, :], dl_ref[i, 0:1, :]
            for r in range(0, t, ROW_GROUP):
                rows = slice(r, r + ROW_GROUP)
                st = s_bufs[slot][rows, :]
                if masked:
                    st = jnp.where(_visible_t(j * t + r, j * t, ROW_GROUP, t), st, NEG)
                p_s[rows, :], ds_s[rows, :] = weights(st, d_bufs[slot][rows, :], l2r, dlr)
            r0 = pl.multiple_of(i * t, t)
            dv_s[...] += jnp.dot(p_s[...], do_ref[pl.ds(r0, t), :], preferred_element_type=F32)
            dk_s[...] += jnp.dot(ds_s[...], q_ref[pl.ds(r0, t), :], preferred_element_type=F32)

        _pipelined_chunks(nt - 1 - j, scores, absorb)
        dk_ref[...] = (dk_s[...] * ATTN_SCALE).astype(BF16)
        dv_ref[...] = dv_s[...].astype(BF16)

    stat_rows = pl.BlockSpec((None, nt, SUBLANES, t), lambda h, j: (h, 0, 0, 0))
    return pl.pallas_call(
        body, name=name, grid=(N_HEADS, nt),
        in_specs=[pl.BlockSpec((lp, HEAD_PAD), lambda h, j: (0, h)),
                  pl.BlockSpec((t, HEAD_PAD), lambda h, j: (j, h)),
                  pl.BlockSpec((t, V_DIM), lambda h, j: (j, h)),
                  pl.BlockSpec((lp, V_DIM), lambda h, j: (0, h)),
                  stat_rows, stat_rows],
        out_specs=[pl.BlockSpec((t, HEAD_PAD), lambda h, j: (j, h)),
                   pl.BlockSpec((t, V_DIM), lambda h, j: (j, h))],
        out_shape=[jax.ShapeDtypeStruct((lp, N_HEADS * HEAD_PAD), BF16),
                   jax.ShapeDtypeStruct((lp, N_HEADS * V_DIM), BF16)],
        scratch_shapes=[pltpu.VMEM((t, HEAD_PAD), F32), pltpu.VMEM((t, V_DIM), F32),
                        pltpu.VMEM((t, t), F32), pltpu.VMEM((t, t), F32), pltpu.VMEM((t, t), F32),
                        pltpu.VMEM((t, t), F32), pltpu.VMEM((t, t), BF16), pltpu.VMEM((t, t), BF16)],
        compiler_params=_params(("parallel", "arbitrary")),
    )(q, k, v, do, l2row, dlrow)


def _shift_down(cur, prev8, k):
    r = pltpu.roll(cur, k, 0)
    row8 = lax.broadcasted_iota(jnp.int32, prev8.shape, 0)
    first = jnp.where(row8 < k, pltpu.roll(prev8, k, 0), r[0:SUBLANES])
    return jnp.concatenate([first, r[SUBLANES:]], axis=0)


def _shift_up(cur, next8, k):
    t = cur.shape[0]
    r = pltpu.roll(cur, t - k, 0)
    row8 = lax.broadcasted_iota(jnp.int32, next8.shape, 0)
    last = jnp.where(row8 >= SUBLANES - k, pltpu.roll(next8, SUBLANES - k, 0), r[t - SUBLANES:])
    return jnp.concatenate([r[:t - SUBLANES], last], axis=0)


def _scan_rows(a, b, edge, reverse):
    t, d = a.shape
    groups = t // SUBLANES
    a3, b3 = a.reshape(groups, SUBLANES, d), b.reshape(groups, SUBLANES, d)
    sub = lax.broadcasted_iota(jnp.int32, a3.shape, 1)
    s = 1
    while s < SUBLANES:
        keep = sub < SUBLANES - s if reverse else sub >= s
        shift = SUBLANES - s if reverse else s
        a_sh = jnp.where(keep, pltpu.roll(a3, shift, 1), 1.0)
        b_sh = jnp.where(keep, pltpu.roll(b3, shift, 1), 0.0)
        b3 = a3 * b_sh + b3
        a3 = a3 * a_sh
        s *= 2
    out = [None] * groups
    for k in (range(groups - 1, -1, -1) if reverse else range(groups)):
        out[k] = b3[k] + a3[k] * edge
        edge = out[k][0:1, :] if reverse else out[k][SUBLANES - 1:SUBLANES, :]
    return jnp.concatenate(out, axis=0)


def _sqrt_one_minus_exp2x(x):
    th = jnp.tanh(x)
    m2 = (-2.0 * th) / (1.0 - th)
    return m2 * lax.rsqrt(jnp.maximum(m2, TINY))


def _log_sigmoid(x):
    return jnp.minimum(x, 0.0) - jnp.log(1.0 + jnp.exp(-jnp.abs(x)))


GELU_C = math.sqrt(2.0 / math.pi)
GELU_K = 0.044715


def _gelu(x):
    th = jnp.tanh(GELU_C * (x + GELU_K * x * x * x))
    return 0.5 * x * (1.0 + th), th


def _block_mm(xb, w_ref, dims):
    rb = D_RNN // RNN_BLOCKS
    return jnp.concatenate(
        [lax.dot_general(xb[:, h * rb:(h + 1) * rb], w_ref[h], dims, preferred_element_type=F32)
         for h in range(RNN_BLOCKS)], axis=1)


def _rglru_gates(ux, prev8, pv_ref, wa_ref, wi_ref):
    shifted = [ux] + [_shift_down(ux, prev8, k) for k in range(1, CONV_WIDTH)]
    xc = pv_ref[4:5, :] + pv_ref[3:4, :] * ux
    for k in range(1, CONV_WIDTH):
        xc = xc + pv_ref[3 - k:4 - k, :] * shifted[k]
    xcb = xc.astype(BF16)
    r_g = _sig(_block_mm(xcb, wa_ref, NN) + pv_ref[5:6, :])
    i_g = _sig(_block_mm(xcb, wi_ref, NN) + pv_ref[6:7, :])
    log_a = LRU_C * r_g * _log_sigmoid(pv_ref[7:8, :])
    a = jnp.exp(log_a)
    mm = _sqrt_one_minus_exp2x(log_a)
    return dict(shifted=shifted, xc=xc, xcb=xcb, r=r_g, i=i_g, a=a, mm=mm)


def _rglru_fwd(ux, ug, pv, wa, wi, t, name):
    lp, d = ux.shape

    def body(ux_ref, ug_ref, pv_ref, wa_ref, wi_ref, y_ref, h_ref, tail_s, hc_s):
        @pl.when(pl.program_id(0) == 0)
        def _():
            tail_s[...] = jnp.zeros_like(tail_s)
            hc_s[...] = jnp.zeros_like(hc_s)

        uxv = ux_ref[...]
        gt = _rglru_gates(uxv, tail_s[...], pv_ref, wa_ref, wi_ref)
        tail_s[...] = ux_ref[t - SUBLANES:t, :]
        h_ref[...] = _scan_rows(gt["a"], gt["mm"] * (gt["i"] * gt["xc"]), hc_s[0:1, :], False)
        hc_s[...] = h_ref[t - SUBLANES:t, :]
        hc_s[0:1, :] = h_ref[t - 1:t, :]
        y_ref[...] = (h_ref[...] * _gelu(ug_ref[...])[0]).astype(BF16)

    tile = pl.BlockSpec((t, d), lambda i: (i, 0))
    return pl.pallas_call(
        body, name=name, grid=(lp // t,),
        in_specs=[tile, tile, pl.BlockSpec(pv.shape, lambda i: (0, 0)),
                  pl.BlockSpec(wa.shape, lambda i: (0, 0, 0)), pl.BlockSpec(wi.shape, lambda i: (0, 0, 0))],
        out_specs=[tile, tile],
        out_shape=[jax.ShapeDtypeStruct((lp, d), BF16), jax.ShapeDtypeStruct((lp, d), F32)],
        scratch_shapes=[pltpu.VMEM((SUBLANES, d), F32), pltpu.VMEM((SUBLANES, d), F32)],
        compiler_params=_params(("arbitrary",)),
    )(ux, ug, pv, wa, wi)


def _rglru_bwd(ux, ug, hs, dy, pv, wa, wi, dep, t, name):
    lp, d = ux.shape
    nt = lp // t
    per = t // SUBLANES
    rb = d // RNN_BLOCKS

    def body(ux_ref, uxp_ref, ug_ref, h_ref, hp_ref, dy_ref, pv_ref, wa_ref, wi_ref, dep_ref,
             dux_ref, dug_ref, dpv_ref, dwa_ref, dwi_ref, ca_s, cg_s, cx_s):
        step = pl.program_id(0)
        first_tile = step == nt - 1

        @pl.when(step == 0)
        def _():
            for ref in (ca_s, cg_s, cx_s, dpv_ref, dwa_ref, dwi_ref):
                ref[...] = jnp.zeros_like(ref)

        uxv = ux_ref[...]
        prev8 = jnp.where(first_tile, 0.0, uxp_ref[...])
        hprev8 = jnp.where(first_tile, 0.0, hp_ref[...])
        gt = _rglru_gates(uxv, prev8, pv_ref, wa_ref, wi_ref)
        a, mm, r_g, i_g, xc = gt["a"], gt["mm"], gt["r"], gt["i"], gt["xc"]
        hv = h_ref[...]
        hprev = _shift_down(hv, hprev8, 1)
        ugv, dyv = ug_ref[...], dy_ref[...]
        gel, th = _gelu(ugv)
        dgel = 0.5 * (1.0 + th) + 0.5 * ugv * (1.0 - th * th) * (GELU_C * (1.0 + 3.0 * GELU_K * ugv * ugv))
        dug_ref[...] = (dyv * hv * dgel).astype(BF16)
        a_up = _shift_up(a, ca_s[...], 1)
        gv = _scan_rows(a_up, dyv * gel, cg_s[0:1, :], True)
        ca_s[...] = a[0:SUBLANES]
        cg_s[...] = gv[0:SUBLANES]
        ixc = i_g * xc
        d_ixc = gv * mm
        d_log_a = gv * hprev * a - (gv * ixc) * (a * a) / mm
        logsig = _log_sigmoid(pv_ref[7:8, :])
        d_pre_a = d_log_a * (LRU_C * logsig) * r_g * (1.0 - r_g)
        d_pre_i = d_ixc * xc * i_g * (1.0 - i_g)
        dab, dib = d_pre_a.astype(BF16), d_pre_i.astype(BF16)
        d_xc = d_ixc * i_g + _block_mm(dab, wa_ref, NT) + _block_mm(dib, wi_ref, NT)
        xcb = gt["xcb"]
        for h in range(RNN_BLOCKS):
            cols = slice(h * rb, (h + 1) * rb)
            dwa_ref[h] += lax.dot_general(xcb[:, cols], dab[:, cols], TN, preferred_element_type=F32)
            dwi_ref[h] += lax.dot_general(xcb[:, cols], dib[:, cols], TN, preferred_element_type=F32)
        csum = lambda v: jnp.sum(v, axis=0, keepdims=True)
        for k in range(CONV_WIDTH):
            dpv_ref[3 - k:4 - k, :] += csum(d_xc * gt["shifted"][k])
        dpv_ref[4:5, :] += csum(d_xc)
        dpv_ref[5:6, :] += csum(d_pre_a)
        dpv_ref[6:7, :] += csum(d_pre_i)
        dpv_ref[7:8, :] += csum(d_log_a * (LRU_C * r_g)) * _sig(-pv_ref[7:8, :])
        dux = pv_ref[3:4, :] * d_xc
        for k in range(1, CONV_WIDTH):
            dux = dux + pv_ref[3 - k:4 - k, :] * _shift_up(d_xc, cx_s[...], k)
        cx_s[...] = d_xc[0:SUBLANES]
        dux_ref[...] = dux.astype(BF16)

    rev = lambda i: (nt - 1 - i, 0)
    before = lambda i: (jnp.maximum((nt - 1 - i) * per - 1, 0), 0)
    tile = pl.BlockSpec((t, d), rev)
    tail = pl.BlockSpec((SUBLANES, d), before)
    fixed2 = lambda arr: pl.BlockSpec(arr.shape, lambda i: (0, 0))
    fixed3 = lambda arr: pl.BlockSpec(arr.shape, lambda i: (0, 0, 0))
    return pl.pallas_call(
        body, name=name, grid=(nt,),
        in_specs=[tile, tail, tile, tile, tail, tile, fixed2(pv), fixed3(wa), fixed3(wi), fixed2(dep)],
        out_specs=[tile, tile, fixed2(pv), fixed3(wa), fixed3(wi)],
        out_shape=[jax.ShapeDtypeStruct((lp, d), BF16), jax.ShapeDtypeStruct((lp, d), BF16),
                   jax.ShapeDtypeStruct(pv.shape, F32), jax.ShapeDtypeStruct(wa.shape, F32),
                   jax.ShapeDtypeStruct(wi.shape, F32)],
        scratch_shapes=[pltpu.VMEM((SUBLANES, d), F32)] * 3,
        compiler_params=_params(("arbitrary",)),
    )(ux, ux, ug, hs, hs, dy, pv, wa, wi, dep)


def _adamw(w, m, v, parts, dep, name):
    rows, cols = w.shape
    tr = _pick(rows, 256, SUBLANES)
    c1 = 1.0 / (1.0 - ADAM_B1 ** ADAM_STEP)
    c2 = 1.0 / (1.0 - ADAM_B2 ** ADAM_STEP)

    def body(w_ref, m_ref, v_ref, p_ref, dep_ref, g_ref, d_ref, mo_ref, vo_ref):
        g = p_ref[0].astype(F32)
        for q in range(1, N_DEV):
            g = g + p_ref[q].astype(F32)
        mn = ADAM_B1 * m_ref[...] + (1.0 - ADAM_B1) * g
        vn = ADAM_B2 * v_ref[...] + (1.0 - ADAM_B2) * (g * g)
        g_ref[...] = g
        mo_ref[...] = mn
        vo_ref[...] = vn
        d_ref[...] = -ADAM_LR * ((mn * c1) / (jnp.sqrt(vn * c2) + ADAM_EPS) + ADAM_WD * w_ref[...])

    blk = pl.BlockSpec((tr, cols), lambda i: (i, 0))
    return pl.pallas_call(
        body, name=name, grid=(rows // tr,),
        in_specs=[blk, blk, blk, pl.BlockSpec((N_DEV, tr, cols), lambda i: (0, i, 0)),
                  pl.BlockSpec(dep.shape, lambda i: (0, 0))],
        out_specs=[blk] * 4, out_shape=[jax.ShapeDtypeStruct((rows, cols), F32)] * 4,
        compiler_params=_params(("parallel",)),
    )(w, m, v, parts, dep)


WEIGHTS = ("meta_tokens", "norm_mix_g", "w_in", "b_gate", "conv_w", "conv_b", "w_rec_a", "b_rec_a", "w_rec_i",
           "b_rec_i", "lru_lambda", "q_norm_g", "w_uq", "kv_norm_g", "w_ukv", "w_branch", "w_out", "norm_ffn_g",
           "w_ffn_in", "w_ffn_out", "final_norm_g")
SHARDED = {"meta_tokens": True, "w_in": True, "b_gate": True, "conv_w": True, "w_uq": True, "w_ukv": True,
           "w_branch": False, "w_out": False, "w_ffn_in": True, "w_ffn_out": False}


def _as2d(a):
    return a.reshape(-1, a.shape[-1])


def _full_from_gathered(g, by_cols):
    if by_cols:
        return jnp.transpose(g, (1, 0, 2)).reshape(g.shape[1], N_DEV * g.shape[2])
    return g.reshape(N_DEV * g.shape[1], g.shape[2])


def _blocks_from_full(full, by_cols):
    if by_cols:
        r, c = full.shape
        return jnp.transpose(full.reshape(r, N_DEV, c // N_DEV), (1, 0, 2))
    return full.reshape(N_DEV, full.shape[0] // N_DEV, full.shape[1])


def kernel(x, meta_tokens, norm_mix_g, w_in, b_gate, conv_w, conv_b, w_rec_a, b_rec_a, w_rec_i, b_rec_i, lru_lambda, q_norm_g, w_uq, kv_norm_g, w_ukv, w_branch, w_out, norm_ffn_g, w_ffn_in, w_ffn_out, final_norm_g, loss_target, m_meta_tokens, m_norm_mix_g, m_w_in, m_b_gate, m_conv_w, m_conv_b, m_w_rec_a, m_b_rec_a, m_w_rec_i, m_b_rec_i, m_lru_lambda, m_q_norm_g, m_w_uq, m_kv_norm_g, m_w_ukv, m_w_branch, m_w_out, m_norm_ffn_g, m_w_ffn_in, m_w_ffn_out, m_final_norm_g, v_meta_tokens, v_norm_mix_g, v_w_in, v_b_gate, v_conv_w, v_conv_b, v_w_rec_a, v_b_rec_a, v_w_rec_i, v_b_rec_i, v_lru_lambda, v_q_norm_g, v_w_uq, v_kv_norm_g, v_w_ukv, v_w_branch, v_w_out, v_norm_ffn_g, v_w_ffn_in, v_w_ffn_out, v_final_norm_g):
    w = dict(meta_tokens=meta_tokens, norm_mix_g=norm_mix_g, w_in=w_in, b_gate=b_gate, conv_w=conv_w, conv_b=conv_b,
             w_rec_a=w_rec_a, b_rec_a=b_rec_a, w_rec_i=w_rec_i, b_rec_i=b_rec_i, lru_lambda=lru_lambda,
             q_norm_g=q_norm_g, w_uq=w_uq, kv_norm_g=kv_norm_g, w_ukv=w_ukv, w_branch=w_branch, w_out=w_out,
             norm_ffn_g=norm_ffn_g, w_ffn_in=w_ffn_in, w_ffn_out=w_ffn_out, final_norm_g=final_norm_g)
    m = dict(meta_tokens=m_meta_tokens, norm_mix_g=m_norm_mix_g, w_in=m_w_in, b_gate=m_b_gate, conv_w=m_conv_w,
             conv_b=m_conv_b, w_rec_a=m_w_rec_a, b_rec_a=m_b_rec_a, w_rec_i=m_w_rec_i, b_rec_i=m_b_rec_i,
             lru_lambda=m_lru_lambda, q_norm_g=m_q_norm_g, w_uq=m_w_uq, kv_norm_g=m_kv_norm_g, w_ukv=m_w_ukv,
             w_branch=m_w_branch, w_out=m_w_out, norm_ffn_g=m_norm_ffn_g, w_ffn_in=m_w_ffn_in,
             w_ffn_out=m_w_ffn_out, final_norm_g=m_final_norm_g)
    v = dict(meta_tokens=v_meta_tokens, norm_mix_g=v_norm_mix_g, w_in=v_w_in, b_gate=v_b_gate, conv_w=v_conv_w,
             conv_b=v_conv_b, w_rec_a=v_w_rec_a, b_rec_a=v_b_rec_a, w_rec_i=v_w_rec_i, b_rec_i=v_b_rec_i,
             lru_lambda=v_lru_lambda, q_norm_g=v_q_norm_g, w_uq=v_w_uq, kv_norm_g=v_kv_norm_g, w_ukv=v_w_ukv,
             w_branch=v_w_branch, w_out=v_w_out, norm_ffn_g=v_norm_ffn_g, w_ffn_in=v_w_ffn_in,
             w_ffn_out=v_w_ffn_out, final_norm_g=v_final_norm_g)

    seq, d_model = x.shape[1], x.shape[2]
    length = N_META + seq
    lp = -(-length // LANES) * LANES
    t_attn = _pick(lp, 640)
    t_rnn = LANES

    small = ("meta_tokens", "b_gate", "conv_w")
    names = list(SHARDED)
    mid, late = ("w_uq", "w_ukv", "w_branch", "w_out"), ("w_ffn_in", "w_ffn_out")
    payload = lambda n: _as2d(w[n]) if n in small else _as2d(w[n]).astype(BF16)
    got = _exchange([payload(n) for n in small], [True] * len(small), "gather_small")
    in_h = _exchange_start([payload("w_in")], [True], got[0], "gather_in_start")
    mid_h = _exchange_start([payload(n) for n in mid], [True] * len(mid), in_h["token"], "gather_mid_start")
    late_h = _exchange_start([payload(n) for n in late], [True] * len(late), mid_h["token"], "gather_late_start")
    full = {n: _full_from_gathered(g, SHARDED[n]) for n, g in zip(small, got)}
    h0 = jnp.concatenate([full["meta_tokens"], x[0], jnp.zeros((lp - length, d_model), F32)], axis=0)
    z = _rmsnorm_fwd(h0, norm_mix_g, "norm_mix")
    full["w_in"] = _full_from_gathered(_exchange_wait(in_h, z, "gather_in_wait")[0], True)

    splits = (D_RNN, D_RNN, Q_RANK, KV_RANK, QK_ROPE, 2 * d_model)
    offs = [0]
    for s in splits:
        offs.append(offs[-1] + s)
    w_x, w_g, w_q, w_kv, w_kr, w_m = (full["w_in"][:, offs[s]:offs[s + 1]] for s in range(6))
    w_kr = jnp.pad(w_kr, ((0, 0), (0, LANES - QK_ROPE)))
    bg = full["b_gate"].reshape(1, 2 * d_model)
    pv = jnp.concatenate([full["conv_w"], conv_b, b_rec_a, b_rec_i, lru_lambda], axis=0)
    wa_b, wi_b = w_rec_a[0].astype(BF16), w_rec_i[0].astype(BF16)
    g_final = final_norm_g.reshape(1, d_model)

    tgt = jnp.pad(loss_target[0], ((N_META, lp - length), (0, 0)))
    tabs = _rope_tables(lp)

    ux = _mm([(z, w_x)], "nn", "in_x")
    ug = _mm([(z, w_g)], "nn", "in_g")
    uq = _mm([(z, w_q)], "nn", "in_q")
    ukv = _mm([(z, w_kv)], "nn", "in_kv")
    ukr = _mm([(z, w_kr)], "nn", "in_kr")
    um = _mm([(z, w_m)], "nn", "in_m", out_dtype=BF16)
    for n, g in zip(mid, _exchange_wait(mid_h, um, "gather_mid_wait")):
        full[n] = _full_from_gathered(g, SHARDED[n])
    w_uq_pad = jnp.pad(full["w_uq"].reshape(Q_RANK, N_HEADS, QK_NOPE + QK_ROPE),
                       ((0, 0), (0, 0), (0, HEAD_PAD - QK_NOPE - QK_ROPE))).reshape(Q_RANK, N_HEADS * HEAD_PAD)
    w_ukv3 = full["w_ukv"].reshape(KV_RANK, N_HEADS, QK_NOPE + V_DIM)
    w_k_pad = jnp.pad(w_ukv3[:, :, :QK_NOPE], ((0, 0), (0, 0), (0, HEAD_PAD - QK_NOPE))).reshape(
        KV_RANK, N_HEADS * HEAD_PAD)
    w_v = w_ukv3[:, :, QK_NOPE:].reshape(KV_RANK, N_HEADS * V_DIM)
    wb_r, wb_a = full["w_branch"][:D_RNN], full["w_branch"][D_RNN:]
    y_rnn, hs = _rglru_fwd(ux, ug, pv, wa_b, wi_b, t_rnn, "rglru_fwd")
    qn = _rmsnorm_fwd(uq, q_norm_g, "norm_q")
    kvn = _rmsnorm_fwd(ukv, kv_norm_g, "norm_kv")
    qpad = _mm([(qn, w_uq_pad)], "nn", "up_q", out_dtype=BF16)
    kpad = _mm([(kvn, w_k_pad)], "nn", "up_k", out_dtype=BF16)
    vh = _mm([(kvn, w_v)], "nn", "up_v", out_dtype=BF16)
    qh, kh = _rope_fwd(qpad, kpad, ukr, tabs, "rope_fwd")
    oh, lse = _attn_fwd(qh, kh, vh, t_attn, "attn_fwd")
    p_rnn = _mm([(y_rnn, wb_r)], "nn", "branch_rnn", out_dtype=BF16)
    p_att = _mm([(oh, wb_a)], "nn", "branch_att", out_dtype=BF16)
    mixed = _mix_fwd(um, p_rnn, p_att, bg, "mix_fwd")
    h1 = _mm([(mixed, full["w_out"])], "nn", "out_proj", res=h0)
    for n, g in zip(late, _exchange_wait(late_h, h1, "gather_late_wait")):
        full[n] = _full_from_gathered(g, SHARDED[n])
    zf = _rmsnorm_fwd(h1, norm_ffn_g, "norm_ffn")
    gate, up, act = _ffn_in_swiglu(zf, full["w_ffn_in"], "ffn_in")
    h2 = _mm([(act, full["w_ffn_out"])], "nn", "ffn_out", res=h1)
    dh2, dg_final, _, loss_part = _loss_bwd(h2, tgt, g_final, seq, "loss_bwd")

    d_gate, d_up = _ffn_out_bwd_swiglu(dh2, full["w_ffn_out"], gate, up, "d_gate_up")
    dw_ffn_out = _mm_tn(act, dh2, "dw_ffn_out")
    dw_ffn_in = jnp.concatenate([_mm_tn(zf, d_gate, "dw_ffn_gate"), _mm_tn(zf, d_up, "dw_ffn_up")], axis=1)
    blocks = lambda n, g: _blocks_from_full(g, SHARDED[n]).astype(F32 if n in small else BF16)
    sent = {("w_ffn_in", "w_ffn_out"): _exchange_start(
        [blocks("w_ffn_in", dw_ffn_in), blocks("w_ffn_out", dw_ffn_out)], [False] * 2, dg_final, "scatter_ffn_start")}
    d_zf = _mm([(d_gate, full["w_ffn_in"], D_FF, 0), (d_up, full["w_ffn_in"], D_FF, 1)], "nt", "d_zf")
    dh1, dg_ffn = _rmsnorm_bwd(h1, d_zf, norm_ffn_g, "norm_ffn_bwd", res=dh2,
                               dep=sent[("w_ffn_in", "w_ffn_out")]["token"])
    d_mixed = _mm([(dh1, full["w_out"])], "nt", "d_mixed", out_dtype=BF16)
    dw_out = _mm_tn(mixed, dh1, "dw_out")
    d_prnn, d_patt, d_um, dbg = _mix_bwd(um, p_rnn, p_att, d_mixed, bg, "mix_bwd")
    d_yrnn = _mm([(d_prnn, wb_r)], "nt", "d_yrnn")
    d_oh = _mm([(d_patt, wb_a)], "nt", "d_oh", out_dtype=BF16)
    dwb_r = _mm_tn(y_rnn, d_prnn, "dw_branch_rnn")
    dwb_a = _mm_tn(oh, d_patt, "dw_branch_att")
    sent[("w_out", "w_branch")] = _exchange_start(
        [blocks("w_out", dw_out), blocks("w_branch", jnp.concatenate([dwb_r, dwb_a], axis=0))], [False] * 2,
        dg_ffn, "scatter_mix_start")
    dqh, l2row, dlrow = _attn_bwd_dq(qh, kh, vh, d_oh, oh, lse, sent[("w_out", "w_branch")]["token"], t_attn,
                                     "attn_bwd_dq")
    dkh, dvh = _attn_bwd_dkv(qh, kh, vh, d_oh, l2row, dlrow, t_attn, "attn_bwd_dkv")
    dqpad, dkpad, dukr = _rope_bwd(dqh, dkh, tabs, "rope_bwd")
    d_qn = _mm([(dqpad, w_uq_pad)], "nt", "d_qn")
    dw_uq_pad = _mm_tn(qn, dqpad, "dw_uq")
    d_kvn = _mm([(dkpad, w_k_pad), (dvh, w_v)], "nt", "d_kvn")
    dw_k_pad = _mm_tn(kvn, dkpad, "dw_uk")
    dw_v = _mm_tn(kvn, dvh, "dw_uv")
    dw_uq = dw_uq_pad.reshape(Q_RANK, N_HEADS, HEAD_PAD)[:, :, :QK_NOPE + QK_ROPE].reshape(Q_RANK, -1)
    dw_ukv = jnp.concatenate([dw_k_pad.reshape(KV_RANK, N_HEADS, HEAD_PAD)[:, :, :QK_NOPE],
                              dw_v.reshape(KV_RANK, N_HEADS, V_DIM)], axis=2).reshape(KV_RANK, -1)
    sent[("w_uq", "w_ukv")] = _exchange_start([blocks("w_uq", dw_uq), blocks("w_ukv", dw_ukv)], [False] * 2,
                                              dbg, "scatter_attn_start")
    duq, dg_q = _rmsnorm_bwd(uq, d_qn, q_norm_g, "norm_q_bwd", out_dtype=BF16)
    dukv, dg_kv = _rmsnorm_bwd(ukv, d_kvn, kv_norm_g, "norm_kv_bwd", out_dtype=BF16)
    dux, dug, dpv, dwa, dwi = _rglru_bwd(ux, ug, hs, d_yrnn, pv, wa_b, wi_b, sent[("w_uq", "w_ukv")]["token"],
                                         t_rnn, "rglru_bwd")
    grad_rep = dict(
        conv_b=dpv[4:5], w_rec_a=dwa, b_rec_a=dpv[5:6], w_rec_i=dwi, b_rec_i=dpv[6:7], lru_lambda=dpv[7:8],
        q_norm_g=dg_q, kv_norm_g=dg_kv, norm_ffn_g=dg_ffn, final_norm_g=dg_final)
    rep_now = tuple(grad_rep)
    sent[("b_gate", "conv_w") + rep_now] = _exchange_start(
        [blocks("b_gate", dbg.reshape(2, d_model)), blocks("conv_w", dpv[:CONV_WIDTH])]
        + [_as2d(grad_rep[n]).astype(BF16 if n in ("w_rec_a", "w_rec_i") else F32) for n in rep_now],
        [False] * 2 + [True] * len(rep_now), dg_kv, "scatter_small_start")
    d_z = _mm([(dux, w_x), (dug, w_g), (duq, w_q), (dukv, w_kv), (dukr, w_kr), (d_um, w_m)], "nt", "d_z")
    dw_in = jnp.concatenate([
        _mm_tn(z, dux, "dw_in_x"), _mm_tn(z, dug, "dw_in_g"), _mm_tn(z, duq, "dw_in_q"),
        _mm_tn(z, dukv, "dw_in_kv"), _mm_tn(z, dukr, "dw_in_kr")[:, :QK_ROPE], _mm_tn(z, d_um, "dw_in_m")], axis=1)
    last_h = _exchange_start([blocks("w_in", dw_in)], [False], sent[("b_gate", "conv_w") + rep_now]["token"],
                             "scatter_in_start")
    sent[("w_in",)] = last_h
    dh0, dg_mix = _rmsnorm_bwd(h0, d_z, norm_mix_g, "norm_mix_bwd", res=dh1, dep=last_h["token"])
    tail = _exchange([blocks("meta_tokens", dh0[:N_META]), dg_mix], [False, True], "exchange_tail")
    grads, deltas, new_m, new_v = {}, {}, {}, {}

    def update(n, parts, dep):
        g2, d2, m2, v2 = _adamw(_as2d(w[n]), _as2d(m[n]), _as2d(v[n]), parts, dep, "adamw_" + n)
        for store, val in ((grads, g2), (deltas, d2), (new_m, m2), (new_v, v2)):
            store[n] = val.reshape(w[n].shape)

    chain = tail[1][0]
    for n, parts in zip(("meta_tokens", "norm_mix_g"), tail):
        update(n, parts, chain)
    for group, handle in sent.items():
        for n, parts in zip(group, _exchange_wait(handle, chain, "scatter_wait_" + group[0])):
            update(n, parts, chain)
            chain = _as2d(deltas[n])[:SUBLANES, :LANES]

    loss = lax.psum(loss_part[0, 0], MESH_AXES)
    grad_x = dh0[N_META:length][None]
    return (loss, grad_x, *[grads[n] for n in WEIGHTS], *[deltas[n] for n in WEIGHTS],
            *[new_m[n] for n in WEIGHTS], *[new_v[n] for n in WEIGHTS])
```

```python
import functools
import math

import jax
import jax.numpy as jnp
from jax import lax
from jax.experimental import pallas as pl
from jax.experimental.pallas import tpu as pltpu

F32 = jnp.float32
BF16 = jnp.bfloat16

N_DEV = 8
MESH_AXES = ("x", "y", "c")
LANES = 128
SUBLANES = 8
VMEM_LIMIT = 56 * 1024 * 1024

N_META = 16
CHUNK_SHIFT = 6
CHUNK_BIAS = 64 - N_META
EPS = 1e-6
D_RNN = 1280
RNN_BLOCKS = 10
CONV_WIDTH = 4
LRU_C = 8.0
N_HEADS = 8
QK_NOPE = 128
QK_ROPE = 64
V_DIM = 128
HEAD_PAD = 256
Q_RANK = 384
KV_RANK = 256
ROPE_THETA = 10000.0
ATTN_SCALE = 1.0 / math.sqrt(QK_NOPE + QK_ROPE)
NEG = -1e30
TINY = 1e-30
LOG2E = 1.0 / math.log(2.0)
SCALE_LOG2E = ATTN_SCALE * LOG2E
Q_SPLIT = 2
Q_ALIGN = LANES // Q_SPLIT
ROW_GROUP = 32
SPILL = LANES
D_FF = 2816

ADAM_LR = 0.001
ADAM_B1 = 0.9
ADAM_B2 = 0.999
ADAM_EPS = 1e-08
ADAM_WD = 0.01
ADAM_STEP = 10

NN = (((1,), (0,)), ((), ()))
NT = (((1,), (1,)), ((), ()))
TN = (((0,), (0,)), ((), ()))


def _pick(n, cap, base=LANES):
    best = None
    for t in range(base, min(n, cap) + 1, base):
        if n % t == 0:
            best = t
    return best if best is not None else n


def _params(sem=None):
    return pltpu.CompilerParams(dimension_semantics=sem, vmem_limit_bytes=VMEM_LIMIT)


def _sig(x):
    return 0.5 + 0.5 * jnp.tanh(0.5 * x)


def _exchange(srcs, gather, name):
    n = len(srcs)
    out_shape = [jax.ShapeDtypeStruct((N_DEV,) + (s.shape if g else s.shape[1:]), s.dtype)
                 for s, g in zip(srcs, gather)]

    def body(*refs):
        src, dst = refs[:n], refs[n:2 * n]
        send_sems, recv_sems, local_sems = refs[2 * n:]
        x, y, c = lax.axis_index("x"), lax.axis_index("y"), lax.axis_index("c")
        me = 4 * x + 2 * y + c
        local = []
        for t in range(n):
            cp = pltpu.make_async_copy(src[t] if gather[t] else src[t].at[me], dst[t].at[me], local_sems.at[t])
            cp.start()
            local.append(cp)
        sends, recvs = [], []
        for k in range(1, N_DEV):
            px = 1 - x if k & 4 else x
            py = 1 - y if k & 2 else y
            pc = 1 - c if k & 1 else c
            peer = 4 * px + 2 * py + pc
            for t in range(n):
                cp = pltpu.make_async_remote_copy(
                    src_ref=src[t] if gather[t] else src[t].at[peer], dst_ref=dst[t].at[me],
                    send_sem=send_sems.at[t, k - 1], recv_sem=recv_sems.at[t, k - 1],
                    device_id=(px, py, pc), device_id_type=pl.DeviceIdType.MESH)
                cp.start()
                sends.append(cp)
                recvs.append(pltpu.make_async_remote_copy(
                    src_ref=src[t] if gather[t] else src[t].at[peer], dst_ref=dst[t].at[peer],
                    send_sem=send_sems.at[t, k - 1], recv_sem=recv_sems.at[t, k - 1],
                    device_id=(px, py, pc), device_id_type=pl.DeviceIdType.MESH))
        for cp in recvs:
            cp.wait_recv()
        for cp in sends:
            cp.wait_send()
        for cp in local:
            cp.wait()

    any_spec = pl.BlockSpec(memory_space=pl.ANY)
    return pl.pallas_call(
        body, name=name, out_shape=out_shape,
        in_specs=[any_spec] * n, out_specs=[any_spec] * n,
        scratch_shapes=[pltpu.SemaphoreType.DMA((n, N_DEV - 1)), pltpu.SemaphoreType.DMA((n, N_DEV - 1)),
                        pltpu.SemaphoreType.DMA((n,))],
    )(*srcs)


HBM_SPEC = pl.BlockSpec(memory_space=pltpu.HBM)
SEM_SPEC = pl.BlockSpec(memory_space=pltpu.SEMAPHORE)
DATAFLOW = pltpu.SideEffectType.DATAFLOW_SIDE_EFFECTING


def _peers(x, y, c):
    out = []
    for k in range(1, N_DEV):
        px = 1 - x if k & 4 else x
        py = 1 - y if k & 2 else y
        pc = 1 - c if k & 1 else c
        out.append((k, (px, py, pc), 4 * px + 2 * py + pc))
    return out


def _split_copies(src, land, gather, send_sems, recv_sems, local_sems):
    x, y, c = lax.axis_index("x"), lax.axis_index("y"), lax.axis_index("c")
    me = 4 * x + 2 * y + c
    n = len(src)
    local = [pltpu.make_async_copy(src[t] if gather[t] else src[t].at[me], land[t].at[me], local_sems.at[t])
             for t in range(n)]
    sends, recvs = [], []
    for k, pos, peer in _peers(x, y, c):
        for t in range(n):
            mine = src[t] if gather[t] else src[t].at[peer]
            slot = t * (N_DEV - 1) + k - 1
            common = dict(send_sem=send_sems.at[slot], recv_sem=recv_sems.at[slot], device_id=pos,
                          device_id_type=pl.DeviceIdType.MESH)
            sends.append(pltpu.make_async_remote_copy(src_ref=mine, dst_ref=land[t].at[me], **common))
            recvs.append(pltpu.make_async_remote_copy(src_ref=mine, dst_ref=land[t].at[peer], **common))
    return local, sends, recvs


def _exchange_start(srcs, gather, after, name):
    n = len(srcs)
    lands = [lax.empty((N_DEV,) + (s.shape if g else s.shape[1:]), s.dtype) for s, g in zip(srcs, gather)]

    def body(*refs):
        src, land = refs[:n], refs[n:2 * n]
        send_sems, recv_sems, local_sems = refs[2 * n + 1:2 * n + 4]
        local, sends, _ = _split_copies(src, land, gather, send_sems, recv_sems, local_sems)
        for cp in local + sends:
            cp.start()
        refs[-1][...] = jnp.zeros_like(refs[-1])

    hbm = lambda a: pltpu.HBM(a.shape, a.dtype)
    outs = pl.pallas_call(
        body, name=name,
        out_shape=(pltpu.SemaphoreType.DMA((n * (N_DEV - 1),)), pltpu.SemaphoreType.DMA((n * (N_DEV - 1),)),
                   pltpu.SemaphoreType.DMA((n,)), *[hbm(s) for s in srcs], *[hbm(a) for a in lands],
                   jax.ShapeDtypeStruct((SUBLANES, LANES), F32)),
        in_specs=[HBM_SPEC] * (2 * n) + [pl.BlockSpec(memory_space=pl.ANY)],
        out_specs=(SEM_SPEC, SEM_SPEC, SEM_SPEC, *[HBM_SPEC] * (2 * n), pl.BlockSpec(memory_space=pltpu.VMEM)),
        input_output_aliases={t: 3 + t for t in range(2 * n)},
        compiler_params=pltpu.CompilerParams(has_side_effects=DATAFLOW),
    )(*[pltpu.with_memory_space_constraint(a, pltpu.HBM) for a in list(srcs) + lands], after)
    return dict(sems=outs[:3], srcs=outs[3:3 + n], lands=outs[3 + n:3 + 2 * n], token=outs[-1], gather=gather)


def _exchange_wait(handle, after, name):
    srcs, lands, gather = handle["srcs"], handle["lands"], handle["gather"]
    n = len(srcs)

    def body(*refs):
        src, land = refs[:n], refs[n:2 * n]
        send_sems, recv_sems, local_sems = refs[2 * n:2 * n + 3]
        local, sends, recvs = _split_copies(src, land, gather, send_sems, recv_sems, local_sems)
        for cp in sends:
            cp.wait_send()
        for cp in recvs:
            cp.wait_recv()
        for cp in local:
            cp.wait()

    hbm = lambda a: pltpu.HBM(a.shape, a.dtype)
    outs = pl.pallas_call(
        body, name=name, out_shape=(*[hbm(s) for s in srcs], *[hbm(a) for a in lands]),
        in_specs=[HBM_SPEC] * (2 * n) + [SEM_SPEC] * 3 + [pl.BlockSpec(memory_space=pl.ANY)],
        out_specs=[HBM_SPEC] * (2 * n), input_output_aliases={t: t for t in range(2 * n)},
        compiler_params=pltpu.CompilerParams(has_side_effects=DATAFLOW),
    )(*srcs, *lands, *handle["sems"], after)
    return outs[n:]


def _mm(pairs, mode, name, res=None, out_dtype=F32):
    pairs = [p if len(p) == 4 else (p[0], p[1], p[0].shape[1], 0) for p in pairs]
    m = pairs[0][0].shape[0]
    n = pairs[0][1].shape[1] if mode == "nn" else pairs[0][1].shape[0]
    tm, tn = _pick(m, 640), _pick(n, 1408)
    np_ = len(pairs)
    dims = NN if mode == "nn" else NT

    def body(*refs):
        acc = None
        for s in range(np_):
            d = lax.dot_general(refs[2 * s][...].astype(BF16), refs[2 * s + 1][...].astype(BF16), dims,
                                preferred_element_type=F32)
            acc = d if acc is None else acc + d
        if res is not None:
            acc = acc + refs[2 * np_][...]
        refs[-1][...] = acc.astype(out_dtype)

    in_specs, args = [], []
    for a, b, kt, kb in pairs:
        in_specs.append(pl.BlockSpec((tm, kt), lambda i, j: (i, 0)))
        if mode == "nn":
            in_specs.append(pl.BlockSpec((kt, tn), lambda i, j, kb=kb: (kb, j)))
        else:
            in_specs.append(pl.BlockSpec((tn, kt), lambda i, j, kb=kb: (j, kb)))
        args += [a, b]
    if res is not None:
        in_specs.append(pl.BlockSpec((tm, tn), lambda i, j: (i, j)))
        args.append(res)
    return pl.pallas_call(
        body, name=name, grid=(m // tm, n // tn), in_specs=in_specs,
        out_specs=pl.BlockSpec((tm, tn), lambda i, j: (i, j)),
        out_shape=jax.ShapeDtypeStruct((m, n), out_dtype),
        compiler_params=_params(("parallel", "parallel")),
    )(*args)


def _mm_tn(a, b, name):
    m, k = a.shape
    n = b.shape[1]
    tm, tk, tn = _pick(m, 1664), _pick(k, 1408), _pick(n, 1408)

    def body(a_ref, b_ref, o_ref):
        @pl.when(pl.program_id(2) == 0)
        def _():
            o_ref[...] = jnp.zeros_like(o_ref)

        o_ref[...] += lax.dot_general(a_ref[...].astype(BF16), b_ref[...].astype(BF16), TN,
                                      preferred_element_type=F32)

    return pl.pallas_call(
        body, name=name, grid=(k // tk, n // tn, m // tm),
        in_specs=[pl.BlockSpec((tm, tk), lambda i, j, r: (r, i)), pl.BlockSpec((tm, tn), lambda i, j, r: (r, j))],
        out_specs=pl.BlockSpec((tk, tn), lambda i, j, r: (i, j)),
        out_shape=jax.ShapeDtypeStruct((k, n), F32),
        compiler_params=_params(("parallel", "parallel", "arbitrary")),
    )(a, b)


ROW_TILE_BYTES = 6 * 1024 * 1024


def _row_tile(rows, row_in, row_out):
    per_row = sum((r[1] * r[0].dtype.itemsize) if isinstance(r, tuple) else (r.shape[1] * r.dtype.itemsize)
                  for r in row_in)
    per_row += sum(w * jnp.dtype(dt).itemsize for w, dt in row_out)
    return _pick(rows, min(640, max(LANES, ROW_TILE_BYTES // per_row)))


def _rowcall(body, name, rows, row_in, full_in, row_out, acc_out=()):
    tr = _row_tile(rows, row_in, row_out)
    n_steps = rows // tr
    in_specs, args = [], []
    for r in row_in:
        arr, w, cb = r if isinstance(r, tuple) else (r, r.shape[1], 0)
        in_specs.append(pl.BlockSpec((tr, w), lambda i, cb=cb: (i, cb)))
        args.append(arr)
    for f in full_in:
        in_specs.append(pl.BlockSpec(f.shape, lambda i, nd=f.ndim: (0,) * nd))
        args.append(f)
    out_specs = [pl.BlockSpec((tr, w), lambda i: (i, 0)) for w, _ in row_out]
    out_shape = [jax.ShapeDtypeStruct((rows, w), dt) for w, dt in row_out]
    for shp, dt in acc_out:
        out_specs.append(pl.BlockSpec(shp, lambda i, nd=len(shp): (0,) * nd))
        out_shape.append(jax.ShapeDtypeStruct(shp, dt))

    def wrapped(*refs):
        body(pl.program_id(0), n_steps, *refs)

    return pl.pallas_call(
        wrapped, name=name, grid=(n_steps,), in_specs=in_specs, out_specs=out_specs, out_shape=out_shape,
        compiler_params=_params(("arbitrary",) if acc_out else ("parallel",)),
    )(*args)


def _rmsnorm_fwd(x, g, name):
    rows, w = x.shape

    def body(i, n, x_ref, g_ref, o_ref):
        xv = x_ref[...]
        r = lax.rsqrt(jnp.mean(xv * xv, axis=-1, keepdims=True) + EPS)
        o_ref[...] = (xv * r * g_ref[...]).astype(BF16)

    return _rowcall(body, name, rows,[x], [g], [(w, BF16)])[0]


def _rmsnorm_bwd_math(xv, dy, g):
    w = xv.shape[-1]
    r = lax.rsqrt(jnp.mean(xv * xv, axis=-1, keepdims=True) + EPS)
    t = dy * g
    dx = r * t - xv * (r * r * r * (jnp.sum(t * xv, axis=-1, keepdims=True) / w))
    dg = jnp.sum(dy * xv * r, axis=0, keepdims=True)
    return dx, dg


def _rmsnorm_bwd(x, dy, g, name, res=None, out_dtype=F32, dep=None):
    rows, w = x.shape

    def body(i, n, *refs):
        x_ref, dy_ref = refs[0], refs[1]
        g_ref, dx_ref, dg_ref = refs[-3], refs[-2], refs[-1]
        dx, dg = _rmsnorm_bwd_math(x_ref[...], dy_ref[...], g_ref[...])
        if res is not None:
            dx = dx + refs[2][...]
        dx_ref[...] = dx.astype(out_dtype)

        @pl.when(i == 0)
        def _():
            dg_ref[...] = jnp.zeros_like(dg_ref)

        dg_ref[...] += dg

    row_in = [x, dy] + ([res] if res is not None else [])
    return _rowcall(body, name, rows, row_in, ([dep] if dep is not None else []) + [g], [(w, out_dtype)],
                    [((1, w), F32)])


def _loss_bwd(h2, tgt, g, seq, name):
    rows, w = h2.shape
    tr = _row_tile(rows, [h2, tgt], [(w, F32)])

    def body(i, n, h_ref, t_ref, g_ref, dh_ref, dg_ref, lcol_ref, loss_ref):
        hv, gv = h_ref[...], g_ref[...]
        row = i * tr + lax.broadcasted_iota(jnp.int32, (tr, w), 0)
        valid = jnp.logical_and(row >= N_META, row < N_META + seq)
        r = lax.rsqrt(jnp.mean(hv * hv, axis=-1, keepdims=True) + EPS)
        err = jnp.where(valid, hv * r * gv - t_ref[...], 0.0)
        dx, dg = _rmsnorm_bwd_math(hv, err * (1.0 / w), gv)
        dh_ref[...] = dx

        @pl.when(i == 0)
        def _():
            dg_ref[...] = jnp.zeros_like(dg_ref)
            lcol_ref[...] = jnp.zeros_like(lcol_ref)

        dg_ref[...] += dg
        lcol_ref[...] += jnp.sum(err * err, axis=0, keepdims=True)

        @pl.when(i == n - 1)
        def _():
            total = jnp.sum(lcol_ref[...], axis=1, keepdims=True) * (0.5 / w)
            loss_ref[...] = jnp.broadcast_to(total, loss_ref.shape)

    return _rowcall(body, name, rows, [h2, tgt], [g], [(w, F32)],
                    [((1, w), F32), ((1, w), F32), ((1, LANES), F32)])


def _mix_out(um, p_rnn, p_att, bg, w_out, res, name):
    rows, d = p_rnn.shape
    tr = _pick(rows, 640)

    def body(u0_ref, u1_ref, pr_ref, pa_ref, bg_ref, w_ref, r_ref, mix_ref, o_ref):
        g0 = _sig(u0_ref[...].astype(F32) + bg_ref[:, :d])
        g1 = _sig(u1_ref[...].astype(F32) + bg_ref[:, d:])
        mixed = (g0 * pr_ref[...].astype(F32) + g1 * pa_ref[...].astype(F32)).astype(BF16)
        mix_ref[...] = mixed
        o_ref[...] = r_ref[...] + jnp.dot(mixed, w_ref[...], preferred_element_type=F32)

    row = pl.BlockSpec((tr, d), lambda i: (i, 0))
    whole = lambda a: pl.BlockSpec(a.shape, lambda i: (0, 0))
    return pl.pallas_call(
        body, name=name, grid=(rows // tr,),
        in_specs=[row, pl.BlockSpec((tr, d), lambda i: (i, 1)), row, row, whole(bg), whole(w_out), row],
        out_specs=[row, row],
        out_shape=[jax.ShapeDtypeStruct((rows, d), BF16), jax.ShapeDtypeStruct((rows, d), F32)],
        compiler_params=_params(("parallel",)),
    )(um, um, p_rnn, p_att, bg, w_out, res)


def _mix_bwd(um, p_rnn, p_att, dmix, bg, name):
    rows, d = p_rnn.shape

    def body(i, n, u0_ref, u1_ref, pr_ref, pa_ref, dm_ref, bg_ref, dpr_ref, dpa_ref, dum_ref, dbg_ref):
        g0 = _sig(u0_ref[...].astype(F32) + bg_ref[:, :d])
        g1 = _sig(u1_ref[...].astype(F32) + bg_ref[:, d:])
        dm = dm_ref[...].astype(F32)
        dpr_ref[...] = (dm * g0).astype(BF16)
        dpa_ref[...] = (dm * g1).astype(BF16)
        du0 = dm * pr_ref[...].astype(F32) * g0 * (1.0 - g0)
        du1 = dm * pa_ref[...].astype(F32) * g1 * (1.0 - g1)
        dum_ref[:, :d] = du0.astype(BF16)
        dum_ref[:, d:] = du1.astype(BF16)

        @pl.when(i == 0)
        def _():
            dbg_ref[...] = jnp.zeros_like(dbg_ref)

        dbg_ref[:, :d] += jnp.sum(du0, axis=0, keepdims=True)
        dbg_ref[:, d:] += jnp.sum(du1, axis=0, keepdims=True)

    return _rowcall(body, name, rows,[(um, d, 0), (um, d, 1), p_rnn, p_att, dmix], [bg],
                    [(d, BF16), (d, BF16), (2 * d, BF16)], [((1, 2 * d), F32)])


def _ffn_in_swiglu(zf, w, name):
    m, k = zf.shape
    f = w.shape[1] // 2
    tm, tn = _pick(m, 640), _pick(f, 1408)
    nb = f // tn

    def body(a_ref, bg_ref, bu_ref, g_ref, u_ref, act_ref):
        a = a_ref[...]
        gate = jnp.dot(a, bg_ref[...], preferred_element_type=F32)
        up = jnp.dot(a, bu_ref[...], preferred_element_type=F32)
        g_ref[...] = gate.astype(BF16)
        u_ref[...] = up.astype(BF16)
        act_ref[...] = (gate * _sig(gate) * up).astype(BF16)

    tile = pl.BlockSpec((tm, tn), lambda i, j: (i, j))
    return pl.pallas_call(
        body, name=name, grid=(m // tm, nb),
        in_specs=[pl.BlockSpec((tm, k), lambda i, j: (i, 0)), pl.BlockSpec((k, tn), lambda i, j: (0, j)),
                  pl.BlockSpec((k, tn), lambda i, j: (0, j + nb))],
        out_specs=[tile] * 3, out_shape=[jax.ShapeDtypeStruct((m, f), BF16)] * 3,
        compiler_params=_params(("parallel", "parallel")),
    )(zf, w, w)


def _ffn_out_bwd_swiglu(dh, w_out, gate, up, name):
    m, k = dh.shape
    f = w_out.shape[0]
    tm, tn = _pick(m, 640), _pick(f, 1408)

    def body(a_ref, b_ref, g_ref, u_ref, dg_ref, du_ref):
        da = lax.dot_general(a_ref[...].astype(BF16), b_ref[...], NT, preferred_element_type=F32)
        gv = g_ref[...].astype(F32)
        sg = _sig(gv)
        dg_ref[...] = (da * u_ref[...].astype(F32) * (sg * (1.0 + gv * (1.0 - sg)))).astype(BF16)
        du_ref[...] = (da * gv * sg).astype(BF16)

    tile = pl.BlockSpec((tm, tn), lambda i, j: (i, j))
    return pl.pallas_call(
        body, name=name, grid=(m // tm, f // tn),
        in_specs=[pl.BlockSpec((tm, k), lambda i, j: (i, 0)), pl.BlockSpec((tn, k), lambda i, j: (j, 0)), tile, tile],
        out_specs=[tile] * 2, out_shape=[jax.ShapeDtypeStruct((m, f), BF16)] * 2,
        compiler_params=_params(("parallel", "parallel")),
    )(dh, w_out, gate, up)


def _rope_tables(lp):
    idx = jnp.arange(lp, dtype=jnp.int32).astype(F32)
    inv_freq = ROPE_THETA ** (-jnp.arange(0, QK_ROPE, 2, dtype=F32) / QK_ROPE)
    ang = idx[:, None] * inv_freq[None, :]
    cos, sin = jnp.cos(ang), jnp.sin(ang)
    half = QK_ROPE // 2
    z = lambda wdt: jnp.zeros((lp, wdt), F32)
    tc = jnp.concatenate([cos, cos, z(LANES - 2 * half)], axis=1)
    ts1 = jnp.concatenate([-sin, z(LANES - half)], axis=1)
    ts2 = jnp.concatenate([z(half), sin, z(LANES - 2 * half)], axis=1)
    return tc, ts1, ts2


def _rope(xv, tc, ts1, ts2):
    half = QK_ROPE // 2
    return xv * tc + pltpu.roll(xv, LANES - half, 1) * ts1 + pltpu.roll(xv, half, 1) * ts2


def _rope_t(dv, tc, ts1, ts2):
    half = QK_ROPE // 2
    return dv * tc + pltpu.roll(dv * ts1, half, 1) + pltpu.roll(dv * ts2, LANES - half, 1)


def _up_rope(xn, w_pad, tabs, name, ukr=None):
    rows, k = xn.shape
    n = w_pad.shape[1]
    tr = _pick(rows, 640)

    def body(*refs):
        x_ref, w_ref, c_ref, s1_ref, s2_ref = refs[:5]
        o_ref = refs[-1]
        tc, ts1, ts2 = c_ref[...], s1_ref[...], s2_ref[...]
        y = jnp.dot(x_ref[...], w_ref[...], preferred_element_type=F32)
        kr = None if ukr is None else _rope(refs[5][...], tc, ts1, ts2).astype(BF16)
        for h in range(N_HEADS):
            lo, mid, hi = h * HEAD_PAD, h * HEAD_PAD + QK_NOPE, (h + 1) * HEAD_PAD
            o_ref[:, lo:mid] = y[:, lo:mid].astype(BF16)
            o_ref[:, mid:hi] = _rope(y[:, mid:hi], tc, ts1, ts2).astype(BF16) if ukr is None else kr

    row = lambda wdt: pl.BlockSpec((tr, wdt), lambda i: (i, 0))
    in_specs = [row(k), pl.BlockSpec((k, n), lambda i: (0, 0)), row(LANES), row(LANES), row(LANES)]
    args = [xn, w_pad, *tabs]
    if ukr is not None:
        in_specs.append(row(LANES))
        args.append(ukr)
    return pl.pallas_call(
        body, name=name, grid=(rows // tr,), in_specs=in_specs, out_specs=row(n),
        out_shape=jax.ShapeDtypeStruct((rows, n), BF16), compiler_params=_params(("parallel",)),
    )(*args)


def _rope_bwd(dq, dk, tabs, name):
    rows, w = dq.shape

    def body(i, n, dq_ref, dk_ref, c_ref, s1_ref, s2_ref, qo_ref, ko_ref, ro_ref):
        tc, ts1, ts2 = c_ref[...], s1_ref[...], s2_ref[...]
        dkr = None
        for h in range(N_HEADS):
            lo, mid, hi = h * HEAD_PAD, h * HEAD_PAD + QK_NOPE, (h + 1) * HEAD_PAD
            qo_ref[:, lo:mid] = dq_ref[:, lo:mid].astype(BF16)
            qo_ref[:, mid:hi] = _rope_t(dq_ref[:, mid:hi].astype(F32), tc, ts1, ts2).astype(BF16)
            ko_ref[:, lo:mid] = dk_ref[:, lo:mid].astype(BF16)
            ko_ref[:, mid:hi] = jnp.zeros((ko_ref.shape[0], hi - mid), BF16)
            part = dk_ref[:, mid:hi].astype(F32)
            dkr = part if dkr is None else dkr + part
        ro_ref[...] = _rope_t(dkr, tc, ts1, ts2).astype(BF16)

    return _rowcall(body, name, rows,[dq, dk, *tabs], [],
                    [(w, BF16), (w, BF16), (LANES, BF16)])


def _visible(q0, k0, tq, tk):
    qrow = q0 + lax.broadcasted_iota(jnp.int32, (tq, tk), 0)
    kcol = k0 + lax.broadcasted_iota(jnp.int32, (tq, tk), 1)
    return ((kcol + CHUNK_BIAS) >> CHUNK_SHIFT) <= ((qrow + CHUNK_BIAS) >> CHUNK_SHIFT)


def _visible_t(k0, q0, tk, tq):
    krow = k0 + lax.broadcasted_iota(jnp.int32, (tk, tq), 0)
    qcol = q0 + lax.broadcasted_iota(jnp.int32, (tk, tq), 1)
    return ((krow + CHUNK_BIAS) >> CHUNK_SHIFT) <= ((qcol + CHUNK_BIAS) >> CHUNK_SHIFT)


def _lanes(v, width):
    return jnp.tile(v, (1, width // LANES))


def _pipelined_chunks(n_full, scores, absorb):
    scores(0, 0)

    def pair(jj, carry):
        a = 2 * jj
        scores(a + 1, 1)
        absorb(a, 0, False)
        scores(a + 2, 0)
        absorb(a + 1, 1, False)
        return carry

    lax.fori_loop(0, n_full // 2, pair, 0)

    @pl.when(n_full % 2 == 0)
    def _():
        absorb(n_full, 0, True)

    @pl.when(n_full % 2 == 1)
    def _():
        scores(n_full, 1)
        absorb(n_full - 1, 0, False)
        absorb(n_full, 1, True)


def _attn_fwd(q, k, v, t, name):
    lp = q.shape[0]
    nt = lp // t

    def body(q_ref, k_ref, v_ref, o_ref, lse_ref, m_s, l_s, acc_s, a_s, sa_s, sb_s, p_s):
        i = pl.program_id(1)
        m_s[...] = jnp.full(m_s.shape, NEG, F32)
        l_s[...] = jnp.zeros(l_s.shape, F32)
        acc_s[...] = jnp.zeros(acc_s.shape, F32)

        s_bufs = (sa_s, sb_s)

        def scores(j, slot):
            r0 = pl.multiple_of(j * t, t)
            s_bufs[slot][...] = lax.dot_general(q_ref[...], k_ref[pl.ds(r0, t), :], NT,
                                                preferred_element_type=F32)

        def absorb(j, slot, masked):
            for r in range(0, t, ROW_GROUP):
                rows = slice(r, r + ROW_GROUP)
                s = s_bufs[slot][rows, :]
                if masked:
                    s = jnp.where(_visible(i * t + r, i * t, ROW_GROUP, t), s, NEG)
                m_prev = m_s[rows, :]
                m_new = jnp.maximum(m_prev, jnp.max(s, axis=1, keepdims=True))
                alpha = jnp.exp2((m_prev - m_new) * SCALE_LOG2E)
                p = jnp.exp2((s - _lanes(m_new, t)) * SCALE_LOG2E)
                l_s[rows, :] = alpha * l_s[rows, :] + jnp.sum(p, axis=1, keepdims=True)
                m_s[rows, :] = m_new
                a_s[rows, :] = alpha
                p_s[rows, :] = p.astype(BF16)
            r0 = pl.multiple_of(j * t, t)
            acc_s[...] = a_s[...] * acc_s[...] + jnp.dot(p_s[...], v_ref[pl.ds(r0, t), :],
                                                         preferred_element_type=F32)

        r1 = pl.multiple_of(jnp.minimum(i + 1, nt - 1) * t, t)
        rows = slice(t - SPILL, t)
        s = lax.dot_general(q_ref[rows, :], k_ref[pl.ds(r1, SPILL), :], NT, preferred_element_type=F32)
        seen = jnp.logical_and(_visible(i * t + t - SPILL, (i + 1) * t, SPILL, SPILL), i + 1 < nt)
        s = jnp.where(seen, s, NEG)
        m_new = jnp.max(s, axis=1, keepdims=True)
        p = jnp.exp2((s - m_new) * SCALE_LOG2E)
        l_s[rows, :] = jnp.broadcast_to(jnp.sum(p, axis=1, keepdims=True), (SPILL, LANES))
        acc_s[rows, :] = jnp.dot(p.astype(BF16), v_ref[pl.ds(r1, SPILL), :], preferred_element_type=F32)
        m_s[rows, :] = jnp.broadcast_to(m_new, (SPILL, LANES))

        _pipelined_chunks(i, scores, absorb)
        o_ref[...] = (acc_s[...] / l_s[...]).astype(BF16)
        lse_ref[...] = m_s[...] * ATTN_SCALE + jnp.log(l_s[...])

    return pl.pallas_call(
        body, name=name, grid=(N_HEADS, nt),
        in_specs=[pl.BlockSpec((t, HEAD_PAD), lambda h, i: (i, h)),
                  pl.BlockSpec((lp, HEAD_PAD), lambda h, i: (0, h)),
                  pl.BlockSpec((lp, V_DIM), lambda h, i: (0, h))],
        out_specs=[pl.BlockSpec((t, V_DIM), lambda h, i: (i, h)),
                   pl.BlockSpec((None, t, LANES), lambda h, i: (h, i, 0))],
        out_shape=[jax.ShapeDtypeStruct((lp, N_HEADS * V_DIM), BF16),
                   jax.ShapeDtypeStruct((N_HEADS, lp, LANES), F32)],
        scratch_shapes=[pltpu.VMEM((t, LANES), F32), pltpu.VMEM((t, LANES), F32), pltpu.VMEM((t, V_DIM), F32),
                        pltpu.VMEM((t, LANES), F32), pltpu.VMEM((t, t), F32), pltpu.VMEM((t, t), F32),
                        pltpu.VMEM((t, t), BF16)],
        compiler_params=_params(("parallel", "arbitrary")),
    )(q, k, v)


def _attn_bwd_dq(q, k, v, do, o, lse, dep, t, name):
    lp = q.shape[0]
    nt = lp // t

    def body(q_ref, k_ref, v_ref, do_ref, o_ref, lse_ref, dep_ref, dq_ref, l2row_ref, dlrow_ref,
             acc_s, l2_s, dl_s, sa_s, sb_s, da_s, db_s, ds_s):
        i = pl.program_id(1)
        delta = jnp.sum(do_ref[...].astype(F32) * o_ref[...].astype(F32), axis=1, keepdims=True)
        dl_s[...] = jnp.broadcast_to(delta, dl_s.shape)
        l2_s[...] = lse_ref[...] * LOG2E
        l2row_ref[...] = l2_s[...].T[0:SUBLANES, :]
        dlrow_ref[...] = dl_s[...].T[0:SUBLANES, :]
        acc_s[...] = jnp.zeros(acc_s.shape, F32)
        s_bufs, d_bufs = (sa_s, sb_s), (da_s, db_s)

        def dscores(s, dp, rows, width):
            p = jnp.exp2(s * SCALE_LOG2E - _lanes(l2_s[rows, :], width))
            return (p * (dp - _lanes(dl_s[rows, :], width))).astype(BF16)

        r1 = pl.multiple_of(jnp.minimum(i + 1, nt - 1) * t, t)
        rows = slice(t - SPILL, t)
        ks, vs = k_ref[pl.ds(r1, SPILL), :], v_ref[pl.ds(r1, SPILL), :]
        s = lax.dot_general(q_ref[rows, :], ks, NT, preferred_element_type=F32)
        seen = jnp.logical_and(_visible(i * t + t - SPILL, (i + 1) * t, SPILL, SPILL), i + 1 < nt)
        s = jnp.where(seen, s, NEG)
        dp = lax.dot_general(do_ref[rows, :], vs, NT, preferred_element_type=F32)
        acc_s[rows, :] = jnp.dot(dscores(s, dp, rows, SPILL), ks, preferred_element_type=F32)

        def scores(j, slot):
            r0 = pl.multiple_of(j * t, t)
            s_bufs[slot][...] = lax.dot_general(q_ref[...], k_ref[pl.ds(r0, t), :], NT,
                                                preferred_element_type=F32)
            d_bufs[slot][...] = lax.dot_general(do_ref[...], v_ref[pl.ds(r0, t), :], NT,
                                                preferred_element_type=F32)

        def absorb(j, slot, masked):
            for r in range(0, t, ROW_GROUP):
                rows = slice(r, r + ROW_GROUP)
                s = s_bufs[slot][rows, :]
                if masked:
                    s = jnp.where(_visible(i * t + r, i * t, ROW_GROUP, t), s, NEG)
                ds_s[rows, :] = dscores(s, d_bufs[slot][rows, :], rows, t)
            r0 = pl.multiple_of(j * t, t)
            acc_s[...] += jnp.dot(ds_s[...], k_ref[pl.ds(r0, t), :], preferred_element_type=F32)

        _pipelined_chunks(i, scores, absorb)
        dq_ref[...] = (acc_s[...] * ATTN_SCALE).astype(BF16)

    stat_row = pl.BlockSpec((None, None, SUBLANES, t), lambda h, i: (h, i, 0, 0))
    return pl.pallas_call(
        body, name=name, grid=(N_HEADS, nt),
        in_specs=[pl.BlockSpec((t, HEAD_PAD), lambda h, i: (i, h)),
                  pl.BlockSpec((lp, HEAD_PAD), lambda h, i: (0, h)),
                  pl.BlockSpec((lp, V_DIM), lambda h, i: (0, h)),
                  pl.BlockSpec((t, V_DIM), lambda h, i: (i, h)),
                  pl.BlockSpec((t, V_DIM), lambda h, i: (i, h)),
                  pl.BlockSpec((None, t, LANES), lambda h, i: (h, i, 0)),
                  pl.BlockSpec(dep.shape, lambda h, i: (0, 0))],
        out_specs=[pl.BlockSpec((t, HEAD_PAD), lambda h, i: (i, h)), stat_row, stat_row],
        out_shape=[jax.ShapeDtypeStruct((lp, N_HEADS * HEAD_PAD), BF16),
                   jax.ShapeDtypeStruct((N_HEADS, nt, SUBLANES, t), F32),
                   jax.ShapeDtypeStruct((N_HEADS, nt, SUBLANES, t), F32)],
        scratch_shapes=[pltpu.VMEM((t, HEAD_PAD), F32), pltpu.VMEM((t, LANES), F32), pltpu.VMEM((t, LANES), F32),
                        pltpu.VMEM((t, t), F32), pltpu.VMEM((t, t), F32), pltpu.VMEM((t, t), F32),
                        pltpu.VMEM((t, t), F32), pltpu.VMEM((t, t), BF16)],
        compiler_params=_params(("parallel", "arbitrary")),
    )(q, k, v, do, o, lse, dep)


def _attn_bwd_dkv(q, k, v, do, l2row, dlrow, t, name):
    lp = q.shape[0]
    nt = lp // t

    def body(q_ref, k_ref, v_ref, do_ref, l2_ref, dl_ref, dk_ref, dv_ref,
             dk_s, dv_s, sa_s, sb_s, da_s, db_s, p_s, ds_s):
        j = pl.program_id(1)
        dk_s[...] = jnp.zeros(dk_s.shape, F32)
        dv_s[...] = jnp.zeros(dv_s.shape, F32)
        s_bufs, d_bufs = (sa_s, sb_s), (da_s, db_s)

        def weights(st, dpt, l2r, dlr):
            pt = jnp.exp2(st * SCALE_LOG2E - l2r)
            return pt.astype(BF16), (pt * (dpt - dlr)).astype(BF16)

        prev = jnp.maximum(j - 1, 0)
        q0 = pl.multiple_of(prev * t + t - SPILL, SPILL)
        rows = slice(0, SPILL)
        qs, dos = q_ref[pl.ds(q0, SPILL), :], do_ref[pl.ds(q0, SPILL), :]
        st = lax.dot_general(k_ref[rows, :], qs, NT, preferred_element_type=F32)
        seen = jnp.logical_and(_visible_t(j * t, j * t - SPILL, SPILL, SPILL), j > 0)
        st = jnp.where(seen, st, NEG)
        dpt = lax.dot_general(v_ref[rows, :], dos, NT, preferred_element_type=F32)
        pt, dst = weights(st, dpt, l2_ref[prev, 0:1, t - SPILL:], dl_ref[prev, 0:1, t - SPILL:])
        dv_s[rows, :] = jnp.dot(pt, dos, preferred_element_type=F32)
        dk_s[rows, :] = jnp.dot(dst, qs, preferred_element_type=F32)

        def scores(c, slot):
            r0 = pl.multiple_of((nt - 1 - c) * t, t)
            s_bufs[slot][...] = lax.dot_general(k_ref[...], q_ref[pl.ds(r0, t), :], NT,
                                                preferred_element_type=F32)
            d_bufs[slot][...] = lax.dot_general(v_ref[...], do_ref[pl.ds(r0, t), :], NT,
                                                preferred_element_type=F32)

        def absorb(c, slot, masked):
            i = nt - 1 - c
            l2r, dlr = l2_ref[i, 0:1, :], dl_ref[i, 0:1, :]
            for r in range(0, t, ROW_GROUP):
                rows = slice(r, r + ROW_GROUP)
                st = s_bufs[slot][rows, :]
                if masked:
                    st = jnp.where(_visible_t(j * t + r, j * t, ROW_GROUP, t), st, NEG)
                p_s[rows, :], ds_s[rows, :] = weights(st, d_bufs[slot][rows, :], l2r, dlr)
            r0 = pl.multiple_of(i * t, t)
            dv_s[...] += jnp.dot(p_s[...], do_ref[pl.ds(r0, t), :], preferred_element_type=F32)
            dk_s[...] += jnp.dot(ds_s[...], q_ref[pl.ds(r0, t), :], preferred_element_type=F32)

        _pipelined_chunks(nt - 1 - j, scores, absorb)
        dk_ref[...] = (dk_s[...] * ATTN_SCALE).astype(BF16)
        dv_ref[...] = dv_s[...].astype(BF16)

    stat_rows = pl.BlockSpec((None, nt, SUBLANES, t), lambda h, j: (h, 0, 0, 0))
    return pl.pallas_call(
        body, name=name, grid=(N_HEADS, nt),
        in_specs=[pl.BlockSpec((lp, HEAD_PAD), lambda h, j: (0, h)),
                  pl.BlockSpec((t, HEAD_PAD), lambda h, j: (j, h)),
                  pl.BlockSpec((t, V_DIM), lambda h, j: (j, h)),
                  pl.BlockSpec((lp, V_DIM), lambda h, j: (0, h)),
                  stat_rows, stat_rows],
        out_specs=[pl.BlockSpec((t, HEAD_PAD), lambda h, j: (j, h)),
                   pl.BlockSpec((t, V_DIM), lambda h, j: (j, h))],
        out_shape=[jax.ShapeDtypeStruct((lp, N_HEADS * HEAD_PAD), BF16),
                   jax.ShapeDtypeStruct((lp, N_HEADS * V_DIM), BF16)],
        scratch_shapes=[pltpu.VMEM((t, HEAD_PAD), F32), pltpu.VMEM((t, V_DIM), F32),
                        pltpu.VMEM((t, t), F32), pltpu.VMEM((t, t), F32), pltpu.VMEM((t, t), F32),
                        pltpu.VMEM((t, t), F32), pltpu.VMEM((t, t), BF16), pltpu.VMEM((t, t), BF16)],
        compiler_params=_params(("parallel", "arbitrary")),
    )(q, k, v, do, l2row, dlrow)


def _shift_down(cur, prev8, k):
    r = pltpu.roll(cur, k, 0)
    row8 = lax.broadcasted_iota(jnp.int32, prev8.shape, 0)
    first = jnp.where(row8 < k, pltpu.roll(prev8, k, 0), r[0:SUBLANES])
    return jnp.concatenate([first, r[SUBLANES:]], axis=0)


def _shift_up(cur, next8, k):
    t = cur.shape[0]
    r = pltpu.roll(cur, t - k, 0)
    row8 = lax.broadcasted_iota(jnp.int32, next8.shape, 0)
    last = jnp.where(row8 >= SUBLANES - k, pltpu.roll(next8, SUBLANES - k, 0), r[t - SUBLANES:])
    return jnp.concatenate([r[:t - SUBLANES], last], axis=0)


def _scan_rows(a, b, edge, reverse):
    t, d = a.shape
    groups = t // SUBLANES
    a3, b3 = a.reshape(groups, SUBLANES, d), b.reshape(groups, SUBLANES, d)
    sub = lax.broadcasted_iota(jnp.int32, a3.shape, 1)
    s = 1
    while s < SUBLANES:
        keep = sub < SUBLANES - s if reverse else sub >= s
        shift = SUBLANES - s if reverse else s
        a_sh = jnp.where(keep, pltpu.roll(a3, shift, 1), 1.0)
        b_sh = jnp.where(keep, pltpu.roll(b3, shift, 1), 0.0)
        b3 = a3 * b_sh + b3
        a3 = a3 * a_sh
        s *= 2
    out = [None] * groups
    for k in (range(groups - 1, -1, -1) if reverse else range(groups)):
        out[k] = b3[k] + a3[k] * edge
        edge = out[k][0:1, :] if reverse else out[k][SUBLANES - 1:SUBLANES, :]
    return jnp.concatenate(out, axis=0)


def _sqrt_one_minus_exp2x(x):
    th = jnp.tanh(x)
    m2 = (-2.0 * th) / (1.0 - th)
    return m2 * lax.rsqrt(jnp.maximum(m2, TINY))


def _log_sigmoid(x):
    return jnp.minimum(x, 0.0) - jnp.log(1.0 + jnp.exp(-jnp.abs(x)))


GELU_C = math.sqrt(2.0 / math.pi)
GELU_K = 0.044715


def _gelu(x):
    th = jnp.tanh(GELU_C * (x + GELU_K * x * x * x))
    return 0.5 * x * (1.0 + th), th


def _block_mm(xb, w_ref, dims):
    rb = D_RNN // RNN_BLOCKS
    return jnp.concatenate(
        [lax.dot_general(xb[:, h * rb:(h + 1) * rb], w_ref[h], dims, preferred_element_type=F32)
         for h in range(RNN_BLOCKS)], axis=1)


def _rglru_gates(ux, prev8, pv_ref, wa_ref, wi_ref):
    shifted = [ux] + [_shift_down(ux, prev8, k) for k in range(1, CONV_WIDTH)]
    xc = pv_ref[4:5, :] + pv_ref[3:4, :] * ux
    for k in range(1, CONV_WIDTH):
        xc = xc + pv_ref[3 - k:4 - k, :] * shifted[k]
    xcb = xc.astype(BF16)
    r_g = _sig(_block_mm(xcb, wa_ref, NN) + pv_ref[5:6, :])
    i_g = _sig(_block_mm(xcb, wi_ref, NN) + pv_ref[6:7, :])
    log_a = LRU_C * r_g * _log_sigmoid(pv_ref[7:8, :])
    a = jnp.exp(log_a)
    mm = _sqrt_one_minus_exp2x(log_a)
    return dict(shifted=shifted, xc=xc, xcb=xcb, r=r_g, i=i_g, a=a, mm=mm)


def _rglru_fwd(ux, ug, pv, wa, wi, t, name):
    lp, d = ux.shape

    def body(ux_ref, ug_ref, pv_ref, wa_ref, wi_ref, y_ref, h_ref, tail_s, hc_s):
        @pl.when(pl.program_id(0) == 0)
        def _():
            tail_s[...] = jnp.zeros_like(tail_s)
            hc_s[...] = jnp.zeros_like(hc_s)

        uxv = ux_ref[...]
        gt = _rglru_gates(uxv, tail_s[...], pv_ref, wa_ref, wi_ref)
        tail_s[...] = ux_ref[t - SUBLANES:t, :]
        h_ref[...] = _scan_rows(gt["a"], gt["mm"] * (gt["i"] * gt["xc"]), hc_s[0:1, :], False)
        hc_s[...] = h_ref[t - SUBLANES:t, :]
        hc_s[0:1, :] = h_ref[t - 1:t, :]
        y_ref[...] = (h_ref[...] * _gelu(ug_ref[...])[0]).astype(BF16)

    tile = pl.BlockSpec((t, d), lambda i: (i, 0))
    return pl.pallas_call(
        body, name=name, grid=(lp // t,),
        in_specs=[tile, tile, pl.BlockSpec(pv.shape, lambda i: (0, 0)),
                  pl.BlockSpec(wa.shape, lambda i: (0, 0, 0)), pl.BlockSpec(wi.shape, lambda i: (0, 0, 0))],
        out_specs=[tile, tile],
        out_shape=[jax.ShapeDtypeStruct((lp, d), BF16), jax.ShapeDtypeStruct((lp, d), F32)],
        scratch_shapes=[pltpu.VMEM((SUBLANES, d), F32), pltpu.VMEM((SUBLANES, d), F32)],
        compiler_params=_params(("arbitrary",)),
    )(ux, ug, pv, wa, wi)


def _rglru_bwd(ux, ug, hs, dy, pv, wa, wi, dep, t, name):
    lp, d = ux.shape
    nt = lp // t
    per = t // SUBLANES
    rb = d // RNN_BLOCKS

    def body(ux_ref, uxp_ref, ug_ref, h_ref, hp_ref, dy_ref, pv_ref, wa_ref, wi_ref, dep_ref,
             dux_ref, dug_ref, dpv_ref, dwa_ref, dwi_ref, ca_s, cg_s, cx_s):
        step = pl.program_id(0)
        first_tile = step == nt - 1

        @pl.when(step == 0)
        def _():
            for ref in (ca_s, cg_s, cx_s, dpv_ref, dwa_ref, dwi_ref):
                ref[...] = jnp.zeros_like(ref)

        uxv = ux_ref[...]
        prev8 = jnp.where(first_tile, 0.0, uxp_ref[...])
        hprev8 = jnp.where(first_tile, 0.0, hp_ref[...])
        gt = _rglru_gates(uxv, prev8, pv_ref, wa_ref, wi_ref)
        a, mm, r_g, i_g, xc = gt["a"], gt["mm"], gt["r"], gt["i"], gt["xc"]
        hv = h_ref[...]
        hprev = _shift_down(hv, hprev8, 1)
        ugv, dyv = ug_ref[...], dy_ref[...]
        gel, th = _gelu(ugv)
        dgel = 0.5 * (1.0 + th) + 0.5 * ugv * (1.0 - th * th) * (GELU_C * (1.0 + 3.0 * GELU_K * ugv * ugv))
        dug_ref[...] = (dyv * hv * dgel).astype(BF16)
        a_up = _shift_up(a, ca_s[...], 1)
        gv = _scan_rows(a_up, dyv * gel, cg_s[0:1, :], True)
        ca_s[...] = a[0:SUBLANES]
        cg_s[...] = gv[0:SUBLANES]
        ixc = i_g * xc
        d_ixc = gv * mm
        d_log_a = gv * hprev * a - (gv * ixc) * (a * a) / mm
        logsig = _log_sigmoid(pv_ref[7:8, :])
        d_pre_a = d_log_a * (LRU_C * logsig) * r_g * (1.0 - r_g)
        d_pre_i = d_ixc * xc * i_g * (1.0 - i_g)
        dab, dib = d_pre_a.astype(BF16), d_pre_i.astype(BF16)
        d_xc = d_ixc * i_g + _block_mm(dab, wa_ref, NT) + _block_mm(dib, wi_ref, NT)
        xcb = gt["xcb"]
        for h in range(RNN_BLOCKS):
            cols = slice(h * rb, (h + 1) * rb)
            dwa_ref[h] += lax.dot_general(xcb[:, cols], dab[:, cols], TN, preferred_element_type=F32)
            dwi_ref[h] += lax.dot_general(xcb[:, cols], dib[:, cols], TN, preferred_element_type=F32)
        csum = lambda v: jnp.sum(v, axis=0, keepdims=True)
        for k in range(CONV_WIDTH):
            dpv_ref[3 - k:4 - k, :] += csum(d_xc * gt["shifted"][k])
        dpv_ref[4:5, :] += csum(d_xc)
        dpv_ref[5:6, :] += csum(d_pre_a)
        dpv_ref[6:7, :] += csum(d_pre_i)
        dpv_ref[7:8, :] += csum(d_log_a * (LRU_C * r_g)) * _sig(-pv_ref[7:8, :])
        dux = pv_ref[3:4, :] * d_xc
        for k in range(1, CONV_WIDTH):
            dux = dux + pv_ref[3 - k:4 - k, :] * _shift_up(d_xc, cx_s[...], k)
        cx_s[...] = d_xc[0:SUBLANES]
        dux_ref[...] = dux.astype(BF16)

    rev = lambda i: (nt - 1 - i, 0)
    before = lambda i: (jnp.maximum((nt - 1 - i) * per - 1, 0), 0)
    tile = pl.BlockSpec((t, d), rev)
    tail = pl.BlockSpec((SUBLANES, d), before)
    fixed2 = lambda arr: pl.BlockSpec(arr.shape, lambda i: (0, 0))
    fixed3 = lambda arr: pl.BlockSpec(arr.shape, lambda i: (0, 0, 0))
    return pl.pallas_call(
        body, name=name, grid=(nt,),
        in_specs=[tile, tail, tile, tile, tail, tile, fixed2(pv), fixed3(wa), fixed3(wi), fixed2(dep)],
        out_specs=[tile, tile, fixed2(pv), fixed3(wa), fixed3(wi)],
        out_shape=[jax.ShapeDtypeStruct((lp, d), BF16), jax.ShapeDtypeStruct((lp, d), BF16),
                   jax.ShapeDtypeStruct(pv.shape, F32), jax.ShapeDtypeStruct(wa.shape, F32),
                   jax.ShapeDtypeStruct(wi.shape, F32)],
        scratch_shapes=[pltpu.VMEM((SUBLANES, d), F32)] * 3,
        compiler_params=_params(("arbitrary",)),
    )(ux, ux, ug, hs, hs, dy, pv, wa, wi, dep)


def _adamw(w, m, v, parts, dep, name):
    rows, cols = w.shape
    tr = _pick(rows, 256, SUBLANES)
    c1 = 1.0 / (1.0 - ADAM_B1 ** ADAM_STEP)
    c2 = 1.0 / (1.0 - ADAM_B2 ** ADAM_STEP)

    def body(w_ref, m_ref, v_ref, p_ref, dep_ref, g_ref, d_ref, mo_ref, vo_ref):
        g = p_ref[0].astype(F32)
        for q in range(1, N_DEV):
            g = g + p_ref[q].astype(F32)
        mn = ADAM_B1 * m_ref[...] + (1.0 - ADAM_B1) * g
        vn = ADAM_B2 * v_ref[...] + (1.0 - ADAM_B2) * (g * g)
        g_ref[...] = g
        mo_ref[...] = mn
        vo_ref[...] = vn
        d_ref[...] = -ADAM_LR * ((mn * c1) / (jnp.sqrt(vn * c2) + ADAM_EPS) + ADAM_WD * w_ref[...])

    blk = pl.BlockSpec((tr, cols), lambda i: (i, 0))
    return pl.pallas_call(
        body, name=name, grid=(rows // tr,),
        in_specs=[blk, blk, blk, pl.BlockSpec((N_DEV, tr, cols), lambda i: (0, i, 0)),
                  pl.BlockSpec(dep.shape, lambda i: (0, 0))],
        out_specs=[blk] * 4, out_shape=[jax.ShapeDtypeStruct((rows, cols), F32)] * 4,
        compiler_params=_params(("parallel",)),
    )(w, m, v, parts, dep)


WEIGHTS = ("meta_tokens", "norm_mix_g", "w_in", "b_gate", "conv_w", "conv_b", "w_rec_a", "b_rec_a", "w_rec_i",
           "b_rec_i", "lru_lambda", "q_norm_g", "w_uq", "kv_norm_g", "w_ukv", "w_branch", "w_out", "norm_ffn_g",
           "w_ffn_in", "w_ffn_out", "final_norm_g")
SHARDED = {"meta_tokens": True, "w_in": True, "b_gate": True, "conv_w": True, "w_uq": True, "w_ukv": True,
           "w_branch": False, "w_out": False, "w_ffn_in": True, "w_ffn_out": False}


def _as2d(a):
    return a.reshape(-1, a.shape[-1])


def _full_from_gathered(g, by_cols):
    if by_cols:
        return jnp.transpose(g, (1, 0, 2)).reshape(g.shape[1], N_DEV * g.shape[2])
    return g.reshape(N_DEV * g.shape[1], g.shape[2])


def _blocks_from_full(full, by_cols):
    if by_cols:
        r, c = full.shape
        return jnp.transpose(full.reshape(r, N_DEV, c // N_DEV), (1, 0, 2))
    return full.reshape(N_DEV, full.shape[0] // N_DEV, full.shape[1])


def kernel(x, meta_tokens, norm_mix_g, w_in, b_gate, conv_w, conv_b, w_rec_a, b_rec_a, w_rec_i, b_rec_i, lru_lambda, q_norm_g, w_uq, kv_norm_g, w_ukv, w_branch, w_out, norm_ffn_g, w_ffn_in, w_ffn_out, final_norm_g, loss_target, m_meta_tokens, m_norm_mix_g, m_w_in, m_b_gate, m_conv_w, m_conv_b, m_w_rec_a, m_b_rec_a, m_w_rec_i, m_b_rec_i, m_lru_lambda, m_q_norm_g, m_w_uq, m_kv_norm_g, m_w_ukv, m_w_branch, m_w_out, m_norm_ffn_g, m_w_ffn_in, m_w_ffn_out, m_final_norm_g, v_meta_tokens, v_norm_mix_g, v_w_in, v_b_gate, v_conv_w, v_conv_b, v_w_rec_a, v_b_rec_a, v_w_rec_i, v_b_rec_i, v_lru_lambda, v_q_norm_g, v_w_uq, v_kv_norm_g, v_w_ukv, v_w_branch, v_w_out, v_norm_ffn_g, v_w_ffn_in, v_w_ffn_out, v_final_norm_g):
    w = dict(meta_tokens=meta_tokens, norm_mix_g=norm_mix_g, w_in=w_in, b_gate=b_gate, conv_w=conv_w, conv_b=conv_b,
             w_rec_a=w_rec_a, b_rec_a=b_rec_a, w_rec_i=w_rec_i, b_rec_i=b_rec_i, lru_lambda=lru_lambda,
             q_norm_g=q_norm_g, w_uq=w_uq, kv_norm_g=kv_norm_g, w_ukv=w_ukv, w_branch=w_branch, w_out=w_out,
             norm_ffn_g=norm_ffn_g, w_ffn_in=w_ffn_in, w_ffn_out=w_ffn_out, final_norm_g=final_norm_g)
    m = dict(meta_tokens=m_meta_tokens, norm_mix_g=m_norm_mix_g, w_in=m_w_in, b_gate=m_b_gate, conv_w=m_conv_w,
             conv_b=m_conv_b, w_rec_a=m_w_rec_a, b_rec_a=m_b_rec_a, w_rec_i=m_w_rec_i, b_rec_i=m_b_rec_i,
             lru_lambda=m_lru_lambda, q_norm_g=m_q_norm_g, w_uq=m_w_uq, kv_norm_g=m_kv_norm_g, w_ukv=m_w_ukv,
             w_branch=m_w_branch, w_out=m_w_out, norm_ffn_g=m_norm_ffn_g, w_ffn_in=m_w_ffn_in,
             w_ffn_out=m_w_ffn_out, final_norm_g=m_final_norm_g)
    v = dict(meta_tokens=v_meta_tokens, norm_mix_g=v_norm_mix_g, w_in=v_w_in, b_gate=v_b_gate, conv_w=v_conv_w,
             conv_b=v_conv_b, w_rec_a=v_w_rec_a, b_rec_a=v_b_rec_a, w_rec_i=v_w_rec_i, b_rec_i=v_b_rec_i,
             lru_lambda=v_lru_lambda, q_norm_g=v_q_norm_g, w_uq=v_w_uq, kv_norm_g=v_kv_norm_g, w_ukv=v_w_ukv,
             w_branch=v_w_branch, w_out=v_w_out, norm_ffn_g=v_norm_ffn_g, w_ffn_in=v_w_ffn_in,
             w_ffn_out=v_w_ffn_out, final_norm_g=v_final_norm_g)

    seq, d_model = x.shape[1], x.shape[2]
    length = N_META + seq
    lp = -(-length // LANES) * LANES
    t_attn = _pick(lp, 640)
    t_rnn = LANES

    small = ("meta_tokens", "b_gate", "conv_w")
    names = list(SHARDED)
    mid, late = ("w_uq", "w_ukv", "w_branch", "w_out"), ("w_ffn_in", "w_ffn_out")
    payload = lambda n: _as2d(w[n]) if n in small else _as2d(w[n]).astype(BF16)
    got = _exchange([payload(n) for n in small], [True] * len(small), "gather_small")
    in_h = _exchange_start([payload("w_in")], [True], got[0], "gather_in_start")
    mid_h = _exchange_start([payload(n) for n in mid], [True] * len(mid), in_h["token"], "gather_mid_start")
    late_h = _exchange_start([payload(n) for n in late], [True] * len(late), mid_h["token"], "gather_late_start")
    full = {n: _full_from_gathered(g, SHARDED[n]) for n, g in zip(small, got)}
    h0 = jnp.concatenate([full["meta_tokens"], x[0], jnp.zeros((lp - length, d_model), F32)], axis=0)
    z = _rmsnorm_fwd(h0, norm_mix_g, "norm_mix")
    full["w_in"] = _full_from_gathered(_exchange_wait(in_h, z, "gather_in_wait")[0], True)

    splits = (D_RNN, D_RNN, Q_RANK, KV_RANK, QK_ROPE, 2 * d_model)
    offs = [0]
    for s in splits:
        offs.append(offs[-1] + s)
    w_x, w_g, w_q, w_kv, w_kr, w_m = (full["w_in"][:, offs[s]:offs[s + 1]] for s in range(6))
    w_kr = jnp.pad(w_kr, ((0, 0), (0, LANES - QK_ROPE)))
    bg = full["b_gate"].reshape(1, 2 * d_model)
    pv = jnp.concatenate([full["conv_w"], conv_b, b_rec_a, b_rec_i, lru_lambda], axis=0)
    wa_b, wi_b = w_rec_a[0].astype(BF16), w_rec_i[0].astype(BF16)
    g_final = final_norm_g.reshape(1, d_model)

    tgt = jnp.pad(loss_target[0], ((N_META, lp - length), (0, 0)))
    tabs = _rope_tables(lp)

    ux = _mm([(z, w_x)], "nn", "in_x")
    ug = _mm([(z, w_g)], "nn", "in_g")
    uq = _mm([(z, w_q)], "nn", "in_q")
    ukv = _mm([(z, w_kv)], "nn", "in_kv")
    ukr = _mm([(z, w_kr)], "nn", "in_kr")
    um = _mm([(z, w_m)], "nn", "in_m", out_dtype=BF16)
    for n, g in zip(mid, _exchange_wait(mid_h, um, "gather_mid_wait")):
        full[n] = _full_from_gathered(g, SHARDED[n])
    w_uq_pad = jnp.pad(full["w_uq"].reshape(Q_RANK, N_HEADS, QK_NOPE + QK_ROPE),
                       ((0, 0), (0, 0), (0, HEAD_PAD - QK_NOPE - QK_ROPE))).reshape(Q_RANK, N_HEADS * HEAD_PAD)
    w_ukv3 = full["w_ukv"].reshape(KV_RANK, N_HEADS, QK_NOPE + V_DIM)
    w_k_pad = jnp.pad(w_ukv3[:, :, :QK_NOPE], ((0, 0), (0, 0), (0, HEAD_PAD - QK_NOPE))).reshape(
        KV_RANK, N_HEADS * HEAD_PAD)
    w_v = w_ukv3[:, :, QK_NOPE:].reshape(KV_RANK, N_HEADS * V_DIM)
    wb_r, wb_a = full["w_branch"][:D_RNN], full["w_branch"][D_RNN:]
    y_rnn, hs = _rglru_fwd(ux, ug, pv, wa_b, wi_b, t_rnn, "rglru_fwd")
    qn = _rmsnorm_fwd(uq, q_norm_g, "norm_q")
    kvn = _rmsnorm_fwd(ukv, kv_norm_g, "norm_kv")
    qh = _up_rope(qn, w_uq_pad, tabs, "up_q")
    kh = _up_rope(kvn, w_k_pad, tabs, "up_k", ukr=ukr)
    vh = _mm([(kvn, w_v)], "nn", "up_v", out_dtype=BF16)
    oh, lse = _attn_fwd(qh, kh, vh, t_attn, "attn_fwd")
    p_rnn = _mm([(y_rnn, wb_r)], "nn", "branch_rnn", out_dtype=BF16)
    p_att = _mm([(oh, wb_a)], "nn", "branch_att", out_dtype=BF16)
    mixed, h1 = _mix_out(um, p_rnn, p_att, bg, full["w_out"], h0, "out_proj")
    for n, g in zip(late, _exchange_wait(late_h, h1, "gather_late_wait")):
        full[n] = _full_from_gathered(g, SHARDED[n])
    zf = _rmsnorm_fwd(h1, norm_ffn_g, "norm_ffn")
    gate, up, act = _ffn_in_swiglu(zf, full["w_ffn_in"], "ffn_in")
    h2 = _mm([(act, full["w_ffn_out"])], "nn", "ffn_out", res=h1)
    dh2, dg_final, _, loss_part = _loss_bwd(h2, tgt, g_final, seq, "loss_bwd")

    d_gate, d_up = _ffn_out_bwd_swiglu(dh2, full["w_ffn_out"], gate, up, "d_gate_up")
    dw_ffn_out = _mm_tn(act, dh2, "dw_ffn_out")
    dw_ffn_in = jnp.concatenate([_mm_tn(zf, d_gate, "dw_ffn_gate"), _mm_tn(zf, d_up, "dw_ffn_up")], axis=1)
    blocks = lambda n, g: _blocks_from_full(g, SHARDED[n]).astype(F32 if n in small else BF16)
    sent = {("w_ffn_in", "w_ffn_out"): _exchange_start(
        [blocks("w_ffn_in", dw_ffn_in), blocks("w_ffn_out", dw_ffn_out)], [False] * 2, dg_final, "scatter_ffn_start")}
    d_zf = _mm([(d_gate, full["w_ffn_in"], D_FF, 0), (d_up, full["w_ffn_in"], D_FF, 1)], "nt", "d_zf")
    dh1, dg_ffn = _rmsnorm_bwd(h1, d_zf, norm_ffn_g, "norm_ffn_bwd", res=dh2,
                               dep=sent[("w_ffn_in", "w_ffn_out")]["token"])
    d_mixed = _mm([(dh1, full["w_out"])], "nt", "d_mixed", out_dtype=BF16)
    dw_out = _mm_tn(mixed, dh1, "dw_out")
    d_prnn, d_patt, d_um, dbg = _mix_bwd(um, p_rnn, p_att, d_mixed, bg, "mix_bwd")
    d_yrnn = _mm([(d_prnn, wb_r)], "nt", "d_yrnn")
    d_oh = _mm([(d_patt, wb_a)], "nt", "d_oh", out_dtype=BF16)
    dwb_r = _mm_tn(y_rnn, d_prnn, "dw_branch_rnn")
    dwb_a = _mm_tn(oh, d_patt, "dw_branch_att")
    sent[("w_out", "w_branch")] = _exchange_start(
        [blocks("w_out", dw_out), blocks("w_branch", jnp.concatenate([dwb_r, dwb_a], axis=0))], [False] * 2,
        dg_ffn, "scatter_mix_start")
    dqh, l2row, dlrow = _attn_bwd_dq(qh, kh, vh, d_oh, oh, lse, sent[("w_out", "w_branch")]["token"], t_attn,
                                     "attn_bwd_dq")
    dkh, dvh = _attn_bwd_dkv(qh, kh, vh, d_oh, l2row, dlrow, t_attn, "attn_bwd_dkv")
    dqpad, dkpad, dukr = _rope_bwd(dqh, dkh, tabs, "rope_bwd")
    d_qn = _mm([(dqpad, w_uq_pad)], "nt", "d_qn")
    dw_uq_pad = _mm_tn(qn, dqpad, "dw_uq")
    d_kvn = _mm([(dkpad, w_k_pad), (dvh, w_v)], "nt", "d_kvn")
    dw_k_pad = _mm_tn(kvn, dkpad, "dw_uk")
    dw_v = _mm_tn(kvn, dvh, "dw_uv")
    dw_uq = dw_uq_pad.reshape(Q_RANK, N_HEADS, HEAD_PAD)[:, :, :QK_NOPE + QK_ROPE].reshape(Q_RANK, -1)
    dw_ukv = jnp.concatenate([dw_k_pad.reshape(KV_RANK, N_HEADS, HEAD_PAD)[:, :, :QK_NOPE],
                              dw_v.reshape(KV_RANK, N_HEADS, V_DIM)], axis=2).reshape(KV_RANK, -1)
    sent[("w_uq", "w_ukv")] = _exchange_start([blocks("w_uq", dw_uq), blocks("w_ukv", dw_ukv)], [False] * 2,
                                              dbg, "scatter_attn_start")
    duq, dg_q = _rmsnorm_bwd(uq, d_qn, q_norm_g, "norm_q_bwd", out_dtype=BF16)
    dukv, dg_kv = _rmsnorm_bwd(ukv, d_kvn, kv_norm_g, "norm_kv_bwd", out_dtype=BF16)
    dux, dug, dpv, dwa, dwi = _rglru_bwd(ux, ug, hs, d_yrnn, pv, wa_b, wi_b, sent[("w_uq", "w_ukv")]["token"],
                                         t_rnn, "rglru_bwd")
    grad_rep = dict(
        conv_b=dpv[4:5], w_rec_a=dwa, b_rec_a=dpv[5:6], w_rec_i=dwi, b_rec_i=dpv[6:7], lru_lambda=dpv[7:8],
        q_norm_g=dg_q, kv_norm_g=dg_kv, norm_ffn_g=dg_ffn, final_norm_g=dg_final)
    rep_now = tuple(grad_rep)
    sent[("b_gate", "conv_w") + rep_now] = _exchange_start(
        [blocks("b_gate", dbg.reshape(2, d_model)), blocks("conv_w", dpv[:CONV_WIDTH])]
        + [_as2d(grad_rep[n]).astype(BF16 if n in ("w_rec_a", "w_rec_i") else F32) for n in rep_now],
        [False] * 2 + [True] * len(rep_now), dg_kv, "scatter_small_start")
    d_z = _mm([(dux, w_x), (dug, w_g), (duq, w_q), (dukv, w_kv), (dukr, w_kr), (d_um, w_m)], "nt", "d_z")
    dw_in = jnp.concatenate([
        _mm_tn(z, dux, "dw_in_x"), _mm_tn(z, dug, "dw_in_g"), _mm_tn(z, duq, "dw_in_q"),
        _mm_tn(z, dukv, "dw_in_kv"), _mm_tn(z, dukr, "dw_in_kr")[:, :QK_ROPE], _mm_tn(z, d_um, "dw_in_m")], axis=1)
    last_h = _exchange_start([blocks("w_in", dw_in)], [False], sent[("b_gate", "conv_w") + rep_now]["token"],
                             "scatter_in_start")
    sent[("w_in",)] = last_h
    dh0, dg_mix = _rmsnorm_bwd(h0, d_z, norm_mix_g, "norm_mix_bwd", res=dh1, dep=last_h["token"])
    sent[("meta_tokens", "norm_mix_g")] = _exchange_start([blocks("meta_tokens", dh0[:N_META]), dg_mix], [False, True],
                                                          dg_mix, "scatter_tail_start")
    grads, deltas, new_m, new_v = {}, {}, {}, {}

    def update(n, parts, dep):
        g2, d2, m2, v2 = _adamw(_as2d(w[n]), _as2d(m[n]), _as2d(v[n]), parts, dep, "adamw_" + n)
        for store, val in ((grads, g2), (deltas, d2), (new_m, m2), (new_v, v2)):
            store[n] = val.reshape(w[n].shape)

    chain = sent[("meta_tokens", "norm_mix_g")]["token"]
    for group, handle in sent.items():
        for n, parts in zip(group, _exchange_wait(handle, chain, "scatter_wait_" + group[0])):
            update(n, parts, chain)
            chain = _as2d(deltas[n])[:SUBLANES, :LANES]

    loss = lax.psum(loss_part[0, 0], MESH_AXES)
    grad_x = dh0[N_META:length][None]
    return (loss, grad_x, *[grads[n] for n in WEIGHTS], *[deltas[n] for n in WEIGHTS],
            *[new_m[n] for n in WEIGHTS], *[new_v[n] for n in WEIGHTS])
```

```python
import functools
import math

import jax
import jax.numpy as jnp
from jax import lax
from jax.experimental import pallas as pl
from jax.experimental.pallas import tpu as pltpu

F32 = jnp.float32
BF16 = jnp.bfloat16

N_DEV = 8
MESH_AXES = ("x", "y", "c")
LANES = 128
SUBLANES = 8
VMEM_LIMIT = 56 * 1024 * 1024

N_META = 16
CHUNK_SHIFT = 6
CHUNK_BIAS = 64 - N_META
EPS = 1e-6
D_RNN = 1280
RNN_BLOCKS = 10
CONV_WIDTH = 4
LRU_C = 8.0
N_HEADS = 8
QK_NOPE = 128
QK_ROPE = 64
V_DIM = 128
HEAD_PAD = 256
Q_RANK = 384
KV_RANK = 256
ROPE_THETA = 10000.0
ATTN_SCALE = 1.0 / math.sqrt(QK_NOPE + QK_ROPE)
NEG = -1e30
TINY = 1e-30
LOG2E = 1.0 / math.log(2.0)
SCALE_LOG2E = ATTN_SCALE * LOG2E
Q_SPLIT = 2
Q_ALIGN = LANES // Q_SPLIT
ROW_GROUP = 32
SPILL = LANES
D_FF = 2816

ADAM_LR = 0.001
ADAM_B1 = 0.9
ADAM_B2 = 0.999
ADAM_EPS = 1e-08
ADAM_WD = 0.01
ADAM_STEP = 10

NN = (((1,), (0,)), ((), ()))
NT = (((1,), (1,)), ((), ()))
TN = (((0,), (0,)), ((), ()))


def _pick(n, cap, base=LANES):
    best = None
    for t in range(base, min(n, cap) + 1, base):
        if n % t == 0:
            best = t
    return best if best is not None else n


def _params(sem=None):
    return pltpu.CompilerParams(dimension_semantics=sem, vmem_limit_bytes=VMEM_LIMIT)


def _sig(x):
    return 0.5 + 0.5 * jnp.tanh(0.5 * x)


def _exchange(srcs, gather, name):
    n = len(srcs)
    out_shape = [jax.ShapeDtypeStruct((N_DEV,) + (s.shape if g else s.shape[1:]), s.dtype)
                 for s, g in zip(srcs, gather)]

    def body(*refs):
        src, dst = refs[:n], refs[n:2 * n]
        send_sems, recv_sems, local_sems = refs[2 * n:]
        x, y, c = lax.axis_index("x"), lax.axis_index("y"), lax.axis_index("c")
        me = 4 * x + 2 * y + c
        local = []
        for t in range(n):
            cp = pltpu.make_async_copy(src[t] if gather[t] else src[t].at[me], dst[t].at[me], local_sems.at[t])
            cp.start()
            local.append(cp)
        sends, recvs = [], []
        for k in range(1, N_DEV):
            px = 1 - x if k & 4 else x
            py = 1 - y if k & 2 else y
            pc = 1 - c if k & 1 else c
            peer = 4 * px + 2 * py + pc
            for t in range(n):
                cp = pltpu.make_async_remote_copy(
                    src_ref=src[t] if gather[t] else src[t].at[peer], dst_ref=dst[t].at[me],
                    send_sem=send_sems.at[t, k - 1], recv_sem=recv_sems.at[t, k - 1],
                    device_id=(px, py, pc), device_id_type=pl.DeviceIdType.MESH)
                cp.start()
                sends.append(cp)
                recvs.append(pltpu.make_async_remote_copy(
                    src_ref=src[t] if gather[t] else src[t].at[peer], dst_ref=dst[t].at[peer],
                    send_sem=send_sems.at[t, k - 1], recv_sem=recv_sems.at[t, k - 1],
                    device_id=(px, py, pc), device_id_type=pl.DeviceIdType.MESH))
        for cp in recvs:
            cp.wait_recv()
        for cp in sends:
            cp.wait_send()
        for cp in local:
            cp.wait()

    any_spec = pl.BlockSpec(memory_space=pl.ANY)
    return pl.pallas_call(
        body, name=name, out_shape=out_shape,
        in_specs=[any_spec] * n, out_specs=[any_spec] * n,
        scratch_shapes=[pltpu.SemaphoreType.DMA((n, N_DEV - 1)), pltpu.SemaphoreType.DMA((n, N_DEV - 1)),
                        pltpu.SemaphoreType.DMA((n,))],
    )(*srcs)


def _gather_two_level(block, name):
    def body(x_ref, out_ref, send_sems, recv_sems, local_sem):
        x, y, c = lax.axis_index("x"), lax.axis_index("y"), lax.axis_index("c")
        me, sibling = (x, y, c), (x, y, 1 - c)
        chips = [(1 - x, y), (x, 1 - y), (1 - x, 1 - y)]

        def slot(px, py, pc):
            return out_ref.at[4 * px + 2 * py + pc]

        def copy(k, owner, to, src=None):
            return pltpu.make_async_remote_copy(
                src_ref=slot(*owner) if src is None else src, dst_ref=slot(*owner),
                send_sem=send_sems.at[k], recv_sem=recv_sems.at[k], device_id=to,
                device_id_type=pl.DeviceIdType.MESH)

        mine = pltpu.make_async_copy(x_ref, slot(*me), local_sem)
        mine.start()
        first = [copy(0, me, sibling, src=x_ref)] + [copy(1 + j, me, (*chip, c), src=x_ref)
                                                     for j, chip in enumerate(chips)]
        for cp in first:
            cp.start()
        passed = [copy(4 + j, (*chip, c), sibling) for j, chip in enumerate(chips)]
        for j, chip in enumerate(chips):
            copy(1 + j, (*chip, c), me).wait_recv()
            passed[j].start()
        copy(0, sibling, me).wait_recv()
        for j, chip in enumerate(chips):
            copy(4 + j, (*chip, 1 - c), me).wait_recv()
        for cp in first + passed:
            cp.wait_send()
        mine.wait()

    any_spec = pl.BlockSpec(memory_space=pl.ANY)
    return pl.pallas_call(
        body, name=name, out_shape=jax.ShapeDtypeStruct((N_DEV,) + block.shape, block.dtype),
        in_specs=[any_spec], out_specs=any_spec,
        scratch_shapes=[pltpu.SemaphoreType.DMA((N_DEV - 1,)), pltpu.SemaphoreType.DMA((N_DEV - 1,)),
                        pltpu.SemaphoreType.DMA],
    )(block)


HBM_SPEC = pl.BlockSpec(memory_space=pltpu.HBM)
SEM_SPEC = pl.BlockSpec(memory_space=pltpu.SEMAPHORE)
DATAFLOW = pltpu.SideEffectType.DATAFLOW_SIDE_EFFECTING


def _peers(x, y, c):
    out = []
    for k in range(1, N_DEV):
        px = 1 - x if k & 4 else x
        py = 1 - y if k & 2 else y
        pc = 1 - c if k & 1 else c
        out.append((k, (px, py, pc), 4 * px + 2 * py + pc))
    return out


def _split_copies(src, land, gather, send_sems, recv_sems, local_sems):
    x, y, c = lax.axis_index("x"), lax.axis_index("y"), lax.axis_index("c")
    me = 4 * x + 2 * y + c
    n = len(src)
    local = [pltpu.make_async_copy(src[t] if gather[t] else src[t].at[me], land[t].at[me], local_sems.at[t])
             for t in range(n)]
    sends, recvs = [], []
    for k, pos, peer in _peers(x, y, c):
        for t in range(n):
            mine = src[t] if gather[t] else src[t].at[peer]
            slot = t * (N_DEV - 1) + k - 1
            common = dict(send_sem=send_sems.at[slot], recv_sem=recv_sems.at[slot], device_id=pos,
                          device_id_type=pl.DeviceIdType.MESH)
            sends.append(pltpu.make_async_remote_copy(src_ref=mine, dst_ref=land[t].at[me], **common))
            recvs.append(pltpu.make_async_remote_copy(src_ref=mine, dst_ref=land[t].at[peer], **common))
    return local, sends, recvs


def _exchange_start(srcs, gather, after, name):
    n = len(srcs)
    lands = [lax.empty((N_DEV,) + (s.shape if g else s.shape[1:]), s.dtype) for s, g in zip(srcs, gather)]

    def body(*refs):
        src, land = refs[:n], refs[n:2 * n]
        send_sems, recv_sems, local_sems = refs[2 * n + 1:2 * n + 4]
        local, sends, _ = _split_copies(src, land, gather, send_sems, recv_sems, local_sems)
        for cp in local + sends:
            cp.start()
        refs[-1][...] = jnp.zeros_like(refs[-1])

    hbm = lambda a: pltpu.HBM(a.shape, a.dtype)
    outs = pl.pallas_call(
        body, name=name,
        out_shape=(pltpu.SemaphoreType.DMA((n * (N_DEV - 1),)), pltpu.SemaphoreType.DMA((n * (N_DEV - 1),)),
                   pltpu.SemaphoreType.DMA((n,)), *[hbm(s) for s in srcs], *[hbm(a) for a in lands],
                   jax.ShapeDtypeStruct((SUBLANES, LANES), F32)),
        in_specs=[HBM_SPEC] * (2 * n) + [pl.BlockSpec(memory_space=pl.ANY)],
        out_specs=(SEM_SPEC, SEM_SPEC, SEM_SPEC, *[HBM_SPEC] * (2 * n), pl.BlockSpec(memory_space=pltpu.VMEM)),
        input_output_aliases={t: 3 + t for t in range(2 * n)},
        compiler_params=pltpu.CompilerParams(has_side_effects=DATAFLOW),
    )(*[pltpu.with_memory_space_constraint(a, pltpu.HBM) for a in list(srcs) + lands], after)
    return dict(sems=outs[:3], srcs=outs[3:3 + n], lands=outs[3 + n:3 + 2 * n], token=outs[-1], gather=gather)


def _exchange_wait(handle, after, name):
    srcs, lands, gather = handle["srcs"], handle["lands"], handle["gather"]
    n = len(srcs)

    def body(*refs):
        src, land = refs[:n], refs[n:2 * n]
        send_sems, recv_sems, local_sems = refs[2 * n:2 * n + 3]
        local, sends, recvs = _split_copies(src, land, gather, send_sems, recv_sems, local_sems)
        for cp in sends:
            cp.wait_send()
        for cp in recvs:
            cp.wait_recv()
        for cp in local:
            cp.wait()

    hbm = lambda a: pltpu.HBM(a.shape, a.dtype)
    outs = pl.pallas_call(
        body, name=name, out_shape=(*[hbm(s) for s in srcs], *[hbm(a) for a in lands]),
        in_specs=[HBM_SPEC] * (2 * n) + [SEM_SPEC] * 3 + [pl.BlockSpec(memory_space=pl.ANY)],
        out_specs=[HBM_SPEC] * (2 * n), input_output_aliases={t: t for t in range(2 * n)},
        compiler_params=pltpu.CompilerParams(has_side_effects=DATAFLOW),
    )(*srcs, *lands, *handle["sems"], after)
    return outs[n:]


def _mm(pairs, mode, name, res=None, out_dtype=F32):
    pairs = [p if len(p) == 4 else (p[0], p[1], p[0].shape[1], 0) for p in pairs]
    m = pairs[0][0].shape[0]
    n = pairs[0][1].shape[1] if mode == "nn" else pairs[0][1].shape[0]
    tm, tn = _pick(m, 640), _pick(n, 1408)
    np_ = len(pairs)
    dims = NN if mode == "nn" else NT

    def body(*refs):
        acc = None
        for s in range(np_):
            d = lax.dot_general(refs[2 * s][...].astype(BF16), refs[2 * s + 1][...].astype(BF16), dims,
                                preferred_element_type=F32)
            acc = d if acc is None else acc + d
        if res is not None:
            acc = acc + refs[2 * np_][...]
        refs[-1][...] = acc.astype(out_dtype)

    in_specs, args = [], []
    for a, b, kt, kb in pairs:
        in_specs.append(pl.BlockSpec((tm, kt), lambda i, j: (i, 0)))
        if mode == "nn":
            in_specs.append(pl.BlockSpec((kt, tn), lambda i, j, kb=kb: (kb, j)))
        else:
            in_specs.append(pl.BlockSpec((tn, kt), lambda i, j, kb=kb: (j, kb)))
        args += [a, b]
    if res is not None:
        in_specs.append(pl.BlockSpec((tm, tn), lambda i, j: (i, j)))
        args.append(res)
    return pl.pallas_call(
        body, name=name, grid=(m // tm, n // tn), in_specs=in_specs,
        out_specs=pl.BlockSpec((tm, tn), lambda i, j: (i, j)),
        out_shape=jax.ShapeDtypeStruct((m, n), out_dtype),
        compiler_params=_params(("parallel", "parallel")),
    )(*args)


def _mm_tn(a, b, name):
    m, k = a.shape
    n = b.shape[1]
    tm, tk, tn = _pick(m, 1664), _pick(k, 1408), _pick(n, 1408)

    def body(a_ref, b_ref, o_ref):
        @pl.when(pl.program_id(2) == 0)
        def _():
            o_ref[...] = jnp.zeros_like(o_ref)

        o_ref[...] += lax.dot_general(a_ref[...].astype(BF16), b_ref[...].astype(BF16), TN,
                                      preferred_element_type=F32)

    return pl.pallas_call(
        body, name=name, grid=(k // tk, n // tn, m // tm),
        in_specs=[pl.BlockSpec((tm, tk), lambda i, j, r: (r, i)), pl.BlockSpec((tm, tn), lambda i, j, r: (r, j))],
        out_specs=pl.BlockSpec((tk, tn), lambda i, j, r: (i, j)),
        out_shape=jax.ShapeDtypeStruct((k, n), F32),
        compiler_params=_params(("parallel", "parallel", "arbitrary")),
    )(a, b)


ROW_TILE_BYTES = 6 * 1024 * 1024


def _row_tile(rows, row_in, row_out):
    per_row = sum((r[1] * r[0].dtype.itemsize) if isinstance(r, tuple) else (r.shape[1] * r.dtype.itemsize)
                  for r in row_in)
    per_row += sum(w * jnp.dtype(dt).itemsize for w, dt in row_out)
    return _pick(rows, min(640, max(LANES, ROW_TILE_BYTES // per_row)))


def _rowcall(body, name, rows, row_in, full_in, row_out, acc_out=()):
    tr = _row_tile(rows, row_in, row_out)
    n_steps = rows // tr
    in_specs, args = [], []
    for r in row_in:
        arr, w, cb = r if isinstance(r, tuple) else (r, r.shape[1], 0)
        in_specs.append(pl.BlockSpec((tr, w), lambda i, cb=cb: (i, cb)))
        args.append(arr)
    for f in full_in:
        in_specs.append(pl.BlockSpec(f.shape, lambda i, nd=f.ndim: (0,) * nd))
        args.append(f)
    out_specs = [pl.BlockSpec((tr, w), lambda i: (i, 0)) for w, _ in row_out]
    out_shape = [jax.ShapeDtypeStruct((rows, w), dt) for w, dt in row_out]
    for shp, dt in acc_out:
        out_specs.append(pl.BlockSpec(shp, lambda i, nd=len(shp): (0,) * nd))
        out_shape.append(jax.ShapeDtypeStruct(shp, dt))

    def wrapped(*refs):
        body(pl.program_id(0), n_steps, *refs)

    return pl.pallas_call(
        wrapped, name=name, grid=(n_steps,), in_specs=in_specs, out_specs=out_specs, out_shape=out_shape,
        compiler_params=_params(("arbitrary",) if acc_out else ("parallel",)),
    )(*args)


def _rmsnorm_fwd(x, g, name):
    rows, w = x.shape

    def body(i, n, x_ref, g_ref, o_ref):
        xv = x_ref[...]
        r = lax.rsqrt(jnp.mean(xv * xv, axis=-1, keepdims=True) + EPS)
        o_ref[...] = (xv * r * g_ref[...]).astype(BF16)

    return _rowcall(body, name, rows,[x], [g], [(w, BF16)])[0]


def _rmsnorm_bwd_math(xv, dy, g):
    w = xv.shape[-1]
    r = lax.rsqrt(jnp.mean(xv * xv, axis=-1, keepdims=True) + EPS)
    t = dy * g
    dx = r * t - xv * (r * r * r * (jnp.sum(t * xv, axis=-1, keepdims=True) / w))
    dg = jnp.sum(dy * xv * r, axis=0, keepdims=True)
    return dx, dg


def _rmsnorm_bwd(x, dy, g, name, res=None, out_dtype=F32, dep=None):
    rows, w = x.shape

    def body(i, n, *refs):
        x_ref, dy_ref = refs[0], refs[1]
        g_ref, dx_ref, dg_ref = refs[-3], refs[-2], refs[-1]
        dx, dg = _rmsnorm_bwd_math(x_ref[...], dy_ref[...], g_ref[...])
        if res is not None:
            dx = dx + refs[2][...]
        dx_ref[...] = dx.astype(out_dtype)

        @pl.when(i == 0)
        def _():
            dg_ref[...] = jnp.zeros_like(dg_ref)

        dg_ref[...] += dg

    row_in = [x, dy] + ([res] if res is not None else [])
    return _rowcall(body, name, rows, row_in, ([dep] if dep is not None else []) + [g], [(w, out_dtype)],
                    [((1, w), F32)])


def _loss_bwd(h2, tgt, g, seq, name):
    rows, w = h2.shape
    tr = _row_tile(rows, [h2, tgt], [(w, F32)])

    def body(i, n, h_ref, t_ref, g_ref, dh_ref, dg_ref, lcol_ref, loss_ref):
        hv, gv = h_ref[...], g_ref[...]
        row = i * tr + lax.broadcasted_iota(jnp.int32, (tr, w), 0)
        valid = jnp.logical_and(row >= N_META, row < N_META + seq)
        r = lax.rsqrt(jnp.mean(hv * hv, axis=-1, keepdims=True) + EPS)
        err = jnp.where(valid, hv * r * gv - t_ref[...], 0.0)
        dx, dg = _rmsnorm_bwd_math(hv, err * (1.0 / w), gv)
        dh_ref[...] = dx

        @pl.when(i == 0)
        def _():
            dg_ref[...] = jnp.zeros_like(dg_ref)
            lcol_ref[...] = jnp.zeros_like(lcol_ref)

        dg_ref[...] += dg
        lcol_ref[...] += jnp.sum(err * err, axis=0, keepdims=True)

        @pl.when(i == n - 1)
        def _():
            total = jnp.sum(lcol_ref[...], axis=1, keepdims=True) * (0.5 / w)
            loss_ref[...] = jnp.broadcast_to(total, loss_ref.shape)

    return _rowcall(body, name, rows, [h2, tgt], [g], [(w, F32)],
                    [((1, w), F32), ((1, w), F32), ((1, LANES), F32)])


def _mix_out(um, p_rnn, p_att, bg, w_out, res, name):
    rows, d = p_rnn.shape
    tr = _pick(rows, 640)

    def body(u0_ref, u1_ref, pr_ref, pa_ref, bg_ref, w_ref, r_ref, mix_ref, o_ref):
        g0 = _sig(u0_ref[...].astype(F32) + bg_ref[:, :d])
        g1 = _sig(u1_ref[...].astype(F32) + bg_ref[:, d:])
        mixed = (g0 * pr_ref[...].astype(F32) + g1 * pa_ref[...].astype(F32)).astype(BF16)
        mix_ref[...] = mixed
        o_ref[...] = r_ref[...] + jnp.dot(mixed, w_ref[...], preferred_element_type=F32)

    row = pl.BlockSpec((tr, d), lambda i: (i, 0))
    whole = lambda a: pl.BlockSpec(a.shape, lambda i: (0, 0))
    return pl.pallas_call(
        body, name=name, grid=(rows // tr,),
        in_specs=[row, pl.BlockSpec((tr, d), lambda i: (i, 1)), row, row, whole(bg), whole(w_out), row],
        out_specs=[row, row],
        out_shape=[jax.ShapeDtypeStruct((rows, d), BF16), jax.ShapeDtypeStruct((rows, d), F32)],
        compiler_params=_params(("parallel",)),
    )(um, um, p_rnn, p_att, bg, w_out, res)


def _mix_bwd(um, p_rnn, p_att, dmix, bg, name):
    rows, d = p_rnn.shape

    def body(i, n, u0_ref, u1_ref, pr_ref, pa_ref, dm_ref, bg_ref, dpr_ref, dpa_ref, dum_ref, dbg_ref):
        g0 = _sig(u0_ref[...].astype(F32) + bg_ref[:, :d])
        g1 = _sig(u1_ref[...].astype(F32) + bg_ref[:, d:])
        dm = dm_ref[...].astype(F32)
        dpr_ref[...] = (dm * g0).astype(BF16)
        dpa_ref[...] = (dm * g1).astype(BF16)
        du0 = dm * pr_ref[...].astype(F32) * g0 * (1.0 - g0)
        du1 = dm * pa_ref[...].astype(F32) * g1 * (1.0 - g1)
        dum_ref[:, :d] = du0.astype(BF16)
        dum_ref[:, d:] = du1.astype(BF16)

        @pl.when(i == 0)
        def _():
            dbg_ref[...] = jnp.zeros_like(dbg_ref)

        dbg_ref[:, :d] += jnp.sum(du0, axis=0, keepdims=True)
        dbg_ref[:, d:] += jnp.sum(du1, axis=0, keepdims=True)

    return _rowcall(body, name, rows,[(um, d, 0), (um, d, 1), p_rnn, p_att, dmix], [bg],
                    [(d, BF16), (d, BF16), (2 * d, BF16)], [((1, 2 * d), F32)])


def _ffn_in_swiglu(zf, w, name):
    m, k = zf.shape
    f = w.shape[1] // 2
    tm, tn = _pick(m, 640), _pick(f, 1408)
    nb = f // tn

    def body(a_ref, bg_ref, bu_ref, g_ref, u_ref, act_ref):
        a = a_ref[...]
        gate = jnp.dot(a, bg_ref[...], preferred_element_type=F32)
        up = jnp.dot(a, bu_ref[...], preferred_element_type=F32)
        g_ref[...] = gate.astype(BF16)
        u_ref[...] = up.astype(BF16)
        act_ref[...] = (gate * _sig(gate) * up).astype(BF16)

    tile = pl.BlockSpec((tm, tn), lambda i, j: (i, j))
    return pl.pallas_call(
        body, name=name, grid=(m // tm, nb),
        in_specs=[pl.BlockSpec((tm, k), lambda i, j: (i, 0)), pl.BlockSpec((k, tn), lambda i, j: (0, j)),
                  pl.BlockSpec((k, tn), lambda i, j: (0, j + nb))],
        out_specs=[tile] * 3, out_shape=[jax.ShapeDtypeStruct((m, f), BF16)] * 3,
        compiler_params=_params(("parallel", "parallel")),
    )(zf, w, w)


def _ffn_out_bwd_swiglu(dh, w_out, gate, up, name):
    m, k = dh.shape
    f = w_out.shape[0]
    tm, tn = _pick(m, 640), _pick(f, 1408)

    def body(a_ref, b_ref, g_ref, u_ref, dg_ref, du_ref):
        da = lax.dot_general(a_ref[...].astype(BF16), b_ref[...], NT, preferred_element_type=F32)
        gv = g_ref[...].astype(F32)
        sg = _sig(gv)
        dg_ref[...] = (da * u_ref[...].astype(F32) * (sg * (1.0 + gv * (1.0 - sg)))).astype(BF16)
        du_ref[...] = (da * gv * sg).astype(BF16)

    tile = pl.BlockSpec((tm, tn), lambda i, j: (i, j))
    return pl.pallas_call(
        body, name=name, grid=(m // tm, f // tn),
        in_specs=[pl.BlockSpec((tm, k), lambda i, j: (i, 0)), pl.BlockSpec((tn, k), lambda i, j: (j, 0)), tile, tile],
        out_specs=[tile] * 2, out_shape=[jax.ShapeDtypeStruct((m, f), BF16)] * 2,
        compiler_params=_params(("parallel", "parallel")),
    )(dh, w_out, gate, up)


def _rope_tables(lp):
    idx = jnp.arange(lp, dtype=jnp.int32).astype(F32)
    inv_freq = ROPE_THETA ** (-jnp.arange(0, QK_ROPE, 2, dtype=F32) / QK_ROPE)
    ang = idx[:, None] * inv_freq[None, :]
    cos, sin = jnp.cos(ang), jnp.sin(ang)
    half = QK_ROPE // 2
    z = lambda wdt: jnp.zeros((lp, wdt), F32)
    tc = jnp.concatenate([cos, cos, z(LANES - 2 * half)], axis=1)
    ts1 = jnp.concatenate([-sin, z(LANES - half)], axis=1)
    ts2 = jnp.concatenate([z(half), sin, z(LANES - 2 * half)], axis=1)
    return tc, ts1, ts2


def _rope(xv, tc, ts1, ts2):
    half = QK_ROPE // 2
    return xv * tc + pltpu.roll(xv, LANES - half, 1) * ts1 + pltpu.roll(xv, half, 1) * ts2


def _rope_t(dv, tc, ts1, ts2):
    half = QK_ROPE // 2
    return dv * tc + pltpu.roll(dv * ts1, half, 1) + pltpu.roll(dv * ts2, LANES - half, 1)


def _up_rope(xn, w_pad, tabs, name, ukr=None):
    rows, k = xn.shape
    n = w_pad.shape[1]
    tr = _pick(rows, 640)

    def body(*refs):
        x_ref, w_ref, c_ref, s1_ref, s2_ref = refs[:5]
        o_ref = refs[-1]
        tc, ts1, ts2 = c_ref[...], s1_ref[...], s2_ref[...]
        y = jnp.dot(x_ref[...], w_ref[...], preferred_element_type=F32)
        kr = None if ukr is None else _rope(refs[5][...], tc, ts1, ts2).astype(BF16)
        for h in range(N_HEADS):
            lo, mid, hi = h * HEAD_PAD, h * HEAD_PAD + QK_NOPE, (h + 1) * HEAD_PAD
            o_ref[:, lo:mid] = y[:, lo:mid].astype(BF16)
            o_ref[:, mid:hi] = _rope(y[:, mid:hi], tc, ts1, ts2).astype(BF16) if ukr is None else kr

    row = lambda wdt: pl.BlockSpec((tr, wdt), lambda i: (i, 0))
    in_specs = [row(k), pl.BlockSpec((k, n), lambda i: (0, 0)), row(LANES), row(LANES), row(LANES)]
    args = [xn, w_pad, *tabs]
    if ukr is not None:
        in_specs.append(row(LANES))
        args.append(ukr)
    return pl.pallas_call(
        body, name=name, grid=(rows // tr,), in_specs=in_specs, out_specs=row(n),
        out_shape=jax.ShapeDtypeStruct((rows, n), BF16), compiler_params=_params(("parallel",)),
    )(*args)


def _rope_bwd(dq, dk, tabs, name):
    rows, w = dq.shape

    def body(i, n, dq_ref, dk_ref, c_ref, s1_ref, s2_ref, qo_ref, ko_ref, ro_ref):
        tc, ts1, ts2 = c_ref[...], s1_ref[...], s2_ref[...]
        dkr = None
        for h in range(N_HEADS):
            lo, mid, hi = h * HEAD_PAD, h * HEAD_PAD + QK_NOPE, (h + 1) * HEAD_PAD
            qo_ref[:, lo:mid] = dq_ref[:, lo:mid].astype(BF16)
            qo_ref[:, mid:hi] = _rope_t(dq_ref[:, mid:hi].astype(F32), tc, ts1, ts2).astype(BF16)
            ko_ref[:, lo:mid] = dk_ref[:, lo:mid].astype(BF16)
            ko_ref[:, mid:hi] = jnp.zeros((ko_ref.shape[0], hi - mid), BF16)
            part = dk_ref[:, mid:hi].astype(F32)
            dkr = part if dkr is None else dkr + part
        ro_ref[...] = _rope_t(dkr, tc, ts1, ts2).astype(BF16)

    return _rowcall(body, name, rows,[dq, dk, *tabs], [],
                    [(w, BF16), (w, BF16), (LANES, BF16)])


def _visible(q0, k0, tq, tk):
    qrow = q0 + lax.broadcasted_iota(jnp.int32, (tq, tk), 0)
    kcol = k0 + lax.broadcasted_iota(jnp.int32, (tq, tk), 1)
    return ((kcol + CHUNK_BIAS) >> CHUNK_SHIFT) <= ((qrow + CHUNK_BIAS) >> CHUNK_SHIFT)


def _visible_t(k0, q0, tk, tq):
    krow = k0 + lax.broadcasted_iota(jnp.int32, (tk, tq), 0)
    qcol = q0 + lax.broadcasted_iota(jnp.int32, (tk, tq), 1)
    return ((krow + CHUNK_BIAS) >> CHUNK_SHIFT) <= ((qcol + CHUNK_BIAS) >> CHUNK_SHIFT)


def _lanes(v, width):
    return jnp.tile(v, (1, width // LANES))


def _pipelined_chunks(n_full, scores, absorb):
    scores(0, 0)

    def pair(jj, carry):
        a = 2 * jj
        scores(a + 1, 1)
        absorb(a, 0, False)
        scores(a + 2, 0)
        absorb(a + 1, 1, False)
        return carry

    lax.fori_loop(0, n_full // 2, pair, 0)

    @pl.when(n_full % 2 == 0)
    def _():
        absorb(n_full, 0, True)

    @pl.when(n_full % 2 == 1)
    def _():
        scores(n_full, 1)
        absorb(n_full - 1, 0, False)
        absorb(n_full, 1, True)


def _attn_fwd(q, k, v, t, name):
    lp = q.shape[0]
    nt = lp // t

    def body(q_ref, k_ref, v_ref, o_ref, lse_ref, m_s, l_s, acc_s, a_s, sa_s, sb_s, p_s):
        i = pl.program_id(1)
        m_s[...] = jnp.full(m_s.shape, NEG, F32)
        l_s[...] = jnp.zeros(l_s.shape, F32)
        acc_s[...] = jnp.zeros(acc_s.shape, F32)

        s_bufs = (sa_s, sb_s)

        def scores(j, slot):
            r0 = pl.multiple_of(j * t, t)
            s_bufs[slot][...] = lax.dot_general(q_ref[...], k_ref[pl.ds(r0, t), :], NT,
                                                preferred_element_type=F32)

        def absorb(j, slot, masked):
            for r in range(0, t, ROW_GROUP):
                rows = slice(r, r + ROW_GROUP)
                s = s_bufs[slot][rows, :]
                if masked:
                    s = jnp.where(_visible(i * t + r, i * t, ROW_GROUP, t), s, NEG)
                m_prev = m_s[rows, :]
                m_new = jnp.maximum(m_prev, jnp.max(s, axis=1, keepdims=True))
                alpha = jnp.exp2((m_prev - m_new) * SCALE_LOG2E)
                p = jnp.exp2((s - _lanes(m_new, t)) * SCALE_LOG2E)
                l_s[rows, :] = alpha * l_s[rows, :] + jnp.sum(p, axis=1, keepdims=True)
                m_s[rows, :] = m_new
                a_s[rows, :] = alpha
                p_s[rows, :] = p.astype(BF16)
            r0 = pl.multiple_of(j * t, t)
            acc_s[...] = a_s[...] * acc_s[...] + jnp.dot(p_s[...], v_ref[pl.ds(r0, t), :],
                                                         preferred_element_type=F32)

        r1 = pl.multiple_of(jnp.minimum(i + 1, nt - 1) * t, t)
        rows = slice(t - SPILL, t)
        s = lax.dot_general(q_ref[rows, :], k_ref[pl.ds(r1, SPILL), :], NT, preferred_element_type=F32)
        seen = jnp.logical_and(_visible(i * t + t - SPILL, (i + 1) * t, SPILL, SPILL), i + 1 < nt)
        s = jnp.where(seen, s, NEG)
        m_new = jnp.max(s, axis=1, keepdims=True)
        p = jnp.exp2((s - m_new) * SCALE_LOG2E)
        l_s[rows, :] = jnp.broadcast_to(jnp.sum(p, axis=1, keepdims=True), (SPILL, LANES))
        acc_s[rows, :] = jnp.dot(p.astype(BF16), v_ref[pl.ds(r1, SPILL), :], preferred_element_type=F32)
        m_s[rows, :] = jnp.broadcast_to(m_new, (SPILL, LANES))

        _pipelined_chunks(i, scores, absorb)
        o_ref[...] = (acc_s[...] / l_s[...]).astype(BF16)
        lse_ref[...] = m_s[...] * ATTN_SCALE + jnp.log(l_s[...])

    return pl.pallas_call(
        body, name=name, grid=(N_HEADS, nt),
        in_specs=[pl.BlockSpec((t, HEAD_PAD), lambda h, i: (i, h)),
                  pl.BlockSpec((lp, HEAD_PAD), lambda h, i: (0, h)),
                  pl.BlockSpec((lp, V_DIM), lambda h, i: (0, h))],
        out_specs=[pl.BlockSpec((t, V_DIM), lambda h, i: (i, h)),
                   pl.BlockSpec((None, t, LANES), lambda h, i: (h, i, 0))],
        out_shape=[jax.ShapeDtypeStruct((lp, N_HEADS * V_DIM), BF16),
                   jax.ShapeDtypeStruct((N_HEADS, lp, LANES), F32)],
        scratch_shapes=[pltpu.VMEM((t, LANES), F32), pltpu.VMEM((t, LANES), F32), pltpu.VMEM((t, V_DIM), F32),
                        pltpu.VMEM((t, LANES), F32), pltpu.VMEM((t, t), F32), pltpu.VMEM((t, t), F32),
                        pltpu.VMEM((t, t), BF16)],
        compiler_params=_params(("parallel", "arbitrary")),
    )(q, k, v)


def _attn_bwd_dq(q, k, v, do, o, lse, dep, t, name):
    lp = q.shape[0]
    nt = lp // t

    def body(q_ref, k_ref, v_ref, do_ref, o_ref, lse_ref, dep_ref, dq_ref, l2row_ref, dlrow_ref,
             acc_s, l2_s, dl_s, sa_s, sb_s, da_s, db_s, ds_s):
        i = pl.program_id(1)
        delta = jnp.sum(do_ref[...].astype(F32) * o_ref[...].astype(F32), axis=1, keepdims=True)
        dl_s[...] = jnp.broadcast_to(delta, dl_s.shape)
        l2_s[...] = lse_ref[...] * LOG2E
        l2row_ref[...] = l2_s[...].T[0:SUBLANES, :]
        dlrow_ref[...] = dl_s[...].T[0:SUBLANES, :]
        acc_s[...] = jnp.zeros(acc_s.shape, F32)
        s_bufs, d_bufs = (sa_s, sb_s), (da_s, db_s)

        def dscores(s, dp, rows, width):
            p = jnp.exp2(s * SCALE_LOG2E - _lanes(l2_s[rows, :], width))
            return (p * (dp - _lanes(dl_s[rows, :], width))).astype(BF16)

        r1 = pl.multiple_of(jnp.minimum(i + 1, nt - 1) * t, t)
        rows = slice(t - SPILL, t)
        ks, vs = k_ref[pl.ds(r1, SPILL), :], v_ref[pl.ds(r1, SPILL), :]
        s = lax.dot_general(q_ref[rows, :], ks, NT, preferred_element_type=F32)
        seen = jnp.logical_and(_visible(i * t + t - SPILL, (i + 1) * t, SPILL, SPILL), i + 1 < nt)
        s = jnp.where(seen, s, NEG)
        dp = lax.dot_general(do_ref[rows, :], vs, NT, preferred_element_type=F32)
        acc_s[rows, :] = jnp.dot(dscores(s, dp, rows, SPILL), ks, preferred_element_type=F32)

        def scores(j, slot):
            r0 = pl.multiple_of(j * t, t)
            s_bufs[slot][...] = lax.dot_general(q_ref[...], k_ref[pl.ds(r0, t), :], NT,
                                                preferred_element_type=F32)
            d_bufs[slot][...] = lax.dot_general(do_ref[...], v_ref[pl.ds(r0, t), :], NT,
                                                preferred_element_type=F32)

        def absorb(j, slot, masked):
            for r in range(0, t, ROW_GROUP):
                rows = slice(r, r + ROW_GROUP)
                s = s_bufs[slot][rows, :]
                if masked:
                    s = jnp.where(_visible(i * t + r, i * t, ROW_GROUP, t), s, NEG)
                ds_s[rows, :] = dscores(s, d_bufs[slot][rows, :], rows, t)
            r0 = pl.multiple_of(j * t, t)
            acc_s[...] += jnp.dot(ds_s[...], k_ref[pl.ds(r0, t), :], preferred_element_type=F32)

        _pipelined_chunks(i, scores, absorb)
        dq_ref[...] = (acc_s[...] * ATTN_SCALE).astype(BF16)

    stat_row = pl.BlockSpec((None, None, SUBLANES, t), lambda h, i: (h, i, 0, 0))
    return pl.pallas_call(
        body, name=name, grid=(N_HEADS, nt),
        in_specs=[pl.BlockSpec((t, HEAD_PAD), lambda h, i: (i, h)),
                  pl.BlockSpec((lp, HEAD_PAD), lambda h, i: (0, h)),
                  pl.BlockSpec((lp, V_DIM), lambda h, i: (0, h)),
                  pl.BlockSpec((t, V_DIM), lambda h, i: (i, h)),
                  pl.BlockSpec((t, V_DIM), lambda h, i: (i, h)),
                  pl.BlockSpec((None, t, LANES), lambda h, i: (h, i, 0)),
                  pl.BlockSpec(dep.shape, lambda h, i: (0, 0))],
        out_specs=[pl.BlockSpec((t, HEAD_PAD), lambda h, i: (i, h)), stat_row, stat_row],
        out_shape=[jax.ShapeDtypeStruct((lp, N_HEADS * HEAD_PAD), BF16),
                   jax.ShapeDtypeStruct((N_HEADS, nt, SUBLANES, t), F32),
                   jax.ShapeDtypeStruct((N_HEADS, nt, SUBLANES, t), F32)],
        scratch_shapes=[pltpu.VMEM((t, HEAD_PAD), F32), pltpu.VMEM((t, LANES), F32), pltpu.VMEM((t, LANES), F32),
                        pltpu.VMEM((t, t), F32), pltpu.VMEM((t, t), F32), pltpu.VMEM((t, t), F32),
                        pltpu.VMEM((t, t), F32), pltpu.VMEM((t, t), BF16)],
        compiler_params=_params(("parallel", "arbitrary")),
    )(q, k, v, do, o, lse, dep)


def _attn_bwd_dkv(q, k, v, do, l2row, dlrow, t, name):
    lp = q.shape[0]
    nt = lp // t

    def body(q_ref, k_ref, v_ref, do_ref, l2_ref, dl_ref, dk_ref, dv_ref,
             dk_s, dv_s, sa_s, sb_s, da_s, db_s, p_s, ds_s):
        j = pl.program_id(1)
        dk_s[...] = jnp.zeros(dk_s.shape, F32)
        dv_s[...] = jnp.zeros(dv_s.shape, F32)
        s_bufs, d_bufs = (sa_s, sb_s), (da_s, db_s)

        def weights(st, dpt, l2r, dlr):
            pt = jnp.exp2(st * SCALE_LOG2E - l2r)
            return pt.astype(BF16), (pt * (dpt - dlr)).astype(BF16)

        prev = jnp.maximum(j - 1, 0)
        q0 = pl.multiple_of(prev * t + t - SPILL, SPILL)
        rows = slice(0, SPILL)
        qs, dos = q_ref[pl.ds(q0, SPILL), :], do_ref[pl.ds(q0, SPILL), :]
        st = lax.dot_general(k_ref[rows, :], qs, NT, preferred_element_type=F32)
        seen = jnp.logical_and(_visible_t(j * t, j * t - SPILL, SPILL, SPILL), j > 0)
        st = jnp.where(seen, st, NEG)
        dpt = lax.dot_general(v_ref[rows, :], dos, NT, preferred_element_type=F32)
        pt, dst = weights(st, dpt, l2_ref[prev, 0:1, t - SPILL:], dl_ref[prev, 0:1, t - SPILL:])
        dv_s[rows, :] = jnp.dot(pt, dos, preferred_element_type=F32)
        dk_s[rows, :] = jnp.dot(dst, qs, preferred_element_type=F32)

        def scores(c, slot):
            r0 = pl.multiple_of((nt - 1 - c) * t, t)
            s_bufs[slot][...] = lax.dot_general(k_ref[...], q_ref[pl.ds(r0, t), :], NT,
                                                preferred_element_type=F32)
            d_bufs[slot][...] = lax.dot_general(v_ref[...], do_ref[pl.ds(r0, t), :], NT,
                                                preferred_element_type=F32)

        def absorb(c, slot, masked):
            i = nt - 1 - c
            l2r, dlr = l2_ref[i, 0:1, :], dl_ref[i, 0:1, :]
            for r in range(0, t, ROW_GROUP):
                rows = slice(r, r + ROW_GROUP)
                st = s_bufs[slot][rows, :]
                if masked:
                    st = jnp.where(_visible_t(j * t + r, j * t, ROW_GROUP, t), st, NEG)
                p_s[rows, :], ds_s[rows, :] = weights(st, d_bufs[slot][rows, :], l2r, dlr)
            r0 = pl.multiple_of(i * t, t)
            dv_s[...] += jnp.dot(p_s[...], do_ref[pl.ds(r0, t), :], preferred_element_type=F32)
            dk_s[...] += jnp.dot(ds_s[...], q_ref[pl.ds(r0, t), :], preferred_element_type=F32)

        _pipelined_chunks(nt - 1 - j, scores, absorb)
        dk_ref[...] = (dk_s[...] * ATTN_SCALE).astype(BF16)
        dv_ref[...] = dv_s[...].astype(BF16)

    stat_rows = pl.BlockSpec((None, nt, SUBLANES, t), lambda h, j: (h, 0, 0, 0))
    return pl.pallas_call(
        body, name=name, grid=(N_HEADS, nt),
        in_specs=[pl.BlockSpec((lp, HEAD_PAD), lambda h, j: (0, h)),
                  pl.BlockSpec((t, HEAD_PAD), lambda h, j: (j, h)),
                  pl.BlockSpec((t, V_DIM), lambda h, j: (j, h)),
                  pl.BlockSpec((lp, V_DIM), lambda h, j: (0, h)),
                  stat_rows, stat_rows],
        out_specs=[pl.BlockSpec((t, HEAD_PAD), lambda h, j: (j, h)),
                   pl.BlockSpec((t, V_DIM), lambda h, j: (j, h))],
        out_shape=[jax.ShapeDtypeStruct((lp, N_HEADS * HEAD_PAD), BF16),
                   jax.ShapeDtypeStruct((lp, N_HEADS * V_DIM), BF16)],
        scratch_shapes=[pltpu.VMEM((t, HEAD_PAD), F32), pltpu.VMEM((t, V_DIM), F32),
                        pltpu.VMEM((t, t), F32), pltpu.VMEM((t, t), F32), pltpu.VMEM((t, t), F32),
                        pltpu.VMEM((t, t), F32), pltpu.VMEM((t, t), BF16), pltpu.VMEM((t, t), BF16)],
        compiler_params=_params(("parallel", "arbitrary")),
    )(q, k, v, do, l2row, dlrow)


def _shift_down(cur, prev8, k):
    r = pltpu.roll(cur, k, 0)
    row8 = lax.broadcasted_iota(jnp.int32, prev8.shape, 0)
    first = jnp.where(row8 < k, pltpu.roll(prev8, k, 0), r[0:SUBLANES])
    return jnp.concatenate([first, r[SUBLANES:]], axis=0)


def _shift_up(cur, next8, k):
    t = cur.shape[0]
    r = pltpu.roll(cur, t - k, 0)
    row8 = lax.broadcasted_iota(jnp.int32, next8.shape, 0)
    last = jnp.where(row8 >= SUBLANES - k, pltpu.roll(next8, SUBLANES - k, 0), r[t - SUBLANES:])
    return jnp.concatenate([r[:t - SUBLANES], last], axis=0)


def _scan_rows(a, b, edge, reverse):
    t, d = a.shape
    groups = t // SUBLANES
    a3, b3 = a.reshape(groups, SUBLANES, d), b.reshape(groups, SUBLANES, d)
    sub = lax.broadcasted_iota(jnp.int32, a3.shape, 1)
    s = 1
    while s < SUBLANES:
        keep = sub < SUBLANES - s if reverse else sub >= s
        shift = SUBLANES - s if reverse else s
        a_sh = jnp.where(keep, pltpu.roll(a3, shift, 1), 1.0)
        b_sh = jnp.where(keep, pltpu.roll(b3, shift, 1), 0.0)
        b3 = a3 * b_sh + b3
        a3 = a3 * a_sh
        s *= 2
    out = [None] * groups
    for k in (range(groups - 1, -1, -1) if reverse else range(groups)):
        out[k] = b3[k] + a3[k] * edge
        edge = out[k][0:1, :] if reverse else out[k][SUBLANES - 1:SUBLANES, :]
    return jnp.concatenate(out, axis=0)


def _sqrt_one_minus_exp2x(x):
    th = jnp.tanh(x)
    m2 = (-2.0 * th) / (1.0 - th)
    return m2 * lax.rsqrt(jnp.maximum(m2, TINY))


def _log_sigmoid(x):
    return jnp.minimum(x, 0.0) - jnp.log(1.0 + jnp.exp(-jnp.abs(x)))


GELU_C = math.sqrt(2.0 / math.pi)
GELU_K = 0.044715


def _gelu(x):
    th = jnp.tanh(GELU_C * (x + GELU_K * x * x * x))
    return 0.5 * x * (1.0 + th), th


def _block_mm(xb, w_ref, dims):
    rb = D_RNN // RNN_BLOCKS
    return jnp.concatenate(
        [lax.dot_general(xb[:, h * rb:(h + 1) * rb], w_ref[h], dims, preferred_element_type=F32)
         for h in range(RNN_BLOCKS)], axis=1)


def _rglru_gates(ux, prev8, pv_ref, wa_ref, wi_ref):
    shifted = [ux] + [_shift_down(ux, prev8, k) for k in range(1, CONV_WIDTH)]
    xc = pv_ref[4:5, :] + pv_ref[3:4, :] * ux
    for k in range(1, CONV_WIDTH):
        xc = xc + pv_ref[3 - k:4 - k, :] * shifted[k]
    xcb = xc.astype(BF16)
    r_g = _sig(_block_mm(xcb, wa_ref, NN) + pv_ref[5:6, :])
    i_g = _sig(_block_mm(xcb, wi_ref, NN) + pv_ref[6:7, :])
    log_a = LRU_C * r_g * _log_sigmoid(pv_ref[7:8, :])
    a = jnp.exp(log_a)
    mm = _sqrt_one_minus_exp2x(log_a)
    return dict(shifted=shifted, xc=xc, xcb=xcb, r=r_g, i=i_g, a=a, mm=mm)


def _rglru_fwd(ux, ug, pv, wa, wi, t, name):
    lp, d = ux.shape

    def body(ux_ref, ug_ref, pv_ref, wa_ref, wi_ref, y_ref, h_ref, tail_s, hc_s):
        @pl.when(pl.program_id(0) == 0)
        def _():
            tail_s[...] = jnp.zeros_like(tail_s)
            hc_s[...] = jnp.zeros_like(hc_s)

        uxv = ux_ref[...]
        gt = _rglru_gates(uxv, tail_s[...], pv_ref, wa_ref, wi_ref)
        tail_s[...] = ux_ref[t - SUBLANES:t, :]
        h_ref[...] = _scan_rows(gt["a"], gt["mm"] * (gt["i"] * gt["xc"]), hc_s[0:1, :], False)
        hc_s[...] = h_ref[t - SUBLANES:t, :]
        hc_s[0:1, :] = h_ref[t - 1:t, :]
        y_ref[...] = (h_ref[...] * _gelu(ug_ref[...])[0]).astype(BF16)

    tile = pl.BlockSpec((t, d), lambda i: (i, 0))
    return pl.pallas_call(
        body, name=name, grid=(lp // t,),
        in_specs=[tile, tile, pl.BlockSpec(pv.shape, lambda i: (0, 0)),
                  pl.BlockSpec(wa.shape, lambda i: (0, 0, 0)), pl.BlockSpec(wi.shape, lambda i: (0, 0, 0))],
        out_specs=[tile, tile],
        out_shape=[jax.ShapeDtypeStruct((lp, d), BF16), jax.ShapeDtypeStruct((lp, d), F32)],
        scratch_shapes=[pltpu.VMEM((SUBLANES, d), F32), pltpu.VMEM((SUBLANES, d), F32)],
        compiler_params=_params(("arbitrary",)),
    )(ux, ug, pv, wa, wi)


def _rglru_bwd(ux, ug, hs, dy, pv, wa, wi, dep, t, name):
    lp, d = ux.shape
    nt = lp // t
    per = t // SUBLANES
    rb = d // RNN_BLOCKS

    def body(ux_ref, uxp_ref, ug_ref, h_ref, hp_ref, dy_ref, pv_ref, wa_ref, wi_ref, dep_ref,
             dux_ref, dug_ref, dpv_ref, dwa_ref, dwi_ref, ca_s, cg_s, cx_s):
        step = pl.program_id(0)
        first_tile = step == nt - 1

        @pl.when(step == 0)
        def _():
            for ref in (ca_s, cg_s, cx_s, dpv_ref, dwa_ref, dwi_ref):
                ref[...] = jnp.zeros_like(ref)

        uxv = ux_ref[...]
        prev8 = jnp.where(first_tile, 0.0, uxp_ref[...])
        hprev8 = jnp.where(first_tile, 0.0, hp_ref[...])
        gt = _rglru_gates(uxv, prev8, pv_ref, wa_ref, wi_ref)
        a, mm, r_g, i_g, xc = gt["a"], gt["mm"], gt["r"], gt["i"], gt["xc"]
        hv = h_ref[...]
        hprev = _shift_down(hv, hprev8, 1)
        ugv, dyv = ug_ref[...], dy_ref[...]
        gel, th = _gelu(ugv)
        dgel = 0.5 * (1.0 + th) + 0.5 * ugv * (1.0 - th * th) * (GELU_C * (1.0 + 3.0 * GELU_K * ugv * ugv))
        dug_ref[...] = (dyv * hv * dgel).astype(BF16)
        a_up = _shift_up(a, ca_s[...], 1)
        gv = _scan_rows(a_up, dyv * gel, cg_s[0:1, :], True)
        ca_s[...] = a[0:SUBLANES]
        cg_s[...] = gv[0:SUBLANES]
        ixc = i_g * xc
        d_ixc = gv * mm
        d_log_a = gv * hprev * a - (gv * ixc) * (a * a) / mm
        logsig = _log_sigmoid(pv_ref[7:8, :])
        d_pre_a = d_log_a * (LRU_C * logsig) * r_g * (1.0 - r_g)
        d_pre_i = d_ixc * xc * i_g * (1.0 - i_g)
        dab, dib = d_pre_a.astype(BF16), d_pre_i.astype(BF16)
        d_xc = d_ixc * i_g + _block_mm(dab, wa_ref, NT) + _block_mm(dib, wi_ref, NT)
        xcb = gt["xcb"]
        for h in range(RNN_BLOCKS):
            cols = slice(h * rb, (h + 1) * rb)
            dwa_ref[h] += lax.dot_general(xcb[:, cols], dab[:, cols], TN, preferred_element_type=F32)
            dwi_ref[h] += lax.dot_general(xcb[:, cols], dib[:, cols], TN, preferred_element_type=F32)
        csum = lambda v: jnp.sum(v, axis=0, keepdims=True)
        for k in range(CONV_WIDTH):
            dpv_ref[3 - k:4 - k, :] += csum(d_xc * gt["shifted"][k])
        dpv_ref[4:5, :] += csum(d_xc)
        dpv_ref[5:6, :] += csum(d_pre_a)
        dpv_ref[6:7, :] += csum(d_pre_i)
        dpv_ref[7:8, :] += csum(d_log_a * (LRU_C * r_g)) * _sig(-pv_ref[7:8, :])
        dux = pv_ref[3:4, :] * d_xc
        for k in range(1, CONV_WIDTH):
            dux = dux + pv_ref[3 - k:4 - k, :] * _shift_up(d_xc, cx_s[...], k)
        cx_s[...] = d_xc[0:SUBLANES]
        dux_ref[...] = dux.astype(BF16)

    rev = lambda i: (nt - 1 - i, 0)
    before = lambda i: (jnp.maximum((nt - 1 - i) * per - 1, 0), 0)
    tile = pl.BlockSpec((t, d), rev)
    tail = pl.BlockSpec((SUBLANES, d), before)
    fixed2 = lambda arr: pl.BlockSpec(arr.shape, lambda i: (0, 0))
    fixed3 = lambda arr: pl.BlockSpec(arr.shape, lambda i: (0, 0, 0))
    return pl.pallas_call(
        body, name=name, grid=(nt,),
        in_specs=[tile, tail, tile, tile, tail, tile, fixed2(pv), fixed3(wa), fixed3(wi), fixed2(dep)],
        out_specs=[tile, tile, fixed2(pv), fixed3(wa), fixed3(wi)],
        out_shape=[jax.ShapeDtypeStruct((lp, d), BF16), jax.ShapeDtypeStruct((lp, d), BF16),
                   jax.ShapeDtypeStruct(pv.shape, F32), jax.ShapeDtypeStruct(wa.shape, F32),
                   jax.ShapeDtypeStruct(wi.shape, F32)],
        scratch_shapes=[pltpu.VMEM((SUBLANES, d), F32)] * 3,
        compiler_params=_params(("arbitrary",)),
    )(ux, ux, ug, hs, hs, dy, pv, wa, wi, dep)


def _adamw(w, m, v, parts, dep, name):
    rows, cols = w.shape
    tr = _pick(rows, 256, SUBLANES)
    c1 = 1.0 / (1.0 - ADAM_B1 ** ADAM_STEP)
    c2 = 1.0 / (1.0 - ADAM_B2 ** ADAM_STEP)

    def body(w_ref, m_ref, v_ref, p_ref, dep_ref, g_ref, d_ref, mo_ref, vo_ref):
        g = p_ref[0].astype(F32)
        for q in range(1, N_DEV):
            g = g + p_ref[q].astype(F32)
        mn = ADAM_B1 * m_ref[...] + (1.0 - ADAM_B1) * g
        vn = ADAM_B2 * v_ref[...] + (1.0 - ADAM_B2) * (g * g)
        g_ref[...] = g
        mo_ref[...] = mn
        vo_ref[...] = vn
        d_ref[...] = -ADAM_LR * ((mn * c1) / (jnp.sqrt(vn * c2) + ADAM_EPS) + ADAM_WD * w_ref[...])

    blk = pl.BlockSpec((tr, cols), lambda i: (i, 0))
    return pl.pallas_call(
        body, name=name, grid=(rows // tr,),
        in_specs=[blk, blk, blk, pl.BlockSpec((N_DEV, tr, cols), lambda i: (0, i, 0)),
                  pl.BlockSpec(dep.shape, lambda i: (0, 0))],
        out_specs=[blk] * 4, out_shape=[jax.ShapeDtypeStruct((rows, cols), F32)] * 4,
        compiler_params=_params(("parallel",)),
    )(w, m, v, parts, dep)


WEIGHTS = ("meta_tokens", "norm_mix_g", "w_in", "b_gate", "conv_w", "conv_b", "w_rec_a", "b_rec_a", "w_rec_i",
           "b_rec_i", "lru_lambda", "q_norm_g", "w_uq", "kv_norm_g", "w_ukv", "w_branch", "w_out", "norm_ffn_g",
           "w_ffn_in", "w_ffn_out", "final_norm_g")
SHARDED = {"meta_tokens": True, "w_in": True, "b_gate": True, "conv_w": True, "w_uq": True, "w_ukv": True,
           "w_branch": False, "w_out": False, "w_ffn_in": True, "w_ffn_out": False}


def _as2d(a):
    return a.reshape(-1, a.shape[-1])


def _full_from_gathered(g, by_cols):
    if by_cols:
        return jnp.transpose(g, (1, 0, 2)).reshape(g.shape[1], N_DEV * g.shape[2])
    return g.reshape(N_DEV * g.shape[1], g.shape[2])


def _blocks_from_full(full, by_cols):
    if by_cols:
        r, c = full.shape
        return jnp.transpose(full.reshape(r, N_DEV, c // N_DEV), (1, 0, 2))
    return full.reshape(N_DEV, full.shape[0] // N_DEV, full.shape[1])


def kernel(x, meta_tokens, norm_mix_g, w_in, b_gate, conv_w, conv_b, w_rec_a, b_rec_a, w_rec_i, b_rec_i, lru_lambda, q_norm_g, w_uq, kv_norm_g, w_ukv, w_branch, w_out, norm_ffn_g, w_ffn_in, w_ffn_out, final_norm_g, loss_target, m_meta_tokens, m_norm_mix_g, m_w_in, m_b_gate, m_conv_w, m_conv_b, m_w_rec_a, m_b_rec_a, m_w_rec_i, m_b_rec_i, m_lru_lambda, m_q_norm_g, m_w_uq, m_kv_norm_g, m_w_ukv, m_w_branch, m_w_out, m_norm_ffn_g, m_w_ffn_in, m_w_ffn_out, m_final_norm_g, v_meta_tokens, v_norm_mix_g, v_w_in, v_b_gate, v_conv_w, v_conv_b, v_w_rec_a, v_b_rec_a, v_w_rec_i, v_b_rec_i, v_lru_lambda, v_q_norm_g, v_w_uq, v_kv_norm_g, v_w_ukv, v_w_branch, v_w_out, v_norm_ffn_g, v_w_ffn_in, v_w_ffn_out, v_final_norm_g):
    w = dict(meta_tokens=meta_tokens, norm_mix_g=norm_mix_g, w_in=w_in, b_gate=b_gate, conv_w=conv_w, conv_b=conv_b,
             w_rec_a=w_rec_a, b_rec_a=b_rec_a, w_rec_i=w_rec_i, b_rec_i=b_rec_i, lru_lambda=lru_lambda,
             q_norm_g=q_norm_g, w_uq=w_uq, kv_norm_g=kv_norm_g, w_ukv=w_ukv, w_branch=w_branch, w_out=w_out,
             norm_ffn_g=norm_ffn_g, w_ffn_in=w_ffn_in, w_ffn_out=w_ffn_out, final_norm_g=final_norm_g)
    m = dict(meta_tokens=m_meta_tokens, norm_mix_g=m_norm_mix_g, w_in=m_w_in, b_gate=m_b_gate, conv_w=m_conv_w,
             conv_b=m_conv_b, w_rec_a=m_w_rec_a, b_rec_a=m_b_rec_a, w_rec_i=m_w_rec_i, b_rec_i=m_b_rec_i,
             lru_lambda=m_lru_lambda, q_norm_g=m_q_norm_g, w_uq=m_w_uq, kv_norm_g=m_kv_norm_g, w_ukv=m_w_ukv,
             w_branch=m_w_branch, w_out=m_w_out, norm_ffn_g=m_norm_ffn_g, w_ffn_in=m_w_ffn_in,
             w_ffn_out=m_w_ffn_out, final_norm_g=m_final_norm_g)
    v = dict(meta_tokens=v_meta_tokens, norm_mix_g=v_norm_mix_g, w_in=v_w_in, b_gate=v_b_gate, conv_w=v_conv_w,
             conv_b=v_conv_b, w_rec_a=v_w_rec_a, b_rec_a=v_b_rec_a, w_rec_i=v_w_rec_i, b_rec_i=v_b_rec_i,
             lru_lambda=v_lru_lambda, q_norm_g=v_q_norm_g, w_uq=v_w_uq, kv_norm_g=v_kv_norm_g, w_ukv=v_w_ukv,
             w_branch=v_w_branch, w_out=v_w_out, norm_ffn_g=v_norm_ffn_g, w_ffn_in=v_w_ffn_in,
             w_ffn_out=v_w_ffn_out, final_norm_g=v_final_norm_g)

    seq, d_model = x.shape[1], x.shape[2]
    length = N_META + seq
    lp = -(-length // LANES) * LANES
    t_attn = _pick(lp, 640)
    t_rnn = LANES

    small = ("meta_tokens", "b_gate", "conv_w")
    names = list(SHARDED)
    mid, late = ("w_uq", "w_ukv", "w_branch", "w_out"), ("w_ffn_in", "w_ffn_out")
    payload = lambda n: _as2d(w[n]) if n in small else _as2d(w[n]).astype(BF16)
    got = _exchange([payload(n) for n in small], [True] * len(small), "gather_small")
    w_in_blocks = _gather_two_level(payload("w_in"), "gather_in")
    mid_h = _exchange_start([payload(n) for n in mid], [True] * len(mid), w_in_blocks, "gather_mid_start")
    late_h = _exchange_start([payload(n) for n in late], [True] * len(late), mid_h["token"], "gather_late_start")
    full = {n: _full_from_gathered(g, SHARDED[n]) for n, g in zip(small, got)}
    h0 = jnp.concatenate([full["meta_tokens"], x[0], jnp.zeros((lp - length, d_model), F32)], axis=0)
    z = _rmsnorm_fwd(h0, norm_mix_g, "norm_mix")
    full["w_in"] = _full_from_gathered(w_in_blocks, True)

    splits = (D_RNN, D_RNN, Q_RANK, KV_RANK, QK_ROPE, 2 * d_model)
    offs = [0]
    for s in splits:
        offs.append(offs[-1] + s)
    w_x, w_g, w_q, w_kv, w_kr, w_m = (full["w_in"][:, offs[s]:offs[s + 1]] for s in range(6))
    w_kr = jnp.pad(w_kr, ((0, 0), (0, LANES - QK_ROPE)))
    bg = full["b_gate"].reshape(1, 2 * d_model)
    pv = jnp.concatenate([full["conv_w"], conv_b, b_rec_a, b_rec_i, lru_lambda], axis=0)
    wa_b, wi_b = w_rec_a[0].astype(BF16), w_rec_i[0].astype(BF16)
    g_final = final_norm_g.reshape(1, d_model)

    tgt = jnp.pad(loss_target[0], ((N_META, lp - length), (0, 0)))
    tabs = _rope_tables(lp)

    ux = _mm([(z, w_x)], "nn", "in_x")
    ug = _mm([(z, w_g)], "nn", "in_g")
    uq = _mm([(z, w_q)], "nn", "in_q")
    ukv = _mm([(z, w_kv)], "nn", "in_kv")
    ukr = _mm([(z, w_kr)], "nn", "in_kr")
    um = _mm([(z, w_m)], "nn", "in_m", out_dtype=BF16)
    for n, g in zip(mid, _exchange_wait(mid_h, um, "gather_mid_wait")):
        full[n] = _full_from_gathered(g, SHARDED[n])
    w_uq_pad = jnp.pad(full["w_uq"].reshape(Q_RANK, N_HEADS, QK_NOPE + QK_ROPE),
                       ((0, 0), (0, 0), (0, HEAD_PAD - QK_NOPE - QK_ROPE))).reshape(Q_RANK, N_HEADS * HEAD_PAD)
    w_ukv3 = full["w_ukv"].reshape(KV_RANK, N_HEADS, QK_NOPE + V_DIM)
    w_k_pad = jnp.pad(w_ukv3[:, :, :QK_NOPE], ((0, 0), (0, 0), (0, HEAD_PAD - QK_NOPE))).reshape(
        KV_RANK, N_HEADS * HEAD_PAD)
    w_v = w_ukv3[:, :, QK_NOPE:].reshape(KV_RANK, N_HEADS * V_DIM)
    wb_r, wb_a = full["w_branch"][:D_RNN], full["w_branch"][D_RNN:]
    y_rnn, hs = _rglru_fwd(ux, ug, pv, wa_b, wi_b, t_rnn, "rglru_fwd")
    qn = _rmsnorm_fwd(uq, q_norm_g, "norm_q")
    kvn = _rmsnorm_fwd(ukv, kv_norm_g, "norm_kv")
    qh = _up_rope(qn, w_uq_pad, tabs, "up_q")
    kh = _up_rope(kvn, w_k_pad, tabs, "up_k", ukr=ukr)
    vh = _mm([(kvn, w_v)], "nn", "up_v", out_dtype=BF16)
    oh, lse = _attn_fwd(qh, kh, vh, t_attn, "attn_fwd")
    p_rnn = _mm([(y_rnn, wb_r)], "nn", "branch_rnn", out_dtype=BF16)
    p_att = _mm([(oh, wb_a)], "nn", "branch_att", out_dtype=BF16)
    mixed, h1 = _mix_out(um, p_rnn, p_att, bg, full["w_out"], h0, "out_proj")
    for n, g in zip(late, _exchange_wait(late_h, h1, "gather_late_wait")):
        full[n] = _full_from_gathered(g, SHARDED[n])
    zf = _rmsnorm_fwd(h1, norm_ffn_g, "norm_ffn")
    gate, up, act = _ffn_in_swiglu(zf, full["w_ffn_in"], "ffn_in")
    h2 = _mm([(act, full["w_ffn_out"])], "nn", "ffn_out", res=h1)
    dh2, dg_final, _, loss_part = _loss_bwd(h2, tgt, g_final, seq, "loss_bwd")

    d_gate, d_up = _ffn_out_bwd_swiglu(dh2, full["w_ffn_out"], gate, up, "d_gate_up")
    dw_ffn_out = _mm_tn(act, dh2, "dw_ffn_out")
    dw_ffn_in = jnp.concatenate([_mm_tn(zf, d_gate, "dw_ffn_gate"), _mm_tn(zf, d_up, "dw_ffn_up")], axis=1)
    blocks = lambda n, g: _blocks_from_full(g, SHARDED[n]).astype(F32 if n in small else BF16)
    sent = {("w_ffn_in", "w_ffn_out"): _exchange_start(
        [blocks("w_ffn_in", dw_ffn_in), blocks("w_ffn_out", dw_ffn_out)], [False] * 2, dg_final, "scatter_ffn_start")}
    d_zf = _mm([(d_gate, full["w_ffn_in"], D_FF, 0), (d_up, full["w_ffn_in"], D_FF, 1)], "nt", "d_zf")
    dh1, dg_ffn = _rmsnorm_bwd(h1, d_zf, norm_ffn_g, "norm_ffn_bwd", res=dh2,
                               dep=sent[("w_ffn_in", "w_ffn_out")]["token"])
    d_mixed = _mm([(dh1, full["w_out"])], "nt", "d_mixed", out_dtype=BF16)
    dw_out = _mm_tn(mixed, dh1, "dw_out")
    d_prnn, d_patt, d_um, dbg = _mix_bwd(um, p_rnn, p_att, d_mixed, bg, "mix_bwd")
    d_yrnn = _mm([(d_prnn, wb_r)], "nt", "d_yrnn")
    d_oh = _mm([(d_patt, wb_a)], "nt", "d_oh", out_dtype=BF16)
    dwb_r = _mm_tn(y_rnn, d_prnn, "dw_branch_rnn")
    dwb_a = _mm_tn(oh, d_patt, "dw_branch_att")
    sent[("w_out", "w_branch")] = _exchange_start(
        [blocks("w_out", dw_out), blocks("w_branch", jnp.concatenate([dwb_r, dwb_a], axis=0))], [False] * 2,
        dg_ffn, "scatter_mix_start")
    dqh, l2row, dlrow = _attn_bwd_dq(qh, kh, vh, d_oh, oh, lse, sent[("w_out", "w_branch")]["token"], t_attn,
                                     "attn_bwd_dq")
    dkh, dvh = _attn_bwd_dkv(qh, kh, vh, d_oh, l2row, dlrow, t_attn, "attn_bwd_dkv")
    dqpad, dkpad, dukr = _rope_bwd(dqh, dkh, tabs, "rope_bwd")
    d_qn = _mm([(dqpad, w_uq_pad)], "nt", "d_qn")
    dw_uq_pad = _mm_tn(qn, dqpad, "dw_uq")
    d_kvn = _mm([(dkpad, w_k_pad), (dvh, w_v)], "nt", "d_kvn")
    dw_k_pad = _mm_tn(kvn, dkpad, "dw_uk")
    dw_v = _mm_tn(kvn, dvh, "dw_uv")
    dw_uq = dw_uq_pad.reshape(Q_RANK, N_HEADS, HEAD_PAD)[:, :, :QK_NOPE + QK_ROPE].reshape(Q_RANK, -1)
    dw_ukv = jnp.concatenate([dw_k_pad.reshape(KV_RANK, N_HEADS, HEAD_PAD)[:, :, :QK_NOPE],
                              dw_v.reshape(KV_RANK, N_HEADS, V_DIM)], axis=2).reshape(KV_RANK, -1)
    sent[("w_uq", "w_ukv")] = _exchange_start([blocks("w_uq", dw_uq), blocks("w_ukv", dw_ukv)], [False] * 2,
                                              dbg, "scatter_attn_start")
    duq, dg_q = _rmsnorm_bwd(uq, d_qn, q_norm_g, "norm_q_bwd", out_dtype=BF16)
    dukv, dg_kv = _rmsnorm_bwd(ukv, d_kvn, kv_norm_g, "norm_kv_bwd", out_dtype=BF16)
    dux, dug, dpv, dwa, dwi = _rglru_bwd(ux, ug, hs, d_yrnn, pv, wa_b, wi_b, sent[("w_uq", "w_ukv")]["token"],
                                         t_rnn, "rglru_bwd")
    grad_rep = dict(
        conv_b=dpv[4:5], w_rec_a=dwa, b_rec_a=dpv[5:6], w_rec_i=dwi, b_rec_i=dpv[6:7], lru_lambda=dpv[7:8],
        q_norm_g=dg_q, kv_norm_g=dg_kv, norm_ffn_g=dg_ffn, final_norm_g=dg_final)
    rep_now = tuple(grad_rep)
    sent[("b_gate", "conv_w") + rep_now] = _exchange_start(
        [blocks("b_gate", dbg.reshape(2, d_model)), blocks("conv_w", dpv[:CONV_WIDTH])]
        + [_as2d(grad_rep[n]).astype(BF16 if n in ("w_rec_a", "w_rec_i") else F32) for n in rep_now],
        [False] * 2 + [True] * len(rep_now), dg_kv, "scatter_small_start")
    d_z = _mm([(dux, w_x), (dug, w_g), (duq, w_q), (dukv, w_kv), (dukr, w_kr), (d_um, w_m)], "nt", "d_z")
    dw_in = jnp.concatenate([
        _mm_tn(z, dux, "dw_in_x"), _mm_tn(z, dug, "dw_in_g"), _mm_tn(z, duq, "dw_in_q"),
        _mm_tn(z, dukv, "dw_in_kv"), _mm_tn(z, dukr, "dw_in_kr")[:, :QK_ROPE], _mm_tn(z, d_um, "dw_in_m")], axis=1)
    last_h = _exchange_start([blocks("w_in", dw_in)], [False], sent[("b_gate", "conv_w") + rep_now]["token"],
                             "scatter_in_start")
    sent[("w_in",)] = last_h
    dh0, dg_mix = _rmsnorm_bwd(h0, d_z, norm_mix_g, "norm_mix_bwd", res=dh1, dep=last_h["token"])
    sent[("meta_tokens", "norm_mix_g")] = _exchange_start([blocks("meta_tokens", dh0[:N_META]), dg_mix], [False, True],
                                                          dg_mix, "scatter_tail_start")
    grads, deltas, new_m, new_v = {}, {}, {}, {}

    def update(n, parts, dep):
        g2, d2, m2, v2 = _adamw(_as2d(w[n]), _as2d(m[n]), _as2d(v[n]), parts, dep, "adamw_" + n)
        for store, val in ((grads, g2), (deltas, d2), (new_m, m2), (new_v, v2)):
            store[n] = val.reshape(w[n].shape)

    chain = sent[("meta_tokens", "norm_mix_g")]["token"]
    for group, handle in sent.items():
        for n, parts in zip(group, _exchange_wait(handle, chain, "scatter_wait_" + group[0])):
            update(n, parts, chain)
            chain = _as2d(deltas[n])[:SUBLANES, :LANES]

    loss = lax.psum(loss_part[0, 0], MESH_AXES)
    grad_x = dh0[N_META:length][None]
    return (loss, grad_x, *[grads[n] for n in WEIGHTS], *[deltas[n] for n in WEIGHTS],
            *[new_m[n] for n in WEIGHTS], *[new_v[n] for n in WEIGHTS])
```

```python
import math

import jax
import jax.numpy as jnp
from jax import lax
from jax.experimental import pallas as pl
from jax.experimental.pallas import tpu as pltpu

F32 = jnp.float32
BF16 = jnp.bfloat16

N_DEV = 8
MESH_AXES = ("x", "y", "c")
LANES = 128
SUBLANES = 8
VMEM_LIMIT = 56 * 1024 * 1024

N_META = 16
CHUNK_SHIFT = 6
CHUNK_BIAS = 64 - N_META
EPS = 1e-6
D_RNN = 1280
RNN_BLOCKS = 10
CONV_WIDTH = 4
LRU_C = 8.0
N_HEADS = 8
QK_NOPE = 128
QK_ROPE = 64
V_DIM = 128
HEAD_PAD = 256
Q_RANK = 384
KV_RANK = 256
ROPE_THETA = 10000.0
ATTN_SCALE = 1.0 / math.sqrt(QK_NOPE + QK_ROPE)
NEG = -1e30
TINY = 1e-30
LOG2E = 1.0 / math.log(2.0)
SCALE_LOG2E = ATTN_SCALE * LOG2E
ROW_GROUP = 32
SPILL = LANES
D_FF = 2816

ADAM_LR = 0.001
ADAM_B1 = 0.9
ADAM_B2 = 0.999
ADAM_EPS = 1e-08
ADAM_WD = 0.01
ADAM_STEP = 10

NN = (((1,), (0,)), ((), ()))
NT = (((1,), (1,)), ((), ()))
TN = (((0,), (0,)), ((), ()))


def _pick(n, cap, base=LANES):
    best = None
    for t in range(base, min(n, cap) + 1, base):
        if n % t == 0:
            best = t
    return best if best is not None else n


def _params(sem=None):
    return pltpu.CompilerParams(dimension_semantics=sem, vmem_limit_bytes=VMEM_LIMIT)


def _sig(x):
    return 0.5 + 0.5 * jnp.tanh(0.5 * x)


def _exchange(srcs, gather, name):
    n = len(srcs)
    out_shape = [jax.ShapeDtypeStruct((N_DEV,) + (s.shape if g else s.shape[1:]), s.dtype)
                 for s, g in zip(srcs, gather)]

    def body(*refs):
        src, dst = refs[:n], refs[n:2 * n]
        send_sems, recv_sems, local_sems = refs[2 * n:]
        x, y, c = lax.axis_index("x"), lax.axis_index("y"), lax.axis_index("c")
        me = 4 * x + 2 * y + c
        local = []
        for t in range(n):
            cp = pltpu.make_async_copy(src[t] if gather[t] else src[t].at[me], dst[t].at[me], local_sems.at[t])
            cp.start()
            local.append(cp)
        sends, recvs = [], []
        for k in range(1, N_DEV):
            px = 1 - x if k & 4 else x
            py = 1 - y if k & 2 else y
            pc = 1 - c if k & 1 else c
            peer = 4 * px + 2 * py + pc
            for t in range(n):
                cp = pltpu.make_async_remote_copy(
                    src_ref=src[t] if gather[t] else src[t].at[peer], dst_ref=dst[t].at[me],
                    send_sem=send_sems.at[t, k - 1], recv_sem=recv_sems.at[t, k - 1],
                    device_id=(px, py, pc), device_id_type=pl.DeviceIdType.MESH)
                cp.start()
                sends.append(cp)
                recvs.append(pltpu.make_async_remote_copy(
                    src_ref=src[t] if gather[t] else src[t].at[peer], dst_ref=dst[t].at[peer],
                    send_sem=send_sems.at[t, k - 1], recv_sem=recv_sems.at[t, k - 1],
                    device_id=(px, py, pc), device_id_type=pl.DeviceIdType.MESH))
        for cp in recvs:
            cp.wait_recv()
        for cp in sends:
            cp.wait_send()
        for cp in local:
            cp.wait()

    any_spec = pl.BlockSpec(memory_space=pl.ANY)
    return pl.pallas_call(
        body, name=name, out_shape=out_shape,
        in_specs=[any_spec] * n, out_specs=[any_spec] * n,
        scratch_shapes=[pltpu.SemaphoreType.DMA((n, N_DEV - 1)), pltpu.SemaphoreType.DMA((n, N_DEV - 1)),
                        pltpu.SemaphoreType.DMA((n,))],
    )(*srcs)


def _gather_two_level(block, name):
    def body(x_ref, out_ref, send_sems, recv_sems, local_sem):
        x, y, c = lax.axis_index("x"), lax.axis_index("y"), lax.axis_index("c")
        me, sibling = (x, y, c), (x, y, 1 - c)
        chips = [(1 - x, y), (x, 1 - y), (1 - x, 1 - y)]

        def slot(px, py, pc):
            return out_ref.at[4 * px + 2 * py + pc]

        def copy(k, owner, to, src=None):
            return pltpu.make_async_remote_copy(
                src_ref=slot(*owner) if src is None else src, dst_ref=slot(*owner),
                send_sem=send_sems.at[k], recv_sem=recv_sems.at[k], device_id=to,
                device_id_type=pl.DeviceIdType.MESH)

        mine = pltpu.make_async_copy(x_ref, slot(*me), local_sem)
        mine.start()
        first = [copy(0, me, sibling, src=x_ref)] + [copy(1 + j, me, (*chip, c), src=x_ref)
                                                     for j, chip in enumerate(chips)]
        for cp in first:
            cp.start()
        passed = [copy(4 + j, (*chip, c), sibling) for j, chip in enumerate(chips)]
        for j, chip in enumerate(chips):
            copy(1 + j, (*chip, c), me).wait_recv()
            passed[j].start()
        copy(0, sibling, me).wait_recv()
        for j, chip in enumerate(chips):
            copy(4 + j, (*chip, 1 - c), me).wait_recv()
        for cp in first + passed:
            cp.wait_send()
        mine.wait()

    any_spec = pl.BlockSpec(memory_space=pl.ANY)
    return pl.pallas_call(
        body, name=name, out_shape=jax.ShapeDtypeStruct((N_DEV,) + block.shape, block.dtype),
        in_specs=[any_spec], out_specs=any_spec,
        scratch_shapes=[pltpu.SemaphoreType.DMA((N_DEV - 1,)), pltpu.SemaphoreType.DMA((N_DEV - 1,)),
                        pltpu.SemaphoreType.DMA],
    )(block)


HBM_SPEC = pl.BlockSpec(memory_space=pltpu.HBM)
SEM_SPEC = pl.BlockSpec(memory_space=pltpu.SEMAPHORE)
DATAFLOW = pltpu.SideEffectType.DATAFLOW_SIDE_EFFECTING


def _peers(x, y, c):
    out = []
    for k in range(1, N_DEV):
        px = 1 - x if k & 4 else x
        py = 1 - y if k & 2 else y
        pc = 1 - c if k & 1 else c
        out.append((k, (px, py, pc), 4 * px + 2 * py + pc))
    return out


def _split_copies(src, land, gather, send_sems, recv_sems, local_sems):
    x, y, c = lax.axis_index("x"), lax.axis_index("y"), lax.axis_index("c")
    me = 4 * x + 2 * y + c
    n = len(src)
    local = [pltpu.make_async_copy(src[t] if gather[t] else src[t].at[me], land[t].at[me], local_sems.at[t])
             for t in range(n)]
    sends, recvs = [], []
    for k, pos, peer in _peers(x, y, c):
        for t in range(n):
            mine = src[t] if gather[t] else src[t].at[peer]
            slot = t * (N_DEV - 1) + k - 1
            common = dict(send_sem=send_sems.at[slot], recv_sem=recv_sems.at[slot], device_id=pos,
                          device_id_type=pl.DeviceIdType.MESH)
            sends.append(pltpu.make_async_remote_copy(src_ref=mine, dst_ref=land[t].at[me], **common))
            recvs.append(pltpu.make_async_remote_copy(src_ref=mine, dst_ref=land[t].at[peer], **common))
    return local, sends, recvs


def _exchange_start(srcs, gather, after, name):
    n = len(srcs)
    lands = [lax.empty((N_DEV,) + (s.shape if g else s.shape[1:]), s.dtype) for s, g in zip(srcs, gather)]

    def body(*refs):
        src, land = refs[:n], refs[n:2 * n]
        send_sems, recv_sems, local_sems = refs[2 * n + 1:2 * n + 4]
        local, sends, _ = _split_copies(src, land, gather, send_sems, recv_sems, local_sems)
        for cp in local + sends:
            cp.start()
        refs[-1][...] = jnp.zeros_like(refs[-1])

    hbm = lambda a: pltpu.HBM(a.shape, a.dtype)
    outs = pl.pallas_call(
        body, name=name,
        out_shape=(pltpu.SemaphoreType.DMA((n * (N_DEV - 1),)), pltpu.SemaphoreType.DMA((n * (N_DEV - 1),)),
                   pltpu.SemaphoreType.DMA((n,)), *[hbm(s) for s in srcs], *[hbm(a) for a in lands],
                   jax.ShapeDtypeStruct((SUBLANES, LANES), F32)),
        in_specs=[HBM_SPEC] * (2 * n) + [pl.BlockSpec(memory_space=pl.ANY)],
        out_specs=(SEM_SPEC, SEM_SPEC, SEM_SPEC, *[HBM_SPEC] * (2 * n), pl.BlockSpec(memory_space=pltpu.VMEM)),
        input_output_aliases={t: 3 + t for t in range(2 * n)},
        compiler_params=pltpu.CompilerParams(has_side_effects=DATAFLOW),
    )(*[pltpu.with_memory_space_constraint(a, pltpu.HBM) for a in list(srcs) + lands], after)
    return dict(sems=outs[:3], srcs=outs[3:3 + n], lands=outs[3 + n:3 + 2 * n], token=outs[-1], gather=gather)


def _exchange_wait(handle, after, name):
    srcs, lands, gather = handle["srcs"], handle["lands"], handle["gather"]
    n = len(srcs)

    def body(*refs):
        src, land = refs[:n], refs[n:2 * n]
        send_sems, recv_sems, local_sems = refs[2 * n:2 * n + 3]
        local, sends, recvs = _split_copies(src, land, gather, send_sems, recv_sems, local_sems)
        for cp in sends:
            cp.wait_send()
        for cp in recvs:
            cp.wait_recv()
        for cp in local:
            cp.wait()

    hbm = lambda a: pltpu.HBM(a.shape, a.dtype)
    outs = pl.pallas_call(
        body, name=name, out_shape=(*[hbm(s) for s in srcs], *[hbm(a) for a in lands]),
        in_specs=[HBM_SPEC] * (2 * n) + [SEM_SPEC] * 3 + [pl.BlockSpec(memory_space=pl.ANY)],
        out_specs=[HBM_SPEC] * (2 * n), input_output_aliases={t: t for t in range(2 * n)},
        compiler_params=pltpu.CompilerParams(has_side_effects=DATAFLOW),
    )(*srcs, *lands, *handle["sems"], after)
    return outs[n:]


def _mm(pairs, mode, name, res=None, out_dtype=F32):
    pairs = [p if len(p) == 4 else (p[0], p[1], p[0].shape[1], 0) for p in pairs]
    m = pairs[0][0].shape[0]
    n = pairs[0][1].shape[1] if mode == "nn" else pairs[0][1].shape[0]
    tm, tn = _pick(m, 640), _pick(n, 1408)
    np_ = len(pairs)
    dims = NN if mode == "nn" else NT

    def body(*refs):
        acc = None
        for s in range(np_):
            d = lax.dot_general(refs[2 * s][...].astype(BF16), refs[2 * s + 1][...].astype(BF16), dims,
                                preferred_element_type=F32)
            acc = d if acc is None else acc + d
        if res is not None:
            acc = acc + refs[2 * np_][...]
        refs[-1][...] = acc.astype(out_dtype)

    in_specs, args = [], []
    for a, b, kt, kb in pairs:
        in_specs.append(pl.BlockSpec((tm, kt), lambda i, j: (i, 0)))
        if mode == "nn":
            in_specs.append(pl.BlockSpec((kt, tn), lambda i, j, kb=kb: (kb, j)))
        else:
            in_specs.append(pl.BlockSpec((tn, kt), lambda i, j, kb=kb: (j, kb)))
        args += [a, b]
    if res is not None:
        in_specs.append(pl.BlockSpec((tm, tn), lambda i, j: (i, j)))
        args.append(res)
    return pl.pallas_call(
        body, name=name, grid=(m // tm, n // tn), in_specs=in_specs,
        out_specs=pl.BlockSpec((tm, tn), lambda i, j: (i, j)),
        out_shape=jax.ShapeDtypeStruct((m, n), out_dtype),
        compiler_params=_params(("parallel", "parallel")),
    )(*args)


def _mm_tn(a, b, name):
    m, k = a.shape
    n = b.shape[1]
    tm, tk, tn = _pick(m, 1664), _pick(k, 1408), _pick(n, 1408)

    def body(a_ref, b_ref, o_ref):
        @pl.when(pl.program_id(2) == 0)
        def _():
            o_ref[...] = jnp.zeros_like(o_ref)

        o_ref[...] += lax.dot_general(a_ref[...].astype(BF16), b_ref[...].astype(BF16), TN,
                                      preferred_element_type=F32)

    return pl.pallas_call(
        body, name=name, grid=(k // tk, n // tn, m // tm),
        in_specs=[pl.BlockSpec((tm, tk), lambda i, j, r: (r, i)), pl.BlockSpec((tm, tn), lambda i, j, r: (r, j))],
        out_specs=pl.BlockSpec((tk, tn), lambda i, j, r: (i, j)),
        out_shape=jax.ShapeDtypeStruct((k, n), F32),
        compiler_params=_params(("parallel", "parallel", "arbitrary")),
    )(a, b)


ROW_TILE_BYTES = 6 * 1024 * 1024


def _row_tile(rows, row_in, row_out):
    per_row = sum((r[1] * r[0].dtype.itemsize) if isinstance(r, tuple) else (r.shape[1] * r.dtype.itemsize)
                  for r in row_in)
    per_row += sum(w * jnp.dtype(dt).itemsize for w, dt in row_out)
    return _pick(rows, min(640, max(LANES, ROW_TILE_BYTES // per_row)))


def _rowcall(body, name, rows, row_in, full_in, row_out, acc_out=()):
    tr = _row_tile(rows, row_in, row_out)
    n_steps = rows // tr
    in_specs, args = [], []
    for r in row_in:
        arr, w, cb = r if isinstance(r, tuple) else (r, r.shape[1], 0)
        in_specs.append(pl.BlockSpec((tr, w), lambda i, cb=cb: (i, cb)))
        args.append(arr)
    for f in full_in:
        in_specs.append(pl.BlockSpec(f.shape, lambda i, nd=f.ndim: (0,) * nd))
        args.append(f)
    out_specs = [pl.BlockSpec((tr, w), lambda i: (i, 0)) for w, _ in row_out]
    out_shape = [jax.ShapeDtypeStruct((rows, w), dt) for w, dt in row_out]
    for shp, dt in acc_out:
        out_specs.append(pl.BlockSpec(shp, lambda i, nd=len(shp): (0,) * nd))
        out_shape.append(jax.ShapeDtypeStruct(shp, dt))

    def wrapped(*refs):
        body(pl.program_id(0), n_steps, *refs)

    return pl.pallas_call(
        wrapped, name=name, grid=(n_steps,), in_specs=in_specs, out_specs=out_specs, out_shape=out_shape,
        compiler_params=_params(("arbitrary",) if acc_out else ("parallel",)),
    )(*args)


def _rmsnorm_fwd(x, g, name):
    rows, w = x.shape

    def body(i, n, x_ref, g_ref, o_ref):
        xv = x_ref[...]
        r = lax.rsqrt(jnp.mean(xv * xv, axis=-1, keepdims=True) + EPS)
        o_ref[...] = (xv * r * g_ref[...]).astype(BF16)

    return _rowcall(body, name, rows,[x], [g], [(w, BF16)])[0]


def _rmsnorm_bwd_math(xv, dy, g):
    w = xv.shape[-1]
    r = lax.rsqrt(jnp.mean(xv * xv, axis=-1, keepdims=True) + EPS)
    t = dy * g
    dx = r * t - xv * (r * r * r * (jnp.sum(t * xv, axis=-1, keepdims=True) / w))
    dg = jnp.sum(dy * xv * r, axis=0, keepdims=True)
    return dx, dg


def _rmsnorm_bwd(x, dy, g, name, res=None, out_dtype=F32, dep=None):
    rows, w = x.shape

    def body(i, n, *refs):
        x_ref, dy_ref = refs[0], refs[1]
        g_ref, dx_ref, dg_ref = refs[-3], refs[-2], refs[-1]
        dx, dg = _rmsnorm_bwd_math(x_ref[...], dy_ref[...], g_ref[...])
        if res is not None:
            dx = dx + refs[2][...]
        dx_ref[...] = dx.astype(out_dtype)

        @pl.when(i == 0)
        def _():
            dg_ref[...] = jnp.zeros_like(dg_ref)

        dg_ref[...] += dg

    row_in = [x, dy] + ([res] if res is not None else [])
    return _rowcall(body, name, rows, row_in, ([dep] if dep is not None else []) + [g], [(w, out_dtype)],
                    [((1, w), F32)])


def _loss_bwd(h2, tgt, g, seq, name):
    rows, w = h2.shape
    tr = _row_tile(rows, [h2, tgt], [(w, F32)])

    def body(i, n, h_ref, t_ref, g_ref, dh_ref, dg_ref, lcol_ref, loss_ref):
        hv, gv = h_ref[...], g_ref[...]
        row = i * tr + lax.broadcasted_iota(jnp.int32, (tr, w), 0)
        valid = jnp.logical_and(row >= N_META, row < N_META + seq)
        r = lax.rsqrt(jnp.mean(hv * hv, axis=-1, keepdims=True) + EPS)
        err = jnp.where(valid, hv * r * gv - t_ref[...], 0.0)
        dx, dg = _rmsnorm_bwd_math(hv, err * (1.0 / w), gv)
        dh_ref[...] = dx

        @pl.when(i == 0)
        def _():
            dg_ref[...] = jnp.zeros_like(dg_ref)
            lcol_ref[...] = jnp.zeros_like(lcol_ref)

        dg_ref[...] += dg
        lcol_ref[...] += jnp.sum(err * err, axis=0, keepdims=True)

        @pl.when(i == n - 1)
        def _():
            total = jnp.sum(lcol_ref[...], axis=1, keepdims=True) * (0.5 / w)
            loss_ref[...] = jnp.broadcast_to(total, loss_ref.shape)

    return _rowcall(body, name, rows, [h2, tgt], [g], [(w, F32)],
                    [((1, w), F32), ((1, w), F32), ((1, LANES), F32)])


def _mix_out(um, p_rnn, p_att, bg, w_out, res, name):
    rows, d = p_rnn.shape
    tr = _pick(rows, 640)

    def body(u0_ref, u1_ref, pr_ref, pa_ref, bg_ref, w_ref, r_ref, mix_ref, o_ref):
        g0 = _sig(u0_ref[...].astype(F32) + bg_ref[:, :d])
        g1 = _sig(u1_ref[...].astype(F32) + bg_ref[:, d:])
        mixed = (g0 * pr_ref[...].astype(F32) + g1 * pa_ref[...].astype(F32)).astype(BF16)
        mix_ref[...] = mixed
        o_ref[...] = r_ref[...] + jnp.dot(mixed, w_ref[...], preferred_element_type=F32)

    row = pl.BlockSpec((tr, d), lambda i: (i, 0))
    whole = lambda a: pl.BlockSpec(a.shape, lambda i: (0, 0))
    return pl.pallas_call(
        body, name=name, grid=(rows // tr,),
        in_specs=[row, pl.BlockSpec((tr, d), lambda i: (i, 1)), row, row, whole(bg), whole(w_out), row],
        out_specs=[row, row],
        out_shape=[jax.ShapeDtypeStruct((rows, d), BF16), jax.ShapeDtypeStruct((rows, d), F32)],
        compiler_params=_params(("parallel",)),
    )(um, um, p_rnn, p_att, bg, w_out, res)


def _mix_bwd(um, p_rnn, p_att, dmix, bg, name):
    rows, d = p_rnn.shape

    def body(i, n, u0_ref, u1_ref, pr_ref, pa_ref, dm_ref, bg_ref, dpr_ref, dpa_ref, dum_ref, dbg_ref):
        g0 = _sig(u0_ref[...].astype(F32) + bg_ref[:, :d])
        g1 = _sig(u1_ref[...].astype(F32) + bg_ref[:, d:])
        dm = dm_ref[...].astype(F32)
        dpr_ref[...] = (dm * g0).astype(BF16)
        dpa_ref[...] = (dm * g1).astype(BF16)
        du0 = dm * pr_ref[...].astype(F32) * g0 * (1.0 - g0)
        du1 = dm * pa_ref[...].astype(F32) * g1 * (1.0 - g1)
        dum_ref[:, :d] = du0.astype(BF16)
        dum_ref[:, d:] = du1.astype(BF16)

        @pl.when(i == 0)
        def _():
            dbg_ref[...] = jnp.zeros_like(dbg_ref)

        dbg_ref[:, :d] += jnp.sum(du0, axis=0, keepdims=True)
        dbg_ref[:, d:] += jnp.sum(du1, axis=0, keepdims=True)

    return _rowcall(body, name, rows,[(um, d, 0), (um, d, 1), p_rnn, p_att, dmix], [bg],
                    [(d, BF16), (d, BF16), (2 * d, BF16)], [((1, 2 * d), F32)])


def _ffn_in_swiglu(zf, w, name):
    m, k = zf.shape
    f = w.shape[1] // 2
    tm, tn = _pick(m, 640), _pick(f, 1408)
    nb = f // tn

    def body(a_ref, bg_ref, bu_ref, g_ref, u_ref, act_ref):
        a = a_ref[...]
        gate = jnp.dot(a, bg_ref[...], preferred_element_type=F32)
        up = jnp.dot(a, bu_ref[...], preferred_element_type=F32)
        g_ref[...] = gate.astype(BF16)
        u_ref[...] = up.astype(BF16)
        act_ref[...] = (gate * _sig(gate) * up).astype(BF16)

    tile = pl.BlockSpec((tm, tn), lambda i, j: (i, j))
    return pl.pallas_call(
        body, name=name, grid=(m // tm, nb),
        in_specs=[pl.BlockSpec((tm, k), lambda i, j: (i, 0)), pl.BlockSpec((k, tn), lambda i, j: (0, j)),
                  pl.BlockSpec((k, tn), lambda i, j: (0, j + nb))],
        out_specs=[tile] * 3, out_shape=[jax.ShapeDtypeStruct((m, f), BF16)] * 3,
        compiler_params=_params(("parallel", "parallel")),
    )(zf, w, w)


def _ffn_out_bwd_swiglu(dh, w_out, gate, up, name):
    m, k = dh.shape
    f = w_out.shape[0]
    tm, tn = _pick(m, 640), _pick(f, 1408)

    def body(a_ref, b_ref, g_ref, u_ref, dg_ref, du_ref):
        da = lax.dot_general(a_ref[...].astype(BF16), b_ref[...], NT, preferred_element_type=F32)
        gv = g_ref[...].astype(F32)
        sg = _sig(gv)
        dg_ref[...] = (da * u_ref[...].astype(F32) * (sg * (1.0 + gv * (1.0 - sg)))).astype(BF16)
        du_ref[...] = (da * gv * sg).astype(BF16)

    tile = pl.BlockSpec((tm, tn), lambda i, j: (i, j))
    return pl.pallas_call(
        body, name=name, grid=(m // tm, f // tn),
        in_specs=[pl.BlockSpec((tm, k), lambda i, j: (i, 0)), pl.BlockSpec((tn, k), lambda i, j: (j, 0)), tile, tile],
        out_specs=[tile] * 2, out_shape=[jax.ShapeDtypeStruct((m, f), BF16)] * 2,
        compiler_params=_params(("parallel", "parallel")),
    )(dh, w_out, gate, up)


def _rope_tables(lp):
    idx = jnp.arange(lp, dtype=jnp.int32).astype(F32)
    inv_freq = ROPE_THETA ** (-jnp.arange(0, QK_ROPE, 2, dtype=F32) / QK_ROPE)
    ang = idx[:, None] * inv_freq[None, :]
    cos, sin = jnp.cos(ang), jnp.sin(ang)
    half = QK_ROPE // 2
    z = lambda wdt: jnp.zeros((lp, wdt), F32)
    tc = jnp.concatenate([cos, cos, z(LANES - 2 * half)], axis=1)
    ts1 = jnp.concatenate([-sin, z(LANES - half)], axis=1)
    ts2 = jnp.concatenate([z(half), sin, z(LANES - 2 * half)], axis=1)
    return tc, ts1, ts2


def _rope(xv, tc, ts1, ts2):
    half = QK_ROPE // 2
    return xv * tc + pltpu.roll(xv, LANES - half, 1) * ts1 + pltpu.roll(xv, half, 1) * ts2


def _rope_t(dv, tc, ts1, ts2):
    half = QK_ROPE // 2
    return dv * tc + pltpu.roll(dv * ts1, half, 1) + pltpu.roll(dv * ts2, LANES - half, 1)


def _up_rope(xn, w_pad, tabs, name, ukr=None):
    rows, k = xn.shape
    n = w_pad.shape[1]
    tr = _pick(rows, 640)

    def body(*refs):
        x_ref, w_ref, c_ref, s1_ref, s2_ref = refs[:5]
        o_ref = refs[-1]
        tc, ts1, ts2 = c_ref[...], s1_ref[...], s2_ref[...]
        y = jnp.dot(x_ref[...], w_ref[...], preferred_element_type=F32)
        kr = None if ukr is None else _rope(refs[5][...], tc, ts1, ts2).astype(BF16)
        for h in range(N_HEADS):
            lo, mid, hi = h * HEAD_PAD, h * HEAD_PAD + QK_NOPE, (h + 1) * HEAD_PAD
            o_ref[:, lo:mid] = y[:, lo:mid].astype(BF16)
            o_ref[:, mid:hi] = _rope(y[:, mid:hi], tc, ts1, ts2).astype(BF16) if ukr is None else kr

    row = lambda wdt: pl.BlockSpec((tr, wdt), lambda i: (i, 0))
    in_specs = [row(k), pl.BlockSpec((k, n), lambda i: (0, 0)), row(LANES), row(LANES), row(LANES)]
    args = [xn, w_pad, *tabs]
    if ukr is not None:
        in_specs.append(row(LANES))
        args.append(ukr)
    return pl.pallas_call(
        body, name=name, grid=(rows // tr,), in_specs=in_specs, out_specs=row(n),
        out_shape=jax.ShapeDtypeStruct((rows, n), BF16), compiler_params=_params(("parallel",)),
    )(*args)


def _rope_bwd(dq, dk, tabs, name):
    rows, w = dq.shape

    def body(i, n, dq_ref, dk_ref, c_ref, s1_ref, s2_ref, qo_ref, ko_ref, ro_ref):
        tc, ts1, ts2 = c_ref[...], s1_ref[...], s2_ref[...]
        dkr = None
        for h in range(N_HEADS):
            lo, mid, hi = h * HEAD_PAD, h * HEAD_PAD + QK_NOPE, (h + 1) * HEAD_PAD
            qo_ref[:, lo:mid] = dq_ref[:, lo:mid].astype(BF16)
            qo_ref[:, mid:hi] = _rope_t(dq_ref[:, mid:hi].astype(F32), tc, ts1, ts2).astype(BF16)
            ko_ref[:, lo:mid] = dk_ref[:, lo:mid].astype(BF16)
            ko_ref[:, mid:hi] = jnp.zeros((ko_ref.shape[0], hi - mid), BF16)
            part = dk_ref[:, mid:hi].astype(F32)
            dkr = part if dkr is None else dkr + part
        ro_ref[...] = _rope_t(dkr, tc, ts1, ts2).astype(BF16)

    return _rowcall(body, name, rows,[dq, dk, *tabs], [],
                    [(w, BF16), (w, BF16), (LANES, BF16)])


def _visible(q0, k0, tq, tk):
    qrow = q0 + lax.broadcasted_iota(jnp.int32, (tq, tk), 0)
    kcol = k0 + lax.broadcasted_iota(jnp.int32, (tq, tk), 1)
    return ((kcol + CHUNK_BIAS) >> CHUNK_SHIFT) <= ((qrow + CHUNK_BIAS) >> CHUNK_SHIFT)


def _visible_t(k0, q0, tk, tq):
    krow = k0 + lax.broadcasted_iota(jnp.int32, (tk, tq), 0)
    qcol = q0 + lax.broadcasted_iota(jnp.int32, (tk, tq), 1)
    return ((krow + CHUNK_BIAS) >> CHUNK_SHIFT) <= ((qcol + CHUNK_BIAS) >> CHUNK_SHIFT)


def _lanes(v, width):
    return jnp.tile(v, (1, width // LANES))


def _pipelined_chunks(n_full, scores, absorb):
    scores(0, 0)

    def pair(jj, carry):
        a = 2 * jj
        scores(a + 1, 1)
        absorb(a, 0, False)
        scores(a + 2, 0)
        absorb(a + 1, 1, False)
        return carry

    lax.fori_loop(0, n_full // 2, pair, 0)

    @pl.when(n_full % 2 == 0)
    def _():
        absorb(n_full, 0, True)

    @pl.when(n_full % 2 == 1)
    def _():
        scores(n_full, 1)
        absorb(n_full - 1, 0, False)
        absorb(n_full, 1, True)


def _attn_fwd(q, k, v, t, name):
    lp = q.shape[0]
    nt = lp // t

    def body(q_ref, k_ref, v_ref, o_ref, lse_ref, m_s, l_s, acc_s, a_s, sa_s, sb_s, p_s):
        i = pl.program_id(1)
        m_s[...] = jnp.full(m_s.shape, NEG, F32)
        l_s[...] = jnp.zeros(l_s.shape, F32)
        acc_s[...] = jnp.zeros(acc_s.shape, F32)

        s_bufs = (sa_s, sb_s)

        def scores(j, slot):
            r0 = pl.multiple_of(j * t, t)
            s_bufs[slot][...] = lax.dot_general(q_ref[...], k_ref[pl.ds(r0, t), :], NT,
                                                preferred_element_type=F32)

        def absorb(j, slot, masked):
            for r in range(0, t, ROW_GROUP):
                rows = slice(r, r + ROW_GROUP)
                s = s_bufs[slot][rows, :]
                if masked:
                    s = jnp.where(_visible(i * t + r, i * t, ROW_GROUP, t), s, NEG)
                m_prev = m_s[rows, :]
                m_new = jnp.maximum(m_prev, jnp.max(s, axis=1, keepdims=True))
                alpha = jnp.exp2((m_prev - m_new) * SCALE_LOG2E)
                p = jnp.exp2((s - _lanes(m_new, t)) * SCALE_LOG2E)
                l_s[rows, :] = alpha * l_s[rows, :] + jnp.sum(p, axis=1, keepdims=True)
                m_s[rows, :] = m_new
                a_s[rows, :] = alpha
                p_s[rows, :] = p.astype(BF16)
            r0 = pl.multiple_of(j * t, t)
            acc_s[...] = a_s[...] * acc_s[...] + jnp.dot(p_s[...], v_ref[pl.ds(r0, t), :],
                                                         preferred_element_type=F32)

        r1 = pl.multiple_of(jnp.minimum(i + 1, nt - 1) * t, t)
        rows = slice(t - SPILL, t)
        s = lax.dot_general(q_ref[rows, :], k_ref[pl.ds(r1, SPILL), :], NT, preferred_element_type=F32)
        seen = jnp.logical_and(_visible(i * t + t - SPILL, (i + 1) * t, SPILL, SPILL), i + 1 < nt)
        s = jnp.where(seen, s, NEG)
        m_new = jnp.max(s, axis=1, keepdims=True)
        p = jnp.exp2((s - m_new) * SCALE_LOG2E)
        l_s[rows, :] = jnp.broadcast_to(jnp.sum(p, axis=1, keepdims=True), (SPILL, LANES))
        acc_s[rows, :] = jnp.dot(p.astype(BF16), v_ref[pl.ds(r1, SPILL), :], preferred_element_type=F32)
        m_s[rows, :] = jnp.broadcast_to(m_new, (SPILL, LANES))

        _pipelined_chunks(i, scores, absorb)
        o_ref[...] = (acc_s[...] / l_s[...]).astype(BF16)
        lse_ref[...] = m_s[...] * ATTN_SCALE + jnp.log(l_s[...])

    return pl.pallas_call(
        body, name=name, grid=(N_HEADS, nt),
        in_specs=[pl.BlockSpec((t, HEAD_PAD), lambda h, i: (i, h)),
                  pl.BlockSpec((lp, HEAD_PAD), lambda h, i: (0, h)),
                  pl.BlockSpec((lp, V_DIM), lambda h, i: (0, h))],
        out_specs=[pl.BlockSpec((t, V_DIM), lambda h, i: (i, h)),
                   pl.BlockSpec((None, t, LANES), lambda h, i: (h, i, 0))],
        out_shape=[jax.ShapeDtypeStruct((lp, N_HEADS * V_DIM), BF16),
                   jax.ShapeDtypeStruct((N_HEADS, lp, LANES), F32)],
        scratch_shapes=[pltpu.VMEM((t, LANES), F32), pltpu.VMEM((t, LANES), F32), pltpu.VMEM((t, V_DIM), F32),
                        pltpu.VMEM((t, LANES), F32), pltpu.VMEM((t, t), F32), pltpu.VMEM((t, t), F32),
                        pltpu.VMEM((t, t), BF16)],
        compiler_params=_params(("parallel", "arbitrary")),
    )(q, k, v)


def _attn_bwd_dq(q, k, v, do, o, lse, dep, t, name):
    lp = q.shape[0]
    nt = lp // t

    def body(q_ref, k_ref, v_ref, do_ref, o_ref, lse_ref, dep_ref, dq_ref, l2row_ref, dlrow_ref,
             acc_s, l2_s, dl_s, sa_s, sb_s, da_s, db_s, ds_s):
        i = pl.program_id(1)
        delta = jnp.sum(do_ref[...].astype(F32) * o_ref[...].astype(F32), axis=1, keepdims=True)
        dl_s[...] = jnp.broadcast_to(delta, dl_s.shape)
        l2_s[...] = lse_ref[...] * LOG2E
        l2row_ref[...] = l2_s[...].T[0:SUBLANES, :]
        dlrow_ref[...] = dl_s[...].T[0:SUBLANES, :]
        acc_s[...] = jnp.zeros(acc_s.shape, F32)
        s_bufs, d_bufs = (sa_s, sb_s), (da_s, db_s)

        def dscores(s, dp, rows, width):
            p = jnp.exp2(s * SCALE_LOG2E - _lanes(l2_s[rows, :], width))
            return (p * (dp - _lanes(dl_s[rows, :], width))).astype(BF16)

        r1 = pl.multiple_of(jnp.minimum(i + 1, nt - 1) * t, t)
        rows = slice(t - SPILL, t)
        ks, vs = k_ref[pl.ds(r1, SPILL), :], v_ref[pl.ds(r1, SPILL), :]
        s = lax.dot_general(q_ref[rows, :], ks, NT, preferred_element_type=F32)
        seen = jnp.logical_and(_visible(i * t + t - SPILL, (i + 1) * t, SPILL, SPILL), i + 1 < nt)
        s = jnp.where(seen, s, NEG)
        dp = lax.dot_general(do_ref[rows, :], vs, NT, preferred_element_type=F32)
        acc_s[rows, :] = jnp.dot(dscores(s, dp, rows, SPILL), ks, preferred_element_type=F32)

        def scores(j, slot):
            r0 = pl.multiple_of(j * t, t)
            s_bufs[slot][...] = lax.dot_general(q_ref[...], k_ref[pl.ds(r0, t), :], NT,
                                                preferred_element_type=F32)
            d_bufs[slot][...] = lax.dot_general(do_ref[...], v_ref[pl.ds(r0, t), :], NT,
                                                preferred_element_type=F32)

        def absorb(j, slot, masked):
            for r in range(0, t, ROW_GROUP):
                rows = slice(r, r + ROW_GROUP)
                s = s_bufs[slot][rows, :]
                if masked:
                    s = jnp.where(_visible(i * t + r, i * t, ROW_GROUP, t), s, NEG)
                ds_s[rows, :] = dscores(s, d_bufs[slot][rows, :], rows, t)
            r0 = pl.multiple_of(j * t, t)
            acc_s[...] += jnp.dot(ds_s[...], k_ref[pl.ds(r0, t), :], preferred_element_type=F32)

        _pipelined_chunks(i, scores, absorb)
        dq_ref[...] = (acc_s[...] * ATTN_SCALE).astype(BF16)

    stat_row = pl.BlockSpec((None, None, SUBLANES, t), lambda h, i: (h, i, 0, 0))
    return pl.pallas_call(
        body, name=name, grid=(N_HEADS, nt),
        in_specs=[pl.BlockSpec((t, HEAD_PAD), lambda h, i: (i, h)),
                  pl.BlockSpec((lp, HEAD_PAD), lambda h, i: (0, h)),
                  pl.BlockSpec((lp, V_DIM), lambda h, i: (0, h)),
                  pl.BlockSpec((t, V_DIM), lambda h, i: (i, h)),
                  pl.BlockSpec((t, V_DIM), lambda h, i: (i, h)),
                  pl.BlockSpec((None, t, LANES), lambda h, i: (h, i, 0)),
                  pl.BlockSpec(dep.shape, lambda h, i: (0, 0))],
        out_specs=[pl.BlockSpec((t, HEAD_PAD), lambda h, i: (i, h)), stat_row, stat_row],
        out_shape=[jax.ShapeDtypeStruct((lp, N_HEADS * HEAD_PAD), BF16),
                   jax.ShapeDtypeStruct((N_HEADS, nt, SUBLANES, t), F32),
                   jax.ShapeDtypeStruct((N_HEADS, nt, SUBLANES, t), F32)],
        scratch_shapes=[pltpu.VMEM((t, HEAD_PAD), F32), pltpu.VMEM((t, LANES), F32), pltpu.VMEM((t, LANES), F32),
                        pltpu.VMEM((t, t), F32), pltpu.VMEM((t, t), F32), pltpu.VMEM((t, t), F32),
                        pltpu.VMEM((t, t), F32), pltpu.VMEM((t, t), BF16)],
        compiler_params=_params(("parallel", "arbitrary")),
    )(q, k, v, do, o, lse, dep)


def _attn_bwd_dkv(q, k, v, do, l2row, dlrow, t, name):
    lp = q.shape[0]
    nt = lp // t

    def body(q_ref, k_ref, v_ref, do_ref, l2_ref, dl_ref, dk_ref, dv_ref,
             dk_s, dv_s, sa_s, sb_s, da_s, db_s, p_s, ds_s):
        j = pl.program_id(1)
        dk_s[...] = jnp.zeros(dk_s.shape, F32)
        dv_s[...] = jnp.zeros(dv_s.shape, F32)
        s_bufs, d_bufs = (sa_s, sb_s), (da_s, db_s)

        def weights(st, dpt, l2r, dlr):
            pt = jnp.exp2(st * SCALE_LOG2E - l2r)
            return pt.astype(BF16), (pt * (dpt - dlr)).astype(BF16)

        prev = jnp.maximum(j - 1, 0)
        q0 = pl.multiple_of(prev * t + t - SPILL, SPILL)
        rows = slice(0, SPILL)
        qs, dos = q_ref[pl.ds(q0, SPILL), :], do_ref[pl.ds(q0, SPILL), :]
        st = lax.dot_general(k_ref[rows, :], qs, NT, preferred_element_type=F32)
        seen = jnp.logical_and(_visible_t(j * t, j * t - SPILL, SPILL, SPILL), j > 0)
        st = jnp.where(seen, st, NEG)
        dpt = lax.dot_general(v_ref[rows, :], dos, NT, preferred_element_type=F32)
        pt, dst = weights(st, dpt, l2_ref[prev, 0:1, t - SPILL:], dl_ref[prev, 0:1, t - SPILL:])
        dv_s[rows, :] = jnp.dot(pt, dos, preferred_element_type=F32)
        dk_s[rows, :] = jnp.dot(dst, qs, preferred_element_type=F32)

        def scores(c, slot):
            r0 = pl.multiple_of((nt - 1 - c) * t, t)
            s_bufs[slot][...] = lax.dot_general(k_ref[...], q_ref[pl.ds(r0, t), :], NT,
                                                preferred_element_type=F32)
            d_bufs[slot][...] = lax.dot_general(v_ref[...], do_ref[pl.ds(r0, t), :], NT,
                                                preferred_element_type=F32)

        def absorb(c, slot, masked):
            i = nt - 1 - c
            l2r, dlr = l2_ref[i, 0:1, :], dl_ref[i, 0:1, :]
            for r in range(0, t, ROW_GROUP):
                rows = slice(r, r + ROW_GROUP)
                st = s_bufs[slot][rows, :]
                if masked:
                    st = jnp.where(_visible_t(j * t + r, j * t, ROW_GROUP, t), st, NEG)
                p_s[rows, :], ds_s[rows, :] = weights(st, d_bufs[slot][rows, :], l2r, dlr)
            r0 = pl.multiple_of(i * t, t)
            dv_s[...] += jnp.dot(p_s[...], do_ref[pl.ds(r0, t), :], preferred_element_type=F32)
            dk_s[...] += jnp.dot(ds_s[...], q_ref[pl.ds(r0, t), :], preferred_element_type=F32)

        _pipelined_chunks(nt - 1 - j, scores, absorb)
        dk_ref[...] = (dk_s[...] * ATTN_SCALE).astype(BF16)
        dv_ref[...] = dv_s[...].astype(BF16)

    stat_rows = pl.BlockSpec((None, nt, SUBLANES, t), lambda h, j: (h, 0, 0, 0))
    return pl.pallas_call(
        body, name=name, grid=(N_HEADS, nt),
        in_specs=[pl.BlockSpec((lp, HEAD_PAD), lambda h, j: (0, h)),
                  pl.BlockSpec((t, HEAD_PAD), lambda h, j: (j, h)),
                  pl.BlockSpec((t, V_DIM), lambda h, j: (j, h)),
                  pl.BlockSpec((lp, V_DIM), lambda h, j: (0, h)),
                  stat_rows, stat_rows],
        out_specs=[pl.BlockSpec((t, HEAD_PAD), lambda h, j: (j, h)),
                   pl.BlockSpec((t, V_DIM), lambda h, j: (j, h))],
        out_shape=[jax.ShapeDtypeStruct((lp, N_HEADS * HEAD_PAD), BF16),
                   jax.ShapeDtypeStruct((lp, N_HEADS * V_DIM), BF16)],
        scratch_shapes=[pltpu.VMEM((t, HEAD_PAD), F32), pltpu.VMEM((t, V_DIM), F32),
                        pltpu.VMEM((t, t), F32), pltpu.VMEM((t, t), F32), pltpu.VMEM((t, t), F32),
                        pltpu.VMEM((t, t), F32), pltpu.VMEM((t, t), BF16), pltpu.VMEM((t, t), BF16)],
        compiler_params=_params(("parallel", "arbitrary")),
    )(q, k, v, do, l2row, dlrow)


def _shift_down(cur, prev8, k):
    r = pltpu.roll(cur, k, 0)
    row8 = lax.broadcasted_iota(jnp.int32, prev8.shape, 0)
    first = jnp.where(row8 < k, pltpu.roll(prev8, k, 0), r[0:SUBLANES])
    return jnp.concatenate([first, r[SUBLANES:]], axis=0)


def _shift_up(cur, next8, k):
    t = cur.shape[0]
    r = pltpu.roll(cur, t - k, 0)
    row8 = lax.broadcasted_iota(jnp.int32, next8.shape, 0)
    last = jnp.where(row8 >= SUBLANES - k, pltpu.roll(next8, SUBLANES - k, 0), r[t - SUBLANES:])
    return jnp.concatenate([r[:t - SUBLANES], last], axis=0)


def _scan_rows(a, b, edge, reverse):
    t, d = a.shape
    groups = t // SUBLANES
    a3, b3 = a.reshape(groups, SUBLANES, d), b.reshape(groups, SUBLANES, d)
    sub = lax.broadcasted_iota(jnp.int32, a3.shape, 1)
    s = 1
    while s < SUBLANES:
        keep = sub < SUBLANES - s if reverse else sub >= s
        shift = SUBLANES - s if reverse else s
        a_sh = jnp.where(keep, pltpu.roll(a3, shift, 1), 1.0)
        b_sh = jnp.where(keep, pltpu.roll(b3, shift, 1), 0.0)
        b3 = a3 * b_sh + b3
        a3 = a3 * a_sh
        s *= 2
    out = [None] * groups
    for k in (range(groups - 1, -1, -1) if reverse else range(groups)):
        out[k] = b3[k] + a3[k] * edge
        edge = out[k][0:1, :] if reverse else out[k][SUBLANES - 1:SUBLANES, :]
    return jnp.concatenate(out, axis=0)


def _sqrt_one_minus_exp2x(x):
    th = jnp.tanh(x)
    m2 = (-2.0 * th) / (1.0 - th)
    return m2 * lax.rsqrt(jnp.maximum(m2, TINY))


def _log_sigmoid(x):
    return jnp.minimum(x, 0.0) - jnp.log(1.0 + jnp.exp(-jnp.abs(x)))


GELU_C = math.sqrt(2.0 / math.pi)
GELU_K = 0.044715


def _gelu(x):
    th = jnp.tanh(GELU_C * (x + GELU_K * x * x * x))
    return 0.5 * x * (1.0 + th), th


def _block_mm(xb, w_ref, dims):
    rb = D_RNN // RNN_BLOCKS
    return jnp.concatenate(
        [lax.dot_general(xb[:, h * rb:(h + 1) * rb], w_ref[h], dims, preferred_element_type=F32)
         for h in range(RNN_BLOCKS)], axis=1)


def _rglru_gates(ux, prev8, pv_ref, wa_ref, wi_ref):
    shifted = [ux] + [_shift_down(ux, prev8, k) for k in range(1, CONV_WIDTH)]
    xc = pv_ref[4:5, :] + pv_ref[3:4, :] * ux
    for k in range(1, CONV_WIDTH):
        xc = xc + pv_ref[3 - k:4 - k, :] * shifted[k]
    xcb = xc.astype(BF16)
    r_g = _sig(_block_mm(xcb, wa_ref, NN) + pv_ref[5:6, :])
    i_g = _sig(_block_mm(xcb, wi_ref, NN) + pv_ref[6:7, :])
    log_a = LRU_C * r_g * _log_sigmoid(pv_ref[7:8, :])
    a = jnp.exp(log_a)
    mm = _sqrt_one_minus_exp2x(log_a)
    return dict(shifted=shifted, xc=xc, xcb=xcb, r=r_g, i=i_g, a=a, mm=mm)


def _rglru_fwd(ux, ug, pv, wa, wi, t, name):
    lp, d = ux.shape

    def body(ux_ref, ug_ref, pv_ref, wa_ref, wi_ref, y_ref, h_ref, tail_s, hc_s):
        @pl.when(pl.program_id(0) == 0)
        def _():
            tail_s[...] = jnp.zeros_like(tail_s)
            hc_s[...] = jnp.zeros_like(hc_s)

        uxv = ux_ref[...]
        gt = _rglru_gates(uxv, tail_s[...], pv_ref, wa_ref, wi_ref)
        tail_s[...] = ux_ref[t - SUBLANES:t, :]
        h_ref[...] = _scan_rows(gt["a"], gt["mm"] * (gt["i"] * gt["xc"]), hc_s[0:1, :], False)
        hc_s[...] = h_ref[t - SUBLANES:t, :]
        hc_s[0:1, :] = h_ref[t - 1:t, :]
        y_ref[...] = (h_ref[...] * _gelu(ug_ref[...])[0]).astype(BF16)

    tile = pl.BlockSpec((t, d), lambda i: (i, 0))
    return pl.pallas_call(
        body, name=name, grid=(lp // t,),
        in_specs=[tile, tile, pl.BlockSpec(pv.shape, lambda i: (0, 0)),
                  pl.BlockSpec(wa.shape, lambda i: (0, 0, 0)), pl.BlockSpec(wi.shape, lambda i: (0, 0, 0))],
        out_specs=[tile, tile],
        out_shape=[jax.ShapeDtypeStruct((lp, d), BF16), jax.ShapeDtypeStruct((lp, d), F32)],
        scratch_shapes=[pltpu.VMEM((SUBLANES, d), F32), pltpu.VMEM((SUBLANES, d), F32)],
        compiler_params=_params(("arbitrary",)),
    )(ux, ug, pv, wa, wi)


def _rglru_bwd(ux, ug, hs, dy, pv, wa, wi, dep, t, name):
    lp, d = ux.shape
    nt = lp // t
    per = t // SUBLANES
    rb = d // RNN_BLOCKS

    def body(ux_ref, uxp_ref, ug_ref, h_ref, hp_ref, dy_ref, pv_ref, wa_ref, wi_ref, dep_ref,
             dux_ref, dug_ref, dpv_ref, dwa_ref, dwi_ref, ca_s, cg_s, cx_s):
        step = pl.program_id(0)
        first_tile = step == nt - 1

        @pl.when(step == 0)
        def _():
            for ref in (ca_s, cg_s, cx_s, dpv_ref, dwa_ref, dwi_ref):
                ref[...] = jnp.zeros_like(ref)

        uxv = ux_ref[...]
        prev8 = jnp.where(first_tile, 0.0, uxp_ref[...])
        hprev8 = jnp.where(first_tile, 0.0, hp_ref[...])
        gt = _rglru_gates(uxv, prev8, pv_ref, wa_ref, wi_ref)
        a, mm, r_g, i_g, xc = gt["a"], gt["mm"], gt["r"], gt["i"], gt["xc"]
        hv = h_ref[...]
        hprev = _shift_down(hv, hprev8, 1)
        ugv, dyv = ug_ref[...], dy_ref[...]
        gel, th = _gelu(ugv)
        dgel = 0.5 * (1.0 + th) + 0.5 * ugv * (1.0 - th * th) * (GELU_C * (1.0 + 3.0 * GELU_K * ugv * ugv))
        dug_ref[...] = (dyv * hv * dgel).astype(BF16)
        a_up = _shift_up(a, ca_s[...], 1)
        gv = _scan_rows(a_up, dyv * gel, cg_s[0:1, :], True)
        ca_s[...] = a[0:SUBLANES]
        cg_s[...] = gv[0:SUBLANES]
        ixc = i_g * xc
        d_ixc = gv * mm
        d_log_a = gv * hprev * a - (gv * ixc) * (a * a) / mm
        logsig = _log_sigmoid(pv_ref[7:8, :])
        d_pre_a = d_log_a * (LRU_C * logsig) * r_g * (1.0 - r_g)
        d_pre_i = d_ixc * xc * i_g * (1.0 - i_g)
        dab, dib = d_pre_a.astype(BF16), d_pre_i.astype(BF16)
        d_xc = d_ixc * i_g + _block_mm(dab, wa_ref, NT) + _block_mm(dib, wi_ref, NT)
        xcb = gt["xcb"]
        for h in range(RNN_BLOCKS):
            cols = slice(h * rb, (h + 1) * rb)
            dwa_ref[h] += lax.dot_general(xcb[:, cols], dab[:, cols], TN, preferred_element_type=F32)
            dwi_ref[h] += lax.dot_general(xcb[:, cols], dib[:, cols], TN, preferred_element_type=F32)
        csum = lambda v: jnp.sum(v, axis=0, keepdims=True)
        for k in range(CONV_WIDTH):
            dpv_ref[3 - k:4 - k, :] += csum(d_xc * gt["shifted"][k])
        dpv_ref[4:5, :] += csum(d_xc)
        dpv_ref[5:6, :] += csum(d_pre_a)
        dpv_ref[6:7, :] += csum(d_pre_i)
        dpv_ref[7:8, :] += csum(d_log_a * (LRU_C * r_g)) * _sig(-pv_ref[7:8, :])
        dux = pv_ref[3:4, :] * d_xc
        for k in range(1, CONV_WIDTH):
            dux = dux + pv_ref[3 - k:4 - k, :] * _shift_up(d_xc, cx_s[...], k)
        cx_s[...] = d_xc[0:SUBLANES]
        dux_ref[...] = dux.astype(BF16)

    rev = lambda i: (nt - 1 - i, 0)
    before = lambda i: (jnp.maximum((nt - 1 - i) * per - 1, 0), 0)
    tile = pl.BlockSpec((t, d), rev)
    tail = pl.BlockSpec((SUBLANES, d), before)
    fixed2 = lambda arr: pl.BlockSpec(arr.shape, lambda i: (0, 0))
    fixed3 = lambda arr: pl.BlockSpec(arr.shape, lambda i: (0, 0, 0))
    return pl.pallas_call(
        body, name=name, grid=(nt,),
        in_specs=[tile, tail, tile, tile, tail, tile, fixed2(pv), fixed3(wa), fixed3(wi), fixed2(dep)],
        out_specs=[tile, tile, fixed2(pv), fixed3(wa), fixed3(wi)],
        out_shape=[jax.ShapeDtypeStruct((lp, d), BF16), jax.ShapeDtypeStruct((lp, d), BF16),
                   jax.ShapeDtypeStruct(pv.shape, F32), jax.ShapeDtypeStruct(wa.shape, F32),
                   jax.ShapeDtypeStruct(wi.shape, F32)],
        scratch_shapes=[pltpu.VMEM((SUBLANES, d), F32)] * 3,
        compiler_params=_params(("arbitrary",)),
    )(ux, ux, ug, hs, hs, dy, pv, wa, wi, dep)


def _adamw(w, m, v, parts, dep, name):
    rows, cols = w.shape
    tr = _pick(rows, 256, SUBLANES)
    c1 = 1.0 / (1.0 - ADAM_B1 ** ADAM_STEP)
    c2 = 1.0 / (1.0 - ADAM_B2 ** ADAM_STEP)

    def body(w_ref, m_ref, v_ref, p_ref, dep_ref, g_ref, d_ref, mo_ref, vo_ref):
        g = p_ref[0].astype(F32)
        for q in range(1, N_DEV):
            g = g + p_ref[q].astype(F32)
        mn = ADAM_B1 * m_ref[...] + (1.0 - ADAM_B1) * g
        vn = ADAM_B2 * v_ref[...] + (1.0 - ADAM_B2) * (g * g)
        g_ref[...] = g
        mo_ref[...] = mn
        vo_ref[...] = vn
        d_ref[...] = -ADAM_LR * ((mn * c1) / (jnp.sqrt(vn * c2) + ADAM_EPS) + ADAM_WD * w_ref[...])

    blk = pl.BlockSpec((tr, cols), lambda i: (i, 0))
    return pl.pallas_call(
        body, name=name, grid=(rows // tr,),
        in_specs=[blk, blk, blk, pl.BlockSpec((N_DEV, tr, cols), lambda i: (0, i, 0)),
                  pl.BlockSpec(dep.shape, lambda i: (0, 0))],
        out_specs=[blk] * 4, out_shape=[jax.ShapeDtypeStruct((rows, cols), F32)] * 4,
        compiler_params=_params(("parallel",)),
    )(w, m, v, parts, dep)


WEIGHTS = ("meta_tokens", "norm_mix_g", "w_in", "b_gate", "conv_w", "conv_b", "w_rec_a", "b_rec_a", "w_rec_i",
           "b_rec_i", "lru_lambda", "q_norm_g", "w_uq", "kv_norm_g", "w_ukv", "w_branch", "w_out", "norm_ffn_g",
           "w_ffn_in", "w_ffn_out", "final_norm_g")
SHARDED = {"meta_tokens": True, "w_in": True, "b_gate": True, "conv_w": True, "w_uq": True, "w_ukv": True,
           "w_branch": False, "w_out": False, "w_ffn_in": True, "w_ffn_out": False}


def _as2d(a):
    return a.reshape(-1, a.shape[-1])


def _full_from_gathered(g, by_cols):
    if by_cols:
        return jnp.transpose(g, (1, 0, 2)).reshape(g.shape[1], N_DEV * g.shape[2])
    return g.reshape(N_DEV * g.shape[1], g.shape[2])


def _blocks_from_full(full, by_cols):
    if by_cols:
        r, c = full.shape
        return jnp.transpose(full.reshape(r, N_DEV, c // N_DEV), (1, 0, 2))
    return full.reshape(N_DEV, full.shape[0] // N_DEV, full.shape[1])


def kernel(x, meta_tokens, norm_mix_g, w_in, b_gate, conv_w, conv_b, w_rec_a, b_rec_a, w_rec_i, b_rec_i, lru_lambda, q_norm_g, w_uq, kv_norm_g, w_ukv, w_branch, w_out, norm_ffn_g, w_ffn_in, w_ffn_out, final_norm_g, loss_target, m_meta_tokens, m_norm_mix_g, m_w_in, m_b_gate, m_conv_w, m_conv_b, m_w_rec_a, m_b_rec_a, m_w_rec_i, m_b_rec_i, m_lru_lambda, m_q_norm_g, m_w_uq, m_kv_norm_g, m_w_ukv, m_w_branch, m_w_out, m_norm_ffn_g, m_w_ffn_in, m_w_ffn_out, m_final_norm_g, v_meta_tokens, v_norm_mix_g, v_w_in, v_b_gate, v_conv_w, v_conv_b, v_w_rec_a, v_b_rec_a, v_w_rec_i, v_b_rec_i, v_lru_lambda, v_q_norm_g, v_w_uq, v_kv_norm_g, v_w_ukv, v_w_branch, v_w_out, v_norm_ffn_g, v_w_ffn_in, v_w_ffn_out, v_final_norm_g):
    w = dict(meta_tokens=meta_tokens, norm_mix_g=norm_mix_g, w_in=w_in, b_gate=b_gate, conv_w=conv_w, conv_b=conv_b,
             w_rec_a=w_rec_a, b_rec_a=b_rec_a, w_rec_i=w_rec_i, b_rec_i=b_rec_i, lru_lambda=lru_lambda,
             q_norm_g=q_norm_g, w_uq=w_uq, kv_norm_g=kv_norm_g, w_ukv=w_ukv, w_branch=w_branch, w_out=w_out,
             norm_ffn_g=norm_ffn_g, w_ffn_in=w_ffn_in, w_ffn_out=w_ffn_out, final_norm_g=final_norm_g)
    m = dict(meta_tokens=m_meta_tokens, norm_mix_g=m_norm_mix_g, w_in=m_w_in, b_gate=m_b_gate, conv_w=m_conv_w,
             conv_b=m_conv_b, w_rec_a=m_w_rec_a, b_rec_a=m_b_rec_a, w_rec_i=m_w_rec_i, b_rec_i=m_b_rec_i,
             lru_lambda=m_lru_lambda, q_norm_g=m_q_norm_g, w_uq=m_w_uq, kv_norm_g=m_kv_norm_g, w_ukv=m_w_ukv,
             w_branch=m_w_branch, w_out=m_w_out, norm_ffn_g=m_norm_ffn_g, w_ffn_in=m_w_ffn_in,
             w_ffn_out=m_w_ffn_out, final_norm_g=m_final_norm_g)
    v = dict(meta_tokens=v_meta_tokens, norm_mix_g=v_norm_mix_g, w_in=v_w_in, b_gate=v_b_gate, conv_w=v_conv_w,
             conv_b=v_conv_b, w_rec_a=v_w_rec_a, b_rec_a=v_b_rec_a, w_rec_i=v_w_rec_i, b_rec_i=v_b_rec_i,
             lru_lambda=v_lru_lambda, q_norm_g=v_q_norm_g, w_uq=v_w_uq, kv_norm_g=v_kv_norm_g, w_ukv=v_w_ukv,
             w_branch=v_w_branch, w_out=v_w_out, norm_ffn_g=v_norm_ffn_g, w_ffn_in=v_w_ffn_in,
             w_ffn_out=v_w_ffn_out, final_norm_g=v_final_norm_g)

    seq, d_model = x.shape[1], x.shape[2]
    length = N_META + seq
    lp = -(-length // LANES) * LANES
    t_attn = _pick(lp, 640)
    t_rnn = LANES

    small = ("meta_tokens", "b_gate", "conv_w")
    names = list(SHARDED)
    mid, late = ("w_uq", "w_ukv", "w_branch", "w_out"), ("w_ffn_in", "w_ffn_out")
    payload = lambda n: _as2d(w[n]) if n in small else _as2d(w[n]).astype(BF16)
    got = _exchange([payload(n) for n in small], [True] * len(small), "gather_small")
    w_in_blocks = _gather_two_level(payload("w_in"), "gather_in")
    mid_h = _exchange_start([payload(n) for n in mid], [True] * len(mid), w_in_blocks, "gather_mid_start")
    late_h = _exchange_start([payload(n) for n in late], [True] * len(late), mid_h["token"], "gather_late_start")
    full = {n: _full_from_gathered(g, SHARDED[n]) for n, g in zip(small, got)}
    h0 = jnp.concatenate([full["meta_tokens"], x[0], jnp.zeros((lp - length, d_model), F32)], axis=0)
    z = _rmsnorm_fwd(h0, norm_mix_g, "norm_mix")
    full["w_in"] = _full_from_gathered(w_in_blocks, True)

    splits = (D_RNN, D_RNN, Q_RANK, KV_RANK, QK_ROPE, 2 * d_model)
    offs = [0]
    for s in splits:
        offs.append(offs[-1] + s)
    w_x, w_g, w_q, w_kv, w_kr, w_m = (full["w_in"][:, offs[s]:offs[s + 1]] for s in range(6))
    w_kr = jnp.pad(w_kr, ((0, 0), (0, LANES - QK_ROPE)))
    bg = full["b_gate"].reshape(1, 2 * d_model)
    pv = jnp.concatenate([full["conv_w"], conv_b, b_rec_a, b_rec_i, lru_lambda], axis=0)
    wa_b, wi_b = w_rec_a[0].astype(BF16), w_rec_i[0].astype(BF16)
    g_final = final_norm_g.reshape(1, d_model)

    tgt = jnp.pad(loss_target[0], ((N_META, lp - length), (0, 0)))
    tabs = _rope_tables(lp)

    ux = _mm([(z, w_x)], "nn", "in_x")
    ug = _mm([(z, w_g)], "nn", "in_g")
    uq = _mm([(z, w_q)], "nn", "in_q")
    ukv = _mm([(z, w_kv)], "nn", "in_kv")
    ukr = _mm([(z, w_kr)], "nn", "in_kr")
    um = _mm([(z, w_m)], "nn", "in_m", out_dtype=BF16)
    y_rnn, hs = _rglru_fwd(ux, ug, pv, wa_b, wi_b, t_rnn, "rglru_fwd")
    for n, g in zip(mid, _exchange_wait(mid_h, hs, "gather_mid_wait")):
        full[n] = _full_from_gathered(g, SHARDED[n])
    w_uq_pad = jnp.pad(full["w_uq"].reshape(Q_RANK, N_HEADS, QK_NOPE + QK_ROPE),
                       ((0, 0), (0, 0), (0, HEAD_PAD - QK_NOPE - QK_ROPE))).reshape(Q_RANK, N_HEADS * HEAD_PAD)
    w_ukv3 = full["w_ukv"].reshape(KV_RANK, N_HEADS, QK_NOPE + V_DIM)
    w_k_pad = jnp.pad(w_ukv3[:, :, :QK_NOPE], ((0, 0), (0, 0), (0, HEAD_PAD - QK_NOPE))).reshape(
        KV_RANK, N_HEADS * HEAD_PAD)
    w_v = w_ukv3[:, :, QK_NOPE:].reshape(KV_RANK, N_HEADS * V_DIM)
    wb_r, wb_a = full["w_branch"][:D_RNN], full["w_branch"][D_RNN:]
    qn = _rmsnorm_fwd(uq, q_norm_g, "norm_q")
    kvn = _rmsnorm_fwd(ukv, kv_norm_g, "norm_kv")
    qh = _up_rope(qn, w_uq_pad, tabs, "up_q")
    kh = _up_rope(kvn, w_k_pad, tabs, "up_k", ukr=ukr)
    vh = _mm([(kvn, w_v)], "nn", "up_v", out_dtype=BF16)
    oh, lse = _attn_fwd(qh, kh, vh, t_attn, "attn_fwd")
    p_rnn = _mm([(y_rnn, wb_r)], "nn", "branch_rnn", out_dtype=BF16)
    p_att = _mm([(oh, wb_a)], "nn", "branch_att", out_dtype=BF16)
    mixed, h1 = _mix_out(um, p_rnn, p_att, bg, full["w_out"], h0, "out_proj")
    for n, g in zip(late, _exchange_wait(late_h, h1, "gather_late_wait")):
        full[n] = _full_from_gathered(g, SHARDED[n])
    zf = _rmsnorm_fwd(h1, norm_ffn_g, "norm_ffn")
    gate, up, act = _ffn_in_swiglu(zf, full["w_ffn_in"], "ffn_in")
    h2 = _mm([(act, full["w_ffn_out"])], "nn", "ffn_out", res=h1)
    dh2, dg_final, _, loss_part = _loss_bwd(h2, tgt, g_final, seq, "loss_bwd")

    d_gate, d_up = _ffn_out_bwd_swiglu(dh2, full["w_ffn_out"], gate, up, "d_gate_up")
    dw_ffn_out = _mm_tn(act, dh2, "dw_ffn_out")
    dw_ffn_in = jnp.concatenate([_mm_tn(zf, d_gate, "dw_ffn_gate"), _mm_tn(zf, d_up, "dw_ffn_up")], axis=1)
    blocks = lambda n, g: _blocks_from_full(g, SHARDED[n]).astype(F32 if n in small else BF16)
    sent = {("w_ffn_in", "w_ffn_out"): _exchange_start(
        [blocks("w_ffn_in", dw_ffn_in), blocks("w_ffn_out", dw_ffn_out)], [False] * 2, dg_final, "scatter_ffn_start")}
    d_zf = _mm([(d_gate, full["w_ffn_in"], D_FF, 0), (d_up, full["w_ffn_in"], D_FF, 1)], "nt", "d_zf")
    dh1, dg_ffn = _rmsnorm_bwd(h1, d_zf, norm_ffn_g, "norm_ffn_bwd", res=dh2,
                               dep=sent[("w_ffn_in", "w_ffn_out")]["token"])
    d_mixed = _mm([(dh1, full["w_out"])], "nt", "d_mixed", out_dtype=BF16)
    dw_out = _mm_tn(mixed, dh1, "dw_out")
    d_prnn, d_patt, d_um, dbg = _mix_bwd(um, p_rnn, p_att, d_mixed, bg, "mix_bwd")
    d_yrnn = _mm([(d_prnn, wb_r)], "nt", "d_yrnn")
    d_oh = _mm([(d_patt, wb_a)], "nt", "d_oh", out_dtype=BF16)
    dwb_r = _mm_tn(y_rnn, d_prnn, "dw_branch_rnn")
    dwb_a = _mm_tn(oh, d_patt, "dw_branch_att")
    sent[("w_out", "w_branch")] = _exchange_start(
        [blocks("w_out", dw_out), blocks("w_branch", jnp.concatenate([dwb_r, dwb_a], axis=0))], [False] * 2,
        dg_ffn, "scatter_mix_start")
    dqh, l2row, dlrow = _attn_bwd_dq(qh, kh, vh, d_oh, oh, lse, sent[("w_out", "w_branch")]["token"], t_attn,
                                     "attn_bwd_dq")
    dkh, dvh = _attn_bwd_dkv(qh, kh, vh, d_oh, l2row, dlrow, t_attn, "attn_bwd_dkv")
    dqpad, dkpad, dukr = _rope_bwd(dqh, dkh, tabs, "rope_bwd")
    d_qn = _mm([(dqpad, w_uq_pad)], "nt", "d_qn")
    dw_uq_pad = _mm_tn(qn, dqpad, "dw_uq")
    d_kvn = _mm([(dkpad, w_k_pad), (dvh, w_v)], "nt", "d_kvn")
    dw_k_pad = _mm_tn(kvn, dkpad, "dw_uk")
    dw_v = _mm_tn(kvn, dvh, "dw_uv")
    dw_uq = dw_uq_pad.reshape(Q_RANK, N_HEADS, HEAD_PAD)[:, :, :QK_NOPE + QK_ROPE].reshape(Q_RANK, -1)
    dw_ukv = jnp.concatenate([dw_k_pad.reshape(KV_RANK, N_HEADS, HEAD_PAD)[:, :, :QK_NOPE],
                              dw_v.reshape(KV_RANK, N_HEADS, V_DIM)], axis=2).reshape(KV_RANK, -1)
    sent[("w_uq", "w_ukv")] = _exchange_start([blocks("w_uq", dw_uq), blocks("w_ukv", dw_ukv)], [False] * 2,
                                              dbg, "scatter_attn_start")
    duq, dg_q = _rmsnorm_bwd(uq, d_qn, q_norm_g, "norm_q_bwd", out_dtype=BF16)
    dukv, dg_kv = _rmsnorm_bwd(ukv, d_kvn, kv_norm_g, "norm_kv_bwd", out_dtype=BF16)
    dux, dug, dpv, dwa, dwi = _rglru_bwd(ux, ug, hs, d_yrnn, pv, wa_b, wi_b, sent[("w_uq", "w_ukv")]["token"],
                                         t_rnn, "rglru_bwd")
    grad_rep = dict(
        conv_b=dpv[4:5], w_rec_a=dwa, b_rec_a=dpv[5:6], w_rec_i=dwi, b_rec_i=dpv[6:7], lru_lambda=dpv[7:8],
        q_norm_g=dg_q, kv_norm_g=dg_kv, norm_ffn_g=dg_ffn, final_norm_g=dg_final)
    rep_now = tuple(grad_rep)
    sent[("b_gate", "conv_w") + rep_now] = _exchange_start(
        [blocks("b_gate", dbg.reshape(2, d_model)), blocks("conv_w", dpv[:CONV_WIDTH])]
        + [_as2d(grad_rep[n]).astype(BF16 if n in ("w_rec_a", "w_rec_i") else F32) for n in rep_now],
        [False] * 2 + [True] * len(rep_now), dg_kv, "scatter_small_start")
    d_z = _mm([(dux, w_x), (dug, w_g), (duq, w_q), (dukv, w_kv), (dukr, w_kr), (d_um, w_m)], "nt", "d_z")
    dw_in = jnp.concatenate([
        _mm_tn(z, dux, "dw_in_x"), _mm_tn(z, dug, "dw_in_g"), _mm_tn(z, duq, "dw_in_q"),
        _mm_tn(z, dukv, "dw_in_kv"), _mm_tn(z, dukr, "dw_in_kr")[:, :QK_ROPE], _mm_tn(z, d_um, "dw_in_m")], axis=1)
    last_h = _exchange_start([blocks("w_in", dw_in)], [False], sent[("b_gate", "conv_w") + rep_now]["token"],
                             "scatter_in_start")
    sent[("w_in",)] = last_h
    dh0, dg_mix = _rmsnorm_bwd(h0, d_z, norm_mix_g, "norm_mix_bwd", res=dh1, dep=last_h["token"])
    sent[("meta_tokens", "norm_mix_g")] = _exchange_start([blocks("meta_tokens", dh0[:N_META]), dg_mix], [False, True],
                                                          dg_mix, "scatter_tail_start")
    grads, deltas, new_m, new_v = {}, {}, {}, {}

    def update(n, parts, dep):
        g2, d2, m2, v2 = _adamw(_as2d(w[n]), _as2d(m[n]), _as2d(v[n]), parts, dep, "adamw_" + n)
        for store, val in ((grads, g2), (deltas, d2), (new_m, m2), (new_v, v2)):
            store[n] = val.reshape(w[n].shape)

    chain = sent[("meta_tokens", "norm_mix_g")]["token"]
    for group, handle in sent.items():
        for n, parts in zip(group, _exchange_wait(handle, chain, "scatter_wait_" + group[0])):
            update(n, parts, chain)
            chain = _as2d(deltas[n])[:SUBLANES, :LANES]

    loss = lax.psum(loss_part[0, 0], MESH_AXES)
    grad_x = dh0[N_META:length][None]
    return (loss, grad_x, *[grads[n] for n in WEIGHTS], *[deltas[n] for n in WEIGHTS],
            *[new_m[n] for n in WEIGHTS], *[new_v[n] for n in WEIGHTS])
```

```python
import math

import jax
import jax.numpy as jnp
from jax import lax
from jax.experimental import pallas as pl
from jax.experimental.pallas import tpu as pltpu

F32 = jnp.float32
BF16 = jnp.bfloat16

N_DEV = 8
MESH_AXES = ("x", "y", "c")
LANES = 128
SUBLANES = 8
VMEM_LIMIT = 56 * 1024 * 1024

N_META = 16
CHUNK_SHIFT = 6
CHUNK_BIAS = 64 - N_META
EPS = 1e-6
D_RNN = 1280
RNN_BLOCKS = 10
CONV_WIDTH = 4
LRU_C = 8.0
N_HEADS = 8
QK_NOPE = 128
QK_ROPE = 64
V_DIM = 128
HEAD_PAD = 256
Q_RANK = 384
KV_RANK = 256
ROPE_THETA = 10000.0
ATTN_SCALE = 1.0 / math.sqrt(QK_NOPE + QK_ROPE)
NEG = -1e30
TINY = 1e-30
LOG2E = 1.0 / math.log(2.0)
SCALE_LOG2E = ATTN_SCALE * LOG2E
ROW_GROUP = 32
SPILL = LANES
D_FF = 2816

ADAM_LR = 0.001
ADAM_B1 = 0.9
ADAM_B2 = 0.999
ADAM_EPS = 1e-08
ADAM_WD = 0.01
ADAM_STEP = 10

NN = (((1,), (0,)), ((), ()))
NT = (((1,), (1,)), ((), ()))
TN = (((0,), (0,)), ((), ()))


def _pick(n, cap, base=LANES):
    best = None
    for t in range(base, min(n, cap) + 1, base):
        if n % t == 0:
            best = t
    return best if best is not None else n


def _params(sem=None):
    return pltpu.CompilerParams(dimension_semantics=sem, vmem_limit_bytes=VMEM_LIMIT)


def _sig(x):
    return 0.5 + 0.5 * jnp.tanh(0.5 * x)


def _exchange(srcs, gather, name):
    n = len(srcs)
    out_shape = [jax.ShapeDtypeStruct((N_DEV,) + (s.shape if g else s.shape[1:]), s.dtype)
                 for s, g in zip(srcs, gather)]

    def body(*refs):
        src, dst = refs[:n], refs[n:2 * n]
        send_sems, recv_sems, local_sems = refs[2 * n:]
        x, y, c = lax.axis_index("x"), lax.axis_index("y"), lax.axis_index("c")
        me = 4 * x + 2 * y + c
        local = []
        for t in range(n):
            cp = pltpu.make_async_copy(src[t] if gather[t] else src[t].at[me], dst[t].at[me], local_sems.at[t])
            cp.start()
            local.append(cp)
        sends, recvs = [], []
        for k in range(1, N_DEV):
            px = 1 - x if k & 4 else x
            py = 1 - y if k & 2 else y
            pc = 1 - c if k & 1 else c
            peer = 4 * px + 2 * py + pc
            for t in range(n):
                cp = pltpu.make_async_remote_copy(
                    src_ref=src[t] if gather[t] else src[t].at[peer], dst_ref=dst[t].at[me],
                    send_sem=send_sems.at[t, k - 1], recv_sem=recv_sems.at[t, k - 1],
                    device_id=(px, py, pc), device_id_type=pl.DeviceIdType.MESH)
                cp.start()
                sends.append(cp)
                recvs.append(pltpu.make_async_remote_copy(
                    src_ref=src[t] if gather[t] else src[t].at[peer], dst_ref=dst[t].at[peer],
                    send_sem=send_sems.at[t, k - 1], recv_sem=recv_sems.at[t, k - 1],
                    device_id=(px, py, pc), device_id_type=pl.DeviceIdType.MESH))
        for cp in recvs:
            cp.wait_recv()
        for cp in sends:
            cp.wait_send()
        for cp in local:
            cp.wait()

    any_spec = pl.BlockSpec(memory_space=pl.ANY)
    return pl.pallas_call(
        body, name=name, out_shape=out_shape,
        in_specs=[any_spec] * n, out_specs=[any_spec] * n,
        scratch_shapes=[pltpu.SemaphoreType.DMA((n, N_DEV - 1)), pltpu.SemaphoreType.DMA((n, N_DEV - 1)),
                        pltpu.SemaphoreType.DMA((n,))],
    )(*srcs)


def _gather_two_level(block, name):
    def body(x_ref, out_ref, send_sems, recv_sems, local_sem):
        x, y, c = lax.axis_index("x"), lax.axis_index("y"), lax.axis_index("c")
        me, sibling = (x, y, c), (x, y, 1 - c)
        chips = [(1 - x, y), (x, 1 - y), (1 - x, 1 - y)]

        def slot(px, py, pc):
            return out_ref.at[4 * px + 2 * py + pc]

        def copy(k, owner, to, src=None):
            return pltpu.make_async_remote_copy(
                src_ref=slot(*owner) if src is None else src, dst_ref=slot(*owner),
                send_sem=send_sems.at[k], recv_sem=recv_sems.at[k], device_id=to,
                device_id_type=pl.DeviceIdType.MESH)

        mine = pltpu.make_async_copy(x_ref, slot(*me), local_sem)
        mine.start()
        first = [copy(0, me, sibling, src=x_ref)] + [copy(1 + j, me, (*chip, c), src=x_ref)
                                                     for j, chip in enumerate(chips)]
        for cp in first:
            cp.start()
        passed = [copy(4 + j, (*chip, c), sibling) for j, chip in enumerate(chips)]
        for j, chip in enumerate(chips):
            copy(1 + j, (*chip, c), me).wait_recv()
            passed[j].start()
        copy(0, sibling, me).wait_recv()
        for j, chip in enumerate(chips):
            copy(4 + j, (*chip, 1 - c), me).wait_recv()
        for cp in first + passed:
            cp.wait_send()
        mine.wait()

    any_spec = pl.BlockSpec(memory_space=pl.ANY)
    return pl.pallas_call(
        body, name=name, out_shape=jax.ShapeDtypeStruct((N_DEV,) + block.shape, block.dtype),
        in_specs=[any_spec], out_specs=any_spec,
        scratch_shapes=[pltpu.SemaphoreType.DMA((N_DEV - 1,)), pltpu.SemaphoreType.DMA((N_DEV - 1,)),
                        pltpu.SemaphoreType.DMA],
    )(block)


HBM_SPEC = pl.BlockSpec(memory_space=pltpu.HBM)
SEM_SPEC = pl.BlockSpec(memory_space=pltpu.SEMAPHORE)
DATAFLOW = pltpu.SideEffectType.DATAFLOW_SIDE_EFFECTING


def _peers(x, y, c):
    out = []
    for k in range(1, N_DEV):
        px = 1 - x if k & 4 else x
        py = 1 - y if k & 2 else y
        pc = 1 - c if k & 1 else c
        out.append((k, (px, py, pc), 4 * px + 2 * py + pc))
    return out


def _split_copies(src, land, gather, send_sems, recv_sems, local_sems):
    x, y, c = lax.axis_index("x"), lax.axis_index("y"), lax.axis_index("c")
    me = 4 * x + 2 * y + c
    n = len(src)
    local = [pltpu.make_async_copy(src[t] if gather[t] else src[t].at[me], land[t].at[me], local_sems.at[t])
             for t in range(n)]
    sends, recvs = [], []
    for k, pos, peer in _peers(x, y, c):
        for t in range(n):
            mine = src[t] if gather[t] else src[t].at[peer]
            slot = t * (N_DEV - 1) + k - 1
            common = dict(send_sem=send_sems.at[slot], recv_sem=recv_sems.at[slot], device_id=pos,
                          device_id_type=pl.DeviceIdType.MESH)
            sends.append(pltpu.make_async_remote_copy(src_ref=mine, dst_ref=land[t].at[me], **common))
            recvs.append(pltpu.make_async_remote_copy(src_ref=mine, dst_ref=land[t].at[peer], **common))
    return local, sends, recvs


def _exchange_start(srcs, gather, after, name):
    n = len(srcs)
    lands = [lax.empty((N_DEV,) + (s.shape if g else s.shape[1:]), s.dtype) for s, g in zip(srcs, gather)]

    def body(*refs):
        src, land = refs[:n], refs[n:2 * n]
        send_sems, recv_sems, local_sems = refs[2 * n + 1:2 * n + 4]
        local, sends, _ = _split_copies(src, land, gather, send_sems, recv_sems, local_sems)
        for cp in local + sends:
            cp.start()
        refs[-1][...] = jnp.zeros_like(refs[-1])

    hbm = lambda a: pltpu.HBM(a.shape, a.dtype)
    outs = pl.pallas_call(
        body, name=name,
        out_shape=(pltpu.SemaphoreType.DMA((n * (N_DEV - 1),)), pltpu.SemaphoreType.DMA((n * (N_DEV - 1),)),
                   pltpu.SemaphoreType.DMA((n,)), *[hbm(s) for s in srcs], *[hbm(a) for a in lands],
                   jax.ShapeDtypeStruct((SUBLANES, LANES), F32)),
        in_specs=[HBM_SPEC] * (2 * n) + [pl.BlockSpec(memory_space=pl.ANY)],
        out_specs=(SEM_SPEC, SEM_SPEC, SEM_SPEC, *[HBM_SPEC] * (2 * n), pl.BlockSpec(memory_space=pltpu.VMEM)),
        input_output_aliases={t: 3 + t for t in range(2 * n)},
        compiler_params=pltpu.CompilerParams(has_side_effects=DATAFLOW),
    )(*[pltpu.with_memory_space_constraint(a, pltpu.HBM) for a in list(srcs) + lands], after)
    return dict(sems=outs[:3], srcs=outs[3:3 + n], lands=outs[3 + n:3 + 2 * n], token=outs[-1], gather=gather)


def _exchange_wait(handle, after, name):
    srcs, lands, gather = handle["srcs"], handle["lands"], handle["gather"]
    n = len(srcs)

    def body(*refs):
        src, land = refs[:n], refs[n:2 * n]
        send_sems, recv_sems, local_sems = refs[2 * n:2 * n + 3]
        local, sends, recvs = _split_copies(src, land, gather, send_sems, recv_sems, local_sems)
        for cp in sends:
            cp.wait_send()
        for cp in recvs:
            cp.wait_recv()
        for cp in local:
            cp.wait()

    hbm = lambda a: pltpu.HBM(a.shape, a.dtype)
    outs = pl.pallas_call(
        body, name=name, out_shape=(*[hbm(s) for s in srcs], *[hbm(a) for a in lands]),
        in_specs=[HBM_SPEC] * (2 * n) + [SEM_SPEC] * 3 + [pl.BlockSpec(memory_space=pl.ANY)],
        out_specs=[HBM_SPEC] * (2 * n), input_output_aliases={t: t for t in range(2 * n)},
        compiler_params=pltpu.CompilerParams(has_side_effects=DATAFLOW),
    )(*srcs, *lands, *handle["sems"], after)
    return outs[n:]


def _mm(pairs, mode, name, res=None, out_dtype=F32):
    pairs = [p if len(p) == 4 else (p[0], p[1], p[0].shape[1], 0) for p in pairs]
    m = pairs[0][0].shape[0]
    n = pairs[0][1].shape[1] if mode == "nn" else pairs[0][1].shape[0]
    tm, tn = _pick(m, 640), _pick(n, 1408)
    np_ = len(pairs)
    dims = NN if mode == "nn" else NT

    def body(*refs):
        acc = None
        for s in range(np_):
            d = lax.dot_general(refs[2 * s][...].astype(BF16), refs[2 * s + 1][...].astype(BF16), dims,
                                preferred_element_type=F32)
            acc = d if acc is None else acc + d
        if res is not None:
            acc = acc + refs[2 * np_][...]
        refs[-1][...] = acc.astype(out_dtype)

    in_specs, args = [], []
    for a, b, kt, kb in pairs:
        in_specs.append(pl.BlockSpec((tm, kt), lambda i, j: (i, 0)))
        if mode == "nn":
            in_specs.append(pl.BlockSpec((kt, tn), lambda i, j, kb=kb: (kb, j)))
        else:
            in_specs.append(pl.BlockSpec((tn, kt), lambda i, j, kb=kb: (j, kb)))
        args += [a, b]
    if res is not None:
        in_specs.append(pl.BlockSpec((tm, tn), lambda i, j: (i, j)))
        args.append(res)
    return pl.pallas_call(
        body, name=name, grid=(m // tm, n // tn), in_specs=in_specs,
        out_specs=pl.BlockSpec((tm, tn), lambda i, j: (i, j)),
        out_shape=jax.ShapeDtypeStruct((m, n), out_dtype),
        compiler_params=_params(("parallel", "parallel")),
    )(*args)


def _mm_tn(a, b, name):
    m, k = a.shape
    n = b.shape[1]
    tm, tk, tn = _pick(m, 1664), _pick(k, 1408), _pick(n, 1408)

    def body(a_ref, b_ref, o_ref):
        @pl.when(pl.program_id(2) == 0)
        def _():
            o_ref[...] = jnp.zeros_like(o_ref)

        o_ref[...] += lax.dot_general(a_ref[...].astype(BF16), b_ref[...].astype(BF16), TN,
                                      preferred_element_type=F32)

    return pl.pallas_call(
        body, name=name, grid=(k // tk, n // tn, m // tm),
        in_specs=[pl.BlockSpec((tm, tk), lambda i, j, r: (r, i)), pl.BlockSpec((tm, tn), lambda i, j, r: (r, j))],
        out_specs=pl.BlockSpec((tk, tn), lambda i, j, r: (i, j)),
        out_shape=jax.ShapeDtypeStruct((k, n), F32),
        compiler_params=_params(("parallel", "parallel", "arbitrary")),
    )(a, b)


ROW_TILE_BYTES = 6 * 1024 * 1024


def _row_tile(rows, row_in, row_out):
    per_row = sum((r[1] * r[0].dtype.itemsize) if isinstance(r, tuple) else (r.shape[1] * r.dtype.itemsize)
                  for r in row_in)
    per_row += sum(w * jnp.dtype(dt).itemsize for w, dt in row_out)
    return _pick(rows, min(640, max(LANES, ROW_TILE_BYTES // per_row)))


def _rowcall(body, name, rows, row_in, full_in, row_out, acc_out=()):
    tr = _row_tile(rows, row_in, row_out)
    n_steps = rows // tr
    in_specs, args = [], []
    for r in row_in:
        arr, w, cb = r if isinstance(r, tuple) else (r, r.shape[1], 0)
        in_specs.append(pl.BlockSpec((tr, w), lambda i, cb=cb: (i, cb)))
        args.append(arr)
    for f in full_in:
        in_specs.append(pl.BlockSpec(f.shape, lambda i, nd=f.ndim: (0,) * nd))
        args.append(f)
    out_specs = [pl.BlockSpec((tr, w), lambda i: (i, 0)) for w, _ in row_out]
    out_shape = [jax.ShapeDtypeStruct((rows, w), dt) for w, dt in row_out]
    for shp, dt in acc_out:
        out_specs.append(pl.BlockSpec(shp, lambda i, nd=len(shp): (0,) * nd))
        out_shape.append(jax.ShapeDtypeStruct(shp, dt))

    def wrapped(*refs):
        body(pl.program_id(0), n_steps, *refs)

    return pl.pallas_call(
        wrapped, name=name, grid=(n_steps,), in_specs=in_specs, out_specs=out_specs, out_shape=out_shape,
        compiler_params=_params(("arbitrary",) if acc_out else ("parallel",)),
    )(*args)


def _rmsnorm_fwd(x, g, name):
    rows, w = x.shape

    def body(i, n, x_ref, g_ref, o_ref):
        xv = x_ref[...]
        r = lax.rsqrt(jnp.mean(xv * xv, axis=-1, keepdims=True) + EPS)
        o_ref[...] = (xv * r * g_ref[...]).astype(BF16)

    return _rowcall(body, name, rows,[x], [g], [(w, BF16)])[0]


def _rmsnorm_bwd_math(xv, dy, g):
    w = xv.shape[-1]
    r = lax.rsqrt(jnp.mean(xv * xv, axis=-1, keepdims=True) + EPS)
    t = dy * g
    dx = r * t - xv * (r * r * r * (jnp.sum(t * xv, axis=-1, keepdims=True) / w))
    dg = jnp.sum(dy * xv * r, axis=0, keepdims=True)
    return dx, dg


def _rmsnorm_bwd(x, dy, g, name, res=None, out_dtype=F32, dep=None):
    rows, w = x.shape

    def body(i, n, *refs):
        x_ref, dy_ref = refs[0], refs[1]
        g_ref, dx_ref, dg_ref = refs[-3], refs[-2], refs[-1]
        dx, dg = _rmsnorm_bwd_math(x_ref[...], dy_ref[...], g_ref[...])
        if res is not None:
            dx = dx + refs[2][...]
        dx_ref[...] = dx.astype(out_dtype)

        @pl.when(i == 0)
        def _():
            dg_ref[...] = jnp.zeros_like(dg_ref)

        dg_ref[...] += dg

    row_in = [x, dy] + ([res] if res is not None else [])
    return _rowcall(body, name, rows, row_in, ([dep] if dep is not None else []) + [g], [(w, out_dtype)],
                    [((1, w), F32)])


def _loss_bwd(h2, tgt, g, seq, name):
    rows, w = h2.shape
    tr = _row_tile(rows, [h2, tgt], [(w, F32)])

    def body(i, n, h_ref, t_ref, g_ref, dh_ref, dg_ref, lcol_ref, loss_ref):
        hv, gv = h_ref[...], g_ref[...]
        row = i * tr + lax.broadcasted_iota(jnp.int32, (tr, w), 0)
        valid = jnp.logical_and(row >= N_META, row < N_META + seq)
        r = lax.rsqrt(jnp.mean(hv * hv, axis=-1, keepdims=True) + EPS)
        err = jnp.where(valid, hv * r * gv - t_ref[...], 0.0)
        dx, dg = _rmsnorm_bwd_math(hv, err * (1.0 / w), gv)
        dh_ref[...] = dx

        @pl.when(i == 0)
        def _():
            dg_ref[...] = jnp.zeros_like(dg_ref)
            lcol_ref[...] = jnp.zeros_like(lcol_ref)

        dg_ref[...] += dg
        lcol_ref[...] += jnp.sum(err * err, axis=0, keepdims=True)

        @pl.when(i == n - 1)
        def _():
            total = jnp.sum(lcol_ref[...], axis=1, keepdims=True) * (0.5 / w)
            loss_ref[...] = jnp.broadcast_to(total, loss_ref.shape)

    return _rowcall(body, name, rows, [h2, tgt], [g], [(w, F32)],
                    [((1, w), F32), ((1, w), F32), ((1, LANES), F32)])


def _mix_out(um, p_rnn, p_att, bg, w_out, res, name):
    rows, d = p_rnn.shape
    tr = _pick(rows, 640)

    def body(u0_ref, u1_ref, pr_ref, pa_ref, bg_ref, w_ref, r_ref, mix_ref, o_ref):
        g0 = _sig(u0_ref[...].astype(F32) + bg_ref[:, :d])
        g1 = _sig(u1_ref[...].astype(F32) + bg_ref[:, d:])
        mixed = (g0 * pr_ref[...].astype(F32) + g1 * pa_ref[...].astype(F32)).astype(BF16)
        mix_ref[...] = mixed
        o_ref[...] = r_ref[...] + jnp.dot(mixed, w_ref[...], preferred_element_type=F32)

    row = pl.BlockSpec((tr, d), lambda i: (i, 0))
    whole = lambda a: pl.BlockSpec(a.shape, lambda i: (0, 0))
    return pl.pallas_call(
        body, name=name, grid=(rows // tr,),
        in_specs=[row, pl.BlockSpec((tr, d), lambda i: (i, 1)), row, row, whole(bg), whole(w_out), row],
        out_specs=[row, row],
        out_shape=[jax.ShapeDtypeStruct((rows, d), BF16), jax.ShapeDtypeStruct((rows, d), F32)],
        compiler_params=_params(("parallel",)),
    )(um, um, p_rnn, p_att, bg, w_out, res)


def _mix_bwd(um, p_rnn, p_att, dmix, bg, name):
    rows, d = p_rnn.shape

    def body(i, n, u0_ref, u1_ref, pr_ref, pa_ref, dm_ref, bg_ref, dpr_ref, dpa_ref, dum_ref, dbg_ref):
        g0 = _sig(u0_ref[...].astype(F32) + bg_ref[:, :d])
        g1 = _sig(u1_ref[...].astype(F32) + bg_ref[:, d:])
        dm = dm_ref[...].astype(F32)
        dpr_ref[...] = (dm * g0).astype(BF16)
        dpa_ref[...] = (dm * g1).astype(BF16)
        du0 = dm * pr_ref[...].astype(F32) * g0 * (1.0 - g0)
        du1 = dm * pa_ref[...].astype(F32) * g1 * (1.0 - g1)
        dum_ref[:, :d] = du0.astype(BF16)
        dum_ref[:, d:] = du1.astype(BF16)

        @pl.when(i == 0)
        def _():
            dbg_ref[...] = jnp.zeros_like(dbg_ref)

        dbg_ref[:, :d] += jnp.sum(du0, axis=0, keepdims=True)
        dbg_ref[:, d:] += jnp.sum(du1, axis=0, keepdims=True)

    return _rowcall(body, name, rows,[(um, d, 0), (um, d, 1), p_rnn, p_att, dmix], [bg],
                    [(d, BF16), (d, BF16), (2 * d, BF16)], [((1, 2 * d), F32)])


def _ffn_in_swiglu(zf, w, name):
    m, k = zf.shape
    f = w.shape[1] // 2
    tm, tn = _pick(m, 640), _pick(f, 1408)
    nb = f // tn

    def body(a_ref, bg_ref, bu_ref, g_ref, u_ref, act_ref):
        a = a_ref[...]
        gate = jnp.dot(a, bg_ref[...], preferred_element_type=F32)
        up = jnp.dot(a, bu_ref[...], preferred_element_type=F32)
        g_ref[...] = gate.astype(BF16)
        u_ref[...] = up.astype(BF16)
        act_ref[...] = (gate * _sig(gate) * up).astype(BF16)

    tile = pl.BlockSpec((tm, tn), lambda i, j: (i, j))
    return pl.pallas_call(
        body, name=name, grid=(m // tm, nb),
        in_specs=[pl.BlockSpec((tm, k), lambda i, j: (i, 0)), pl.BlockSpec((k, tn), lambda i, j: (0, j)),
                  pl.BlockSpec((k, tn), lambda i, j: (0, j + nb))],
        out_specs=[tile] * 3, out_shape=[jax.ShapeDtypeStruct((m, f), BF16)] * 3,
        compiler_params=_params(("parallel", "parallel")),
    )(zf, w, w)


def _ffn_out_bwd_swiglu(dh, w_out, gate, up, name):
    m, k = dh.shape
    f = w_out.shape[0]
    tm, tn = _pick(m, 640), _pick(f, 1408)

    def body(a_ref, b_ref, g_ref, u_ref, dg_ref, du_ref):
        da = lax.dot_general(a_ref[...].astype(BF16), b_ref[...], NT, preferred_element_type=F32)
        gv = g_ref[...].astype(F32)
        sg = _sig(gv)
        dg_ref[...] = (da * u_ref[...].astype(F32) * (sg * (1.0 + gv * (1.0 - sg)))).astype(BF16)
        du_ref[...] = (da * gv * sg).astype(BF16)

    tile = pl.BlockSpec((tm, tn), lambda i, j: (i, j))
    return pl.pallas_call(
        body, name=name, grid=(m // tm, f // tn),
        in_specs=[pl.BlockSpec((tm, k), lambda i, j: (i, 0)), pl.BlockSpec((tn, k), lambda i, j: (j, 0)), tile, tile],
        out_specs=[tile] * 2, out_shape=[jax.ShapeDtypeStruct((m, f), BF16)] * 2,
        compiler_params=_params(("parallel", "parallel")),
    )(dh, w_out, gate, up)


def _rope_tables(lp):
    idx = jnp.arange(lp, dtype=jnp.int32).astype(F32)
    inv_freq = ROPE_THETA ** (-jnp.arange(0, QK_ROPE, 2, dtype=F32) / QK_ROPE)
    ang = idx[:, None] * inv_freq[None, :]
    cos, sin = jnp.cos(ang), jnp.sin(ang)
    half = QK_ROPE // 2
    z = lambda wdt: jnp.zeros((lp, wdt), F32)
    tc = jnp.concatenate([cos, cos, z(LANES - 2 * half)], axis=1)
    ts1 = jnp.concatenate([-sin, z(LANES - half)], axis=1)
    ts2 = jnp.concatenate([z(half), sin, z(LANES - 2 * half)], axis=1)
    return tc, ts1, ts2


def _rope(xv, tc, ts1, ts2):
    half = QK_ROPE // 2
    return xv * tc + pltpu.roll(xv, LANES - half, 1) * ts1 + pltpu.roll(xv, half, 1) * ts2


def _rope_t(dv, tc, ts1, ts2):
    half = QK_ROPE // 2
    return dv * tc + pltpu.roll(dv * ts1, half, 1) + pltpu.roll(dv * ts2, LANES - half, 1)


def _up_rope(xn, w_pad, tabs, name, ukr=None):
    rows, k = xn.shape
    n = w_pad.shape[1]
    tr = _pick(rows, 640)

    def body(*refs):
        x_ref, w_ref, c_ref, s1_ref, s2_ref = refs[:5]
        o_ref = refs[-1]
        tc, ts1, ts2 = c_ref[...], s1_ref[...], s2_ref[...]
        y = jnp.dot(x_ref[...], w_ref[...], preferred_element_type=F32)
        kr = None if ukr is None else _rope(refs[5][...], tc, ts1, ts2).astype(BF16)
        for h in range(N_HEADS):
            lo, mid, hi = h * HEAD_PAD, h * HEAD_PAD + QK_NOPE, (h + 1) * HEAD_PAD
            o_ref[:, lo:mid] = y[:, lo:mid].astype(BF16)
            o_ref[:, mid:hi] = _rope(y[:, mid:hi], tc, ts1, ts2).astype(BF16) if ukr is None else kr

    row = lambda wdt: pl.BlockSpec((tr, wdt), lambda i: (i, 0))
    in_specs = [row(k), pl.BlockSpec((k, n), lambda i: (0, 0)), row(LANES), row(LANES), row(LANES)]
    args = [xn, w_pad, *tabs]
    if ukr is not None:
        in_specs.append(row(LANES))
        args.append(ukr)
    return pl.pallas_call(
        body, name=name, grid=(rows // tr,), in_specs=in_specs, out_specs=row(n),
        out_shape=jax.ShapeDtypeStruct((rows, n), BF16), compiler_params=_params(("parallel",)),
    )(*args)


def _rope_bwd(dq, dk, tabs, name):
    rows = dq.shape[0]

    def body(i, n, *refs):
        q_refs, k_refs = refs[:N_HEADS], refs[N_HEADS:2 * N_HEADS]
        c_ref, s1_ref, s2_ref, qo_ref, ro_ref = refs[2 * N_HEADS:]
        tc, ts1, ts2 = c_ref[...], s1_ref[...], s2_ref[...]
        dkr = None
        for h in range(N_HEADS):
            qo_ref[:, h * LANES:(h + 1) * LANES] = _rope_t(q_refs[h][...].astype(F32), tc, ts1, ts2).astype(BF16)
            part = k_refs[h][...].astype(F32)
            dkr = part if dkr is None else dkr + part
        ro_ref[...] = _rope_t(dkr, tc, ts1, ts2).astype(BF16)

    rope_lanes = lambda a: [(a, LANES, 2 * h + 1) for h in range(N_HEADS)]
    return _rowcall(body, name, rows, rope_lanes(dq) + rope_lanes(dk) + list(tabs), [],
                    [(N_HEADS * LANES, BF16), (LANES, BF16)])


def _visible(q0, k0, tq, tk):
    qrow = q0 + lax.broadcasted_iota(jnp.int32, (tq, tk), 0)
    kcol = k0 + lax.broadcasted_iota(jnp.int32, (tq, tk), 1)
    return ((kcol + CHUNK_BIAS) >> CHUNK_SHIFT) <= ((qrow + CHUNK_BIAS) >> CHUNK_SHIFT)


def _visible_t(k0, q0, tk, tq):
    krow = k0 + lax.broadcasted_iota(jnp.int32, (tk, tq), 0)
    qcol = q0 + lax.broadcasted_iota(jnp.int32, (tk, tq), 1)
    return ((krow + CHUNK_BIAS) >> CHUNK_SHIFT) <= ((qcol + CHUNK_BIAS) >> CHUNK_SHIFT)


def _lanes(v, width):
    return jnp.tile(v, (1, width // LANES))


def _pipelined_chunks(n_full, scores, absorb):
    scores(0, 0)

    def pair(jj, carry):
        a = 2 * jj
        scores(a + 1, 1)
        absorb(a, 0, False)
        scores(a + 2, 0)
        absorb(a + 1, 1, False)
        return carry

    lax.fori_loop(0, n_full // 2, pair, 0)

    @pl.when(n_full % 2 == 0)
    def _():
        absorb(n_full, 0, True)

    @pl.when(n_full % 2 == 1)
    def _():
        scores(n_full, 1)
        absorb(n_full - 1, 0, False)
        absorb(n_full, 1, True)


def _attn_fwd(q, k, v, t, name):
    lp = q.shape[0]
    nt = lp // t

    def body(q_ref, k_ref, v_ref, o_ref, lse_ref, m_s, l_s, acc_s, a_s, sa_s, sb_s, p_s):
        i = pl.program_id(1)
        m_s[...] = jnp.full(m_s.shape, NEG, F32)
        l_s[...] = jnp.zeros(l_s.shape, F32)
        acc_s[...] = jnp.zeros(acc_s.shape, F32)

        s_bufs = (sa_s, sb_s)

        def scores(j, slot):
            r0 = pl.multiple_of(j * t, t)
            s_bufs[slot][...] = lax.dot_general(q_ref[...], k_ref[pl.ds(r0, t), :], NT,
                                                preferred_element_type=F32)

        def absorb(j, slot, masked):
            for r in range(0, t, ROW_GROUP):
                rows = slice(r, r + ROW_GROUP)
                s = s_bufs[slot][rows, :]
                if masked:
                    s = jnp.where(_visible(i * t + r, i * t, ROW_GROUP, t), s, NEG)
                m_prev = m_s[rows, :]
                m_new = jnp.maximum(m_prev, jnp.max(s, axis=1, keepdims=True))
                alpha = jnp.exp2((m_prev - m_new) * SCALE_LOG2E)
                p = jnp.exp2((s - _lanes(m_new, t)) * SCALE_LOG2E)
                l_s[rows, :] = alpha * l_s[rows, :] + jnp.sum(p, axis=1, keepdims=True)
                m_s[rows, :] = m_new
                a_s[rows, :] = alpha
                p_s[rows, :] = p.astype(BF16)
            r0 = pl.multiple_of(j * t, t)
            acc_s[...] = a_s[...] * acc_s[...] + jnp.dot(p_s[...], v_ref[pl.ds(r0, t), :],
                                                         preferred_element_type=F32)

        r1 = pl.multiple_of(jnp.minimum(i + 1, nt - 1) * t, t)
        rows = slice(t - SPILL, t)
        s = lax.dot_general(q_ref[rows, :], k_ref[pl.ds(r1, SPILL), :], NT, preferred_element_type=F32)
        seen = jnp.logical_and(_visible(i * t + t - SPILL, (i + 1) * t, SPILL, SPILL), i + 1 < nt)
        s = jnp.where(seen, s, NEG)
        m_new = jnp.max(s, axis=1, keepdims=True)
        p = jnp.exp2((s - m_new) * SCALE_LOG2E)
        l_s[rows, :] = jnp.broadcast_to(jnp.sum(p, axis=1, keepdims=True), (SPILL, LANES))
        acc_s[rows, :] = jnp.dot(p.astype(BF16), v_ref[pl.ds(r1, SPILL), :], preferred_element_type=F32)
        m_s[rows, :] = jnp.broadcast_to(m_new, (SPILL, LANES))

        _pipelined_chunks(i, scores, absorb)
        o_ref[...] = (acc_s[...] / l_s[...]).astype(BF16)
        lse_ref[...] = m_s[...] * ATTN_SCALE + jnp.log(l_s[...])

    return pl.pallas_call(
        body, name=name, grid=(N_HEADS, nt),
        in_specs=[pl.BlockSpec((t, HEAD_PAD), lambda h, i: (i, h)),
                  pl.BlockSpec((lp, HEAD_PAD), lambda h, i: (0, h)),
                  pl.BlockSpec((lp, V_DIM), lambda h, i: (0, h))],
        out_specs=[pl.BlockSpec((t, V_DIM), lambda h, i: (i, h)),
                   pl.BlockSpec((None, t, LANES), lambda h, i: (h, i, 0))],
        out_shape=[jax.ShapeDtypeStruct((lp, N_HEADS * V_DIM), BF16),
                   jax.ShapeDtypeStruct((N_HEADS, lp, LANES), F32)],
        scratch_shapes=[pltpu.VMEM((t, LANES), F32), pltpu.VMEM((t, LANES), F32), pltpu.VMEM((t, V_DIM), F32),
                        pltpu.VMEM((t, LANES), F32), pltpu.VMEM((t, t), F32), pltpu.VMEM((t, t), F32),
                        pltpu.VMEM((t, t), BF16)],
        compiler_params=_params(("parallel", "arbitrary")),
    )(q, k, v)


def _attn_bwd_dq(q, k, v, do, o, lse, dep, t, name):
    lp = q.shape[0]
    nt = lp // t

    def body(q_ref, k_ref, v_ref, do_ref, o_ref, lse_ref, dep_ref, dq_ref, l2row_ref, dlrow_ref,
             acc_s, l2_s, dl_s, sa_s, sb_s, da_s, db_s, ds_s):
        i = pl.program_id(1)
        delta = jnp.sum(do_ref[...].astype(F32) * o_ref[...].astype(F32), axis=1, keepdims=True)
        dl_s[...] = jnp.broadcast_to(delta, dl_s.shape)
        l2_s[...] = lse_ref[...] * LOG2E
        l2row_ref[...] = l2_s[...].T[0:SUBLANES, :]
        dlrow_ref[...] = dl_s[...].T[0:SUBLANES, :]
        acc_s[...] = jnp.zeros(acc_s.shape, F32)
        s_bufs, d_bufs = (sa_s, sb_s), (da_s, db_s)

        def dscores(s, dp, rows, width):
            p = jnp.exp2(s * SCALE_LOG2E - _lanes(l2_s[rows, :], width))
            return (p * (dp - _lanes(dl_s[rows, :], width))).astype(BF16)

        r1 = pl.multiple_of(jnp.minimum(i + 1, nt - 1) * t, t)
        rows = slice(t - SPILL, t)
        ks, vs = k_ref[pl.ds(r1, SPILL), :], v_ref[pl.ds(r1, SPILL), :]
        s = lax.dot_general(q_ref[rows, :], ks, NT, preferred_element_type=F32)
        seen = jnp.logical_and(_visible(i * t + t - SPILL, (i + 1) * t, SPILL, SPILL), i + 1 < nt)
        s = jnp.where(seen, s, NEG)
        dp = lax.dot_general(do_ref[rows, :], vs, NT, preferred_element_type=F32)
        acc_s[rows, :] = jnp.dot(dscores(s, dp, rows, SPILL), ks, preferred_element_type=F32)

        def scores(j, slot):
            r0 = pl.multiple_of(j * t, t)
            s_bufs[slot][...] = lax.dot_general(q_ref[...], k_ref[pl.ds(r0, t), :], NT,
                                                preferred_element_type=F32)
            d_bufs[slot][...] = lax.dot_general(do_ref[...], v_ref[pl.ds(r0, t), :], NT,
                                                preferred_element_type=F32)

        def absorb(j, slot, masked):
            for r in range(0, t, ROW_GROUP):
                rows = slice(r, r + ROW_GROUP)
                s = s_bufs[slot][rows, :]
                if masked:
                    s = jnp.where(_visible(i * t + r, i * t, ROW_GROUP, t), s, NEG)
                ds_s[rows, :] = dscores(s, d_bufs[slot][rows, :], rows, t)
            r0 = pl.multiple_of(j * t, t)
            acc_s[...] += jnp.dot(ds_s[...], k_ref[pl.ds(r0, t), :], preferred_element_type=F32)

        _pipelined_chunks(i, scores, absorb)
        dq_ref[...] = (acc_s[...] * ATTN_SCALE).astype(BF16)

    stat_row = pl.BlockSpec((None, None, SUBLANES, t), lambda h, i: (h, i, 0, 0))
    return pl.pallas_call(
        body, name=name, grid=(N_HEADS, nt),
        in_specs=[pl.BlockSpec((t, HEAD_PAD), lambda h, i: (i, h)),
                  pl.BlockSpec((lp, HEAD_PAD), lambda h, i: (0, h)),
                  pl.BlockSpec((lp, V_DIM), lambda h, i: (0, h)),
                  pl.BlockSpec((t, V_DIM), lambda h, i: (i, h)),
                  pl.BlockSpec((t, V_DIM), lambda h, i: (i, h)),
                  pl.BlockSpec((None, t, LANES), lambda h, i: (h, i, 0)),
                  pl.BlockSpec(dep.shape, lambda h, i: (0, 0))],
        out_specs=[pl.BlockSpec((t, HEAD_PAD), lambda h, i: (i, h)), stat_row, stat_row],
        out_shape=[jax.ShapeDtypeStruct((lp, N_HEADS * HEAD_PAD), BF16),
                   jax.ShapeDtypeStruct((N_HEADS, nt, SUBLANES, t), F32),
                   jax.ShapeDtypeStruct((N_HEADS, nt, SUBLANES, t), F32)],
        scratch_shapes=[pltpu.VMEM((t, HEAD_PAD), F32), pltpu.VMEM((t, LANES), F32), pltpu.VMEM((t, LANES), F32),
                        pltpu.VMEM((t, t), F32), pltpu.VMEM((t, t), F32), pltpu.VMEM((t, t), F32),
                        pltpu.VMEM((t, t), F32), pltpu.VMEM((t, t), BF16)],
        compiler_params=_params(("parallel", "arbitrary")),
    )(q, k, v, do, o, lse, dep)


def _attn_bwd_dkv(q, k, v, do, l2row, dlrow, t, name):
    lp = q.shape[0]
    nt = lp // t

    def body(q_ref, k_ref, v_ref, do_ref, l2_ref, dl_ref, dk_ref, dv_ref,
             dk_s, dv_s, sa_s, sb_s, da_s, db_s, p_s, ds_s):
        j = pl.program_id(1)
        dk_s[...] = jnp.zeros(dk_s.shape, F32)
        dv_s[...] = jnp.zeros(dv_s.shape, F32)
        s_bufs, d_bufs = (sa_s, sb_s), (da_s, db_s)

        def weights(st, dpt, l2r, dlr):
            pt = jnp.exp2(st * SCALE_LOG2E - l2r)
            return pt.astype(BF16), (pt * (dpt - dlr)).astype(BF16)

        prev = jnp.maximum(j - 1, 0)
        q0 = pl.multiple_of(prev * t + t - SPILL, SPILL)
        rows = slice(0, SPILL)
        qs, dos = q_ref[pl.ds(q0, SPILL), :], do_ref[pl.ds(q0, SPILL), :]
        st = lax.dot_general(k_ref[rows, :], qs, NT, preferred_element_type=F32)
        seen = jnp.logical_and(_visible_t(j * t, j * t - SPILL, SPILL, SPILL), j > 0)
        st = jnp.where(seen, st, NEG)
        dpt = lax.dot_general(v_ref[rows, :], dos, NT, preferred_element_type=F32)
        pt, dst = weights(st, dpt, l2_ref[prev, 0:1, t - SPILL:], dl_ref[prev, 0:1, t - SPILL:])
        dv_s[rows, :] = jnp.dot(pt, dos, preferred_element_type=F32)
        dk_s[rows, :] = jnp.dot(dst, qs, preferred_element_type=F32)

        def scores(c, slot):
            r0 = pl.multiple_of((nt - 1 - c) * t, t)
            s_bufs[slot][...] = lax.dot_general(k_ref[...], q_ref[pl.ds(r0, t), :], NT,
                                                preferred_element_type=F32)
            d_bufs[slot][...] = lax.dot_general(v_ref[...], do_ref[pl.ds(r0, t), :], NT,
                                                preferred_element_type=F32)

        def absorb(c, slot, masked):
            i = nt - 1 - c
            l2r, dlr = l2_ref[i, 0:1, :], dl_ref[i, 0:1, :]
            for r in range(0, t, ROW_GROUP):
                rows = slice(r, r + ROW_GROUP)
                st = s_bufs[slot][rows, :]
                if masked:
                    st = jnp.where(_visible_t(j * t + r, j * t, ROW_GROUP, t), st, NEG)
                p_s[rows, :], ds_s[rows, :] = weights(st, d_bufs[slot][rows, :], l2r, dlr)
            r0 = pl.multiple_of(i * t, t)
            dv_s[...] += jnp.dot(p_s[...], do_ref[pl.ds(r0, t), :], preferred_element_type=F32)
            dk_s[...] += jnp.dot(ds_s[...], q_ref[pl.ds(r0, t), :], preferred_element_type=F32)

        _pipelined_chunks(nt - 1 - j, scores, absorb)
        dk_ref[...] = (dk_s[...] * ATTN_SCALE).astype(BF16)
        dv_ref[...] = dv_s[...].astype(BF16)

    stat_rows = pl.BlockSpec((None, nt, SUBLANES, t), lambda h, j: (h, 0, 0, 0))
    return pl.pallas_call(
        body, name=name, grid=(N_HEADS, nt),
        in_specs=[pl.BlockSpec((lp, HEAD_PAD), lambda h, j: (0, h)),
                  pl.BlockSpec((t, HEAD_PAD), lambda h, j: (j, h)),
                  pl.BlockSpec((t, V_DIM), lambda h, j: (j, h)),
                  pl.BlockSpec((lp, V_DIM), lambda h, j: (0, h)),
                  stat_rows, stat_rows],
        out_specs=[pl.BlockSpec((t, HEAD_PAD), lambda h, j: (j, h)),
                   pl.BlockSpec((t, V_DIM), lambda h, j: (j, h))],
        out_shape=[jax.ShapeDtypeStruct((lp, N_HEADS * HEAD_PAD), BF16),
                   jax.ShapeDtypeStruct((lp, N_HEADS * V_DIM), BF16)],
        scratch_shapes=[pltpu.VMEM((t, HEAD_PAD), F32), pltpu.VMEM((t, V_DIM), F32),
                        pltpu.VMEM((t, t), F32), pltpu.VMEM((t, t), F32), pltpu.VMEM((t, t), F32),
                        pltpu.VMEM((t, t), F32), pltpu.VMEM((t, t), BF16), pltpu.VMEM((t, t), BF16)],
        compiler_params=_params(("parallel", "arbitrary")),
    )(q, k, v, do, l2row, dlrow)


def _shift_down(cur, prev8, k):
    r = pltpu.roll(cur, k, 0)
    row8 = lax.broadcasted_iota(jnp.int32, prev8.shape, 0)
    first = jnp.where(row8 < k, pltpu.roll(prev8, k, 0), r[0:SUBLANES])
    return jnp.concatenate([first, r[SUBLANES:]], axis=0)


def _shift_up(cur, next8, k):
    t = cur.shape[0]
    r = pltpu.roll(cur, t - k, 0)
    row8 = lax.broadcasted_iota(jnp.int32, next8.shape, 0)
    last = jnp.where(row8 >= SUBLANES - k, pltpu.roll(next8, SUBLANES - k, 0), r[t - SUBLANES:])
    return jnp.concatenate([r[:t - SUBLANES], last], axis=0)


def _scan_rows(a, b, edge, reverse):
    t, d = a.shape
    groups = t // SUBLANES
    a3, b3 = a.reshape(groups, SUBLANES, d), b.reshape(groups, SUBLANES, d)
    sub = lax.broadcasted_iota(jnp.int32, a3.shape, 1)
    s = 1
    while s < SUBLANES:
        keep = sub < SUBLANES - s if reverse else sub >= s
        shift = SUBLANES - s if reverse else s
        a_sh = jnp.where(keep, pltpu.roll(a3, shift, 1), 1.0)
        b_sh = jnp.where(keep, pltpu.roll(b3, shift, 1), 0.0)
        b3 = a3 * b_sh + b3
        a3 = a3 * a_sh
        s *= 2
    out = [None] * groups
    for k in (range(groups - 1, -1, -1) if reverse else range(groups)):
        out[k] = b3[k] + a3[k] * edge
        edge = out[k][0:1, :] if reverse else out[k][SUBLANES - 1:SUBLANES, :]
    return jnp.concatenate(out, axis=0)


def _sqrt_one_minus_exp2x(x):
    th = jnp.tanh(x)
    m2 = (-2.0 * th) / (1.0 - th)
    return m2 * lax.rsqrt(jnp.maximum(m2, TINY))


def _log_sigmoid(x):
    return jnp.minimum(x, 0.0) - jnp.log(1.0 + jnp.exp(-jnp.abs(x)))


GELU_C = math.sqrt(2.0 / math.pi)
GELU_K = 0.044715


def _gelu(x):
    th = jnp.tanh(GELU_C * (x + GELU_K * x * x * x))
    return 0.5 * x * (1.0 + th), th


def _block_mm(xb, w_ref, dims):
    rb = D_RNN // RNN_BLOCKS
    return jnp.concatenate(
        [lax.dot_general(xb[:, h * rb:(h + 1) * rb], w_ref[h], dims, preferred_element_type=F32)
         for h in range(RNN_BLOCKS)], axis=1)


def _rglru_gates(ux, prev8, pv_ref, wa_ref, wi_ref):
    shifted = [ux] + [_shift_down(ux, prev8, k) for k in range(1, CONV_WIDTH)]
    xc = pv_ref[4:5, :] + pv_ref[3:4, :] * ux
    for k in range(1, CONV_WIDTH):
        xc = xc + pv_ref[3 - k:4 - k, :] * shifted[k]
    xcb = xc.astype(BF16)
    r_g = _sig(_block_mm(xcb, wa_ref, NN) + pv_ref[5:6, :])
    i_g = _sig(_block_mm(xcb, wi_ref, NN) + pv_ref[6:7, :])
    log_a = LRU_C * r_g * _log_sigmoid(pv_ref[7:8, :])
    a = jnp.exp(log_a)
    mm = _sqrt_one_minus_exp2x(log_a)
    return dict(shifted=shifted, xc=xc, xcb=xcb, r=r_g, i=i_g, a=a, mm=mm)


def _rglru_fwd(ux, ug, pv, wa, wi, t, name):
    lp, d = ux.shape

    def body(ux_ref, ug_ref, pv_ref, wa_ref, wi_ref, y_ref, h_ref, tail_s, hc_s):
        @pl.when(pl.program_id(0) == 0)
        def _():
            tail_s[...] = jnp.zeros_like(tail_s)
            hc_s[...] = jnp.zeros_like(hc_s)

        uxv = ux_ref[...]
        gt = _rglru_gates(uxv, tail_s[...], pv_ref, wa_ref, wi_ref)
        tail_s[...] = ux_ref[t - SUBLANES:t, :]
        h_ref[...] = _scan_rows(gt["a"], gt["mm"] * (gt["i"] * gt["xc"]), hc_s[0:1, :], False)
        hc_s[...] = h_ref[t - SUBLANES:t, :]
        hc_s[0:1, :] = h_ref[t - 1:t, :]
        y_ref[...] = (h_ref[...] * _gelu(ug_ref[...])[0]).astype(BF16)

    tile = pl.BlockSpec((t, d), lambda i: (i, 0))
    return pl.pallas_call(
        body, name=name, grid=(lp // t,),
        in_specs=[tile, tile, pl.BlockSpec(pv.shape, lambda i: (0, 0)),
                  pl.BlockSpec(wa.shape, lambda i: (0, 0, 0)), pl.BlockSpec(wi.shape, lambda i: (0, 0, 0))],
        out_specs=[tile, tile],
        out_shape=[jax.ShapeDtypeStruct((lp, d), BF16), jax.ShapeDtypeStruct((lp, d), F32)],
        scratch_shapes=[pltpu.VMEM((SUBLANES, d), F32), pltpu.VMEM((SUBLANES, d), F32)],
        compiler_params=_params(("arbitrary",)),
    )(ux, ug, pv, wa, wi)


def _rglru_bwd(ux, ug, hs, dy, pv, wa, wi, dep, t, name):
    lp, d = ux.shape
    nt = lp // t
    per = t // SUBLANES
    rb = d // RNN_BLOCKS

    def body(ux_ref, uxp_ref, ug_ref, h_ref, hp_ref, dy_ref, pv_ref, wa_ref, wi_ref, dep_ref,
             dux_ref, dug_ref, dpv_ref, dwa_ref, dwi_ref, ca_s, cg_s, cx_s):
        step = pl.program_id(0)
        first_tile = step == nt - 1

        @pl.when(step == 0)
        def _():
            for ref in (ca_s, cg_s, cx_s, dpv_ref, dwa_ref, dwi_ref):
                ref[...] = jnp.zeros_like(ref)

        uxv = ux_ref[...]
        prev8 = jnp.where(first_tile, 0.0, uxp_ref[...])
        hprev8 = jnp.where(first_tile, 0.0, hp_ref[...])
        gt = _rglru_gates(uxv, prev8, pv_ref, wa_ref, wi_ref)
        a, mm, r_g, i_g, xc = gt["a"], gt["mm"], gt["r"], gt["i"], gt["xc"]
        hv = h_ref[...]
        hprev = _shift_down(hv, hprev8, 1)
        ugv, dyv = ug_ref[...], dy_ref[...]
        gel, th = _gelu(ugv)
        dgel = 0.5 * (1.0 + th) + 0.5 * ugv * (1.0 - th * th) * (GELU_C * (1.0 + 3.0 * GELU_K * ugv * ugv))
        dug_ref[...] = (dyv * hv * dgel).astype(BF16)
        a_up = _shift_up(a, ca_s[...], 1)
        gv = _scan_rows(a_up, dyv * gel, cg_s[0:1, :], True)
        ca_s[...] = a[0:SUBLANES]
        cg_s[...] = gv[0:SUBLANES]
        ixc = i_g * xc
        d_ixc = gv * mm
        d_log_a = gv * hprev * a - (gv * ixc) * (a * a) / mm
        logsig = _log_sigmoid(pv_ref[7:8, :])
        d_pre_a = d_log_a * (LRU_C * logsig) * r_g * (1.0 - r_g)
        d_pre_i = d_ixc * xc * i_g * (1.0 - i_g)
        dab, dib = d_pre_a.astype(BF16), d_pre_i.astype(BF16)
        d_xc = d_ixc * i_g + _block_mm(dab, wa_ref, NT) + _block_mm(dib, wi_ref, NT)
        xcb = gt["xcb"]
        for h in range(RNN_BLOCKS):
            cols = slice(h * rb, (h + 1) * rb)
            dwa_ref[h] += lax.dot_general(xcb[:, cols], dab[:, cols], TN, preferred_element_type=F32)
            dwi_ref[h] += lax.dot_general(xcb[:, cols], dib[:, cols], TN, preferred_element_type=F32)
        csum = lambda v: jnp.sum(v, axis=0, keepdims=True)
        for k in range(CONV_WIDTH):
            dpv_ref[3 - k:4 - k, :] += csum(d_xc * gt["shifted"][k])
        dpv_ref[4:5, :] += csum(d_xc)
        dpv_ref[5:6, :] += csum(d_pre_a)
        dpv_ref[6:7, :] += csum(d_pre_i)
        dpv_ref[7:8, :] += csum(d_log_a * (LRU_C * r_g)) * _sig(-pv_ref[7:8, :])
        dux = pv_ref[3:4, :] * d_xc
        for k in range(1, CONV_WIDTH):
            dux = dux + pv_ref[3 - k:4 - k, :] * _shift_up(d_xc, cx_s[...], k)
        cx_s[...] = d_xc[0:SUBLANES]
        dux_ref[...] = dux.astype(BF16)

    rev = lambda i: (nt - 1 - i, 0)
    before = lambda i: (jnp.maximum((nt - 1 - i) * per - 1, 0), 0)
    tile = pl.BlockSpec((t, d), rev)
    tail = pl.BlockSpec((SUBLANES, d), before)
    fixed2 = lambda arr: pl.BlockSpec(arr.shape, lambda i: (0, 0))
    fixed3 = lambda arr: pl.BlockSpec(arr.shape, lambda i: (0, 0, 0))
    return pl.pallas_call(
        body, name=name, grid=(nt,),
        in_specs=[tile, tail, tile, tile, tail, tile, fixed2(pv), fixed3(wa), fixed3(wi), fixed2(dep)],
        out_specs=[tile, tile, fixed2(pv), fixed3(wa), fixed3(wi)],
        out_shape=[jax.ShapeDtypeStruct((lp, d), BF16), jax.ShapeDtypeStruct((lp, d), BF16),
                   jax.ShapeDtypeStruct(pv.shape, F32), jax.ShapeDtypeStruct(wa.shape, F32),
                   jax.ShapeDtypeStruct(wi.shape, F32)],
        scratch_shapes=[pltpu.VMEM((SUBLANES, d), F32)] * 3,
        compiler_params=_params(("arbitrary",)),
    )(ux, ux, ug, hs, hs, dy, pv, wa, wi, dep)


def _adamw(w, m, v, parts, dep, name):
    rows, cols = w.shape
    tr = _pick(rows, 256, SUBLANES)
    c1 = 1.0 / (1.0 - ADAM_B1 ** ADAM_STEP)
    c2 = 1.0 / (1.0 - ADAM_B2 ** ADAM_STEP)

    def body(w_ref, m_ref, v_ref, p_ref, dep_ref, g_ref, d_ref, mo_ref, vo_ref):
        g = p_ref[0].astype(F32)
        for q in range(1, N_DEV):
            g = g + p_ref[q].astype(F32)
        mn = ADAM_B1 * m_ref[...] + (1.0 - ADAM_B1) * g
        vn = ADAM_B2 * v_ref[...] + (1.0 - ADAM_B2) * (g * g)
        g_ref[...] = g
        mo_ref[...] = mn
        vo_ref[...] = vn
        d_ref[...] = -ADAM_LR * ((mn * c1) / (jnp.sqrt(vn * c2) + ADAM_EPS) + ADAM_WD * w_ref[...])

    blk = pl.BlockSpec((tr, cols), lambda i: (i, 0))
    return pl.pallas_call(
        body, name=name, grid=(rows // tr,),
        in_specs=[blk, blk, blk, pl.BlockSpec((N_DEV, tr, cols), lambda i: (0, i, 0)),
                  pl.BlockSpec(dep.shape, lambda i: (0, 0))],
        out_specs=[blk] * 4, out_shape=[jax.ShapeDtypeStruct((rows, cols), F32)] * 4,
        compiler_params=_params(("parallel",)),
    )(w, m, v, parts, dep)


WEIGHTS = ("meta_tokens", "norm_mix_g", "w_in", "b_gate", "conv_w", "conv_b", "w_rec_a", "b_rec_a", "w_rec_i",
           "b_rec_i", "lru_lambda", "q_norm_g", "w_uq", "kv_norm_g", "w_ukv", "w_branch", "w_out", "norm_ffn_g",
           "w_ffn_in", "w_ffn_out", "final_norm_g")
SHARDED = {"meta_tokens": True, "w_in": True, "b_gate": True, "conv_w": True, "w_uq": True, "w_ukv": True,
           "w_branch": False, "w_out": False, "w_ffn_in": True, "w_ffn_out": False}


def _as2d(a):
    return a.reshape(-1, a.shape[-1])


def _full_from_gathered(g, by_cols):
    if by_cols:
        return jnp.transpose(g, (1, 0, 2)).reshape(g.shape[1], N_DEV * g.shape[2])
    return g.reshape(N_DEV * g.shape[1], g.shape[2])


def _blocks_from_full(full, by_cols):
    if by_cols:
        r, c = full.shape
        return jnp.transpose(full.reshape(r, N_DEV, c // N_DEV), (1, 0, 2))
    return full.reshape(N_DEV, full.shape[0] // N_DEV, full.shape[1])


def kernel(x, meta_tokens, norm_mix_g, w_in, b_gate, conv_w, conv_b, w_rec_a, b_rec_a, w_rec_i, b_rec_i, lru_lambda, q_norm_g, w_uq, kv_norm_g, w_ukv, w_branch, w_out, norm_ffn_g, w_ffn_in, w_ffn_out, final_norm_g, loss_target, m_meta_tokens, m_norm_mix_g, m_w_in, m_b_gate, m_conv_w, m_conv_b, m_w_rec_a, m_b_rec_a, m_w_rec_i, m_b_rec_i, m_lru_lambda, m_q_norm_g, m_w_uq, m_kv_norm_g, m_w_ukv, m_w_branch, m_w_out, m_norm_ffn_g, m_w_ffn_in, m_w_ffn_out, m_final_norm_g, v_meta_tokens, v_norm_mix_g, v_w_in, v_b_gate, v_conv_w, v_conv_b, v_w_rec_a, v_b_rec_a, v_w_rec_i, v_b_rec_i, v_lru_lambda, v_q_norm_g, v_w_uq, v_kv_norm_g, v_w_ukv, v_w_branch, v_w_out, v_norm_ffn_g, v_w_ffn_in, v_w_ffn_out, v_final_norm_g):
    w = dict(meta_tokens=meta_tokens, norm_mix_g=norm_mix_g, w_in=w_in, b_gate=b_gate, conv_w=conv_w, conv_b=conv_b,
             w_rec_a=w_rec_a, b_rec_a=b_rec_a, w_rec_i=w_rec_i, b_rec_i=b_rec_i, lru_lambda=lru_lambda,
             q_norm_g=q_norm_g, w_uq=w_uq, kv_norm_g=kv_norm_g, w_ukv=w_ukv, w_branch=w_branch, w_out=w_out,
             norm_ffn_g=norm_ffn_g, w_ffn_in=w_ffn_in, w_ffn_out=w_ffn_out, final_norm_g=final_norm_g)
    m = dict(meta_tokens=m_meta_tokens, norm_mix_g=m_norm_mix_g, w_in=m_w_in, b_gate=m_b_gate, conv_w=m_conv_w,
             conv_b=m_conv_b, w_rec_a=m_w_rec_a, b_rec_a=m_b_rec_a, w_rec_i=m_w_rec_i, b_rec_i=m_b_rec_i,
             lru_lambda=m_lru_lambda, q_norm_g=m_q_norm_g, w_uq=m_w_uq, kv_norm_g=m_kv_norm_g, w_ukv=m_w_ukv,
             w_branch=m_w_branch, w_out=m_w_out, norm_ffn_g=m_norm_ffn_g, w_ffn_in=m_w_ffn_in,
             w_ffn_out=m_w_ffn_out, final_norm_g=m_final_norm_g)
    v = dict(meta_tokens=v_meta_tokens, norm_mix_g=v_norm_mix_g, w_in=v_w_in, b_gate=v_b_gate, conv_w=v_conv_w,
             conv_b=v_conv_b, w_rec_a=v_w_rec_a, b_rec_a=v_b_rec_a, w_rec_i=v_w_rec_i, b_rec_i=v_b_rec_i,
             lru_lambda=v_lru_lambda, q_norm_g=v_q_norm_g, w_uq=v_w_uq, kv_norm_g=v_kv_norm_g, w_ukv=v_w_ukv,
             w_branch=v_w_branch, w_out=v_w_out, norm_ffn_g=v_norm_ffn_g, w_ffn_in=v_w_ffn_in,
             w_ffn_out=v_w_ffn_out, final_norm_g=v_final_norm_g)

    seq, d_model = x.shape[1], x.shape[2]
    length = N_META + seq
    lp = -(-length // LANES) * LANES
    t_attn = _pick(lp, 640)
    t_rnn = LANES

    small = ("meta_tokens", "b_gate", "conv_w")
    names = list(SHARDED)
    mid, late = ("w_uq", "w_ukv", "w_branch", "w_out"), ("w_ffn_in", "w_ffn_out")
    payload = lambda n: _as2d(w[n]) if n in small else _as2d(w[n]).astype(BF16)
    got = _exchange([payload(n) for n in small], [True] * len(small), "gather_small")
    w_in_blocks = _gather_two_level(payload("w_in"), "gather_in")
    mid_h = _exchange_start([payload(n) for n in mid], [True] * len(mid), w_in_blocks, "gather_mid_start")
    late_h = _exchange_start([payload(n) for n in late], [True] * len(late), mid_h["token"], "gather_late_start")
    full = {n: _full_from_gathered(g, SHARDED[n]) for n, g in zip(small, got)}
    h0 = jnp.concatenate([full["meta_tokens"], x[0], jnp.zeros((lp - length, d_model), F32)], axis=0)
    z = _rmsnorm_fwd(h0, norm_mix_g, "norm_mix")
    full["w_in"] = _full_from_gathered(w_in_blocks, True)

    splits = (D_RNN, D_RNN, Q_RANK, KV_RANK, QK_ROPE, 2 * d_model)
    offs = [0]
    for s in splits:
        offs.append(offs[-1] + s)
    w_x, w_g, w_q, w_kv, w_kr, w_m = (full["w_in"][:, offs[s]:offs[s + 1]] for s in range(6))
    w_kr = jnp.pad(w_kr, ((0, 0), (0, LANES - QK_ROPE)))
    bg = full["b_gate"].reshape(1, 2 * d_model)
    pv = jnp.concatenate([full["conv_w"], conv_b, b_rec_a, b_rec_i, lru_lambda], axis=0)
    wa_b, wi_b = w_rec_a[0].astype(BF16), w_rec_i[0].astype(BF16)
    g_final = final_norm_g.reshape(1, d_model)

    tgt = jnp.pad(loss_target[0], ((N_META, lp - length), (0, 0)))
    tabs = _rope_tables(lp)

    ux = _mm([(z, w_x)], "nn", "in_x")
    ug = _mm([(z, w_g)], "nn", "in_g")
    uq = _mm([(z, w_q)], "nn", "in_q")
    ukv = _mm([(z, w_kv)], "nn", "in_kv")
    ukr = _mm([(z, w_kr)], "nn", "in_kr")
    um = _mm([(z, w_m)], "nn", "in_m", out_dtype=BF16)
    y_rnn, hs = _rglru_fwd(ux, ug, pv, wa_b, wi_b, t_rnn, "rglru_fwd")
    for n, g in zip(mid, _exchange_wait(mid_h, hs, "gather_mid_wait")):
        full[n] = _full_from_gathered(g, SHARDED[n])
    w_uq3 = full["w_uq"].reshape(Q_RANK, N_HEADS, QK_NOPE + QK_ROPE)
    w_uq_pad = jnp.pad(w_uq3, ((0, 0), (0, 0), (0, HEAD_PAD - QK_NOPE - QK_ROPE))).reshape(Q_RANK, N_HEADS * HEAD_PAD)
    w_q_nope = jnp.pad(w_uq3[:, :, :QK_NOPE], ((0, 0), (0, 0), (0, HEAD_PAD - QK_NOPE))).reshape(
        Q_RANK, N_HEADS * HEAD_PAD)
    w_q_rope = jnp.pad(w_uq3[:, :, QK_NOPE:], ((0, 0), (0, 0), (0, LANES - QK_ROPE))).reshape(Q_RANK, N_HEADS * LANES)
    w_ukv3 = full["w_ukv"].reshape(KV_RANK, N_HEADS, QK_NOPE + V_DIM)
    w_k_pad = jnp.pad(w_ukv3[:, :, :QK_NOPE], ((0, 0), (0, 0), (0, HEAD_PAD - QK_NOPE))).reshape(
        KV_RANK, N_HEADS * HEAD_PAD)
    w_v = w_ukv3[:, :, QK_NOPE:].reshape(KV_RANK, N_HEADS * V_DIM)
    wb_r, wb_a = full["w_branch"][:D_RNN], full["w_branch"][D_RNN:]
    qn = _rmsnorm_fwd(uq, q_norm_g, "norm_q")
    kvn = _rmsnorm_fwd(ukv, kv_norm_g, "norm_kv")
    qh = _up_rope(qn, w_uq_pad, tabs, "up_q")
    kh = _up_rope(kvn, w_k_pad, tabs, "up_k", ukr=ukr)
    vh = _mm([(kvn, w_v)], "nn", "up_v", out_dtype=BF16)
    oh, lse = _attn_fwd(qh, kh, vh, t_attn, "attn_fwd")
    p_rnn = _mm([(y_rnn, wb_r)], "nn", "branch_rnn", out_dtype=BF16)
    p_att = _mm([(oh, wb_a)], "nn", "branch_att", out_dtype=BF16)
    mixed, h1 = _mix_out(um, p_rnn, p_att, bg, full["w_out"], h0, "out_proj")
    for n, g in zip(late, _exchange_wait(late_h, h1, "gather_late_wait")):
        full[n] = _full_from_gathered(g, SHARDED[n])
    zf = _rmsnorm_fwd(h1, norm_ffn_g, "norm_ffn")
    gate, up, act = _ffn_in_swiglu(zf, full["w_ffn_in"], "ffn_in")
    h2 = _mm([(act, full["w_ffn_out"])], "nn", "ffn_out", res=h1)
    dh2, dg_final, _, loss_part = _loss_bwd(h2, tgt, g_final, seq, "loss_bwd")

    d_gate, d_up = _ffn_out_bwd_swiglu(dh2, full["w_ffn_out"], gate, up, "d_gate_up")
    dw_ffn_out = _mm_tn(act, dh2, "dw_ffn_out")
    dw_ffn_in = jnp.concatenate([_mm_tn(zf, d_gate, "dw_ffn_gate"), _mm_tn(zf, d_up, "dw_ffn_up")], axis=1)
    blocks = lambda n, g: _blocks_from_full(g, SHARDED[n]).astype(F32 if n in small else BF16)
    sent = {("w_ffn_in", "w_ffn_out"): _exchange_start(
        [blocks("w_ffn_in", dw_ffn_in), blocks("w_ffn_out", dw_ffn_out)], [False] * 2, dg_final, "scatter_ffn_start")}
    d_zf = _mm([(d_gate, full["w_ffn_in"], D_FF, 0), (d_up, full["w_ffn_in"], D_FF, 1)], "nt", "d_zf")
    dh1, dg_ffn = _rmsnorm_bwd(h1, d_zf, norm_ffn_g, "norm_ffn_bwd", res=dh2,
                               dep=sent[("w_ffn_in", "w_ffn_out")]["token"])
    d_mixed = _mm([(dh1, full["w_out"])], "nt", "d_mixed", out_dtype=BF16)
    dw_out = _mm_tn(mixed, dh1, "dw_out")
    d_prnn, d_patt, d_um, dbg = _mix_bwd(um, p_rnn, p_att, d_mixed, bg, "mix_bwd")
    d_yrnn = _mm([(d_prnn, wb_r)], "nt", "d_yrnn")
    d_oh = _mm([(d_patt, wb_a)], "nt", "d_oh", out_dtype=BF16)
    dwb_r = _mm_tn(y_rnn, d_prnn, "dw_branch_rnn")
    dwb_a = _mm_tn(oh, d_patt, "dw_branch_att")
    sent[("w_out", "w_branch")] = _exchange_start(
        [blocks("w_out", dw_out), blocks("w_branch", jnp.concatenate([dwb_r, dwb_a], axis=0))], [False] * 2,
        dg_ffn, "scatter_mix_start")
    dqh, l2row, dlrow = _attn_bwd_dq(qh, kh, vh, d_oh, oh, lse, sent[("w_out", "w_branch")]["token"], t_attn,
                                     "attn_bwd_dq")
    dkh, dvh = _attn_bwd_dkv(qh, kh, vh, d_oh, l2row, dlrow, t_attn, "attn_bwd_dkv")
    dq_rope, dukr = _rope_bwd(dqh, dkh, tabs, "rope_bwd")
    d_qn = _mm([(dqh, w_q_nope), (dq_rope, w_q_rope)], "nt", "d_qn")
    dw_q_nope = _mm_tn(qn, dqh, "dw_uq_nope")
    dw_q_rope = _mm_tn(qn, dq_rope, "dw_uq_rope")
    d_kvn = _mm([(dkh, w_k_pad), (dvh, w_v)], "nt", "d_kvn")
    dw_k_pad = _mm_tn(kvn, dkh, "dw_uk")
    dw_v = _mm_tn(kvn, dvh, "dw_uv")
    dw_uq = jnp.concatenate([dw_q_nope.reshape(Q_RANK, N_HEADS, HEAD_PAD)[:, :, :QK_NOPE],
                             dw_q_rope.reshape(Q_RANK, N_HEADS, LANES)[:, :, :QK_ROPE]], axis=2).reshape(Q_RANK, -1)
    dw_ukv = jnp.concatenate([dw_k_pad.reshape(KV_RANK, N_HEADS, HEAD_PAD)[:, :, :QK_NOPE],
                              dw_v.reshape(KV_RANK, N_HEADS, V_DIM)], axis=2).reshape(KV_RANK, -1)
    sent[("w_uq", "w_ukv")] = _exchange_start([blocks("w_uq", dw_uq), blocks("w_ukv", dw_ukv)], [False] * 2,
                                              dbg, "scatter_attn_start")
    duq, dg_q = _rmsnorm_bwd(uq, d_qn, q_norm_g, "norm_q_bwd", out_dtype=BF16)
    dukv, dg_kv = _rmsnorm_bwd(ukv, d_kvn, kv_norm_g, "norm_kv_bwd", out_dtype=BF16)
    dux, dug, dpv, dwa, dwi = _rglru_bwd(ux, ug, hs, d_yrnn, pv, wa_b, wi_b, sent[("w_uq", "w_ukv")]["token"],
                                         t_rnn, "rglru_bwd")
    grad_rep = dict(
        conv_b=dpv[4:5], w_rec_a=dwa, b_rec_a=dpv[5:6], w_rec_i=dwi, b_rec_i=dpv[6:7], lru_lambda=dpv[7:8],
        q_norm_g=dg_q, kv_norm_g=dg_kv, norm_ffn_g=dg_ffn, final_norm_g=dg_final)
    rep_now = tuple(grad_rep)
    sent[("b_gate", "conv_w") + rep_now] = _exchange_start(
        [blocks("b_gate", dbg.reshape(2, d_model)), blocks("conv_w", dpv[:CONV_WIDTH])]
        + [_as2d(grad_rep[n]).astype(BF16 if n in ("w_rec_a", "w_rec_i") else F32) for n in rep_now],
        [False] * 2 + [True] * len(rep_now), dg_kv, "scatter_small_start")
    d_z = _mm([(dux, w_x), (dug, w_g), (duq, w_q), (dukv, w_kv), (dukr, w_kr), (d_um, w_m)], "nt", "d_z")
    dw_in = jnp.concatenate([
        _mm_tn(z, dux, "dw_in_x"), _mm_tn(z, dug, "dw_in_g"), _mm_tn(z, duq, "dw_in_q"),
        _mm_tn(z, dukv, "dw_in_kv"), _mm_tn(z, dukr, "dw_in_kr")[:, :QK_ROPE], _mm_tn(z, d_um, "dw_in_m")], axis=1)
    last_h = _exchange_start([blocks("w_in", dw_in)], [False], sent[("b_gate", "conv_w") + rep_now]["token"],
                             "scatter_in_start")
    sent[("w_in",)] = last_h
    dh0, dg_mix = _rmsnorm_bwd(h0, d_z, norm_mix_g, "norm_mix_bwd", res=dh1, dep=last_h["token"])
    sent[("meta_tokens", "norm_mix_g")] = _exchange_start([blocks("meta_tokens", dh0[:N_META]), dg_mix], [False, True],
                                                          dg_mix, "scatter_tail_start")
    grads, deltas, new_m, new_v = {}, {}, {}, {}

    def update(n, parts, dep):
        g2, d2, m2, v2 = _adamw(_as2d(w[n]), _as2d(m[n]), _as2d(v[n]), parts, dep, "adamw_" + n)
        for store, val in ((grads, g2), (deltas, d2), (new_m, m2), (new_v, v2)):
            store[n] = val.reshape(w[n].shape)

    chain = sent[("meta_tokens", "norm_mix_g")]["token"]
    for group, handle in sent.items():
        for n, parts in zip(group, _exchange_wait(handle, chain, "scatter_wait_" + group[0])):
            update(n, parts, chain)
            chain = _as2d(deltas[n])[:SUBLANES, :LANES]

    loss = lax.psum(loss_part[0, 0], MESH_AXES)
    grad_x = dh0[N_META:length][None]
    return (loss, grad_x, *[grads[n] for n in WEIGHTS], *[deltas[n] for n in WEIGHTS],
            *[new_m[n] for n in WEIGHTS], *[new_v[n] for n in WEIGHTS])
```

```python
import math

import jax
import jax.numpy as jnp
from jax import lax
from jax.experimental import pallas as pl
from jax.experimental.pallas import tpu as pltpu

F32 = jnp.float32
BF16 = jnp.bfloat16

N_DEV = 8
MESH_AXES = ("x", "y", "c")
LANES = 128
SUBLANES = 8
VMEM_LIMIT = 56 * 1024 * 1024

N_META = 16
CHUNK_SHIFT = 6
CHUNK_BIAS = 64 - N_META
EPS = 1e-6
D_RNN = 1280
RNN_BLOCKS = 10
CONV_WIDTH = 4
LRU_C = 8.0
N_HEADS = 8
QK_NOPE = 128
QK_ROPE = 64
V_DIM = 128
HEAD_PAD = 256
Q_RANK = 384
KV_RANK = 256
ROPE_THETA = 10000.0
ATTN_SCALE = 1.0 / math.sqrt(QK_NOPE + QK_ROPE)
NEG = -1e30
TINY = 1e-30
LOG2E = 1.0 / math.log(2.0)
SCALE_LOG2E = ATTN_SCALE * LOG2E
ROW_GROUP = 32
SPILL = LANES
D_FF = 2816

ADAM_LR = 0.001
ADAM_B1 = 0.9
ADAM_B2 = 0.999
ADAM_EPS = 1e-08
ADAM_WD = 0.01
ADAM_STEP = 10

NN = (((1,), (0,)), ((), ()))
NT = (((1,), (1,)), ((), ()))
TN = (((0,), (0,)), ((), ()))


def _pick(n, cap, base=LANES):
    best = None
    for t in range(base, min(n, cap) + 1, base):
        if n % t == 0:
            best = t
    return best if best is not None else n


def _params(sem=None):
    return pltpu.CompilerParams(dimension_semantics=sem, vmem_limit_bytes=VMEM_LIMIT)


def _sig(x):
    return 0.5 + 0.5 * jnp.tanh(0.5 * x)


def _exchange(srcs, gather, name):
    n = len(srcs)
    out_shape = [jax.ShapeDtypeStruct((N_DEV,) + (s.shape if g else s.shape[1:]), s.dtype)
                 for s, g in zip(srcs, gather)]

    def body(*refs):
        src, dst = refs[:n], refs[n:2 * n]
        send_sems, recv_sems, local_sems = refs[2 * n:]
        x, y, c = lax.axis_index("x"), lax.axis_index("y"), lax.axis_index("c")
        me = 4 * x + 2 * y + c
        local = []
        for t in range(n):
            cp = pltpu.make_async_copy(src[t] if gather[t] else src[t].at[me], dst[t].at[me], local_sems.at[t])
            cp.start()
            local.append(cp)
        sends, recvs = [], []
        for k in range(1, N_DEV):
            px = 1 - x if k & 4 else x
            py = 1 - y if k & 2 else y
            pc = 1 - c if k & 1 else c
            peer = 4 * px + 2 * py + pc
            for t in range(n):
                cp = pltpu.make_async_remote_copy(
                    src_ref=src[t] if gather[t] else src[t].at[peer], dst_ref=dst[t].at[me],
                    send_sem=send_sems.at[t, k - 1], recv_sem=recv_sems.at[t, k - 1],
                    device_id=(px, py, pc), device_id_type=pl.DeviceIdType.MESH)
                cp.start()
                sends.append(cp)
                recvs.append(pltpu.make_async_remote_copy(
                    src_ref=src[t] if gather[t] else src[t].at[peer], dst_ref=dst[t].at[peer],
                    send_sem=send_sems.at[t, k - 1], recv_sem=recv_sems.at[t, k - 1],
                    device_id=(px, py, pc), device_id_type=pl.DeviceIdType.MESH))
        for cp in recvs:
            cp.wait_recv()
        for cp in sends:
            cp.wait_send()
        for cp in local:
            cp.wait()

    any_spec = pl.BlockSpec(memory_space=pl.ANY)
    return pl.pallas_call(
        body, name=name, out_shape=out_shape,
        in_specs=[any_spec] * n, out_specs=[any_spec] * n,
        scratch_shapes=[pltpu.SemaphoreType.DMA((n, N_DEV - 1)), pltpu.SemaphoreType.DMA((n, N_DEV - 1)),
                        pltpu.SemaphoreType.DMA((n,))],
    )(*srcs)


def _gather_two_level(block, name):
    def body(x_ref, out_ref, send_sems, recv_sems, local_sem):
        x, y, c = lax.axis_index("x"), lax.axis_index("y"), lax.axis_index("c")
        me, sibling = (x, y, c), (x, y, 1 - c)
        chips = [(1 - x, y), (x, 1 - y), (1 - x, 1 - y)]

        def slot(px, py, pc):
            return out_ref.at[4 * px + 2 * py + pc]

        def copy(k, owner, to, src=None):
            return pltpu.make_async_remote_copy(
                src_ref=slot(*owner) if src is None else src, dst_ref=slot(*owner),
                send_sem=send_sems.at[k], recv_sem=recv_sems.at[k], device_id=to,
                device_id_type=pl.DeviceIdType.MESH)

        mine = pltpu.make_async_copy(x_ref, slot(*me), local_sem)
        mine.start()
        first = [copy(0, me, sibling, src=x_ref)] + [copy(1 + j, me, (*chip, c), src=x_ref)
                                                     for j, chip in enumerate(chips)]
        for cp in first:
            cp.start()
        passed = [copy(4 + j, (*chip, c), sibling) for j, chip in enumerate(chips)]
        for j, chip in enumerate(chips):
            copy(1 + j, (*chip, c), me).wait_recv()
            passed[j].start()
        copy(0, sibling, me).wait_recv()
        for j, chip in enumerate(chips):
            copy(4 + j, (*chip, 1 - c), me).wait_recv()
        for cp in first + passed:
            cp.wait_send()
        mine.wait()

    any_spec = pl.BlockSpec(memory_space=pl.ANY)
    return pl.pallas_call(
        body, name=name, out_shape=jax.ShapeDtypeStruct((N_DEV,) + block.shape, block.dtype),
        in_specs=[any_spec], out_specs=any_spec,
        scratch_shapes=[pltpu.SemaphoreType.DMA((N_DEV - 1,)), pltpu.SemaphoreType.DMA((N_DEV - 1,)),
                        pltpu.SemaphoreType.DMA],
    )(block)


HBM_SPEC = pl.BlockSpec(memory_space=pltpu.HBM)
SEM_SPEC = pl.BlockSpec(memory_space=pltpu.SEMAPHORE)
DATAFLOW = pltpu.SideEffectType.DATAFLOW_SIDE_EFFECTING


def _peers(x, y, c):
    out = []
    for k in range(1, N_DEV):
        px = 1 - x if k & 4 else x
        py = 1 - y if k & 2 else y
        pc = 1 - c if k & 1 else c
        out.append((k, (px, py, pc), 4 * px + 2 * py + pc))
    return out


def _split_copies(src, land, gather, send_sems, recv_sems, local_sems):
    x, y, c = lax.axis_index("x"), lax.axis_index("y"), lax.axis_index("c")
    me = 4 * x + 2 * y + c
    n = len(src)
    local = [pltpu.make_async_copy(src[t] if gather[t] else src[t].at[me], land[t].at[me], local_sems.at[t])
             for t in range(n)]
    sends, recvs = [], []
    for k, pos, peer in _peers(x, y, c):
        for t in range(n):
            mine = src[t] if gather[t] else src[t].at[peer]
            slot = t * (N_DEV - 1) + k - 1
            common = dict(send_sem=send_sems.at[slot], recv_sem=recv_sems.at[slot], device_id=pos,
                          device_id_type=pl.DeviceIdType.MESH)
            sends.append(pltpu.make_async_remote_copy(src_ref=mine, dst_ref=land[t].at[me], **common))
            recvs.append(pltpu.make_async_remote_copy(src_ref=mine, dst_ref=land[t].at[peer], **common))
    return local, sends, recvs


def _exchange_start(srcs, gather, after, name):
    n = len(srcs)
    lands = [lax.empty((N_DEV,) + (s.shape if g else s.shape[1:]), s.dtype) for s, g in zip(srcs, gather)]

    def body(*refs):
        src, land = refs[:n], refs[n:2 * n]
        send_sems, recv_sems, local_sems = refs[2 * n + 1:2 * n + 4]
        local, sends, _ = _split_copies(src, land, gather, send_sems, recv_sems, local_sems)
        for cp in local + sends:
            cp.start()
        refs[-1][...] = jnp.zeros_like(refs[-1])

    hbm = lambda a: pltpu.HBM(a.shape, a.dtype)
    outs = pl.pallas_call(
        body, name=name,
        out_shape=(pltpu.SemaphoreType.DMA((n * (N_DEV - 1),)), pltpu.SemaphoreType.DMA((n * (N_DEV - 1),)),
                   pltpu.SemaphoreType.DMA((n,)), *[hbm(s) for s in srcs], *[hbm(a) for a in lands],
                   jax.ShapeDtypeStruct((SUBLANES, LANES), F32)),
        in_specs=[HBM_SPEC] * (2 * n) + [pl.BlockSpec(memory_space=pl.ANY)],
        out_specs=(SEM_SPEC, SEM_SPEC, SEM_SPEC, *[HBM_SPEC] * (2 * n), pl.BlockSpec(memory_space=pltpu.VMEM)),
        input_output_aliases={t: 3 + t for t in range(2 * n)},
        compiler_params=pltpu.CompilerParams(has_side_effects=DATAFLOW),
    )(*[pltpu.with_memory_space_constraint(a, pltpu.HBM) for a in list(srcs) + lands], after)
    return dict(sems=outs[:3], srcs=outs[3:3 + n], lands=outs[3 + n:3 + 2 * n], token=outs[-1], gather=gather)


def _exchange_wait(handle, after, name):
    srcs, lands, gather = handle["srcs"], handle["lands"], handle["gather"]
    n = len(srcs)

    def body(*refs):
        src, land = refs[:n], refs[n:2 * n]
        send_sems, recv_sems, local_sems = refs[2 * n:2 * n + 3]
        local, sends, recvs = _split_copies(src, land, gather, send_sems, recv_sems, local_sems)
        for cp in sends:
            cp.wait_send()
        for cp in recvs:
            cp.wait_recv()
        for cp in local:
            cp.wait()

    hbm = lambda a: pltpu.HBM(a.shape, a.dtype)
    outs = pl.pallas_call(
        body, name=name, out_shape=(*[hbm(s) for s in srcs], *[hbm(a) for a in lands]),
        in_specs=[HBM_SPEC] * (2 * n) + [SEM_SPEC] * 3 + [pl.BlockSpec(memory_space=pl.ANY)],
        out_specs=[HBM_SPEC] * (2 * n), input_output_aliases={t: t for t in range(2 * n)},
        compiler_params=pltpu.CompilerParams(has_side_effects=DATAFLOW),
    )(*srcs, *lands, *handle["sems"], after)
    return outs[n:]


def _mm(pairs, mode, name, res=None, out_dtype=F32):
    pairs = [p if len(p) == 4 else (p[0], p[1], p[0].shape[1], 0) for p in pairs]
    m = pairs[0][0].shape[0]
    n = pairs[0][1].shape[1] if mode == "nn" else pairs[0][1].shape[0]
    tm, tn = _pick(m, 640), _pick(n, 1408)
    np_ = len(pairs)
    dims = NN if mode == "nn" else NT

    def body(*refs):
        acc = None
        for s in range(np_):
            d = lax.dot_general(refs[2 * s][...].astype(BF16), refs[2 * s + 1][...].astype(BF16), dims,
                                preferred_element_type=F32)
            acc = d if acc is None else acc + d
        if res is not None:
            acc = acc + refs[2 * np_][...]
        refs[-1][...] = acc.astype(out_dtype)

    in_specs, args = [], []
    for a, b, kt, kb in pairs:
        in_specs.append(pl.BlockSpec((tm, kt), lambda i, j: (i, 0)))
        if mode == "nn":
            in_specs.append(pl.BlockSpec((kt, tn), lambda i, j, kb=kb: (kb, j)))
        else:
            in_specs.append(pl.BlockSpec((tn, kt), lambda i, j, kb=kb: (j, kb)))
        args += [a, b]
    if res is not None:
        in_specs.append(pl.BlockSpec((tm, tn), lambda i, j: (i, j)))
        args.append(res)
    return pl.pallas_call(
        body, name=name, grid=(m // tm, n // tn), in_specs=in_specs,
        out_specs=pl.BlockSpec((tm, tn), lambda i, j: (i, j)),
        out_shape=jax.ShapeDtypeStruct((m, n), out_dtype),
        compiler_params=_params(("parallel", "parallel")),
    )(*args)


def _mm_tn(a, b, name):
    m, k = a.shape
    n = b.shape[1]
    tm, tk, tn = _pick(m, 1664), _pick(k, 1408), _pick(n, 1408)

    def body(a_ref, b_ref, o_ref):
        @pl.when(pl.program_id(2) == 0)
        def _():
            o_ref[...] = jnp.zeros_like(o_ref)

        o_ref[...] += lax.dot_general(a_ref[...].astype(BF16), b_ref[...].astype(BF16), TN,
                                      preferred_element_type=F32)

    return pl.pallas_call(
        body, name=name, grid=(k // tk, n // tn, m // tm),
        in_specs=[pl.BlockSpec((tm, tk), lambda i, j, r: (r, i)), pl.BlockSpec((tm, tn), lambda i, j, r: (r, j))],
        out_specs=pl.BlockSpec((tk, tn), lambda i, j, r: (i, j)),
        out_shape=jax.ShapeDtypeStruct((k, n), F32),
        compiler_params=_params(("parallel", "parallel", "arbitrary")),
    )(a, b)


ROW_TILE_BYTES = 6 * 1024 * 1024


def _row_tile(rows, row_in, row_out):
    per_row = sum((r[1] * r[0].dtype.itemsize) if isinstance(r, tuple) else (r.shape[1] * r.dtype.itemsize)
                  for r in row_in)
    per_row += sum(w * jnp.dtype(dt).itemsize for w, dt in row_out)
    return _pick(rows, min(640, max(LANES, ROW_TILE_BYTES // per_row)))


def _rowcall(body, name, rows, row_in, full_in, row_out, acc_out=(), tr=None):
    tr = tr or _row_tile(rows, row_in, row_out)
    n_steps = rows // tr
    in_specs, args = [], []
    for r in row_in:
        arr, w, cb = r if isinstance(r, tuple) else (r, r.shape[1], 0)
        in_specs.append(pl.BlockSpec((tr, w), lambda i, cb=cb: (i, cb)))
        args.append(arr)
    for f in full_in:
        in_specs.append(pl.BlockSpec(f.shape, lambda i, nd=f.ndim: (0,) * nd))
        args.append(f)
    out_specs = [pl.BlockSpec((tr, w), lambda i: (i, 0)) for w, _ in row_out]
    out_shape = [jax.ShapeDtypeStruct((rows, w), dt) for w, dt in row_out]
    for shp, dt in acc_out:
        out_specs.append(pl.BlockSpec(shp, lambda i, nd=len(shp): (0,) * nd))
        out_shape.append(jax.ShapeDtypeStruct(shp, dt))

    def wrapped(*refs):
        body(pl.program_id(0), n_steps, *refs)

    return pl.pallas_call(
        wrapped, name=name, grid=(n_steps,), in_specs=in_specs, out_specs=out_specs, out_shape=out_shape,
        compiler_params=_params(("arbitrary",) if acc_out else ("parallel",)),
    )(*args)


def _rmsnorm_fwd(x, g, name):
    rows, w = x.shape

    def body(i, n, x_ref, g_ref, o_ref):
        xv = x_ref[...]
        r = lax.rsqrt(jnp.mean(xv * xv, axis=-1, keepdims=True) + EPS)
        o_ref[...] = (xv * r * g_ref[...]).astype(BF16)

    return _rowcall(body, name, rows,[x], [g], [(w, BF16)])[0]


def _rmsnorm_bwd_math(xv, dy, g):
    w = xv.shape[-1]
    r = lax.rsqrt(jnp.mean(xv * xv, axis=-1, keepdims=True) + EPS)
    t = dy * g
    dx = r * t - xv * (r * r * r * (jnp.sum(t * xv, axis=-1, keepdims=True) / w))
    dg = jnp.sum(dy * xv * r, axis=0, keepdims=True)
    return dx, dg


def _rmsnorm_bwd(x, dy, g, name, res=None, out_dtype=F32, dep=None):
    rows, w = x.shape

    def body(i, n, *refs):
        x_ref, dy_ref = refs[0], refs[1]
        g_ref, dx_ref, dg_ref = refs[-3], refs[-2], refs[-1]
        dx, dg = _rmsnorm_bwd_math(x_ref[...], dy_ref[...], g_ref[...])
        if res is not None:
            dx = dx + refs[2][...]
        dx_ref[...] = dx.astype(out_dtype)

        @pl.when(i == 0)
        def _():
            dg_ref[...] = jnp.zeros_like(dg_ref)

        dg_ref[...] += dg

    row_in = [x, dy] + ([res] if res is not None else [])
    return _rowcall(body, name, rows, row_in, ([dep] if dep is not None else []) + [g], [(w, out_dtype)],
                    [((1, w), F32)])


def _loss_bwd(h2, tgt, g, seq, name):
    rows, w = h2.shape
    tr = _row_tile(rows, [h2, tgt], [(w, F32)])

    def body(i, n, h_ref, t_ref, g_ref, dh_ref, dg_ref, lcol_ref, loss_ref):
        hv, gv = h_ref[...], g_ref[...]
        row = i * tr + lax.broadcasted_iota(jnp.int32, (tr, w), 0)
        valid = jnp.logical_and(row >= N_META, row < N_META + seq)
        r = lax.rsqrt(jnp.mean(hv * hv, axis=-1, keepdims=True) + EPS)
        err = jnp.where(valid, hv * r * gv - t_ref[...], 0.0)
        dx, dg = _rmsnorm_bwd_math(hv, err * (1.0 / w), gv)
        dh_ref[...] = dx

        @pl.when(i == 0)
        def _():
            dg_ref[...] = jnp.zeros_like(dg_ref)
            lcol_ref[...] = jnp.zeros_like(lcol_ref)

        dg_ref[...] += dg
        lcol_ref[...] += jnp.sum(err * err, axis=0, keepdims=True)

        @pl.when(i == n - 1)
        def _():
            total = jnp.sum(lcol_ref[...], axis=1, keepdims=True) * (0.5 / w)
            loss_ref[...] = jnp.broadcast_to(total, loss_ref.shape)

    return _rowcall(body, name, rows, [h2, tgt], [g], [(w, F32)],
                    [((1, w), F32), ((1, w), F32), ((1, LANES), F32)])


def _mix_out(um, p_rnn, p_att, bg, w_out, res, name):
    rows, d = p_rnn.shape
    tr = _pick(rows, 640)

    def body(u0_ref, u1_ref, pr_ref, pa_ref, bg_ref, w_ref, r_ref, mix_ref, o_ref):
        g0 = _sig(u0_ref[...].astype(F32) + bg_ref[:, :d])
        g1 = _sig(u1_ref[...].astype(F32) + bg_ref[:, d:])
        mixed = (g0 * pr_ref[...].astype(F32) + g1 * pa_ref[...].astype(F32)).astype(BF16)
        mix_ref[...] = mixed
        o_ref[...] = r_ref[...] + jnp.dot(mixed, w_ref[...], preferred_element_type=F32)

    row = pl.BlockSpec((tr, d), lambda i: (i, 0))
    whole = lambda a: pl.BlockSpec(a.shape, lambda i: (0, 0))
    return pl.pallas_call(
        body, name=name, grid=(rows // tr,),
        in_specs=[row, pl.BlockSpec((tr, d), lambda i: (i, 1)), row, row, whole(bg), whole(w_out), row],
        out_specs=[row, row],
        out_shape=[jax.ShapeDtypeStruct((rows, d), BF16), jax.ShapeDtypeStruct((rows, d), F32)],
        compiler_params=_params(("parallel",)),
    )(um, um, p_rnn, p_att, bg, w_out, res)


def _mix_bwd(um, p_rnn, p_att, dh, w_out, bg, name):
    rows, d = p_rnn.shape

    def body(i, n, u0_ref, u1_ref, pr_ref, pa_ref, dh_ref, w_ref, bg_ref, dpr_ref, dpa_ref, dum_ref, dbg_ref):
        g0 = _sig(u0_ref[...].astype(F32) + bg_ref[:, :d])
        g1 = _sig(u1_ref[...].astype(F32) + bg_ref[:, d:])
        dm = lax.dot_general(dh_ref[...].astype(BF16), w_ref[...], NT, preferred_element_type=F32)
        dpr_ref[...] = (dm * g0).astype(BF16)
        dpa_ref[...] = (dm * g1).astype(BF16)
        du0 = dm * pr_ref[...].astype(F32) * g0 * (1.0 - g0)
        du1 = dm * pa_ref[...].astype(F32) * g1 * (1.0 - g1)
        dum_ref[:, :d] = du0.astype(BF16)
        dum_ref[:, d:] = du1.astype(BF16)

        @pl.when(i == 0)
        def _():
            dbg_ref[...] = jnp.zeros_like(dbg_ref)

        dbg_ref[:, :d] += jnp.sum(du0, axis=0, keepdims=True)
        dbg_ref[:, d:] += jnp.sum(du1, axis=0, keepdims=True)

    return _rowcall(body, name, rows, [(um, d, 0), (um, d, 1), p_rnn, p_att, dh], [w_out, bg],
                    [(d, BF16), (d, BF16), (2 * d, BF16)], [((1, 2 * d), F32)], tr=_pick(rows, 640))


def _ffn_in_swiglu(zf, w, name):
    m, k = zf.shape
    f = w.shape[1] // 2
    tm, tn = _pick(m, 640), _pick(f, 1408)
    nb = f // tn

    def body(a_ref, bg_ref, bu_ref, g_ref, u_ref, act_ref):
        a = a_ref[...]
        gate = jnp.dot(a, bg_ref[...], preferred_element_type=F32)
        up = jnp.dot(a, bu_ref[...], preferred_element_type=F32)
        g_ref[...] = gate.astype(BF16)
        u_ref[...] = up.astype(BF16)
        act_ref[...] = (gate * _sig(gate) * up).astype(BF16)

    tile = pl.BlockSpec((tm, tn), lambda i, j: (i, j))
    return pl.pallas_call(
        body, name=name, grid=(m // tm, nb),
        in_specs=[pl.BlockSpec((tm, k), lambda i, j: (i, 0)), pl.BlockSpec((k, tn), lambda i, j: (0, j)),
                  pl.BlockSpec((k, tn), lambda i, j: (0, j + nb))],
        out_specs=[tile] * 3, out_shape=[jax.ShapeDtypeStruct((m, f), BF16)] * 3,
        compiler_params=_params(("parallel", "parallel")),
    )(zf, w, w)


def _ffn_out_bwd_swiglu(dh, w_out, gate, up, name):
    m, k = dh.shape
    f = w_out.shape[0]
    tm, tn = _pick(m, 640), _pick(f, 1408)

    def body(a_ref, b_ref, g_ref, u_ref, dg_ref, du_ref):
        da = lax.dot_general(a_ref[...].astype(BF16), b_ref[...], NT, preferred_element_type=F32)
        gv = g_ref[...].astype(F32)
        sg = _sig(gv)
        dg_ref[...] = (da * u_ref[...].astype(F32) * (sg * (1.0 + gv * (1.0 - sg)))).astype(BF16)
        du_ref[...] = (da * gv * sg).astype(BF16)

    tile = pl.BlockSpec((tm, tn), lambda i, j: (i, j))
    return pl.pallas_call(
        body, name=name, grid=(m // tm, f // tn),
        in_specs=[pl.BlockSpec((tm, k), lambda i, j: (i, 0)), pl.BlockSpec((tn, k), lambda i, j: (j, 0)), tile, tile],
        out_specs=[tile] * 2, out_shape=[jax.ShapeDtypeStruct((m, f), BF16)] * 2,
        compiler_params=_params(("parallel", "parallel")),
    )(dh, w_out, gate, up)


def _rope_tables(lp):
    idx = jnp.arange(lp, dtype=jnp.int32).astype(F32)
    inv_freq = ROPE_THETA ** (-jnp.arange(0, QK_ROPE, 2, dtype=F32) / QK_ROPE)
    ang = idx[:, None] * inv_freq[None, :]
    cos, sin = jnp.cos(ang), jnp.sin(ang)
    half = QK_ROPE // 2
    z = lambda wdt: jnp.zeros((lp, wdt), F32)
    tc = jnp.concatenate([cos, cos, z(LANES - 2 * half)], axis=1)
    ts1 = jnp.concatenate([-sin, z(LANES - half)], axis=1)
    ts2 = jnp.concatenate([z(half), sin, z(LANES - 2 * half)], axis=1)
    return tc, ts1, ts2


def _rope(xv, tc, ts1, ts2):
    half = QK_ROPE // 2
    return xv * tc + pltpu.roll(xv, LANES - half, 1) * ts1 + pltpu.roll(xv, half, 1) * ts2


def _rope_t(dv, tc, ts1, ts2):
    half = QK_ROPE // 2
    return dv * tc + pltpu.roll(dv * ts1, half, 1) + pltpu.roll(dv * ts2, LANES - half, 1)


def _up_rope(xn, w_pad, tabs, name, ukr=None):
    rows, k = xn.shape
    n = w_pad.shape[1]
    tr = _pick(rows, 640)

    def body(*refs):
        x_ref, w_ref, c_ref, s1_ref, s2_ref = refs[:5]
        o_ref = refs[-1]
        tc, ts1, ts2 = c_ref[...], s1_ref[...], s2_ref[...]
        y = jnp.dot(x_ref[...], w_ref[...], preferred_element_type=F32)
        kr = None if ukr is None else _rope(refs[5][...], tc, ts1, ts2).astype(BF16)
        for h in range(N_HEADS):
            lo, mid, hi = h * HEAD_PAD, h * HEAD_PAD + QK_NOPE, (h + 1) * HEAD_PAD
            o_ref[:, lo:mid] = y[:, lo:mid].astype(BF16)
            o_ref[:, mid:hi] = _rope(y[:, mid:hi], tc, ts1, ts2).astype(BF16) if ukr is None else kr

    row = lambda wdt: pl.BlockSpec((tr, wdt), lambda i: (i, 0))
    in_specs = [row(k), pl.BlockSpec((k, n), lambda i: (0, 0)), row(LANES), row(LANES), row(LANES)]
    args = [xn, w_pad, *tabs]
    if ukr is not None:
        in_specs.append(row(LANES))
        args.append(ukr)
    return pl.pallas_call(
        body, name=name, grid=(rows // tr,), in_specs=in_specs, out_specs=row(n),
        out_shape=jax.ShapeDtypeStruct((rows, n), BF16), compiler_params=_params(("parallel",)),
    )(*args)


def _rope_bwd(dq, dk, tabs, name):
    rows = dq.shape[0]

    def body(i, n, *refs):
        q_refs, k_refs = refs[:N_HEADS], refs[N_HEADS:2 * N_HEADS]
        c_ref, s1_ref, s2_ref, qo_ref, ro_ref = refs[2 * N_HEADS:]
        tc, ts1, ts2 = c_ref[...], s1_ref[...], s2_ref[...]
        dkr = None
        for h in range(N_HEADS):
            qo_ref[:, h * LANES:(h + 1) * LANES] = _rope_t(q_refs[h][...].astype(F32), tc, ts1, ts2).astype(BF16)
            part = k_refs[h][...].astype(F32)
            dkr = part if dkr is None else dkr + part
        ro_ref[...] = _rope_t(dkr, tc, ts1, ts2).astype(BF16)

    rope_lanes = lambda a: [(a, LANES, 2 * h + 1) for h in range(N_HEADS)]
    return _rowcall(body, name, rows, rope_lanes(dq) + rope_lanes(dk) + list(tabs), [],
                    [(N_HEADS * LANES, BF16), (LANES, BF16)])


def _visible(q0, k0, tq, tk):
    qrow = q0 + lax.broadcasted_iota(jnp.int32, (tq, tk), 0)
    kcol = k0 + lax.broadcasted_iota(jnp.int32, (tq, tk), 1)
    return ((kcol + CHUNK_BIAS) >> CHUNK_SHIFT) <= ((qrow + CHUNK_BIAS) >> CHUNK_SHIFT)


def _visible_t(k0, q0, tk, tq):
    krow = k0 + lax.broadcasted_iota(jnp.int32, (tk, tq), 0)
    qcol = q0 + lax.broadcasted_iota(jnp.int32, (tk, tq), 1)
    return ((krow + CHUNK_BIAS) >> CHUNK_SHIFT) <= ((qcol + CHUNK_BIAS) >> CHUNK_SHIFT)


def _lanes(v, width):
    return jnp.tile(v, (1, width // LANES))


def _pipelined_chunks(n_full, scores, absorb):
    scores(0, 0)

    def pair(jj, carry):
        a = 2 * jj
        scores(a + 1, 1)
        absorb(a, 0, False)
        scores(a + 2, 0)
        absorb(a + 1, 1, False)
        return carry

    lax.fori_loop(0, n_full // 2, pair, 0)

    @pl.when(n_full % 2 == 0)
    def _():
        absorb(n_full, 0, True)

    @pl.when(n_full % 2 == 1)
    def _():
        scores(n_full, 1)
        absorb(n_full - 1, 0, False)
        absorb(n_full, 1, True)


def _attn_fwd(q, k, v, t, name):
    lp = q.shape[0]
    nt = lp // t

    def body(q_ref, k_ref, v_ref, o_ref, lse_ref, m_s, l_s, acc_s, a_s, sa_s, sb_s, p_s):
        i = pl.program_id(1)
        m_s[...] = jnp.full(m_s.shape, NEG, F32)
        l_s[...] = jnp.zeros(l_s.shape, F32)
        acc_s[...] = jnp.zeros(acc_s.shape, F32)

        s_bufs = (sa_s, sb_s)

        def scores(j, slot):
            r0 = pl.multiple_of(j * t, t)
            s_bufs[slot][...] = lax.dot_general(q_ref[...], k_ref[pl.ds(r0, t), :], NT,
                                                preferred_element_type=F32)

        def absorb(j, slot, masked):
            for r in range(0, t, ROW_GROUP):
                rows = slice(r, r + ROW_GROUP)
                s = s_bufs[slot][rows, :]
                if masked:
                    s = jnp.where(_visible(i * t + r, i * t, ROW_GROUP, t), s, NEG)
                m_prev = m_s[rows, :]
                m_new = jnp.maximum(m_prev, jnp.max(s, axis=1, keepdims=True))
                alpha = jnp.exp2((m_prev - m_new) * SCALE_LOG2E)
                p = jnp.exp2((s - _lanes(m_new, t)) * SCALE_LOG2E)
                l_s[rows, :] = alpha * l_s[rows, :] + jnp.sum(p, axis=1, keepdims=True)
                m_s[rows, :] = m_new
                a_s[rows, :] = alpha
                p_s[rows, :] = p.astype(BF16)
            r0 = pl.multiple_of(j * t, t)
            acc_s[...] = a_s[...] * acc_s[...] + jnp.dot(p_s[...], v_ref[pl.ds(r0, t), :],
                                                         preferred_element_type=F32)

        r1 = pl.multiple_of(jnp.minimum(i + 1, nt - 1) * t, t)
        rows = slice(t - SPILL, t)
        s = lax.dot_general(q_ref[rows, :], k_ref[pl.ds(r1, SPILL), :], NT, preferred_element_type=F32)
        seen = jnp.logical_and(_visible(i * t + t - SPILL, (i + 1) * t, SPILL, SPILL), i + 1 < nt)
        s = jnp.where(seen, s, NEG)
        m_new = jnp.max(s, axis=1, keepdims=True)
        p = jnp.exp2((s - m_new) * SCALE_LOG2E)
        l_s[rows, :] = jnp.broadcast_to(jnp.sum(p, axis=1, keepdims=True), (SPILL, LANES))
        acc_s[rows, :] = jnp.dot(p.astype(BF16), v_ref[pl.ds(r1, SPILL), :], preferred_element_type=F32)
        m_s[rows, :] = jnp.broadcast_to(m_new, (SPILL, LANES))

        _pipelined_chunks(i, scores, absorb)
        o_ref[...] = (acc_s[...] / l_s[...]).astype(BF16)
        lse_ref[...] = m_s[...] * ATTN_SCALE + jnp.log(l_s[...])

    return pl.pallas_call(
        body, name=name, grid=(N_HEADS, nt),
        in_specs=[pl.BlockSpec((t, HEAD_PAD), lambda h, i: (i, h)),
                  pl.BlockSpec((lp, HEAD_PAD), lambda h, i: (0, h)),
                  pl.BlockSpec((lp, V_DIM), lambda h, i: (0, h))],
        out_specs=[pl.BlockSpec((t, V_DIM), lambda h, i: (i, h)),
                   pl.BlockSpec((None, t, LANES), lambda h, i: (h, i, 0))],
        out_shape=[jax.ShapeDtypeStruct((lp, N_HEADS * V_DIM), BF16),
                   jax.ShapeDtypeStruct((N_HEADS, lp, LANES), F32)],
        scratch_shapes=[pltpu.VMEM((t, LANES), F32), pltpu.VMEM((t, LANES), F32), pltpu.VMEM((t, V_DIM), F32),
                        pltpu.VMEM((t, LANES), F32), pltpu.VMEM((t, t), F32), pltpu.VMEM((t, t), F32),
                        pltpu.VMEM((t, t), BF16)],
        compiler_params=_params(("parallel", "arbitrary")),
    )(q, k, v)


def _attn_bwd_dq(q, k, v, do, o, lse, dep, t, name):
    lp = q.shape[0]
    nt = lp // t

    def body(q_ref, k_ref, v_ref, do_ref, o_ref, lse_ref, dep_ref, dq_ref, l2row_ref, dlrow_ref,
             acc_s, l2_s, dl_s, sa_s, sb_s, da_s, db_s, ds_s):
        i = pl.program_id(1)
        delta = jnp.sum(do_ref[...].astype(F32) * o_ref[...].astype(F32), axis=1, keepdims=True)
        dl_s[...] = jnp.broadcast_to(delta, dl_s.shape)
        l2_s[...] = lse_ref[...] * LOG2E
        l2row_ref[...] = l2_s[...].T[0:SUBLANES, :]
        dlrow_ref[...] = dl_s[...].T[0:SUBLANES, :]
        acc_s[...] = jnp.zeros(acc_s.shape, F32)
        s_bufs, d_bufs = (sa_s, sb_s), (da_s, db_s)

        def dscores(s, dp, rows, width):
            p = jnp.exp2(s * SCALE_LOG2E - _lanes(l2_s[rows, :], width))
            return (p * (dp - _lanes(dl_s[rows, :], width))).astype(BF16)

        r1 = pl.multiple_of(jnp.minimum(i + 1, nt - 1) * t, t)
        rows = slice(t - SPILL, t)
        ks, vs = k_ref[pl.ds(r1, SPILL), :], v_ref[pl.ds(r1, SPILL), :]
        s = lax.dot_general(q_ref[rows, :], ks, NT, preferred_element_type=F32)
        seen = jnp.logical_and(_visible(i * t + t - SPILL, (i + 1) * t, SPILL, SPILL), i + 1 < nt)
        s = jnp.where(seen, s, NEG)
        dp = lax.dot_general(do_ref[rows, :], vs, NT, preferred_element_type=F32)
        acc_s[rows, :] = jnp.dot(dscores(s, dp, rows, SPILL), ks, preferred_element_type=F32)

        def scores(j, slot):
            r0 = pl.multiple_of(j * t, t)
            s_bufs[slot][...] = lax.dot_general(q_ref[...], k_ref[pl.ds(r0, t), :], NT,
                                                preferred_element_type=F32)
            d_bufs[slot][...] = lax.dot_general(do_ref[...], v_ref[pl.ds(r0, t), :], NT,
                                                preferred_element_type=F32)

        def absorb(j, slot, masked):
            for r in range(0, t, ROW_GROUP):
                rows = slice(r, r + ROW_GROUP)
                s = s_bufs[slot][rows, :]
                if masked:
                    s = jnp.where(_visible(i * t + r, i * t, ROW_GROUP, t), s, NEG)
                ds_s[rows, :] = dscores(s, d_bufs[slot][rows, :], rows, t)
            r0 = pl.multiple_of(j * t, t)
            acc_s[...] += jnp.dot(ds_s[...], k_ref[pl.ds(r0, t), :], preferred_element_type=F32)

        _pipelined_chunks(i, scores, absorb)
        dq_ref[...] = (acc_s[...] * ATTN_SCALE).astype(BF16)

    stat_row = pl.BlockSpec((None, None, SUBLANES, t), lambda h, i: (h, i, 0, 0))
    return pl.pallas_call(
        body, name=name, grid=(N_HEADS, nt),
        in_specs=[pl.BlockSpec((t, HEAD_PAD), lambda h, i: (i, h)),
                  pl.BlockSpec((lp, HEAD_PAD), lambda h, i: (0, h)),
                  pl.BlockSpec((lp, V_DIM), lambda h, i: (0, h)),
                  pl.BlockSpec((t, V_DIM), lambda h, i: (i, h)),
                  pl.BlockSpec((t, V_DIM), lambda h, i: (i, h)),
                  pl.BlockSpec((None, t, LANES), lambda h, i: (h, i, 0)),
                  pl.BlockSpec(dep.shape, lambda h, i: (0, 0))],
        out_specs=[pl.BlockSpec((t, HEAD_PAD), lambda h, i: (i, h)), stat_row, stat_row],
        out_shape=[jax.ShapeDtypeStruct((lp, N_HEADS * HEAD_PAD), BF16),
                   jax.ShapeDtypeStruct((N_HEADS, nt, SUBLANES, t), F32),
                   jax.ShapeDtypeStruct((N_HEADS, nt, SUBLANES, t), F32)],
        scratch_shapes=[pltpu.VMEM((t, HEAD_PAD), F32), pltpu.VMEM((t, LANES), F32), pltpu.VMEM((t, LANES), F32),
                        pltpu.VMEM((t, t), F32), pltpu.VMEM((t, t), F32), pltpu.VMEM((t, t), F32),
                        pltpu.VMEM((t, t), F32), pltpu.VMEM((t, t), BF16)],
        compiler_params=_params(("parallel", "arbitrary")),
    )(q, k, v, do, o, lse, dep)


def _attn_bwd_dkv(q, k, v, do, l2row, dlrow, t, name):
    lp = q.shape[0]
    nt = lp // t

    def body(q_ref, k_ref, v_ref, do_ref, l2_ref, dl_ref, dk_ref, dv_ref,
             dk_s, dv_s, sa_s, sb_s, da_s, db_s, p_s, ds_s):
        j = pl.program_id(1)
        dk_s[...] = jnp.zeros(dk_s.shape, F32)
        dv_s[...] = jnp.zeros(dv_s.shape, F32)
        s_bufs, d_bufs = (sa_s, sb_s), (da_s, db_s)

        def weights(st, dpt, l2r, dlr):
            pt = jnp.exp2(st * SCALE_LOG2E - l2r)
            return pt.astype(BF16), (pt * (dpt - dlr)).astype(BF16)

        prev = jnp.maximum(j - 1, 0)
        q0 = pl.multiple_of(prev * t + t - SPILL, SPILL)
        rows = slice(0, SPILL)
        qs, dos = q_ref[pl.ds(q0, SPILL), :], do_ref[pl.ds(q0, SPILL), :]
        st = lax.dot_general(k_ref[rows, :], qs, NT, preferred_element_type=F32)
        seen = jnp.logical_and(_visible_t(j * t, j * t - SPILL, SPILL, SPILL), j > 0)
        st = jnp.where(seen, st, NEG)
        dpt = lax.dot_general(v_ref[rows, :], dos, NT, preferred_element_type=F32)
        pt, dst = weights(st, dpt, l2_ref[prev, 0:1, t - SPILL:], dl_ref[prev, 0:1, t - SPILL:])
        dv_s[rows, :] = jnp.dot(pt, dos, preferred_element_type=F32)
        dk_s[rows, :] = jnp.dot(dst, qs, preferred_element_type=F32)

        def scores(c, slot):
            r0 = pl.multiple_of((nt - 1 - c) * t, t)
            s_bufs[slot][...] = lax.dot_general(k_ref[...], q_ref[pl.ds(r0, t), :], NT,
                                                preferred_element_type=F32)
            d_bufs[slot][...] = lax.dot_general(v_ref[...], do_ref[pl.ds(r0, t), :], NT,
                                                preferred_element_type=F32)

        def absorb(c, slot, masked):
            i = nt - 1 - c
            l2r, dlr = l2_ref[i, 0:1, :], dl_ref[i, 0:1, :]
            for r in range(0, t, ROW_GROUP):
                rows = slice(r, r + ROW_GROUP)
                st = s_bufs[slot][rows, :]
                if masked:
                    st = jnp.where(_visible_t(j * t + r, j * t, ROW_GROUP, t), st, NEG)
                p_s[rows, :], ds_s[rows, :] = weights(st, d_bufs[slot][rows, :], l2r, dlr)
            r0 = pl.multiple_of(i * t, t)
            dv_s[...] += jnp.dot(p_s[...], do_ref[pl.ds(r0, t), :], preferred_element_type=F32)
            dk_s[...] += jnp.dot(ds_s[...], q_ref[pl.ds(r0, t), :], preferred_element_type=F32)

        _pipelined_chunks(nt - 1 - j, scores, absorb)
        dk_ref[...] = (dk_s[...] * ATTN_SCALE).astype(BF16)
        dv_ref[...] = dv_s[...].astype(BF16)

    stat_rows = pl.BlockSpec((None, nt, SUBLANES, t), lambda h, j: (h, 0, 0, 0))
    return pl.pallas_call(
        body, name=name, grid=(N_HEADS, nt),
        in_specs=[pl.BlockSpec((lp, HEAD_PAD), lambda h, j: (0, h)),
                  pl.BlockSpec((t, HEAD_PAD), lambda h, j: (j, h)),
                  pl.BlockSpec((t, V_DIM), lambda h, j: (j, h)),
                  pl.BlockSpec((lp, V_DIM), lambda h, j: (0, h)),
                  stat_rows, stat_rows],
        out_specs=[pl.BlockSpec((t, HEAD_PAD), lambda h, j: (j, h)),
                   pl.BlockSpec((t, V_DIM), lambda h, j: (j, h))],
        out_shape=[jax.ShapeDtypeStruct((lp, N_HEADS * HEAD_PAD), BF16),
                   jax.ShapeDtypeStruct((lp, N_HEADS * V_DIM), BF16)],
        scratch_shapes=[pltpu.VMEM((t, HEAD_PAD), F32), pltpu.VMEM((t, V_DIM), F32),
                        pltpu.VMEM((t, t), F32), pltpu.VMEM((t, t), F32), pltpu.VMEM((t, t), F32),
                        pltpu.VMEM((t, t), F32), pltpu.VMEM((t, t), BF16), pltpu.VMEM((t, t), BF16)],
        compiler_params=_params(("parallel", "arbitrary")),
    )(q, k, v, do, l2row, dlrow)


def _shift_down(cur, prev8, k):
    r = pltpu.roll(cur, k, 0)
    row8 = lax.broadcasted_iota(jnp.int32, prev8.shape, 0)
    first = jnp.where(row8 < k, pltpu.roll(prev8, k, 0), r[0:SUBLANES])
    return jnp.concatenate([first, r[SUBLANES:]], axis=0)


def _shift_up(cur, next8, k):
    t = cur.shape[0]
    r = pltpu.roll(cur, t - k, 0)
    row8 = lax.broadcasted_iota(jnp.int32, next8.shape, 0)
    last = jnp.where(row8 >= SUBLANES - k, pltpu.roll(next8, SUBLANES - k, 0), r[t - SUBLANES:])
    return jnp.concatenate([r[:t - SUBLANES], last], axis=0)


def _scan_rows(a, b, edge, reverse):
    t, d = a.shape
    groups = t // SUBLANES
    a3, b3 = a.reshape(groups, SUBLANES, d), b.reshape(groups, SUBLANES, d)
    sub = lax.broadcasted_iota(jnp.int32, a3.shape, 1)
    s = 1
    while s < SUBLANES:
        keep = sub < SUBLANES - s if reverse else sub >= s
        shift = SUBLANES - s if reverse else s
        a_sh = jnp.where(keep, pltpu.roll(a3, shift, 1), 1.0)
        b_sh = jnp.where(keep, pltpu.roll(b3, shift, 1), 0.0)
        b3 = a3 * b_sh + b3
        a3 = a3 * a_sh
        s *= 2
    out = [None] * groups
    for k in (range(groups - 1, -1, -1) if reverse else range(groups)):
        out[k] = b3[k] + a3[k] * edge
        edge = out[k][0:1, :] if reverse else out[k][SUBLANES - 1:SUBLANES, :]
    return jnp.concatenate(out, axis=0)


def _sqrt_one_minus_exp2x(x):
    th = jnp.tanh(x)
    m2 = (-2.0 * th) / (1.0 - th)
    return m2 * lax.rsqrt(jnp.maximum(m2, TINY))


def _log_sigmoid(x):
    return jnp.minimum(x, 0.0) - jnp.log(1.0 + jnp.exp(-jnp.abs(x)))


GELU_C = math.sqrt(2.0 / math.pi)
GELU_K = 0.044715


def _gelu(x):
    th = jnp.tanh(GELU_C * (x + GELU_K * x * x * x))
    return 0.5 * x * (1.0 + th), th


def _block_mm(xb, w_ref, dims):
    rb = D_RNN // RNN_BLOCKS
    return jnp.concatenate(
        [lax.dot_general(xb[:, h * rb:(h + 1) * rb], w_ref[h], dims, preferred_element_type=F32)
         for h in range(RNN_BLOCKS)], axis=1)


def _rglru_gates(ux, prev8, pv_ref, wa_ref, wi_ref):
    shifted = [ux] + [_shift_down(ux, prev8, k) for k in range(1, CONV_WIDTH)]
    xc = pv_ref[4:5, :] + pv_ref[3:4, :] * ux
    for k in range(1, CONV_WIDTH):
        xc = xc + pv_ref[3 - k:4 - k, :] * shifted[k]
    xcb = xc.astype(BF16)
    r_g = _sig(_block_mm(xcb, wa_ref, NN) + pv_ref[5:6, :])
    i_g = _sig(_block_mm(xcb, wi_ref, NN) + pv_ref[6:7, :])
    log_a = LRU_C * r_g * _log_sigmoid(pv_ref[7:8, :])
    a = jnp.exp(log_a)
    mm = _sqrt_one_minus_exp2x(log_a)
    return dict(shifted=shifted, xc=xc, xcb=xcb, r=r_g, i=i_g, a=a, mm=mm)


def _rglru_fwd(ux, ug, pv, wa, wi, t, name):
    lp, d = ux.shape

    def body(ux_ref, ug_ref, pv_ref, wa_ref, wi_ref, y_ref, h_ref, tail_s, hc_s):
        @pl.when(pl.program_id(0) == 0)
        def _():
            tail_s[...] = jnp.zeros_like(tail_s)
            hc_s[...] = jnp.zeros_like(hc_s)

        uxv = ux_ref[...]
        gt = _rglru_gates(uxv, tail_s[...], pv_ref, wa_ref, wi_ref)
        tail_s[...] = ux_ref[t - SUBLANES:t, :]
        h_ref[...] = _scan_rows(gt["a"], gt["mm"] * (gt["i"] * gt["xc"]), hc_s[0:1, :], False)
        hc_s[...] = h_ref[t - SUBLANES:t, :]
        hc_s[0:1, :] = h_ref[t - 1:t, :]
        y_ref[...] = (h_ref[...] * _gelu(ug_ref[...])[0]).astype(BF16)

    tile = pl.BlockSpec((t, d), lambda i: (i, 0))
    return pl.pallas_call(
        body, name=name, grid=(lp // t,),
        in_specs=[tile, tile, pl.BlockSpec(pv.shape, lambda i: (0, 0)),
                  pl.BlockSpec(wa.shape, lambda i: (0, 0, 0)), pl.BlockSpec(wi.shape, lambda i: (0, 0, 0))],
        out_specs=[tile, tile],
        out_shape=[jax.ShapeDtypeStruct((lp, d), BF16), jax.ShapeDtypeStruct((lp, d), F32)],
        scratch_shapes=[pltpu.VMEM((SUBLANES, d), F32), pltpu.VMEM((SUBLANES, d), F32)],
        compiler_params=_params(("arbitrary",)),
    )(ux, ug, pv, wa, wi)


def _rglru_bwd(ux, ug, hs, dy, pv, wa, wi, dep, t, name):
    lp, d = ux.shape
    nt = lp // t
    per = t // SUBLANES
    rb = d // RNN_BLOCKS

    def body(ux_ref, uxp_ref, ug_ref, h_ref, hp_ref, dy_ref, pv_ref, wa_ref, wi_ref, dep_ref,
             dux_ref, dug_ref, dpv_ref, dwa_ref, dwi_ref, ca_s, cg_s, cx_s):
        step = pl.program_id(0)
        first_tile = step == nt - 1

        @pl.when(step == 0)
        def _():
            for ref in (ca_s, cg_s, cx_s, dpv_ref, dwa_ref, dwi_ref):
                ref[...] = jnp.zeros_like(ref)

        uxv = ux_ref[...]
        prev8 = jnp.where(first_tile, 0.0, uxp_ref[...])
        hprev8 = jnp.where(first_tile, 0.0, hp_ref[...])
        gt = _rglru_gates(uxv, prev8, pv_ref, wa_ref, wi_ref)
        a, mm, r_g, i_g, xc = gt["a"], gt["mm"], gt["r"], gt["i"], gt["xc"]
        hv = h_ref[...]
        hprev = _shift_down(hv, hprev8, 1)
        ugv, dyv = ug_ref[...], dy_ref[...]
        gel, th = _gelu(ugv)
        dgel = 0.5 * (1.0 + th) + 0.5 * ugv * (1.0 - th * th) * (GELU_C * (1.0 + 3.0 * GELU_K * ugv * ugv))
        dug_ref[...] = (dyv * hv * dgel).astype(BF16)
        a_up = _shift_up(a, ca_s[...], 1)
        gv = _scan_rows(a_up, dyv * gel, cg_s[0:1, :], True)
        ca_s[...] = a[0:SUBLANES]
        cg_s[...] = gv[0:SUBLANES]
        ixc = i_g * xc
        d_ixc = gv * mm
        d_log_a = gv * hprev * a - (gv * ixc) * (a * a) / mm
        logsig = _log_sigmoid(pv_ref[7:8, :])
        d_pre_a = d_log_a * (LRU_C * logsig) * r_g * (1.0 - r_g)
        d_pre_i = d_ixc * xc * i_g * (1.0 - i_g)
        dab, dib = d_pre_a.astype(BF16), d_pre_i.astype(BF16)
        d_xc = d_ixc * i_g + _block_mm(dab, wa_ref, NT) + _block_mm(dib, wi_ref, NT)
        xcb = gt["xcb"]
        for h in range(RNN_BLOCKS):
            cols = slice(h * rb, (h + 1) * rb)
            dwa_ref[h] += lax.dot_general(xcb[:, cols], dab[:, cols], TN, preferred_element_type=F32)
            dwi_ref[h] += lax.dot_general(xcb[:, cols], dib[:, cols], TN, preferred_element_type=F32)
        csum = lambda v: jnp.sum(v, axis=0, keepdims=True)
        for k in range(CONV_WIDTH):
            dpv_ref[3 - k:4 - k, :] += csum(d_xc * gt["shifted"][k])
        dpv_ref[4:5, :] += csum(d_xc)
        dpv_ref[5:6, :] += csum(d_pre_a)
        dpv_ref[6:7, :] += csum(d_pre_i)
        dpv_ref[7:8, :] += csum(d_log_a * (LRU_C * r_g)) * _sig(-pv_ref[7:8, :])
        dux = pv_ref[3:4, :] * d_xc
        for k in range(1, CONV_WIDTH):
            dux = dux + pv_ref[3 - k:4 - k, :] * _shift_up(d_xc, cx_s[...], k)
        cx_s[...] = d_xc[0:SUBLANES]
        dux_ref[...] = dux.astype(BF16)

    rev = lambda i: (nt - 1 - i, 0)
    before = lambda i: (jnp.maximum((nt - 1 - i) * per - 1, 0), 0)
    tile = pl.BlockSpec((t, d), rev)
    tail = pl.BlockSpec((SUBLANES, d), before)
    fixed2 = lambda arr: pl.BlockSpec(arr.shape, lambda i: (0, 0))
    fixed3 = lambda arr: pl.BlockSpec(arr.shape, lambda i: (0, 0, 0))
    return pl.pallas_call(
        body, name=name, grid=(nt,),
        in_specs=[tile, tail, tile, tile, tail, tile, fixed2(pv), fixed3(wa), fixed3(wi), fixed2(dep)],
        out_specs=[tile, tile, fixed2(pv), fixed3(wa), fixed3(wi)],
        out_shape=[jax.ShapeDtypeStruct((lp, d), BF16), jax.ShapeDtypeStruct((lp, d), BF16),
                   jax.ShapeDtypeStruct(pv.shape, F32), jax.ShapeDtypeStruct(wa.shape, F32),
                   jax.ShapeDtypeStruct(wi.shape, F32)],
        scratch_shapes=[pltpu.VMEM((SUBLANES, d), F32)] * 3,
        compiler_params=_params(("arbitrary",)),
    )(ux, ux, ug, hs, hs, dy, pv, wa, wi, dep)


def _adamw(w, m, v, parts, dep, name):
    rows, cols = w.shape
    tr = _pick(rows, 256, SUBLANES)
    c1 = 1.0 / (1.0 - ADAM_B1 ** ADAM_STEP)
    c2 = 1.0 / (1.0 - ADAM_B2 ** ADAM_STEP)

    def body(w_ref, m_ref, v_ref, p_ref, dep_ref, g_ref, d_ref, mo_ref, vo_ref):
        g = p_ref[0].astype(F32)
        for q in range(1, N_DEV):
            g = g + p_ref[q].astype(F32)
        mn = ADAM_B1 * m_ref[...] + (1.0 - ADAM_B1) * g
        vn = ADAM_B2 * v_ref[...] + (1.0 - ADAM_B2) * (g * g)
        g_ref[...] = g
        mo_ref[...] = mn
        vo_ref[...] = vn
        d_ref[...] = -ADAM_LR * ((mn * c1) / (jnp.sqrt(vn * c2) + ADAM_EPS) + ADAM_WD * w_ref[...])

    blk = pl.BlockSpec((tr, cols), lambda i: (i, 0))
    return pl.pallas_call(
        body, name=name, grid=(rows // tr,),
        in_specs=[blk, blk, blk, pl.BlockSpec((N_DEV, tr, cols), lambda i: (0, i, 0)),
                  pl.BlockSpec(dep.shape, lambda i: (0, 0))],
        out_specs=[blk] * 4, out_shape=[jax.ShapeDtypeStruct((rows, cols), F32)] * 4,
        compiler_params=_params(("parallel",)),
    )(w, m, v, parts, dep)


WEIGHTS = ("meta_tokens", "norm_mix_g", "w_in", "b_gate", "conv_w", "conv_b", "w_rec_a", "b_rec_a", "w_rec_i",
           "b_rec_i", "lru_lambda", "q_norm_g", "w_uq", "kv_norm_g", "w_ukv", "w_branch", "w_out", "norm_ffn_g",
           "w_ffn_in", "w_ffn_out", "final_norm_g")
SHARDED = {"meta_tokens": True, "w_in": True, "b_gate": True, "conv_w": True, "w_uq": True, "w_ukv": True,
           "w_branch": False, "w_out": False, "w_ffn_in": True, "w_ffn_out": False}


def _as2d(a):
    return a.reshape(-1, a.shape[-1])


def _full_from_gathered(g, by_cols):
    if by_cols:
        return jnp.transpose(g, (1, 0, 2)).reshape(g.shape[1], N_DEV * g.shape[2])
    return g.reshape(N_DEV * g.shape[1], g.shape[2])


def _blocks_from_full(full, by_cols):
    if by_cols:
        r, c = full.shape
        return jnp.transpose(full.reshape(r, N_DEV, c // N_DEV), (1, 0, 2))
    return full.reshape(N_DEV, full.shape[0] // N_DEV, full.shape[1])


def kernel(x, meta_tokens, norm_mix_g, w_in, b_gate, conv_w, conv_b, w_rec_a, b_rec_a, w_rec_i, b_rec_i, lru_lambda, q_norm_g, w_uq, kv_norm_g, w_ukv, w_branch, w_out, norm_ffn_g, w_ffn_in, w_ffn_out, final_norm_g, loss_target, m_meta_tokens, m_norm_mix_g, m_w_in, m_b_gate, m_conv_w, m_conv_b, m_w_rec_a, m_b_rec_a, m_w_rec_i, m_b_rec_i, m_lru_lambda, m_q_norm_g, m_w_uq, m_kv_norm_g, m_w_ukv, m_w_branch, m_w_out, m_norm_ffn_g, m_w_ffn_in, m_w_ffn_out, m_final_norm_g, v_meta_tokens, v_norm_mix_g, v_w_in, v_b_gate, v_conv_w, v_conv_b, v_w_rec_a, v_b_rec_a, v_w_rec_i, v_b_rec_i, v_lru_lambda, v_q_norm_g, v_w_uq, v_kv_norm_g, v_w_ukv, v_w_branch, v_w_out, v_norm_ffn_g, v_w_ffn_in, v_w_ffn_out, v_final_norm_g):
    w = dict(meta_tokens=meta_tokens, norm_mix_g=norm_mix_g, w_in=w_in, b_gate=b_gate, conv_w=conv_w, conv_b=conv_b,
             w_rec_a=w_rec_a, b_rec_a=b_rec_a, w_rec_i=w_rec_i, b_rec_i=b_rec_i, lru_lambda=lru_lambda,
             q_norm_g=q_norm_g, w_uq=w_uq, kv_norm_g=kv_norm_g, w_ukv=w_ukv, w_branch=w_branch, w_out=w_out,
             norm_ffn_g=norm_ffn_g, w_ffn_in=w_ffn_in, w_ffn_out=w_ffn_out, final_norm_g=final_norm_g)
    m = dict(meta_tokens=m_meta_tokens, norm_mix_g=m_norm_mix_g, w_in=m_w_in, b_gate=m_b_gate, conv_w=m_conv_w,
             conv_b=m_conv_b, w_rec_a=m_w_rec_a, b_rec_a=m_b_rec_a, w_rec_i=m_w_rec_i, b_rec_i=m_b_rec_i,
             lru_lambda=m_lru_lambda, q_norm_g=m_q_norm_g, w_uq=m_w_uq, kv_norm_g=m_kv_norm_g, w_ukv=m_w_ukv,
             w_branch=m_w_branch, w_out=m_w_out, norm_ffn_g=m_norm_ffn_g, w_ffn_in=m_w_ffn_in,
             w_ffn_out=m_w_ffn_out, final_norm_g=m_final_norm_g)
    v = dict(meta_tokens=v_meta_tokens, norm_mix_g=v_norm_mix_g, w_in=v_w_in, b_gate=v_b_gate, conv_w=v_conv_w,
             conv_b=v_conv_b, w_rec_a=v_w_rec_a, b_rec_a=v_b_rec_a, w_rec_i=v_w_rec_i, b_rec_i=v_b_rec_i,
             lru_lambda=v_lru_lambda, q_norm_g=v_q_norm_g, w_uq=v_w_uq, kv_norm_g=v_kv_norm_g, w_ukv=v_w_ukv,
             w_branch=v_w_branch, w_out=v_w_out, norm_ffn_g=v_norm_ffn_g, w_ffn_in=v_w_ffn_in,
             w_ffn_out=v_w_ffn_out, final_norm_g=v_final_norm_g)

    seq, d_model = x.shape[1], x.shape[2]
    length = N_META + seq
    lp = -(-length // LANES) * LANES
    t_attn = _pick(lp, 640)
    t_rnn = LANES

    small = ("meta_tokens", "b_gate", "conv_w")
    names = list(SHARDED)
    mid, late = ("w_uq", "w_ukv", "w_branch", "w_out"), ("w_ffn_in", "w_ffn_out")
    payload = lambda n: _as2d(w[n]) if n in small else _as2d(w[n]).astype(BF16)
    got = _exchange([payload(n) for n in small], [True] * len(small), "gather_small")
    w_in_blocks = _gather_two_level(payload("w_in"), "gather_in")
    mid_h = _exchange_start([payload(n) for n in mid], [True] * len(mid), w_in_blocks, "gather_mid_start")
    late_h = _exchange_start([payload(n) for n in late], [True] * len(late), mid_h["token"], "gather_late_start")
    full = {n: _full_from_gathered(g, SHARDED[n]) for n, g in zip(small, got)}
    h0 = jnp.concatenate([full["meta_tokens"], x[0], jnp.zeros((lp - length, d_model), F32)], axis=0)
    z = _rmsnorm_fwd(h0, norm_mix_g, "norm_mix")
    full["w_in"] = _full_from_gathered(w_in_blocks, True)

    splits = (D_RNN, D_RNN, Q_RANK, KV_RANK, QK_ROPE, 2 * d_model)
    offs = [0]
    for s in splits:
        offs.append(offs[-1] + s)
    w_x, w_g, w_q, w_kv, w_kr, w_m = (full["w_in"][:, offs[s]:offs[s + 1]] for s in range(6))
    w_kr = jnp.pad(w_kr, ((0, 0), (0, LANES - QK_ROPE)))
    bg = full["b_gate"].reshape(1, 2 * d_model)
    pv = jnp.concatenate([full["conv_w"], conv_b, b_rec_a, b_rec_i, lru_lambda], axis=0)
    wa_b, wi_b = w_rec_a[0].astype(BF16), w_rec_i[0].astype(BF16)
    g_final = final_norm_g.reshape(1, d_model)

    tgt = jnp.pad(loss_target[0], ((N_META, lp - length), (0, 0)))
    tabs = _rope_tables(lp)

    ux = _mm([(z, w_x)], "nn", "in_x")
    ug = _mm([(z, w_g)], "nn", "in_g")
    uq = _mm([(z, w_q)], "nn", "in_q")
    ukv = _mm([(z, w_kv)], "nn", "in_kv")
    ukr = _mm([(z, w_kr)], "nn", "in_kr")
    um = _mm([(z, w_m)], "nn", "in_m", out_dtype=BF16)
    y_rnn, hs = _rglru_fwd(ux, ug, pv, wa_b, wi_b, t_rnn, "rglru_fwd")
    for n, g in zip(mid, _exchange_wait(mid_h, hs, "gather_mid_wait")):
        full[n] = _full_from_gathered(g, SHARDED[n])
    w_uq3 = full["w_uq"].reshape(Q_RANK, N_HEADS, QK_NOPE + QK_ROPE)
    w_uq_pad = jnp.pad(w_uq3, ((0, 0), (0, 0), (0, HEAD_PAD - QK_NOPE - QK_ROPE))).reshape(Q_RANK, N_HEADS * HEAD_PAD)
    w_q_nope = jnp.pad(w_uq3[:, :, :QK_NOPE], ((0, 0), (0, 0), (0, HEAD_PAD - QK_NOPE))).reshape(
        Q_RANK, N_HEADS * HEAD_PAD)
    w_q_rope = jnp.pad(w_uq3[:, :, QK_NOPE:], ((0, 0), (0, 0), (0, LANES - QK_ROPE))).reshape(Q_RANK, N_HEADS * LANES)
    w_ukv3 = full["w_ukv"].reshape(KV_RANK, N_HEADS, QK_NOPE + V_DIM)
    w_k_pad = jnp.pad(w_ukv3[:, :, :QK_NOPE], ((0, 0), (0, 0), (0, HEAD_PAD - QK_NOPE))).reshape(
        KV_RANK, N_HEADS * HEAD_PAD)
    w_v = w_ukv3[:, :, QK_NOPE:].reshape(KV_RANK, N_HEADS * V_DIM)
    wb_r, wb_a = full["w_branch"][:D_RNN], full["w_branch"][D_RNN:]
    qn = _rmsnorm_fwd(uq, q_norm_g, "norm_q")
    kvn = _rmsnorm_fwd(ukv, kv_norm_g, "norm_kv")
    qh = _up_rope(qn, w_uq_pad, tabs, "up_q")
    kh = _up_rope(kvn, w_k_pad, tabs, "up_k", ukr=ukr)
    vh = _mm([(kvn, w_v)], "nn", "up_v", out_dtype=BF16)
    oh, lse = _attn_fwd(qh, kh, vh, t_attn, "attn_fwd")
    p_rnn = _mm([(y_rnn, wb_r)], "nn", "branch_rnn", out_dtype=BF16)
    p_att = _mm([(oh, wb_a)], "nn", "branch_att", out_dtype=BF16)
    mixed, h1 = _mix_out(um, p_rnn, p_att, bg, full["w_out"], h0, "out_proj")
    for n, g in zip(late, _exchange_wait(late_h, h1, "gather_late_wait")):
        full[n] = _full_from_gathered(g, SHARDED[n])
    zf = _rmsnorm_fwd(h1, norm_ffn_g, "norm_ffn")
    gate, up, act = _ffn_in_swiglu(zf, full["w_ffn_in"], "ffn_in")
    h2 = _mm([(act, full["w_ffn_out"])], "nn", "ffn_out", res=h1)
    dh2, dg_final, _, loss_part = _loss_bwd(h2, tgt, g_final, seq, "loss_bwd")

    d_gate, d_up = _ffn_out_bwd_swiglu(dh2, full["w_ffn_out"], gate, up, "d_gate_up")
    dw_ffn_out = _mm_tn(act, dh2, "dw_ffn_out")
    dw_ffn_in = jnp.concatenate([_mm_tn(zf, d_gate, "dw_ffn_gate"), _mm_tn(zf, d_up, "dw_ffn_up")], axis=1)
    blocks = lambda n, g: _blocks_from_full(g, SHARDED[n]).astype(F32 if n in small else BF16)
    sent = {("w_ffn_in", "w_ffn_out"): _exchange_start(
        [blocks("w_ffn_in", dw_ffn_in), blocks("w_ffn_out", dw_ffn_out)], [False] * 2, dg_final, "scatter_ffn_start")}
    d_zf = _mm([(d_gate, full["w_ffn_in"], D_FF, 0), (d_up, full["w_ffn_in"], D_FF, 1)], "nt", "d_zf")
    dh1, dg_ffn = _rmsnorm_bwd(h1, d_zf, norm_ffn_g, "norm_ffn_bwd", res=dh2,
                               dep=sent[("w_ffn_in", "w_ffn_out")]["token"])
    dw_out = _mm_tn(mixed, dh1, "dw_out")
    d_prnn, d_patt, d_um, dbg = _mix_bwd(um, p_rnn, p_att, dh1, full["w_out"], bg, "mix_bwd")
    d_yrnn = _mm([(d_prnn, wb_r)], "nt", "d_yrnn")
    d_oh = _mm([(d_patt, wb_a)], "nt", "d_oh", out_dtype=BF16)
    dwb_r = _mm_tn(y_rnn, d_prnn, "dw_branch_rnn")
    dwb_a = _mm_tn(oh, d_patt, "dw_branch_att")
    sent[("w_out", "w_branch")] = _exchange_start(
        [blocks("w_out", dw_out), blocks("w_branch", jnp.concatenate([dwb_r, dwb_a], axis=0))], [False] * 2,
        dg_ffn, "scatter_mix_start")
    dqh, l2row, dlrow = _attn_bwd_dq(qh, kh, vh, d_oh, oh, lse, sent[("w_out", "w_branch")]["token"], t_attn,
                                     "attn_bwd_dq")
    dkh, dvh = _attn_bwd_dkv(qh, kh, vh, d_oh, l2row, dlrow, t_attn, "attn_bwd_dkv")
    dq_rope, dukr = _rope_bwd(dqh, dkh, tabs, "rope_bwd")
    d_qn = _mm([(dqh, w_q_nope), (dq_rope, w_q_rope)], "nt", "d_qn")
    dw_q_nope = _mm_tn(qn, dqh, "dw_uq_nope")
    dw_q_rope = _mm_tn(qn, dq_rope, "dw_uq_rope")
    d_kvn = _mm([(dkh, w_k_pad), (dvh, w_v)], "nt", "d_kvn")
    dw_k_pad = _mm_tn(kvn, dkh, "dw_uk")
    dw_v = _mm_tn(kvn, dvh, "dw_uv")
    dw_uq = jnp.concatenate([dw_q_nope.reshape(Q_RANK, N_HEADS, HEAD_PAD)[:, :, :QK_NOPE],
                             dw_q_rope.reshape(Q_RANK, N_HEADS, LANES)[:, :, :QK_ROPE]], axis=2).reshape(Q_RANK, -1)
    dw_ukv = jnp.concatenate([dw_k_pad.reshape(KV_RANK, N_HEADS, HEAD_PAD)[:, :, :QK_NOPE],
                              dw_v.reshape(KV_RANK, N_HEADS, V_DIM)], axis=2).reshape(KV_RANK, -1)
    sent[("w_uq", "w_ukv")] = _exchange_start([blocks("w_uq", dw_uq), blocks("w_ukv", dw_ukv)], [False] * 2,
                                              dbg, "scatter_attn_start")
    duq, dg_q = _rmsnorm_bwd(uq, d_qn, q_norm_g, "norm_q_bwd", out_dtype=BF16)
    dukv, dg_kv = _rmsnorm_bwd(ukv, d_kvn, kv_norm_g, "norm_kv_bwd", out_dtype=BF16)
    dux, dug, dpv, dwa, dwi = _rglru_bwd(ux, ug, hs, d_yrnn, pv, wa_b, wi_b, sent[("w_uq", "w_ukv")]["token"],
                                         t_rnn, "rglru_bwd")
    grad_rep = dict(
        conv_b=dpv[4:5], w_rec_a=dwa, b_rec_a=dpv[5:6], w_rec_i=dwi, b_rec_i=dpv[6:7], lru_lambda=dpv[7:8],
        q_norm_g=dg_q, kv_norm_g=dg_kv, norm_ffn_g=dg_ffn, final_norm_g=dg_final)
    rep_now = tuple(grad_rep)
    sent[("b_gate", "conv_w") + rep_now] = _exchange_start(
        [blocks("b_gate", dbg.reshape(2, d_model)), blocks("conv_w", dpv[:CONV_WIDTH])]
        + [_as2d(grad_rep[n]).astype(BF16 if n in ("w_rec_a", "w_rec_i") else F32) for n in rep_now],
        [False] * 2 + [True] * len(rep_now), dg_kv, "scatter_small_start")
    d_z = _mm([(dux, w_x), (dug, w_g), (duq, w_q), (dukv, w_kv), (dukr, w_kr), (d_um, w_m)], "nt", "d_z")
    dw_in = jnp.concatenate([
        _mm_tn(z, dux, "dw_in_x"), _mm_tn(z, dug, "dw_in_g"), _mm_tn(z, duq, "dw_in_q"),
        _mm_tn(z, dukv, "dw_in_kv"), _mm_tn(z, dukr, "dw_in_kr")[:, :QK_ROPE], _mm_tn(z, d_um, "dw_in_m")], axis=1)
    last_h = _exchange_start([blocks("w_in", dw_in)], [False], sent[("b_gate", "conv_w") + rep_now]["token"],
                             "scatter_in_start")
    sent[("w_in",)] = last_h
    dh0, dg_mix = _rmsnorm_bwd(h0, d_z, norm_mix_g, "norm_mix_bwd", res=dh1, dep=last_h["token"])
    sent[("meta_tokens", "norm_mix_g")] = _exchange_start([blocks("meta_tokens", dh0[:N_META]), dg_mix], [False, True],
                                                          dg_mix, "scatter_tail_start")
    grads, deltas, new_m, new_v = {}, {}, {}, {}

    def update(n, parts, dep):
        g2, d2, m2, v2 = _adamw(_as2d(w[n]), _as2d(m[n]), _as2d(v[n]), parts, dep, "adamw_" + n)
        for store, val in ((grads, g2), (deltas, d2), (new_m, m2), (new_v, v2)):
            store[n] = val.reshape(w[n].shape)

    chain = sent[("meta_tokens", "norm_mix_g")]["token"]
    for group, handle in sent.items():
        for n, parts in zip(group, _exchange_wait(handle, chain, "scatter_wait_" + group[0])):
            update(n, parts, chain)
            chain = _as2d(deltas[n])[:SUBLANES, :LANES]

    loss = lax.psum(loss_part[0, 0], MESH_AXES)
    grad_x = dh0[N_META:length][None]
    return (loss, grad_x, *[grads[n] for n in WEIGHTS], *[deltas[n] for n in WEIGHTS],
            *[new_m[n] for n in WEIGHTS], *[new_v[n] for n in WEIGHTS])
```
